```python
import jax, jax.numpy as jnp
from jax import lax
import numpy as np

D_MODEL = 1024
BATCH = 2
SEQ = 16384
DEPTH = 2

N_A_LAYERS = max(1, DEPTH // 2)
N_B_LAYERS = DEPTH - N_A_LAYERS

D_RNN = D_MODEL
N_RG_BLOCKS = 8
RG_BW = D_RNN // N_RG_BLOCKS
CONV_WIDTH = 4
RG_C = 8.0

N_HEADS = 8
HEAD_DIM = 128
D_ATTN = N_HEADS * HEAD_DIM
MOBA_BLOCK = 256
MOBA_TOPK = 3
Q_CHUNK = 64
ROPE_THETA = 10000.0
EPS = 1e-6
NEG_INF = -1e30

kernel_name = "hawk_moba_yoco_hybrid"


def rms_norm(x, g):
    xf = x.astype(jnp.float32)
    y = xf * lax.rsqrt(jnp.mean(xf * xf, axis=-1, keepdims=True) + EPS)
    return (y * g.astype(jnp.float32)).astype(x.dtype)


def rope(x, pos):
    half = HEAD_DIM // 2
    inv = ROPE_THETA ** (-jnp.arange(half, dtype=jnp.float32) / half)
    ang = pos.astype(jnp.float32)[:, None] * inv[None, :]
    cos = jnp.cos(ang)[None, :, None, :]
    sin = jnp.sin(ang)[None, :, None, :]
    xf = x.astype(jnp.float32)
    x1, x2 = xf[..., :half], xf[..., half:]
    return jnp.concatenate([x1 * cos - x2 * sin, x2 * cos + x1 * sin], axis=-1).astype(x.dtype)


def causal_depthwise_conv(x, w, b):
    S = x.shape[1]
    xp = jnp.pad(x, ((0, 0), (CONV_WIDTH - 1, 0), (0, 0)))
    y = xp[:, 0:S] * w[0]
    for k in range(1, CONV_WIDTH):
        y = y + xp[:, k:k + S] * w[k]
    return y + b


def block_diag_linear(x, w, b):
    B, S, _ = x.shape
    xb = x.reshape(B, S, N_RG_BLOCKS, RG_BW)
    y = jnp.einsum('bsni,nij->bsnj', xb, w).reshape(B, S, D_RNN)
    return y + b


def rg_lru(x, r_w, r_b, i_w, i_b, lam):
    r = jax.nn.sigmoid(block_diag_linear(x, r_w, r_b).astype(jnp.float32))
    i = jax.nn.sigmoid(block_diag_linear(x, i_w, i_b).astype(jnp.float32))
    log_a = -RG_C * r * jax.nn.softplus(-lam.astype(jnp.float32))
    a = jnp.exp(log_a)
    mult = jnp.sqrt(-jnp.expm1(2.0 * log_a))
    bterm = mult * (i * x.astype(jnp.float32))

    def combine(left, right):
        a_l, b_l = left
        a_r, b_r = right
        return a_l * a_r, a_r * b_l + b_r

    _, h = lax.associative_scan(combine, (a, bterm), axis=1)
    return h.astype(x.dtype)


def hawk_layer(x, norm_g, in_w, conv_w, conv_b, r_w, r_b, i_w, i_b, lam, out_w):
    h = rms_norm(x, norm_g)
    u = h @ in_w
    xb, gate = u[..., :D_RNN], u[..., D_RNN:]
    xb = causal_depthwise_conv(xb, conv_w, conv_b)
    y = rg_lru(xb, r_w, r_b, i_w, i_b, lam) * jax.nn.silu(gate)
    return x + y @ out_w


def shared_kv(x, kv_norm_g, kv_w, k_norm_g, pos):
    B, S, _ = x.shape
    h = rms_norm(x, kv_norm_g)
    kv = (h @ kv_w).reshape(B, S, 2, N_HEADS, HEAD_DIM)
    k = rope(rms_norm(kv[:, :, 0], k_norm_g), pos)
    v = kv[:, :, 1]
    nb = -(-S // MOBA_BLOCK)
    pad = nb * MOBA_BLOCK - S
    k = jnp.pad(k, ((0, 0), (0, pad), (0, 0), (0, 0)))
    v = jnp.pad(v, ((0, 0), (0, pad), (0, 0), (0, 0)))
    kb = k.reshape(B, nb, MOBA_BLOCK, N_HEADS, HEAD_DIM).transpose(0, 3, 1, 2, 4)
    vb = v.reshape(B, nb, MOBA_BLOCK, N_HEADS, HEAD_DIM).transpose(0, 3, 1, 2, 4)
    kmean = jnp.mean(kb.astype(jnp.float32), axis=3)
    return kb, vb, kmean


def moba_attention(q, kb, vb, kmean):
    B, S, H, Dh = q.shape
    nb = kb.shape[2]
    topk = min(MOBA_TOPK, nb)
    qh = jnp.transpose(q, (0, 2, 1, 3))
    scale = HEAD_DIM ** -0.5
    blk_ids = jnp.arange(nb)
    gather_blocks = jax.vmap(jax.vmap(lambda t, idx: t[idx]))

    def chunk(c):
        q0 = c * Q_CHUNK
        qc = lax.dynamic_slice_in_dim(qh, q0, Q_CHUNK, axis=2)
        qpos = q0 + jnp.arange(Q_CHUNK)
        own = q0 // MOBA_BLOCK
        g = jnp.einsum('bhqd,bhnd->bhqn', qc.astype(jnp.float32), kmean)
        g = jnp.where(blk_ids < own, g, -jnp.inf)
        _, sel = lax.top_k(g, topk)
        valid = sel < own
        k_sel = gather_blocks(kb, sel)
        v_sel = gather_blocks(vb, sel)
        s_sel = jnp.einsum('bhqd,bhqkld->bhqkl', qc, k_sel).astype(jnp.float32) * scale
        s_sel = jnp.where(valid[..., None], s_sel, NEG_INF)
        k_own = lax.dynamic_index_in_dim(kb, own, axis=2, keepdims=False)
        v_own = lax.dynamic_index_in_dim(vb, own, axis=2, keepdims=False)
        s_own = jnp.einsum('bhqd,bhld->bhql', qc, k_own).astype(jnp.float32) * scale
        kpos = own * MOBA_BLOCK + jnp.arange(MOBA_BLOCK)
        s_own = jnp.where(kpos[None, :] <= qpos[:, None], s_own, NEG_INF)
        s = jnp.concatenate([s_sel.reshape(B, H, Q_CHUNK, topk * MOBA_BLOCK), s_own], axis=-1)
        p = jax.nn.softmax(s, axis=-1).astype(q.dtype)
        p_sel = p[..., :topk * MOBA_BLOCK].reshape(B, H, Q_CHUNK, topk, MOBA_BLOCK)
        p_own = p[..., topk * MOBA_BLOCK:]
        return (jnp.einsum('bhqkl,bhqkld->bhqd', p_sel, v_sel)
                + jnp.einsum('bhql,bhld->bhqd', p_own, v_own))

    o = lax.map(chunk, jnp.arange(S // Q_CHUNK))
    return jnp.transpose(o, (1, 0, 3, 2, 4)).reshape(B, S, H * Dh)


def moba_layer(x, norm_g, in_w, q_norm_g, out_w, kb, vb, kmean, pos):
    B, S, _ = x.shape
    h = rms_norm(x, norm_g)
    u = h @ in_w
    q = u[..., :D_ATTN].reshape(B, S, N_HEADS, HEAD_DIM)
    gate = u[..., D_ATTN:]
    q = rope(rms_norm(q, q_norm_g), pos)
    o = moba_attention(q, kb, vb, kmean) * jax.nn.silu(gate)
    return x + o @ out_w


def setup_inputs(seed: int = 0) -> dict:
    key = jax.random.key(seed)
    ks = jax.random.split(key, 24)
    f32 = jnp.float32
    nA, nB = N_A_LAYERS, N_B_LAYERS

    def nrm(k, shape, scale):
        return jax.random.normal(k, shape, f32) * scale

    x = jax.random.normal(ks[0], (BATCH, SEQ, D_MODEL), f32)
    a_norm = 1.0 + nrm(ks[1], (nA, D_MODEL), 0.02)
    a_in_w = nrm(ks[2], (nA, D_MODEL, 2 * D_RNN), D_MODEL ** -0.5)
    a_conv_w = nrm(ks[3], (nA, CONV_WIDTH, D_RNN), CONV_WIDTH ** -0.5)
    a_conv_b = nrm(ks[4], (nA, D_RNN), 0.02)
    a_r_w = nrm(ks[5], (nA, N_RG_BLOCKS, RG_BW, RG_BW), RG_BW ** -0.5)
    a_r_b = nrm(ks[6], (nA, D_RNN), 0.02)
    a_i_w = nrm(ks[7], (nA, N_RG_BLOCKS, RG_BW, RG_BW), RG_BW ** -0.5)
    a_i_b = nrm(ks[8], (nA, D_RNN), 0.02)
    a_c = jax.random.uniform(ks[9], (nA, D_RNN), f32, 0.9, 0.999)
    a_base = a_c ** (1.0 / RG_C)
    a_lambda = jnp.log(a_base) - jnp.log1p(-a_base)
    a_out_w = nrm(ks[10], (nA, D_RNN, D_MODEL), D_RNN ** -0.5)
    kv_norm = 1.0 + nrm(ks[11], (D_MODEL,), 0.02)
    kv_w = nrm(ks[12], (D_MODEL, 2 * D_ATTN), D_MODEL ** -0.5)
    k_norm = 1.0 + nrm(ks[13], (HEAD_DIM,), 0.02)
    b_norm = 1.0 + nrm(ks[14], (nB, D_MODEL), 0.02)
    b_in_w = nrm(ks[15], (nB, D_MODEL, 2 * D_ATTN), D_MODEL ** -0.5)
    q_norm = 1.0 + nrm(ks[16], (nB, HEAD_DIM), 0.02)
    b_out_w = nrm(ks[17], (nB, D_ATTN, D_MODEL), D_ATTN ** -0.5)
    return {"x": x, "a_norm": a_norm, "a_in_w": a_in_w, "a_conv_w": a_conv_w, "a_conv_b": a_conv_b,
            "a_r_w": a_r_w, "a_r_b": a_r_b, "a_i_w": a_i_w, "a_i_b": a_i_b, "a_lambda": a_lambda,
            "a_out_w": a_out_w, "kv_norm": kv_norm, "kv_w": kv_w, "k_norm": k_norm,
            "b_norm": b_norm, "b_in_w": b_in_w, "q_norm": q_norm, "b_out_w": b_out_w}


def reference(x, a_norm, a_in_w, a_conv_w, a_conv_b, a_r_w, a_r_b, a_i_w, a_i_b, a_lambda,
              a_out_w, kv_norm, kv_w, k_norm, b_norm, b_in_w, q_norm, b_out_w):
    S = x.shape[1]
    pos = jnp.arange(S, dtype=jnp.int32)
    for l in range(N_A_LAYERS):
        x = hawk_layer(x, a_norm[l], a_in_w[l], a_conv_w[l], a_conv_b[l], a_r_w[l], a_r_b[l],
                       a_i_w[l], a_i_b[l], a_lambda[l], a_out_w[l])
    kb, vb, kmean = shared_kv(x, kv_norm, kv_w, k_norm, pos)
    for j in range(N_B_LAYERS):
        x = moba_layer(x, b_norm[j], b_in_w[j], q_norm[j], b_out_w[j], kb, vb, kmean, pos)
    return x
```

```python
import functools
import math

import jax
import jax.numpy as jnp
from jax import lax
from jax.experimental import pallas as pl
from jax.experimental.pallas import tpu as pltpu

N_HEADS = 8
HEAD_DIM = 128
MOBA_BLOCK = 256
MOBA_TOPK = 3
CONV_WIDTH = 4
N_RG_BLOCKS = 8
RG_BW = 128
RG_C = 8.0
ROPE_THETA = 10000.0
EPS = 1e-6
NEG_INF = -1e30

SUBLANES = 8
ROW_TILE = 512
VMEM_LIMIT = 56 * 1024 * 1024

F32 = jnp.float32
BF16 = jnp.bfloat16


def _cparams(sem):
    return pltpu.CompilerParams(dimension_semantics=sem, vmem_limit_bytes=VMEM_LIMIT)


def _rms_norm(x, g):
    ms = jnp.mean(x * x, axis=-1, keepdims=True)
    return (x * lax.rsqrt(ms + EPS)) * g


def _silu(x):
    return x / (1.0 + jnp.exp(-x))


def _sigmoid(x):
    return 1.0 / (1.0 + jnp.exp(-x))


def _head_norm_rope(z, g, cosf, sinf):
    outs = []
    for h in range(N_HEADS):
        zh = z[:, h * HEAD_DIM:(h + 1) * HEAD_DIM]
        zh = _rms_norm(zh, g)
        outs.append(zh * cosf + pltpu.roll(zh, HEAD_DIM // 2, 1) * sinf)
    return outs


def _hawk_in_kernel(x_ref, g_ref, w_ref, u_ref):
    h = _rms_norm(x_ref[...], g_ref[...]).astype(BF16)
    u_ref[...] = jnp.dot(h, w_ref[...], preferred_element_type=F32)


def _hawk_in(x2d, g, w):
    T, D = x2d.shape
    N = w.shape[1]
    return pl.pallas_call(
        _hawk_in_kernel,
        grid=(T // ROW_TILE,),
        in_specs=[pl.BlockSpec((ROW_TILE, D), lambda i: (i, 0)),
                  pl.BlockSpec((1, D), lambda i: (0, 0)),
                  pl.BlockSpec((D, N), lambda i: (0, 0))],
        out_specs=pl.BlockSpec((ROW_TILE, N), lambda i: (i, 0)),
        out_shape=jax.ShapeDtypeStruct((T, N), F32),
        compiler_params=_cparams(("arbitrary",)),
        name="hawk_in",
    )(x2d, g, w)


def _hawk_lru_kernel(xb_ref, gate_ref, cw_ref, cb_ref, wri_ref, rb_ref, ib_ref, lam_ref,
                     y_ref, xpad_ref, a_ref, b_ref, hc_ref):
    tm = xb_ref.shape[0]
    s = pl.program_id(1)

    @pl.when(s == 0)
    def _():
        xpad_ref[pl.ds(0, SUBLANES), :] = jnp.zeros((SUBLANES, xpad_ref.shape[1]), F32)
        hc_ref[...] = jnp.zeros_like(hc_ref)

    xpad_ref[pl.ds(SUBLANES, tm), :] = xb_ref[...]
    cw = cw_ref[...]
    xc = xpad_ref[pl.ds(SUBLANES - (CONV_WIDTH - 1), tm), :] * cw[0:1, :]
    for k in range(1, CONV_WIDTH):
        xc = xc + xpad_ref[pl.ds(SUBLANES - (CONV_WIDTH - 1) + k, tm), :] * cw[k:k + 1, :]
    xc = xc + cb_ref[...]
    xpad_ref[pl.ds(0, SUBLANES), :] = xpad_ref[pl.ds(tm, SUBLANES), :]

    lam = lam_ref[...]
    sp = jnp.maximum(-lam, 0.0) + jnp.log1p(jnp.exp(-jnp.abs(lam)))
    xc16 = xc.astype(BF16)
    for g in range(N_RG_BLOCKS):
        lo = g * RG_BW
        z = jnp.dot(xc16[:, lo:lo + RG_BW], wri_ref[g], preferred_element_type=F32)
        r = _sigmoid(z[:, :RG_BW] + rb_ref[:, lo:lo + RG_BW])
        i = _sigmoid(z[:, RG_BW:] + ib_ref[:, lo:lo + RG_BW])
        log_a = (-RG_C) * r * sp[:, lo:lo + RG_BW]
        a = jnp.exp(log_a)
        mult = jnp.sqrt(1.0 - a * a)
        a_ref[:, lo:lo + RG_BW] = a
        b_ref[:, lo:lo + RG_BW] = mult * (i * xc[:, lo:lo + RG_BW])

    row = lax.broadcasted_iota(jnp.int32, (SUBLANES, a_ref.shape[1]), 0)

    def scan_body(c, hc):
        r0 = pl.multiple_of(c * SUBLANES, SUBLANES)
        a = a_ref[pl.ds(r0, SUBLANES), :]
        b = b_ref[pl.ds(r0, SUBLANES), :]
        for k in (1, 2, 4):
            keep = row >= k
            b = jnp.where(keep, a * pltpu.roll(b, k, 0) + b, b)
            a = jnp.where(keep, a * pltpu.roll(a, k, 0), a)
        h = a * hc + b
        b_ref[pl.ds(r0, SUBLANES), :] = h
        return h[SUBLANES - 1:SUBLANES, :]

    hc = lax.fori_loop(0, tm // SUBLANES, scan_body, hc_ref[0:1, :])
    hc_ref[0:1, :] = hc

    y_ref[...] = (b_ref[...] * _silu(gate_ref[...])).astype(BF16)


def _hawk_lru(u, B, S, cw, cb, wri, rb, ib, lam):
    T = u.shape[0]
    C = u.shape[1] // 2
    tm = ROW_TILE
    ns = S // tm
    vec = lambda: pl.BlockSpec((1, C), lambda b, s: (0, 0))
    return pl.pallas_call(
        _hawk_lru_kernel,
        grid=(B, ns),
        in_specs=[pl.BlockSpec((tm, C), lambda b, s: (b * ns + s, 0)),
                  pl.BlockSpec((tm, C), lambda b, s: (b * ns + s, 1)),
                  pl.BlockSpec((CONV_WIDTH, C), lambda b, s: (0, 0)),
                  vec(),
                  pl.BlockSpec((N_RG_BLOCKS, RG_BW, 2 * RG_BW), lambda b, s: (0, 0, 0)),
                  vec(), vec(), vec()],
        out_specs=pl.BlockSpec((tm, C), lambda b, s: (b * ns + s, 0)),
        out_shape=jax.ShapeDtypeStruct((T, C), BF16),
        scratch_shapes=[pltpu.VMEM((tm + SUBLANES, C), F32),
                        pltpu.VMEM((tm, C), F32),
                        pltpu.VMEM((tm, C), F32),
                        pltpu.VMEM((SUBLANES, C), F32)],
        compiler_params=_cparams(("arbitrary", "arbitrary")),
        name="hawk_lru",
    )(u, u, cw, cb, wri, rb, ib, lam)


def _proj_residual_kernel(y_ref, w_ref, x_ref, o_ref):
    o_ref[...] = x_ref[...] + jnp.dot(y_ref[...], w_ref[...], preferred_element_type=F32)


def _proj_residual(y, w, x2d, name):
    T, D = x2d.shape
    K = y.shape[1]
    return pl.pallas_call(
        _proj_residual_kernel,
        grid=(T // ROW_TILE,),
        in_specs=[pl.BlockSpec((ROW_TILE, K), lambda i: (i, 0)),
                  pl.BlockSpec((K, D), lambda i: (0, 0)),
                  pl.BlockSpec((ROW_TILE, D), lambda i: (i, 0))],
        out_specs=pl.BlockSpec((ROW_TILE, D), lambda i: (i, 0)),
        out_shape=jax.ShapeDtypeStruct((T, D), F32),
        compiler_params=_cparams(("arbitrary",)),
        name=name,
    )(y, w, x2d)


def _kv_proj_kernel(x_ref, g_ref, w_ref, kg_ref, cos_ref, sin_ref, k_ref, v_ref, km_ref):
    tm = x_ref.shape[0]
    d_attn = N_HEADS * HEAD_DIM
    h = _rms_norm(x_ref[...], g_ref[...]).astype(BF16)
    kv = jnp.dot(h, w_ref[...], preferred_element_type=F32)
    ks = _head_norm_rope(kv[:, :d_attn], kg_ref[...], cos_ref[...], sin_ref[...])
    for hd in range(N_HEADS):
        k_ref[0, hd] = ks[hd].astype(BF16)
        v_ref[0, hd] = kv[:, d_attn + hd * HEAD_DIM:d_attn + (hd + 1) * HEAD_DIM].astype(BF16)
        for r in range(tm // MOBA_BLOCK):
            blk = ks[hd][r * MOBA_BLOCK:(r + 1) * MOBA_BLOCK, :]
            km_ref[0, r, pl.ds(hd, 1), :] = jnp.mean(blk, axis=0, keepdims=True)


def _kv_proj(x2d, B, S, g, w, kg, cosf, sinf):
    T, D = x2d.shape
    tm = ROW_TILE
    ns = S // tm
    nb = S // MOBA_BLOCK
    rpt = tm // MOBA_BLOCK
    kv_shape = jax.ShapeDtypeStruct((B, N_HEADS, S, HEAD_DIM), BF16)
    return pl.pallas_call(
        _kv_proj_kernel,
        grid=(B, ns),
        in_specs=[pl.BlockSpec((tm, D), lambda b, s: (b * ns + s, 0)),
                  pl.BlockSpec((1, D), lambda b, s: (0, 0)),
                  pl.BlockSpec((D, 2 * N_HEADS * HEAD_DIM), lambda b, s: (0, 0)),
                  pl.BlockSpec((1, HEAD_DIM), lambda b, s: (0, 0)),
                  pl.BlockSpec((tm, HEAD_DIM), lambda b, s: (s, 0)),
                  pl.BlockSpec((tm, HEAD_DIM), lambda b, s: (s, 0))],
        out_specs=[pl.BlockSpec((1, N_HEADS, tm, HEAD_DIM), lambda b, s: (b, 0, s, 0)),
                   pl.BlockSpec((1, N_HEADS, tm, HEAD_DIM), lambda b, s: (b, 0, s, 0)),
                   pl.BlockSpec((1, rpt, N_HEADS, HEAD_DIM), lambda b, s: (b, s, 0, 0))],
        out_shape=[kv_shape, kv_shape,
                   jax.ShapeDtypeStruct((B, nb, N_HEADS, HEAD_DIM), F32)],
        compiler_params=_cparams(("arbitrary", "arbitrary")),
        name="kv_proj",
    )(x2d, g, w, kg, cosf, sinf)


def _q_proj_kernel(x_ref, g_ref, w_ref, qg_ref, cos_ref, sin_ref, q_ref, sg_ref):
    d_attn = N_HEADS * HEAD_DIM
    h = _rms_norm(x_ref[...], g_ref[...]).astype(BF16)
    u = jnp.dot(h, w_ref[...], preferred_element_type=F32)
    qs = _head_norm_rope(u[:, :d_attn], qg_ref[...], cos_ref[...], sin_ref[...])
    for hd in range(N_HEADS):
        q_ref[0, hd] = qs[hd].astype(BF16)
    sg_ref[...] = _silu(u[:, d_attn:])


def _q_proj(x2d, B, S, g, w, qg, cosf, sinf):
    T, D = x2d.shape
    tm = ROW_TILE
    ns = S // tm
    d_attn = N_HEADS * HEAD_DIM
    return pl.pallas_call(
        _q_proj_kernel,
        grid=(B, ns),
        in_specs=[pl.BlockSpec((tm, D), lambda b, s: (b * ns + s, 0)),
                  pl.BlockSpec((1, D), lambda b, s: (0, 0)),
                  pl.BlockSpec((D, 2 * d_attn), lambda b, s: (0, 0)),
                  pl.BlockSpec((1, HEAD_DIM), lambda b, s: (0, 0)),
                  pl.BlockSpec((tm, HEAD_DIM), lambda b, s: (s, 0)),
                  pl.BlockSpec((tm, HEAD_DIM), lambda b, s: (s, 0))],
        out_specs=[pl.BlockSpec((1, N_HEADS, tm, HEAD_DIM), lambda b, s: (b, 0, s, 0)),
                   pl.BlockSpec((tm, d_attn), lambda b, s: (b * ns + s, 0))],
        out_shape=[jax.ShapeDtypeStruct((B, N_HEADS, S, HEAD_DIM), BF16),
                   jax.ShapeDtypeStruct((T, d_attn), F32)],
        compiler_params=_cparams(("arbitrary", "arbitrary")),
        name="q_proj",
    )(x2d, g, w, qg, cosf, sinf)


def _nt_dot(a, b):
    return lax.dot_general(a, b, (((1,), (1,)), ((), ())), preferred_element_type=F32)


def _moba_kernel(q_ref, k_ref, v_ref, km_ref, o_ref):
    j = pl.program_id(2)
    nb = km_ref.shape[2]
    scale = HEAD_DIM ** -0.5
    q = q_ref[0, 0]

    g = _nt_dot(q, km_ref[0, 0].astype(BF16))
    blk = lax.broadcasted_iota(jnp.int32, g.shape, 1)
    g = jnp.where(blk < j, g, -jnp.inf)
    sel = jnp.zeros(g.shape, F32)
    for _ in range(MOBA_TOPK):
        mx = jnp.max(g, axis=1, keepdims=True)
        idx = jnp.min(jnp.where(g == mx, blk, nb), axis=1, keepdims=True)
        hit = blk == idx
        sel = jnp.where(hit & (mx > -jnp.inf), 1.0, sel)
        g = jnp.where(hit, -jnp.inf, g)

    j0 = pl.multiple_of(j * MOBA_BLOCK, MOBA_BLOCK)
    s = _nt_dot(q, k_ref[0, 0, pl.ds(j0, MOBA_BLOCK), :]) * scale
    qi = lax.broadcasted_iota(jnp.int32, s.shape, 0)
    ki = lax.broadcasted_iota(jnp.int32, s.shape, 1)
    s = jnp.where(ki <= qi, s, NEG_INF)
    m = jnp.max(s, axis=1, keepdims=True)
    p = jnp.exp(s - m)
    l = jnp.sum(p, axis=1, keepdims=True)
    acc = jnp.dot(p.astype(BF16), v_ref[0, 0, pl.ds(j0, MOBA_BLOCK), :], preferred_element_type=F32)

    def body(n, carry):
        m, l, acc = carry
        picked = jnp.sum(jnp.where(blk == n, sel, 0.0), axis=1, keepdims=True) > 0.0
        n0 = pl.multiple_of(n * MOBA_BLOCK, MOBA_BLOCK)
        s = _nt_dot(q, k_ref[0, 0, pl.ds(n0, MOBA_BLOCK), :]) * scale
        s = jnp.where(picked, s, NEG_INF)
        m_new = jnp.maximum(m, jnp.max(s, axis=1, keepdims=True))
        alpha = jnp.exp(m - m_new)
        p = jnp.exp(s - m_new)
        l = alpha * l + jnp.sum(p, axis=1, keepdims=True)
        acc = alpha * acc + jnp.dot(p.astype(BF16), v_ref[0, 0, pl.ds(n0, MOBA_BLOCK), :],
                                    preferred_element_type=F32)
        return m_new, l, acc

    m, l, acc = lax.fori_loop(0, j, body, (m, l, acc))
    o_ref[...] = acc / l


def _moba_attn(q, k, v, kmean):
    B, H, S, Dh = q.shape
    nb = S // MOBA_BLOCK
    return pl.pallas_call(
        _moba_kernel,
        grid=(B, H, nb),
        in_specs=[pl.BlockSpec((1, 1, MOBA_BLOCK, Dh), lambda b, h, j: (b, h, j, 0)),
                  pl.BlockSpec((1, 1, S, Dh), lambda b, h, j: (b, h, 0, 0)),
                  pl.BlockSpec((1, 1, S, Dh), lambda b, h, j: (b, h, 0, 0)),
                  pl.BlockSpec((1, 1, nb, Dh), lambda b, h, j: (b, h, 0, 0))],
        out_specs=pl.BlockSpec((MOBA_BLOCK, Dh), lambda b, h, j: (b * nb + j, h)),
        out_shape=jax.ShapeDtypeStruct((B * S, H * Dh), F32),
        compiler_params=_cparams(("arbitrary", "arbitrary", "arbitrary")),
        name="moba_attn",
    )(q, k, v, kmean)


def _gated_proj_residual_kernel(o_ref, sg_ref, w_ref, x_ref, out_ref):
    y = (o_ref[...] * sg_ref[...]).astype(BF16)
    out_ref[...] = x_ref[...] + jnp.dot(y, w_ref[...], preferred_element_type=F32)


def _gated_proj_residual(o, sg, w, x2d):
    T, D = x2d.shape
    K = o.shape[1]
    return pl.pallas_call(
        _gated_proj_residual_kernel,
        grid=(T // ROW_TILE,),
        in_specs=[pl.BlockSpec((ROW_TILE, K), lambda i: (i, 0)),
                  pl.BlockSpec((ROW_TILE, K), lambda i: (i, 0)),
                  pl.BlockSpec((K, D), lambda i: (0, 0)),
                  pl.BlockSpec((ROW_TILE, D), lambda i: (i, 0))],
        out_specs=pl.BlockSpec((ROW_TILE, D), lambda i: (i, 0)),
        out_shape=jax.ShapeDtypeStruct((T, D), F32),
        compiler_params=_cparams(("arbitrary",)),
        name="moba_out",
    )(o, sg, w, x2d)


def _rope_tables(S):
    half = HEAD_DIM // 2
    inv = ROPE_THETA ** (-jnp.arange(half, dtype=F32) / half)
    ang = jnp.arange(S, dtype=jnp.int32).astype(F32)[:, None] * inv[None, :]
    cos, sin = jnp.cos(ang), jnp.sin(ang)
    return jnp.concatenate([cos, cos], axis=-1), jnp.concatenate([-sin, sin], axis=-1)


def kernel(x, a_norm, a_in_w, a_conv_w, a_conv_b, a_r_w, a_r_b, a_i_w, a_i_b, a_lambda, a_out_w,
           kv_norm, kv_w, k_norm, b_norm, b_in_w, q_norm, b_out_w):
    B, S, D = x.shape
    assert S % ROW_TILE == 0 and ROW_TILE % MOBA_BLOCK == 0
    assert D == N_RG_BLOCKS * RG_BW == N_HEADS * HEAD_DIM
    xs = x.reshape(B * S, D)
    row = lambda v: v.reshape(1, -1)

    for l in range(a_in_w.shape[0]):
        u = _hawk_in(xs, row(a_norm[l]), a_in_w[l].astype(BF16))
        wri = jnp.concatenate([a_r_w[l], a_i_w[l]], axis=-1).astype(BF16)
        y = _hawk_lru(u, B, S, a_conv_w[l], row(a_conv_b[l]), wri, row(a_r_b[l]), row(a_i_b[l]),
                      row(a_lambda[l]))
        xs = _proj_residual(y, a_out_w[l].astype(BF16), xs, "hawk_out")

    cosf, sinf = _rope_tables(S)
    k, v, kmean = _kv_proj(xs, B, S, row(kv_norm), kv_w.astype(BF16), row(k_norm), cosf, sinf)
    kmean = jnp.transpose(kmean, (0, 2, 1, 3))

    for jl in range(b_in_w.shape[0]):
        q, sg = _q_proj(xs, B, S, row(b_norm[jl]), b_in_w[jl].astype(BF16), row(q_norm[jl]), cosf, sinf)
        o = _moba_attn(q, k, v, kmean)
        xs = _gated_proj_residual(o, sg, b_out_w[jl].astype(BF16), xs)
    return xs.reshape(B, S, D)
```

```python
import functools

import jax
import jax.numpy as jnp
from jax import lax
from jax.experimental import pallas as pl
from jax.experimental.pallas import tpu as pltpu
from jax.experimental.pallas import tpu_sc as plsc

N_HEADS = 8
HEAD_DIM = 128
MOBA_BLOCK = 256
MOBA_TOPK = 3
CONV_WIDTH = 4
N_RG_BLOCKS = 8
RG_BW = 128
RG_C = 8.0
ROPE_THETA = 10000.0
EPS = 1e-6
NEG_INF = -1e30

SUBLANES = 8
LANES = 128
ROW_TILE = 512
VMEM_LIMIT = 56 * 1024 * 1024

GROUP_TILE = MOBA_BLOCK
PART_W = 2 * HEAD_DIM
L_LANE = HEAD_DIM
M_LANE = HEAD_DIM + 1

SC_CORES = 2
SC_SUBCORES = 16
SC_WORKERS = SC_CORES * SC_SUBCORES
IDX_CHUNK = 128

F32 = jnp.float32
BF16 = jnp.bfloat16
I32 = jnp.int32


def _cparams(sem):
    return pltpu.CompilerParams(dimension_semantics=sem, vmem_limit_bytes=VMEM_LIMIT)


def _rms_norm(x, g):
    ms = jnp.mean(x * x, axis=-1, keepdims=True)
    return (x * lax.rsqrt(ms + EPS)) * g


def _silu(x):
    return x / (1.0 + jnp.exp(-x))


def _sigmoid(x):
    return 1.0 / (1.0 + jnp.exp(-x))


def _nt_dot(a, b):
    return lax.dot_general(a, b, (((1,), (1,)), ((), ())), preferred_element_type=F32)


def _head_norm_rope(z, g, cosf, sinf):
    outs = []
    for h in range(N_HEADS):
        zh = z[:, h * HEAD_DIM:(h + 1) * HEAD_DIM]
        zh = _rms_norm(zh, g)
        outs.append(zh * cosf + pltpu.roll(zh, HEAD_DIM // 2, 1) * sinf)
    return outs


def _hawk_in_kernel(x_ref, g_ref, w_ref, u_ref):
    h = _rms_norm(x_ref[...], g_ref[...]).astype(BF16)
    u_ref[...] = jnp.dot(h, w_ref[...], preferred_element_type=F32)


def _hawk_in(x2d, g, w):
    T, D = x2d.shape
    N = w.shape[1]
    return pl.pallas_call(
        _hawk_in_kernel,
        grid=(T // ROW_TILE,),
        in_specs=[pl.BlockSpec((ROW_TILE, D), lambda i: (i, 0)),
                  pl.BlockSpec((1, D), lambda i: (0, 0)),
                  pl.BlockSpec((D, N), lambda i: (0, 0))],
        out_specs=pl.BlockSpec((ROW_TILE, N), lambda i: (i, 0)),
        out_shape=jax.ShapeDtypeStruct((T, N), F32),
        compiler_params=_cparams(("arbitrary",)),
        name="hawk_in",
    )(x2d, g, w)


def _hawk_lru_kernel(xb_ref, gate_ref, cw_ref, cb_ref, wri_ref, rb_ref, ib_ref, lam_ref,
                     y_ref, xpad_ref, a_ref, b_ref, hc_ref):
    tm = xb_ref.shape[0]
    s = pl.program_id(1)

    @pl.when(s == 0)
    def _():
        xpad_ref[pl.ds(0, SUBLANES), :] = jnp.zeros((SUBLANES, xpad_ref.shape[1]), F32)
        hc_ref[...] = jnp.zeros_like(hc_ref)

    xpad_ref[pl.ds(SUBLANES, tm), :] = xb_ref[...]
    cw = cw_ref[...]
    xc = xpad_ref[pl.ds(SUBLANES - (CONV_WIDTH - 1), tm), :] * cw[0:1, :]
    for k in range(1, CONV_WIDTH):
        xc = xc + xpad_ref[pl.ds(SUBLANES - (CONV_WIDTH - 1) + k, tm), :] * cw[k:k + 1, :]
    xc = xc + cb_ref[...]
    xpad_ref[pl.ds(0, SUBLANES), :] = xpad_ref[pl.ds(tm, SUBLANES), :]

    lam = lam_ref[...]
    sp = jnp.maximum(-lam, 0.0) + jnp.log1p(jnp.exp(-jnp.abs(lam)))
    xc16 = xc.astype(BF16)
    for g in range(N_RG_BLOCKS):
        lo = g * RG_BW
        z = jnp.dot(xc16[:, lo:lo + RG_BW], wri_ref[g], preferred_element_type=F32)
        r = _sigmoid(z[:, :RG_BW] + rb_ref[:, lo:lo + RG_BW])
        i = _sigmoid(z[:, RG_BW:] + ib_ref[:, lo:lo + RG_BW])
        log_a = (-RG_C) * r * sp[:, lo:lo + RG_BW]
        a = jnp.exp(log_a)
        mult = jnp.sqrt(1.0 - a * a)
        a_ref[:, lo:lo + RG_BW] = a
        b_ref[:, lo:lo + RG_BW] = mult * (i * xc[:, lo:lo + RG_BW])

    row = lax.broadcasted_iota(I32, (SUBLANES, a_ref.shape[1]), 0)

    def scan_body(c, hc):
        r0 = pl.multiple_of(c * SUBLANES, SUBLANES)
        a = a_ref[pl.ds(r0, SUBLANES), :]
        b = b_ref[pl.ds(r0, SUBLANES), :]
        for k in (1, 2, 4):
            keep = row >= k
            b = jnp.where(keep, a * pltpu.roll(b, k, 0) + b, b)
            a = jnp.where(keep, a * pltpu.roll(a, k, 0), a)
        h = a * hc + b
        b_ref[pl.ds(r0, SUBLANES), :] = h
        return h[SUBLANES - 1:SUBLANES, :]

    hc = lax.fori_loop(0, tm // SUBLANES, scan_body, hc_ref[0:1, :])
    hc_ref[0:1, :] = hc

    y_ref[...] = (b_ref[...] * _silu(gate_ref[...])).astype(BF16)


def _hawk_lru(u, B, S, cw, cb, wri, rb, ib, lam):
    T = u.shape[0]
    C = u.shape[1] // 2
    tm = ROW_TILE
    ns = S // tm
    vec = lambda: pl.BlockSpec((1, C), lambda b, s: (0, 0))
    return pl.pallas_call(
        _hawk_lru_kernel,
        grid=(B, ns),
        in_specs=[pl.BlockSpec((tm, C), lambda b, s: (b * ns + s, 0)),
                  pl.BlockSpec((tm, C), lambda b, s: (b * ns + s, 1)),
                  pl.BlockSpec((CONV_WIDTH, C), lambda b, s: (0, 0)),
                  vec(),
                  pl.BlockSpec((N_RG_BLOCKS, RG_BW, 2 * RG_BW), lambda b, s: (0, 0, 0)),
                  vec(), vec(), vec()],
        out_specs=pl.BlockSpec((tm, C), lambda b, s: (b * ns + s, 0)),
        out_shape=jax.ShapeDtypeStruct((T, C), BF16),
        scratch_shapes=[pltpu.VMEM((tm + SUBLANES, C), F32),
                        pltpu.VMEM((tm, C), F32),
                        pltpu.VMEM((tm, C), F32),
                        pltpu.VMEM((SUBLANES, C), F32)],
        compiler_params=_cparams(("arbitrary", "arbitrary")),
        name="hawk_lru",
    )(u, u, cw, cb, wri, rb, ib, lam)


def _proj_residual_kernel(y_ref, w_ref, x_ref, o_ref):
    o_ref[...] = x_ref[...] + jnp.dot(y_ref[...], w_ref[...], preferred_element_type=F32)


def _proj_residual(y, w, x2d, name):
    T, D = x2d.shape
    K = y.shape[1]
    return pl.pallas_call(
        _proj_residual_kernel,
        grid=(T // ROW_TILE,),
        in_specs=[pl.BlockSpec((ROW_TILE, K), lambda i: (i, 0)),
                  pl.BlockSpec((K, D), lambda i: (0, 0)),
                  pl.BlockSpec((ROW_TILE, D), lambda i: (i, 0))],
        out_specs=pl.BlockSpec((ROW_TILE, D), lambda i: (i, 0)),
        out_shape=jax.ShapeDtypeStruct((T, D), F32),
        compiler_params=_cparams(("arbitrary",)),
        name=name,
    )(y, w, x2d)


def _kv_proj_kernel(x_ref, g_ref, w_ref, kg_ref, cos_ref, sin_ref, k_ref, v_ref, km_ref):
    tm = x_ref.shape[0]
    d_attn = N_HEADS * HEAD_DIM
    h = _rms_norm(x_ref[...], g_ref[...]).astype(BF16)
    kv = jnp.dot(h, w_ref[...], preferred_element_type=F32)
    ks = _head_norm_rope(kv[:, :d_attn], kg_ref[...], cos_ref[...], sin_ref[...])
    lane = lax.broadcasted_iota(I32, (tm, HEAD_DIM), 1)
    ones_col = jnp.where(lane == 0, 1.0, 0.0).astype(BF16)
    for hd in range(N_HEADS):
        k_ref[0, hd] = ks[hd].astype(BF16)
        vh = kv[:, d_attn + hd * HEAD_DIM:d_attn + (hd + 1) * HEAD_DIM].astype(BF16)
        v_ref[0, hd] = jnp.concatenate([vh, ones_col], axis=1)
        for r in range(tm // MOBA_BLOCK):
            blk = ks[hd][r * MOBA_BLOCK:(r + 1) * MOBA_BLOCK, :]
            km_ref[0, r, pl.ds(hd, 1), :] = jnp.mean(blk, axis=0, keepdims=True)


def _kv_proj(x2d, B, S, g, w, kg, cosf, sinf):
    T, D = x2d.shape
    tm = ROW_TILE
    ns = S // tm
    nb = S // MOBA_BLOCK
    rpt = tm // MOBA_BLOCK
    return pl.pallas_call(
        _kv_proj_kernel,
        grid=(B, ns),
        in_specs=[pl.BlockSpec((tm, D), lambda b, s: (b * ns + s, 0)),
                  pl.BlockSpec((1, D), lambda b, s: (0, 0)),
                  pl.BlockSpec((D, 2 * N_HEADS * HEAD_DIM), lambda b, s: (0, 0)),
                  pl.BlockSpec((1, HEAD_DIM), lambda b, s: (0, 0)),
                  pl.BlockSpec((tm, HEAD_DIM), lambda b, s: (s, 0)),
                  pl.BlockSpec((tm, HEAD_DIM), lambda b, s: (s, 0))],
        out_specs=[pl.BlockSpec((1, N_HEADS, tm, HEAD_DIM), lambda b, s: (b, 0, s, 0)),
                   pl.BlockSpec((1, N_HEADS, tm, PART_W), lambda b, s: (b, 0, s, 0)),
                   pl.BlockSpec((1, rpt, N_HEADS, HEAD_DIM), lambda b, s: (b, s, 0, 0))],
        out_shape=[jax.ShapeDtypeStruct((B, N_HEADS, S, HEAD_DIM), BF16),
                   jax.ShapeDtypeStruct((B, N_HEADS, S, PART_W), BF16),
                   jax.ShapeDtypeStruct((B, nb, N_HEADS, HEAD_DIM), F32)],
        compiler_params=_cparams(("arbitrary", "arbitrary")),
        name="kv_proj",
    )(x2d, g, w, kg, cosf, sinf)


def _q_proj_kernel(x_ref, g_ref, w_ref, qg_ref, cos_ref, sin_ref, q_ref, sg_ref):
    d_attn = N_HEADS * HEAD_DIM
    h = _rms_norm(x_ref[...], g_ref[...]).astype(BF16)
    u = jnp.dot(h, w_ref[...], preferred_element_type=F32)
    qs = _head_norm_rope(u[:, :d_attn], qg_ref[...], cos_ref[...], sin_ref[...])
    for hd in range(N_HEADS):
        q_ref[0, hd] = qs[hd]
    sg_ref[...] = _silu(u[:, d_attn:])


def _q_proj(x2d, B, S, g, w, qg, cosf, sinf):
    T, D = x2d.shape
    tm = ROW_TILE
    ns = S // tm
    d_attn = N_HEADS * HEAD_DIM
    return pl.pallas_call(
        _q_proj_kernel,
        grid=(B, ns),
        in_specs=[pl.BlockSpec((tm, D), lambda b, s: (b * ns + s, 0)),
                  pl.BlockSpec((1, D), lambda b, s: (0, 0)),
                  pl.BlockSpec((D, 2 * d_attn), lambda b, s: (0, 0)),
                  pl.BlockSpec((1, HEAD_DIM), lambda b, s: (0, 0)),
                  pl.BlockSpec((tm, HEAD_DIM), lambda b, s: (s, 0)),
                  pl.BlockSpec((tm, HEAD_DIM), lambda b, s: (s, 0))],
        out_specs=[pl.BlockSpec((1, N_HEADS, tm, HEAD_DIM), lambda b, s: (b, 0, s, 0)),
                   pl.BlockSpec((tm, d_attn), lambda b, s: (b * ns + s, 0))],
        out_shape=[jax.ShapeDtypeStruct((B, N_HEADS, S, HEAD_DIM), F32),
                   jax.ShapeDtypeStruct((T, d_attn), F32)],
        compiler_params=_cparams(("arbitrary", "arbitrary")),
        name="q_proj",
    )(x2d, g, w, qg, cosf, sinf)


def _pick_blocks(q16, km, j):
    nb = km.shape[0]
    g = _nt_dot(q16, km.astype(BF16))
    blk = lax.broadcasted_iota(I32, g.shape, 1)
    g = jnp.where(blk < j, g, -jnp.inf)
    hits = []
    for _ in range(MOBA_TOPK):
        mx = jnp.max(g, axis=1, keepdims=True)
        idx = jnp.min(jnp.where(g == mx, blk, nb), axis=1, keepdims=True)
        hit = blk == idx
        hits.append(jnp.where(hit & (mx > -jnp.inf), 1.0, 0.0))
        g = jnp.where(hit, -jnp.inf, g)
    return hits


def _moba_count_kernel(q_ref, km_ref, cnt_ref):
    j = pl.program_id(2)

    @pl.when(j == 0)
    def _():
        cnt_ref[...] = jnp.zeros_like(cnt_ref)

    hits = _pick_blocks(q_ref[0, 0].astype(BF16), km_ref[0, 0], j)
    sel = hits[0] + hits[1] + hits[2]
    cnt_ref[0, 0] += jnp.sum(sel, axis=0, keepdims=True).astype(I32)


def _moba_count(q, kmean):
    B, H, S, Dh = q.shape
    nb = S // MOBA_BLOCK
    return pl.pallas_call(
        _moba_count_kernel,
        grid=(B, H, nb),
        in_specs=[pl.BlockSpec((1, 1, MOBA_BLOCK, Dh), lambda b, h, j: (b, h, j, 0)),
                  pl.BlockSpec((1, 1, nb, Dh), lambda b, h, j: (b, h, 0, 0))],
        out_specs=pl.BlockSpec((1, 1, 1, nb), lambda b, h, j: (b, h, 0, 0)),
        out_shape=jax.ShapeDtypeStruct((B, H, 1, nb), I32),
        compiler_params=_cparams(("arbitrary", "arbitrary", "arbitrary")),
        name="moba_count",
    )(q, kmean)


def _moba_route_kernel(q_ref, km_ref, base_ref, pos_ref, run_ref, *, slab_rows, null_row0):
    b, h, j = pl.program_id(0), pl.program_id(1), pl.program_id(2)
    nb = km_ref.shape[2]

    @pl.when(j == 0)
    def _():
        run_ref[...] = jnp.zeros_like(run_ref)

    hits = _pick_blocks(q_ref[0, 0].astype(BF16), km_ref[0, 0], j)
    sel = hits[0] + hits[1] + hits[2]
    qi = lax.broadcasted_iota(I32, (MOBA_BLOCK, MOBA_BLOCK), 0)
    qj = lax.broadcasted_iota(I32, (MOBA_BLOCK, MOBA_BLOCK), 1)
    before = jnp.where(qj < qi, 1.0, 0.0).astype(BF16)
    rank = jnp.dot(before, sel.astype(BF16), preferred_element_type=F32)
    dest = base_ref[0, 0].astype(F32) + run_ref[...] + rank
    run_ref[...] += jnp.sum(sel, axis=0, keepdims=True)

    hi = jnp.floor(dest * (1.0 / 256.0))
    lo = dest - 256.0 * hi
    ones = jnp.ones((SUBLANES, nb), BF16)
    slab0 = (b * pl.num_programs(1) + h) * slab_rows
    lane = lax.broadcasted_iota(I32, (1, MOBA_BLOCK), 1)
    for r in range(MOBA_TOPK):
        hit = hits[r]
        d_hi = _nt_dot(ones, (hit * hi).astype(BF16))[0:1, :]
        d_lo = _nt_dot(ones, (hit * lo).astype(BF16))[0:1, :]
        ok = _nt_dot(ones, hit.astype(BF16))[0:1, :]
        row = (256.0 * d_hi + d_lo).astype(I32)
        row = jnp.where(ok > 0.5, row, null_row0 + lane)
        row = row + slab0
        for c in range(MOBA_BLOCK // IDX_CHUNK):
            pos_ref[c, pl.ds(r, 1), :] = row[:, c * IDX_CHUNK:(c + 1) * IDX_CHUNK]


def _moba_route(q, kmean, base, slab_rows, null_row0):
    B, H, S, Dh = q.shape
    nb = S // MOBA_BLOCK
    cpb = MOBA_BLOCK // IDX_CHUNK
    return pl.pallas_call(
        functools.partial(_moba_route_kernel, slab_rows=slab_rows, null_row0=null_row0),
        grid=(B, H, nb),
        in_specs=[pl.BlockSpec((1, 1, MOBA_BLOCK, Dh), lambda b, h, j: (b, h, j, 0)),
                  pl.BlockSpec((1, 1, nb, Dh), lambda b, h, j: (b, h, 0, 0)),
                  pl.BlockSpec((1, 1, 1, nb), lambda b, h, j: (b, h, 0, 0))],
        out_specs=pl.BlockSpec((cpb, MOBA_TOPK, IDX_CHUNK), lambda b, h, j: ((b * H + h) * nb + j, 0, 0)),
        out_shape=jax.ShapeDtypeStruct((B * H * nb * cpb, MOBA_TOPK, IDX_CHUNK), I32),
        scratch_shapes=[pltpu.VMEM((1, nb), F32)],
        compiler_params=_cparams(("arbitrary", "arbitrary", "arbitrary")),
        name="moba_route",
    )(q, kmean, base)


def _sc_mesh():
    return plsc.VectorSubcoreMesh(core_axis_name="c", subcore_axis_name="s")


def _sc_scatter_rows(src, pos, n_out_rows):
    N, D = src.shape
    n_chunks = N // IDX_CHUNK
    per_w = n_chunks // SC_WORKERS
    assert per_w * SC_WORKERS == n_chunks

    @functools.partial(
        pl.kernel, mesh=_sc_mesh(),
        out_type=jax.ShapeDtypeStruct((n_out_rows, D), src.dtype),
        scratch_types=[pltpu.VMEM((MOBA_TOPK, IDX_CHUNK), I32),
                       pltpu.VMEM((IDX_CHUNK, D), src.dtype),
                       pltpu.SemaphoreType.DMA],
        name="sc_scatter_rows",
    )
    def k(src_hbm, pos_hbm, out_hbm, idx_v, rows_v, sem):
        wid = lax.axis_index("s") * SC_CORES + lax.axis_index("c")

        @pl.loop(0, per_w)
        def _(i):
            c = wid * per_w + i
            pltpu.sync_copy(pos_hbm.at[c], idx_v)
            pltpu.sync_copy(src_hbm.at[pl.ds(pl.multiple_of(c * IDX_CHUNK, IDX_CHUNK), IDX_CHUNK)], rows_v)
            copies = [pltpu.async_copy(rows_v, out_hbm.at[idx_v.at[r]], sem) for r in range(MOBA_TOPK)]
            for cp in copies:
                cp.wait()

    return k(src, pos)


def _sc_gather_rows(table, pos):
    R, D = table.shape
    n_chunks = pos.shape[0]
    N = n_chunks * IDX_CHUNK
    per_w = n_chunks // SC_WORKERS
    assert per_w * SC_WORKERS == n_chunks

    @functools.partial(
        pl.kernel, mesh=_sc_mesh(),
        out_type=jax.ShapeDtypeStruct((MOBA_TOPK, N, D), table.dtype),
        scratch_types=[pltpu.VMEM((MOBA_TOPK, IDX_CHUNK), I32),
                       pltpu.VMEM((IDX_CHUNK, D), table.dtype),
                       pltpu.SemaphoreType.DMA],
        name="sc_gather_rows",
    )
    def k(table_hbm, pos_hbm, out_hbm, idx_v, rows_v, sem):
        wid = lax.axis_index("s") * SC_CORES + lax.axis_index("c")

        @pl.loop(0, per_w)
        def _(i):
            c = wid * per_w + i
            row0 = pl.multiple_of(c * IDX_CHUNK, IDX_CHUNK)
            pltpu.sync_copy(pos_hbm.at[c], idx_v)
            for r in range(MOBA_TOPK):
                pltpu.async_copy(table_hbm.at[idx_v.at[r]], rows_v, sem).wait()
                pltpu.sync_copy(rows_v, out_hbm.at[r, pl.ds(row0, IDX_CHUNK)])

    return k(table, pos)


def _null_partial(shape):
    lane = lax.broadcasted_iota(I32, shape, 1)
    return jnp.where(lane == M_LANE, NEG_INF, 0.0)


def _moba_group_kernel(tblk_ref, tvalid_ref, nt_ref, qg_ref, k_ref, v_ref, o_ref, *, max_tiles):
    g, t = pl.program_id(0), pl.program_id(1)
    valid = tvalid_ref[g * max_tiles + t]
    n = tblk_ref[g * max_tiles + t]
    scale = HEAD_DIM ** -0.5

    @pl.when(valid > 0)
    def _():
        row = lax.broadcasted_iota(I32, (GROUP_TILE, 1), 0)
        live = row < valid
        q = jnp.where(live, qg_ref[...], 0.0).astype(BF16)
        n0 = pl.multiple_of(n * MOBA_BLOCK, MOBA_BLOCK)
        s = _nt_dot(q, k_ref[0, 0, pl.ds(n0, MOBA_BLOCK), :]) * scale
        m = jnp.max(s, axis=1, keepdims=True).astype(BF16).astype(F32)
        p = jnp.exp(s - m).astype(BF16)
        part = jnp.dot(p, v_ref[0, 0, pl.ds(n0, MOBA_BLOCK), :], preferred_element_type=F32)
        lane = lax.broadcasted_iota(I32, part.shape, 1)
        part = jnp.where(lane == M_LANE, m, part)
        o_ref[...] = jnp.where(live, part, _null_partial(part.shape))

    @pl.when((valid == 0) & (t == max_tiles - 1))
    def _():
        o_ref[...] = _null_partial(o_ref.shape)


def _moba_group(qg, k, vaug, tblk, tvalid, ntiles, max_tiles):
    B, H, S, Dh = k.shape
    G = B * H

    def tile_idx(g, t, tblk, tvalid, nt):
        return (g * max_tiles + jnp.where(t < nt[g], t, max_tiles - 1), 0)

    return pl.pallas_call(
        functools.partial(_moba_group_kernel, max_tiles=max_tiles),
        grid_spec=pltpu.PrefetchScalarGridSpec(
            num_scalar_prefetch=3,
            grid=(G, max_tiles),
            in_specs=[pl.BlockSpec((GROUP_TILE, Dh), tile_idx),
                      pl.BlockSpec((1, 1, S, Dh), lambda g, t, *_: (g // H, g % H, 0, 0)),
                      pl.BlockSpec((1, 1, S, PART_W), lambda g, t, *_: (g // H, g % H, 0, 0))],
            out_specs=pl.BlockSpec((GROUP_TILE, PART_W), tile_idx),
        ),
        out_shape=jax.ShapeDtypeStruct((G * max_tiles * GROUP_TILE, PART_W), F32),
        compiler_params=_cparams(("arbitrary", "arbitrary")),
        name="moba_group",
    )(tblk, tvalid, ntiles, qg, k, vaug)


def _moba_merge_kernel(q_ref, k_ref, v_ref, og_ref, o_ref):
    scale = HEAD_DIM ** -0.5
    q = q_ref[0, 0].astype(BF16)
    s = _nt_dot(q, k_ref[0, 0]) * scale
    qi = lax.broadcasted_iota(I32, s.shape, 0)
    ki = lax.broadcasted_iota(I32, s.shape, 1)
    s = jnp.where(ki <= qi, s, NEG_INF)
    m_own = jnp.max(s, axis=1, keepdims=True)
    p = jnp.exp(s - m_own).astype(BF16)
    own = jnp.dot(p, v_ref[0, 0], preferred_element_type=F32)
    ms = [og_ref[r][:, M_LANE:M_LANE + 1] for r in range(MOBA_TOPK)]
    m_all = m_own
    for m in ms:
        m_all = jnp.maximum(m_all, m)
    tot = jnp.exp(m_own - m_all) * own
    for r in range(MOBA_TOPK):
        tot = tot + jnp.exp(ms[r] - m_all) * og_ref[r]
    o_ref[...] = tot[:, :HEAD_DIM] / tot[:, L_LANE:L_LANE + 1]


def _moba_merge(q, k, vaug, og):
    B, H, S, Dh = q.shape
    nb = S // MOBA_BLOCK
    return pl.pallas_call(
        _moba_merge_kernel,
        grid=(B, H, nb),
        in_specs=[pl.BlockSpec((1, 1, MOBA_BLOCK, Dh), lambda b, h, j: (b, h, j, 0)),
                  pl.BlockSpec((1, 1, MOBA_BLOCK, Dh), lambda b, h, j: (b, h, j, 0)),
                  pl.BlockSpec((1, 1, MOBA_BLOCK, PART_W), lambda b, h, j: (b, h, j, 0)),
                  pl.BlockSpec((MOBA_TOPK, MOBA_BLOCK, PART_W), lambda b, h, j: (0, (b * H + h) * nb + j, 0))],
        out_specs=pl.BlockSpec((MOBA_BLOCK, Dh), lambda b, h, j: (b * nb + j, h)),
        out_shape=jax.ShapeDtypeStruct((B * S, H * Dh), F32),
        compiler_params=_cparams(("arbitrary", "arbitrary", "arbitrary")),
        name="moba_merge",
    )(q, k, vaug, og)


def _group_layout(counts, max_tiles):
    G, nb = counts.shape
    tiles = (counts + GROUP_TILE - 1) // GROUP_TILE
    end = jnp.cumsum(tiles, axis=1)
    start = end - tiles
    t = jnp.arange(max_tiles, dtype=I32)
    tblk = jnp.sum(t[None, :, None] >= end[:, None, :], axis=2).astype(I32)
    tblk = jnp.minimum(tblk, nb - 1)
    c_t = jnp.take_along_axis(counts, tblk, axis=1)
    s_t = jnp.take_along_axis(start, tblk, axis=1)
    tvalid = jnp.clip(c_t - (t[None, :] - s_t) * GROUP_TILE, 0, GROUP_TILE)
    tvalid = jnp.where(t[None, :] < end[:, -1:], tvalid, 0).astype(I32)
    return start * GROUP_TILE, tblk, tvalid, end[:, -1].astype(I32)


def _moba_attention(q, k, vaug, kmean):
    B, H, S, Dh = q.shape
    nb = S // MOBA_BLOCK
    G = B * H
    max_tiles = (MOBA_TOPK * S) // GROUP_TILE + nb + 1
    slab_rows = max_tiles * GROUP_TILE
    null_row0 = (max_tiles - 1) * GROUP_TILE

    counts = _moba_count(q, kmean)
    base, tblk, tvalid, ntiles = _group_layout(counts.reshape(G, nb), max_tiles)
    pos = _moba_route(q, kmean, base.astype(I32).reshape(B, H, 1, nb), slab_rows, null_row0)
    qg = _sc_scatter_rows(q.reshape(G * S, Dh), pos, G * slab_rows)
    parts = _moba_group(qg, k, vaug, tblk.reshape(-1), tvalid.reshape(-1), ntiles, max_tiles)
    og = _sc_gather_rows(parts, pos)
    return _moba_merge(q, k, vaug, og)


def _gated_proj_residual_kernel(o_ref, sg_ref, w_ref, x_ref, out_ref):
    y = (o_ref[...] * sg_ref[...]).astype(BF16)
    out_ref[...] = x_ref[...] + jnp.dot(y, w_ref[...], preferred_element_type=F32)


def _gated_proj_residual(o, sg, w, x2d):
    T, D = x2d.shape
    K = o.shape[1]
    return pl.pallas_call(
        _gated_proj_residual_kernel,
        grid=(T // ROW_TILE,),
        in_specs=[pl.BlockSpec((ROW_TILE, K), lambda i: (i, 0)),
                  pl.BlockSpec((ROW_TILE, K), lambda i: (i, 0)),
                  pl.BlockSpec((K, D), lambda i: (0, 0)),
                  pl.BlockSpec((ROW_TILE, D), lambda i: (i, 0))],
        out_specs=pl.BlockSpec((ROW_TILE, D), lambda i: (i, 0)),
        out_shape=jax.ShapeDtypeStruct((T, D), F32),
        compiler_params=_cparams(("arbitrary",)),
        name="moba_out",
    )(o, sg, w, x2d)


def _rope_tables(S):
    half = HEAD_DIM // 2
    inv = ROPE_THETA ** (-jnp.arange(half, dtype=F32) / half)
    ang = jnp.arange(S, dtype=I32).astype(F32)[:, None] * inv[None, :]
    cos, sin = jnp.cos(ang), jnp.sin(ang)
    return jnp.concatenate([cos, cos], axis=-1), jnp.concatenate([-sin, sin], axis=-1)


def kernel(x, a_norm, a_in_w, a_conv_w, a_conv_b, a_r_w, a_r_b, a_i_w, a_i_b, a_lambda, a_out_w,
           kv_norm, kv_w, k_norm, b_norm, b_in_w, q_norm, b_out_w):
    B, S, D = x.shape
    assert S % ROW_TILE == 0 and ROW_TILE % MOBA_BLOCK == 0
    assert D == N_RG_BLOCKS * RG_BW == N_HEADS * HEAD_DIM
    assert (B * N_HEADS * S) % (IDX_CHUNK * SC_WORKERS) == 0
    xs = x.reshape(B * S, D)
    row = lambda v: v.reshape(1, -1)

    for l in range(a_in_w.shape[0]):
        u = _hawk_in(xs, row(a_norm[l]), a_in_w[l].astype(BF16))
        wri = jnp.concatenate([a_r_w[l], a_i_w[l]], axis=-1).astype(BF16)
        y = _hawk_lru(u, B, S, a_conv_w[l], row(a_conv_b[l]), wri, row(a_r_b[l]), row(a_i_b[l]),
                      row(a_lambda[l]))
        xs = _proj_residual(y, a_out_w[l].astype(BF16), xs, "hawk_out")

    cosf, sinf = _rope_tables(S)
    k, vaug, kmean = _kv_proj(xs, B, S, row(kv_norm), kv_w.astype(BF16), row(k_norm), cosf, sinf)
    kmean = jnp.transpose(kmean, (0, 2, 1, 3))

    for jl in range(b_in_w.shape[0]):
        q, sg = _q_proj(xs, B, S, row(b_norm[jl]), b_in_w[jl].astype(BF16), row(q_norm[jl]), cosf, sinf)
        o = _moba_attention(q, k, vaug, kmean)
        xs = _gated_proj_residual(o, sg, b_out_w[jl].astype(BF16), xs)
    return xs.reshape(B, S, D)
```

```python
import functools

import jax
import jax.numpy as jnp
from jax import lax
from jax.experimental import pallas as pl
from jax.experimental.pallas import tpu as pltpu
from jax.experimental.pallas import tpu_sc as plsc

N_HEADS = 8
HEAD_DIM = 128
MOBA_BLOCK = 256
MOBA_TOPK = 3
CONV_WIDTH = 4
N_RG_BLOCKS = 8
RG_BW = 128
RG_C = 8.0
ROPE_THETA = 10000.0
EPS = 1e-6
NEG_INF = -1e30

SUBLANES = 8
LANES = 128
ROW_TILE = 512
VMEM_LIMIT = 56 * 1024 * 1024

GROUP_TILE = MOBA_BLOCK
TILES_PER_STEP = 4
HEADS_PER_STEP = 4
PART_W = 2 * HEAD_DIM
L_LANE = HEAD_DIM
M_LANE = HEAD_DIM + 1

SC_CORES = 2
SC_SUBCORES = 16
SC_WORKERS = SC_CORES * SC_SUBCORES
IDX_CHUNK = 128

F32 = jnp.float32
BF16 = jnp.bfloat16
I32 = jnp.int32


def _cparams(sem):
    return pltpu.CompilerParams(dimension_semantics=sem, vmem_limit_bytes=VMEM_LIMIT)


def _rms_norm(x, g):
    ms = jnp.mean(x * x, axis=-1, keepdims=True)
    return (x * lax.rsqrt(ms + EPS)) * g


def _silu(x):
    return x / (1.0 + jnp.exp(-x))


def _sigmoid(x):
    return 1.0 / (1.0 + jnp.exp(-x))


def _nt_dot(a, b):
    return lax.dot_general(a, b, (((1,), (1,)), ((), ())), preferred_element_type=F32)


def _head_norm_rope(z, g, cosf, sinf):
    outs = []
    for h in range(N_HEADS):
        zh = z[:, h * HEAD_DIM:(h + 1) * HEAD_DIM]
        zh = _rms_norm(zh, g)
        outs.append(zh * cosf + pltpu.roll(zh, HEAD_DIM // 2, 1) * sinf)
    return outs


def _hawk_in_kernel(x_ref, g_ref, w_ref, u_ref):
    h = _rms_norm(x_ref[...], g_ref[...]).astype(BF16)
    u_ref[...] = jnp.dot(h, w_ref[...], preferred_element_type=F32)


def _hawk_in(x2d, g, w):
    T, D = x2d.shape
    N = w.shape[1]
    return pl.pallas_call(
        _hawk_in_kernel,
        grid=(T // ROW_TILE,),
        in_specs=[pl.BlockSpec((ROW_TILE, D), lambda i: (i, 0)),
                  pl.BlockSpec((1, D), lambda i: (0, 0)),
                  pl.BlockSpec((D, N), lambda i: (0, 0))],
        out_specs=pl.BlockSpec((ROW_TILE, N), lambda i: (i, 0)),
        out_shape=jax.ShapeDtypeStruct((T, N), F32),
        compiler_params=_cparams(("arbitrary",)),
        name="hawk_in",
    )(x2d, g, w)


def _hawk_lru_kernel(xb_ref, gate_ref, cw_ref, cb_ref, wri_ref, rb_ref, ib_ref, lam_ref,
                     y_ref, xpad_ref, a_ref, b_ref, hc_ref):
    tm = xb_ref.shape[0]
    s = pl.program_id(1)

    @pl.when(s == 0)
    def _():
        xpad_ref[pl.ds(0, SUBLANES), :] = jnp.zeros((SUBLANES, xpad_ref.shape[1]), F32)
        hc_ref[...] = jnp.zeros_like(hc_ref)

    xpad_ref[pl.ds(SUBLANES, tm), :] = xb_ref[...]
    cw = cw_ref[...]
    xc = xpad_ref[pl.ds(SUBLANES - (CONV_WIDTH - 1), tm), :] * cw[0:1, :]
    for k in range(1, CONV_WIDTH):
        xc = xc + xpad_ref[pl.ds(SUBLANES - (CONV_WIDTH - 1) + k, tm), :] * cw[k:k + 1, :]
    xc = xc + cb_ref[...]
    xpad_ref[pl.ds(0, SUBLANES), :] = xpad_ref[pl.ds(tm, SUBLANES), :]

    lam = lam_ref[...]
    sp = jnp.maximum(-lam, 0.0) + jnp.log1p(jnp.exp(-jnp.abs(lam)))
    xc16 = xc.astype(BF16)
    for g in range(N_RG_BLOCKS):
        lo = g * RG_BW
        z = jnp.dot(xc16[:, lo:lo + RG_BW], wri_ref[g], preferred_element_type=F32)
        r = _sigmoid(z[:, :RG_BW] + rb_ref[:, lo:lo + RG_BW])
        i = _sigmoid(z[:, RG_BW:] + ib_ref[:, lo:lo + RG_BW])
        log_a = (-RG_C) * r * sp[:, lo:lo + RG_BW]
        a = jnp.exp(log_a)
        mult = jnp.sqrt(1.0 - a * a)
        a_ref[:, lo:lo + RG_BW] = a
        b_ref[:, lo:lo + RG_BW] = mult * (i * xc[:, lo:lo + RG_BW])

    row = lax.broadcasted_iota(I32, (SUBLANES, a_ref.shape[1]), 0)

    def scan_body(c, hc):
        r0 = pl.multiple_of(c * SUBLANES, SUBLANES)
        a = a_ref[pl.ds(r0, SUBLANES), :]
        b = b_ref[pl.ds(r0, SUBLANES), :]
        for k in (1, 2, 4):
            keep = row >= k
            b = jnp.where(keep, a * pltpu.roll(b, k, 0) + b, b)
            a = jnp.where(keep, a * pltpu.roll(a, k, 0), a)
        h = a * hc + b
        b_ref[pl.ds(r0, SUBLANES), :] = h
        return h[SUBLANES - 1:SUBLANES, :]

    hc = lax.fori_loop(0, tm // SUBLANES, scan_body, hc_ref[0:1, :])
    hc_ref[0:1, :] = hc

    y_ref[...] = (b_ref[...] * _silu(gate_ref[...])).astype(BF16)


def _hawk_lru(u, B, S, cw, cb, wri, rb, ib, lam):
    T = u.shape[0]
    C = u.shape[1] // 2
    tm = ROW_TILE
    ns = S // tm
    vec = lambda: pl.BlockSpec((1, C), lambda b, s: (0, 0))
    return pl.pallas_call(
        _hawk_lru_kernel,
        grid=(B, ns),
        in_specs=[pl.BlockSpec((tm, C), lambda b, s: (b * ns + s, 0)),
                  pl.BlockSpec((tm, C), lambda b, s: (b * ns + s, 1)),
                  pl.BlockSpec((CONV_WIDTH, C), lambda b, s: (0, 0)),
                  vec(),
                  pl.BlockSpec((N_RG_BLOCKS, RG_BW, 2 * RG_BW), lambda b, s: (0, 0, 0)),
                  vec(), vec(), vec()],
        out_specs=pl.BlockSpec((tm, C), lambda b, s: (b * ns + s, 0)),
        out_shape=jax.ShapeDtypeStruct((T, C), BF16),
        scratch_shapes=[pltpu.VMEM((tm + SUBLANES, C), F32),
                        pltpu.VMEM((tm, C), F32),
                        pltpu.VMEM((tm, C), F32),
                        pltpu.VMEM((SUBLANES, C), F32)],
        compiler_params=_cparams(("arbitrary", "arbitrary")),
        name="hawk_lru",
    )(u, u, cw, cb, wri, rb, ib, lam)


def _proj_residual_kernel(y_ref, w_ref, x_ref, o_ref):
    o_ref[...] = x_ref[...] + jnp.dot(y_ref[...], w_ref[...], preferred_element_type=F32)


def _proj_residual(y, w, x2d, name):
    T, D = x2d.shape
    K = y.shape[1]
    return pl.pallas_call(
        _proj_residual_kernel,
        grid=(T // ROW_TILE,),
        in_specs=[pl.BlockSpec((ROW_TILE, K), lambda i: (i, 0)),
                  pl.BlockSpec((K, D), lambda i: (0, 0)),
                  pl.BlockSpec((ROW_TILE, D), lambda i: (i, 0))],
        out_specs=pl.BlockSpec((ROW_TILE, D), lambda i: (i, 0)),
        out_shape=jax.ShapeDtypeStruct((T, D), F32),
        compiler_params=_cparams(("arbitrary",)),
        name=name,
    )(y, w, x2d)


def _kv_proj_kernel(x_ref, g_ref, w_ref, kg_ref, cos_ref, sin_ref, k_ref, v_ref, km_ref):
    tm = x_ref.shape[0]
    d_attn = N_HEADS * HEAD_DIM
    h = _rms_norm(x_ref[...], g_ref[...]).astype(BF16)
    kv = jnp.dot(h, w_ref[...], preferred_element_type=F32)
    ks = _head_norm_rope(kv[:, :d_attn], kg_ref[...], cos_ref[...], sin_ref[...])
    lane = lax.broadcasted_iota(I32, (tm, HEAD_DIM), 1)
    ones_col = jnp.where(lane == 0, 1.0, 0.0).astype(BF16)
    for hd in range(N_HEADS):
        k_ref[0, hd] = ks[hd].astype(BF16)
        vh = kv[:, d_attn + hd * HEAD_DIM:d_attn + (hd + 1) * HEAD_DIM].astype(BF16)
        v_ref[0, hd] = jnp.concatenate([vh, ones_col], axis=1)
        for r in range(tm // MOBA_BLOCK):
            blk = ks[hd][r * MOBA_BLOCK:(r + 1) * MOBA_BLOCK, :]
            km_ref[0, r, pl.ds(hd, 1), :] = jnp.mean(blk, axis=0, keepdims=True)


def _kv_proj(x2d, B, S, g, w, kg, cosf, sinf):
    T, D = x2d.shape
    tm = ROW_TILE
    ns = S // tm
    nb = S // MOBA_BLOCK
    rpt = tm // MOBA_BLOCK
    return pl.pallas_call(
        _kv_proj_kernel,
        grid=(B, ns),
        in_specs=[pl.BlockSpec((tm, D), lambda b, s: (b * ns + s, 0)),
                  pl.BlockSpec((1, D), lambda b, s: (0, 0)),
                  pl.BlockSpec((D, 2 * N_HEADS * HEAD_DIM), lambda b, s: (0, 0)),
                  pl.BlockSpec((1, HEAD_DIM), lambda b, s: (0, 0)),
                  pl.BlockSpec((tm, HEAD_DIM), lambda b, s: (s, 0)),
                  pl.BlockSpec((tm, HEAD_DIM), lambda b, s: (s, 0))],
        out_specs=[pl.BlockSpec((1, N_HEADS, tm, HEAD_DIM), lambda b, s: (b, 0, s, 0)),
                   pl.BlockSpec((1, N_HEADS, tm, PART_W), lambda b, s: (b, 0, s, 0)),
                   pl.BlockSpec((1, rpt, N_HEADS, HEAD_DIM), lambda b, s: (b, s, 0, 0))],
        out_shape=[jax.ShapeDtypeStruct((B, N_HEADS, S, HEAD_DIM), BF16),
                   jax.ShapeDtypeStruct((B, N_HEADS, S, PART_W), BF16),
                   jax.ShapeDtypeStruct((B, nb, N_HEADS, HEAD_DIM), F32)],
        compiler_params=_cparams(("arbitrary", "arbitrary")),
        name="kv_proj",
    )(x2d, g, w, kg, cosf, sinf)


def _q_proj_kernel(x_ref, g_ref, w_ref, qg_ref, cos_ref, sin_ref, q_ref, sg_ref):
    d_attn = N_HEADS * HEAD_DIM
    h = _rms_norm(x_ref[...], g_ref[...]).astype(BF16)
    u = jnp.dot(h, w_ref[...], preferred_element_type=F32)
    qs = _head_norm_rope(u[:, :d_attn], qg_ref[...], cos_ref[...], sin_ref[...])
    for hd in range(N_HEADS):
        q_ref[0, hd] = qs[hd]
    sg_ref[...] = _silu(u[:, d_attn:])


def _q_proj(x2d, B, S, g, w, qg, cosf, sinf):
    T, D = x2d.shape
    tm = ROW_TILE
    ns = S // tm
    d_attn = N_HEADS * HEAD_DIM
    return pl.pallas_call(
        _q_proj_kernel,
        grid=(B, ns),
        in_specs=[pl.BlockSpec((tm, D), lambda b, s: (b * ns + s, 0)),
                  pl.BlockSpec((1, D), lambda b, s: (0, 0)),
                  pl.BlockSpec((D, 2 * d_attn), lambda b, s: (0, 0)),
                  pl.BlockSpec((1, HEAD_DIM), lambda b, s: (0, 0)),
                  pl.BlockSpec((tm, HEAD_DIM), lambda b, s: (s, 0)),
                  pl.BlockSpec((tm, HEAD_DIM), lambda b, s: (s, 0))],
        out_specs=[pl.BlockSpec((1, N_HEADS, tm, HEAD_DIM), lambda b, s: (b, 0, s, 0)),
                   pl.BlockSpec((tm, d_attn), lambda b, s: (b * ns + s, 0))],
        out_shape=[jax.ShapeDtypeStruct((B, N_HEADS, S, HEAD_DIM), F32),
                   jax.ShapeDtypeStruct((T, d_attn), F32)],
        compiler_params=_cparams(("arbitrary", "arbitrary")),
        name="q_proj",
    )(x2d, g, w, qg, cosf, sinf)


def _moba_route_kernel(q_ref, km_ref, rt_ref, cnt_ref, run_ref):
    j = pl.program_id(2)
    nb = km_ref.shape[2]

    @pl.when(j == 0)
    def _():
        run_ref[...] = jnp.zeros_like(run_ref)

    g = _nt_dot(km_ref[0, 0].astype(BF16), q_ref[0, 0].astype(BF16))
    blk = lax.broadcasted_iota(I32, g.shape, 0)
    g = jnp.where(blk < j, g, -jnp.inf)
    hits, ids = [], []
    for _ in range(MOBA_TOPK):
        mx = jnp.max(g, axis=0, keepdims=True)
        idx = jnp.min(jnp.where(g == mx, blk, nb), axis=0, keepdims=True)
        hit = blk == idx
        ok = mx > -jnp.inf
        hits.append(jnp.where(hit & ok, 1.0, 0.0))
        ids.append(jnp.where(ok, idx, -1))
        g = jnp.where(hit, -jnp.inf, g)
    sel = hits[0] + hits[1] + hits[2]

    qa = lax.broadcasted_iota(I32, (MOBA_BLOCK, MOBA_BLOCK), 0)
    qb = lax.broadcasted_iota(I32, (MOBA_BLOCK, MOBA_BLOCK), 1)
    earlier = jnp.where(qa < qb, 1.0, 0.0).astype(BF16)
    rank = jnp.dot(sel.astype(BF16), earlier, preferred_element_type=F32)
    dest = run_ref[...] + rank
    run = run_ref[...] + jnp.sum(sel, axis=1, keepdims=True)
    run_ref[...] = run
    cnt_ref[0] = run.astype(I32)

    for r in range(MOBA_TOPK):
        rt_ref[0, pl.ds(r, 1), :] = jnp.sum(hits[r] * dest, axis=0, keepdims=True).astype(I32)
        rt_ref[0, pl.ds(MOBA_TOPK + r, 1), :] = ids[r]
    rt_ref[0, pl.ds(2 * MOBA_TOPK, SUBLANES - 2 * MOBA_TOPK), :] = jnp.zeros(
        (SUBLANES - 2 * MOBA_TOPK, MOBA_BLOCK), I32)


def _moba_route(q, kmean):
    B, H, S, Dh = q.shape
    nb = S // MOBA_BLOCK
    return pl.pallas_call(
        _moba_route_kernel,
        grid=(B, H, nb),
        in_specs=[pl.BlockSpec((1, 1, MOBA_BLOCK, Dh), lambda b, h, j: (b, h, j, 0)),
                  pl.BlockSpec((1, 1, nb, Dh), lambda b, h, j: (b, h, 0, 0))],
        out_specs=[pl.BlockSpec((1, SUBLANES, MOBA_BLOCK), lambda b, h, j: ((b * H + h) * nb + j, 0, 0)),
                   pl.BlockSpec((1, nb, 1), lambda b, h, j: (b * H + h, 0, 0))],
        out_shape=[jax.ShapeDtypeStruct((B * H * nb, SUBLANES, MOBA_BLOCK), I32),
                   jax.ShapeDtypeStruct((B * H, nb, 1), I32)],
        scratch_shapes=[pltpu.VMEM((nb, 1), F32)],
        compiler_params=_cparams(("arbitrary", "arbitrary", "arbitrary")),
        name="moba_route",
    )(q, kmean)


def _moba_pos_kernel(base_ref, rt_ref, pos_ref, *, slab_rows, null_row0):
    g = pl.program_id(0)
    nb = rt_ref.shape[0]
    lane = lax.broadcasted_iota(I32, (nb, IDX_CHUNK), 1)
    for r in range(MOBA_TOPK):
        for c in range(MOBA_BLOCK // IDX_CHUNK):
            loc = rt_ref[:, r, pl.ds(c * IDX_CHUNK, IDX_CHUNK)]
            blk = rt_ref[:, MOBA_TOPK + r, pl.ds(c * IDX_CHUNK, IDX_CHUNK)]
            base = jnp.zeros_like(loc)
            for n in range(nb):
                base = jnp.where(blk == n, base_ref[g * nb + n], base)
            row = jnp.where(blk >= 0, base + loc, null_row0 + c * IDX_CHUNK + lane)
            pos_ref[:, c, r, :] = row + g * slab_rows


def _moba_pos(rt, base, nb, slab_rows, null_row0):
    G = rt.shape[0] // nb
    cpb = MOBA_BLOCK // IDX_CHUNK
    pos = pl.pallas_call(
        functools.partial(_moba_pos_kernel, slab_rows=slab_rows, null_row0=null_row0),
        grid_spec=pltpu.PrefetchScalarGridSpec(
            num_scalar_prefetch=1,
            grid=(G,),
            in_specs=[pl.BlockSpec((nb, SUBLANES, MOBA_BLOCK), lambda g, base: (g, 0, 0))],
            out_specs=pl.BlockSpec((nb, cpb, MOBA_TOPK, IDX_CHUNK), lambda g, base: (g, 0, 0, 0)),
        ),
        out_shape=jax.ShapeDtypeStruct((G * nb, cpb, MOBA_TOPK, IDX_CHUNK), I32),
        compiler_params=_cparams(("arbitrary",)),
        name="moba_pos",
    )(base, rt)
    return pos.reshape(G * nb * cpb, MOBA_TOPK, IDX_CHUNK)


def _sc_mesh():
    return plsc.VectorSubcoreMesh(core_axis_name="c", subcore_axis_name="s")


def _sc_scatter_rows(src, pos, n_out_rows):
    N, D = src.shape
    n_chunks = N // IDX_CHUNK
    per_w = n_chunks // SC_WORKERS
    assert per_w * SC_WORKERS == n_chunks

    @functools.partial(
        pl.kernel, mesh=_sc_mesh(),
        out_type=jax.ShapeDtypeStruct((n_out_rows, D), src.dtype),
        scratch_types=[pltpu.VMEM((MOBA_TOPK, IDX_CHUNK), I32),
                       pltpu.VMEM((IDX_CHUNK, D), src.dtype),
                       pltpu.SemaphoreType.DMA],
        name="sc_scatter_rows",
    )
    def k(src_hbm, pos_hbm, out_hbm, idx_v, rows_v, sem):
        wid = lax.axis_index("s") * SC_CORES + lax.axis_index("c")

        @pl.loop(0, per_w)
        def _(i):
            c = wid * per_w + i
            pltpu.sync_copy(pos_hbm.at[c], idx_v)
            pltpu.sync_copy(src_hbm.at[pl.ds(pl.multiple_of(c * IDX_CHUNK, IDX_CHUNK), IDX_CHUNK)], rows_v)
            copies = [pltpu.async_copy(rows_v, out_hbm.at[idx_v.at[r]], sem) for r in range(MOBA_TOPK)]
            for cp in copies:
                cp.wait()

    return k(src, pos)


def _sc_gather_rows(table, pos):
    R, D = table.shape
    n_chunks = pos.shape[0]
    N = n_chunks * IDX_CHUNK
    per_w = n_chunks // SC_WORKERS
    assert per_w * SC_WORKERS == n_chunks

    @functools.partial(
        pl.kernel, mesh=_sc_mesh(),
        out_type=jax.ShapeDtypeStruct((MOBA_TOPK, N, D), table.dtype),
        scratch_types=[pltpu.VMEM((MOBA_TOPK, IDX_CHUNK), I32),
                       pltpu.VMEM((IDX_CHUNK, D), table.dtype),
                       pltpu.SemaphoreType.DMA],
        name="sc_gather_rows",
    )
    def k(table_hbm, pos_hbm, out_hbm, idx_v, rows_v, sem):
        wid = lax.axis_index("s") * SC_CORES + lax.axis_index("c")

        @pl.loop(0, per_w)
        def _(i):
            c = wid * per_w + i
            row0 = pl.multiple_of(c * IDX_CHUNK, IDX_CHUNK)
            pltpu.sync_copy(pos_hbm.at[c], idx_v)
            for r in range(MOBA_TOPK):
                pltpu.async_copy(table_hbm.at[idx_v.at[r]], rows_v, sem).wait()
                pltpu.sync_copy(rows_v, out_hbm.at[r, pl.ds(row0, IDX_CHUNK)])

    return k(table, pos)


def _null_partial(shape):
    lane = lax.broadcasted_iota(I32, shape, 1)
    return jnp.where(lane == M_LANE, NEG_INF, 0.0)


def _moba_group_kernel(tblk_ref, tvalid_ref, nt_ref, qg_ref, k_ref, v_ref, o_ref, *, max_tiles):
    g, step = pl.program_id(0), pl.program_id(1)
    t0 = g * max_tiles + step * TILES_PER_STEP
    scale = HEAD_DIM ** -0.5

    @pl.when((tvalid_ref[t0] > 0) | (step == pl.num_programs(1) - 1))
    def _():
        row = lax.broadcasted_iota(I32, (GROUP_TILE, 1), 0)
        for u in range(TILES_PER_STEP):
            rows = pl.ds(u * GROUP_TILE, GROUP_TILE)
            live = row < tvalid_ref[t0 + u]
            q = jnp.where(live, qg_ref[rows, :], 0.0).astype(BF16)
            n0 = pl.multiple_of(tblk_ref[t0 + u] * MOBA_BLOCK, MOBA_BLOCK)
            s = _nt_dot(q, k_ref[0, 0, pl.ds(n0, MOBA_BLOCK), :]) * scale
            m = jnp.max(s, axis=1, keepdims=True).astype(BF16).astype(F32)
            p = jnp.exp(s - m).astype(BF16)
            part = jnp.dot(p, v_ref[0, 0, pl.ds(n0, MOBA_BLOCK), :], preferred_element_type=F32)
            lane = lax.broadcasted_iota(I32, part.shape, 1)
            part = jnp.where(lane == M_LANE, m, part)
            o_ref[rows, :] = jnp.where(live, part, _null_partial(part.shape))


def _moba_group(qg, k, vaug, tblk, tvalid, ntiles, max_tiles):
    B, H, S, Dh = k.shape
    G = B * H
    n_steps = max_tiles // TILES_PER_STEP
    step_rows = TILES_PER_STEP * GROUP_TILE

    def tile_idx(g, s, tblk, tvalid, nt):
        return (g * n_steps + jnp.where(s * TILES_PER_STEP < nt[g], s, n_steps - 1), 0)

    return pl.pallas_call(
        functools.partial(_moba_group_kernel, max_tiles=max_tiles),
        grid_spec=pltpu.PrefetchScalarGridSpec(
            num_scalar_prefetch=3,
            grid=(G, n_steps),
            in_specs=[pl.BlockSpec((step_rows, Dh), tile_idx),
                      pl.BlockSpec((1, 1, S, Dh), lambda g, s, *_: (g // H, g % H, 0, 0)),
                      pl.BlockSpec((1, 1, S, PART_W), lambda g, s, *_: (g // H, g % H, 0, 0))],
            out_specs=pl.BlockSpec((step_rows, PART_W), tile_idx),
        ),
        out_shape=jax.ShapeDtypeStruct((G * max_tiles * GROUP_TILE, PART_W), F32),
        compiler_params=_cparams(("arbitrary", "arbitrary")),
        name="moba_group",
    )(tblk, tvalid, ntiles, qg, k, vaug)


def _moba_merge_kernel(q_ref, k_ref, v_ref, *refs):
    og_refs, o_ref = refs[:-1], refs[-1]
    scale = HEAD_DIM ** -0.5
    qi = lax.broadcasted_iota(I32, (MOBA_BLOCK, MOBA_BLOCK), 0)
    ki = lax.broadcasted_iota(I32, (MOBA_BLOCK, MOBA_BLOCK), 1)
    for u, og_ref in enumerate(og_refs):
        q = q_ref[0, u].astype(BF16)
        s = _nt_dot(q, k_ref[0, u]) * scale
        s = jnp.where(ki <= qi, s, NEG_INF)
        m_own = jnp.max(s, axis=1, keepdims=True)
        p = jnp.exp(s - m_own).astype(BF16)
        own = jnp.dot(p, v_ref[0, u], preferred_element_type=F32)
        ms = [og_ref[r][:, M_LANE:M_LANE + 1] for r in range(MOBA_TOPK)]
        m_all = m_own
        for m in ms:
            m_all = jnp.maximum(m_all, m)
        tot = jnp.exp(m_own - m_all) * own
        for r in range(MOBA_TOPK):
            tot = tot + jnp.exp(ms[r] - m_all) * og_ref[r]
        o_ref[:, u * HEAD_DIM:(u + 1) * HEAD_DIM] = tot[:, :HEAD_DIM] / tot[:, L_LANE:L_LANE + 1]


def _moba_merge(q, k, vaug, og):
    B, H, S, Dh = q.shape
    nb = S // MOBA_BLOCK
    hps = HEADS_PER_STEP
    og_spec = lambda u: pl.BlockSpec((MOBA_TOPK, MOBA_BLOCK, PART_W),
                                     lambda b, hp, j: (0, (b * H + hp * hps + u) * nb + j, 0))
    return pl.pallas_call(
        _moba_merge_kernel,
        grid=(B, H // hps, nb),
        in_specs=[pl.BlockSpec((1, hps, MOBA_BLOCK, Dh), lambda b, hp, j: (b, hp, j, 0)),
                  pl.BlockSpec((1, hps, MOBA_BLOCK, Dh), lambda b, hp, j: (b, hp, j, 0)),
                  pl.BlockSpec((1, hps, MOBA_BLOCK, PART_W), lambda b, hp, j: (b, hp, j, 0))]
                 + [og_spec(u) for u in range(hps)],
        out_specs=pl.BlockSpec((MOBA_BLOCK, hps * Dh), lambda b, hp, j: (b * nb + j, hp)),
        out_shape=jax.ShapeDtypeStruct((B * S, H * Dh), F32),
        compiler_params=_cparams(("arbitrary", "arbitrary", "arbitrary")),
        name="moba_merge",
    )(q, k, vaug, *([og] * hps))


def _group_layout(counts, max_tiles):
    G, nb = counts.shape
    tiles = (counts + GROUP_TILE - 1) // GROUP_TILE
    end = jnp.cumsum(tiles, axis=1)
    start = end - tiles
    t = jnp.arange(max_tiles, dtype=I32)
    tblk = jnp.sum(t[None, :, None] >= end[:, None, :], axis=2).astype(I32)
    tblk = jnp.minimum(tblk, nb - 1)
    c_t = jnp.take_along_axis(counts, tblk, axis=1)
    s_t = jnp.take_along_axis(start, tblk, axis=1)
    tvalid = jnp.clip(c_t - (t[None, :] - s_t) * GROUP_TILE, 0, GROUP_TILE)
    tvalid = jnp.where(t[None, :] < end[:, -1:], tvalid, 0).astype(I32)
    return start * GROUP_TILE, tblk, tvalid, end[:, -1].astype(I32)


def _moba_attention(q, k, vaug, kmean):
    B, H, S, Dh = q.shape
    nb = S // MOBA_BLOCK
    G = B * H
    max_tiles = (MOBA_TOPK * S) // GROUP_TILE + nb + 1
    max_tiles = -(-max_tiles // TILES_PER_STEP) * TILES_PER_STEP
    slab_rows = max_tiles * GROUP_TILE
    null_row0 = (max_tiles - 1) * GROUP_TILE

    rt, counts = _moba_route(q, kmean)
    base, tblk, tvalid, ntiles = _group_layout(counts.reshape(G, nb), max_tiles)
    pos = _moba_pos(rt, base.astype(I32).reshape(-1), nb, slab_rows, null_row0)
    qg = _sc_scatter_rows(q.reshape(G * S, Dh), pos, G * slab_rows)
    parts = _moba_group(qg, k, vaug, tblk.reshape(-1), tvalid.reshape(-1), ntiles, max_tiles)
    og = _sc_gather_rows(parts, pos)
    return _moba_merge(q, k, vaug, og)


def _gated_proj_residual_kernel(o_ref, sg_ref, w_ref, x_ref, out_ref):
    y = (o_ref[...] * sg_ref[...]).astype(BF16)
    out_ref[...] = x_ref[...] + jnp.dot(y, w_ref[...], preferred_element_type=F32)


def _gated_proj_residual(o, sg, w, x2d):
    T, D = x2d.shape
    K = o.shape[1]
    return pl.pallas_call(
        _gated_proj_residual_kernel,
        grid=(T // ROW_TILE,),
        in_specs=[pl.BlockSpec((ROW_TILE, K), lambda i: (i, 0)),
                  pl.BlockSpec((ROW_TILE, K), lambda i: (i, 0)),
                  pl.BlockSpec((K, D), lambda i: (0, 0)),
                  pl.BlockSpec((ROW_TILE, D), lambda i: (i, 0))],
        out_specs=pl.BlockSpec((ROW_TILE, D), lambda i: (i, 0)),
        out_shape=jax.ShapeDtypeStruct((T, D), F32),
        compiler_params=_cparams(("arbitrary",)),
        name="moba_out",
    )(o, sg, w, x2d)


def _rope_tables(S):
    half = HEAD_DIM // 2
    inv = ROPE_THETA ** (-jnp.arange(half, dtype=F32) / half)
    ang = jnp.arange(S, dtype=I32).astype(F32)[:, None] * inv[None, :]
    cos, sin = jnp.cos(ang), jnp.sin(ang)
    return jnp.concatenate([cos, cos], axis=-1), jnp.concatenate([-sin, sin], axis=-1)


def kernel(x, a_norm, a_in_w, a_conv_w, a_conv_b, a_r_w, a_r_b, a_i_w, a_i_b, a_lambda, a_out_w,
           kv_norm, kv_w, k_norm, b_norm, b_in_w, q_norm, b_out_w):
    B, S, D = x.shape
    assert S % ROW_TILE == 0 and ROW_TILE % MOBA_BLOCK == 0
    assert D == N_RG_BLOCKS * RG_BW == N_HEADS * HEAD_DIM
    assert (B * N_HEADS * S) % (IDX_CHUNK * SC_WORKERS) == 0
    xs = x.reshape(B * S, D)
    row = lambda v: v.reshape(1, -1)

    for l in range(a_in_w.shape[0]):
        u = _hawk_in(xs, row(a_norm[l]), a_in_w[l].astype(BF16))
        wri = jnp.concatenate([a_r_w[l], a_i_w[l]], axis=-1).astype(BF16)
        y = _hawk_lru(u, B, S, a_conv_w[l], row(a_conv_b[l]), wri, row(a_r_b[l]), row(a_i_b[l]),
                      row(a_lambda[l]))
        xs = _proj_residual(y, a_out_w[l].astype(BF16), xs, "hawk_out")

    cosf, sinf = _rope_tables(S)
    k, vaug, kmean = _kv_proj(xs, B, S, row(kv_norm), kv_w.astype(BF16), row(k_norm), cosf, sinf)
    kmean = jnp.transpose(kmean, (0, 2, 1, 3))

    for jl in range(b_in_w.shape[0]):
        q, sg = _q_proj(xs, B, S, row(b_norm[jl]), b_in_w[jl].astype(BF16), row(q_norm[jl]), cosf, sinf)
        o = _moba_attention(q, k, vaug, kmean)
        xs = _gated_proj_residual(o, sg, b_out_w[jl].astype(BF16), xs)
    return xs.reshape(B, S, D)
```

```python
import functools

import jax
import jax.numpy as jnp
import numpy as np
from jax import lax
from jax.experimental import pallas as pl
from jax.experimental.pallas import tpu as pltpu
from jax.experimental.pallas import tpu_sc as plsc

N_HEADS = 8
HEAD_DIM = 128
MOBA_BLOCK = 256
MOBA_TOPK = 3
CONV_WIDTH = 4
N_RG_BLOCKS = 8
RG_BW = 128
RG_C = 8.0
ROPE_THETA = 10000.0
EPS = 1e-6
NEG_INF = -1e30

SUBLANES = 8
LANES = 128
ROW_TILE = 512
VMEM_LIMIT = 56 * 1024 * 1024

GROUP_TILE = MOBA_BLOCK
TILES_PER_STEP = 4
HEADS_PER_STEP = 4
ROUTE_TILES = 4
HALF = HEAD_DIM // 2
VA_W = 2 * HEAD_DIM
L_LANE = HALF
M_LANE = HALF + 1
HI16_MASK = -65536
NEG_INF_BITS = int(np.float32(NEG_INF).view(np.int32))

SC_CORES = 2
SC_SUBCORES = 16
SC_WORKERS = SC_CORES * SC_SUBCORES
IDX_CHUNK = 128

F32 = jnp.float32
BF16 = jnp.bfloat16
I32 = jnp.int32


def _cparams(sem):
    return pltpu.CompilerParams(dimension_semantics=sem, vmem_limit_bytes=VMEM_LIMIT)


def _rms_norm(x, g):
    ms = jnp.mean(x * x, axis=-1, keepdims=True)
    return (x * lax.rsqrt(ms + EPS)) * g


def _silu(x):
    return x / (1.0 + jnp.exp(-x))


def _sigmoid(x):
    return 1.0 / (1.0 + jnp.exp(-x))


def _nt_dot(a, b):
    return lax.dot_general(a, b, (((1,), (1,)), ((), ())), preferred_element_type=F32)


def _head_norm_rope(z, g, cosf, sinf):
    outs = []
    for h in range(N_HEADS):
        zh = z[:, h * HEAD_DIM:(h + 1) * HEAD_DIM]
        zh = _rms_norm(zh, g)
        outs.append(zh * cosf + pltpu.roll(zh, HEAD_DIM // 2, 1) * sinf)
    return outs


def _hawk_in_kernel(x_ref, g_ref, w_ref, u_ref):
    h = _rms_norm(x_ref[...], g_ref[...]).astype(BF16)
    u_ref[...] = jnp.dot(h, w_ref[...], preferred_element_type=F32)


def _hawk_in(x2d, g, w):
    T, D = x2d.shape
    N = w.shape[1]
    return pl.pallas_call(
        _hawk_in_kernel,
        grid=(T // ROW_TILE,),
        in_specs=[pl.BlockSpec((ROW_TILE, D), lambda i: (i, 0)),
                  pl.BlockSpec((1, D), lambda i: (0, 0)),
                  pl.BlockSpec((D, N), lambda i: (0, 0))],
        out_specs=pl.BlockSpec((ROW_TILE, N), lambda i: (i, 0)),
        out_shape=jax.ShapeDtypeStruct((T, N), F32),
        compiler_params=_cparams(("arbitrary",)),
        name="hawk_in",
    )(x2d, g, w)


def _hawk_lru_kernel(xb_ref, gate_ref, cw_ref, cb_ref, wri_ref, rb_ref, ib_ref, lam_ref,
                     y_ref, xpad_ref, a_ref, b_ref, hc_ref):
    tm = xb_ref.shape[0]
    s = pl.program_id(1)

    @pl.when(s == 0)
    def _():
        xpad_ref[pl.ds(0, SUBLANES), :] = jnp.zeros((SUBLANES, xpad_ref.shape[1]), F32)
        hc_ref[...] = jnp.zeros_like(hc_ref)

    xpad_ref[pl.ds(SUBLANES, tm), :] = xb_ref[...]
    cw = cw_ref[...]
    xc = xpad_ref[pl.ds(SUBLANES - (CONV_WIDTH - 1), tm), :] * cw[0:1, :]
    for k in range(1, CONV_WIDTH):
        xc = xc + xpad_ref[pl.ds(SUBLANES - (CONV_WIDTH - 1) + k, tm), :] * cw[k:k + 1, :]
    xc = xc + cb_ref[...]
    xpad_ref[pl.ds(0, SUBLANES), :] = xpad_ref[pl.ds(tm, SUBLANES), :]

    lam = lam_ref[...]
    sp = jnp.maximum(-lam, 0.0) + jnp.log1p(jnp.exp(-jnp.abs(lam)))
    xc16 = xc.astype(BF16)
    for g in range(N_RG_BLOCKS):
        lo = g * RG_BW
        z = jnp.dot(xc16[:, lo:lo + RG_BW], wri_ref[g], preferred_element_type=F32)
        r = _sigmoid(z[:, :RG_BW] + rb_ref[:, lo:lo + RG_BW])
        i = _sigmoid(z[:, RG_BW:] + ib_ref[:, lo:lo + RG_BW])
        log_a = (-RG_C) * r * sp[:, lo:lo + RG_BW]
        a = jnp.exp(log_a)
        mult = jnp.sqrt(1.0 - a * a)
        a_ref[:, lo:lo + RG_BW] = a
        b_ref[:, lo:lo + RG_BW] = mult * (i * xc[:, lo:lo + RG_BW])

    row = lax.broadcasted_iota(I32, (SUBLANES, a_ref.shape[1]), 0)

    def scan_body(c, hc):
        r0 = pl.multiple_of(c * SUBLANES, SUBLANES)
        a = a_ref[pl.ds(r0, SUBLANES), :]
        b = b_ref[pl.ds(r0, SUBLANES), :]
        for k in (1, 2, 4):
            keep = row >= k
            b = jnp.where(keep, a * pltpu.roll(b, k, 0) + b, b)
            a = jnp.where(keep, a * pltpu.roll(a, k, 0), a)
        h = a * hc + b
        b_ref[pl.ds(r0, SUBLANES), :] = h
        return h[SUBLANES - 1:SUBLANES, :]

    hc = lax.fori_loop(0, tm // SUBLANES, scan_body, hc_ref[0:1, :])
    hc_ref[0:1, :] = hc

    y_ref[...] = (b_ref[...] * _silu(gate_ref[...])).astype(BF16)


def _hawk_lru(u, B, S, cw, cb, wri, rb, ib, lam):
    T = u.shape[0]
    C = u.shape[1] // 2
    tm = ROW_TILE
    ns = S // tm
    vec = lambda: pl.BlockSpec((1, C), lambda b, s: (0, 0))
    return pl.pallas_call(
        _hawk_lru_kernel,
        grid=(B, ns),
        in_specs=[pl.BlockSpec((tm, C), lambda b, s: (b * ns + s, 0)),
                  pl.BlockSpec((tm, C), lambda b, s: (b * ns + s, 1)),
                  pl.BlockSpec((CONV_WIDTH, C), lambda b, s: (0, 0)),
                  vec(),
                  pl.BlockSpec((N_RG_BLOCKS, RG_BW, 2 * RG_BW), lambda b, s: (0, 0, 0)),
                  vec(), vec(), vec()],
        out_specs=pl.BlockSpec((tm, C), lambda b, s: (b * ns + s, 0)),
        out_shape=jax.ShapeDtypeStruct((T, C), BF16),
        scratch_shapes=[pltpu.VMEM((tm + SUBLANES, C), F32),
                        pltpu.VMEM((tm, C), F32),
                        pltpu.VMEM((tm, C), F32),
                        pltpu.VMEM((SUBLANES, C), F32)],
        compiler_params=_cparams(("arbitrary", "arbitrary")),
        name="hawk_lru",
    )(u, u, cw, cb, wri, rb, ib, lam)


def _proj_residual_kernel(y_ref, w_ref, x_ref, o_ref):
    o_ref[...] = x_ref[...] + jnp.dot(y_ref[...], w_ref[...], preferred_element_type=F32)


def _proj_residual(y, w, x2d, name):
    T, D = x2d.shape
    K = y.shape[1]
    return pl.pallas_call(
        _proj_residual_kernel,
        grid=(T // ROW_TILE,),
        in_specs=[pl.BlockSpec((ROW_TILE, K), lambda i: (i, 0)),
                  pl.BlockSpec((K, D), lambda i: (0, 0)),
                  pl.BlockSpec((ROW_TILE, D), lambda i: (i, 0))],
        out_specs=pl.BlockSpec((ROW_TILE, D), lambda i: (i, 0)),
        out_shape=jax.ShapeDtypeStruct((T, D), F32),
        compiler_params=_cparams(("arbitrary",)),
        name=name,
    )(y, w, x2d)


def _kv_proj_kernel(x_ref, g_ref, w_ref, kg_ref, cos_ref, sin_ref, k_ref, v_ref, km_ref):
    tm = x_ref.shape[0]
    d_attn = N_HEADS * HEAD_DIM
    h = _rms_norm(x_ref[...], g_ref[...]).astype(BF16)
    kv = jnp.dot(h, w_ref[...], preferred_element_type=F32)
    ks = _head_norm_rope(kv[:, :d_attn], kg_ref[...], cos_ref[...], sin_ref[...])
    lane = lax.broadcasted_iota(I32, (tm, HEAD_DIM), 1)
    for hd in range(N_HEADS):
        k_ref[0, hd] = ks[hd].astype(BF16)
        vh = kv[:, d_attn + hd * HEAD_DIM:d_attn + (hd + 1) * HEAD_DIM]
        lo = jnp.where(lane < HALF, vh, jnp.where(lane == L_LANE, 1.0, 0.0))
        hi = jnp.where(lane < HALF, pltpu.roll(vh, HALF, 1), 0.0)
        v_ref[0, hd] = jnp.concatenate([lo, hi], axis=1).astype(BF16)
        for r in range(tm // MOBA_BLOCK):
            blk = ks[hd][r * MOBA_BLOCK:(r + 1) * MOBA_BLOCK, :]
            km_ref[0, r, pl.ds(hd, 1), :] = jnp.mean(blk, axis=0, keepdims=True)


def _kv_proj(x2d, B, S, g, w, kg, cosf, sinf):
    T, D = x2d.shape
    tm = ROW_TILE
    ns = S // tm
    nb = S // MOBA_BLOCK
    rpt = tm // MOBA_BLOCK
    return pl.pallas_call(
        _kv_proj_kernel,
        grid=(B, ns),
        in_specs=[pl.BlockSpec((tm, D), lambda b, s: (b * ns + s, 0)),
                  pl.BlockSpec((1, D), lambda b, s: (0, 0)),
                  pl.BlockSpec((D, 2 * N_HEADS * HEAD_DIM), lambda b, s: (0, 0)),
                  pl.BlockSpec((1, HEAD_DIM), lambda b, s: (0, 0)),
                  pl.BlockSpec((tm, HEAD_DIM), lambda b, s: (s, 0)),
                  pl.BlockSpec((tm, HEAD_DIM), lambda b, s: (s, 0))],
        out_specs=[pl.BlockSpec((1, N_HEADS, tm, HEAD_DIM), lambda b, s: (b, 0, s, 0)),
                   pl.BlockSpec((1, N_HEADS, tm, VA_W), lambda b, s: (b, 0, s, 0)),
                   pl.BlockSpec((1, rpt, N_HEADS, HEAD_DIM), lambda b, s: (b, s, 0, 0))],
        out_shape=[jax.ShapeDtypeStruct((B, N_HEADS, S, HEAD_DIM), BF16),
                   jax.ShapeDtypeStruct((B, N_HEADS, S, VA_W), BF16),
                   jax.ShapeDtypeStruct((B, nb, N_HEADS, HEAD_DIM), F32)],
        compiler_params=_cparams(("arbitrary", "arbitrary")),
        name="kv_proj",
    )(x2d, g, w, kg, cosf, sinf)


def _q_proj_kernel(x_ref, g_ref, w_ref, qg_ref, cos_ref, sin_ref, q_ref, sg_ref):
    d_attn = N_HEADS * HEAD_DIM
    h = _rms_norm(x_ref[...], g_ref[...]).astype(BF16)
    u = jnp.dot(h, w_ref[...], preferred_element_type=F32)
    qs = _head_norm_rope(u[:, :d_attn], qg_ref[...], cos_ref[...], sin_ref[...])
    for hd in range(N_HEADS):
        q_ref[0, hd] = qs[hd]
    sg_ref[...] = _silu(u[:, d_attn:])


def _q_proj(x2d, B, S, g, w, qg, cosf, sinf):
    T, D = x2d.shape
    tm = ROW_TILE
    ns = S // tm
    d_attn = N_HEADS * HEAD_DIM
    return pl.pallas_call(
        _q_proj_kernel,
        grid=(B, ns),
        in_specs=[pl.BlockSpec((tm, D), lambda b, s: (b * ns + s, 0)),
                  pl.BlockSpec((1, D), lambda b, s: (0, 0)),
                  pl.BlockSpec((D, 2 * d_attn), lambda b, s: (0, 0)),
                  pl.BlockSpec((1, HEAD_DIM), lambda b, s: (0, 0)),
                  pl.BlockSpec((tm, HEAD_DIM), lambda b, s: (s, 0)),
                  pl.BlockSpec((tm, HEAD_DIM), lambda b, s: (s, 0))],
        out_specs=[pl.BlockSpec((1, N_HEADS, tm, HEAD_DIM), lambda b, s: (b, 0, s, 0)),
                   pl.BlockSpec((tm, d_attn), lambda b, s: (b * ns + s, 0))],
        out_shape=[jax.ShapeDtypeStruct((B, N_HEADS, S, HEAD_DIM), F32),
                   jax.ShapeDtypeStruct((T, d_attn), F32)],
        compiler_params=_cparams(("arbitrary", "arbitrary")),
        name="q_proj",
    )(x2d, g, w, qg, cosf, sinf)


def _moba_route_kernel(q_ref, km_ref, rt_ref, cnt_ref, run_ref):
    step = pl.program_id(2)
    nb = km_ref.shape[2]

    @pl.when(step == 0)
    def _():
        run_ref[...] = jnp.zeros_like(run_ref)

    km = km_ref[0, 0].astype(BF16)
    blk = lax.broadcasted_iota(I32, (nb, MOBA_BLOCK), 0)
    qa = lax.broadcasted_iota(I32, (MOBA_BLOCK, MOBA_BLOCK), 0)
    qb = lax.broadcasted_iota(I32, (MOBA_BLOCK, MOBA_BLOCK), 1)
    earlier = jnp.where(qa < qb, 1.0, 0.0).astype(BF16)
    run = run_ref[...]
    for u in range(ROUTE_TILES):
        j = step * ROUTE_TILES + u
        g = _nt_dot(km, q_ref[0, 0, pl.ds(u * MOBA_BLOCK, MOBA_BLOCK), :].astype(BF16))
        g = jnp.where(blk < j, g, -jnp.inf)
        hits, ids = [], []
        for _ in range(MOBA_TOPK):
            mx = jnp.max(g, axis=0, keepdims=True)
            idx = jnp.min(jnp.where(g == mx, blk, nb), axis=0, keepdims=True)
            hit = blk == idx
            ok = mx > -jnp.inf
            hits.append(jnp.where(hit & ok, 1.0, 0.0))
            ids.append(jnp.where(ok, idx, -1))
            g = jnp.where(hit, -jnp.inf, g)
        sel = hits[0] + hits[1] + hits[2]

        rank = jnp.dot(sel.astype(BF16), earlier, preferred_element_type=F32)
        dest = run + rank
        run = run + jnp.sum(sel, axis=1, keepdims=True)
        for r in range(MOBA_TOPK):
            rt_ref[u, pl.ds(r, 1), :] = jnp.sum(hits[r] * dest, axis=0, keepdims=True).astype(I32)
            rt_ref[u, pl.ds(MOBA_TOPK + r, 1), :] = ids[r]
        rt_ref[u, pl.ds(2 * MOBA_TOPK, SUBLANES - 2 * MOBA_TOPK), :] = jnp.zeros(
            (SUBLANES - 2 * MOBA_TOPK, MOBA_BLOCK), I32)
    run_ref[...] = run
    cnt_ref[0] = run.astype(I32)


def _moba_route(q, kmean):
    B, H, S, Dh = q.shape
    nb = S // MOBA_BLOCK
    rows = ROUTE_TILES * MOBA_BLOCK
    nsteps = nb // ROUTE_TILES
    return pl.pallas_call(
        _moba_route_kernel,
        grid=(B, H, nsteps),
        in_specs=[pl.BlockSpec((1, 1, rows, Dh), lambda b, h, j: (b, h, j, 0)),
                  pl.BlockSpec((1, 1, nb, Dh), lambda b, h, j: (b, h, 0, 0))],
        out_specs=[pl.BlockSpec((ROUTE_TILES, SUBLANES, MOBA_BLOCK),
                                lambda b, h, j: ((b * H + h) * nsteps + j, 0, 0)),
                   pl.BlockSpec((1, nb, 1), lambda b, h, j: (b * H + h, 0, 0))],
        out_shape=[jax.ShapeDtypeStruct((B * H * nb, SUBLANES, MOBA_BLOCK), I32),
                   jax.ShapeDtypeStruct((B * H, nb, 1), I32)],
        scratch_shapes=[pltpu.VMEM((nb, 1), F32)],
        compiler_params=_cparams(("arbitrary", "arbitrary", "arbitrary")),
        name="moba_route",
    )(q, kmean)


def _moba_pos_kernel(base_ref, rt_ref, pos_ref, *, slab_rows, null_row0):
    g = pl.program_id(0)
    nb = rt_ref.shape[0]
    lane = lax.broadcasted_iota(I32, (nb, IDX_CHUNK), 1)
    for r in range(MOBA_TOPK):
        for c in range(MOBA_BLOCK // IDX_CHUNK):
            loc = rt_ref[:, r, pl.ds(c * IDX_CHUNK, IDX_CHUNK)]
            blk = rt_ref[:, MOBA_TOPK + r, pl.ds(c * IDX_CHUNK, IDX_CHUNK)]
            base = jnp.zeros_like(loc)
            for n in range(nb):
                base = jnp.where(blk == n, base_ref[g * nb + n], base)
            row = jnp.where(blk >= 0, base + loc, null_row0 + c * IDX_CHUNK + lane)
            pos_ref[:, c, r, :] = row + g * slab_rows


def _moba_pos(rt, base, nb, slab_rows, null_row0):
    G = rt.shape[0] // nb
    cpb = MOBA_BLOCK // IDX_CHUNK
    pos = pl.pallas_call(
        functools.partial(_moba_pos_kernel, slab_rows=slab_rows, null_row0=null_row0),
        grid_spec=pltpu.PrefetchScalarGridSpec(
            num_scalar_prefetch=1,
            grid=(G,),
            in_specs=[pl.BlockSpec((nb, SUBLANES, MOBA_BLOCK), lambda g, base: (g, 0, 0))],
            out_specs=pl.BlockSpec((nb, cpb, MOBA_TOPK, IDX_CHUNK), lambda g, base: (g, 0, 0, 0)),
        ),
        out_shape=jax.ShapeDtypeStruct((G * nb, cpb, MOBA_TOPK, IDX_CHUNK), I32),
        compiler_params=_cparams(("arbitrary",)),
        name="moba_pos",
    )(base, rt)
    return pos.reshape(G * nb * cpb, MOBA_TOPK, IDX_CHUNK)


def _sc_mesh():
    return plsc.VectorSubcoreMesh(core_axis_name="c", subcore_axis_name="s")


def _sc_scatter_rows(src, pos, n_out_rows):
    N, D = src.shape
    n_chunks = N // IDX_CHUNK
    per_w = n_chunks // SC_WORKERS
    assert per_w * SC_WORKERS == n_chunks

    @functools.partial(
        pl.kernel, mesh=_sc_mesh(),
        out_type=jax.ShapeDtypeStruct((n_out_rows, D), src.dtype),
        scratch_types=[pltpu.VMEM((MOBA_TOPK, IDX_CHUNK), I32),
                       pltpu.VMEM((IDX_CHUNK, D), src.dtype),
                       pltpu.SemaphoreType.DMA],
        name="sc_scatter_rows",
    )
    def k(src_hbm, pos_hbm, out_hbm, idx_v, rows_v, sem):
        wid = lax.axis_index("s") * SC_CORES + lax.axis_index("c")

        @pl.loop(0, per_w)
        def _(i):
            c = wid * per_w + i
            pltpu.sync_copy(pos_hbm.at[c], idx_v)
            pltpu.sync_copy(src_hbm.at[pl.ds(pl.multiple_of(c * IDX_CHUNK, IDX_CHUNK), IDX_CHUNK)], rows_v)
            copies = [pltpu.async_copy(rows_v, out_hbm.at[idx_v.at[r]], sem) for r in range(MOBA_TOPK)]
            for cp in copies:
                cp.wait()

    return k(src, pos)


def _sc_gather_rows(table, pos):
    R, D = table.shape
    n_chunks = pos.shape[0]
    N = n_chunks * IDX_CHUNK
    per_w = n_chunks // SC_WORKERS
    assert per_w * SC_WORKERS == n_chunks

    @functools.partial(
        pl.kernel, mesh=_sc_mesh(),
        out_type=jax.ShapeDtypeStruct((MOBA_TOPK, N, D), table.dtype),
        scratch_types=[pltpu.VMEM((MOBA_TOPK, IDX_CHUNK), I32),
                       pltpu.VMEM((IDX_CHUNK, D), table.dtype),
                       pltpu.SemaphoreType.DMA],
        name="sc_gather_rows",
    )
    def k(table_hbm, pos_hbm, out_hbm, idx_v, rows_v, sem):
        wid = lax.axis_index("s") * SC_CORES + lax.axis_index("c")

        @pl.loop(0, per_w)
        def _(i):
            c = wid * per_w + i
            row0 = pl.multiple_of(c * IDX_CHUNK, IDX_CHUNK)
            pltpu.sync_copy(pos_hbm.at[c], idx_v)
            for r in range(MOBA_TOPK):
                pltpu.async_copy(table_hbm.at[idx_v.at[r]], rows_v, sem).wait()
                pltpu.sync_copy(rows_v, out_hbm.at[r, pl.ds(row0, IDX_CHUNK)])

    return k(table, pos)


def _bits(x):
    return lax.bitcast_convert_type(x, I32)


def _pack_partial(part, m, live):
    lo = _bits(part[:, :HEAD_DIM].astype(BF16).astype(F32))
    hi = _bits(part[:, HEAD_DIM:].astype(BF16).astype(F32))
    word = (hi & HI16_MASK) | lax.shift_right_logical(lo, 16)
    lane = lax.broadcasted_iota(I32, word.shape, 1)
    word = jnp.where(lane == L_LANE, _bits(part[:, :HEAD_DIM]), word)
    word = jnp.where(lane == M_LANE, _bits(m), word)
    return jnp.where(live, word, jnp.where(lane == M_LANE, NEG_INF_BITS, 0))


def _unpack_partial(word):
    lo = lax.bitcast_convert_type(lax.shift_left(word, 16), F32)
    hi = lax.bitcast_convert_type(word & HI16_MASK, F32)
    wf = lax.bitcast_convert_type(word, F32)
    return lo, hi, wf[:, L_LANE:L_LANE + 1], wf[:, M_LANE:M_LANE + 1]


def _moba_group_kernel(tblk_ref, tvalid_ref, nt_ref, qg_ref, k_ref, v_ref, o_ref, *, max_tiles):
    g, step = pl.program_id(0), pl.program_id(1)
    t0 = g * max_tiles + step * TILES_PER_STEP
    scale = HEAD_DIM ** -0.5

    @pl.when((tvalid_ref[t0] > 0) | (step == pl.num_programs(1) - 1))
    def _():
        row = lax.broadcasted_iota(I32, (GROUP_TILE, 1), 0)
        for u in range(TILES_PER_STEP):
            rows = pl.ds(u * GROUP_TILE, GROUP_TILE)
            live = row < tvalid_ref[t0 + u]
            q = jnp.where(live, qg_ref[rows, :], 0.0).astype(BF16)
            n0 = pl.multiple_of(tblk_ref[t0 + u] * MOBA_BLOCK, MOBA_BLOCK)
            s = _nt_dot(q, k_ref[0, 0, pl.ds(n0, MOBA_BLOCK), :]) * scale
            m = jnp.max(s, axis=1, keepdims=True).astype(BF16).astype(F32)
            p = jnp.exp(s - m).astype(BF16)
            part = jnp.dot(p, v_ref[0, 0, pl.ds(n0, MOBA_BLOCK), :], preferred_element_type=F32)
            o_ref[rows, :] = _pack_partial(part, m, live)


def _moba_group(qg, k, vaug, tblk, tvalid, ntiles, max_tiles):
    B, H, S, Dh = k.shape
    G = B * H
    n_steps = max_tiles // TILES_PER_STEP
    step_rows = TILES_PER_STEP * GROUP_TILE

    def tile_idx(g, s, tblk, tvalid, nt):
        return (g * n_steps + jnp.where(s * TILES_PER_STEP < nt[g], s, n_steps - 1), 0)

    return pl.pallas_call(
        functools.partial(_moba_group_kernel, max_tiles=max_tiles),
        grid_spec=pltpu.PrefetchScalarGridSpec(
            num_scalar_prefetch=3,
            grid=(G, n_steps),
            in_specs=[pl.BlockSpec((step_rows, Dh), tile_idx),
                      pl.BlockSpec((1, 1, S, Dh), lambda g, s, *_: (g // H, g % H, 0, 0)),
                      pl.BlockSpec((1, 1, S, VA_W), lambda g, s, *_: (g // H, g % H, 0, 0))],
            out_specs=pl.BlockSpec((step_rows, LANES), tile_idx),
        ),
        out_shape=jax.ShapeDtypeStruct((G * max_tiles * GROUP_TILE, LANES), I32),
        compiler_params=_cparams(("arbitrary", "arbitrary")),
        name="moba_group",
    )(tblk, tvalid, ntiles, qg, k, vaug)


def _moba_merge_kernel(q_ref, k_ref, v_ref, *refs):
    og_refs, o_ref = refs[:-1], refs[-1]
    scale = HEAD_DIM ** -0.5
    qi = lax.broadcasted_iota(I32, (MOBA_BLOCK, MOBA_BLOCK), 0)
    ki = lax.broadcasted_iota(I32, (MOBA_BLOCK, MOBA_BLOCK), 1)
    lane = lax.broadcasted_iota(I32, (MOBA_BLOCK, HEAD_DIM), 1)
    for u, og_ref in enumerate(og_refs):
        q = q_ref[0, u].astype(BF16)
        s = _nt_dot(q, k_ref[0, u]) * scale
        s = jnp.where(ki <= qi, s, NEG_INF)
        m_own = jnp.max(s, axis=1, keepdims=True)
        p = jnp.exp(s - m_own).astype(BF16)
        own = jnp.dot(p, v_ref[0, u], preferred_element_type=F32)
        parts = [_unpack_partial(og_ref[r]) for r in range(MOBA_TOPK)]
        m_all = m_own
        for _, _, _, m in parts:
            m_all = jnp.maximum(m_all, m)
        w_own = jnp.exp(m_own - m_all)
        tot_lo = w_own * own[:, :HEAD_DIM]
        tot_hi = w_own * own[:, HEAD_DIM:]
        l_tot = w_own * own[:, L_LANE:L_LANE + 1]
        for lo, hi, l, m in parts:
            w = jnp.exp(m - m_all)
            tot_lo = tot_lo + w * lo
            tot_hi = tot_hi + w * hi
            l_tot = l_tot + w * l
        acc = jnp.where(lane < HALF, tot_lo, pltpu.roll(tot_hi, HALF, 1))
        o_ref[:, u * HEAD_DIM:(u + 1) * HEAD_DIM] = acc / l_tot


def _moba_merge(q, k, vaug, og):
    B, H, S, Dh = q.shape
    nb = S // MOBA_BLOCK
    hps = HEADS_PER_STEP
    og_spec = lambda u: pl.BlockSpec((MOBA_TOPK, MOBA_BLOCK, LANES),
                                     lambda b, hp, j: (0, (b * H + hp * hps + u) * nb + j, 0))
    return pl.pallas_call(
        _moba_merge_kernel,
        grid=(B, H // hps, nb),
        in_specs=[pl.BlockSpec((1, hps, MOBA_BLOCK, Dh), lambda b, hp, j: (b, hp, j, 0)),
                  pl.BlockSpec((1, hps, MOBA_BLOCK, Dh), lambda b, hp, j: (b, hp, j, 0)),
                  pl.BlockSpec((1, hps, MOBA_BLOCK, VA_W), lambda b, hp, j: (b, hp, j, 0))]
                 + [og_spec(u) for u in range(hps)],
        out_specs=pl.BlockSpec((MOBA_BLOCK, hps * Dh), lambda b, hp, j: (b * nb + j, hp)),
        out_shape=jax.ShapeDtypeStruct((B * S, H * Dh), F32),
        compiler_params=_cparams(("arbitrary", "arbitrary", "arbitrary")),
        name="moba_merge",
    )(q, k, vaug, *([og] * hps))


def _group_layout(counts, max_tiles):
    G, nb = counts.shape
    tiles = (counts + GROUP_TILE - 1) // GROUP_TILE
    end = jnp.cumsum(tiles, axis=1)
    start = end - tiles
    t = jnp.arange(max_tiles, dtype=I32)
    tblk = jnp.sum(t[None, :, None] >= end[:, None, :], axis=2).astype(I32)
    tblk = jnp.minimum(tblk, nb - 1)
    c_t = jnp.take_along_axis(counts, tblk, axis=1)
    s_t = jnp.take_along_axis(start, tblk, axis=1)
    tvalid = jnp.clip(c_t - (t[None, :] - s_t) * GROUP_TILE, 0, GROUP_TILE)
    tvalid = jnp.where(t[None, :] < end[:, -1:], tvalid, 0).astype(I32)
    return start * GROUP_TILE, tblk, tvalid, end[:, -1].astype(I32)


def _moba_attention(q, k, vaug, kmean):
    B, H, S, Dh = q.shape
    nb = S // MOBA_BLOCK
    G = B * H
    max_tiles = (MOBA_TOPK * S) // GROUP_TILE + nb + 1
    max_tiles = -(-max_tiles // TILES_PER_STEP) * TILES_PER_STEP
    slab_rows = max_tiles * GROUP_TILE
    null_row0 = (max_tiles - 1) * GROUP_TILE

    rt, counts = _moba_route(q, kmean)
    base, tblk, tvalid, ntiles = _group_layout(counts.reshape(G, nb), max_tiles)
    pos = _moba_pos(rt, base.astype(I32).reshape(-1), nb, slab_rows, null_row0)
    qg = _sc_scatter_rows(q.reshape(G * S, Dh), pos, G * slab_rows)
    parts = _moba_group(qg, k, vaug, tblk.reshape(-1), tvalid.reshape(-1), ntiles, max_tiles)
    og = _sc_gather_rows(parts, pos)
    return _moba_merge(q, k, vaug, og)


def _gated_proj_residual_kernel(o_ref, sg_ref, w_ref, x_ref, out_ref):
    y = (o_ref[...] * sg_ref[...]).astype(BF16)
    out_ref[...] = x_ref[...] + jnp.dot(y, w_ref[...], preferred_element_type=F32)


def _gated_proj_residual(o, sg, w, x2d):
    T, D = x2d.shape
    K = o.shape[1]
    return pl.pallas_call(
        _gated_proj_residual_kernel,
        grid=(T // ROW_TILE,),
        in_specs=[pl.BlockSpec((ROW_TILE, K), lambda i: (i, 0)),
                  pl.BlockSpec((ROW_TILE, K), lambda i: (i, 0)),
                  pl.BlockSpec((K, D), lambda i: (0, 0)),
                  pl.BlockSpec((ROW_TILE, D), lambda i: (i, 0))],
        out_specs=pl.BlockSpec((ROW_TILE, D), lambda i: (i, 0)),
        out_shape=jax.ShapeDtypeStruct((T, D), F32),
        compiler_params=_cparams(("arbitrary",)),
        name="moba_out",
    )(o, sg, w, x2d)


def _rope_tables(S):
    half = HEAD_DIM // 2
    inv = ROPE_THETA ** (-jnp.arange(half, dtype=F32) / half)
    ang = jnp.arange(S, dtype=I32).astype(F32)[:, None] * inv[None, :]
    cos, sin = jnp.cos(ang), jnp.sin(ang)
    return jnp.concatenate([cos, cos], axis=-1), jnp.concatenate([-sin, sin], axis=-1)


def kernel(x, a_norm, a_in_w, a_conv_w, a_conv_b, a_r_w, a_r_b, a_i_w, a_i_b, a_lambda, a_out_w,
           kv_norm, kv_w, k_norm, b_norm, b_in_w, q_norm, b_out_w):
    B, S, D = x.shape
    assert S % ROW_TILE == 0 and ROW_TILE % MOBA_BLOCK == 0
    assert D == N_RG_BLOCKS * RG_BW == N_HEADS * HEAD_DIM
    assert (B * N_HEADS * S) % (IDX_CHUNK * SC_WORKERS) == 0
    xs = x.reshape(B * S, D)
    row = lambda v: v.reshape(1, -1)

    for l in range(a_in_w.shape[0]):
        u = _hawk_in(xs, row(a_norm[l]), a_in_w[l].astype(BF16))
        wri = jnp.concatenate([a_r_w[l], a_i_w[l]], axis=-1).astype(BF16)
        y = _hawk_lru(u, B, S, a_conv_w[l], row(a_conv_b[l]), wri, row(a_r_b[l]), row(a_i_b[l]),
                      row(a_lambda[l]))
        xs = _proj_residual(y, a_out_w[l].astype(BF16), xs, "hawk_out")

    cosf, sinf = _rope_tables(S)
    k, vaug, kmean = _kv_proj(xs, B, S, row(kv_norm), kv_w.astype(BF16), row(k_norm), cosf, sinf)
    kmean = jnp.transpose(kmean, (0, 2, 1, 3))

    for jl in range(b_in_w.shape[0]):
        q, sg = _q_proj(xs, B, S, row(b_norm[jl]), b_in_w[jl].astype(BF16), row(q_norm[jl]), cosf, sinf)
        o = _moba_attention(q, k, vaug, kmean)
        xs = _gated_proj_residual(o, sg, b_out_w[jl].astype(BF16), xs)
    return xs.reshape(B, S, D)
```

```python
import functools

import jax
import jax.numpy as jnp
import numpy as np
from jax import lax
from jax.experimental import pallas as pl
from jax.experimental.pallas import tpu as pltpu
from jax.experimental.pallas import tpu_sc as plsc

N_HEADS = 8
HEAD_DIM = 128
MOBA_BLOCK = 256
MOBA_TOPK = 3
CONV_WIDTH = 4
N_RG_BLOCKS = 8
RG_BW = 128
RG_C = 8.0
ROPE_THETA = 10000.0
EPS = 1e-6
NEG_INF = -1e30
LOG2_E = 1.4426950408889634

SUBLANES = 8
LANES = 128
ROW_TILE = 512
VMEM_LIMIT = 56 * 1024 * 1024

GROUP_TILE = MOBA_BLOCK
TILES_PER_STEP = 4
HEADS_PER_STEP = 4
ROUTE_TILES = 4
HALF = HEAD_DIM // 2
VA_W = 2 * HEAD_DIM
L_LANE = HALF
M_LANE = HALF + 1
HI16_MASK = -65536
BF16_HALF_ULP = 0x8000
NEG_INF_BITS = int(np.float32(NEG_INF).view(np.int32))

SC_CORES = 2
SC_SUBCORES = 16
SC_WORKERS = SC_CORES * SC_SUBCORES
IDX_CHUNK = 128

F32 = jnp.float32
BF16 = jnp.bfloat16
I32 = jnp.int32


def _cparams(sem):
    return pltpu.CompilerParams(dimension_semantics=sem, vmem_limit_bytes=VMEM_LIMIT)


def _rms_norm(x, g):
    ms = jnp.mean(x * x, axis=-1, keepdims=True)
    return (x * lax.rsqrt(ms + EPS)) * g


def _silu(x):
    hx = 0.5 * x
    return hx * jnp.tanh(hx) + hx


def _sigmoid(x):
    return 0.5 * jnp.tanh(0.5 * x) + 0.5


def _nt_dot(a, b):
    return lax.dot_general(a, b, (((1,), (1,)), ((), ())), preferred_element_type=F32)


def _head_norm_rope(z, g, cosf, sinf):
    outs = []
    for h in range(N_HEADS):
        zh = z[:, h * HEAD_DIM:(h + 1) * HEAD_DIM]
        zh = _rms_norm(zh, g)
        outs.append(zh * cosf + pltpu.roll(zh, HEAD_DIM // 2, 1) * sinf)
    return outs


def _hawk_kernel(x_ref, g_ref, inw_ref, cw_ref, cb_ref, wri_ref, rb_ref, ib_ref, lam_ref, outw_ref,
                 o_ref, gate_ref, xpad_ref, a_ref, b_ref, hc_ref):
    tm, C = gate_ref.shape
    s = pl.program_id(1)

    @pl.when(s == 0)
    def _():
        xpad_ref[pl.ds(0, SUBLANES), :] = jnp.zeros((SUBLANES, xpad_ref.shape[1]), F32)
        hc_ref[...] = jnp.zeros_like(hc_ref)

    h = _rms_norm(x_ref[...], g_ref[...]).astype(BF16)
    gate_ref[...] = jnp.dot(h, inw_ref[:, C:], preferred_element_type=F32)

    xpad_ref[pl.ds(SUBLANES, tm), :] = jnp.dot(h, inw_ref[:, :C], preferred_element_type=F32)
    cw = cw_ref[...]
    xc = xpad_ref[pl.ds(SUBLANES - (CONV_WIDTH - 1), tm), :] * cw[0:1, :]
    for k in range(1, CONV_WIDTH):
        xc = xc + xpad_ref[pl.ds(SUBLANES - (CONV_WIDTH - 1) + k, tm), :] * cw[k:k + 1, :]
    xc = xc + cb_ref[...]
    xpad_ref[pl.ds(0, SUBLANES), :] = xpad_ref[pl.ds(tm, SUBLANES), :]

    lam = lam_ref[...]
    sp = jnp.maximum(-lam, 0.0) + jnp.log1p(jnp.exp(-jnp.abs(lam)))
    log2a_per_r = (-RG_C * LOG2_E) * sp
    xc16 = xc.astype(BF16)
    for g in range(N_RG_BLOCKS):
        lo = g * RG_BW
        z = jnp.dot(xc16[:, lo:lo + RG_BW], wri_ref[g], preferred_element_type=F32)
        r = _sigmoid(z[:, :RG_BW] + rb_ref[:, lo:lo + RG_BW])
        i = _sigmoid(z[:, RG_BW:] + ib_ref[:, lo:lo + RG_BW])
        a = jnp.exp2(r * log2a_per_r[:, lo:lo + RG_BW])
        mult = jnp.sqrt(1.0 - a * a)
        a_ref[:, lo:lo + RG_BW] = a
        b_ref[:, lo:lo + RG_BW] = mult * (i * xc[:, lo:lo + RG_BW])

    row = lax.broadcasted_iota(I32, (SUBLANES, a_ref.shape[1]), 0)

    def scan_body(c, hc):
        r0 = pl.multiple_of(c * SUBLANES, SUBLANES)
        a = a_ref[pl.ds(r0, SUBLANES), :]
        b = b_ref[pl.ds(r0, SUBLANES), :]
        for k in (1, 2, 4):
            keep = row >= k
            b = jnp.where(keep, a * pltpu.roll(b, k, 0) + b, b)
            a = jnp.where(keep, a * pltpu.roll(a, k, 0), a)
        h = a * hc + b
        b_ref[pl.ds(r0, SUBLANES), :] = h
        return h[SUBLANES - 1:SUBLANES, :]

    hc = lax.fori_loop(0, tm // SUBLANES, scan_body, hc_ref[0:1, :])
    hc_ref[0:1, :] = hc

    y = (b_ref[...] * _silu(gate_ref[...])).astype(BF16)
    o_ref[...] = x_ref[...] + jnp.dot(y, outw_ref[...], preferred_element_type=F32)


def _hawk_layer(x2d, B, S, g, in_w, cw, cb, wri, rb, ib, lam, out_w):
    T, D = x2d.shape
    C = in_w.shape[1] // 2
    tm = ROW_TILE
    ns = S // tm
    vec = lambda n: pl.BlockSpec((1, n), lambda b, s: (0, 0))
    return pl.pallas_call(
        _hawk_kernel,
        grid=(B, ns),
        in_specs=[pl.BlockSpec((tm, D), lambda b, s: (b * ns + s, 0)),
                  vec(D),
                  pl.BlockSpec((D, 2 * C), lambda b, s: (0, 0)),
                  pl.BlockSpec((CONV_WIDTH, C), lambda b, s: (0, 0)),
                  vec(C),
                  pl.BlockSpec((N_RG_BLOCKS, RG_BW, 2 * RG_BW), lambda b, s: (0, 0, 0)),
                  vec(C), vec(C), vec(C),
                  pl.BlockSpec((C, D), lambda b, s: (0, 0))],
        out_specs=pl.BlockSpec((tm, D), lambda b, s: (b * ns + s, 0)),
        out_shape=jax.ShapeDtypeStruct((T, D), F32),
        scratch_shapes=[pltpu.VMEM((tm, C), F32),
                        pltpu.VMEM((tm + SUBLANES, C), F32),
                        pltpu.VMEM((tm, C), F32),
                        pltpu.VMEM((tm, C), F32),
                        pltpu.VMEM((SUBLANES, C), F32)],
        compiler_params=_cparams(("arbitrary", "arbitrary")),
        name="hawk_layer",
    )(x2d, g, in_w, cw, cb, wri, rb, ib, lam, out_w)


def _kv_proj_kernel(x_ref, g_ref, w_ref, kg_ref, cos_ref, sin_ref, k_ref, v_ref, km_ref):
    tm = x_ref.shape[0]
    d_attn = N_HEADS * HEAD_DIM
    h = _rms_norm(x_ref[...], g_ref[...]).astype(BF16)
    kv = jnp.dot(h, w_ref[...], preferred_element_type=F32)
    ks = _head_norm_rope(kv[:, :d_attn], kg_ref[...], cos_ref[...], sin_ref[...])
    lane = lax.broadcasted_iota(I32, (tm, HEAD_DIM), 1)
    for hd in range(N_HEADS):
        k_ref[0, hd] = ks[hd].astype(BF16)
        vh = kv[:, d_attn + hd * HEAD_DIM:d_attn + (hd + 1) * HEAD_DIM]
        lo = jnp.where(lane < HALF, vh, jnp.where(lane == L_LANE, 1.0, 0.0))
        hi = jnp.where(lane < HALF, pltpu.roll(vh, HALF, 1), 0.0)
        v_ref[0, hd] = jnp.concatenate([lo, hi], axis=1).astype(BF16)
        for r in range(tm // MOBA_BLOCK):
            blk = ks[hd][r * MOBA_BLOCK:(r + 1) * MOBA_BLOCK, :]
            km_ref[0, r, pl.ds(hd, 1), :] = jnp.mean(blk, axis=0, keepdims=True)


def _kv_proj(x2d, B, S, g, w, kg, cosf, sinf):
    T, D = x2d.shape
    tm = ROW_TILE
    ns = S // tm
    nb = S // MOBA_BLOCK
    rpt = tm // MOBA_BLOCK
    return pl.pallas_call(
        _kv_proj_kernel,
        grid=(B, ns),
        in_specs=[pl.BlockSpec((tm, D), lambda b, s: (b * ns + s, 0)),
                  pl.BlockSpec((1, D), lambda b, s: (0, 0)),
                  pl.BlockSpec((D, 2 * N_HEADS * HEAD_DIM), lambda b, s: (0, 0)),
                  pl.BlockSpec((1, HEAD_DIM), lambda b, s: (0, 0)),
                  pl.BlockSpec((tm, HEAD_DIM), lambda b, s: (s, 0)),
                  pl.BlockSpec((tm, HEAD_DIM), lambda b, s: (s, 0))],
        out_specs=[pl.BlockSpec((1, N_HEADS, tm, HEAD_DIM), lambda b, s: (b, 0, s, 0)),
                   pl.BlockSpec((1, N_HEADS, tm, VA_W), lambda b, s: (b, 0, s, 0)),
                   pl.BlockSpec((1, rpt, N_HEADS, HEAD_DIM), lambda b, s: (b, s, 0, 0))],
        out_shape=[jax.ShapeDtypeStruct((B, N_HEADS, S, HEAD_DIM), BF16),
                   jax.ShapeDtypeStruct((B, N_HEADS, S, VA_W), BF16),
                   jax.ShapeDtypeStruct((B, nb, N_HEADS, HEAD_DIM), F32)],
        compiler_params=_cparams(("arbitrary", "arbitrary")),
        name="kv_proj",
    )(x2d, g, w, kg, cosf, sinf)


def _q_proj_kernel(x_ref, g_ref, w_ref, qg_ref, cos_ref, sin_ref, q_ref, sg_ref):
    d_attn = N_HEADS * HEAD_DIM
    h = _rms_norm(x_ref[...], g_ref[...]).astype(BF16)
    u = jnp.dot(h, w_ref[...], preferred_element_type=F32)
    qs = _head_norm_rope(u[:, :d_attn], qg_ref[...], cos_ref[...], sin_ref[...])
    for hd in range(N_HEADS):
        q_ref[0, hd] = qs[hd]
    sg_ref[...] = _silu(u[:, d_attn:])


def _q_proj(x2d, B, S, g, w, qg, cosf, sinf):
    T, D = x2d.shape
    tm = ROW_TILE
    ns = S // tm
    d_attn = N_HEADS * HEAD_DIM
    return pl.pallas_call(
        _q_proj_kernel,
        grid=(B, ns),
        in_specs=[pl.BlockSpec((tm, D), lambda b, s: (b * ns + s, 0)),
                  pl.BlockSpec((1, D), lambda b, s: (0, 0)),
                  pl.BlockSpec((D, 2 * d_attn), lambda b, s: (0, 0)),
                  pl.BlockSpec((1, HEAD_DIM), lambda b, s: (0, 0)),
                  pl.BlockSpec((tm, HEAD_DIM), lambda b, s: (s, 0)),
                  pl.BlockSpec((tm, HEAD_DIM), lambda b, s: (s, 0))],
        out_specs=[pl.BlockSpec((1, N_HEADS, tm, HEAD_DIM), lambda b, s: (b, 0, s, 0)),
                   pl.BlockSpec((tm, d_attn), lambda b, s: (b * ns + s, 0))],
        out_shape=[jax.ShapeDtypeStruct((B, N_HEADS, S, HEAD_DIM), F32),
                   jax.ShapeDtypeStruct((T, d_attn), F32)],
        compiler_params=_cparams(("arbitrary", "arbitrary")),
        name="q_proj",
    )(x2d, g, w, qg, cosf, sinf)


def _moba_route_kernel(q_ref, km_ref, rt_ref, cnt_ref, run_ref):
    step = pl.program_id(2)
    nb = km_ref.shape[2]

    @pl.when(step == 0)
    def _():
        run_ref[...] = jnp.zeros_like(run_ref)

    km = km_ref[0, 0].astype(BF16)
    blk = lax.broadcasted_iota(I32, (nb, MOBA_BLOCK), 0)
    qa = lax.broadcasted_iota(I32, (MOBA_BLOCK, MOBA_BLOCK), 0)
    qb = lax.broadcasted_iota(I32, (MOBA_BLOCK, MOBA_BLOCK), 1)
    earlier = jnp.where(qa < qb, 1.0, 0.0).astype(BF16)
    run = run_ref[...]
    for u in range(ROUTE_TILES):
        j = step * ROUTE_TILES + u
        g = _nt_dot(km, q_ref[0, 0, pl.ds(u * MOBA_BLOCK, MOBA_BLOCK), :].astype(BF16))
        g = jnp.where(blk < j, g, -jnp.inf)
        hits, ids = [], []
        for _ in range(MOBA_TOPK):
            mx = jnp.max(g, axis=0, keepdims=True)
            idx = jnp.min(jnp.where(g == mx, blk, nb), axis=0, keepdims=True)
            hit = blk == idx
            ok = mx > -jnp.inf
            hits.append(jnp.where(hit & ok, 1.0, 0.0))
            ids.append(jnp.where(ok, idx, -1))
            g = jnp.where(hit, -jnp.inf, g)
        sel = hits[0] + hits[1] + hits[2]

        rank = jnp.dot(sel.astype(BF16), earlier, preferred_element_type=F32)
        dest = run + rank
        run = run + jnp.sum(sel, axis=1, keepdims=True)
        for r in range(MOBA_TOPK):
            rt_ref[u, pl.ds(r, 1), :] = jnp.sum(hits[r] * dest, axis=0, keepdims=True).astype(I32)
            rt_ref[u, pl.ds(MOBA_TOPK + r, 1), :] = ids[r]
        rt_ref[u, pl.ds(2 * MOBA_TOPK, SUBLANES - 2 * MOBA_TOPK), :] = jnp.zeros(
            (SUBLANES - 2 * MOBA_TOPK, MOBA_BLOCK), I32)
    run_ref[...] = run
    cnt_ref[0] = run.astype(I32)


def _moba_route(q, kmean):
    B, H, S, Dh = q.shape
    nb = S // MOBA_BLOCK
    rows = ROUTE_TILES * MOBA_BLOCK
    nsteps = nb // ROUTE_TILES
    return pl.pallas_call(
        _moba_route_kernel,
        grid=(B, H, nsteps),
        in_specs=[pl.BlockSpec((1, 1, rows, Dh), lambda b, h, j: (b, h, j, 0)),
                  pl.BlockSpec((1, 1, nb, Dh), lambda b, h, j: (b, h, 0, 0))],
        out_specs=[pl.BlockSpec((ROUTE_TILES, SUBLANES, MOBA_BLOCK),
                                lambda b, h, j: ((b * H + h) * nsteps + j, 0, 0)),
                   pl.BlockSpec((1, nb, 1), lambda b, h, j: (b * H + h, 0, 0))],
        out_shape=[jax.ShapeDtypeStruct((B * H * nb, SUBLANES, MOBA_BLOCK), I32),
                   jax.ShapeDtypeStruct((B * H, nb, 1), I32)],
        scratch_shapes=[pltpu.VMEM((nb, 1), F32)],
        compiler_params=_cparams(("arbitrary", "arbitrary", "arbitrary")),
        name="moba_route",
    )(q, kmean)


def _moba_pos_kernel(base_ref, rt_ref, pos_ref, *, slab_rows, null_row0):
    g = pl.program_id(0)
    nb = rt_ref.shape[0]
    lane = lax.broadcasted_iota(I32, (nb, IDX_CHUNK), 1)
    for r in range(MOBA_TOPK):
        for c in range(MOBA_BLOCK // IDX_CHUNK):
            loc = rt_ref[:, r, pl.ds(c * IDX_CHUNK, IDX_CHUNK)]
            blk = rt_ref[:, MOBA_TOPK + r, pl.ds(c * IDX_CHUNK, IDX_CHUNK)]
            base = jnp.zeros_like(loc)
            for n in range(nb):
                base = jnp.where(blk == n, base_ref[g * nb + n], base)
            row = jnp.where(blk >= 0, base + loc, null_row0 + c * IDX_CHUNK + lane)
            pos_ref[:, c, r, :] = row + g * slab_rows


def _moba_pos(rt, base, nb, slab_rows, null_row0):
    G = rt.shape[0] // nb
    cpb = MOBA_BLOCK // IDX_CHUNK
    pos = pl.pallas_call(
        functools.partial(_moba_pos_kernel, slab_rows=slab_rows, null_row0=null_row0),
        grid_spec=pltpu.PrefetchScalarGridSpec(
            num_scalar_prefetch=1,
            grid=(G,),
            in_specs=[pl.BlockSpec((nb, SUBLANES, MOBA_BLOCK), lambda g, base: (g, 0, 0))],
            out_specs=pl.BlockSpec((nb, cpb, MOBA_TOPK, IDX_CHUNK), lambda g, base: (g, 0, 0, 0)),
        ),
        out_shape=jax.ShapeDtypeStruct((G * nb, cpb, MOBA_TOPK, IDX_CHUNK), I32),
        compiler_params=_cparams(("arbitrary",)),
        name="moba_pos",
    )(base, rt)
    return pos.reshape(G * nb * cpb, MOBA_TOPK, IDX_CHUNK)


def _sc_mesh():
    return plsc.VectorSubcoreMesh(core_axis_name="c", subcore_axis_name="s")


def _sc_scatter_rows(src, pos, n_out_rows):
    N, D = src.shape
    n_chunks = N // IDX_CHUNK
    per_w = n_chunks // SC_WORKERS
    assert per_w * SC_WORKERS == n_chunks

    @functools.partial(
        pl.kernel, mesh=_sc_mesh(),
        out_type=jax.ShapeDtypeStruct((n_out_rows, D), src.dtype),
        scratch_types=[pltpu.VMEM((MOBA_TOPK, IDX_CHUNK), I32),
                       pltpu.VMEM((IDX_CHUNK, D), src.dtype),
                       pltpu.SemaphoreType.DMA],
        name="sc_scatter_rows",
    )
    def k(src_hbm, pos_hbm, out_hbm, idx_v, rows_v, sem):
        wid = lax.axis_index("s") * SC_CORES + lax.axis_index("c")

        @pl.loop(0, per_w)
        def _(i):
            c = wid * per_w + i
            pltpu.sync_copy(pos_hbm.at[c], idx_v)
            pltpu.sync_copy(src_hbm.at[pl.ds(pl.multiple_of(c * IDX_CHUNK, IDX_CHUNK), IDX_CHUNK)], rows_v)
            copies = [pltpu.async_copy(rows_v, out_hbm.at[idx_v.at[r]], sem) for r in range(MOBA_TOPK)]
            for cp in copies:
                cp.wait()

    return k(src, pos)


def _sc_gather_rows(table, pos):
    R, D = table.shape
    n_chunks = pos.shape[0]
    N = n_chunks * IDX_CHUNK
    per_w = n_chunks // SC_WORKERS
    assert per_w * SC_WORKERS == n_chunks

    @functools.partial(
        pl.kernel, mesh=_sc_mesh(),
        out_type=jax.ShapeDtypeStruct((MOBA_TOPK, N, D), table.dtype),
        scratch_types=[pltpu.VMEM((MOBA_TOPK, IDX_CHUNK), I32),
                       pltpu.VMEM((IDX_CHUNK, D), table.dtype),
                       pltpu.SemaphoreType.DMA],
        name="sc_gather_rows",
    )
    def k(table_hbm, pos_hbm, out_hbm, idx_v, rows_v, sem):
        wid = lax.axis_index("s") * SC_CORES + lax.axis_index("c")

        @pl.loop(0, per_w)
        def _(i):
            c = wid * per_w + i
            row0 = pl.multiple_of(c * IDX_CHUNK, IDX_CHUNK)
            pltpu.sync_copy(pos_hbm.at[c], idx_v)
            for r in range(MOBA_TOPK):
                pltpu.async_copy(table_hbm.at[idx_v.at[r]], rows_v, sem).wait()
                pltpu.sync_copy(rows_v, out_hbm.at[r, pl.ds(row0, IDX_CHUNK)])

    return k(table, pos)


def _bits(x):
    return lax.bitcast_convert_type(x, I32)


def _pack_partial(part, m, live):
    lo = _bits(part[:, :HEAD_DIM]) + BF16_HALF_ULP
    hi = _bits(part[:, HEAD_DIM:]) + BF16_HALF_ULP
    word = (hi & HI16_MASK) | lax.shift_right_logical(lo, 16)
    lane = lax.broadcasted_iota(I32, word.shape, 1)
    word = jnp.where(lane == L_LANE, _bits(part[:, :HEAD_DIM]), word)
    word = jnp.where(lane == M_LANE, _bits(m), word)
    return jnp.where(live, word, jnp.where(lane == M_LANE, NEG_INF_BITS, 0))


def _unpack_partial(word):
    lo = lax.bitcast_convert_type(lax.shift_left(word, 16), F32)
    hi = lax.bitcast_convert_type(word & HI16_MASK, F32)
    wf = lax.bitcast_convert_type(word, F32)
    return lo, hi, wf[:, L_LANE:L_LANE + 1], wf[:, M_LANE:M_LANE + 1]


def _moba_group_kernel(tblk_ref, tvalid_ref, nt_ref, qg_ref, k_ref, v_ref, o_ref, *, max_tiles):
    g, step = pl.program_id(0), pl.program_id(1)
    t0 = g * max_tiles + step * TILES_PER_STEP
    scale = HEAD_DIM ** -0.5

    @pl.when((tvalid_ref[t0] > 0) | (step == pl.num_programs(1) - 1))
    def _():
        row = lax.broadcasted_iota(I32, (GROUP_TILE, 1), 0)
        for u in range(TILES_PER_STEP):
            rows = pl.ds(u * GROUP_TILE, GROUP_TILE)
            live = row < tvalid_ref[t0 + u]
            q = jnp.where(live, qg_ref[rows, :], 0.0).astype(BF16)
            n0 = pl.multiple_of(tblk_ref[t0 + u] * MOBA_BLOCK, MOBA_BLOCK)
            s = _nt_dot(q, k_ref[0, 0, pl.ds(n0, MOBA_BLOCK), :])
            m = (jnp.max(s, axis=1, keepdims=True) * scale).astype(BF16).astype(F32)
            p = jnp.exp2(s * (scale * LOG2_E) - m * LOG2_E).astype(BF16)
            part = jnp.dot(p, v_ref[0, 0, pl.ds(n0, MOBA_BLOCK), :], preferred_element_type=F32)
            o_ref[rows, :] = _pack_partial(part, m, live)


def _moba_group(qg, k, vaug, tblk, tvalid, ntiles, max_tiles):
    B, H, S, Dh = k.shape
    G = B * H
    n_steps = max_tiles // TILES_PER_STEP
    step_rows = TILES_PER_STEP * GROUP_TILE

    def tile_idx(g, s, tblk, tvalid, nt):
        return (g * n_steps + jnp.where(s * TILES_PER_STEP < nt[g], s, n_steps - 1), 0)

    return pl.pallas_call(
        functools.partial(_moba_group_kernel, max_tiles=max_tiles),
        grid_spec=pltpu.PrefetchScalarGridSpec(
            num_scalar_prefetch=3,
            grid=(G, n_steps),
            in_specs=[pl.BlockSpec((step_rows, Dh), tile_idx),
                      pl.BlockSpec((1, 1, S, Dh), lambda g, s, *_: (g // H, g % H, 0, 0)),
                      pl.BlockSpec((1, 1, S, VA_W), lambda g, s, *_: (g // H, g % H, 0, 0))],
            out_specs=pl.BlockSpec((step_rows, LANES), tile_idx),
        ),
        out_shape=jax.ShapeDtypeStruct((G * max_tiles * GROUP_TILE, LANES), I32),
        compiler_params=_cparams(("arbitrary", "arbitrary")),
        name="moba_group",
    )(tblk, tvalid, ntiles, qg, k, vaug)


def _moba_merge_kernel(q_ref, k_ref, v_ref, *refs):
    og_refs, o_ref = refs[:-1], refs[-1]
    scale = HEAD_DIM ** -0.5
    qi = lax.broadcasted_iota(I32, (MOBA_BLOCK, MOBA_BLOCK), 0)
    ki = lax.broadcasted_iota(I32, (MOBA_BLOCK, MOBA_BLOCK), 1)
    lane = lax.broadcasted_iota(I32, (MOBA_BLOCK, HEAD_DIM), 1)
    for u, og_ref in enumerate(og_refs):
        q = q_ref[0, u].astype(BF16)
        s = _nt_dot(q, k_ref[0, u])
        s = jnp.where(ki <= qi, s, NEG_INF)
        m_raw = jnp.max(s, axis=1, keepdims=True)
        m_own = m_raw * scale
        p = jnp.exp2((s - m_raw) * (scale * LOG2_E)).astype(BF16)
        own = jnp.dot(p, v_ref[0, u], preferred_element_type=F32)
        parts = [_unpack_partial(og_ref[r]) for r in range(MOBA_TOPK)]
        m_all = m_own
        for _, _, _, m in parts:
            m_all = jnp.maximum(m_all, m)
        w_own = jnp.exp(m_own - m_all)
        tot_lo = w_own * own[:, :HEAD_DIM]
        tot_hi = w_own * own[:, HEAD_DIM:]
        l_tot = w_own * own[:, L_LANE:L_LANE + 1]
        for lo, hi, l, m in parts:
            w = jnp.exp(m - m_all)
            tot_lo = tot_lo + w * lo
            tot_hi = tot_hi + w * hi
            l_tot = l_tot + w * l
        acc = jnp.where(lane < HALF, tot_lo, pltpu.roll(tot_hi, HALF, 1))
        o_ref[:, u * HEAD_DIM:(u + 1) * HEAD_DIM] = acc / l_tot


def _moba_merge(q, k, vaug, og):
    B, H, S, Dh = q.shape
    nb = S // MOBA_BLOCK
    hps = HEADS_PER_STEP
    og_spec = lambda u: pl.BlockSpec((MOBA_TOPK, MOBA_BLOCK, LANES),
                                     lambda b, hp, j: (0, (b * H + hp * hps + u) * nb + j, 0))
    return pl.pallas_call(
        _moba_merge_kernel,
        grid=(B, H // hps, nb),
        in_specs=[pl.BlockSpec((1, hps, MOBA_BLOCK, Dh), lambda b, hp, j: (b, hp, j, 0)),
                  pl.BlockSpec((1, hps, MOBA_BLOCK, Dh), lambda b, hp, j: (b, hp, j, 0)),
                  pl.BlockSpec((1, hps, MOBA_BLOCK, VA_W), lambda b, hp, j: (b, hp, j, 0))]
                 + [og_spec(u) for u in range(hps)],
        out_specs=pl.BlockSpec((MOBA_BLOCK, hps * Dh), lambda b, hp, j: (b * nb + j, hp)),
        out_shape=jax.ShapeDtypeStruct((B * S, H * Dh), F32),
        compiler_params=_cparams(("arbitrary", "arbitrary", "arbitrary")),
        name="moba_merge",
    )(q, k, vaug, *([og] * hps))


def _group_layout(counts, max_tiles):
    G, nb = counts.shape
    tiles = (counts + GROUP_TILE - 1) // GROUP_TILE
    end = jnp.cumsum(tiles, axis=1)
    start = end - tiles
    t = jnp.arange(max_tiles, dtype=I32)
    tblk = jnp.sum(t[None, :, None] >= end[:, None, :], axis=2).astype(I32)
    tblk = jnp.minimum(tblk, nb - 1)
    c_t = jnp.take_along_axis(counts, tblk, axis=1)
    s_t = jnp.take_along_axis(start, tblk, axis=1)
    tvalid = jnp.clip(c_t - (t[None, :] - s_t) * GROUP_TILE, 0, GROUP_TILE)
    tvalid = jnp.where(t[None, :] < end[:, -1:], tvalid, 0).astype(I32)
    return start * GROUP_TILE, tblk, tvalid, end[:, -1].astype(I32)


def _moba_attention(q, k, vaug, kmean):
    B, H, S, Dh = q.shape
    nb = S // MOBA_BLOCK
    G = B * H
    max_tiles = (MOBA_TOPK * S) // GROUP_TILE + nb + 1
    max_tiles = -(-max_tiles // TILES_PER_STEP) * TILES_PER_STEP
    slab_rows = max_tiles * GROUP_TILE
    null_row0 = (max_tiles - 1) * GROUP_TILE

    rt, counts = _moba_route(q, kmean)
    base, tblk, tvalid, ntiles = _group_layout(counts.reshape(G, nb), max_tiles)
    pos = _moba_pos(rt, base.astype(I32).reshape(-1), nb, slab_rows, null_row0)
    qg = _sc_scatter_rows(q.reshape(G * S, Dh), pos, G * slab_rows)
    parts = _moba_group(qg, k, vaug, tblk.reshape(-1), tvalid.reshape(-1), ntiles, max_tiles)
    og = _sc_gather_rows(parts, pos)
    return _moba_merge(q, k, vaug, og)


def _gated_proj_residual_kernel(o_ref, sg_ref, w_ref, x_ref, out_ref):
    y = (o_ref[...] * sg_ref[...]).astype(BF16)
    out_ref[...] = x_ref[...] + jnp.dot(y, w_ref[...], preferred_element_type=F32)


def _gated_proj_residual(o, sg, w, x2d):
    T, D = x2d.shape
    K = o.shape[1]
    return pl.pallas_call(
        _gated_proj_residual_kernel,
        grid=(T // ROW_TILE,),
        in_specs=[pl.BlockSpec((ROW_TILE, K), lambda i: (i, 0)),
                  pl.BlockSpec((ROW_TILE, K), lambda i: (i, 0)),
                  pl.BlockSpec((K, D), lambda i: (0, 0)),
                  pl.BlockSpec((ROW_TILE, D), lambda i: (i, 0))],
        out_specs=pl.BlockSpec((ROW_TILE, D), lambda i: (i, 0)),
        out_shape=jax.ShapeDtypeStruct((T, D), F32),
        compiler_params=_cparams(("arbitrary",)),
        name="moba_out",
    )(o, sg, w, x2d)


def _rope_tables(S):
    half = HEAD_DIM // 2
    inv = ROPE_THETA ** (-jnp.arange(half, dtype=F32) / half)
    ang = jnp.arange(S, dtype=I32).astype(F32)[:, None] * inv[None, :]
    cos, sin = jnp.cos(ang), jnp.sin(ang)
    return jnp.concatenate([cos, cos], axis=-1), jnp.concatenate([-sin, sin], axis=-1)


def kernel(x, a_norm, a_in_w, a_conv_w, a_conv_b, a_r_w, a_r_b, a_i_w, a_i_b, a_lambda, a_out_w,
           kv_norm, kv_w, k_norm, b_norm, b_in_w, q_norm, b_out_w):
    B, S, D = x.shape
    assert S % ROW_TILE == 0 and ROW_TILE % MOBA_BLOCK == 0
    assert D == N_RG_BLOCKS * RG_BW == N_HEADS * HEAD_DIM
    assert (B * N_HEADS * S) % (IDX_CHUNK * SC_WORKERS) == 0
    xs = x.reshape(B * S, D)
    row = lambda v: v.reshape(1, -1)

    for l in range(a_in_w.shape[0]):
        wri = jnp.concatenate([a_r_w[l], a_i_w[l]], axis=-1).astype(BF16)
        xs = _hawk_layer(xs, B, S, row(a_norm[l]), a_in_w[l].astype(BF16), a_conv_w[l], row(a_conv_b[l]), wri,
                         row(a_r_b[l]), row(a_i_b[l]), row(a_lambda[l]), a_out_w[l].astype(BF16))

    cosf, sinf = _rope_tables(S)
    k, vaug, kmean = _kv_proj(xs, B, S, row(kv_norm), kv_w.astype(BF16), row(k_norm), cosf, sinf)
    kmean = jnp.transpose(kmean, (0, 2, 1, 3))

    for jl in range(b_in_w.shape[0]):
        q, sg = _q_proj(xs, B, S, row(b_norm[jl]), b_in_w[jl].astype(BF16), row(q_norm[jl]), cosf, sinf)
        o = _moba_attention(q, k, vaug, kmean)
        xs = _gated_proj_residual(o, sg, b_out_w[jl].astype(BF16), xs)
    return xs.reshape(B, S, D)
```

```python
import functools

import jax
import jax.numpy as jnp
import numpy as np
from jax import lax
from jax.experimental import pallas as pl
from jax.experimental.pallas import tpu as pltpu
from jax.experimental.pallas import tpu_sc as plsc

N_HEADS = 8
HEAD_DIM = 128
MOBA_BLOCK = 256
MOBA_TOPK = 3
CONV_WIDTH = 4
N_RG_BLOCKS = 8
RG_BW = 128
RG_C = 8.0
ROPE_THETA = 10000.0
EPS = 1e-6
NEG_INF = -1e30
LOG2_E = 1.4426950408889634

SUBLANES = 8
LANES = 128
ROW_TILE = 512
VMEM_LIMIT = 56 * 1024 * 1024

GROUP_TILE = MOBA_BLOCK
TILES_PER_STEP = 8
HEADS_PER_STEP = 8
ROUTE_TILES = 4
HALF = HEAD_DIM // 2
VA_W = 2 * HEAD_DIM
L_LANE = HALF
HI16_MASK = -65536
BF16_HALF_ULP = 0x8000
NEG_INF_BITS = int(np.float32(NEG_INF).view(np.int32))

SC_CORES = 2
SC_SUBCORES = 16
SC_WORKERS = SC_CORES * SC_SUBCORES
IDX_CHUNK = 128

F32 = jnp.float32
BF16 = jnp.bfloat16
I32 = jnp.int32


def _cparams(sem):
    return pltpu.CompilerParams(dimension_semantics=sem, vmem_limit_bytes=VMEM_LIMIT)


def _rms_norm(x, g):
    ms = jnp.mean(x * x, axis=-1, keepdims=True)
    return (x * lax.rsqrt(ms + EPS)) * g


def _silu(x):
    hx = 0.5 * x
    return hx * jnp.tanh(hx) + hx


def _sigmoid(x):
    return 0.5 * jnp.tanh(0.5 * x) + 0.5


def _nt_dot(a, b):
    return lax.dot_general(a, b, (((1,), (1,)), ((), ())), preferred_element_type=F32)


def _head_norm_rope(z, g, cosf, sinf):
    outs = []
    for h in range(N_HEADS):
        zh = z[:, h * HEAD_DIM:(h + 1) * HEAD_DIM]
        zh = _rms_norm(zh, g)
        outs.append(zh * cosf + pltpu.roll(zh, HEAD_DIM // 2, 1) * sinf)
    return outs


def _hawk_kernel(x_ref, g_ref, inw_ref, cw_ref, cb_ref, wri_ref, rb_ref, ib_ref, lam_ref, outw_ref,
                 o_ref, gate_ref, xpad_ref, a_ref, b_ref, hc_ref):
    tm, C = gate_ref.shape
    s = pl.program_id(1)

    @pl.when(s == 0)
    def _():
        xpad_ref[pl.ds(0, SUBLANES), :] = jnp.zeros((SUBLANES, xpad_ref.shape[1]), F32)
        hc_ref[...] = jnp.zeros_like(hc_ref)

    h = _rms_norm(x_ref[...], g_ref[...]).astype(BF16)
    gate_ref[...] = jnp.dot(h, inw_ref[:, C:], preferred_element_type=F32)

    xpad_ref[pl.ds(SUBLANES, tm), :] = jnp.dot(h, inw_ref[:, :C], preferred_element_type=F32)
    cw = cw_ref[...]
    xc = xpad_ref[pl.ds(SUBLANES - (CONV_WIDTH - 1), tm), :] * cw[0:1, :]
    for k in range(1, CONV_WIDTH):
        xc = xc + xpad_ref[pl.ds(SUBLANES - (CONV_WIDTH - 1) + k, tm), :] * cw[k:k + 1, :]
    xc = xc + cb_ref[...]
    xpad_ref[pl.ds(0, SUBLANES), :] = xpad_ref[pl.ds(tm, SUBLANES), :]

    lam = lam_ref[...]
    sp = jnp.maximum(-lam, 0.0) + jnp.log1p(jnp.exp(-jnp.abs(lam)))
    log2a_per_r = (-RG_C * LOG2_E) * sp
    xc16 = xc.astype(BF16)
    for g in range(N_RG_BLOCKS):
        lo = g * RG_BW
        z = jnp.dot(xc16[:, lo:lo + RG_BW], wri_ref[g], preferred_element_type=F32)
        r = _sigmoid(z[:, :RG_BW] + rb_ref[:, lo:lo + RG_BW])
        i = _sigmoid(z[:, RG_BW:] + ib_ref[:, lo:lo + RG_BW])
        a = jnp.exp2(r * log2a_per_r[:, lo:lo + RG_BW])
        mult = jnp.sqrt(1.0 - a * a)
        a_ref[:, lo:lo + RG_BW] = a
        b_ref[:, lo:lo + RG_BW] = mult * (i * xc[:, lo:lo + RG_BW])

    row = lax.broadcasted_iota(I32, (SUBLANES, a_ref.shape[1]), 0)

    def scan_body(c, hc):
        r0 = pl.multiple_of(c * SUBLANES, SUBLANES)
        a = a_ref[pl.ds(r0, SUBLANES), :]
        b = b_ref[pl.ds(r0, SUBLANES), :]
        for k in (1, 2, 4):
            keep = row >= k
            b = jnp.where(keep, a * pltpu.roll(b, k, 0) + b, b)
            a = jnp.where(keep, a * pltpu.roll(a, k, 0), a)
        h = a * hc + b
        b_ref[pl.ds(r0, SUBLANES), :] = h
        return h[SUBLANES - 1:SUBLANES, :]

    hc = lax.fori_loop(0, tm // SUBLANES, scan_body, hc_ref[0:1, :])
    hc_ref[0:1, :] = hc

    y = (b_ref[...] * _silu(gate_ref[...])).astype(BF16)
    o_ref[...] = x_ref[...] + jnp.dot(y, outw_ref[...], preferred_element_type=F32)


def _hawk_layer(x2d, B, S, g, in_w, cw, cb, wri, rb, ib, lam, out_w):
    T, D = x2d.shape
    C = in_w.shape[1] // 2
    tm = ROW_TILE
    ns = S // tm
    vec = lambda n: pl.BlockSpec((1, n), lambda b, s: (0, 0))
    return pl.pallas_call(
        _hawk_kernel,
        grid=(B, ns),
        in_specs=[pl.BlockSpec((tm, D), lambda b, s: (b * ns + s, 0)),
                  vec(D),
                  pl.BlockSpec((D, 2 * C), lambda b, s: (0, 0)),
                  pl.BlockSpec((CONV_WIDTH, C), lambda b, s: (0, 0)),
                  vec(C),
                  pl.BlockSpec((N_RG_BLOCKS, RG_BW, 2 * RG_BW), lambda b, s: (0, 0, 0)),
                  vec(C), vec(C), vec(C),
                  pl.BlockSpec((C, D), lambda b, s: (0, 0))],
        out_specs=pl.BlockSpec((tm, D), lambda b, s: (b * ns + s, 0)),
        out_shape=jax.ShapeDtypeStruct((T, D), F32),
        scratch_shapes=[pltpu.VMEM((tm, C), F32),
                        pltpu.VMEM((tm + SUBLANES, C), F32),
                        pltpu.VMEM((tm, C), F32),
                        pltpu.VMEM((tm, C), F32),
                        pltpu.VMEM((SUBLANES, C), F32)],
        compiler_params=_cparams(("arbitrary", "arbitrary")),
        name="hawk_layer",
    )(x2d, g, in_w, cw, cb, wri, rb, ib, lam, out_w)


def _kv_proj_kernel(x_ref, g_ref, w_ref, kg_ref, cos_ref, sin_ref, k_ref, v_ref, km_ref):
    tm = x_ref.shape[0]
    d_attn = N_HEADS * HEAD_DIM
    h = _rms_norm(x_ref[...], g_ref[...]).astype(BF16)
    kv = jnp.dot(h, w_ref[...], preferred_element_type=F32)
    ks = _head_norm_rope(kv[:, :d_attn], kg_ref[...], cos_ref[...], sin_ref[...])
    lane = lax.broadcasted_iota(I32, (tm, HEAD_DIM), 1)
    for hd in range(N_HEADS):
        k_ref[0, hd] = ks[hd].astype(BF16)
        vh = kv[:, d_attn + hd * HEAD_DIM:d_attn + (hd + 1) * HEAD_DIM]
        lo = jnp.where(lane < HALF, vh, jnp.where(lane == L_LANE, 1.0, 0.0))
        hi = jnp.where(lane < HALF, pltpu.roll(vh, HALF, 1), 0.0)
        v_ref[0, hd] = jnp.concatenate([lo, hi], axis=1).astype(BF16)
        for r in range(tm // MOBA_BLOCK):
            blk = ks[hd][r * MOBA_BLOCK:(r + 1) * MOBA_BLOCK, :]
            km_ref[0, r, pl.ds(hd, 1), :] = jnp.mean(blk, axis=0, keepdims=True)


def _kv_proj(x2d, B, S, g, w, kg, cosf, sinf):
    T, D = x2d.shape
    tm = ROW_TILE
    ns = S // tm
    nb = S // MOBA_BLOCK
    rpt = tm // MOBA_BLOCK
    return pl.pallas_call(
        _kv_proj_kernel,
        grid=(B, ns),
        in_specs=[pl.BlockSpec((tm, D), lambda b, s: (b * ns + s, 0)),
                  pl.BlockSpec((1, D), lambda b, s: (0, 0)),
                  pl.BlockSpec((D, 2 * N_HEADS * HEAD_DIM), lambda b, s: (0, 0)),
                  pl.BlockSpec((1, HEAD_DIM), lambda b, s: (0, 0)),
                  pl.BlockSpec((tm, HEAD_DIM), lambda b, s: (s, 0)),
                  pl.BlockSpec((tm, HEAD_DIM), lambda b, s: (s, 0))],
        out_specs=[pl.BlockSpec((1, N_HEADS, tm, HEAD_DIM), lambda b, s: (b, 0, s, 0)),
                   pl.BlockSpec((1, N_HEADS, tm, VA_W), lambda b, s: (b, 0, s, 0)),
                   pl.BlockSpec((1, rpt, N_HEADS, HEAD_DIM), lambda b, s: (b, s, 0, 0))],
        out_shape=[jax.ShapeDtypeStruct((B, N_HEADS, S, HEAD_DIM), BF16),
                   jax.ShapeDtypeStruct((B, N_HEADS, S, VA_W), BF16),
                   jax.ShapeDtypeStruct((B, nb, N_HEADS, HEAD_DIM), F32)],
        compiler_params=_cparams(("arbitrary", "arbitrary")),
        name="kv_proj",
    )(x2d, g, w, kg, cosf, sinf)


def _q_proj_kernel(x_ref, g_ref, w_ref, qg_ref, cos_ref, sin_ref, q_ref, sg_ref):
    d_attn = N_HEADS * HEAD_DIM
    h = _rms_norm(x_ref[...], g_ref[...]).astype(BF16)
    u = jnp.dot(h, w_ref[...], preferred_element_type=F32)
    qs = _head_norm_rope(u[:, :d_attn], qg_ref[...], cos_ref[...], sin_ref[...])
    for hd in range(N_HEADS):
        q_ref[0, hd] = qs[hd]
    sg_ref[...] = _silu(u[:, d_attn:])


def _q_proj(x2d, B, S, g, w, qg, cosf, sinf):
    T, D = x2d.shape
    tm = ROW_TILE
    ns = S // tm
    d_attn = N_HEADS * HEAD_DIM
    return pl.pallas_call(
        _q_proj_kernel,
        grid=(B, ns),
        in_specs=[pl.BlockSpec((tm, D), lambda b, s: (b * ns + s, 0)),
                  pl.BlockSpec((1, D), lambda b, s: (0, 0)),
                  pl.BlockSpec((D, 2 * d_attn), lambda b, s: (0, 0)),
                  pl.BlockSpec((1, HEAD_DIM), lambda b, s: (0, 0)),
                  pl.BlockSpec((tm, HEAD_DIM), lambda b, s: (s, 0)),
                  pl.BlockSpec((tm, HEAD_DIM), lambda b, s: (s, 0))],
        out_specs=[pl.BlockSpec((1, N_HEADS, tm, HEAD_DIM), lambda b, s: (b, 0, s, 0)),
                   pl.BlockSpec((tm, d_attn), lambda b, s: (b * ns + s, 0))],
        out_shape=[jax.ShapeDtypeStruct((B, N_HEADS, S, HEAD_DIM), F32),
                   jax.ShapeDtypeStruct((T, d_attn), F32)],
        compiler_params=_cparams(("arbitrary", "arbitrary")),
        name="q_proj",
    )(x2d, g, w, qg, cosf, sinf)


def _moba_route_kernel(q_ref, km_ref, rt_ref, cnt_ref, run_ref):
    step = pl.program_id(2)
    nb = km_ref.shape[2]

    @pl.when(step == 0)
    def _():
        run_ref[...] = jnp.zeros_like(run_ref)

    km = km_ref[0, 0].astype(BF16)
    blk = lax.broadcasted_iota(I32, (nb, MOBA_BLOCK), 0)
    qa = lax.broadcasted_iota(I32, (MOBA_BLOCK, MOBA_BLOCK), 0)
    qb = lax.broadcasted_iota(I32, (MOBA_BLOCK, MOBA_BLOCK), 1)
    earlier = jnp.where(qa < qb, 1.0, 0.0).astype(BF16)
    run = run_ref[...]
    for u in range(ROUTE_TILES):
        j = step * ROUTE_TILES + u
        g = _nt_dot(km, q_ref[0, 0, pl.ds(u * MOBA_BLOCK, MOBA_BLOCK), :].astype(BF16))
        g = jnp.where(blk < j, g, -jnp.inf)
        hits, ids = [], []
        for _ in range(MOBA_TOPK):
            mx = jnp.max(g, axis=0, keepdims=True)
            idx = jnp.min(jnp.where(g == mx, blk, nb), axis=0, keepdims=True)
            hit = blk == idx
            ok = mx > -jnp.inf
            hits.append(jnp.where(hit & ok, 1.0, 0.0))
            ids.append(jnp.where(ok, idx, -1))
            g = jnp.where(hit, -jnp.inf, g)
        sel = hits[0] + hits[1] + hits[2]

        rank = jnp.dot(sel.astype(BF16), earlier, preferred_element_type=F32)
        dest = run + rank
        run = run + jnp.sum(sel, axis=1, keepdims=True)
        for r in range(MOBA_TOPK):
            rt_ref[u, pl.ds(r, 1), :] = jnp.sum(hits[r] * dest, axis=0, keepdims=True).astype(I32)
            rt_ref[u, pl.ds(MOBA_TOPK + r, 1), :] = ids[r]
        rt_ref[u, pl.ds(2 * MOBA_TOPK, SUBLANES - 2 * MOBA_TOPK), :] = jnp.zeros(
            (SUBLANES - 2 * MOBA_TOPK, MOBA_BLOCK), I32)
    run_ref[...] = run
    cnt_ref[0] = run.astype(I32)


def _moba_route(q, kmean):
    B, H, S, Dh = q.shape
    nb = S // MOBA_BLOCK
    rows = ROUTE_TILES * MOBA_BLOCK
    nsteps = nb // ROUTE_TILES
    return pl.pallas_call(
        _moba_route_kernel,
        grid=(B, H, nsteps),
        in_specs=[pl.BlockSpec((1, 1, rows, Dh), lambda b, h, j: (b, h, j, 0)),
                  pl.BlockSpec((1, 1, nb, Dh), lambda b, h, j: (b, h, 0, 0))],
        out_specs=[pl.BlockSpec((ROUTE_TILES, SUBLANES, MOBA_BLOCK),
                                lambda b, h, j: ((b * H + h) * nsteps + j, 0, 0)),
                   pl.BlockSpec((1, nb, 1), lambda b, h, j: (b * H + h, 0, 0))],
        out_shape=[jax.ShapeDtypeStruct((B * H * nb, SUBLANES, MOBA_BLOCK), I32),
                   jax.ShapeDtypeStruct((B * H, nb, 1), I32)],
        scratch_shapes=[pltpu.VMEM((nb, 1), F32)],
        compiler_params=_cparams(("arbitrary", "arbitrary", "arbitrary")),
        name="moba_route",
    )(q, kmean)


def _moba_pos_kernel(base_ref, rt_ref, pos_ref, *, slab_rows, null_row0):
    g = pl.program_id(0)
    nb = rt_ref.shape[0]
    lane = lax.broadcasted_iota(I32, (nb, IDX_CHUNK), 1)
    for r in range(MOBA_TOPK):
        for c in range(MOBA_BLOCK // IDX_CHUNK):
            loc = rt_ref[:, r, pl.ds(c * IDX_CHUNK, IDX_CHUNK)]
            blk = rt_ref[:, MOBA_TOPK + r, pl.ds(c * IDX_CHUNK, IDX_CHUNK)]
            base = jnp.zeros_like(loc)
            for n in range(nb):
                base = jnp.where(blk == n, base_ref[g * nb + n], base)
            row = jnp.where(blk >= 0, base + loc, null_row0 + c * IDX_CHUNK + lane)
            pos_ref[:, c, r, :] = row + g * slab_rows


def _moba_pos(rt, base, nb, slab_rows, null_row0):
    G = rt.shape[0] // nb
    cpb = MOBA_BLOCK // IDX_CHUNK
    pos = pl.pallas_call(
        functools.partial(_moba_pos_kernel, slab_rows=slab_rows, null_row0=null_row0),
        grid_spec=pltpu.PrefetchScalarGridSpec(
            num_scalar_prefetch=1,
            grid=(G,),
            in_specs=[pl.BlockSpec((nb, SUBLANES, MOBA_BLOCK), lambda g, base: (g, 0, 0))],
            out_specs=pl.BlockSpec((nb, cpb, MOBA_TOPK, IDX_CHUNK), lambda g, base: (g, 0, 0, 0)),
        ),
        out_shape=jax.ShapeDtypeStruct((G * nb, cpb, MOBA_TOPK, IDX_CHUNK), I32),
        compiler_params=_cparams(("arbitrary",)),
        name="moba_pos",
    )(base, rt)
    return pos.reshape(G * nb * cpb, MOBA_TOPK, IDX_CHUNK)


def _sc_mesh():
    return plsc.VectorSubcoreMesh(core_axis_name="c", subcore_axis_name="s")


def _sc_scatter_rows(src, pos, n_out_rows):
    N, D = src.shape
    n_chunks = N // IDX_CHUNK
    per_w = n_chunks // SC_WORKERS
    assert per_w * SC_WORKERS == n_chunks

    @functools.partial(
        pl.kernel, mesh=_sc_mesh(),
        out_type=jax.ShapeDtypeStruct((n_out_rows, D), src.dtype),
        scratch_types=[pltpu.VMEM((MOBA_TOPK, IDX_CHUNK), I32),
                       pltpu.VMEM((IDX_CHUNK, D), src.dtype),
                       pltpu.SemaphoreType.DMA],
        name="sc_scatter_rows",
    )
    def k(src_hbm, pos_hbm, out_hbm, idx_v, rows_v, sem):
        wid = lax.axis_index("s") * SC_CORES + lax.axis_index("c")

        @pl.loop(0, per_w)
        def _(i):
            c = wid * per_w + i
            pltpu.sync_copy(pos_hbm.at[c], idx_v)
            pltpu.sync_copy(src_hbm.at[pl.ds(pl.multiple_of(c * IDX_CHUNK, IDX_CHUNK), IDX_CHUNK)], rows_v)
            copies = [pltpu.async_copy(rows_v, out_hbm.at[idx_v.at[r]], sem) for r in range(MOBA_TOPK)]
            for cp in copies:
                cp.wait()

    return k(src, pos)


def _sc_gather_rows(table, pos):
    R, D = table.shape
    n_chunks = pos.shape[0]
    N = n_chunks * IDX_CHUNK
    per_w = n_chunks // SC_WORKERS
    assert per_w * SC_WORKERS == n_chunks

    @functools.partial(
        pl.kernel, mesh=_sc_mesh(),
        out_type=jax.ShapeDtypeStruct((MOBA_TOPK, N, D), table.dtype),
        scratch_types=[pltpu.VMEM((MOBA_TOPK, IDX_CHUNK), I32),
                       pltpu.VMEM((IDX_CHUNK, D), table.dtype),
                       pltpu.SemaphoreType.DMA],
        name="sc_gather_rows",
    )
    def k(table_hbm, pos_hbm, out_hbm, idx_v, rows_v, sem):
        wid = lax.axis_index("s") * SC_CORES + lax.axis_index("c")

        @pl.loop(0, per_w)
        def _(i):
            c = wid * per_w + i
            row0 = pl.multiple_of(c * IDX_CHUNK, IDX_CHUNK)
            pltpu.sync_copy(pos_hbm.at[c], idx_v)
            for r in range(MOBA_TOPK):
                pltpu.async_copy(table_hbm.at[idx_v.at[r]], rows_v, sem).wait()
                pltpu.sync_copy(rows_v, out_hbm.at[r, pl.ds(row0, IDX_CHUNK)])

    return k(table, pos)


def _bits(x):
    return lax.bitcast_convert_type(x, I32)


def _pack_partial(part, m, live):
    lane = lax.broadcasted_iota(I32, (part.shape[0], HEAD_DIM), 1)
    lo = _bits(part[:, :HEAD_DIM]) + BF16_HALF_ULP
    hi = _bits(jnp.where(lane == L_LANE, m, part[:, HEAD_DIM:])) + BF16_HALF_ULP
    word = (hi & HI16_MASK) | lax.shift_right_logical(lo, 16)
    return jnp.where(live, word, jnp.where(lane == L_LANE, NEG_INF_BITS & HI16_MASK, 0))


def _unpack_partial(word):
    lo = lax.bitcast_convert_type(lax.shift_left(word, 16), F32)
    hi = lax.bitcast_convert_type(word & HI16_MASK, F32)
    return lo, hi


def _moba_group_kernel(tblk_ref, tvalid_ref, nt_ref, qg_ref, k_ref, v_ref, o_ref, *, max_tiles):
    g, step = pl.program_id(0), pl.program_id(1)
    t0 = g * max_tiles + step * TILES_PER_STEP
    scale = HEAD_DIM ** -0.5

    @pl.when((tvalid_ref[t0] > 0) | (step == pl.num_programs(1) - 1))
    def _():
        row = lax.broadcasted_iota(I32, (GROUP_TILE, 1), 0)
        for u in range(TILES_PER_STEP):
            rows = pl.ds(u * GROUP_TILE, GROUP_TILE)
            live = row < tvalid_ref[t0 + u]
            q = jnp.where(live, qg_ref[rows, :], 0.0).astype(BF16)
            n0 = pl.multiple_of(tblk_ref[t0 + u] * MOBA_BLOCK, MOBA_BLOCK)
            s = _nt_dot(q, k_ref[0, 0, pl.ds(n0, MOBA_BLOCK), :])
            m = (jnp.max(s, axis=1, keepdims=True) * scale).astype(BF16).astype(F32)
            p = jnp.exp2(s * (scale * LOG2_E) - m * LOG2_E).astype(BF16)
            part = jnp.dot(p, v_ref[0, 0, pl.ds(n0, MOBA_BLOCK), :], preferred_element_type=F32)
            o_ref[rows, :] = _pack_partial(part, m, live)


def _moba_group(qg, k, vaug, tblk, tvalid, ntiles, max_tiles):
    B, H, S, Dh = k.shape
    G = B * H
    n_steps = max_tiles // TILES_PER_STEP
    step_rows = TILES_PER_STEP * GROUP_TILE

    def tile_idx(g, s, tblk, tvalid, nt):
        return (g * n_steps + jnp.where(s * TILES_PER_STEP < nt[g], s, n_steps - 1), 0)

    return pl.pallas_call(
        functools.partial(_moba_group_kernel, max_tiles=max_tiles),
        grid_spec=pltpu.PrefetchScalarGridSpec(
            num_scalar_prefetch=3,
            grid=(G, n_steps),
            in_specs=[pl.BlockSpec((step_rows, Dh), tile_idx),
                      pl.BlockSpec((1, 1, S, Dh), lambda g, s, *_: (g // H, g % H, 0, 0)),
                      pl.BlockSpec((1, 1, S, VA_W), lambda g, s, *_: (g // H, g % H, 0, 0))],
            out_specs=pl.BlockSpec((step_rows, LANES), tile_idx),
        ),
        out_shape=jax.ShapeDtypeStruct((G * max_tiles * GROUP_TILE, LANES), I32),
        compiler_params=_cparams(("arbitrary", "arbitrary")),
        name="moba_group",
    )(tblk, tvalid, ntiles, qg, k, vaug)


def _moba_merge_kernel(q_ref, k_ref, v_ref, *refs):
    og_refs, o_ref = refs[:-1], refs[-1]
    scale = HEAD_DIM ** -0.5
    qi = lax.broadcasted_iota(I32, (MOBA_BLOCK, MOBA_BLOCK), 0)
    ki = lax.broadcasted_iota(I32, (MOBA_BLOCK, MOBA_BLOCK), 1)
    lane = lax.broadcasted_iota(I32, (MOBA_BLOCK, HEAD_DIM), 1)
    spread = jnp.where(lax.broadcasted_iota(I32, (HEAD_DIM, HEAD_DIM), 0) == L_LANE, 1.0, 0.0).astype(BF16)
    for u, og_ref in enumerate(og_refs):
        q = q_ref[0, u].astype(BF16)
        s = _nt_dot(q, k_ref[0, u])
        s = jnp.where(ki <= qi, s, NEG_INF)
        m_raw = jnp.max(s, axis=1, keepdims=True)
        p = jnp.exp2((s - m_raw) * (scale * LOG2_E)).astype(BF16)
        own = jnp.dot(p, v_ref[0, u], preferred_element_type=F32)
        parts = [_unpack_partial(og_ref[r]) for r in range(MOBA_TOPK)]
        ms = [jnp.dot(jnp.where(lane == L_LANE, hi, 0.0).astype(BF16), spread, preferred_element_type=F32)
              for _, hi in parts]
        m_own = jnp.broadcast_to(m_raw * scale, lane.shape)
        m_all = m_own
        for m in ms:
            m_all = jnp.maximum(m_all, m)
        w_own = jnp.exp2((m_own - m_all) * LOG2_E)
        tot_lo = w_own * own[:, :HEAD_DIM]
        tot_hi = w_own * own[:, HEAD_DIM:]
        for (lo, hi), m in zip(parts, ms):
            w = jnp.exp2((m - m_all) * LOG2_E)
            tot_lo = tot_lo + w * lo
            tot_hi = tot_hi + w * hi
        acc = jnp.where(lane < HALF, tot_lo, pltpu.roll(tot_hi, HALF, 1))
        o_ref[:, u * HEAD_DIM:(u + 1) * HEAD_DIM] = acc / tot_lo[:, L_LANE:L_LANE + 1]


def _moba_merge(q, k, vaug, og):
    B, H, S, Dh = q.shape
    nb = S // MOBA_BLOCK
    hps = HEADS_PER_STEP
    og_spec = lambda u: pl.BlockSpec((MOBA_TOPK, MOBA_BLOCK, LANES),
                                     lambda b, hp, j: (0, (b * H + hp * hps + u) * nb + j, 0))
    return pl.pallas_call(
        _moba_merge_kernel,
        grid=(B, H // hps, nb),
        in_specs=[pl.BlockSpec((1, hps, MOBA_BLOCK, Dh), lambda b, hp, j: (b, hp, j, 0)),
                  pl.BlockSpec((1, hps, MOBA_BLOCK, Dh), lambda b, hp, j: (b, hp, j, 0)),
                  pl.BlockSpec((1, hps, MOBA_BLOCK, VA_W), lambda b, hp, j: (b, hp, j, 0))]
                 + [og_spec(u) for u in range(hps)],
        out_specs=pl.BlockSpec((MOBA_BLOCK, hps * Dh), lambda b, hp, j: (b * nb + j, hp)),
        out_shape=jax.ShapeDtypeStruct((B * S, H * Dh), F32),
        compiler_params=_cparams(("arbitrary", "arbitrary", "arbitrary")),
        name="moba_merge",
    )(q, k, vaug, *([og] * hps))


def _group_layout(counts, max_tiles):
    G, nb = counts.shape
    tiles = (counts + GROUP_TILE - 1) // GROUP_TILE
    end = jnp.cumsum(tiles, axis=1)
    start = end - tiles
    t = jnp.arange(max_tiles, dtype=I32)
    tblk = jnp.sum(t[None, :, None] >= end[:, None, :], axis=2).astype(I32)
    tblk = jnp.minimum(tblk, nb - 1)
    c_t = jnp.take_along_axis(counts, tblk, axis=1)
    s_t = jnp.take_along_axis(start, tblk, axis=1)
    tvalid = jnp.clip(c_t - (t[None, :] - s_t) * GROUP_TILE, 0, GROUP_TILE)
    tvalid = jnp.where(t[None, :] < end[:, -1:], tvalid, 0).astype(I32)
    return start * GROUP_TILE, tblk, tvalid, end[:, -1].astype(I32)


def _moba_attention(q, k, vaug, kmean):
    B, H, S, Dh = q.shape
    nb = S // MOBA_BLOCK
    G = B * H
    max_tiles = (MOBA_TOPK * S) // GROUP_TILE + nb + 1
    max_tiles = -(-max_tiles // TILES_PER_STEP) * TILES_PER_STEP
    slab_rows = max_tiles * GROUP_TILE
    null_row0 = (max_tiles - 1) * GROUP_TILE

    rt, counts = _moba_route(q, kmean)
    base, tblk, tvalid, ntiles = _group_layout(counts.reshape(G, nb), max_tiles)
    pos = _moba_pos(rt, base.astype(I32).reshape(-1), nb, slab_rows, null_row0)
    qg = _sc_scatter_rows(q.reshape(G * S, Dh), pos, G * slab_rows)
    parts = _moba_group(qg, k, vaug, tblk.reshape(-1), tvalid.reshape(-1), ntiles, max_tiles)
    og = _sc_gather_rows(parts, pos)
    return _moba_merge(q, k, vaug, og)


def _gated_proj_residual_kernel(o_ref, sg_ref, w_ref, x_ref, out_ref):
    y = (o_ref[...] * sg_ref[...]).astype(BF16)
    out_ref[...] = x_ref[...] + jnp.dot(y, w_ref[...], preferred_element_type=F32)


def _gated_proj_residual(o, sg, w, x2d):
    T, D = x2d.shape
    K = o.shape[1]
    return pl.pallas_call(
        _gated_proj_residual_kernel,
        grid=(T // ROW_TILE,),
        in_specs=[pl.BlockSpec((ROW_TILE, K), lambda i: (i, 0)),
                  pl.BlockSpec((ROW_TILE, K), lambda i: (i, 0)),
                  pl.BlockSpec((K, D), lambda i: (0, 0)),
                  pl.BlockSpec((ROW_TILE, D), lambda i: (i, 0))],
        out_specs=pl.BlockSpec((ROW_TILE, D), lambda i: (i, 0)),
        out_shape=jax.ShapeDtypeStruct((T, D), F32),
        compiler_params=_cparams(("arbitrary",)),
        name="moba_out",
    )(o, sg, w, x2d)


def _rope_tables(S):
    half = HEAD_DIM // 2
    inv = ROPE_THETA ** (-jnp.arange(half, dtype=F32) / half)
    ang = jnp.arange(S, dtype=I32).astype(F32)[:, None] * inv[None, :]
    cos, sin = jnp.cos(ang), jnp.sin(ang)
    return jnp.concatenate([cos, cos], axis=-1), jnp.concatenate([-sin, sin], axis=-1)


def kernel(x, a_norm, a_in_w, a_conv_w, a_conv_b, a_r_w, a_r_b, a_i_w, a_i_b, a_lambda, a_out_w,
           kv_norm, kv_w, k_norm, b_norm, b_in_w, q_norm, b_out_w):
    B, S, D = x.shape
    assert S % ROW_TILE == 0 and ROW_TILE % MOBA_BLOCK == 0
    assert D == N_RG_BLOCKS * RG_BW == N_HEADS * HEAD_DIM
    assert (B * N_HEADS * S) % (IDX_CHUNK * SC_WORKERS) == 0
    xs = x.reshape(B * S, D)
    row = lambda v: v.reshape(1, -1)

    for l in range(a_in_w.shape[0]):
        wri = jnp.concatenate([a_r_w[l], a_i_w[l]], axis=-1).astype(BF16)
        xs = _hawk_layer(xs, B, S, row(a_norm[l]), a_in_w[l].astype(BF16), a_conv_w[l], row(a_conv_b[l]), wri,
                         row(a_r_b[l]), row(a_i_b[l]), row(a_lambda[l]), a_out_w[l].astype(BF16))

    cosf, sinf = _rope_tables(S)
    k, vaug, kmean = _kv_proj(xs, B, S, row(kv_norm), kv_w.astype(BF16), row(k_norm), cosf, sinf)
    kmean = jnp.transpose(kmean, (0, 2, 1, 3))

    for jl in range(b_in_w.shape[0]):
        q, sg = _q_proj(xs, B, S, row(b_norm[jl]), b_in_w[jl].astype(BF16), row(q_norm[jl]), cosf, sinf)
        o = _moba_attention(q, k, vaug, kmean)
        xs = _gated_proj_residual(o, sg, b_out_w[jl].astype(BF16), xs)
    return xs.reshape(B, S, D)
```

```python
import functools

import jax
import jax.numpy as jnp
import numpy as np
from jax import lax
from jax.experimental import pallas as pl
from jax.experimental.pallas import tpu as pltpu
from jax.experimental.pallas import tpu_sc as plsc

N_HEADS = 8
HEAD_DIM = 128
MOBA_BLOCK = 256
MOBA_TOPK = 3
CONV_WIDTH = 4
N_RG_BLOCKS = 8
RG_BW = 128
RG_C = 8.0
ROPE_THETA = 10000.0
EPS = 1e-6
NEG_INF = -1e30
LOG2_E = 1.4426950408889634

SUBLANES = 8
LANES = 128
ROW_TILE = 512
VMEM_LIMIT = 56 * 1024 * 1024

GROUP_TILE = MOBA_BLOCK
TILES_PER_STEP = 8
HEADS_PER_STEP = 8
ROUTE_TILES = 4
ROUTE_BLK_SHIFT = 16
ROUTE_BLK_UNIT = 1 << ROUTE_BLK_SHIFT
HALF = HEAD_DIM // 2
VA_W = 2 * HEAD_DIM
L_LANE = HALF
HI16_MASK = -65536
BF16_HALF_ULP = 0x8000
NEG_INF_BITS = int(np.float32(NEG_INF).view(np.int32))

SC_CORES = 2
SC_SUBCORES = 16
SC_WORKERS = SC_CORES * SC_SUBCORES
IDX_CHUNK = 128

F32 = jnp.float32
BF16 = jnp.bfloat16
I32 = jnp.int32


def _cparams(sem):
    return pltpu.CompilerParams(dimension_semantics=sem, vmem_limit_bytes=VMEM_LIMIT)


def _rms_norm(x, g):
    ms = jnp.mean(x * x, axis=-1, keepdims=True)
    return (x * lax.rsqrt(ms + EPS)) * g


def _silu(x):
    hx = 0.5 * x
    return hx * jnp.tanh(hx) + hx


def _sigmoid(x):
    return 0.5 * jnp.tanh(0.5 * x) + 0.5


def _nt_dot(a, b):
    return lax.dot_general(a, b, (((1,), (1,)), ((), ())), preferred_element_type=F32)


def _head_norm_rope(z, g, cosf, sinf):
    outs = []
    for h in range(N_HEADS):
        zh = z[:, h * HEAD_DIM:(h + 1) * HEAD_DIM]
        zh = _rms_norm(zh, g)
        outs.append(zh * cosf + pltpu.roll(zh, HEAD_DIM // 2, 1) * sinf)
    return outs


def _hawk_kernel(x_ref, g_ref, inw_ref, cw_ref, cb_ref, wri_ref, rb_ref, ib_ref, lam_ref, outw_ref,
                 o_ref, gate_ref, xpad_ref, a_ref, b_ref, hc_ref):
    tm, C = gate_ref.shape
    s = pl.program_id(1)

    @pl.when(s == 0)
    def _():
        xpad_ref[pl.ds(0, SUBLANES), :] = jnp.zeros((SUBLANES, xpad_ref.shape[1]), F32)
        hc_ref[...] = jnp.zeros_like(hc_ref)

    h = _rms_norm(x_ref[...], g_ref[...]).astype(BF16)
    gate_ref[...] = jnp.dot(h, inw_ref[:, C:], preferred_element_type=F32)

    xpad_ref[pl.ds(SUBLANES, tm), :] = jnp.dot(h, inw_ref[:, :C], preferred_element_type=F32)
    cw = cw_ref[...]
    xc = xpad_ref[pl.ds(SUBLANES - (CONV_WIDTH - 1), tm), :] * cw[0:1, :]
    for k in range(1, CONV_WIDTH):
        xc = xc + xpad_ref[pl.ds(SUBLANES - (CONV_WIDTH - 1) + k, tm), :] * cw[k:k + 1, :]
    xc = xc + cb_ref[...]
    xpad_ref[pl.ds(0, SUBLANES), :] = xpad_ref[pl.ds(tm, SUBLANES), :]

    lam = lam_ref[...]
    sp = jnp.maximum(-lam, 0.0) + jnp.log1p(jnp.exp(-jnp.abs(lam)))
    log2a_per_r = (-RG_C * LOG2_E) * sp
    xc16 = xc.astype(BF16)
    for g in range(N_RG_BLOCKS):
        lo = g * RG_BW
        z = jnp.dot(xc16[:, lo:lo + RG_BW], wri_ref[g], preferred_element_type=F32)
        r = _sigmoid(z[:, :RG_BW] + rb_ref[:, lo:lo + RG_BW])
        i = _sigmoid(z[:, RG_BW:] + ib_ref[:, lo:lo + RG_BW])
        a = jnp.exp2(r * log2a_per_r[:, lo:lo + RG_BW])
        mult = jnp.sqrt(1.0 - a * a)
        a_ref[:, lo:lo + RG_BW] = a
        b_ref[:, lo:lo + RG_BW] = mult * (i * xc[:, lo:lo + RG_BW])

    row = lax.broadcasted_iota(I32, (SUBLANES, a_ref.shape[1]), 0)

    def scan_body(c, hc):
        r0 = pl.multiple_of(c * SUBLANES, SUBLANES)
        a = a_ref[pl.ds(r0, SUBLANES), :]
        b = b_ref[pl.ds(r0, SUBLANES), :]
        for k in (1, 2, 4):
            keep = row >= k
            b = jnp.where(keep, a * pltpu.roll(b, k, 0) + b, b)
            a = jnp.where(keep, a * pltpu.roll(a, k, 0), a)
        h = a * hc + b
        b_ref[pl.ds(r0, SUBLANES), :] = h
        return h[SUBLANES - 1:SUBLANES, :]

    hc = lax.fori_loop(0, tm // SUBLANES, scan_body, hc_ref[0:1, :])
    hc_ref[0:1, :] = hc

    y = (b_ref[...] * _silu(gate_ref[...])).astype(BF16)
    o_ref[...] = x_ref[...] + jnp.dot(y, outw_ref[...], preferred_element_type=F32)


def _hawk_layer(x2d, B, S, g, in_w, cw, cb, wri, rb, ib, lam, out_w):
    T, D = x2d.shape
    C = in_w.shape[1] // 2
    tm = ROW_TILE
    ns = S // tm
    vec = lambda n: pl.BlockSpec((1, n), lambda b, s: (0, 0))
    return pl.pallas_call(
        _hawk_kernel,
        grid=(B, ns),
        in_specs=[pl.BlockSpec((tm, D), lambda b, s: (b * ns + s, 0)),
                  vec(D),
                  pl.BlockSpec((D, 2 * C), lambda b, s: (0, 0)),
                  pl.BlockSpec((CONV_WIDTH, C), lambda b, s: (0, 0)),
                  vec(C),
                  pl.BlockSpec((N_RG_BLOCKS, RG_BW, 2 * RG_BW), lambda b, s: (0, 0, 0)),
                  vec(C), vec(C), vec(C),
                  pl.BlockSpec((C, D), lambda b, s: (0, 0))],
        out_specs=pl.BlockSpec((tm, D), lambda b, s: (b * ns + s, 0)),
        out_shape=jax.ShapeDtypeStruct((T, D), F32),
        scratch_shapes=[pltpu.VMEM((tm, C), F32),
                        pltpu.VMEM((tm + SUBLANES, C), F32),
                        pltpu.VMEM((tm, C), F32),
                        pltpu.VMEM((tm, C), F32),
                        pltpu.VMEM((SUBLANES, C), F32)],
        compiler_params=_cparams(("arbitrary", "arbitrary")),
        name="hawk_layer",
    )(x2d, g, in_w, cw, cb, wri, rb, ib, lam, out_w)


def _kv_proj_kernel(x_ref, g_ref, w_ref, kg_ref, cos_ref, sin_ref, k_ref, v_ref, km_ref):
    tm = x_ref.shape[0]
    d_attn = N_HEADS * HEAD_DIM
    h = _rms_norm(x_ref[...], g_ref[...]).astype(BF16)
    kv = jnp.dot(h, w_ref[...], preferred_element_type=F32)
    ks = _head_norm_rope(kv[:, :d_attn], kg_ref[...], cos_ref[...], sin_ref[...])
    lane = lax.broadcasted_iota(I32, (tm, HEAD_DIM), 1)
    for hd in range(N_HEADS):
        k_ref[0, hd] = ks[hd].astype(BF16)
        vh = kv[:, d_attn + hd * HEAD_DIM:d_attn + (hd + 1) * HEAD_DIM]
        lo = jnp.where(lane < HALF, vh, jnp.where(lane == L_LANE, 1.0, 0.0))
        hi = jnp.where(lane < HALF, pltpu.roll(vh, HALF, 1), 0.0)
        v_ref[0, hd] = jnp.concatenate([lo, hi], axis=1).astype(BF16)
        for r in range(tm // MOBA_BLOCK):
            blk = ks[hd][r * MOBA_BLOCK:(r + 1) * MOBA_BLOCK, :]
            km_ref[0, r, pl.ds(hd, 1), :] = jnp.mean(blk, axis=0, keepdims=True)


def _kv_proj(x2d, B, S, g, w, kg, cosf, sinf):
    T, D = x2d.shape
    tm = ROW_TILE
    ns = S // tm
    nb = S // MOBA_BLOCK
    rpt = tm // MOBA_BLOCK
    return pl.pallas_call(
        _kv_proj_kernel,
        grid=(B, ns),
        in_specs=[pl.BlockSpec((tm, D), lambda b, s: (b * ns + s, 0)),
                  pl.BlockSpec((1, D), lambda b, s: (0, 0)),
                  pl.BlockSpec((D, 2 * N_HEADS * HEAD_DIM), lambda b, s: (0, 0)),
                  pl.BlockSpec((1, HEAD_DIM), lambda b, s: (0, 0)),
                  pl.BlockSpec((tm, HEAD_DIM), lambda b, s: (s, 0)),
                  pl.BlockSpec((tm, HEAD_DIM), lambda b, s: (s, 0))],
        out_specs=[pl.BlockSpec((1, N_HEADS, tm, HEAD_DIM), lambda b, s: (b, 0, s, 0)),
                   pl.BlockSpec((1, N_HEADS, tm, VA_W), lambda b, s: (b, 0, s, 0)),
                   pl.BlockSpec((1, rpt, N_HEADS, HEAD_DIM), lambda b, s: (b, s, 0, 0))],
        out_shape=[jax.ShapeDtypeStruct((B, N_HEADS, S, HEAD_DIM), BF16),
                   jax.ShapeDtypeStruct((B, N_HEADS, S, VA_W), BF16),
                   jax.ShapeDtypeStruct((B, nb, N_HEADS, HEAD_DIM), F32)],
        compiler_params=_cparams(("arbitrary", "arbitrary")),
        name="kv_proj",
    )(x2d, g, w, kg, cosf, sinf)


def _q_proj_kernel(x_ref, g_ref, w_ref, qg_ref, cos_ref, sin_ref, q_ref, sg_ref):
    d_attn = N_HEADS * HEAD_DIM
    h = _rms_norm(x_ref[...], g_ref[...]).astype(BF16)
    u = jnp.dot(h, w_ref[...], preferred_element_type=F32)
    qs = _head_norm_rope(u[:, :d_attn], qg_ref[...], cos_ref[...], sin_ref[...])
    for hd in range(N_HEADS):
        q_ref[0, hd] = qs[hd]
    sg_ref[...] = _silu(u[:, d_attn:])


def _q_proj(x2d, B, S, g, w, qg, cosf, sinf):
    T, D = x2d.shape
    tm = ROW_TILE
    ns = S // tm
    d_attn = N_HEADS * HEAD_DIM
    return pl.pallas_call(
        _q_proj_kernel,
        grid=(B, ns),
        in_specs=[pl.BlockSpec((tm, D), lambda b, s: (b * ns + s, 0)),
                  pl.BlockSpec((1, D), lambda b, s: (0, 0)),
                  pl.BlockSpec((D, 2 * d_attn), lambda b, s: (0, 0)),
                  pl.BlockSpec((1, HEAD_DIM), lambda b, s: (0, 0)),
                  pl.BlockSpec((tm, HEAD_DIM), lambda b, s: (s, 0)),
                  pl.BlockSpec((tm, HEAD_DIM), lambda b, s: (s, 0))],
        out_specs=[pl.BlockSpec((1, N_HEADS, tm, HEAD_DIM), lambda b, s: (b, 0, s, 0)),
                   pl.BlockSpec((tm, d_attn), lambda b, s: (b * ns + s, 0))],
        out_shape=[jax.ShapeDtypeStruct((B, N_HEADS, S, HEAD_DIM), F32),
                   jax.ShapeDtypeStruct((T, d_attn), F32)],
        compiler_params=_cparams(("arbitrary", "arbitrary")),
        name="q_proj",
    )(x2d, g, w, qg, cosf, sinf)


def _moba_route_kernel(q_ref, km_ref, rt_ref, cnt_ref, run_ref):
    step = pl.program_id(2)
    nb = km_ref.shape[2]

    @pl.when(step == 0)
    def _():
        run_ref[...] = jnp.zeros_like(run_ref)

    km = km_ref[0, 0].astype(BF16)
    blk = lax.broadcasted_iota(I32, (nb, MOBA_BLOCK), 0)
    qa = lax.broadcasted_iota(I32, (MOBA_BLOCK, MOBA_BLOCK), 0)
    qb = lax.broadcasted_iota(I32, (MOBA_BLOCK, MOBA_BLOCK), 1)
    earlier = jnp.where(qa < qb, 1.0, 0.0).astype(BF16)
    run = run_ref[...]
    for u in range(ROUTE_TILES):
        j = step * ROUTE_TILES + u
        g = _nt_dot(km, q_ref[0, 0, pl.ds(u * MOBA_BLOCK, MOBA_BLOCK), :].astype(BF16))
        g = jnp.where(blk < j, g, -jnp.inf)
        hits, ids = [], []
        for _ in range(MOBA_TOPK):
            mx = jnp.max(g, axis=0, keepdims=True)
            idx = jnp.min(jnp.where(g == mx, blk, nb), axis=0, keepdims=True)
            hit = blk == idx
            ok = mx > -jnp.inf
            hits.append(jnp.where(hit & ok, 1.0, 0.0))
            ids.append(jnp.where(ok, idx, -1))
            g = jnp.where(hit, -jnp.inf, g)
        sel = hits[0] + hits[1] + hits[2]

        rank = jnp.dot(sel.astype(BF16), earlier, preferred_element_type=F32)
        dest = run + rank
        run = run + jnp.sum(sel, axis=1, keepdims=True)
        for r in range(MOBA_TOPK):
            loc = jnp.sum(hits[r] * dest, axis=0, keepdims=True).astype(I32)
            rt_ref[u, pl.ds(r, 1), :] = jnp.where(ids[r] >= 0, ids[r] * ROUTE_BLK_UNIT + loc, -1)
        rt_ref[u, pl.ds(MOBA_TOPK, SUBLANES - MOBA_TOPK), :] = jnp.full(
            (SUBLANES - MOBA_TOPK, MOBA_BLOCK), -1, I32)
    run_ref[...] = run
    cnt_ref[0] = run.astype(I32)


def _moba_route(q, kmean):
    B, H, S, Dh = q.shape
    nb = S // MOBA_BLOCK
    rows = ROUTE_TILES * MOBA_BLOCK
    nsteps = nb // ROUTE_TILES
    return pl.pallas_call(
        _moba_route_kernel,
        grid=(B, H, nsteps),
        in_specs=[pl.BlockSpec((1, 1, rows, Dh), lambda b, h, j: (b, h, j, 0)),
                  pl.BlockSpec((1, 1, nb, Dh), lambda b, h, j: (b, h, 0, 0))],
        out_specs=[pl.BlockSpec((ROUTE_TILES, SUBLANES, MOBA_BLOCK),
                                lambda b, h, j: ((b * H + h) * nsteps + j, 0, 0)),
                   pl.BlockSpec((1, nb, 1), lambda b, h, j: (b * H + h, 0, 0))],
        out_shape=[jax.ShapeDtypeStruct((B * H * nb, SUBLANES, MOBA_BLOCK), I32),
                   jax.ShapeDtypeStruct((B * H, nb, 1), I32)],
        scratch_shapes=[pltpu.VMEM((nb, 1), F32)],
        compiler_params=_cparams(("arbitrary", "arbitrary", "arbitrary")),
        name="moba_route",
    )(q, kmean)


def _moba_layout_kernel(cnt_ref, rt_ref, pos_ref, tblk_ref, tvalid_ref, nt_ref, base_ref, *, max_tiles):
    g = pl.program_id(0)
    nb = rt_ref.shape[0]
    slab_rows = max_tiles * GROUP_TILE
    null_row0 = (max_tiles - 1) * GROUP_TILE

    def per_block(n, first_tile):
        count = cnt_ref[g * nb + n]
        tiles = (count + (GROUP_TILE - 1)) // GROUP_TILE
        base_ref[n] = first_tile * GROUP_TILE

        def per_tile(i, carry):
            tblk_ref[g * max_tiles + first_tile + i] = n
            tvalid_ref[g * max_tiles + first_tile + i] = jnp.minimum(count - i * GROUP_TILE, GROUP_TILE)
            return carry

        lax.fori_loop(0, tiles, per_tile, 0)
        return first_tile + tiles

    used = lax.fori_loop(0, nb, per_block, 0)
    nt_ref[g] = used

    def unused_tile(t, carry):
        tblk_ref[g * max_tiles + t] = nb - 1
        tvalid_ref[g * max_tiles + t] = 0
        return carry

    lax.fori_loop(used, max_tiles, unused_tile, 0)

    code = rt_ref[...].reshape(nb * SUBLANES, MOBA_BLOCK)
    blk = lax.shift_right_arithmetic(code, ROUTE_BLK_SHIFT)
    base = jnp.zeros_like(code)
    for n in range(nb):
        base = jnp.where(blk == n, base_ref[n], base)
    lane = lax.broadcasted_iota(I32, code.shape, 1)
    row = jnp.where(code >= 0, base + (code & (ROUTE_BLK_UNIT - 1)), null_row0 + lane)
    pos_ref[...] = (row + g * slab_rows).reshape(pos_ref.shape)


def _moba_layout(rt, counts, nb, max_tiles):
    G = rt.shape[0] // nb
    smem = pl.BlockSpec(memory_space=pltpu.SMEM)
    return pl.pallas_call(
        functools.partial(_moba_layout_kernel, max_tiles=max_tiles),
        grid_spec=pltpu.PrefetchScalarGridSpec(
            num_scalar_prefetch=1,
            grid=(G,),
            in_specs=[pl.BlockSpec((nb, SUBLANES, MOBA_BLOCK), lambda g, cnt: (g, 0, 0))],
            out_specs=[pl.BlockSpec((nb, SUBLANES, MOBA_BLOCK), lambda g, cnt: (g, 0, 0)), smem, smem, smem],
            scratch_shapes=[pltpu.SMEM((nb,), I32)],
        ),
        out_shape=[jax.ShapeDtypeStruct((G * nb, SUBLANES, MOBA_BLOCK), I32),
                   jax.ShapeDtypeStruct((G * max_tiles,), I32),
                   jax.ShapeDtypeStruct((G * max_tiles,), I32),
                   jax.ShapeDtypeStruct((G,), I32)],
        compiler_params=_cparams(("arbitrary",)),
        name="moba_layout",
    )(counts, rt)


def _sc_mesh():
    return plsc.VectorSubcoreMesh(core_axis_name="c", subcore_axis_name="s")


def _sc_scatter_rows(src, pos, n_out_rows):
    N, D = src.shape
    n_tiles = N // MOBA_BLOCK
    per_w = n_tiles // SC_WORKERS
    assert per_w * SC_WORKERS == n_tiles

    @functools.partial(
        pl.kernel, mesh=_sc_mesh(),
        out_type=jax.ShapeDtypeStruct((n_out_rows, D), src.dtype),
        scratch_types=[pltpu.VMEM((MOBA_TOPK, IDX_CHUNK), I32),
                       pltpu.VMEM((IDX_CHUNK, D), src.dtype),
                       pltpu.SemaphoreType.DMA],
        name="sc_scatter_rows",
    )
    def k(src_hbm, pos_hbm, out_hbm, idx_v, rows_v, sem):
        wid = lax.axis_index("s") * SC_CORES + lax.axis_index("c")

        @pl.loop(0, per_w)
        def _(i):
            t = wid * per_w + i
            for c in range(MOBA_BLOCK // IDX_CHUNK):
                row0 = pl.multiple_of(t * MOBA_BLOCK + c * IDX_CHUNK, IDX_CHUNK)
                pltpu.sync_copy(pos_hbm.at[t, pl.ds(0, MOBA_TOPK), pl.ds(c * IDX_CHUNK, IDX_CHUNK)], idx_v)
                pltpu.sync_copy(src_hbm.at[pl.ds(row0, IDX_CHUNK)], rows_v)
                copies = [pltpu.async_copy(rows_v, out_hbm.at[idx_v.at[r]], sem) for r in range(MOBA_TOPK)]
                for cp in copies:
                    cp.wait()

    return k(src, pos)


def _sc_gather_rows(table, pos):
    R, D = table.shape
    n_tiles = pos.shape[0]
    N = n_tiles * MOBA_BLOCK
    per_w = n_tiles // SC_WORKERS
    assert per_w * SC_WORKERS == n_tiles

    @functools.partial(
        pl.kernel, mesh=_sc_mesh(),
        out_type=jax.ShapeDtypeStruct((MOBA_TOPK, N, D), table.dtype),
        scratch_types=[pltpu.VMEM((MOBA_TOPK, IDX_CHUNK), I32),
                       pltpu.VMEM((IDX_CHUNK, D), table.dtype),
                       pltpu.SemaphoreType.DMA],
        name="sc_gather_rows",
    )
    def k(table_hbm, pos_hbm, out_hbm, idx_v, rows_v, sem):
        wid = lax.axis_index("s") * SC_CORES + lax.axis_index("c")

        @pl.loop(0, per_w)
        def _(i):
            t = wid * per_w + i
            for c in range(MOBA_BLOCK // IDX_CHUNK):
                row0 = pl.multiple_of(t * MOBA_BLOCK + c * IDX_CHUNK, IDX_CHUNK)
                pltpu.sync_copy(pos_hbm.at[t, pl.ds(0, MOBA_TOPK), pl.ds(c * IDX_CHUNK, IDX_CHUNK)], idx_v)
                for r in range(MOBA_TOPK):
                    pltpu.async_copy(table_hbm.at[idx_v.at[r]], rows_v, sem).wait()
                    pltpu.sync_copy(rows_v, out_hbm.at[r, pl.ds(row0, IDX_CHUNK)])

    return k(table, pos)


def _bits(x):
    return lax.bitcast_convert_type(x, I32)


def _pack_partial(part, m, live):
    lane = lax.broadcasted_iota(I32, (part.shape[0], HEAD_DIM), 1)
    lo = _bits(part[:, :HEAD_DIM]) + BF16_HALF_ULP
    hi = _bits(jnp.where(lane == L_LANE, m, part[:, HEAD_DIM:])) + BF16_HALF_ULP
    word = (hi & HI16_MASK) | lax.shift_right_logical(lo, 16)
    return jnp.where(live, word, jnp.where(lane == L_LANE, NEG_INF_BITS & HI16_MASK, 0))


def _unpack_partial(word):
    lo = lax.bitcast_convert_type(lax.shift_left(word, 16), F32)
    hi = lax.bitcast_convert_type(word & HI16_MASK, F32)
    return lo, hi


def _moba_group_kernel(tblk_ref, tvalid_ref, nt_ref, qg_ref, k_ref, v_ref, o_ref, *, max_tiles):
    g, step = pl.program_id(0), pl.program_id(1)
    t0 = g * max_tiles + step * TILES_PER_STEP
    scale = HEAD_DIM ** -0.5

    @pl.when((tvalid_ref[t0] > 0) | (step == pl.num_programs(1) - 1))
    def _():
        row = lax.broadcasted_iota(I32, (GROUP_TILE, 1), 0)
        for u in range(TILES_PER_STEP):
            rows = pl.ds(u * GROUP_TILE, GROUP_TILE)
            live = row < tvalid_ref[t0 + u]
            q = jnp.where(live, qg_ref[rows, :], 0.0).astype(BF16)
            n0 = pl.multiple_of(tblk_ref[t0 + u] * MOBA_BLOCK, MOBA_BLOCK)
            s = _nt_dot(q, k_ref[0, 0, pl.ds(n0, MOBA_BLOCK), :])
            m = (jnp.max(s, axis=1, keepdims=True) * scale).astype(BF16).astype(F32)
            p = jnp.exp2(s * (scale * LOG2_E) - m * LOG2_E).astype(BF16)
            part = jnp.dot(p, v_ref[0, 0, pl.ds(n0, MOBA_BLOCK), :], preferred_element_type=F32)
            o_ref[rows, :] = _pack_partial(part, m, live)


def _moba_group(qg, k, vaug, tblk, tvalid, ntiles, max_tiles):
    B, H, S, Dh = k.shape
    G = B * H
    n_steps = max_tiles // TILES_PER_STEP
    step_rows = TILES_PER_STEP * GROUP_TILE

    def tile_idx(g, s, tblk, tvalid, nt):
        return (g * n_steps + jnp.where(s * TILES_PER_STEP < nt[g], s, n_steps - 1), 0)

    return pl.pallas_call(
        functools.partial(_moba_group_kernel, max_tiles=max_tiles),
        grid_spec=pltpu.PrefetchScalarGridSpec(
            num_scalar_prefetch=3,
            grid=(G, n_steps),
            in_specs=[pl.BlockSpec((step_rows, Dh), tile_idx),
                      pl.BlockSpec((1, 1, S, Dh), lambda g, s, *_: (g // H, g % H, 0, 0)),
                      pl.BlockSpec((1, 1, S, VA_W), lambda g, s, *_: (g // H, g % H, 0, 0))],
            out_specs=pl.BlockSpec((step_rows, LANES), tile_idx),
        ),
        out_shape=jax.ShapeDtypeStruct((G * max_tiles * GROUP_TILE, LANES), I32),
        compiler_params=_cparams(("arbitrary", "arbitrary")),
        name="moba_group",
    )(tblk, tvalid, ntiles, qg, k, vaug)


def _moba_merge_kernel(q_ref, k_ref, v_ref, *refs):
    og_refs, o_ref = refs[:-1], refs[-1]
    scale = HEAD_DIM ** -0.5
    qi = lax.broadcasted_iota(I32, (MOBA_BLOCK, MOBA_BLOCK), 0)
    ki = lax.broadcasted_iota(I32, (MOBA_BLOCK, MOBA_BLOCK), 1)
    lane = lax.broadcasted_iota(I32, (MOBA_BLOCK, HEAD_DIM), 1)
    spread = jnp.where(lax.broadcasted_iota(I32, (HEAD_DIM, HEAD_DIM), 0) == L_LANE, 1.0, 0.0).astype(BF16)
    for u, og_ref in enumerate(og_refs):
        q = q_ref[0, u].astype(BF16)
        s = _nt_dot(q, k_ref[0, u])
        s = jnp.where(ki <= qi, s, NEG_INF)
        m_raw = jnp.max(s, axis=1, keepdims=True)
        p = jnp.exp2((s - m_raw) * (scale * LOG2_E)).astype(BF16)
        own = jnp.dot(p, v_ref[0, u], preferred_element_type=F32)
        parts = [_unpack_partial(og_ref[r]) for r in range(MOBA_TOPK)]
        ms = [jnp.dot(jnp.where(lane == L_LANE, hi, 0.0).astype(BF16), spread, preferred_element_type=F32)
              for _, hi in parts]
        m_own = jnp.broadcast_to(m_raw * scale, lane.shape)
        m_all = m_own
        for m in ms:
            m_all = jnp.maximum(m_all, m)
        w_own = jnp.exp2((m_own - m_all) * LOG2_E)
        tot_lo = w_own * own[:, :HEAD_DIM]
        tot_hi = w_own * own[:, HEAD_DIM:]
        for (lo, hi), m in zip(parts, ms):
            w = jnp.exp2((m - m_all) * LOG2_E)
            tot_lo = tot_lo + w * lo
            tot_hi = tot_hi + w * hi
        acc = jnp.where(lane < HALF, tot_lo, pltpu.roll(tot_hi, HALF, 1))
        o_ref[:, u * HEAD_DIM:(u + 1) * HEAD_DIM] = acc / tot_lo[:, L_LANE:L_LANE + 1]


def _moba_merge(q, k, vaug, og):
    B, H, S, Dh = q.shape
    nb = S // MOBA_BLOCK
    hps = HEADS_PER_STEP
    og_spec = lambda u: pl.BlockSpec((MOBA_TOPK, MOBA_BLOCK, LANES),
                                     lambda b, hp, j: (0, (b * H + hp * hps + u) * nb + j, 0))
    return pl.pallas_call(
        _moba_merge_kernel,
        grid=(B, H // hps, nb),
        in_specs=[pl.BlockSpec((1, hps, MOBA_BLOCK, Dh), lambda b, hp, j: (b, hp, j, 0)),
                  pl.BlockSpec((1, hps, MOBA_BLOCK, Dh), lambda b, hp, j: (b, hp, j, 0)),
                  pl.BlockSpec((1, hps, MOBA_BLOCK, VA_W), lambda b, hp, j: (b, hp, j, 0))]
                 + [og_spec(u) for u in range(hps)],
        out_specs=pl.BlockSpec((MOBA_BLOCK, hps * Dh), lambda b, hp, j: (b * nb + j, hp)),
        out_shape=jax.ShapeDtypeStruct((B * S, H * Dh), F32),
        compiler_params=_cparams(("arbitrary", "arbitrary", "arbitrary")),
        name="moba_merge",
    )(q, k, vaug, *([og] * hps))


def _moba_attention(q, k, vaug, kmean):
    B, H, S, Dh = q.shape
    nb = S // MOBA_BLOCK
    G = B * H
    max_tiles = (MOBA_TOPK * S) // GROUP_TILE + nb + 1
    max_tiles = -(-max_tiles // TILES_PER_STEP) * TILES_PER_STEP
    slab_rows = max_tiles * GROUP_TILE

    rt, counts = _moba_route(q, kmean)
    pos, tblk, tvalid, ntiles = _moba_layout(rt, counts.reshape(-1), nb, max_tiles)
    qg = _sc_scatter_rows(q.reshape(G * S, Dh), pos, G * slab_rows)
    parts = _moba_group(qg, k, vaug, tblk, tvalid, ntiles, max_tiles)
    og = _sc_gather_rows(parts, pos)
    return _moba_merge(q, k, vaug, og)


def _gated_proj_residual_kernel(o_ref, sg_ref, w_ref, x_ref, out_ref):
    y = (o_ref[...] * sg_ref[...]).astype(BF16)
    out_ref[...] = x_ref[...] + jnp.dot(y, w_ref[...], preferred_element_type=F32)


def _gated_proj_residual(o, sg, w, x2d):
    T, D = x2d.shape
    K = o.shape[1]
    return pl.pallas_call(
        _gated_proj_residual_kernel,
        grid=(T // ROW_TILE,),
        in_specs=[pl.BlockSpec((ROW_TILE, K), lambda i: (i, 0)),
                  pl.BlockSpec((ROW_TILE, K), lambda i: (i, 0)),
                  pl.BlockSpec((K, D), lambda i: (0, 0)),
                  pl.BlockSpec((ROW_TILE, D), lambda i: (i, 0))],
        out_specs=pl.BlockSpec((ROW_TILE, D), lambda i: (i, 0)),
        out_shape=jax.ShapeDtypeStruct((T, D), F32),
        compiler_params=_cparams(("arbitrary",)),
        name="moba_out",
    )(o, sg, w, x2d)


def _rope_tables(S):
    half = HEAD_DIM // 2
    inv = ROPE_THETA ** (-jnp.arange(half, dtype=F32) / half)
    ang = jnp.arange(S, dtype=I32).astype(F32)[:, None] * inv[None, :]
    cos, sin = jnp.cos(ang), jnp.sin(ang)
    return jnp.concatenate([cos, cos], axis=-1), jnp.concatenate([-sin, sin], axis=-1)


def kernel(x, a_norm, a_in_w, a_conv_w, a_conv_b, a_r_w, a_r_b, a_i_w, a_i_b, a_lambda, a_out_w,
           kv_norm, kv_w, k_norm, b_norm, b_in_w, q_norm, b_out_w):
    B, S, D = x.shape
    assert S % ROW_TILE == 0 and ROW_TILE % MOBA_BLOCK == 0
    assert D == N_RG_BLOCKS * RG_BW == N_HEADS * HEAD_DIM
    assert (B * N_HEADS * S) % (IDX_CHUNK * SC_WORKERS) == 0
    xs = x.reshape(B * S, D)
    row = lambda v: v.reshape(1, -1)

    for l in range(a_in_w.shape[0]):
        wri = jnp.concatenate([a_r_w[l], a_i_w[l]], axis=-1).astype(BF16)
        xs = _hawk_layer(xs, B, S, row(a_norm[l]), a_in_w[l].astype(BF16), a_conv_w[l], row(a_conv_b[l]), wri,
                         row(a_r_b[l]), row(a_i_b[l]), row(a_lambda[l]), a_out_w[l].astype(BF16))

    cosf, sinf = _rope_tables(S)
    k, vaug, kmean = _kv_proj(xs, B, S, row(kv_norm), kv_w.astype(BF16), row(k_norm), cosf, sinf)
    kmean = jnp.transpose(kmean, (0, 2, 1, 3))

    for jl in range(b_in_w.shape[0]):
        q, sg = _q_proj(xs, B, S, row(b_norm[jl]), b_in_w[jl].astype(BF16), row(q_norm[jl]), cosf, sinf)
        o = _moba_attention(q, k, vaug, kmean)
        xs = _gated_proj_residual(o, sg, b_out_w[jl].astype(BF16), xs)
    return xs.reshape(B, S, D)
```

```python
import functools

import jax
import jax.numpy as jnp
import numpy as np
from jax import lax
from jax.experimental import pallas as pl
from jax.experimental.pallas import tpu as pltpu
from jax.experimental.pallas import tpu_sc as plsc

N_HEADS = 8
HEAD_DIM = 128
MOBA_BLOCK = 256
MOBA_TOPK = 3
CONV_WIDTH = 4
N_RG_BLOCKS = 8
RG_BW = 128
RG_C = 8.0
ROPE_THETA = 10000.0
EPS = 1e-6
NEG_INF = -1e30
LOG2_E = 1.4426950408889634

SUBLANES = 8
LANES = 128
ROW_TILE = 512
VMEM_LIMIT = 56 * 1024 * 1024

GROUP_TILE = MOBA_BLOCK
TILES_PER_STEP = 8
HEADS_PER_STEP = 8
ROUTE_TILES = 4
ROUTE_BLK_SHIFT = 16
ROUTE_BLK_UNIT = 1 << ROUTE_BLK_SHIFT
HALF = HEAD_DIM // 2
VA_W = 2 * HEAD_DIM
L_LANE = HALF
HI16_MASK = -65536
BF16_HALF_ULP = 0x8000
NEG_INF_BITS = int(np.float32(NEG_INF).view(np.int32))

SC_CORES = 2
SC_SUBCORES = 16
SC_WORKERS = SC_CORES * SC_SUBCORES
IDX_CHUNK = 128
SC_HALVES = MOBA_BLOCK // IDX_CHUNK

F32 = jnp.float32
BF16 = jnp.bfloat16
I32 = jnp.int32


def _cparams(sem):
    return pltpu.CompilerParams(dimension_semantics=sem, vmem_limit_bytes=VMEM_LIMIT)


def _rms_norm(x, g):
    ms = jnp.mean(x * x, axis=-1, keepdims=True)
    return (x * lax.rsqrt(ms + EPS)) * g


def _silu(x):
    hx = 0.5 * x
    return hx * jnp.tanh(hx) + hx


def _sigmoid(x):
    return 0.5 * jnp.tanh(0.5 * x) + 0.5


def _nt_dot(a, b):
    return lax.dot_general(a, b, (((1,), (1,)), ((), ())), preferred_element_type=F32)


def _head_norm_rope(z, g, cosf, sinf):
    outs = []
    for h in range(N_HEADS):
        zh = z[:, h * HEAD_DIM:(h + 1) * HEAD_DIM]
        zh = _rms_norm(zh, g)
        outs.append(zh * cosf + pltpu.roll(zh, HEAD_DIM // 2, 1) * sinf)
    return outs


def _hawk_kernel(x_ref, g_ref, inw_ref, cw_ref, cb_ref, wri_ref, rb_ref, ib_ref, lam_ref, outw_ref,
                 o_ref, gate_ref, xpad_ref, a_ref, b_ref, hc_ref):
    tm, C = gate_ref.shape
    s = pl.program_id(1)

    @pl.when(s == 0)
    def _():
        xpad_ref[pl.ds(0, SUBLANES), :] = jnp.zeros((SUBLANES, xpad_ref.shape[1]), F32)
        hc_ref[...] = jnp.zeros_like(hc_ref)

    h = _rms_norm(x_ref[...], g_ref[...]).astype(BF16)
    gate_ref[...] = jnp.dot(h, inw_ref[:, C:], preferred_element_type=F32)

    xpad_ref[pl.ds(SUBLANES, tm), :] = jnp.dot(h, inw_ref[:, :C], preferred_element_type=F32)
    cw = cw_ref[...]
    xc = xpad_ref[pl.ds(SUBLANES - (CONV_WIDTH - 1), tm), :] * cw[0:1, :]
    for k in range(1, CONV_WIDTH):
        xc = xc + xpad_ref[pl.ds(SUBLANES - (CONV_WIDTH - 1) + k, tm), :] * cw[k:k + 1, :]
    xc = xc + cb_ref[...]
    xpad_ref[pl.ds(0, SUBLANES), :] = xpad_ref[pl.ds(tm, SUBLANES), :]

    lam = lam_ref[...]
    sp = jnp.maximum(-lam, 0.0) + jnp.log1p(jnp.exp(-jnp.abs(lam)))
    log2a_per_r = (-RG_C * LOG2_E) * sp
    xc16 = xc.astype(BF16)
    for g in range(N_RG_BLOCKS):
        lo = g * RG_BW
        z = jnp.dot(xc16[:, lo:lo + RG_BW], wri_ref[g], preferred_element_type=F32)
        r = _sigmoid(z[:, :RG_BW] + rb_ref[:, lo:lo + RG_BW])
        i = _sigmoid(z[:, RG_BW:] + ib_ref[:, lo:lo + RG_BW])
        a = jnp.exp2(r * log2a_per_r[:, lo:lo + RG_BW])
        mult = jnp.sqrt(1.0 - a * a)
        a_ref[:, lo:lo + RG_BW] = a
        b_ref[:, lo:lo + RG_BW] = mult * (i * xc[:, lo:lo + RG_BW])

    row = lax.broadcasted_iota(I32, (SUBLANES, a_ref.shape[1]), 0)

    def scan_body(c, hc):
        r0 = pl.multiple_of(c * SUBLANES, SUBLANES)
        a = a_ref[pl.ds(r0, SUBLANES), :]
        b = b_ref[pl.ds(r0, SUBLANES), :]
        for k in (1, 2, 4):
            keep = row >= k
            b = jnp.where(keep, a * pltpu.roll(b, k, 0) + b, b)
            a = jnp.where(keep, a * pltpu.roll(a, k, 0), a)
        h = a * hc + b
        b_ref[pl.ds(r0, SUBLANES), :] = h
        return h[SUBLANES - 1:SUBLANES, :]

    hc = lax.fori_loop(0, tm // SUBLANES, scan_body, hc_ref[0:1, :])
    hc_ref[0:1, :] = hc

    y = (b_ref[...] * _silu(gate_ref[...])).astype(BF16)
    o_ref[...] = x_ref[...] + jnp.dot(y, outw_ref[...], preferred_element_type=F32)


def _hawk_layer(x2d, B, S, g, in_w, cw, cb, wri, rb, ib, lam, out_w):
    T, D = x2d.shape
    C = in_w.shape[1] // 2
    tm = ROW_TILE
    ns = S // tm
    vec = lambda n: pl.BlockSpec((1, n), lambda b, s: (0, 0))
    return pl.pallas_call(
        _hawk_kernel,
        grid=(B, ns),
        in_specs=[pl.BlockSpec((tm, D), lambda b, s: (b * ns + s, 0)),
                  vec(D),
                  pl.BlockSpec((D, 2 * C), lambda b, s: (0, 0)),
                  pl.BlockSpec((CONV_WIDTH, C), lambda b, s: (0, 0)),
                  vec(C),
                  pl.BlockSpec((N_RG_BLOCKS, RG_BW, 2 * RG_BW), lambda b, s: (0, 0, 0)),
                  vec(C), vec(C), vec(C),
                  pl.BlockSpec((C, D), lambda b, s: (0, 0))],
        out_specs=pl.BlockSpec((tm, D), lambda b, s: (b * ns + s, 0)),
        out_shape=jax.ShapeDtypeStruct((T, D), F32),
        scratch_shapes=[pltpu.VMEM((tm, C), F32),
                        pltpu.VMEM((tm + SUBLANES, C), F32),
                        pltpu.VMEM((tm, C), F32),
                        pltpu.VMEM((tm, C), F32),
                        pltpu.VMEM((SUBLANES, C), F32)],
        compiler_params=_cparams(("arbitrary", "arbitrary")),
        name="hawk_layer",
    )(x2d, g, in_w, cw, cb, wri, rb, ib, lam, out_w)


def _kv_proj_kernel(x_ref, g_ref, w_ref, kg_ref, cos_ref, sin_ref, k_ref, v_ref, km_ref):
    tm = x_ref.shape[0]
    d_attn = N_HEADS * HEAD_DIM
    h = _rms_norm(x_ref[...], g_ref[...]).astype(BF16)
    kv = jnp.dot(h, w_ref[...], preferred_element_type=F32)
    ks = _head_norm_rope(kv[:, :d_attn], kg_ref[...], cos_ref[...], sin_ref[...])
    lane = lax.broadcasted_iota(I32, (tm, HEAD_DIM), 1)
    for hd in range(N_HEADS):
        k_ref[0, hd] = ks[hd].astype(BF16)
        vh = kv[:, d_attn + hd * HEAD_DIM:d_attn + (hd + 1) * HEAD_DIM]
        lo = jnp.where(lane < HALF, vh, jnp.where(lane == L_LANE, 1.0, 0.0))
        hi = jnp.where(lane < HALF, pltpu.roll(vh, HALF, 1), 0.0)
        v_ref[0, hd] = jnp.concatenate([lo, hi], axis=1).astype(BF16)
        for r in range(tm // MOBA_BLOCK):
            blk = ks[hd][r * MOBA_BLOCK:(r + 1) * MOBA_BLOCK, :]
            km_ref[0, r, pl.ds(hd, 1), :] = jnp.mean(blk, axis=0, keepdims=True)


def _kv_proj(x2d, B, S, g, w, kg, cosf, sinf):
    T, D = x2d.shape
    tm = ROW_TILE
    ns = S // tm
    nb = S // MOBA_BLOCK
    rpt = tm // MOBA_BLOCK
    return pl.pallas_call(
        _kv_proj_kernel,
        grid=(B, ns),
        in_specs=[pl.BlockSpec((tm, D), lambda b, s: (b * ns + s, 0)),
                  pl.BlockSpec((1, D), lambda b, s: (0, 0)),
                  pl.BlockSpec((D, 2 * N_HEADS * HEAD_DIM), lambda b, s: (0, 0)),
                  pl.BlockSpec((1, HEAD_DIM), lambda b, s: (0, 0)),
                  pl.BlockSpec((tm, HEAD_DIM), lambda b, s: (s, 0)),
                  pl.BlockSpec((tm, HEAD_DIM), lambda b, s: (s, 0))],
        out_specs=[pl.BlockSpec((1, N_HEADS, tm, HEAD_DIM), lambda b, s: (b, 0, s, 0)),
                   pl.BlockSpec((1, N_HEADS, tm, VA_W), lambda b, s: (b, 0, s, 0)),
                   pl.BlockSpec((1, rpt, N_HEADS, HEAD_DIM), lambda b, s: (b, s, 0, 0))],
        out_shape=[jax.ShapeDtypeStruct((B, N_HEADS, S, HEAD_DIM), BF16),
                   jax.ShapeDtypeStruct((B, N_HEADS, S, VA_W), BF16),
                   jax.ShapeDtypeStruct((B, nb, N_HEADS, HEAD_DIM), F32)],
        compiler_params=_cparams(("arbitrary", "arbitrary")),
        name="kv_proj",
    )(x2d, g, w, kg, cosf, sinf)


def _q_proj_kernel(x_ref, g_ref, w_ref, qg_ref, cos_ref, sin_ref, q_ref, sg_ref):
    d_attn = N_HEADS * HEAD_DIM
    h = _rms_norm(x_ref[...], g_ref[...]).astype(BF16)
    u = jnp.dot(h, w_ref[...], preferred_element_type=F32)
    qs = _head_norm_rope(u[:, :d_attn], qg_ref[...], cos_ref[...], sin_ref[...])
    for hd in range(N_HEADS):
        q_ref[0, hd] = qs[hd]
    sg_ref[...] = _silu(u[:, d_attn:])


def _q_proj(x2d, B, S, g, w, qg, cosf, sinf):
    T, D = x2d.shape
    tm = ROW_TILE
    ns = S // tm
    d_attn = N_HEADS * HEAD_DIM
    return pl.pallas_call(
        _q_proj_kernel,
        grid=(B, ns),
        in_specs=[pl.BlockSpec((tm, D), lambda b, s: (b * ns + s, 0)),
                  pl.BlockSpec((1, D), lambda b, s: (0, 0)),
                  pl.BlockSpec((D, 2 * d_attn), lambda b, s: (0, 0)),
                  pl.BlockSpec((1, HEAD_DIM), lambda b, s: (0, 0)),
                  pl.BlockSpec((tm, HEAD_DIM), lambda b, s: (s, 0)),
                  pl.BlockSpec((tm, HEAD_DIM), lambda b, s: (s, 0))],
        out_specs=[pl.BlockSpec((1, N_HEADS, tm, HEAD_DIM), lambda b, s: (b, 0, s, 0)),
                   pl.BlockSpec((tm, d_attn), lambda b, s: (b * ns + s, 0))],
        out_shape=[jax.ShapeDtypeStruct((B, N_HEADS, S, HEAD_DIM), F32),
                   jax.ShapeDtypeStruct((T, d_attn), F32)],
        compiler_params=_cparams(("arbitrary", "arbitrary")),
        name="q_proj",
    )(x2d, g, w, qg, cosf, sinf)


def _moba_route_kernel(q_ref, km_ref, rt_ref, cnt_ref, run_ref):
    step = pl.program_id(2)
    nb = km_ref.shape[2]

    @pl.when(step == 0)
    def _():
        run_ref[...] = jnp.zeros_like(run_ref)

    km = km_ref[0, 0].astype(BF16)
    blk = lax.broadcasted_iota(I32, (nb, MOBA_BLOCK), 0)
    qa = lax.broadcasted_iota(I32, (MOBA_BLOCK, MOBA_BLOCK), 0)
    qb = lax.broadcasted_iota(I32, (MOBA_BLOCK, MOBA_BLOCK), 1)
    earlier = jnp.where(qa < qb, 1.0, 0.0).astype(BF16)
    run = run_ref[...]
    for u in range(ROUTE_TILES):
        j = step * ROUTE_TILES + u
        g = _nt_dot(km, q_ref[0, 0, pl.ds(u * MOBA_BLOCK, MOBA_BLOCK), :].astype(BF16))
        g = jnp.where(blk < j, g, -jnp.inf)
        hits, ids = [], []
        for _ in range(MOBA_TOPK):
            mx = jnp.max(g, axis=0, keepdims=True)
            idx = jnp.min(jnp.where(g == mx, blk, nb), axis=0, keepdims=True)
            hit = blk == idx
            ok = mx > -jnp.inf
            hits.append(jnp.where(hit & ok, 1.0, 0.0))
            ids.append(jnp.where(ok, idx, -1))
            g = jnp.where(hit, -jnp.inf, g)
        sel = hits[0] + hits[1] + hits[2]

        rank = jnp.dot(sel.astype(BF16), earlier, preferred_element_type=F32)
        dest = run + rank
        run = run + jnp.sum(sel, axis=1, keepdims=True)
        for r in range(MOBA_TOPK):
            loc = jnp.sum(hits[r] * dest, axis=0, keepdims=True).astype(I32)
            rt_ref[u, pl.ds(r, 1), :] = jnp.where(ids[r] >= 0, ids[r] * ROUTE_BLK_UNIT + loc, -1)
        rt_ref[u, pl.ds(MOBA_TOPK, SUBLANES - MOBA_TOPK), :] = jnp.full(
            (SUBLANES - MOBA_TOPK, MOBA_BLOCK), -1, I32)
    run_ref[...] = run
    cnt_ref[0] = run.astype(I32)


def _moba_route(q, kmean):
    B, H, S, Dh = q.shape
    nb = S // MOBA_BLOCK
    rows = ROUTE_TILES * MOBA_BLOCK
    nsteps = nb // ROUTE_TILES
    return pl.pallas_call(
        _moba_route_kernel,
        grid=(B, H, nsteps),
        in_specs=[pl.BlockSpec((1, 1, rows, Dh), lambda b, h, j: (b, h, j, 0)),
                  pl.BlockSpec((1, 1, nb, Dh), lambda b, h, j: (b, h, 0, 0))],
        out_specs=[pl.BlockSpec((ROUTE_TILES, SUBLANES, MOBA_BLOCK),
                                lambda b, h, j: ((b * H + h) * nsteps + j, 0, 0)),
                   pl.BlockSpec((1, nb, 1), lambda b, h, j: (b * H + h, 0, 0))],
        out_shape=[jax.ShapeDtypeStruct((B * H * nb, SUBLANES, MOBA_BLOCK), I32),
                   jax.ShapeDtypeStruct((B * H, nb, 1), I32)],
        scratch_shapes=[pltpu.VMEM((nb, 1), F32)],
        compiler_params=_cparams(("arbitrary", "arbitrary", "arbitrary")),
        name="moba_route",
    )(q, kmean)


def _moba_layout_kernel(cnt_ref, rt_ref, pos_ref, tblk_ref, tvalid_ref, nt_ref, base_ref, *, max_tiles):
    g = pl.program_id(0)
    nb = rt_ref.shape[0]
    slab_rows = max_tiles * GROUP_TILE
    null_row0 = (max_tiles - 1) * GROUP_TILE

    def per_block(n, first_tile):
        count = cnt_ref[g * nb + n]
        tiles = (count + (GROUP_TILE - 1)) // GROUP_TILE
        base_ref[n] = first_tile * GROUP_TILE

        def per_tile(i, carry):
            tblk_ref[g * max_tiles + first_tile + i] = n
            tvalid_ref[g * max_tiles + first_tile + i] = jnp.minimum(count - i * GROUP_TILE, GROUP_TILE)
            return carry

        lax.fori_loop(0, tiles, per_tile, 0)
        return first_tile + tiles

    used = lax.fori_loop(0, nb, per_block, 0)
    nt_ref[g] = used

    def unused_tile(t, carry):
        tblk_ref[g * max_tiles + t] = nb - 1
        tvalid_ref[g * max_tiles + t] = 0
        return carry

    lax.fori_loop(used, max_tiles, unused_tile, 0)

    code = rt_ref[...].reshape(nb * SUBLANES, MOBA_BLOCK)
    blk = lax.shift_right_arithmetic(code, ROUTE_BLK_SHIFT)
    base = jnp.zeros_like(code)
    for n in range(nb):
        base = jnp.where(blk == n, base_ref[n], base)
    lane = lax.broadcasted_iota(I32, code.shape, 1)
    row = jnp.where(code >= 0, base + (code & (ROUTE_BLK_UNIT - 1)), null_row0 + lane)
    pos_ref[...] = (row + g * slab_rows).reshape(pos_ref.shape)


def _moba_layout(rt, counts, nb, max_tiles):
    G = rt.shape[0] // nb
    smem = pl.BlockSpec(memory_space=pltpu.SMEM)
    return pl.pallas_call(
        functools.partial(_moba_layout_kernel, max_tiles=max_tiles),
        grid_spec=pltpu.PrefetchScalarGridSpec(
            num_scalar_prefetch=1,
            grid=(G,),
            in_specs=[pl.BlockSpec((nb, SUBLANES, MOBA_BLOCK), lambda g, cnt: (g, 0, 0))],
            out_specs=[pl.BlockSpec((nb, SUBLANES, MOBA_BLOCK), lambda g, cnt: (g, 0, 0)), smem, smem, smem],
            scratch_shapes=[pltpu.SMEM((nb,), I32)],
        ),
        out_shape=[jax.ShapeDtypeStruct((G * nb, SUBLANES, MOBA_BLOCK), I32),
                   jax.ShapeDtypeStruct((G * max_tiles,), I32),
                   jax.ShapeDtypeStruct((G * max_tiles,), I32),
                   jax.ShapeDtypeStruct((G,), I32)],
        compiler_params=_cparams(("arbitrary",)),
        name="moba_layout",
    )(counts, rt)


def _sc_mesh():
    return plsc.VectorSubcoreMesh(core_axis_name="c", subcore_axis_name="s")


def _sc_load_indices(pos_hbm, idx_v, t0, n_tiles):
    for c in range(SC_HALVES):
        pltpu.sync_copy(pos_hbm.at[pl.ds(t0, n_tiles), pl.ds(0, MOBA_TOPK), pl.ds(c * IDX_CHUNK, IDX_CHUNK)],
                        idx_v.at[c])


def _sc_scatter_rows(src, pos, n_out_rows):
    N, D = src.shape
    n_tiles = N // MOBA_BLOCK
    per_w = n_tiles // SC_WORKERS
    assert per_w * SC_WORKERS == n_tiles

    @functools.partial(
        pl.kernel, mesh=_sc_mesh(),
        out_type=jax.ShapeDtypeStruct((n_out_rows, D), src.dtype),
        scratch_types=[pltpu.VMEM((SC_HALVES, per_w, MOBA_TOPK, IDX_CHUNK), I32),
                       pltpu.VMEM((SC_HALVES, IDX_CHUNK, D), src.dtype),
                       pltpu.SemaphoreType.DMA((SC_HALVES,)),
                       pltpu.SemaphoreType.DMA((SC_HALVES,))],
        name="sc_scatter_rows",
    )
    def k(src_hbm, pos_hbm, out_hbm, idx_v, rows_v, load_sem, scat_sem):
        t0 = (lax.axis_index("s") * SC_CORES + lax.axis_index("c")) * per_w
        _sc_load_indices(pos_hbm, idx_v, t0, per_w)

        def load(i, c):
            row0 = pl.multiple_of((t0 + i) * MOBA_BLOCK + c * IDX_CHUNK, IDX_CHUNK)
            return pltpu.make_async_copy(src_hbm.at[pl.ds(row0, IDX_CHUNK)], rows_v.at[c], load_sem.at[c])

        def scatters(i, c):
            return [pltpu.make_async_copy(rows_v.at[c], out_hbm.at[idx_v.at[c, i, r]], scat_sem.at[c])
                    for r in range(MOBA_TOPK)]

        load(0, 0).start()

        @pl.loop(0, per_w)
        def _(i):
            for c in range(SC_HALVES):
                pi, pc = (i, 0) if c else (i - 1, 1)
                ni, nc = (i + 1, 0) if c else (i, 1)
                load(i, c).wait()
                for cp in scatters(i, c):
                    cp.start()

                @pl.when(pi >= 0)
                def _():
                    for cp in scatters(pi, pc):
                        cp.wait()

                @pl.when(ni < per_w)
                def _():
                    load(ni, nc).start()

        for cp in scatters(per_w - 1, SC_HALVES - 1):
            cp.wait()

    return k(src, pos)


def _sc_gather_rows(table, pos):
    R, D = table.shape
    n_tiles = pos.shape[0]
    N = n_tiles * MOBA_BLOCK
    per_w = n_tiles // SC_WORKERS
    assert per_w * SC_WORKERS == n_tiles

    items = SC_HALVES * MOBA_TOPK
    n_buf = 2
    assert items % n_buf == 0

    @functools.partial(
        pl.kernel, mesh=_sc_mesh(),
        out_type=jax.ShapeDtypeStruct((MOBA_TOPK, N, D), table.dtype),
        scratch_types=[pltpu.VMEM((SC_HALVES, per_w, MOBA_TOPK, IDX_CHUNK), I32),
                       pltpu.VMEM((n_buf, IDX_CHUNK, D), table.dtype),
                       pltpu.SemaphoreType.DMA((n_buf,)),
                       pltpu.SemaphoreType.DMA((n_buf,))],
        name="sc_gather_rows",
    )
    def k(table_hbm, pos_hbm, out_hbm, idx_v, rows_v, gat_sem, put_sem):
        t0 = (lax.axis_index("s") * SC_CORES + lax.axis_index("c")) * per_w
        _sc_load_indices(pos_hbm, idx_v, t0, per_w)

        def gather(i, j):
            c, r = divmod(j, MOBA_TOPK)
            return pltpu.make_async_copy(table_hbm.at[idx_v.at[c, i, r]], rows_v.at[j % n_buf], gat_sem.at[j % n_buf])

        def put(i, j):
            c, r = divmod(j, MOBA_TOPK)
            row0 = pl.multiple_of((t0 + i) * MOBA_BLOCK + c * IDX_CHUNK, IDX_CHUNK)
            return pltpu.make_async_copy(rows_v.at[j % n_buf], out_hbm.at[r, pl.ds(row0, IDX_CHUNK)],
                                         put_sem.at[j % n_buf])

        gather(0, 0).start()

        @pl.loop(0, per_w)
        def _(i):
            for j in range(items):
                pi, pj = (i, j - 1) if j else (i - 1, items - 1)
                ni, nj = (i, j + 1) if j + 1 < items else (i + 1, 0)
                gather(i, j).wait()
                put(i, j).start()

                @pl.when(pi >= 0)
                def _():
                    put(pi, pj).wait()

                @pl.when(ni < per_w)
                def _():
                    gather(ni, nj).start()

        put(per_w - 1, items - 1).wait()

    return k(table, pos)


def _bits(x):
    return lax.bitcast_convert_type(x, I32)


def _pack_partial(part, m, live):
    lane = lax.broadcasted_iota(I32, (part.shape[0], HEAD_DIM), 1)
    lo = _bits(part[:, :HEAD_DIM]) + BF16_HALF_ULP
    hi = _bits(jnp.where(lane == L_LANE, m, part[:, HEAD_DIM:])) + BF16_HALF_ULP
    word = (hi & HI16_MASK) | lax.shift_right_logical(lo, 16)
    return jnp.where(live, word, jnp.where(lane == L_LANE, NEG_INF_BITS & HI16_MASK, 0))


def _unpack_partial(word):
    lo = lax.bitcast_convert_type(lax.shift_left(word, 16), F32)
    hi = lax.bitcast_convert_type(word & HI16_MASK, F32)
    return lo, hi


def _moba_group_kernel(tblk_ref, tvalid_ref, nt_ref, qg_ref, k_ref, v_ref, o_ref, *, max_tiles):
    g, step = pl.program_id(0), pl.program_id(1)
    t0 = g * max_tiles + step * TILES_PER_STEP
    scale = HEAD_DIM ** -0.5

    @pl.when((tvalid_ref[t0] > 0) | (step == pl.num_programs(1) - 1))
    def _():
        row = lax.broadcasted_iota(I32, (GROUP_TILE, 1), 0)
        for u in range(TILES_PER_STEP):
            rows = pl.ds(u * GROUP_TILE, GROUP_TILE)
            live = row < tvalid_ref[t0 + u]
            q = jnp.where(live, qg_ref[rows, :], 0.0).astype(BF16)
            n0 = pl.multiple_of(tblk_ref[t0 + u] * MOBA_BLOCK, MOBA_BLOCK)
            s = _nt_dot(q, k_ref[0, 0, pl.ds(n0, MOBA_BLOCK), :])
            m = (jnp.max(s, axis=1, keepdims=True) * scale).astype(BF16).astype(F32)
            p = jnp.exp2(s * (scale * LOG2_E) - m * LOG2_E).astype(BF16)
            part = jnp.dot(p, v_ref[0, 0, pl.ds(n0, MOBA_BLOCK), :], preferred_element_type=F32)
            o_ref[rows, :] = _pack_partial(part, m, live)


def _moba_group(qg, k, vaug, tblk, tvalid, ntiles, max_tiles):
    B, H, S, Dh = k.shape
    G = B * H
    n_steps = max_tiles // TILES_PER_STEP
    step_rows = TILES_PER_STEP * GROUP_TILE

    def tile_idx(g, s, tblk, tvalid, nt):
        return (g * n_steps + jnp.where(s * TILES_PER_STEP < nt[g], s, n_steps - 1), 0)

    return pl.pallas_call(
        functools.partial(_moba_group_kernel, max_tiles=max_tiles),
        grid_spec=pltpu.PrefetchScalarGridSpec(
            num_scalar_prefetch=3,
            grid=(G, n_steps),
            in_specs=[pl.BlockSpec((step_rows, Dh), tile_idx),
                      pl.BlockSpec((1, 1, S, Dh), lambda g, s, *_: (g // H, g % H, 0, 0)),
                      pl.BlockSpec((1, 1, S, VA_W), lambda g, s, *_: (g // H, g % H, 0, 0))],
            out_specs=pl.BlockSpec((step_rows, LANES), tile_idx),
        ),
        out_shape=jax.ShapeDtypeStruct((G * max_tiles * GROUP_TILE, LANES), I32),
        compiler_params=_cparams(("arbitrary", "arbitrary")),
        name="moba_group",
    )(tblk, tvalid, ntiles, qg, k, vaug)


def _moba_merge_kernel(q_ref, k_ref, v_ref, *refs):
    og_refs, o_ref = refs[:-1], refs[-1]
    scale = HEAD_DIM ** -0.5
    qi = lax.broadcasted_iota(I32, (MOBA_BLOCK, MOBA_BLOCK), 0)
    ki = lax.broadcasted_iota(I32, (MOBA_BLOCK, MOBA_BLOCK), 1)
    lane = lax.broadcasted_iota(I32, (MOBA_BLOCK, HEAD_DIM), 1)
    spread = jnp.where(lax.broadcasted_iota(I32, (HEAD_DIM, HEAD_DIM), 0) == L_LANE, 1.0, 0.0).astype(BF16)
    for u, og_ref in enumerate(og_refs):
        q = q_ref[0, u].astype(BF16)
        s = _nt_dot(q, k_ref[0, u])
        s = jnp.where(ki <= qi, s, NEG_INF)
        m_raw = jnp.max(s, axis=1, keepdims=True)
        p = jnp.exp2((s - m_raw) * (scale * LOG2_E)).astype(BF16)
        own = jnp.dot(p, v_ref[0, u], preferred_element_type=F32)
        parts = [_unpack_partial(og_ref[r]) for r in range(MOBA_TOPK)]
        ms = [jnp.dot(jnp.where(lane == L_LANE, hi, 0.0).astype(BF16), spread, preferred_element_type=F32)
              for _, hi in parts]
        m_own = jnp.broadcast_to(m_raw * scale, lane.shape)
        m_all = m_own
        for m in ms:
            m_all = jnp.maximum(m_all, m)
        w_own = jnp.exp2((m_own - m_all) * LOG2_E)
        tot_lo = w_own * own[:, :HEAD_DIM]
        tot_hi = w_own * own[:, HEAD_DIM:]
        for (lo, hi), m in zip(parts, ms):
            w = jnp.exp2((m - m_all) * LOG2_E)
            tot_lo = tot_lo + w * lo
            tot_hi = tot_hi + w * hi
        acc = jnp.where(lane < HALF, tot_lo, pltpu.roll(tot_hi, HALF, 1))
        o_ref[:, u * HEAD_DIM:(u + 1) * HEAD_DIM] = acc / tot_lo[:, L_LANE:L_LANE + 1]


def _moba_merge(q, k, vaug, og):
    B, H, S, Dh = q.shape
    nb = S // MOBA_BLOCK
    hps = HEADS_PER_STEP
    og_spec = lambda u: pl.BlockSpec((MOBA_TOPK, MOBA_BLOCK, LANES),
                                     lambda b, hp, j: (0, (b * H + hp * hps + u) * nb + j, 0))
    return pl.pallas_call(
        _moba_merge_kernel,
        grid=(B, H // hps, nb),
        in_specs=[pl.BlockSpec((1, hps, MOBA_BLOCK, Dh), lambda b, hp, j: (b, hp, j, 0)),
                  pl.BlockSpec((1, hps, MOBA_BLOCK, Dh), lambda b, hp, j: (b, hp, j, 0)),
                  pl.BlockSpec((1, hps, MOBA_BLOCK, VA_W), lambda b, hp, j: (b, hp, j, 0))]
                 + [og_spec(u) for u in range(hps)],
        out_specs=pl.BlockSpec((MOBA_BLOCK, hps * Dh), lambda b, hp, j: (b * nb + j, hp)),
        out_shape=jax.ShapeDtypeStruct((B * S, H * Dh), F32),
        compiler_params=_cparams(("arbitrary", "arbitrary", "arbitrary")),
        name="moba_merge",
    )(q, k, vaug, *([og] * hps))


def _moba_attention(q, k, vaug, kmean):
    B, H, S, Dh = q.shape
    nb = S // MOBA_BLOCK
    G = B * H
    max_tiles = (MOBA_TOPK * S) // GROUP_TILE + nb + 1
    max_tiles = -(-max_tiles // TILES_PER_STEP) * TILES_PER_STEP
    slab_rows = max_tiles * GROUP_TILE

    rt, counts = _moba_route(q, kmean)
    pos, tblk, tvalid, ntiles = _moba_layout(rt, counts.reshape(-1), nb, max_tiles)
    qg = _sc_scatter_rows(q.reshape(G * S, Dh), pos, G * slab_rows)
    parts = _moba_group(qg, k, vaug, tblk, tvalid, ntiles, max_tiles)
    og = _sc_gather_rows(parts, pos)
    return _moba_merge(q, k, vaug, og)


def _gated_proj_residual_kernel(o_ref, sg_ref, w_ref, x_ref, out_ref):
    y = (o_ref[...] * sg_ref[...]).astype(BF16)
    out_ref[...] = x_ref[...] + jnp.dot(y, w_ref[...], preferred_element_type=F32)


def _gated_proj_residual(o, sg, w, x2d):
    T, D = x2d.shape
    K = o.shape[1]
    return pl.pallas_call(
        _gated_proj_residual_kernel,
        grid=(T // ROW_TILE,),
        in_specs=[pl.BlockSpec((ROW_TILE, K), lambda i: (i, 0)),
                  pl.BlockSpec((ROW_TILE, K), lambda i: (i, 0)),
                  pl.BlockSpec((K, D), lambda i: (0, 0)),
                  pl.BlockSpec((ROW_TILE, D), lambda i: (i, 0))],
        out_specs=pl.BlockSpec((ROW_TILE, D), lambda i: (i, 0)),
        out_shape=jax.ShapeDtypeStruct((T, D), F32),
        compiler_params=_cparams(("arbitrary",)),
        name="moba_out",
    )(o, sg, w, x2d)


def _rope_tables(S):
    inv = np.float32(ROPE_THETA) ** (-np.arange(HALF, dtype=np.float32) / np.float32(HALF))
    ang = np.arange(S, dtype=np.float32)[:, None] * inv[None, :].astype(np.float32)
    cos, sin = np.cos(ang).astype(np.float32), np.sin(ang).astype(np.float32)
    return (jnp.asarray(np.concatenate([cos, cos], axis=-1)),
            jnp.asarray(np.concatenate([-sin, sin], axis=-1)))


def kernel(x, a_norm, a_in_w, a_conv_w, a_conv_b, a_r_w, a_r_b, a_i_w, a_i_b, a_lambda, a_out_w,
           kv_norm, kv_w, k_norm, b_norm, b_in_w, q_norm, b_out_w):
    B, S, D = x.shape
    assert S % ROW_TILE == 0 and ROW_TILE % MOBA_BLOCK == 0
    assert D == N_RG_BLOCKS * RG_BW == N_HEADS * HEAD_DIM
    assert (B * N_HEADS * S) % (IDX_CHUNK * SC_WORKERS) == 0
    xs = x.reshape(B * S, D)
    row = lambda v: v.reshape(1, -1)

    for l in range(a_in_w.shape[0]):
        wri = jnp.concatenate([a_r_w[l], a_i_w[l]], axis=-1).astype(BF16)
        xs = _hawk_layer(xs, B, S, row(a_norm[l]), a_in_w[l].astype(BF16), a_conv_w[l], row(a_conv_b[l]), wri,
                         row(a_r_b[l]), row(a_i_b[l]), row(a_lambda[l]), a_out_w[l].astype(BF16))

    cosf, sinf = _rope_tables(S)
    k, vaug, kmean = _kv_proj(xs, B, S, row(kv_norm), kv_w.astype(BF16), row(k_norm), cosf, sinf)
    kmean = jnp.transpose(kmean, (0, 2, 1, 3))

    for jl in range(b_in_w.shape[0]):
        q, sg = _q_proj(xs, B, S, row(b_norm[jl]), b_in_w[jl].astype(BF16), row(q_norm[jl]), cosf, sinf)
        o = _moba_attention(q, k, vaug, kmean)
        xs = _gated_proj_residual(o, sg, b_out_w[jl].astype(BF16), xs)
    return xs.reshape(B, S, D)
```

```python
import functools

import jax
import jax.numpy as jnp
import numpy as np
from jax import lax
from jax.experimental import pallas as pl
from jax.experimental.pallas import tpu as pltpu
from jax.experimental.pallas import tpu_sc as plsc

N_HEADS = 8
HEAD_DIM = 128
MOBA_BLOCK = 256
MOBA_TOPK = 3
CONV_WIDTH = 4
N_RG_BLOCKS = 8
RG_BW = 128
RG_C = 8.0
ROPE_THETA = 10000.0
EPS = 1e-6
NEG_INF = -1e30
LOG2_E = 1.4426950408889634

SUBLANES = 8
LANES = 128
ROW_TILE = 512
VMEM_LIMIT = 56 * 1024 * 1024

GROUP_TILE = MOBA_BLOCK
TILES_PER_STEP = 8
HEADS_PER_STEP = 8
ROUTE_TILES = 4
ROUTE_BLK_SHIFT = 16
ROUTE_BLK_UNIT = 1 << ROUTE_BLK_SHIFT
HALF = HEAD_DIM // 2
VA_W = 2 * HEAD_DIM
L_LANE = HALF
HI16_MASK = -65536
BF16_HALF_ULP = 0x8000
NEG_INF_BITS = int(np.float32(NEG_INF).view(np.int32))

SC_CORES = 2
SC_SUBCORES = 16
SC_WORKERS = SC_CORES * SC_SUBCORES
IDX_CHUNK = 128
SC_HALVES = MOBA_BLOCK // IDX_CHUNK

F32 = jnp.float32
BF16 = jnp.bfloat16
I32 = jnp.int32


def _cparams(sem):
    return pltpu.CompilerParams(dimension_semantics=sem, vmem_limit_bytes=VMEM_LIMIT)


def _rms_norm(x, g):
    ms = jnp.mean(x * x, axis=-1, keepdims=True)
    return (x * lax.rsqrt(ms + EPS)) * g


def _silu(x):
    hx = 0.5 * x
    return hx * jnp.tanh(hx) + hx


def _sigmoid(x):
    return 0.5 * jnp.tanh(0.5 * x) + 0.5


def _nt_dot(a, b):
    return lax.dot_general(a, b, (((1,), (1,)), ((), ())), preferred_element_type=F32)


def _head_norm_rope(z, g, cosf, sinf):
    outs = []
    for h in range(N_HEADS):
        zh = z[:, h * HEAD_DIM:(h + 1) * HEAD_DIM]
        zh = _rms_norm(zh, g)
        outs.append(zh * cosf + pltpu.roll(zh, HEAD_DIM // 2, 1) * sinf)
    return outs


def _hawk_kernel(x_ref, g_ref, inw_ref, cw_ref, cb_ref, wri_ref, rb_ref, ib_ref, lam_ref, outw_ref,
                 o_ref, gate_ref, xpad_ref, a_ref, b_ref, hc_ref):
    tm, C = gate_ref.shape
    s = pl.program_id(1)

    @pl.when(s == 0)
    def _():
        xpad_ref[pl.ds(0, SUBLANES), :] = jnp.zeros((SUBLANES, xpad_ref.shape[1]), F32)
        hc_ref[...] = jnp.zeros_like(hc_ref)

    h = _rms_norm(x_ref[...], g_ref[...]).astype(BF16)
    gate_ref[...] = jnp.dot(h, inw_ref[:, C:], preferred_element_type=F32)

    xpad_ref[pl.ds(SUBLANES, tm), :] = jnp.dot(h, inw_ref[:, :C], preferred_element_type=F32)
    cw = cw_ref[...]
    xc = xpad_ref[pl.ds(SUBLANES - (CONV_WIDTH - 1), tm), :] * cw[0:1, :]
    for k in range(1, CONV_WIDTH):
        xc = xc + xpad_ref[pl.ds(SUBLANES - (CONV_WIDTH - 1) + k, tm), :] * cw[k:k + 1, :]
    xc = xc + cb_ref[...]
    xpad_ref[pl.ds(0, SUBLANES), :] = xpad_ref[pl.ds(tm, SUBLANES), :]

    lam = lam_ref[...]
    sp = jnp.maximum(-lam, 0.0) + jnp.log1p(jnp.exp(-jnp.abs(lam)))
    log2a_per_r = (-RG_C * LOG2_E) * sp
    xc16 = xc.astype(BF16)
    for g in range(N_RG_BLOCKS):
        lo = g * RG_BW
        z = jnp.dot(xc16[:, lo:lo + RG_BW], wri_ref[g], preferred_element_type=F32)
        r = _sigmoid(z[:, :RG_BW] + rb_ref[:, lo:lo + RG_BW])
        i = _sigmoid(z[:, RG_BW:] + ib_ref[:, lo:lo + RG_BW])
        a = jnp.exp2(r * log2a_per_r[:, lo:lo + RG_BW])
        mult = jnp.sqrt(1.0 - a * a)
        a_ref[:, lo:lo + RG_BW] = a
        b_ref[:, lo:lo + RG_BW] = mult * (i * xc[:, lo:lo + RG_BW])

    row = lax.broadcasted_iota(I32, (SUBLANES, a_ref.shape[1]), 0)

    def scan_body(c, hc):
        r0 = pl.multiple_of(c * SUBLANES, SUBLANES)
        a = a_ref[pl.ds(r0, SUBLANES), :]
        b = b_ref[pl.ds(r0, SUBLANES), :]
        for k in (1, 2, 4):
            keep = row >= k
            b = jnp.where(keep, a * pltpu.roll(b, k, 0) + b, b)
            a = jnp.where(keep, a * pltpu.roll(a, k, 0), a)
        h = a * hc + b
        b_ref[pl.ds(r0, SUBLANES), :] = h
        return h[SUBLANES - 1:SUBLANES, :]

    hc = lax.fori_loop(0, tm // SUBLANES, scan_body, hc_ref[0:1, :])
    hc_ref[0:1, :] = hc

    y = (b_ref[...] * _silu(gate_ref[...])).astype(BF16)
    o_ref[...] = x_ref[...] + jnp.dot(y, outw_ref[...], preferred_element_type=F32)


def _hawk_layer(x2d, B, S, g, in_w, cw, cb, wri, rb, ib, lam, out_w):
    T, D = x2d.shape
    C = in_w.shape[1] // 2
    tm = ROW_TILE
    ns = S // tm
    vec = lambda n: pl.BlockSpec((1, n), lambda b, s: (0, 0))
    return pl.pallas_call(
        _hawk_kernel,
        grid=(B, ns),
        in_specs=[pl.BlockSpec((tm, D), lambda b, s: (b * ns + s, 0)),
                  vec(D),
                  pl.BlockSpec((D, 2 * C), lambda b, s: (0, 0)),
                  pl.BlockSpec((CONV_WIDTH, C), lambda b, s: (0, 0)),
                  vec(C),
                  pl.BlockSpec((N_RG_BLOCKS, RG_BW, 2 * RG_BW), lambda b, s: (0, 0, 0)),
                  vec(C), vec(C), vec(C),
                  pl.BlockSpec((C, D), lambda b, s: (0, 0))],
        out_specs=pl.BlockSpec((tm, D), lambda b, s: (b * ns + s, 0)),
        out_shape=jax.ShapeDtypeStruct((T, D), F32),
        scratch_shapes=[pltpu.VMEM((tm, C), F32),
                        pltpu.VMEM((tm + SUBLANES, C), F32),
                        pltpu.VMEM((tm, C), F32),
                        pltpu.VMEM((tm, C), F32),
                        pltpu.VMEM((SUBLANES, C), F32)],
        compiler_params=_cparams(("arbitrary", "arbitrary")),
        name="hawk_layer",
    )(x2d, g, in_w, cw, cb, wri, rb, ib, lam, out_w)


def _kv_proj_kernel(x_ref, g_ref, w_ref, kg_ref, cos_ref, sin_ref, k_ref, v_ref, km_ref):
    tm = x_ref.shape[0]
    d_attn = N_HEADS * HEAD_DIM
    h = _rms_norm(x_ref[...], g_ref[...]).astype(BF16)
    kv = jnp.dot(h, w_ref[...], preferred_element_type=F32)
    ks = _head_norm_rope(kv[:, :d_attn], kg_ref[...], cos_ref[...], sin_ref[...])
    lane = lax.broadcasted_iota(I32, (tm, HEAD_DIM), 1)
    for hd in range(N_HEADS):
        k_ref[0, hd] = ks[hd].astype(BF16)
        vh = kv[:, d_attn + hd * HEAD_DIM:d_attn + (hd + 1) * HEAD_DIM]
        lo = jnp.where(lane < HALF, vh, jnp.where(lane == L_LANE, 1.0, 0.0))
        hi = jnp.where(lane < HALF, pltpu.roll(vh, HALF, 1), 0.0)
        v_ref[0, hd] = jnp.concatenate([lo, hi], axis=1).astype(BF16)
        for r in range(tm // MOBA_BLOCK):
            blk = ks[hd][r * MOBA_BLOCK:(r + 1) * MOBA_BLOCK, :]
            km_ref[0, r, pl.ds(hd, 1), :] = jnp.mean(blk, axis=0, keepdims=True)


def _kv_proj(x2d, B, S, g, w, kg, cosf, sinf):
    T, D = x2d.shape
    tm = ROW_TILE
    ns = S // tm
    nb = S // MOBA_BLOCK
    rpt = tm // MOBA_BLOCK
    return pl.pallas_call(
        _kv_proj_kernel,
        grid=(B, ns),
        in_specs=[pl.BlockSpec((tm, D), lambda b, s: (b * ns + s, 0)),
                  pl.BlockSpec((1, D), lambda b, s: (0, 0)),
                  pl.BlockSpec((D, 2 * N_HEADS * HEAD_DIM), lambda b, s: (0, 0)),
                  pl.BlockSpec((1, HEAD_DIM), lambda b, s: (0, 0)),
                  pl.BlockSpec((tm, HEAD_DIM), lambda b, s: (s, 0)),
                  pl.BlockSpec((tm, HEAD_DIM), lambda b, s: (s, 0))],
        out_specs=[pl.BlockSpec((1, N_HEADS, tm, HEAD_DIM), lambda b, s: (b, 0, s, 0)),
                   pl.BlockSpec((1, N_HEADS, tm, VA_W), lambda b, s: (b, 0, s, 0)),
                   pl.BlockSpec((1, rpt, N_HEADS, HEAD_DIM), lambda b, s: (b, s, 0, 0))],
        out_shape=[jax.ShapeDtypeStruct((B, N_HEADS, S, HEAD_DIM), BF16),
                   jax.ShapeDtypeStruct((B, N_HEADS, S, VA_W), BF16),
                   jax.ShapeDtypeStruct((B, nb, N_HEADS, HEAD_DIM), F32)],
        compiler_params=_cparams(("arbitrary", "arbitrary")),
        name="kv_proj",
    )(x2d, g, w, kg, cosf, sinf)


def _q_proj_kernel(x_ref, g_ref, w_ref, qg_ref, cos_ref, sin_ref, q_ref, sg_ref):
    d_attn = N_HEADS * HEAD_DIM
    h = _rms_norm(x_ref[...], g_ref[...]).astype(BF16)
    u = jnp.dot(h, w_ref[...], preferred_element_type=F32)
    qs = _head_norm_rope(u[:, :d_attn], qg_ref[...], cos_ref[...], sin_ref[...])
    for hd in range(N_HEADS):
        q_ref[0, hd] = qs[hd]
    sg_ref[...] = _silu(u[:, d_attn:])


def _q_proj(x2d, B, S, g, w, qg, cosf, sinf):
    T, D = x2d.shape
    tm = ROW_TILE
    ns = S // tm
    d_attn = N_HEADS * HEAD_DIM
    return pl.pallas_call(
        _q_proj_kernel,
        grid=(B, ns),
        in_specs=[pl.BlockSpec((tm, D), lambda b, s: (b * ns + s, 0)),
                  pl.BlockSpec((1, D), lambda b, s: (0, 0)),
                  pl.BlockSpec((D, 2 * d_attn), lambda b, s: (0, 0)),
                  pl.BlockSpec((1, HEAD_DIM), lambda b, s: (0, 0)),
                  pl.BlockSpec((tm, HEAD_DIM), lambda b, s: (s, 0)),
                  pl.BlockSpec((tm, HEAD_DIM), lambda b, s: (s, 0))],
        out_specs=[pl.BlockSpec((1, N_HEADS, tm, HEAD_DIM), lambda b, s: (b, 0, s, 0)),
                   pl.BlockSpec((tm, d_attn), lambda b, s: (b * ns + s, 0))],
        out_shape=[jax.ShapeDtypeStruct((B, N_HEADS, S, HEAD_DIM), F32),
                   jax.ShapeDtypeStruct((T, d_attn), F32)],
        compiler_params=_cparams(("arbitrary", "arbitrary")),
        name="q_proj",
    )(x2d, g, w, qg, cosf, sinf)


def _moba_route_kernel(q_ref, km_ref, rt_ref, cnt_ref, run_ref):
    step = pl.program_id(2)
    nb = km_ref.shape[2]

    @pl.when(step == 0)
    def _():
        run_ref[...] = jnp.zeros_like(run_ref)

    km = km_ref[0, 0].astype(BF16)
    blk = lax.broadcasted_iota(I32, (nb, MOBA_BLOCK), 0)
    qa = lax.broadcasted_iota(I32, (MOBA_BLOCK, MOBA_BLOCK), 0)
    qb = lax.broadcasted_iota(I32, (MOBA_BLOCK, MOBA_BLOCK), 1)
    earlier = jnp.where(qa < qb, 1.0, 0.0).astype(BF16)
    run = run_ref[...]
    for u in range(ROUTE_TILES):
        j = step * ROUTE_TILES + u
        g = _nt_dot(km, q_ref[0, 0, pl.ds(u * MOBA_BLOCK, MOBA_BLOCK), :].astype(BF16))
        g = jnp.where(blk < j, g, -jnp.inf)
        hits, ids = [], []
        for _ in range(MOBA_TOPK):
            mx = jnp.max(g, axis=0, keepdims=True)
            idx = jnp.min(jnp.where(g == mx, blk, nb), axis=0, keepdims=True)
            hit = blk == idx
            ok = mx > -jnp.inf
            hits.append(jnp.where(hit & ok, 1.0, 0.0))
            ids.append(jnp.where(ok, idx, -1))
            g = jnp.where(hit, -jnp.inf, g)
        sel = hits[0] + hits[1] + hits[2]

        rank = jnp.dot(sel.astype(BF16), earlier, preferred_element_type=F32)
        dest = run + rank
        run = run + jnp.sum(sel, axis=1, keepdims=True)
        for r in range(MOBA_TOPK):
            loc = jnp.sum(hits[r] * dest, axis=0, keepdims=True).astype(I32)
            rt_ref[u, pl.ds(r, 1), :] = jnp.where(ids[r] >= 0, ids[r] * ROUTE_BLK_UNIT + loc, -1)
        rt_ref[u, pl.ds(MOBA_TOPK, SUBLANES - MOBA_TOPK), :] = jnp.full(
            (SUBLANES - MOBA_TOPK, MOBA_BLOCK), -1, I32)
    run_ref[...] = run
    cnt_ref[0] = run.astype(I32)


def _moba_route(q, kmean, b):
    _, H, S, Dh = q.shape
    nb = S // MOBA_BLOCK
    rows = ROUTE_TILES * MOBA_BLOCK
    nsteps = nb // ROUTE_TILES
    return pl.pallas_call(
        _moba_route_kernel,
        grid=(1, H, nsteps),
        in_specs=[pl.BlockSpec((1, 1, rows, Dh), lambda _, h, j: (b, h, j, 0)),
                  pl.BlockSpec((1, 1, nb, Dh), lambda _, h, j: (b, h, 0, 0))],
        out_specs=[pl.BlockSpec((ROUTE_TILES, SUBLANES, MOBA_BLOCK), lambda _, h, j: (h * nsteps + j, 0, 0)),
                   pl.BlockSpec((1, nb, 1), lambda _, h, j: (h, 0, 0))],
        out_shape=[jax.ShapeDtypeStruct((H * nb, SUBLANES, MOBA_BLOCK), I32),
                   jax.ShapeDtypeStruct((H, nb, 1), I32)],
        scratch_shapes=[pltpu.VMEM((nb, 1), F32)],
        compiler_params=_cparams(("arbitrary", "arbitrary", "arbitrary")),
        name="moba_route",
    )(q, kmean)


def _moba_layout_kernel(cnt_ref, rt_ref, pos_ref, tblk_ref, tvalid_ref, nt_ref, base_ref, *, max_tiles):
    g = pl.program_id(0)
    nb = rt_ref.shape[0]
    slab_rows = max_tiles * GROUP_TILE
    null_row0 = (max_tiles - 1) * GROUP_TILE

    def per_block(n, first_tile):
        count = cnt_ref[g * nb + n]
        tiles = (count + (GROUP_TILE - 1)) // GROUP_TILE
        base_ref[n] = first_tile * GROUP_TILE

        def per_tile(i, carry):
            tblk_ref[g * max_tiles + first_tile + i] = n
            tvalid_ref[g * max_tiles + first_tile + i] = jnp.minimum(count - i * GROUP_TILE, GROUP_TILE)
            return carry

        lax.fori_loop(0, tiles, per_tile, 0)
        return first_tile + tiles

    used = lax.fori_loop(0, nb, per_block, 0)
    nt_ref[g] = used

    def unused_tile(t, carry):
        tblk_ref[g * max_tiles + t] = nb - 1
        tvalid_ref[g * max_tiles + t] = 0
        return carry

    lax.fori_loop(used, max_tiles, unused_tile, 0)

    code = rt_ref[...].reshape(nb * SUBLANES, MOBA_BLOCK)
    blk = lax.shift_right_arithmetic(code, ROUTE_BLK_SHIFT)
    base = jnp.zeros_like(code)
    for n in range(nb):
        base = jnp.where(blk == n, base_ref[n], base)
    lane = lax.broadcasted_iota(I32, code.shape, 1)
    row = jnp.where(code >= 0, base + (code & (ROUTE_BLK_UNIT - 1)), null_row0 + lane)
    pos_ref[...] = (row + g * slab_rows).reshape(pos_ref.shape)


def _moba_layout(rt, counts, nb, max_tiles):
    G = rt.shape[0] // nb
    smem = pl.BlockSpec(memory_space=pltpu.SMEM)
    return pl.pallas_call(
        functools.partial(_moba_layout_kernel, max_tiles=max_tiles),
        grid_spec=pltpu.PrefetchScalarGridSpec(
            num_scalar_prefetch=1,
            grid=(G,),
            in_specs=[pl.BlockSpec((nb, SUBLANES, MOBA_BLOCK), lambda g, cnt: (g, 0, 0))],
            out_specs=[pl.BlockSpec((nb, SUBLANES, MOBA_BLOCK), lambda g, cnt: (g, 0, 0)), smem, smem, smem],
            scratch_shapes=[pltpu.SMEM((nb,), I32)],
        ),
        out_shape=[jax.ShapeDtypeStruct((G * nb, SUBLANES, MOBA_BLOCK), I32),
                   jax.ShapeDtypeStruct((G * max_tiles,), I32),
                   jax.ShapeDtypeStruct((G * max_tiles,), I32),
                   jax.ShapeDtypeStruct((G,), I32)],
        compiler_params=_cparams(("arbitrary",)),
        name="moba_layout",
    )(counts, rt)


def _sc_mesh():
    return plsc.VectorSubcoreMesh(core_axis_name="c", subcore_axis_name="s")


def _sc_load_indices(pos_hbm, idx_v, t0, n_tiles):
    for c in range(SC_HALVES):
        pltpu.sync_copy(pos_hbm.at[pl.ds(t0, n_tiles), pl.ds(0, MOBA_TOPK), pl.ds(c * IDX_CHUNK, IDX_CHUNK)],
                        idx_v.at[c])


def _sc_scatter_rows(src, src_tile0, pos, n_out_rows):
    D = src.shape[1]
    n_tiles = pos.shape[0]
    per_w = n_tiles // SC_WORKERS
    assert per_w * SC_WORKERS == n_tiles

    @functools.partial(
        pl.kernel, mesh=_sc_mesh(),
        out_type=jax.ShapeDtypeStruct((n_out_rows, D), src.dtype),
        scratch_types=[pltpu.VMEM((SC_HALVES, per_w, MOBA_TOPK, IDX_CHUNK), I32),
                       pltpu.VMEM((SC_HALVES, IDX_CHUNK, D), src.dtype),
                       pltpu.SemaphoreType.DMA((SC_HALVES,)),
                       pltpu.SemaphoreType.DMA((SC_HALVES,))],
        name="sc_scatter_rows",
    )
    def k(src_hbm, pos_hbm, out_hbm, idx_v, rows_v, load_sem, scat_sem):
        t0 = (lax.axis_index("s") * SC_CORES + lax.axis_index("c")) * per_w
        _sc_load_indices(pos_hbm, idx_v, t0, per_w)

        def load(i, c):
            row0 = pl.multiple_of((src_tile0 + t0 + i) * MOBA_BLOCK + c * IDX_CHUNK, IDX_CHUNK)
            return pltpu.make_async_copy(src_hbm.at[pl.ds(row0, IDX_CHUNK)], rows_v.at[c], load_sem.at[c])

        def scatters(i, c):
            return [pltpu.make_async_copy(rows_v.at[c], out_hbm.at[idx_v.at[c, i, r]], scat_sem.at[c])
                    for r in range(MOBA_TOPK)]

        load(0, 0).start()

        @pl.loop(0, per_w)
        def _(i):
            for c in range(SC_HALVES):
                pi, pc = (i, 0) if c else (i - 1, 1)
                ni, nc = (i + 1, 0) if c else (i, 1)
                load(i, c).wait()
                for cp in scatters(i, c):
                    cp.start()

                @pl.when(pi >= 0)
                def _():
                    for cp in scatters(pi, pc):
                        cp.wait()

                @pl.when(ni < per_w)
                def _():
                    load(ni, nc).start()

        for cp in scatters(per_w - 1, SC_HALVES - 1):
            cp.wait()

    return k(src, pos)


def _sc_gather_rows(table, pos):
    R, D = table.shape
    n_tiles = pos.shape[0]
    N = n_tiles * MOBA_BLOCK
    per_w = n_tiles // SC_WORKERS
    assert per_w * SC_WORKERS == n_tiles

    items = SC_HALVES * MOBA_TOPK
    n_buf = 2
    assert items % n_buf == 0

    @functools.partial(
        pl.kernel, mesh=_sc_mesh(),
        out_type=jax.ShapeDtypeStruct((MOBA_TOPK, N, D), table.dtype),
        scratch_types=[pltpu.VMEM((SC_HALVES, per_w, MOBA_TOPK, IDX_CHUNK), I32),
                       pltpu.VMEM((n_buf, IDX_CHUNK, D), table.dtype),
                       pltpu.SemaphoreType.DMA((n_buf,)),
                       pltpu.SemaphoreType.DMA((n_buf,))],
        name="sc_gather_rows",
    )
    def k(table_hbm, pos_hbm, out_hbm, idx_v, rows_v, gat_sem, put_sem):
        t0 = (lax.axis_index("s") * SC_CORES + lax.axis_index("c")) * per_w
        _sc_load_indices(pos_hbm, idx_v, t0, per_w)

        def gather(i, j):
            c, r = divmod(j, MOBA_TOPK)
            return pltpu.make_async_copy(table_hbm.at[idx_v.at[c, i, r]], rows_v.at[j % n_buf], gat_sem.at[j % n_buf])

        def put(i, j):
            c, r = divmod(j, MOBA_TOPK)
            row0 = pl.multiple_of((t0 + i) * MOBA_BLOCK + c * IDX_CHUNK, IDX_CHUNK)
            return pltpu.make_async_copy(rows_v.at[j % n_buf], out_hbm.at[r, pl.ds(row0, IDX_CHUNK)],
                                         put_sem.at[j % n_buf])

        gather(0, 0).start()

        @pl.loop(0, per_w)
        def _(i):
            for j in range(items):
                pi, pj = (i, j - 1) if j else (i - 1, items - 1)
                ni, nj = (i, j + 1) if j + 1 < items else (i + 1, 0)
                gather(i, j).wait()
                put(i, j).start()

                @pl.when(pi >= 0)
                def _():
                    put(pi, pj).wait()

                @pl.when(ni < per_w)
                def _():
                    gather(ni, nj).start()

        put(per_w - 1, items - 1).wait()

    return k(table, pos)


def _bits(x):
    return lax.bitcast_convert_type(x, I32)


def _pack_partial(part, m, live):
    lane = lax.broadcasted_iota(I32, (part.shape[0], HEAD_DIM), 1)
    lo = _bits(part[:, :HEAD_DIM]) + BF16_HALF_ULP
    hi = _bits(jnp.where(lane == L_LANE, m, part[:, HEAD_DIM:])) + BF16_HALF_ULP
    word = (hi & HI16_MASK) | lax.shift_right_logical(lo, 16)
    return jnp.where(live, word, jnp.where(lane == L_LANE, NEG_INF_BITS & HI16_MASK, 0))


def _unpack_partial(word):
    lo = lax.bitcast_convert_type(lax.shift_left(word, 16), F32)
    hi = lax.bitcast_convert_type(word & HI16_MASK, F32)
    return lo, hi


def _moba_group_kernel(tblk_ref, tvalid_ref, nt_ref, qg_ref, k_ref, v_ref, o_ref, *, max_tiles):
    g, step = pl.program_id(0), pl.program_id(1)
    t0 = g * max_tiles + step * TILES_PER_STEP
    scale = HEAD_DIM ** -0.5

    @pl.when((tvalid_ref[t0] > 0) | (step == pl.num_programs(1) - 1))
    def _():
        row = lax.broadcasted_iota(I32, (GROUP_TILE, 1), 0)
        for u in range(TILES_PER_STEP):
            rows = pl.ds(u * GROUP_TILE, GROUP_TILE)
            live = row < tvalid_ref[t0 + u]
            q = jnp.where(live, qg_ref[rows, :], 0.0).astype(BF16)
            n0 = pl.multiple_of(tblk_ref[t0 + u] * MOBA_BLOCK, MOBA_BLOCK)
            s = _nt_dot(q, k_ref[0, 0, pl.ds(n0, MOBA_BLOCK), :])
            m = (jnp.max(s, axis=1, keepdims=True) * scale).astype(BF16).astype(F32)
            p = jnp.exp2(s * (scale * LOG2_E) - m * LOG2_E).astype(BF16)
            part = jnp.dot(p, v_ref[0, 0, pl.ds(n0, MOBA_BLOCK), :], preferred_element_type=F32)
            o_ref[rows, :] = _pack_partial(part, m, live)


def _moba_group(qg, k, vaug, b, tblk, tvalid, ntiles, max_tiles):
    _, H, S, Dh = k.shape
    G = H
    n_steps = max_tiles // TILES_PER_STEP
    step_rows = TILES_PER_STEP * GROUP_TILE

    def tile_idx(g, s, tblk, tvalid, nt):
        return (g * n_steps + jnp.where(s * TILES_PER_STEP < nt[g], s, n_steps - 1), 0)

    return pl.pallas_call(
        functools.partial(_moba_group_kernel, max_tiles=max_tiles),
        grid_spec=pltpu.PrefetchScalarGridSpec(
            num_scalar_prefetch=3,
            grid=(G, n_steps),
            in_specs=[pl.BlockSpec((step_rows, Dh), tile_idx),
                      pl.BlockSpec((1, 1, S, Dh), lambda g, s, *_: (b, g, 0, 0)),
                      pl.BlockSpec((1, 1, S, VA_W), lambda g, s, *_: (b, g, 0, 0))],
            out_specs=pl.BlockSpec((step_rows, LANES), tile_idx),
        ),
        out_shape=jax.ShapeDtypeStruct((G * max_tiles * GROUP_TILE, LANES), I32),
        compiler_params=_cparams(("arbitrary", "arbitrary")),
        name="moba_group",
    )(tblk, tvalid, ntiles, qg, k, vaug)


def _moba_merge_kernel(*refs, n_skip):
    q_ref, k_ref, v_ref = refs[n_skip:n_skip + 3]
    og_refs, o_ref = refs[n_skip + 3:-1], refs[-1]
    scale = HEAD_DIM ** -0.5
    qi = lax.broadcasted_iota(I32, (MOBA_BLOCK, MOBA_BLOCK), 0)
    ki = lax.broadcasted_iota(I32, (MOBA_BLOCK, MOBA_BLOCK), 1)
    lane = lax.broadcasted_iota(I32, (MOBA_BLOCK, HEAD_DIM), 1)
    spread = jnp.where(lax.broadcasted_iota(I32, (HEAD_DIM, HEAD_DIM), 0) == L_LANE, 1.0, 0.0).astype(BF16)
    for u, og_ref in enumerate(og_refs):
        q = q_ref[0, u].astype(BF16)
        s = _nt_dot(q, k_ref[0, u])
        s = jnp.where(ki <= qi, s, NEG_INF)
        m_raw = jnp.max(s, axis=1, keepdims=True)
        p = jnp.exp2((s - m_raw) * (scale * LOG2_E)).astype(BF16)
        own = jnp.dot(p, v_ref[0, u], preferred_element_type=F32)
        parts = [_unpack_partial(og_ref[r]) for r in range(MOBA_TOPK)]
        ms = [jnp.dot(jnp.where(lane == L_LANE, hi, 0.0).astype(BF16), spread, preferred_element_type=F32)
              for _, hi in parts]
        m_own = jnp.broadcast_to(m_raw * scale, lane.shape)
        m_all = m_own
        for m in ms:
            m_all = jnp.maximum(m_all, m)
        w_own = jnp.exp2((m_own - m_all) * LOG2_E)
        tot_lo = w_own * own[:, :HEAD_DIM]
        tot_hi = w_own * own[:, HEAD_DIM:]
        for (lo, hi), m in zip(parts, ms):
            w = jnp.exp2((m - m_all) * LOG2_E)
            tot_lo = tot_lo + w * lo
            tot_hi = tot_hi + w * hi
        acc = jnp.where(lane < HALF, tot_lo, pltpu.roll(tot_hi, HALF, 1))
        o_ref[:, u * HEAD_DIM:(u + 1) * HEAD_DIM] = acc / tot_lo[:, L_LANE:L_LANE + 1]


def _moba_merge(q, k, vaug, og, b, o_prev):
    B, H, S, Dh = q.shape
    nb = S // MOBA_BLOCK
    hps = HEADS_PER_STEP
    og_spec = lambda u: pl.BlockSpec((MOBA_TOPK, MOBA_BLOCK, LANES),
                                     lambda _, hp, j: (0, (hp * hps + u) * nb + j, 0))
    prev_specs = [] if o_prev is None else [pl.BlockSpec(memory_space=pl.ANY)]
    prev_args = [] if o_prev is None else [o_prev]
    return pl.pallas_call(
        functools.partial(_moba_merge_kernel, n_skip=len(prev_args)),
        grid=(1, H // hps, nb),
        in_specs=prev_specs
                 + [pl.BlockSpec((1, hps, MOBA_BLOCK, Dh), lambda _, hp, j: (b, hp, j, 0)),
                    pl.BlockSpec((1, hps, MOBA_BLOCK, Dh), lambda _, hp, j: (b, hp, j, 0)),
                    pl.BlockSpec((1, hps, MOBA_BLOCK, VA_W), lambda _, hp, j: (b, hp, j, 0))]
                 + [og_spec(u) for u in range(hps)],
        out_specs=pl.BlockSpec((MOBA_BLOCK, hps * Dh), lambda _, hp, j: (b * nb + j, hp)),
        out_shape=jax.ShapeDtypeStruct((B * S, H * Dh), F32),
        input_output_aliases={} if o_prev is None else {0: 0},
        compiler_params=_cparams(("arbitrary", "arbitrary", "arbitrary")),
        name="moba_merge",
    )(*prev_args, q, k, vaug, *([og] * hps))


def _moba_attention(q, k, vaug, kmean):
    B, H, S, Dh = q.shape
    nb = S // MOBA_BLOCK
    max_tiles = (MOBA_TOPK * S) // GROUP_TILE + nb + 1
    max_tiles = -(-max_tiles // TILES_PER_STEP) * TILES_PER_STEP
    slab_rows = max_tiles * GROUP_TILE

    q_rows = q.reshape(B * H * S, Dh)
    o = None
    for b in range(B):
        rt, counts = _moba_route(q, kmean, b)
        pos, tblk, tvalid, ntiles = _moba_layout(rt, counts.reshape(-1), nb, max_tiles)
        qg = _sc_scatter_rows(q_rows, b * H * nb, pos, H * slab_rows)
        parts = _moba_group(qg, k, vaug, b, tblk, tvalid, ntiles, max_tiles)
        og = _sc_gather_rows(parts, pos)
        o = _moba_merge(q, k, vaug, og, b, o)
    return o


def _gated_proj_residual_kernel(o_ref, sg_ref, w_ref, x_ref, out_ref):
    y = (o_ref[...] * sg_ref[...]).astype(BF16)
    out_ref[...] = x_ref[...] + jnp.dot(y, w_ref[...], preferred_element_type=F32)


def _gated_proj_residual(o, sg, w, x2d):
    T, D = x2d.shape
    K = o.shape[1]
    return pl.pallas_call(
        _gated_proj_residual_kernel,
        grid=(T // ROW_TILE,),
        in_specs=[pl.BlockSpec((ROW_TILE, K), lambda i: (i, 0)),
                  pl.BlockSpec((ROW_TILE, K), lambda i: (i, 0)),
                  pl.BlockSpec((K, D), lambda i: (0, 0)),
                  pl.BlockSpec((ROW_TILE, D), lambda i: (i, 0))],
        out_specs=pl.BlockSpec((ROW_TILE, D), lambda i: (i, 0)),
        out_shape=jax.ShapeDtypeStruct((T, D), F32),
        compiler_params=_cparams(("arbitrary",)),
        name="moba_out",
    )(o, sg, w, x2d)


def _rope_tables(S):
    inv = np.float32(ROPE_THETA) ** (-np.arange(HALF, dtype=np.float32) / np.float32(HALF))
    ang = np.arange(S, dtype=np.float32)[:, None] * inv[None, :].astype(np.float32)
    cos, sin = np.cos(ang).astype(np.float32), np.sin(ang).astype(np.float32)
    return (jnp.asarray(np.concatenate([cos, cos], axis=-1)),
            jnp.asarray(np.concatenate([-sin, sin], axis=-1)))


def kernel(x, a_norm, a_in_w, a_conv_w, a_conv_b, a_r_w, a_r_b, a_i_w, a_i_b, a_lambda, a_out_w,
           kv_norm, kv_w, k_norm, b_norm, b_in_w, q_norm, b_out_w):
    B, S, D = x.shape
    assert S % ROW_TILE == 0 and ROW_TILE % MOBA_BLOCK == 0
    assert D == N_RG_BLOCKS * RG_BW == N_HEADS * HEAD_DIM
    assert (B * N_HEADS * S) % (IDX_CHUNK * SC_WORKERS) == 0
    xs = x.reshape(B * S, D)
    row = lambda v: v.reshape(1, -1)

    for l in range(a_in_w.shape[0]):
        wri = jnp.concatenate([a_r_w[l], a_i_w[l]], axis=-1).astype(BF16)
        xs = _hawk_layer(xs, B, S, row(a_norm[l]), a_in_w[l].astype(BF16), a_conv_w[l], row(a_conv_b[l]), wri,
                         row(a_r_b[l]), row(a_i_b[l]), row(a_lambda[l]), a_out_w[l].astype(BF16))

    cosf, sinf = _rope_tables(S)
    k, vaug, kmean = _kv_proj(xs, B, S, row(kv_norm), kv_w.astype(BF16), row(k_norm), cosf, sinf)
    kmean = jnp.transpose(kmean, (0, 2, 1, 3))

    for jl in range(b_in_w.shape[0]):
        q, sg = _q_proj(xs, B, S, row(b_norm[jl]), b_in_w[jl].astype(BF16), row(q_norm[jl]), cosf, sinf)
        o = _moba_attention(q, k, vaug, kmean)
        xs = _gated_proj_residual(o, sg, b_out_w[jl].astype(BF16), xs)
    return xs.reshape(B, S, D)
```

```python
import functools

import jax
import jax.numpy as jnp
import numpy as np
from jax import lax
from jax.experimental import pallas as pl
from jax.experimental.pallas import tpu as pltpu
from jax.experimental.pallas import tpu_sc as plsc

N_HEADS = 8
HEAD_DIM = 128
MOBA_BLOCK = 256
MOBA_TOPK = 3
CONV_WIDTH = 4
N_RG_BLOCKS = 8
RG_BW = 128
RG_C = 8.0
ROPE_THETA = 10000.0
EPS = 1e-6
NEG_INF = -1e30
LOG2_E = 1.4426950408889634

SUBLANES = 8
LANES = 128
ROW_TILE = 512
SUB_ROWS = 128
VMEM_LIMIT = 56 * 1024 * 1024

GROUP_TILE = MOBA_BLOCK
TILES_PER_STEP = 8
HEADS_PER_STEP = 8
ROUTE_TILES = 4
ROUTE_BLK_SHIFT = 16
ROUTE_BLK_UNIT = 1 << ROUTE_BLK_SHIFT
HALF = HEAD_DIM // 2
VA_W = 2 * HEAD_DIM
L_LANE = HALF
HI16_MASK = -65536
BF16_HALF_ULP = 0x8000
NEG_INF_BITS = int(np.float32(NEG_INF).view(np.int32))

SC_CORES = 2
SC_SUBCORES = 16
SC_WORKERS = SC_CORES * SC_SUBCORES
IDX_CHUNK = 128
SC_HALVES = MOBA_BLOCK // IDX_CHUNK

F32 = jnp.float32
BF16 = jnp.bfloat16
I32 = jnp.int32


def _cparams(sem):
    return pltpu.CompilerParams(dimension_semantics=sem, vmem_limit_bytes=VMEM_LIMIT)


def _rms_norm(x, g):
    ms = jnp.mean(x * x, axis=-1, keepdims=True)
    return (x * lax.rsqrt(ms + EPS)) * g


def _silu(x):
    hx = 0.5 * x
    return hx * jnp.tanh(hx) + hx


def _sigmoid(x):
    return 0.5 * jnp.tanh(0.5 * x) + 0.5


def _nt_dot(a, b):
    return lax.dot_general(a, b, (((1,), (1,)), ((), ())), preferred_element_type=F32)


def _head_norm_rope(z, g, cosf, sinf):
    outs = []
    for h in range(N_HEADS):
        zh = z[:, h * HEAD_DIM:(h + 1) * HEAD_DIM]
        zh = _rms_norm(zh, g)
        outs.append(zh * cosf + pltpu.roll(zh, HEAD_DIM // 2, 1) * sinf)
    return outs


def _hawk_kernel(x_ref, g_ref, inw_ref, cw_ref, cb_ref, wri_ref, rb_ref, ib_ref, lam_ref, outw_ref,
                 o_ref, gate_ref, xpad_ref, a_ref, b_ref, hc_ref):
    tm, C = gate_ref.shape
    s = pl.program_id(1)

    @pl.when(s == 0)
    def _():
        xpad_ref[pl.ds(0, SUBLANES), :] = jnp.zeros((SUBLANES, xpad_ref.shape[1]), F32)
        hc_ref[...] = jnp.zeros_like(hc_ref)

    h = _rms_norm(x_ref[...], g_ref[...]).astype(BF16)
    gate_ref[...] = jnp.dot(h, inw_ref[:, C:], preferred_element_type=F32)

    xpad_ref[pl.ds(SUBLANES, tm), :] = jnp.dot(h, inw_ref[:, :C], preferred_element_type=F32)
    cw = cw_ref[...]
    xc = xpad_ref[pl.ds(SUBLANES - (CONV_WIDTH - 1), tm), :] * cw[0:1, :]
    for k in range(1, CONV_WIDTH):
        xc = xc + xpad_ref[pl.ds(SUBLANES - (CONV_WIDTH - 1) + k, tm), :] * cw[k:k + 1, :]
    xc = xc + cb_ref[...]
    xpad_ref[pl.ds(0, SUBLANES), :] = xpad_ref[pl.ds(tm, SUBLANES), :]

    lam = lam_ref[...]
    sp = jnp.maximum(-lam, 0.0) + jnp.log1p(jnp.exp(-jnp.abs(lam)))
    log2a_per_r = (-RG_C * LOG2_E) * sp
    xc16 = xc.astype(BF16)
    for g in range(N_RG_BLOCKS):
        lo = g * RG_BW
        z = jnp.dot(xc16[:, lo:lo + RG_BW], wri_ref[g], preferred_element_type=F32)
        r = _sigmoid(z[:, :RG_BW] + rb_ref[:, lo:lo + RG_BW])
        i = _sigmoid(z[:, RG_BW:] + ib_ref[:, lo:lo + RG_BW])
        a = jnp.exp2(r * log2a_per_r[:, lo:lo + RG_BW])
        mult = jnp.sqrt(1.0 - a * a)
        a_ref[:, lo:lo + RG_BW] = a
        b_ref[:, lo:lo + RG_BW] = mult * (i * xc[:, lo:lo + RG_BW])

    row = lax.broadcasted_iota(I32, (SUBLANES, a_ref.shape[1]), 0)

    def scan_body(c, hc):
        r0 = pl.multiple_of(c * SUBLANES, SUBLANES)
        a = a_ref[pl.ds(r0, SUBLANES), :]
        b = b_ref[pl.ds(r0, SUBLANES), :]
        for k in (1, 2, 4):
            keep = row >= k
            b = jnp.where(keep, a * pltpu.roll(b, k, 0) + b, b)
            a = jnp.where(keep, a * pltpu.roll(a, k, 0), a)
        h = a * hc + b
        b_ref[pl.ds(r0, SUBLANES), :] = h
        return h[SUBLANES - 1:SUBLANES, :]

    hc = lax.fori_loop(0, tm // SUBLANES, scan_body, hc_ref[0:1, :])
    hc_ref[0:1, :] = hc

    y = (b_ref[...] * _silu(gate_ref[...])).astype(BF16)
    o_ref[...] = x_ref[...] + jnp.dot(y, outw_ref[...], preferred_element_type=F32)


def _hawk_layer(x2d, B, S, g, in_w, cw, cb, wri, rb, ib, lam, out_w):
    T, D = x2d.shape
    C = in_w.shape[1] // 2
    tm = ROW_TILE
    ns = S // tm
    vec = lambda n: pl.BlockSpec((1, n), lambda b, s: (0, 0))
    return pl.pallas_call(
        _hawk_kernel,
        grid=(B, ns),
        in_specs=[pl.BlockSpec((tm, D), lambda b, s: (b * ns + s, 0)),
                  vec(D),
                  pl.BlockSpec((D, 2 * C), lambda b, s: (0, 0)),
                  pl.BlockSpec((CONV_WIDTH, C), lambda b, s: (0, 0)),
                  vec(C),
                  pl.BlockSpec((N_RG_BLOCKS, RG_BW, 2 * RG_BW), lambda b, s: (0, 0, 0)),
                  vec(C), vec(C), vec(C),
                  pl.BlockSpec((C, D), lambda b, s: (0, 0))],
        out_specs=pl.BlockSpec((tm, D), lambda b, s: (b * ns + s, 0)),
        out_shape=jax.ShapeDtypeStruct((T, D), F32),
        scratch_shapes=[pltpu.VMEM((tm, C), F32),
                        pltpu.VMEM((tm + SUBLANES, C), F32),
                        pltpu.VMEM((tm, C), F32),
                        pltpu.VMEM((tm, C), F32),
                        pltpu.VMEM((SUBLANES, C), F32)],
        compiler_params=_cparams(("arbitrary", "arbitrary")),
        name="hawk_layer",
    )(x2d, g, in_w, cw, cb, wri, rb, ib, lam, out_w)


def _kv_proj_kernel(x_ref, g_ref, w_ref, kg_ref, cos_ref, sin_ref, k_ref, v_ref, km_ref):
    tm = x_ref.shape[0]
    d_attn = N_HEADS * HEAD_DIM
    lane = lax.broadcasted_iota(I32, (SUB_ROWS, HEAD_DIM), 1)
    ksum = {}
    for t in range(tm // SUB_ROWS):
        rows = pl.ds(t * SUB_ROWS, SUB_ROWS)
        h = _rms_norm(x_ref[rows, :], g_ref[...]).astype(BF16)
        kv = jnp.dot(h, w_ref[...], preferred_element_type=F32)
        ks = _head_norm_rope(kv[:, :d_attn], kg_ref[...], cos_ref[rows, :], sin_ref[rows, :])
        for hd in range(N_HEADS):
            k_ref[0, hd, rows, :] = ks[hd].astype(BF16)
            vh = kv[:, d_attn + hd * HEAD_DIM:d_attn + (hd + 1) * HEAD_DIM]
            lo = jnp.where(lane < HALF, vh, jnp.where(lane == L_LANE, 1.0, 0.0))
            hi = jnp.where(lane < HALF, pltpu.roll(vh, HALF, 1), 0.0)
            v_ref[0, hd, rows, :] = jnp.concatenate([lo, hi], axis=1).astype(BF16)
            key = (hd, (t * SUB_ROWS) // MOBA_BLOCK)
            part = jnp.sum(ks[hd], axis=0, keepdims=True)
            ksum[key] = part if key not in ksum else ksum[key] + part
    for (hd, r), total in ksum.items():
        km_ref[0, r, pl.ds(hd, 1), :] = total * (1.0 / MOBA_BLOCK)


def _kv_proj(x2d, B, S, g, w, kg, cosf, sinf):
    T, D = x2d.shape
    tm = ROW_TILE
    ns = S // tm
    nb = S // MOBA_BLOCK
    rpt = tm // MOBA_BLOCK
    return pl.pallas_call(
        _kv_proj_kernel,
        grid=(B, ns),
        in_specs=[pl.BlockSpec((tm, D), lambda b, s: (b * ns + s, 0)),
                  pl.BlockSpec((1, D), lambda b, s: (0, 0)),
                  pl.BlockSpec((D, 2 * N_HEADS * HEAD_DIM), lambda b, s: (0, 0)),
                  pl.BlockSpec((1, HEAD_DIM), lambda b, s: (0, 0)),
                  pl.BlockSpec((tm, HEAD_DIM), lambda b, s: (s, 0)),
                  pl.BlockSpec((tm, HEAD_DIM), lambda b, s: (s, 0))],
        out_specs=[pl.BlockSpec((1, N_HEADS, tm, HEAD_DIM), lambda b, s: (b, 0, s, 0)),
                   pl.BlockSpec((1, N_HEADS, tm, VA_W), lambda b, s: (b, 0, s, 0)),
                   pl.BlockSpec((1, rpt, N_HEADS, HEAD_DIM), lambda b, s: (b, s, 0, 0))],
        out_shape=[jax.ShapeDtypeStruct((B, N_HEADS, S, HEAD_DIM), BF16),
                   jax.ShapeDtypeStruct((B, N_HEADS, S, VA_W), BF16),
                   jax.ShapeDtypeStruct((B, nb, N_HEADS, HEAD_DIM), F32)],
        compiler_params=_cparams(("arbitrary", "arbitrary")),
        name="kv_proj",
    )(x2d, g, w, kg, cosf, sinf)


def _q_proj_kernel(x_ref, g_ref, w_ref, qg_ref, cos_ref, sin_ref, q_ref, sg_ref):
    d_attn = N_HEADS * HEAD_DIM
    for t in range(x_ref.shape[0] // SUB_ROWS):
        rows = pl.ds(t * SUB_ROWS, SUB_ROWS)
        h = _rms_norm(x_ref[rows, :], g_ref[...]).astype(BF16)
        u = jnp.dot(h, w_ref[...], preferred_element_type=F32)
        qs = _head_norm_rope(u[:, :d_attn], qg_ref[...], cos_ref[rows, :], sin_ref[rows, :])
        for hd in range(N_HEADS):
            q_ref[0, hd, rows, :] = qs[hd]
        sg_ref[rows, :] = _silu(u[:, d_attn:])


def _q_proj(x2d, B, S, g, w, qg, cosf, sinf):
    T, D = x2d.shape
    tm = ROW_TILE
    ns = S // tm
    d_attn = N_HEADS * HEAD_DIM
    return pl.pallas_call(
        _q_proj_kernel,
        grid=(B, ns),
        in_specs=[pl.BlockSpec((tm, D), lambda b, s: (b * ns + s, 0)),
                  pl.BlockSpec((1, D), lambda b, s: (0, 0)),
                  pl.BlockSpec((D, 2 * d_attn), lambda b, s: (0, 0)),
                  pl.BlockSpec((1, HEAD_DIM), lambda b, s: (0, 0)),
                  pl.BlockSpec((tm, HEAD_DIM), lambda b, s: (s, 0)),
                  pl.BlockSpec((tm, HEAD_DIM), lambda b, s: (s, 0))],
        out_specs=[pl.BlockSpec((1, N_HEADS, tm, HEAD_DIM), lambda b, s: (b, 0, s, 0)),
                   pl.BlockSpec((tm, d_attn), lambda b, s: (b * ns + s, 0))],
        out_shape=[jax.ShapeDtypeStruct((B, N_HEADS, S, HEAD_DIM), F32),
                   jax.ShapeDtypeStruct((T, d_attn), F32)],
        compiler_params=_cparams(("arbitrary", "arbitrary")),
        name="q_proj",
    )(x2d, g, w, qg, cosf, sinf)


def _moba_route_kernel(q_ref, km_ref, rt_ref, cnt_ref, run_ref):
    step = pl.program_id(2)
    nb = km_ref.shape[2]

    @pl.when(step == 0)
    def _():
        run_ref[...] = jnp.zeros_like(run_ref)

    km = km_ref[0, 0].astype(BF16)
    blk = lax.broadcasted_iota(I32, (nb, MOBA_BLOCK), 0)
    qa = lax.broadcasted_iota(I32, (MOBA_BLOCK, MOBA_BLOCK), 0)
    qb = lax.broadcasted_iota(I32, (MOBA_BLOCK, MOBA_BLOCK), 1)
    earlier = jnp.where(qa < qb, 1.0, 0.0).astype(BF16)
    run = run_ref[...]
    for u in range(ROUTE_TILES):
        j = step * ROUTE_TILES + u
        g = _nt_dot(km, q_ref[0, 0, pl.ds(u * MOBA_BLOCK, MOBA_BLOCK), :].astype(BF16))
        g = jnp.where(blk < j, g, -jnp.inf)
        hits, ids = [], []
        for _ in range(MOBA_TOPK):
            mx = jnp.max(g, axis=0, keepdims=True)
            idx = jnp.min(jnp.where(g == mx, blk, nb), axis=0, keepdims=True)
            hit = blk == idx
            ok = mx > -jnp.inf
            hits.append(jnp.where(hit & ok, 1.0, 0.0))
            ids.append(jnp.where(ok, idx, -1))
            g = jnp.where(hit, -jnp.inf, g)
        sel = hits[0] + hits[1] + hits[2]

        rank = jnp.dot(sel.astype(BF16), earlier, preferred_element_type=F32)
        dest = run + rank
        run = run + jnp.sum(sel, axis=1, keepdims=True)
        for r in range(MOBA_TOPK):
            loc = jnp.sum(hits[r] * dest, axis=0, keepdims=True).astype(I32)
            rt_ref[u, pl.ds(r, 1), :] = jnp.where(ids[r] >= 0, ids[r] * ROUTE_BLK_UNIT + loc, -1)
        rt_ref[u, pl.ds(MOBA_TOPK, SUBLANES - MOBA_TOPK), :] = jnp.full(
            (SUBLANES - MOBA_TOPK, MOBA_BLOCK), -1, I32)
    run_ref[...] = run
    cnt_ref[0] = run.astype(I32)


def _moba_route(q, kmean, b):
    _, H, S, Dh = q.shape
    nb = S // MOBA_BLOCK
    rows = ROUTE_TILES * MOBA_BLOCK
    nsteps = nb // ROUTE_TILES
    return pl.pallas_call(
        _moba_route_kernel,
        grid=(1, H, nsteps),
        in_specs=[pl.BlockSpec((1, 1, rows, Dh), lambda _, h, j: (b, h, j, 0)),
                  pl.BlockSpec((1, 1, nb, Dh), lambda _, h, j: (b, h, 0, 0))],
        out_specs=[pl.BlockSpec((ROUTE_TILES, SUBLANES, MOBA_BLOCK), lambda _, h, j: (h * nsteps + j, 0, 0)),
                   pl.BlockSpec((1, nb, 1), lambda _, h, j: (h, 0, 0))],
        out_shape=[jax.ShapeDtypeStruct((H * nb, SUBLANES, MOBA_BLOCK), I32),
                   jax.ShapeDtypeStruct((H, nb, 1), I32)],
        scratch_shapes=[pltpu.VMEM((nb, 1), F32)],
        compiler_params=_cparams(("arbitrary", "arbitrary", "arbitrary")),
        name="moba_route",
    )(q, kmean)


def _moba_layout_kernel(cnt_ref, rt_ref, pos_ref, tblk_ref, tvalid_ref, nt_ref, base_ref, *, max_tiles):
    g = pl.program_id(0)
    nb = rt_ref.shape[0]
    slab_rows = max_tiles * GROUP_TILE
    null_row0 = (max_tiles - 1) * GROUP_TILE

    def per_block(n, first_tile):
        count = cnt_ref[g * nb + n]
        tiles = (count + (GROUP_TILE - 1)) // GROUP_TILE
        base_ref[n] = first_tile * GROUP_TILE

        def per_tile(i, carry):
            tblk_ref[g * max_tiles + first_tile + i] = n
            tvalid_ref[g * max_tiles + first_tile + i] = jnp.minimum(count - i * GROUP_TILE, GROUP_TILE)
            return carry

        lax.fori_loop(0, tiles, per_tile, 0)
        return first_tile + tiles

    used = lax.fori_loop(0, nb, per_block, 0)
    nt_ref[g] = used

    def unused_tile(t, carry):
        tblk_ref[g * max_tiles + t] = nb - 1
        tvalid_ref[g * max_tiles + t] = 0
        return carry

    lax.fori_loop(used, max_tiles, unused_tile, 0)

    code = rt_ref[...].reshape(nb * SUBLANES, MOBA_BLOCK)
    blk = lax.shift_right_arithmetic(code, ROUTE_BLK_SHIFT)
    base = jnp.zeros_like(code)
    for n in range(nb):
        base = jnp.where(blk == n, base_ref[n], base)
    lane = lax.broadcasted_iota(I32, code.shape, 1)
    row = jnp.where(code >= 0, base + (code & (ROUTE_BLK_UNIT - 1)), null_row0 + lane)
    pos_ref[...] = (row + g * slab_rows).reshape(pos_ref.shape)


def _moba_layout(rt, counts, nb, max_tiles):
    G = rt.shape[0] // nb
    smem = pl.BlockSpec(memory_space=pltpu.SMEM)
    return pl.pallas_call(
        functools.partial(_moba_layout_kernel, max_tiles=max_tiles),
        grid_spec=pltpu.PrefetchScalarGridSpec(
            num_scalar_prefetch=1,
            grid=(G,),
            in_specs=[pl.BlockSpec((nb, SUBLANES, MOBA_BLOCK), lambda g, cnt: (g, 0, 0))],
            out_specs=[pl.BlockSpec((nb, SUBLANES, MOBA_BLOCK), lambda g, cnt: (g, 0, 0)), smem, smem, smem],
            scratch_shapes=[pltpu.SMEM((nb,), I32)],
        ),
        out_shape=[jax.ShapeDtypeStruct((G * nb, SUBLANES, MOBA_BLOCK), I32),
                   jax.ShapeDtypeStruct((G * max_tiles,), I32),
                   jax.ShapeDtypeStruct((G * max_tiles,), I32),
                   jax.ShapeDtypeStruct((G,), I32)],
        compiler_params=_cparams(("arbitrary",)),
        name="moba_layout",
    )(counts, rt)


def _sc_mesh():
    return plsc.VectorSubcoreMesh(core_axis_name="c", subcore_axis_name="s")


def _sc_load_indices(pos_hbm, idx_v, t0, n_tiles):
    for c in range(SC_HALVES):
        pltpu.sync_copy(pos_hbm.at[pl.ds(t0, n_tiles), pl.ds(0, MOBA_TOPK), pl.ds(c * IDX_CHUNK, IDX_CHUNK)],
                        idx_v.at[c])


def _sc_scatter_rows(src, src_tile0, pos, n_out_rows):
    D = src.shape[1]
    n_tiles = pos.shape[0]
    per_w = n_tiles // SC_WORKERS
    assert per_w * SC_WORKERS == n_tiles

    @functools.partial(
        pl.kernel, mesh=_sc_mesh(),
        out_type=jax.ShapeDtypeStruct((n_out_rows, D), src.dtype),
        scratch_types=[pltpu.VMEM((SC_HALVES, per_w, MOBA_TOPK, IDX_CHUNK), I32),
                       pltpu.VMEM((SC_HALVES, IDX_CHUNK, D), src.dtype),
                       pltpu.SemaphoreType.DMA((SC_HALVES,)),
                       pltpu.SemaphoreType.DMA((SC_HALVES,))],
        name="sc_scatter_rows",
    )
    def k(src_hbm, pos_hbm, out_hbm, idx_v, rows_v, load_sem, scat_sem):
        t0 = (lax.axis_index("s") * SC_CORES + lax.axis_index("c")) * per_w
        _sc_load_indices(pos_hbm, idx_v, t0, per_w)

        def load(i, c):
            row0 = pl.multiple_of((src_tile0 + t0 + i) * MOBA_BLOCK + c * IDX_CHUNK, IDX_CHUNK)
            return pltpu.make_async_copy(src_hbm.at[pl.ds(row0, IDX_CHUNK)], rows_v.at[c], load_sem.at[c])

        def scatters(i, c):
            return [pltpu.make_async_copy(rows_v.at[c], out_hbm.at[idx_v.at[c, i, r]], scat_sem.at[c])
                    for r in range(MOBA_TOPK)]

        load(0, 0).start()

        @pl.loop(0, per_w)
        def _(i):
            for c in range(SC_HALVES):
                pi, pc = (i, 0) if c else (i - 1, 1)
                ni, nc = (i + 1, 0) if c else (i, 1)
                load(i, c).wait()
                for cp in scatters(i, c):
                    cp.start()

                @pl.when(pi >= 0)
                def _():
                    for cp in scatters(pi, pc):
                        cp.wait()

                @pl.when(ni < per_w)
                def _():
                    load(ni, nc).start()

        for cp in scatters(per_w - 1, SC_HALVES - 1):
            cp.wait()

    return k(src, pos)


def _sc_gather_rows(table, pos):
    R, D = table.shape
    n_tiles = pos.shape[0]
    N = n_tiles * MOBA_BLOCK
    per_w = n_tiles // SC_WORKERS
    assert per_w * SC_WORKERS == n_tiles

    items = SC_HALVES * MOBA_TOPK
    n_buf = 2
    assert items % n_buf == 0

    @functools.partial(
        pl.kernel, mesh=_sc_mesh(),
        out_type=jax.ShapeDtypeStruct((MOBA_TOPK, N, D), table.dtype),
        scratch_types=[pltpu.VMEM((SC_HALVES, per_w, MOBA_TOPK, IDX_CHUNK), I32),
                       pltpu.VMEM((n_buf, IDX_CHUNK, D), table.dtype),
                       pltpu.SemaphoreType.DMA((n_buf,)),
                       pltpu.SemaphoreType.DMA((n_buf,))],
        name="sc_gather_rows",
    )
    def k(table_hbm, pos_hbm, out_hbm, idx_v, rows_v, gat_sem, put_sem):
        t0 = (lax.axis_index("s") * SC_CORES + lax.axis_index("c")) * per_w
        _sc_load_indices(pos_hbm, idx_v, t0, per_w)

        def gather(i, j):
            c, r = divmod(j, MOBA_TOPK)
            return pltpu.make_async_copy(table_hbm.at[idx_v.at[c, i, r]], rows_v.at[j % n_buf], gat_sem.at[j % n_buf])

        def put(i, j):
            c, r = divmod(j, MOBA_TOPK)
            row0 = pl.multiple_of((t0 + i) * MOBA_BLOCK + c * IDX_CHUNK, IDX_CHUNK)
            return pltpu.make_async_copy(rows_v.at[j % n_buf], out_hbm.at[r, pl.ds(row0, IDX_CHUNK)],
                                         put_sem.at[j % n_buf])

        gather(0, 0).start()

        @pl.loop(0, per_w)
        def _(i):
            for j in range(items):
                pi, pj = (i, j - 1) if j else (i - 1, items - 1)
                ni, nj = (i, j + 1) if j + 1 < items else (i + 1, 0)
                gather(i, j).wait()
                put(i, j).start()

                @pl.when(pi >= 0)
                def _():
                    put(pi, pj).wait()

                @pl.when(ni < per_w)
                def _():
                    gather(ni, nj).start()

        put(per_w - 1, items - 1).wait()

    return k(table, pos)


def _bits(x):
    return lax.bitcast_convert_type(x, I32)


def _pack_partial(part, m, live):
    lane = lax.broadcasted_iota(I32, (part.shape[0], HEAD_DIM), 1)
    lo = _bits(part[:, :HEAD_DIM]) + BF16_HALF_ULP
    hi = _bits(jnp.where(lane == L_LANE, m, part[:, HEAD_DIM:])) + BF16_HALF_ULP
    word = (hi & HI16_MASK) | lax.shift_right_logical(lo, 16)
    return jnp.where(live, word, jnp.where(lane == L_LANE, NEG_INF_BITS & HI16_MASK, 0))


def _unpack_partial(word):
    lo = lax.bitcast_convert_type(lax.shift_left(word, 16), F32)
    hi = lax.bitcast_convert_type(word & HI16_MASK, F32)
    return lo, hi


def _moba_group_kernel(tblk_ref, tvalid_ref, nt_ref, qg_ref, k_ref, v_ref, o_ref, *, max_tiles):
    g, step = pl.program_id(0), pl.program_id(1)
    t0 = g * max_tiles + step * TILES_PER_STEP
    scale = HEAD_DIM ** -0.5

    @pl.when((tvalid_ref[t0] > 0) | (step == pl.num_programs(1) - 1))
    def _():
        row = lax.broadcasted_iota(I32, (GROUP_TILE, 1), 0)
        for u in range(TILES_PER_STEP):
            rows = pl.ds(u * GROUP_TILE, GROUP_TILE)
            live = row < tvalid_ref[t0 + u]
            q = jnp.where(live, qg_ref[rows, :], 0.0).astype(BF16)
            n0 = pl.multiple_of(tblk_ref[t0 + u] * MOBA_BLOCK, MOBA_BLOCK)
            s = _nt_dot(q, k_ref[0, 0, pl.ds(n0, MOBA_BLOCK), :])
            m = (jnp.max(s, axis=1, keepdims=True) * scale).astype(BF16).astype(F32)
            p = jnp.exp2(s * (scale * LOG2_E) - m * LOG2_E).astype(BF16)
            part = jnp.dot(p, v_ref[0, 0, pl.ds(n0, MOBA_BLOCK), :], preferred_element_type=F32)
            o_ref[rows, :] = _pack_partial(part, m, live)


def _moba_group(qg, k, vaug, b, tblk, tvalid, ntiles, max_tiles):
    _, H, S, Dh = k.shape
    G = H
    n_steps = max_tiles // TILES_PER_STEP
    step_rows = TILES_PER_STEP * GROUP_TILE

    def tile_idx(g, s, tblk, tvalid, nt):
        return (g * n_steps + jnp.where(s * TILES_PER_STEP < nt[g], s, n_steps - 1), 0)

    return pl.pallas_call(
        functools.partial(_moba_group_kernel, max_tiles=max_tiles),
        grid_spec=pltpu.PrefetchScalarGridSpec(
            num_scalar_prefetch=3,
            grid=(G, n_steps),
            in_specs=[pl.BlockSpec((step_rows, Dh), tile_idx),
                      pl.BlockSpec((1, 1, S, Dh), lambda g, s, *_: (b, g, 0, 0)),
                      pl.BlockSpec((1, 1, S, VA_W), lambda g, s, *_: (b, g, 0, 0))],
            out_specs=pl.BlockSpec((step_rows, LANES), tile_idx),
        ),
        out_shape=jax.ShapeDtypeStruct((G * max_tiles * GROUP_TILE, LANES), I32),
        compiler_params=_cparams(("arbitrary", "arbitrary")),
        name="moba_group",
    )(tblk, tvalid, ntiles, qg, k, vaug)


def _moba_merge_kernel(*refs, n_skip):
    q_ref, k_ref, v_ref, sg_ref = refs[n_skip:n_skip + 4]
    og_refs, o_ref = refs[n_skip + 4:-1], refs[-1]
    scale = HEAD_DIM ** -0.5
    qi = lax.broadcasted_iota(I32, (MOBA_BLOCK, MOBA_BLOCK), 0)
    ki = lax.broadcasted_iota(I32, (MOBA_BLOCK, MOBA_BLOCK), 1)
    lane = lax.broadcasted_iota(I32, (MOBA_BLOCK, HEAD_DIM), 1)
    spread = jnp.where(lax.broadcasted_iota(I32, (HEAD_DIM, HEAD_DIM), 0) == L_LANE, 1.0, 0.0).astype(BF16)
    for u, og_ref in enumerate(og_refs):
        q = q_ref[0, u].astype(BF16)
        s = _nt_dot(q, k_ref[0, u])
        s = jnp.where(ki <= qi, s, NEG_INF)
        m_raw = jnp.max(s, axis=1, keepdims=True)
        p = jnp.exp2((s - m_raw) * (scale * LOG2_E)).astype(BF16)
        own = jnp.dot(p, v_ref[0, u], preferred_element_type=F32)
        parts = [_unpack_partial(og_ref[r]) for r in range(MOBA_TOPK)]
        ms = [jnp.dot(jnp.where(lane == L_LANE, hi, 0.0).astype(BF16), spread, preferred_element_type=F32)
              for _, hi in parts]
        m_own = jnp.broadcast_to(m_raw * scale, lane.shape)
        m_all = m_own
        for m in ms:
            m_all = jnp.maximum(m_all, m)
        w_own = jnp.exp2((m_own - m_all) * LOG2_E)
        tot_lo = w_own * own[:, :HEAD_DIM]
        tot_hi = w_own * own[:, HEAD_DIM:]
        for (lo, hi), m in zip(parts, ms):
            w = jnp.exp2((m - m_all) * LOG2_E)
            tot_lo = tot_lo + w * lo
            tot_hi = tot_hi + w * hi
        acc = jnp.where(lane < HALF, tot_lo, pltpu.roll(tot_hi, HALF, 1))
        cols = pl.ds(u * HEAD_DIM, HEAD_DIM)
        o_ref[:, cols] = ((acc / tot_lo[:, L_LANE:L_LANE + 1]) * sg_ref[:, cols]).astype(BF16)


def _moba_merge(q, k, vaug, sg, og, b, o_prev):
    B, H, S, Dh = q.shape
    nb = S // MOBA_BLOCK
    hps = HEADS_PER_STEP
    og_spec = lambda u: pl.BlockSpec((MOBA_TOPK, MOBA_BLOCK, LANES),
                                     lambda _, hp, j: (0, (hp * hps + u) * nb + j, 0))
    prev_specs = [] if o_prev is None else [pl.BlockSpec(memory_space=pl.ANY)]
    prev_args = [] if o_prev is None else [o_prev]
    return pl.pallas_call(
        functools.partial(_moba_merge_kernel, n_skip=len(prev_args)),
        grid=(1, H // hps, nb),
        in_specs=prev_specs
                 + [pl.BlockSpec((1, hps, MOBA_BLOCK, Dh), lambda _, hp, j: (b, hp, j, 0)),
                    pl.BlockSpec((1, hps, MOBA_BLOCK, Dh), lambda _, hp, j: (b, hp, j, 0)),
                    pl.BlockSpec((1, hps, MOBA_BLOCK, VA_W), lambda _, hp, j: (b, hp, j, 0)),
                    pl.BlockSpec((MOBA_BLOCK, hps * Dh), lambda _, hp, j: (b * nb + j, hp))]
                 + [og_spec(u) for u in range(hps)],
        out_specs=pl.BlockSpec((MOBA_BLOCK, hps * Dh), lambda _, hp, j: (b * nb + j, hp)),
        out_shape=jax.ShapeDtypeStruct((B * S, H * Dh), BF16),
        input_output_aliases={} if o_prev is None else {0: 0},
        compiler_params=_cparams(("arbitrary", "arbitrary", "arbitrary")),
        name="moba_merge",
    )(*prev_args, q, k, vaug, sg, *([og] * hps))


def _moba_attention(q, k, vaug, kmean, sg):
    B, H, S, Dh = q.shape
    nb = S // MOBA_BLOCK
    max_tiles = (MOBA_TOPK * S) // GROUP_TILE + nb + 1
    max_tiles = -(-max_tiles // TILES_PER_STEP) * TILES_PER_STEP
    slab_rows = max_tiles * GROUP_TILE

    q_rows = q.reshape(B * H * S, Dh)
    o = None
    for b in range(B):
        rt, counts = _moba_route(q, kmean, b)
        pos, tblk, tvalid, ntiles = _moba_layout(rt, counts.reshape(-1), nb, max_tiles)
        qg = _sc_scatter_rows(q_rows, b * H * nb, pos, H * slab_rows)
        parts = _moba_group(qg, k, vaug, b, tblk, tvalid, ntiles, max_tiles)
        og = _sc_gather_rows(parts, pos)
        o = _moba_merge(q, k, vaug, sg, og, b, o)
    return o


def _proj_residual_kernel(y_ref, w_ref, x_ref, out_ref):
    out_ref[...] = x_ref[...] + jnp.dot(y_ref[...], w_ref[...], preferred_element_type=F32)


def _proj_residual(y, w, x2d):
    T, D = x2d.shape
    K = y.shape[1]
    return pl.pallas_call(
        _proj_residual_kernel,
        grid=(T // ROW_TILE,),
        in_specs=[pl.BlockSpec((ROW_TILE, K), lambda i: (i, 0)),
                  pl.BlockSpec((K, D), lambda i: (0, 0)),
                  pl.BlockSpec((ROW_TILE, D), lambda i: (i, 0))],
        out_specs=pl.BlockSpec((ROW_TILE, D), lambda i: (i, 0)),
        out_shape=jax.ShapeDtypeStruct((T, D), F32),
        compiler_params=_cparams(("arbitrary",)),
        name="moba_out",
    )(y, w, x2d)


def _rope_tables(S):
    inv = np.float32(ROPE_THETA) ** (-np.arange(HALF, dtype=np.float32) / np.float32(HALF))
    ang = np.arange(S, dtype=np.float32)[:, None] * inv[None, :].astype(np.float32)
    cos, sin = np.cos(ang).astype(np.float32), np.sin(ang).astype(np.float32)
    return (jnp.asarray(np.concatenate([cos, cos], axis=-1)),
            jnp.asarray(np.concatenate([-sin, sin], axis=-1)))


def kernel(x, a_norm, a_in_w, a_conv_w, a_conv_b, a_r_w, a_r_b, a_i_w, a_i_b, a_lambda, a_out_w,
           kv_norm, kv_w, k_norm, b_norm, b_in_w, q_norm, b_out_w):
    B, S, D = x.shape
    assert S % ROW_TILE == 0 and ROW_TILE % MOBA_BLOCK == 0
    assert D == N_RG_BLOCKS * RG_BW == N_HEADS * HEAD_DIM
    assert (B * N_HEADS * S) % (IDX_CHUNK * SC_WORKERS) == 0
    xs = x.reshape(B * S, D)
    row = lambda v: v.reshape(1, -1)

    for l in range(a_in_w.shape[0]):
        wri = jnp.concatenate([a_r_w[l], a_i_w[l]], axis=-1).astype(BF16)
        xs = _hawk_layer(xs, B, S, row(a_norm[l]), a_in_w[l].astype(BF16), a_conv_w[l], row(a_conv_b[l]), wri,
                         row(a_r_b[l]), row(a_i_b[l]), row(a_lambda[l]), a_out_w[l].astype(BF16))

    cosf, sinf = _rope_tables(S)
    k, vaug, kmean = _kv_proj(xs, B, S, row(kv_norm), kv_w.astype(BF16), row(k_norm), cosf, sinf)
    kmean = jnp.transpose(kmean, (0, 2, 1, 3))

    for jl in range(b_in_w.shape[0]):
        q, sg = _q_proj(xs, B, S, row(b_norm[jl]), b_in_w[jl].astype(BF16), row(q_norm[jl]), cosf, sinf)
        y = _moba_attention(q, k, vaug, kmean, sg)
        xs = _proj_residual(y, b_out_w[jl].astype(BF16), xs)
    return xs.reshape(B, S, D)
```

```python
import functools

import jax
import jax.numpy as jnp
import numpy as np
from jax import lax
from jax.experimental import pallas as pl
from jax.experimental.pallas import tpu as pltpu
from jax.experimental.pallas import tpu_sc as plsc

N_HEADS = 8
HEAD_DIM = 128
MOBA_BLOCK = 256
MOBA_TOPK = 3
CONV_WIDTH = 4
N_RG_BLOCKS = 8
RG_BW = 128
RG_C = 8.0
ROPE_THETA = 10000.0
EPS = 1e-6
NEG_INF = -1e30
LOG2_E = 1.4426950408889634

SUBLANES = 8
LANES = 128
ROW_TILE = 512
SUB_ROWS = 128
VMEM_LIMIT = 56 * 1024 * 1024

GROUP_TILE = MOBA_BLOCK
TILES_PER_STEP = 16
HEADS_PER_STEP = 8
MERGE_TILES = 2
ROUTE_TILES = 4
ROUTE_BLK_SHIFT = 16
ROUTE_BLK_UNIT = 1 << ROUTE_BLK_SHIFT
HALF = HEAD_DIM // 2
VA_W = 2 * HEAD_DIM
L_LANE = HALF
HI16_MASK = -65536
BF16_HALF_ULP = 0x8000
NEG_INF_BITS = int(np.float32(NEG_INF).view(np.int32))

SC_CORES = 2
SC_SUBCORES = 16
SC_WORKERS = SC_CORES * SC_SUBCORES
IDX_CHUNK = 128
SC_HALVES = MOBA_BLOCK // IDX_CHUNK

F32 = jnp.float32
BF16 = jnp.bfloat16
I32 = jnp.int32


def _cparams(sem):
    return pltpu.CompilerParams(dimension_semantics=sem, vmem_limit_bytes=VMEM_LIMIT)


def _rms_norm(x, g):
    ms = jnp.mean(x * x, axis=-1, keepdims=True)
    return (x * lax.rsqrt(ms + EPS)) * g


def _silu(x):
    hx = 0.5 * x
    return hx * jnp.tanh(hx) + hx


def _sigmoid(x):
    return 0.5 * jnp.tanh(0.5 * x) + 0.5


def _nt_dot(a, b):
    return lax.dot_general(a, b, (((1,), (1,)), ((), ())), preferred_element_type=F32)


def _head_norm_rope(z, g, cosf, sinf):
    outs = []
    for h in range(N_HEADS):
        zh = z[:, h * HEAD_DIM:(h + 1) * HEAD_DIM]
        zh = _rms_norm(zh, g)
        outs.append(zh * cosf + pltpu.roll(zh, HEAD_DIM // 2, 1) * sinf)
    return outs


def _hawk_kernel(x_ref, g_ref, inw_ref, cw_ref, cb_ref, wri_ref, rb_ref, ib_ref, lam_ref, outw_ref,
                 o_ref, gate_ref, xpad_ref, a_ref, b_ref, hc_ref):
    tm, C = gate_ref.shape
    s = pl.program_id(1)

    @pl.when(s == 0)
    def _():
        xpad_ref[pl.ds(0, SUBLANES), :] = jnp.zeros((SUBLANES, xpad_ref.shape[1]), F32)
        hc_ref[...] = jnp.zeros_like(hc_ref)

    h = _rms_norm(x_ref[...], g_ref[...]).astype(BF16)
    gate_ref[...] = jnp.dot(h, inw_ref[:, C:], preferred_element_type=F32)

    xpad_ref[pl.ds(SUBLANES, tm), :] = jnp.dot(h, inw_ref[:, :C], preferred_element_type=F32)
    cw = cw_ref[...]
    xc = xpad_ref[pl.ds(SUBLANES - (CONV_WIDTH - 1), tm), :] * cw[0:1, :]
    for k in range(1, CONV_WIDTH):
        xc = xc + xpad_ref[pl.ds(SUBLANES - (CONV_WIDTH - 1) + k, tm), :] * cw[k:k + 1, :]
    xc = xc + cb_ref[...]
    xpad_ref[pl.ds(0, SUBLANES), :] = xpad_ref[pl.ds(tm, SUBLANES), :]

    lam = lam_ref[...]
    sp = jnp.maximum(-lam, 0.0) + jnp.log1p(jnp.exp(-jnp.abs(lam)))
    log2a_per_r = (-RG_C * LOG2_E) * sp
    xc16 = xc.astype(BF16)
    for g in range(N_RG_BLOCKS):
        lo = g * RG_BW
        z = jnp.dot(xc16[:, lo:lo + RG_BW], wri_ref[g], preferred_element_type=F32)
        r = _sigmoid(z[:, :RG_BW] + rb_ref[:, lo:lo + RG_BW])
        i = _sigmoid(z[:, RG_BW:] + ib_ref[:, lo:lo + RG_BW])
        a = jnp.exp2(r * log2a_per_r[:, lo:lo + RG_BW])
        mult = jnp.sqrt(1.0 - a * a)
        a_ref[:, lo:lo + RG_BW] = a
        b_ref[:, lo:lo + RG_BW] = mult * (i * xc[:, lo:lo + RG_BW])

    row = lax.broadcasted_iota(I32, (SUBLANES, a_ref.shape[1]), 0)

    def scan_body(c, hc):
        r0 = pl.multiple_of(c * SUBLANES, SUBLANES)
        a = a_ref[pl.ds(r0, SUBLANES), :]
        b = b_ref[pl.ds(r0, SUBLANES), :]
        for k in (1, 2, 4):
            keep = row >= k
            b = jnp.where(keep, a * pltpu.roll(b, k, 0) + b, b)
            a = jnp.where(keep, a * pltpu.roll(a, k, 0), a)
        h = a * hc + b
        b_ref[pl.ds(r0, SUBLANES), :] = h
        return h[SUBLANES - 1:SUBLANES, :]

    hc = lax.fori_loop(0, tm // SUBLANES, scan_body, hc_ref[0:1, :])
    hc_ref[0:1, :] = hc

    y = (b_ref[...] * _silu(gate_ref[...])).astype(BF16)
    o_ref[...] = x_ref[...] + jnp.dot(y, outw_ref[...], preferred_element_type=F32)


def _hawk_layer(x2d, B, S, g, in_w, cw, cb, wri, rb, ib, lam, out_w):
    T, D = x2d.shape
    C = in_w.shape[1] // 2
    tm = ROW_TILE
    ns = S // tm
    vec = lambda n: pl.BlockSpec((1, n), lambda b, s: (0, 0))
    return pl.pallas_call(
        _hawk_kernel,
        grid=(B, ns),
        in_specs=[pl.BlockSpec((tm, D), lambda b, s: (b * ns + s, 0)),
                  vec(D),
                  pl.BlockSpec((D, 2 * C), lambda b, s: (0, 0)),
                  pl.BlockSpec((CONV_WIDTH, C), lambda b, s: (0, 0)),
                  vec(C),
                  pl.BlockSpec((N_RG_BLOCKS, RG_BW, 2 * RG_BW), lambda b, s: (0, 0, 0)),
                  vec(C), vec(C), vec(C),
                  pl.BlockSpec((C, D), lambda b, s: (0, 0))],
        out_specs=pl.BlockSpec((tm, D), lambda b, s: (b * ns + s, 0)),
        out_shape=jax.ShapeDtypeStruct((T, D), F32),
        scratch_shapes=[pltpu.VMEM((tm, C), F32),
                        pltpu.VMEM((tm + SUBLANES, C), F32),
                        pltpu.VMEM((tm, C), F32),
                        pltpu.VMEM((tm, C), F32),
                        pltpu.VMEM((SUBLANES, C), F32)],
        compiler_params=_cparams(("arbitrary", "arbitrary")),
        name="hawk_layer",
    )(x2d, g, in_w, cw, cb, wri, rb, ib, lam, out_w)


def _kv_proj_kernel(x_ref, g_ref, w_ref, kg_ref, cos_ref, sin_ref, k_ref, v_ref, km_ref):
    tm = x_ref.shape[0]
    d_attn = N_HEADS * HEAD_DIM
    lane = lax.broadcasted_iota(I32, (SUB_ROWS, HEAD_DIM), 1)
    ksum = {}
    for t in range(tm // SUB_ROWS):
        rows = pl.ds(t * SUB_ROWS, SUB_ROWS)
        h = _rms_norm(x_ref[rows, :], g_ref[...]).astype(BF16)
        kv = jnp.dot(h, w_ref[...], preferred_element_type=F32)
        ks = _head_norm_rope(kv[:, :d_attn], kg_ref[...], cos_ref[rows, :], sin_ref[rows, :])
        for hd in range(N_HEADS):
            k_ref[0, hd, rows, :] = ks[hd].astype(BF16)
            vh = kv[:, d_attn + hd * HEAD_DIM:d_attn + (hd + 1) * HEAD_DIM]
            lo = jnp.where(lane < HALF, vh, jnp.where(lane == L_LANE, 1.0, 0.0))
            hi = jnp.where(lane < HALF, pltpu.roll(vh, HALF, 1), 0.0)
            v_ref[0, hd, rows, :] = jnp.concatenate([lo, hi], axis=1).astype(BF16)
            key = (hd, (t * SUB_ROWS) // MOBA_BLOCK)
            part = jnp.sum(ks[hd], axis=0, keepdims=True)
            ksum[key] = part if key not in ksum else ksum[key] + part
    for (hd, r), total in ksum.items():
        km_ref[0, r, pl.ds(hd, 1), :] = total * (1.0 / MOBA_BLOCK)


def _kv_proj(x2d, B, S, g, w, kg, cosf, sinf):
    T, D = x2d.shape
    tm = ROW_TILE
    ns = S // tm
    nb = S // MOBA_BLOCK
    rpt = tm // MOBA_BLOCK
    return pl.pallas_call(
        _kv_proj_kernel,
        grid=(B, ns),
        in_specs=[pl.BlockSpec((tm, D), lambda b, s: (b * ns + s, 0)),
                  pl.BlockSpec((1, D), lambda b, s: (0, 0)),
                  pl.BlockSpec((D, 2 * N_HEADS * HEAD_DIM), lambda b, s: (0, 0)),
                  pl.BlockSpec((1, HEAD_DIM), lambda b, s: (0, 0)),
                  pl.BlockSpec((tm, HEAD_DIM), lambda b, s: (s, 0)),
                  pl.BlockSpec((tm, HEAD_DIM), lambda b, s: (s, 0))],
        out_specs=[pl.BlockSpec((1, N_HEADS, tm, HEAD_DIM), lambda b, s: (b, 0, s, 0)),
                   pl.BlockSpec((1, N_HEADS, tm, VA_W), lambda b, s: (b, 0, s, 0)),
                   pl.BlockSpec((1, rpt, N_HEADS, HEAD_DIM), lambda b, s: (b, s, 0, 0))],
        out_shape=[jax.ShapeDtypeStruct((B, N_HEADS, S, HEAD_DIM), BF16),
                   jax.ShapeDtypeStruct((B, N_HEADS, S, VA_W), BF16),
                   jax.ShapeDtypeStruct((B, nb, N_HEADS, HEAD_DIM), F32)],
        compiler_params=_cparams(("arbitrary", "arbitrary")),
        name="kv_proj",
    )(x2d, g, w, kg, cosf, sinf)


def _q_proj_kernel(x_ref, g_ref, w_ref, qg_ref, cos_ref, sin_ref, q_ref, sg_ref):
    d_attn = N_HEADS * HEAD_DIM
    for t in range(x_ref.shape[0] // SUB_ROWS):
        rows = pl.ds(t * SUB_ROWS, SUB_ROWS)
        h = _rms_norm(x_ref[rows, :], g_ref[...]).astype(BF16)
        u = jnp.dot(h, w_ref[...], preferred_element_type=F32)
        qs = _head_norm_rope(u[:, :d_attn], qg_ref[...], cos_ref[rows, :], sin_ref[rows, :])
        for hd in range(N_HEADS):
            q_ref[0, hd, rows, :] = qs[hd]
        sg_ref[rows, :] = _silu(u[:, d_attn:])


def _q_proj(x2d, B, S, g, w, qg, cosf, sinf):
    T, D = x2d.shape
    tm = ROW_TILE
    ns = S // tm
    d_attn = N_HEADS * HEAD_DIM
    return pl.pallas_call(
        _q_proj_kernel,
        grid=(B, ns),
        in_specs=[pl.BlockSpec((tm, D), lambda b, s: (b * ns + s, 0)),
                  pl.BlockSpec((1, D), lambda b, s: (0, 0)),
                  pl.BlockSpec((D, 2 * d_attn), lambda b, s: (0, 0)),
                  pl.BlockSpec((1, HEAD_DIM), lambda b, s: (0, 0)),
                  pl.BlockSpec((tm, HEAD_DIM), lambda b, s: (s, 0)),
                  pl.BlockSpec((tm, HEAD_DIM), lambda b, s: (s, 0))],
        out_specs=[pl.BlockSpec((1, N_HEADS, tm, HEAD_DIM), lambda b, s: (b, 0, s, 0)),
                   pl.BlockSpec((tm, d_attn), lambda b, s: (b * ns + s, 0))],
        out_shape=[jax.ShapeDtypeStruct((B, N_HEADS, S, HEAD_DIM), F32),
                   jax.ShapeDtypeStruct((T, d_attn), F32)],
        compiler_params=_cparams(("arbitrary", "arbitrary")),
        name="q_proj",
    )(x2d, g, w, qg, cosf, sinf)


def _moba_route_kernel(q_ref, km_ref, rt_ref, cnt_ref, run_ref):
    step = pl.program_id(2)
    nb = km_ref.shape[2]

    @pl.when(step == 0)
    def _():
        run_ref[...] = jnp.zeros_like(run_ref)

    km = km_ref[0, 0].astype(BF16)
    blk = lax.broadcasted_iota(I32, (nb, MOBA_BLOCK), 0)
    qa = lax.broadcasted_iota(I32, (MOBA_BLOCK, MOBA_BLOCK), 0)
    qb = lax.broadcasted_iota(I32, (MOBA_BLOCK, MOBA_BLOCK), 1)
    earlier = jnp.where(qa < qb, 1.0, 0.0).astype(BF16)
    run = run_ref[...]
    for u in range(ROUTE_TILES):
        j = step * ROUTE_TILES + u
        g = _nt_dot(km, q_ref[0, 0, pl.ds(u * MOBA_BLOCK, MOBA_BLOCK), :].astype(BF16))
        g = jnp.where(blk < j, g, -jnp.inf)
        hits, ids = [], []
        for _ in range(MOBA_TOPK):
            mx = jnp.max(g, axis=0, keepdims=True)
            idx = jnp.min(jnp.where(g == mx, blk, nb), axis=0, keepdims=True)
            hit = blk == idx
            ok = mx > -jnp.inf
            hits.append(jnp.where(hit & ok, 1.0, 0.0))
            ids.append(jnp.where(ok, idx, -1))
            g = jnp.where(hit, -jnp.inf, g)
        sel = hits[0] + hits[1] + hits[2]

        rank = jnp.dot(sel.astype(BF16), earlier, preferred_element_type=F32)
        dest = run + rank
        run = run + jnp.sum(sel, axis=1, keepdims=True)
        for r in range(MOBA_TOPK):
            loc = jnp.sum(hits[r] * dest, axis=0, keepdims=True).astype(I32)
            rt_ref[u, pl.ds(r, 1), :] = jnp.where(ids[r] >= 0, ids[r] * ROUTE_BLK_UNIT + loc, -1)
        rt_ref[u, pl.ds(MOBA_TOPK, SUBLANES - MOBA_TOPK), :] = jnp.full(
            (SUBLANES - MOBA_TOPK, MOBA_BLOCK), -1, I32)
    run_ref[...] = run
    cnt_ref[0] = run.astype(I32)


def _moba_route(q, kmean, b):
    _, H, S, Dh = q.shape
    nb = S // MOBA_BLOCK
    rows = ROUTE_TILES * MOBA_BLOCK
    nsteps = nb // ROUTE_TILES
    return pl.pallas_call(
        _moba_route_kernel,
        grid=(1, H, nsteps),
        in_specs=[pl.BlockSpec((1, 1, rows, Dh), lambda _, h, j: (b, h, j, 0)),
                  pl.BlockSpec((1, 1, nb, Dh), lambda _, h, j: (b, h, 0, 0))],
        out_specs=[pl.BlockSpec((ROUTE_TILES, SUBLANES, MOBA_BLOCK), lambda _, h, j: (h * nsteps + j, 0, 0)),
                   pl.BlockSpec((1, nb, 1), lambda _, h, j: (h, 0, 0))],
        out_shape=[jax.ShapeDtypeStruct((H * nb, SUBLANES, MOBA_BLOCK), I32),
                   jax.ShapeDtypeStruct((H, nb, 1), I32)],
        scratch_shapes=[pltpu.VMEM((nb, 1), F32)],
        compiler_params=_cparams(("arbitrary", "arbitrary", "arbitrary")),
        name="moba_route",
    )(q, kmean)


def _moba_layout_kernel(cnt_ref, rt_ref, pos_ref, tblk_ref, tvalid_ref, nt_ref, base_ref, *, max_tiles):
    g = pl.program_id(0)
    nb = rt_ref.shape[0]
    slab_rows = max_tiles * GROUP_TILE
    null_row0 = (max_tiles - 1) * GROUP_TILE

    def per_block(n, first_tile):
        count = cnt_ref[g * nb + n]
        tiles = (count + (GROUP_TILE - 1)) // GROUP_TILE
        base_ref[n] = first_tile * GROUP_TILE

        def per_tile(i, carry):
            tblk_ref[g * max_tiles + first_tile + i] = n
            tvalid_ref[g * max_tiles + first_tile + i] = jnp.minimum(count - i * GROUP_TILE, GROUP_TILE)
            return carry

        lax.fori_loop(0, tiles, per_tile, 0)
        return first_tile + tiles

    used = lax.fori_loop(0, nb, per_block, 0)
    nt_ref[g] = used

    def unused_tile(t, carry):
        tblk_ref[g * max_tiles + t] = nb - 1
        tvalid_ref[g * max_tiles + t] = 0
        return carry

    lax.fori_loop(used, max_tiles, unused_tile, 0)

    code = rt_ref[...].reshape(nb * SUBLANES, MOBA_BLOCK)
    blk = lax.shift_right_arithmetic(code, ROUTE_BLK_SHIFT)
    base = jnp.zeros_like(code)
    for n in range(nb):
        base = jnp.where(blk == n, base_ref[n], base)
    lane = lax.broadcasted_iota(I32, code.shape, 1)
    row = jnp.where(code >= 0, base + (code & (ROUTE_BLK_UNIT - 1)), null_row0 + lane)
    pos_ref[...] = (row + g * slab_rows).reshape(pos_ref.shape)


def _moba_layout(rt, counts, nb, max_tiles):
    G = rt.shape[0] // nb
    smem = pl.BlockSpec(memory_space=pltpu.SMEM)
    return pl.pallas_call(
        functools.partial(_moba_layout_kernel, max_tiles=max_tiles),
        grid_spec=pltpu.PrefetchScalarGridSpec(
            num_scalar_prefetch=1,
            grid=(G,),
            in_specs=[pl.BlockSpec((nb, SUBLANES, MOBA_BLOCK), lambda g, cnt: (g, 0, 0))],
            out_specs=[pl.BlockSpec((nb, SUBLANES, MOBA_BLOCK), lambda g, cnt: (g, 0, 0)), smem, smem, smem],
            scratch_shapes=[pltpu.SMEM((nb,), I32)],
        ),
        out_shape=[jax.ShapeDtypeStruct((G * nb, SUBLANES, MOBA_BLOCK), I32),
                   jax.ShapeDtypeStruct((G * max_tiles,), I32),
                   jax.ShapeDtypeStruct((G * max_tiles,), I32),
                   jax.ShapeDtypeStruct((G,), I32)],
        compiler_params=_cparams(("arbitrary",)),
        name="moba_layout",
    )(counts, rt)


def _sc_mesh():
    return plsc.VectorSubcoreMesh(core_axis_name="c", subcore_axis_name="s")


def _sc_load_indices(pos_hbm, idx_v, t0, n_tiles):
    for c in range(SC_HALVES):
        pltpu.sync_copy(pos_hbm.at[pl.ds(t0, n_tiles), pl.ds(0, MOBA_TOPK), pl.ds(c * IDX_CHUNK, IDX_CHUNK)],
                        idx_v.at[c])


def _sc_scatter_rows(src, src_tile0, pos, n_out_rows):
    D = src.shape[1]
    n_tiles = pos.shape[0]
    per_w = n_tiles // SC_WORKERS
    assert per_w * SC_WORKERS == n_tiles

    @functools.partial(
        pl.kernel, mesh=_sc_mesh(),
        out_type=jax.ShapeDtypeStruct((n_out_rows, D), src.dtype),
        scratch_types=[pltpu.VMEM((SC_HALVES, per_w, MOBA_TOPK, IDX_CHUNK), I32),
                       pltpu.VMEM((SC_HALVES, IDX_CHUNK, D), src.dtype),
                       pltpu.SemaphoreType.DMA((SC_HALVES,)),
                       pltpu.SemaphoreType.DMA((SC_HALVES,))],
        name="sc_scatter_rows",
    )
    def k(src_hbm, pos_hbm, out_hbm, idx_v, rows_v, load_sem, scat_sem):
        t0 = (lax.axis_index("s") * SC_CORES + lax.axis_index("c")) * per_w
        _sc_load_indices(pos_hbm, idx_v, t0, per_w)

        def load(i, c):
            row0 = pl.multiple_of((src_tile0 + t0 + i) * MOBA_BLOCK + c * IDX_CHUNK, IDX_CHUNK)
            return pltpu.make_async_copy(src_hbm.at[pl.ds(row0, IDX_CHUNK)], rows_v.at[c], load_sem.at[c])

        def scatters(i, c):
            return [pltpu.make_async_copy(rows_v.at[c], out_hbm.at[idx_v.at[c, i, r]], scat_sem.at[c])
                    for r in range(MOBA_TOPK)]

        load(0, 0).start()

        @pl.loop(0, per_w)
        def _(i):
            for c in range(SC_HALVES):
                pi, pc = (i, 0) if c else (i - 1, 1)
                ni, nc = (i + 1, 0) if c else (i, 1)
                load(i, c).wait()
                for cp in scatters(i, c):
                    cp.start()

                @pl.when(pi >= 0)
                def _():
                    for cp in scatters(pi, pc):
                        cp.wait()

                @pl.when(ni < per_w)
                def _():
                    load(ni, nc).start()

        for cp in scatters(per_w - 1, SC_HALVES - 1):
            cp.wait()

    return k(src, pos)


def _sc_gather_rows(table, pos):
    R, D = table.shape
    n_tiles = pos.shape[0]
    N = n_tiles * MOBA_BLOCK
    per_w = n_tiles // SC_WORKERS
    assert per_w * SC_WORKERS == n_tiles

    items = SC_HALVES * MOBA_TOPK
    n_buf = 2
    assert items % n_buf == 0

    @functools.partial(
        pl.kernel, mesh=_sc_mesh(),
        out_type=jax.ShapeDtypeStruct((MOBA_TOPK, N, D), table.dtype),
        scratch_types=[pltpu.VMEM((SC_HALVES, per_w, MOBA_TOPK, IDX_CHUNK), I32),
                       pltpu.VMEM((n_buf, IDX_CHUNK, D), table.dtype),
                       pltpu.SemaphoreType.DMA((n_buf,)),
                       pltpu.SemaphoreType.DMA((n_buf,))],
        name="sc_gather_rows",
    )
    def k(table_hbm, pos_hbm, out_hbm, idx_v, rows_v, gat_sem, put_sem):
        t0 = (lax.axis_index("s") * SC_CORES + lax.axis_index("c")) * per_w
        _sc_load_indices(pos_hbm, idx_v, t0, per_w)

        def gather(i, j):
            c, r = divmod(j, MOBA_TOPK)
            return pltpu.make_async_copy(table_hbm.at[idx_v.at[c, i, r]], rows_v.at[j % n_buf], gat_sem.at[j % n_buf])

        def put(i, j):
            c, r = divmod(j, MOBA_TOPK)
            row0 = pl.multiple_of((t0 + i) * MOBA_BLOCK + c * IDX_CHUNK, IDX_CHUNK)
            return pltpu.make_async_copy(rows_v.at[j % n_buf], out_hbm.at[r, pl.ds(row0, IDX_CHUNK)],
                                         put_sem.at[j % n_buf])

        gather(0, 0).start()

        @pl.loop(0, per_w)
        def _(i):
            for j in range(items):
                pi, pj = (i, j - 1) if j else (i - 1, items - 1)
                ni, nj = (i, j + 1) if j + 1 < items else (i + 1, 0)
                gather(i, j).wait()
                put(i, j).start()

                @pl.when(pi >= 0)
                def _():
                    put(pi, pj).wait()

                @pl.when(ni < per_w)
                def _():
                    gather(ni, nj).start()

        put(per_w - 1, items - 1).wait()

    return k(table, pos)


def _bits(x):
    return lax.bitcast_convert_type(x, I32)


def _pack_partial(part, m):
    lane = lax.broadcasted_iota(I32, (part.shape[0], HEAD_DIM), 1)
    lo = _bits(part[:, :HEAD_DIM]) + BF16_HALF_ULP
    hi = _bits(jnp.where(lane == L_LANE, m, part[:, HEAD_DIM:])) + BF16_HALF_ULP
    return (hi & HI16_MASK) | lax.shift_right_logical(lo, 16)


def _null_partial(rows):
    lane = lax.broadcasted_iota(I32, (rows, HEAD_DIM), 1)
    return jnp.where(lane == L_LANE, NEG_INF_BITS & HI16_MASK, 0)


def _unpack_partial(word):
    lo = lax.bitcast_convert_type(lax.shift_left(word, 16), F32)
    hi = lax.bitcast_convert_type(word & HI16_MASK, F32)
    return lo, hi


def _moba_group_kernel(tblk_ref, tvalid_ref, nt_ref, qg_ref, k_ref, v_ref, o_ref, *, max_tiles):
    g, step = pl.program_id(0), pl.program_id(1)
    t0 = g * max_tiles + step * TILES_PER_STEP
    scale = HEAD_DIM ** -0.5

    @pl.when(tvalid_ref[t0] > 0)
    def _():
        for u in range(TILES_PER_STEP):
            rows = pl.ds(u * GROUP_TILE, GROUP_TILE)
            q = qg_ref[rows, :].astype(BF16)
            n0 = pl.multiple_of(tblk_ref[t0 + u] * MOBA_BLOCK, MOBA_BLOCK)
            s = _nt_dot(q, k_ref[0, 0, pl.ds(n0, MOBA_BLOCK), :])
            m = (jnp.max(s, axis=1, keepdims=True) * scale).astype(BF16).astype(F32)
            p = jnp.exp2(s * (scale * LOG2_E) - m * LOG2_E).astype(BF16)
            part = jnp.dot(p, v_ref[0, 0, pl.ds(n0, MOBA_BLOCK), :], preferred_element_type=F32)
            o_ref[rows, :] = _pack_partial(part, m)

    @pl.when(step == pl.num_programs(1) - 1)
    def _():
        o_ref[pl.ds((TILES_PER_STEP - 1) * GROUP_TILE, GROUP_TILE), :] = _null_partial(GROUP_TILE)


def _moba_group(qg, k, vaug, b, tblk, tvalid, ntiles, max_tiles):
    _, H, S, Dh = k.shape
    G = H
    n_steps = max_tiles // TILES_PER_STEP
    step_rows = TILES_PER_STEP * GROUP_TILE

    def tile_idx(g, s, tblk, tvalid, nt):
        return (g * n_steps + jnp.where(s * TILES_PER_STEP < nt[g], s, n_steps - 1), 0)

    return pl.pallas_call(
        functools.partial(_moba_group_kernel, max_tiles=max_tiles),
        grid_spec=pltpu.PrefetchScalarGridSpec(
            num_scalar_prefetch=3,
            grid=(G, n_steps),
            in_specs=[pl.BlockSpec((step_rows, Dh), tile_idx),
                      pl.BlockSpec((1, 1, S, Dh), lambda g, s, *_: (b, g, 0, 0)),
                      pl.BlockSpec((1, 1, S, VA_W), lambda g, s, *_: (b, g, 0, 0))],
            out_specs=pl.BlockSpec((step_rows, LANES), tile_idx),
        ),
        out_shape=jax.ShapeDtypeStruct((G * max_tiles * GROUP_TILE, LANES), I32),
        compiler_params=_cparams(("arbitrary", "arbitrary")),
        name="moba_group",
    )(tblk, tvalid, ntiles, qg, k, vaug)


def _moba_merge_kernel(*refs, n_skip):
    q_ref, k_ref, v_ref, sg_ref = refs[n_skip:n_skip + 4]
    og_refs, o_ref = refs[n_skip + 4:-1], refs[-1]
    scale = HEAD_DIM ** -0.5
    qi = lax.broadcasted_iota(I32, (MOBA_BLOCK, MOBA_BLOCK), 0)
    ki = lax.broadcasted_iota(I32, (MOBA_BLOCK, MOBA_BLOCK), 1)
    lane = lax.broadcasted_iota(I32, (MOBA_BLOCK, HEAD_DIM), 1)
    spread = jnp.where(lax.broadcasted_iota(I32, (HEAD_DIM, HEAD_DIM), 0) == L_LANE, 1.0, 0.0).astype(BF16)
    for u, t in [(u, t) for t in range(MERGE_TILES) for u in range(len(og_refs))]:
        og_ref = og_refs[u]
        rows = pl.ds(t * MOBA_BLOCK, MOBA_BLOCK)
        q = q_ref[0, u, rows, :].astype(BF16)
        s = _nt_dot(q, k_ref[0, u, rows, :])
        s = jnp.where(ki <= qi, s, NEG_INF)
        m_raw = jnp.max(s, axis=1, keepdims=True)
        p = jnp.exp2((s - m_raw) * (scale * LOG2_E)).astype(BF16)
        own = jnp.dot(p, v_ref[0, u, rows, :], preferred_element_type=F32)
        parts = [_unpack_partial(og_ref[r, rows, :]) for r in range(MOBA_TOPK)]
        ms = [jnp.dot(jnp.where(lane == L_LANE, hi, 0.0).astype(BF16), spread, preferred_element_type=F32)
              for _, hi in parts]
        m_own = jnp.broadcast_to(m_raw * scale, lane.shape)
        m_all = m_own
        for m in ms:
            m_all = jnp.maximum(m_all, m)
        w_own = jnp.exp2((m_own - m_all) * LOG2_E)
        tot_lo = w_own * own[:, :HEAD_DIM]
        tot_hi = w_own * own[:, HEAD_DIM:]
        for (lo, hi), m in zip(parts, ms):
            w = jnp.exp2((m - m_all) * LOG2_E)
            tot_lo = tot_lo + w * lo
            tot_hi = tot_hi + w * hi
        acc = jnp.where(lane < HALF, tot_lo, pltpu.roll(tot_hi, HALF, 1))
        cols = pl.ds(u * HEAD_DIM, HEAD_DIM)
        o_ref[rows, cols] = ((acc / tot_lo[:, L_LANE:L_LANE + 1]) * sg_ref[rows, cols]).astype(BF16)


def _moba_merge(q, k, vaug, sg, og, b, o_prev):
    B, H, S, Dh = q.shape
    nb = S // MOBA_BLOCK
    hps = HEADS_PER_STEP
    rows = MERGE_TILES * MOBA_BLOCK
    ns = S // rows
    og_spec = lambda u: pl.BlockSpec((MOBA_TOPK, rows, LANES),
                                     lambda _, hp, j: (0, (hp * hps + u) * ns + j, 0))
    prev_specs = [] if o_prev is None else [pl.BlockSpec(memory_space=pl.ANY)]
    prev_args = [] if o_prev is None else [o_prev]
    return pl.pallas_call(
        functools.partial(_moba_merge_kernel, n_skip=len(prev_args)),
        grid=(1, H // hps, ns),
        in_specs=prev_specs
                 + [pl.BlockSpec((1, hps, rows, Dh), lambda _, hp, j: (b, hp, j, 0)),
                    pl.BlockSpec((1, hps, rows, Dh), lambda _, hp, j: (b, hp, j, 0)),
                    pl.BlockSpec((1, hps, rows, VA_W), lambda _, hp, j: (b, hp, j, 0)),
                    pl.BlockSpec((rows, hps * Dh), lambda _, hp, j: (b * ns + j, hp))]
                 + [og_spec(u) for u in range(hps)],
        out_specs=pl.BlockSpec((rows, hps * Dh), lambda _, hp, j: (b * ns + j, hp)),
        out_shape=jax.ShapeDtypeStruct((B * S, H * Dh), BF16),
        input_output_aliases={} if o_prev is None else {0: 0},
        compiler_params=_cparams(("arbitrary", "arbitrary", "arbitrary")),
        name="moba_merge",
    )(*prev_args, q, k, vaug, sg, *([og] * hps))


def _moba_attention(q, k, vaug, kmean, sg):
    B, H, S, Dh = q.shape
    nb = S // MOBA_BLOCK
    max_tiles = (MOBA_TOPK * S) // GROUP_TILE + nb + 1
    max_tiles = -(-max_tiles // TILES_PER_STEP) * TILES_PER_STEP
    slab_rows = max_tiles * GROUP_TILE

    q_rows = q.reshape(B * H * S, Dh)
    o = None
    for b in range(B):
        rt, counts = _moba_route(q, kmean, b)
        pos, tblk, tvalid, ntiles = _moba_layout(rt, counts.reshape(-1), nb, max_tiles)
        qg = _sc_scatter_rows(q_rows, b * H * nb, pos, H * slab_rows)
        parts = _moba_group(qg, k, vaug, b, tblk, tvalid, ntiles, max_tiles)
        og = _sc_gather_rows(parts, pos)
        o = _moba_merge(q, k, vaug, sg, og, b, o)
    return o


def _proj_residual_kernel(y_ref, w_ref, x_ref, out_ref):
    out_ref[...] = x_ref[...] + jnp.dot(y_ref[...], w_ref[...], preferred_element_type=F32)


def _proj_residual(y, w, x2d):
    T, D = x2d.shape
    K = y.shape[1]
    return pl.pallas_call(
        _proj_residual_kernel,
        grid=(T // ROW_TILE,),
        in_specs=[pl.BlockSpec((ROW_TILE, K), lambda i: (i, 0)),
                  pl.BlockSpec((K, D), lambda i: (0, 0)),
                  pl.BlockSpec((ROW_TILE, D), lambda i: (i, 0))],
        out_specs=pl.BlockSpec((ROW_TILE, D), lambda i: (i, 0)),
        out_shape=jax.ShapeDtypeStruct((T, D), F32),
        compiler_params=_cparams(("arbitrary",)),
        name="moba_out",
    )(y, w, x2d)


def _rope_tables(S):
    inv = np.float32(ROPE_THETA) ** (-np.arange(HALF, dtype=np.float32) / np.float32(HALF))
    ang = np.arange(S, dtype=np.float32)[:, None] * inv[None, :].astype(np.float32)
    cos, sin = np.cos(ang).astype(np.float32), np.sin(ang).astype(np.float32)
    return (jnp.asarray(np.concatenate([cos, cos], axis=-1)),
            jnp.asarray(np.concatenate([-sin, sin], axis=-1)))


def kernel(x, a_norm, a_in_w, a_conv_w, a_conv_b, a_r_w, a_r_b, a_i_w, a_i_b, a_lambda, a_out_w,
           kv_norm, kv_w, k_norm, b_norm, b_in_w, q_norm, b_out_w):
    B, S, D = x.shape
    assert S % ROW_TILE == 0 and ROW_TILE % MOBA_BLOCK == 0
    assert D == N_RG_BLOCKS * RG_BW == N_HEADS * HEAD_DIM
    assert (B * N_HEADS * S) % (IDX_CHUNK * SC_WORKERS) == 0
    xs = x.reshape(B * S, D)
    row = lambda v: v.reshape(1, -1)

    for l in range(a_in_w.shape[0]):
        wri = jnp.concatenate([a_r_w[l], a_i_w[l]], axis=-1).astype(BF16)
        xs = _hawk_layer(xs, B, S, row(a_norm[l]), a_in_w[l].astype(BF16), a_conv_w[l], row(a_conv_b[l]), wri,
                         row(a_r_b[l]), row(a_i_b[l]), row(a_lambda[l]), a_out_w[l].astype(BF16))

    cosf, sinf = _rope_tables(S)
    k, vaug, kmean = _kv_proj(xs, B, S, row(kv_norm), kv_w.astype(BF16), row(k_norm), cosf, sinf)
    kmean = jnp.transpose(kmean, (0, 2, 1, 3))

    for jl in range(b_in_w.shape[0]):
        q, sg = _q_proj(xs, B, S, row(b_norm[jl]), b_in_w[jl].astype(BF16), row(q_norm[jl]), cosf, sinf)
        y = _moba_attention(q, k, vaug, kmean, sg)
        xs = _proj_residual(y, b_out_w[jl].astype(BF16), xs)
    return xs.reshape(B, S, D)
```

```python
import functools

import jax
import jax.numpy as jnp
import numpy as np
from jax import lax
from jax.experimental import pallas as pl
from jax.experimental.pallas import tpu as pltpu
from jax.experimental.pallas import tpu_sc as plsc

N_HEADS = 8
HEAD_DIM = 128
MOBA_BLOCK = 256
MOBA_TOPK = 3
CONV_WIDTH = 4
N_RG_BLOCKS = 8
RG_BW = 128
RG_C = 8.0
ROPE_THETA = 10000.0
EPS = 1e-6
NEG_INF = -1e30
LOG2_E = 1.4426950408889634
TINY = 1e-30

SUBLANES = 8
LANES = 128
ROW_TILE = 512
SUB_ROWS = 128
VMEM_LIMIT = 56 * 1024 * 1024

GROUP_TILE = MOBA_BLOCK
TILES_PER_STEP = 32
HEADS_PER_STEP = 8
MERGE_TILES = 2
ROUTE_TILES = 8
ROUTE_BLK_SHIFT = 16
ROUTE_BLK_UNIT = 1 << ROUTE_BLK_SHIFT
HALF = HEAD_DIM // 2
VA_W = 2 * HEAD_DIM
L_LANE = HALF
HI16_MASK = -65536
BF16_HALF_ULP = 0x8000
NEG_INF_BITS = int(np.float32(NEG_INF).view(np.int32))

SC_CORES = 2
SC_SUBCORES = 16
SC_WORKERS = SC_CORES * SC_SUBCORES
IDX_CHUNK = 128
SC_HALVES = MOBA_BLOCK // IDX_CHUNK

F32 = jnp.float32
BF16 = jnp.bfloat16
I32 = jnp.int32


def _cparams(sem):
    return pltpu.CompilerParams(dimension_semantics=sem, vmem_limit_bytes=VMEM_LIMIT)


def _rms_norm(x, g):
    ms = jnp.mean(x * x, axis=-1, keepdims=True)
    return (x * lax.rsqrt(ms + EPS)) * g


def _silu(x):
    hx = 0.5 * x
    return hx * jnp.tanh(hx) + hx


def _sigmoid(x):
    return 0.5 * jnp.tanh(0.5 * x) + 0.5


def _nt_dot(a, b):
    return lax.dot_general(a, b, (((1,), (1,)), ((), ())), preferred_element_type=F32)


def _head_norm_rope(z, g, cosf, sinf):
    outs = []
    for h in range(N_HEADS):
        zh = z[:, h * HEAD_DIM:(h + 1) * HEAD_DIM]
        zh = _rms_norm(zh, g)
        outs.append(zh * cosf + pltpu.roll(zh, HEAD_DIM // 2, 1) * sinf)
    return outs


def _hawk_kernel(x_ref, g_ref, inw_ref, cw_ref, cb_ref, wri_ref, rb_ref, ib_ref, lam_ref, outw_ref,
                 o_ref, gate_ref, xpad_ref, a_ref, b_ref, hc_ref):
    tm, C = gate_ref.shape
    s = pl.program_id(1)

    @pl.when(s == 0)
    def _():
        xpad_ref[pl.ds(0, SUBLANES), :] = jnp.zeros((SUBLANES, xpad_ref.shape[1]), F32)
        hc_ref[...] = jnp.zeros_like(hc_ref)

    h = _rms_norm(x_ref[...], g_ref[...]).astype(BF16)
    gate_ref[...] = jnp.dot(h, inw_ref[:, C:], preferred_element_type=F32)

    xpad_ref[pl.ds(SUBLANES, tm), :] = jnp.dot(h, inw_ref[:, :C], preferred_element_type=F32)
    cw = cw_ref[...]
    xc = xpad_ref[pl.ds(SUBLANES - (CONV_WIDTH - 1), tm), :] * cw[0:1, :]
    for k in range(1, CONV_WIDTH):
        xc = xc + xpad_ref[pl.ds(SUBLANES - (CONV_WIDTH - 1) + k, tm), :] * cw[k:k + 1, :]
    xc = xc + cb_ref[...]
    xpad_ref[pl.ds(0, SUBLANES), :] = xpad_ref[pl.ds(tm, SUBLANES), :]

    lam = lam_ref[...]
    sp = jnp.maximum(-lam, 0.0) + jnp.log1p(jnp.exp(-jnp.abs(lam)))
    log2a_per_r = (-RG_C * LOG2_E) * sp
    xc16 = xc.astype(BF16)
    for g in range(N_RG_BLOCKS):
        lo = g * RG_BW
        z = jnp.dot(xc16[:, lo:lo + RG_BW], wri_ref[g], preferred_element_type=F32)
        r = _sigmoid(z[:, :RG_BW] + rb_ref[:, lo:lo + RG_BW])
        i = _sigmoid(z[:, RG_BW:] + ib_ref[:, lo:lo + RG_BW])
        a = jnp.exp2(r * log2a_per_r[:, lo:lo + RG_BW])
        om = 1.0 - a * a
        mult = om * lax.rsqrt(jnp.maximum(om, TINY))
        a_ref[:, lo:lo + RG_BW] = a
        b_ref[:, lo:lo + RG_BW] = mult * (i * xc[:, lo:lo + RG_BW])

    row = lax.broadcasted_iota(I32, (SUBLANES, a_ref.shape[1]), 0)

    def scan_body(c, hc):
        r0 = pl.multiple_of(c * SUBLANES, SUBLANES)
        a = a_ref[pl.ds(r0, SUBLANES), :]
        b = b_ref[pl.ds(r0, SUBLANES), :]
        for k in (1, 2, 4):
            keep = row >= k
            b = jnp.where(keep, a * pltpu.roll(b, k, 0) + b, b)
            a = jnp.where(keep, a * pltpu.roll(a, k, 0), a)
        h = a * hc + b
        b_ref[pl.ds(r0, SUBLANES), :] = h
        return h[SUBLANES - 1:SUBLANES, :]

    hc = lax.fori_loop(0, tm // SUBLANES, scan_body, hc_ref[0:1, :])
    hc_ref[0:1, :] = hc

    y = (b_ref[...] * _silu(gate_ref[...])).astype(BF16)
    o_ref[...] = x_ref[...] + jnp.dot(y, outw_ref[...], preferred_element_type=F32)


def _hawk_layer(x2d, B, S, g, in_w, cw, cb, wri, rb, ib, lam, out_w):
    T, D = x2d.shape
    C = in_w.shape[1] // 2
    tm = ROW_TILE
    ns = S // tm
    vec = lambda n: pl.BlockSpec((1, n), lambda b, s: (0, 0))
    return pl.pallas_call(
        _hawk_kernel,
        grid=(B, ns),
        in_specs=[pl.BlockSpec((tm, D), lambda b, s: (b * ns + s, 0)),
                  vec(D),
                  pl.BlockSpec((D, 2 * C), lambda b, s: (0, 0)),
                  pl.BlockSpec((CONV_WIDTH, C), lambda b, s: (0, 0)),
                  vec(C),
                  pl.BlockSpec((N_RG_BLOCKS, RG_BW, 2 * RG_BW), lambda b, s: (0, 0, 0)),
                  vec(C), vec(C), vec(C),
                  pl.BlockSpec((C, D), lambda b, s: (0, 0))],
        out_specs=pl.BlockSpec((tm, D), lambda b, s: (b * ns + s, 0)),
        out_shape=jax.ShapeDtypeStruct((T, D), F32),
        scratch_shapes=[pltpu.VMEM((tm, C), F32),
                        pltpu.VMEM((tm + SUBLANES, C), F32),
                        pltpu.VMEM((tm, C), F32),
                        pltpu.VMEM((tm, C), F32),
                        pltpu.VMEM((SUBLANES, C), F32)],
        compiler_params=_cparams(("arbitrary", "arbitrary")),
        name="hawk_layer",
    )(x2d, g, in_w, cw, cb, wri, rb, ib, lam, out_w)


def _kv_proj_kernel(x_ref, g_ref, w_ref, kg_ref, cos_ref, sin_ref, k_ref, v_ref, km_ref):
    tm = x_ref.shape[0]
    d_attn = N_HEADS * HEAD_DIM
    lane = lax.broadcasted_iota(I32, (SUB_ROWS, HEAD_DIM), 1)
    ksum = {}
    for t in range(tm // SUB_ROWS):
        rows = pl.ds(t * SUB_ROWS, SUB_ROWS)
        h = _rms_norm(x_ref[rows, :], g_ref[...]).astype(BF16)
        kv = jnp.dot(h, w_ref[...], preferred_element_type=F32)
        ks = _head_norm_rope(kv[:, :d_attn], kg_ref[...], cos_ref[rows, :], sin_ref[rows, :])
        for hd in range(N_HEADS):
            k_ref[0, hd, rows, :] = ks[hd].astype(BF16)
            vh = kv[:, d_attn + hd * HEAD_DIM:d_attn + (hd + 1) * HEAD_DIM]
            lo = jnp.where(lane < HALF, vh, jnp.where(lane == L_LANE, 1.0, 0.0))
            hi = jnp.where(lane < HALF, pltpu.roll(vh, HALF, 1), 0.0)
            v_ref[0, hd, rows, :] = jnp.concatenate([lo, hi], axis=1).astype(BF16)
            key = (hd, (t * SUB_ROWS) // MOBA_BLOCK)
            part = jnp.sum(ks[hd], axis=0, keepdims=True)
            ksum[key] = part if key not in ksum else ksum[key] + part
    for (hd, r), total in ksum.items():
        km_ref[0, r, pl.ds(hd, 1), :] = total * (1.0 / MOBA_BLOCK)


def _kv_proj(x2d, B, S, g, w, kg, cosf, sinf):
    T, D = x2d.shape
    tm = ROW_TILE
    ns = S // tm
    nb = S // MOBA_BLOCK
    rpt = tm // MOBA_BLOCK
    return pl.pallas_call(
        _kv_proj_kernel,
        grid=(B, ns),
        in_specs=[pl.BlockSpec((tm, D), lambda b, s: (b * ns + s, 0)),
                  pl.BlockSpec((1, D), lambda b, s: (0, 0)),
                  pl.BlockSpec((D, 2 * N_HEADS * HEAD_DIM), lambda b, s: (0, 0)),
                  pl.BlockSpec((1, HEAD_DIM), lambda b, s: (0, 0)),
                  pl.BlockSpec((tm, HEAD_DIM), lambda b, s: (s, 0)),
                  pl.BlockSpec((tm, HEAD_DIM), lambda b, s: (s, 0))],
        out_specs=[pl.BlockSpec((1, N_HEADS, tm, HEAD_DIM), lambda b, s: (b, 0, s, 0)),
                   pl.BlockSpec((1, N_HEADS, tm, VA_W), lambda b, s: (b, 0, s, 0)),
                   pl.BlockSpec((1, rpt, N_HEADS, HEAD_DIM), lambda b, s: (b, s, 0, 0))],
        out_shape=[jax.ShapeDtypeStruct((B, N_HEADS, S, HEAD_DIM), BF16),
                   jax.ShapeDtypeStruct((B, N_HEADS, S, VA_W), BF16),
                   jax.ShapeDtypeStruct((B, nb, N_HEADS, HEAD_DIM), F32)],
        compiler_params=_cparams(("arbitrary", "arbitrary")),
        name="kv_proj",
    )(x2d, g, w, kg, cosf, sinf)


def _q_proj_kernel(x_ref, g_ref, w_ref, qg_ref, cos_ref, sin_ref, q_ref, sg_ref):
    d_attn = N_HEADS * HEAD_DIM
    for t in range(x_ref.shape[0] // SUB_ROWS):
        rows = pl.ds(t * SUB_ROWS, SUB_ROWS)
        h = _rms_norm(x_ref[rows, :], g_ref[...]).astype(BF16)
        u = jnp.dot(h, w_ref[...], preferred_element_type=F32)
        qs = _head_norm_rope(u[:, :d_attn], qg_ref[...], cos_ref[rows, :], sin_ref[rows, :])
        for hd in range(N_HEADS):
            q_ref[0, hd, rows, :] = qs[hd]
        sg_ref[rows, :] = _silu(u[:, d_attn:])


def _q_proj(x2d, B, S, g, w, qg, cosf, sinf):
    T, D = x2d.shape
    tm = ROW_TILE
    ns = S // tm
    d_attn = N_HEADS * HEAD_DIM
    return pl.pallas_call(
        _q_proj_kernel,
        grid=(B, ns),
        in_specs=[pl.BlockSpec((tm, D), lambda b, s: (b * ns + s, 0)),
                  pl.BlockSpec((1, D), lambda b, s: (0, 0)),
                  pl.BlockSpec((D, 2 * d_attn), lambda b, s: (0, 0)),
                  pl.BlockSpec((1, HEAD_DIM), lambda b, s: (0, 0)),
                  pl.BlockSpec((tm, HEAD_DIM), lambda b, s: (s, 0)),
                  pl.BlockSpec((tm, HEAD_DIM), lambda b, s: (s, 0))],
        out_specs=[pl.BlockSpec((1, N_HEADS, tm, HEAD_DIM), lambda b, s: (b, 0, s, 0)),
                   pl.BlockSpec((tm, d_attn), lambda b, s: (b * ns + s, 0))],
        out_shape=[jax.ShapeDtypeStruct((B, N_HEADS, S, HEAD_DIM), F32),
                   jax.ShapeDtypeStruct((T, d_attn), F32)],
        compiler_params=_cparams(("arbitrary", "arbitrary")),
        name="q_proj",
    )(x2d, g, w, qg, cosf, sinf)


def _moba_route_kernel(q_ref, km_ref, rt_ref, cnt_ref, run_ref):
    step = pl.program_id(2)
    nb = km_ref.shape[2]

    @pl.when(step == 0)
    def _():
        run_ref[...] = jnp.zeros_like(run_ref)

    km = km_ref[0, 0].astype(BF16)
    blk = lax.broadcasted_iota(I32, (nb, MOBA_BLOCK), 0)
    qa = lax.broadcasted_iota(I32, (MOBA_BLOCK, MOBA_BLOCK), 0)
    qb = lax.broadcasted_iota(I32, (MOBA_BLOCK, MOBA_BLOCK), 1)
    earlier = jnp.where(qa < qb, 1.0, 0.0).astype(BF16)
    run = run_ref[...]
    for u in range(ROUTE_TILES):
        j = step * ROUTE_TILES + u
        g = _nt_dot(km, q_ref[0, 0, pl.ds(u * MOBA_BLOCK, MOBA_BLOCK), :].astype(BF16))
        g = jnp.where(blk < j, g, -jnp.inf)
        hits, ids = [], []
        for _ in range(MOBA_TOPK):
            mx = jnp.max(g, axis=0, keepdims=True)
            idx = jnp.min(jnp.where(g == mx, blk, nb), axis=0, keepdims=True)
            hit = blk == idx
            ok = mx > -jnp.inf
            hits.append(jnp.where(hit & ok, 1.0, 0.0))
            ids.append(jnp.where(ok, idx, -1))
            g = jnp.where(hit, -jnp.inf, g)
        sel = hits[0] + hits[1] + hits[2]

        rank = jnp.dot(sel.astype(BF16), earlier, preferred_element_type=F32)
        dest = run + rank
        run = run + jnp.sum(sel, axis=1, keepdims=True)
        for r in range(MOBA_TOPK):
            loc = jnp.sum(hits[r] * dest, axis=0, keepdims=True).astype(I32)
            rt_ref[u, pl.ds(r, 1), :] = jnp.where(ids[r] >= 0, ids[r] * ROUTE_BLK_UNIT + loc, -1)
        rt_ref[u, pl.ds(MOBA_TOPK, SUBLANES - MOBA_TOPK), :] = jnp.full(
            (SUBLANES - MOBA_TOPK, MOBA_BLOCK), -1, I32)
    run_ref[...] = run
    cnt_ref[0] = run.astype(I32)


def _moba_route(q, kmean, b):
    _, H, S, Dh = q.shape
    nb = S // MOBA_BLOCK
    rows = ROUTE_TILES * MOBA_BLOCK
    nsteps = nb // ROUTE_TILES
    return pl.pallas_call(
        _moba_route_kernel,
        grid=(1, H, nsteps),
        in_specs=[pl.BlockSpec((1, 1, rows, Dh), lambda _, h, j: (b, h, j, 0)),
                  pl.BlockSpec((1, 1, nb, Dh), lambda _, h, j: (b, h, 0, 0))],
        out_specs=[pl.BlockSpec((ROUTE_TILES, SUBLANES, MOBA_BLOCK), lambda _, h, j: (h * nsteps + j, 0, 0)),
                   pl.BlockSpec((1, nb, 1), lambda _, h, j: (h, 0, 0))],
        out_shape=[jax.ShapeDtypeStruct((H * nb, SUBLANES, MOBA_BLOCK), I32),
                   jax.ShapeDtypeStruct((H, nb, 1), I32)],
        scratch_shapes=[pltpu.VMEM((nb, 1), F32)],
        compiler_params=_cparams(("arbitrary", "arbitrary", "arbitrary")),
        name="moba_route",
    )(q, kmean)


def _moba_layout_kernel(cnt_ref, rt_ref, pos_ref, tblk_ref, tvalid_ref, nt_ref, base_ref, *, max_tiles):
    g = pl.program_id(0)
    nb = rt_ref.shape[0]
    slab_rows = max_tiles * GROUP_TILE
    null_row0 = (max_tiles - 1) * GROUP_TILE

    def per_block(n, first_tile):
        count = cnt_ref[g * nb + n]
        tiles = (count + (GROUP_TILE - 1)) // GROUP_TILE
        base_ref[n] = first_tile * GROUP_TILE

        def per_tile(i, carry):
            tblk_ref[g * max_tiles + first_tile + i] = n
            tvalid_ref[g * max_tiles + first_tile + i] = jnp.minimum(count - i * GROUP_TILE, GROUP_TILE)
            return carry

        lax.fori_loop(0, tiles, per_tile, 0)
        return first_tile + tiles

    used = lax.fori_loop(0, nb, per_block, 0)
    nt_ref[g] = used

    def unused_tile(t, carry):
        tblk_ref[g * max_tiles + t] = nb - 1
        tvalid_ref[g * max_tiles + t] = 0
        return carry

    lax.fori_loop(used, max_tiles, unused_tile, 0)

    code = rt_ref[...].reshape(nb * SUBLANES, MOBA_BLOCK)
    blk = lax.shift_right_arithmetic(code, ROUTE_BLK_SHIFT)
    base = jnp.zeros_like(code)
    for n in range(nb):
        base = jnp.where(blk == n, base_ref[n], base)
    lane = lax.broadcasted_iota(I32, code.shape, 1)
    row = jnp.where(code >= 0, base + (code & (ROUTE_BLK_UNIT - 1)), null_row0 + lane)
    pos_ref[...] = (row + g * slab_rows).reshape(pos_ref.shape)


def _moba_layout(rt, counts, nb, max_tiles):
    G = rt.shape[0] // nb
    smem = pl.BlockSpec(memory_space=pltpu.SMEM)
    return pl.pallas_call(
        functools.partial(_moba_layout_kernel, max_tiles=max_tiles),
        grid_spec=pltpu.PrefetchScalarGridSpec(
            num_scalar_prefetch=1,
            grid=(G,),
            in_specs=[pl.BlockSpec((nb, SUBLANES, MOBA_BLOCK), lambda g, cnt: (g, 0, 0))],
            out_specs=[pl.BlockSpec((nb, SUBLANES, MOBA_BLOCK), lambda g, cnt: (g, 0, 0)), smem, smem, smem],
            scratch_shapes=[pltpu.SMEM((nb,), I32)],
        ),
        out_shape=[jax.ShapeDtypeStruct((G * nb, SUBLANES, MOBA_BLOCK), I32),
                   jax.ShapeDtypeStruct((G * max_tiles,), I32),
                   jax.ShapeDtypeStruct((G * max_tiles,), I32),
                   jax.ShapeDtypeStruct((G,), I32)],
        compiler_params=_cparams(("arbitrary",)),
        name="moba_layout",
    )(counts, rt)


def _sc_mesh():
    return plsc.VectorSubcoreMesh(core_axis_name="c", subcore_axis_name="s")


def _sc_load_indices(pos_hbm, idx_v, t0, n_tiles):
    for c in range(SC_HALVES):
        pltpu.sync_copy(pos_hbm.at[pl.ds(t0, n_tiles), pl.ds(0, MOBA_TOPK), pl.ds(c * IDX_CHUNK, IDX_CHUNK)],
                        idx_v.at[c])


def _sc_scatter_rows(src, src_tile0, pos, n_out_rows):
    D = src.shape[1]
    n_tiles = pos.shape[0]
    per_w = n_tiles // SC_WORKERS
    assert per_w * SC_WORKERS == n_tiles

    @functools.partial(
        pl.kernel, mesh=_sc_mesh(),
        out_type=jax.ShapeDtypeStruct((n_out_rows, D), src.dtype),
        scratch_types=[pltpu.VMEM((SC_HALVES, per_w, MOBA_TOPK, IDX_CHUNK), I32),
                       pltpu.VMEM((SC_HALVES, IDX_CHUNK, D), src.dtype),
                       pltpu.SemaphoreType.DMA((SC_HALVES,)),
                       pltpu.SemaphoreType.DMA((SC_HALVES,))],
        name="sc_scatter_rows",
    )
    def k(src_hbm, pos_hbm, out_hbm, idx_v, rows_v, load_sem, scat_sem):
        t0 = (lax.axis_index("s") * SC_CORES + lax.axis_index("c")) * per_w
        _sc_load_indices(pos_hbm, idx_v, t0, per_w)

        def load(i, c):
            row0 = pl.multiple_of((src_tile0 + t0 + i) * MOBA_BLOCK + c * IDX_CHUNK, IDX_CHUNK)
            return pltpu.make_async_copy(src_hbm.at[pl.ds(row0, IDX_CHUNK)], rows_v.at[c], load_sem.at[c])

        def scatters(i, c):
            return [pltpu.make_async_copy(rows_v.at[c], out_hbm.at[idx_v.at[c, i, r]], scat_sem.at[c])
                    for r in range(MOBA_TOPK)]

        load(0, 0).start()

        @pl.loop(0, per_w)
        def _(i):
            for c in range(SC_HALVES):
                pi, pc = (i, 0) if c else (i - 1, 1)
                ni, nc = (i + 1, 0) if c else (i, 1)
                load(i, c).wait()
                for cp in scatters(i, c):
                    cp.start()

                @pl.when(pi >= 0)
                def _():
                    for cp in scatters(pi, pc):
                        cp.wait()

                @pl.when(ni < per_w)
                def _():
                    load(ni, nc).start()

        for cp in scatters(per_w - 1, SC_HALVES - 1):
            cp.wait()

    return k(src, pos)


def _sc_gather_rows(table, pos):
    R, D = table.shape
    n_tiles = pos.shape[0]
    N = n_tiles * MOBA_BLOCK
    per_w = n_tiles // SC_WORKERS
    assert per_w * SC_WORKERS == n_tiles

    items = SC_HALVES * MOBA_TOPK
    n_buf = 2
    assert items % n_buf == 0

    @functools.partial(
        pl.kernel, mesh=_sc_mesh(),
        out_type=jax.ShapeDtypeStruct((MOBA_TOPK, N, D), table.dtype),
        scratch_types=[pltpu.VMEM((SC_HALVES, per_w, MOBA_TOPK, IDX_CHUNK), I32),
                       pltpu.VMEM((n_buf, IDX_CHUNK, D), table.dtype),
                       pltpu.SemaphoreType.DMA((n_buf,)),
                       pltpu.SemaphoreType.DMA((n_buf,))],
        name="sc_gather_rows",
    )
    def k(table_hbm, pos_hbm, out_hbm, idx_v, rows_v, gat_sem, put_sem):
        t0 = (lax.axis_index("s") * SC_CORES + lax.axis_index("c")) * per_w
        _sc_load_indices(pos_hbm, idx_v, t0, per_w)

        def gather(i, j):
            c, r = divmod(j, MOBA_TOPK)
            return pltpu.make_async_copy(table_hbm.at[idx_v.at[c, i, r]], rows_v.at[j % n_buf], gat_sem.at[j % n_buf])

        def put(i, j):
            c, r = divmod(j, MOBA_TOPK)
            row0 = pl.multiple_of((t0 + i) * MOBA_BLOCK + c * IDX_CHUNK, IDX_CHUNK)
            return pltpu.make_async_copy(rows_v.at[j % n_buf], out_hbm.at[r, pl.ds(row0, IDX_CHUNK)],
                                         put_sem.at[j % n_buf])

        gather(0, 0).start()

        @pl.loop(0, per_w)
        def _(i):
            for j in range(items):
                pi, pj = (i, j - 1) if j else (i - 1, items - 1)
                ni, nj = (i, j + 1) if j + 1 < items else (i + 1, 0)
                gather(i, j).wait()
                put(i, j).start()

                @pl.when(pi >= 0)
                def _():
                    put(pi, pj).wait()

                @pl.when(ni < per_w)
                def _():
                    gather(ni, nj).start()

        put(per_w - 1, items - 1).wait()

    return k(table, pos)


def _bits(x):
    return lax.bitcast_convert_type(x, I32)


def _pack_partial(part, m):
    lane = lax.broadcasted_iota(I32, (part.shape[0], HEAD_DIM), 1)
    lo = _bits(part[:, :HEAD_DIM]) + BF16_HALF_ULP
    hi = _bits(jnp.where(lane == L_LANE, m, part[:, HEAD_DIM:])) + BF16_HALF_ULP
    return (hi & HI16_MASK) | lax.shift_right_logical(lo, 16)


def _null_partial(rows):
    lane = lax.broadcasted_iota(I32, (rows, HEAD_DIM), 1)
    return jnp.where(lane == L_LANE, NEG_INF_BITS & HI16_MASK, 0)


def _unpack_partial(word):
    lo = lax.bitcast_convert_type(lax.shift_left(word, 16), F32)
    hi = lax.bitcast_convert_type(word & HI16_MASK, F32)
    return lo, hi


def _moba_group_kernel(tblk_ref, tvalid_ref, nt_ref, qg_ref, k_ref, v_ref, o_ref, *, max_tiles):
    g, step = pl.program_id(0), pl.program_id(1)
    t0 = g * max_tiles + step * TILES_PER_STEP
    scale = HEAD_DIM ** -0.5

    @pl.when(tvalid_ref[t0] > 0)
    def _():
        for u in range(TILES_PER_STEP):
            rows = pl.ds(u * GROUP_TILE, GROUP_TILE)
            q = qg_ref[rows, :].astype(BF16)
            n0 = pl.multiple_of(tblk_ref[t0 + u] * MOBA_BLOCK, MOBA_BLOCK)
            s = _nt_dot(q, k_ref[0, 0, pl.ds(n0, MOBA_BLOCK), :])
            m = (jnp.max(s, axis=1, keepdims=True) * scale).astype(BF16).astype(F32)
            p = jnp.exp2(s * (scale * LOG2_E) - m * LOG2_E).astype(BF16)
            part = jnp.dot(p, v_ref[0, 0, pl.ds(n0, MOBA_BLOCK), :], preferred_element_type=F32)
            o_ref[rows, :] = _pack_partial(part, m)

    @pl.when(step == pl.num_programs(1) - 1)
    def _():
        o_ref[pl.ds((TILES_PER_STEP - 1) * GROUP_TILE, GROUP_TILE), :] = _null_partial(GROUP_TILE)


def _moba_group(qg, k, vaug, b, tblk, tvalid, ntiles, max_tiles):
    _, H, S, Dh = k.shape
    G = H
    n_steps = max_tiles // TILES_PER_STEP
    step_rows = TILES_PER_STEP * GROUP_TILE

    def tile_idx(g, s, tblk, tvalid, nt):
        return (g * n_steps + jnp.where(s * TILES_PER_STEP < nt[g], s, n_steps - 1), 0)

    return pl.pallas_call(
        functools.partial(_moba_group_kernel, max_tiles=max_tiles),
        grid_spec=pltpu.PrefetchScalarGridSpec(
            num_scalar_prefetch=3,
            grid=(G, n_steps),
            in_specs=[pl.BlockSpec((step_rows, Dh), tile_idx),
                      pl.BlockSpec((1, 1, S, Dh), lambda g, s, *_: (b, g, 0, 0)),
                      pl.BlockSpec((1, 1, S, VA_W), lambda g, s, *_: (b, g, 0, 0))],
            out_specs=pl.BlockSpec((step_rows, LANES), tile_idx),
        ),
        out_shape=jax.ShapeDtypeStruct((G * max_tiles * GROUP_TILE, LANES), I32),
        compiler_params=_cparams(("arbitrary", "arbitrary")),
        name="moba_group",
    )(tblk, tvalid, ntiles, qg, k, vaug)


def _moba_merge_kernel(*refs, n_skip):
    q_ref, k_ref, v_ref, sg_ref = refs[n_skip:n_skip + 4]
    og_refs, o_ref = refs[n_skip + 4:-1], refs[-1]
    scale = HEAD_DIM ** -0.5
    qi = lax.broadcasted_iota(I32, (MOBA_BLOCK, MOBA_BLOCK), 0)
    ki = lax.broadcasted_iota(I32, (MOBA_BLOCK, MOBA_BLOCK), 1)
    lane = lax.broadcasted_iota(I32, (MOBA_BLOCK, HEAD_DIM), 1)
    spread = jnp.where(lax.broadcasted_iota(I32, (HEAD_DIM, HEAD_DIM), 0) == L_LANE, 1.0, 0.0).astype(BF16)
    for u, t in [(u, t) for t in range(MERGE_TILES) for u in range(len(og_refs))]:
        og_ref = og_refs[u]
        rows = pl.ds(t * MOBA_BLOCK, MOBA_BLOCK)
        q = q_ref[0, u, rows, :].astype(BF16)
        s = _nt_dot(q, k_ref[0, u, rows, :])
        s = jnp.where(ki <= qi, s, NEG_INF)
        m_raw = jnp.max(s, axis=1, keepdims=True)
        p = jnp.exp2((s - m_raw) * (scale * LOG2_E)).astype(BF16)
        own = jnp.dot(p, v_ref[0, u, rows, :], preferred_element_type=F32)
        parts = [_unpack_partial(og_ref[r, rows, :]) for r in range(MOBA_TOPK)]
        ms = [jnp.dot(jnp.where(lane == L_LANE, hi, 0.0).astype(BF16), spread, preferred_element_type=F32)
              for _, hi in parts]
        m_own = jnp.broadcast_to(m_raw * scale, lane.shape)
        m_all = m_own
        for m in ms:
            m_all = jnp.maximum(m_all, m)
        w_own = jnp.exp2((m_own - m_all) * LOG2_E)
        tot_lo = w_own * own[:, :HEAD_DIM]
        tot_hi = w_own * own[:, HEAD_DIM:]
        for (lo, hi), m in zip(parts, ms):
            w = jnp.exp2((m - m_all) * LOG2_E)
            tot_lo = tot_lo + w * lo
            tot_hi = tot_hi + w * hi
        acc = jnp.where(lane < HALF, tot_lo, pltpu.roll(tot_hi, HALF, 1))
        cols = pl.ds(u * HEAD_DIM, HEAD_DIM)
        o_ref[rows, cols] = ((acc / tot_lo[:, L_LANE:L_LANE + 1]) * sg_ref[rows, cols]).astype(BF16)


def _moba_merge(q, k, vaug, sg, og, b, o_prev):
    B, H, S, Dh = q.shape
    nb = S // MOBA_BLOCK
    hps = HEADS_PER_STEP
    rows = MERGE_TILES * MOBA_BLOCK
    ns = S // rows
    og_spec = lambda u: pl.BlockSpec((MOBA_TOPK, rows, LANES),
                                     lambda _, hp, j: (0, (hp * hps + u) * ns + j, 0))
    prev_specs = [] if o_prev is None else [pl.BlockSpec(memory_space=pl.ANY)]
    prev_args = [] if o_prev is None else [o_prev]
    return pl.pallas_call(
        functools.partial(_moba_merge_kernel, n_skip=len(prev_args)),
        grid=(1, H // hps, ns),
        in_specs=prev_specs
                 + [pl.BlockSpec((1, hps, rows, Dh), lambda _, hp, j: (b, hp, j, 0)),
                    pl.BlockSpec((1, hps, rows, Dh), lambda _, hp, j: (b, hp, j, 0)),
                    pl.BlockSpec((1, hps, rows, VA_W), lambda _, hp, j: (b, hp, j, 0)),
                    pl.BlockSpec((rows, hps * Dh), lambda _, hp, j: (b * ns + j, hp))]
                 + [og_spec(u) for u in range(hps)],
        out_specs=pl.BlockSpec((rows, hps * Dh), lambda _, hp, j: (b * ns + j, hp)),
        out_shape=jax.ShapeDtypeStruct((B * S, H * Dh), BF16),
        input_output_aliases={} if o_prev is None else {0: 0},
        compiler_params=_cparams(("arbitrary", "arbitrary", "arbitrary")),
        name="moba_merge",
    )(*prev_args, q, k, vaug, sg, *([og] * hps))


def _moba_attention(q, k, vaug, kmean, sg):
    B, H, S, Dh = q.shape
    nb = S // MOBA_BLOCK
    max_tiles = (MOBA_TOPK * S) // GROUP_TILE + nb + 1
    max_tiles = -(-max_tiles // TILES_PER_STEP) * TILES_PER_STEP
    slab_rows = max_tiles * GROUP_TILE

    q_rows = q.reshape(B * H * S, Dh)
    o = None
    for b in range(B):
        rt, counts = _moba_route(q, kmean, b)
        pos, tblk, tvalid, ntiles = _moba_layout(rt, counts.reshape(-1), nb, max_tiles)
        qg = _sc_scatter_rows(q_rows, b * H * nb, pos, H * slab_rows)
        parts = _moba_group(qg, k, vaug, b, tblk, tvalid, ntiles, max_tiles)
        og = _sc_gather_rows(parts, pos)
        o = _moba_merge(q, k, vaug, sg, og, b, o)
    return o


def _proj_residual_kernel(y_ref, w_ref, x_ref, out_ref):
    out_ref[...] = x_ref[...] + jnp.dot(y_ref[...], w_ref[...], preferred_element_type=F32)


def _proj_residual(y, w, x2d):
    T, D = x2d.shape
    K = y.shape[1]
    return pl.pallas_call(
        _proj_residual_kernel,
        grid=(T // ROW_TILE,),
        in_specs=[pl.BlockSpec((ROW_TILE, K), lambda i: (i, 0)),
                  pl.BlockSpec((K, D), lambda i: (0, 0)),
                  pl.BlockSpec((ROW_TILE, D), lambda i: (i, 0))],
        out_specs=pl.BlockSpec((ROW_TILE, D), lambda i: (i, 0)),
        out_shape=jax.ShapeDtypeStruct((T, D), F32),
        compiler_params=_cparams(("arbitrary",)),
        name="moba_out",
    )(y, w, x2d)


def _rope_tables(S):
    inv = np.float32(ROPE_THETA) ** (-np.arange(HALF, dtype=np.float32) / np.float32(HALF))
    ang = np.arange(S, dtype=np.float32)[:, None] * inv[None, :].astype(np.float32)
    cos, sin = np.cos(ang).astype(np.float32), np.sin(ang).astype(np.float32)
    return (jnp.asarray(np.concatenate([cos, cos], axis=-1)),
            jnp.asarray(np.concatenate([-sin, sin], axis=-1)))


def kernel(x, a_norm, a_in_w, a_conv_w, a_conv_b, a_r_w, a_r_b, a_i_w, a_i_b, a_lambda, a_out_w,
           kv_norm, kv_w, k_norm, b_norm, b_in_w, q_norm, b_out_w):
    B, S, D = x.shape
    assert S % ROW_TILE == 0 and ROW_TILE % MOBA_BLOCK == 0
    assert D == N_RG_BLOCKS * RG_BW == N_HEADS * HEAD_DIM
    assert (B * N_HEADS * S) % (IDX_CHUNK * SC_WORKERS) == 0
    xs = x.reshape(B * S, D)
    row = lambda v: v.reshape(1, -1)

    for l in range(a_in_w.shape[0]):
        wri = jnp.concatenate([a_r_w[l], a_i_w[l]], axis=-1).astype(BF16)
        xs = _hawk_layer(xs, B, S, row(a_norm[l]), a_in_w[l].astype(BF16), a_conv_w[l], row(a_conv_b[l]), wri,
                         row(a_r_b[l]), row(a_i_b[l]), row(a_lambda[l]), a_out_w[l].astype(BF16))

    cosf, sinf = _rope_tables(S)
    k, vaug, kmean = _kv_proj(xs, B, S, row(kv_norm), kv_w.astype(BF16), row(k_norm), cosf, sinf)
    kmean = jnp.transpose(kmean, (0, 2, 1, 3))

    for jl in range(b_in_w.shape[0]):
        q, sg = _q_proj(xs, B, S, row(b_norm[jl]), b_in_w[jl].astype(BF16), row(q_norm[jl]), cosf, sinf)
        y = _moba_attention(q, k, vaug, kmean, sg)
        xs = _proj_residual(y, b_out_w[jl].astype(BF16), xs)
    return xs.reshape(B, S, D)
```

```python
import functools

import jax
import jax.numpy as jnp
import numpy as np
from jax import lax
from jax.experimental import pallas as pl
from jax.experimental.pallas import tpu as pltpu
from jax.experimental.pallas import tpu_sc as plsc

N_HEADS = 8
HEAD_DIM = 128
MOBA_BLOCK = 256
MOBA_TOPK = 3
CONV_WIDTH = 4
N_RG_BLOCKS = 8
RG_BW = 128
RG_C = 8.0
ROPE_THETA = 10000.0
EPS = 1e-6
NEG_INF = -1e30
LOG2_E = 1.4426950408889634
TINY = 1e-30

SUBLANES = 8
LANES = 128
ROW_TILE = 512
SUB_ROWS = 128
VMEM_LIMIT = 56 * 1024 * 1024

GROUP_TILE = MOBA_BLOCK
TILES_PER_STEP = 32
HEADS_PER_STEP = 8
MERGE_TILES = 2
ROUTE_TILES = 8
ROUTE_BLK_SHIFT = 16
ROUTE_BLK_UNIT = 1 << ROUTE_BLK_SHIFT
HALF = HEAD_DIM // 2
VA_W = 2 * HEAD_DIM
L_LANE = HALF
HI16_MASK = -65536
BF16_HALF_ULP = 0x8000
NEG_INF_BITS = int(np.float32(NEG_INF).view(np.int32))

SC_CORES = 2
SC_SUBCORES = 16
SC_WORKERS = SC_CORES * SC_SUBCORES
IDX_CHUNK = 128
SC_HALVES = MOBA_BLOCK // IDX_CHUNK

F32 = jnp.float32
BF16 = jnp.bfloat16
I32 = jnp.int32


def _cparams(sem):
    return pltpu.CompilerParams(dimension_semantics=sem, vmem_limit_bytes=VMEM_LIMIT)


def _rms_norm(x, g):
    ms = jnp.mean(x * x, axis=-1, keepdims=True)
    return (x * lax.rsqrt(ms + EPS)) * g


def _silu(x):
    hx = 0.5 * x
    return hx * jnp.tanh(hx) + hx


def _sigmoid(x):
    return 0.5 * jnp.tanh(0.5 * x) + 0.5


def _nt_dot(a, b):
    return lax.dot_general(a, b, (((1,), (1,)), ((), ())), preferred_element_type=F32)


def _head_norm_rope(z, g, cosf, sinf):
    outs = []
    for h in range(N_HEADS):
        zh = z[:, h * HEAD_DIM:(h + 1) * HEAD_DIM]
        zh = _rms_norm(zh, g)
        outs.append(zh * cosf + pltpu.roll(zh, HEAD_DIM // 2, 1) * sinf)
    return outs


def _hawk_kernel(x_ref, g_ref, inw_ref, cw_ref, cb_ref, wri_ref, rb_ref, ib_ref, lam_ref, outw_ref,
                 o_ref, gate_ref, xpad_ref, a_ref, b_ref, hc_ref):
    tm, C = gate_ref.shape
    s = pl.program_id(1)

    @pl.when(s == 0)
    def _():
        xpad_ref[pl.ds(0, SUBLANES), :] = jnp.zeros((SUBLANES, xpad_ref.shape[1]), F32)
        hc_ref[...] = jnp.zeros_like(hc_ref)

    h = _rms_norm(x_ref[...], g_ref[...]).astype(BF16)
    gate_ref[...] = jnp.dot(h, inw_ref[:, C:], preferred_element_type=F32)

    xpad_ref[pl.ds(SUBLANES, tm), :] = jnp.dot(h, inw_ref[:, :C], preferred_element_type=F32)
    cw = cw_ref[...]
    xc = xpad_ref[pl.ds(SUBLANES - (CONV_WIDTH - 1), tm), :] * cw[0:1, :]
    for k in range(1, CONV_WIDTH):
        xc = xc + xpad_ref[pl.ds(SUBLANES - (CONV_WIDTH - 1) + k, tm), :] * cw[k:k + 1, :]
    xc = xc + cb_ref[...]
    xpad_ref[pl.ds(0, SUBLANES), :] = xpad_ref[pl.ds(tm, SUBLANES), :]

    lam = lam_ref[...]
    sp = jnp.maximum(-lam, 0.0) + jnp.log1p(jnp.exp(-jnp.abs(lam)))
    log2a_per_r = (-RG_C * LOG2_E) * sp
    xc16 = xc.astype(BF16)
    for g in range(N_RG_BLOCKS):
        lo = g * RG_BW
        z = jnp.dot(xc16[:, lo:lo + RG_BW], wri_ref[g], preferred_element_type=F32)
        r = _sigmoid(z[:, :RG_BW] + rb_ref[:, lo:lo + RG_BW])
        i = _sigmoid(z[:, RG_BW:] + ib_ref[:, lo:lo + RG_BW])
        a = jnp.exp2(r * log2a_per_r[:, lo:lo + RG_BW])
        om = 1.0 - a * a
        mult = om * lax.rsqrt(jnp.maximum(om, TINY))
        a_ref[:, lo:lo + RG_BW] = a
        b_ref[:, lo:lo + RG_BW] = mult * (i * xc[:, lo:lo + RG_BW])

    row = lax.broadcasted_iota(I32, (SUBLANES, a_ref.shape[1]), 0)

    def scan_body(c, hc):
        r0 = pl.multiple_of(c * SUBLANES, SUBLANES)
        a = a_ref[pl.ds(r0, SUBLANES), :]
        b = b_ref[pl.ds(r0, SUBLANES), :]
        for k in (1, 2, 4):
            keep = row >= k
            b = jnp.where(keep, a * pltpu.roll(b, k, 0) + b, b)
            a = jnp.where(keep, a * pltpu.roll(a, k, 0), a)
        h = a * hc + b
        b_ref[pl.ds(r0, SUBLANES), :] = h
        return h[SUBLANES - 1:SUBLANES, :]

    hc = lax.fori_loop(0, tm // SUBLANES, scan_body, hc_ref[0:1, :])
    hc_ref[0:1, :] = hc

    y = (b_ref[...] * _silu(gate_ref[...])).astype(BF16)
    o_ref[...] = x_ref[...] + jnp.dot(y, outw_ref[...], preferred_element_type=F32)


def _hawk_layer(x2d, B, S, g, in_w, cw, cb, wri, rb, ib, lam, out_w):
    T, D = x2d.shape
    C = in_w.shape[1] // 2
    tm = ROW_TILE
    ns = S // tm
    vec = lambda n: pl.BlockSpec((1, n), lambda b, s: (0, 0))
    return pl.pallas_call(
        _hawk_kernel,
        grid=(B, ns),
        in_specs=[pl.BlockSpec((tm, D), lambda b, s: (b * ns + s, 0)),
                  vec(D),
                  pl.BlockSpec((D, 2 * C), lambda b, s: (0, 0)),
                  pl.BlockSpec((CONV_WIDTH, C), lambda b, s: (0, 0)),
                  vec(C),
                  pl.BlockSpec((N_RG_BLOCKS, RG_BW, 2 * RG_BW), lambda b, s: (0, 0, 0)),
                  vec(C), vec(C), vec(C),
                  pl.BlockSpec((C, D), lambda b, s: (0, 0))],
        out_specs=pl.BlockSpec((tm, D), lambda b, s: (b * ns + s, 0)),
        out_shape=jax.ShapeDtypeStruct((T, D), F32),
        scratch_shapes=[pltpu.VMEM((tm, C), F32),
                        pltpu.VMEM((tm + SUBLANES, C), F32),
                        pltpu.VMEM((tm, C), F32),
                        pltpu.VMEM((tm, C), F32),
                        pltpu.VMEM((SUBLANES, C), F32)],
        compiler_params=_cparams(("arbitrary", "arbitrary")),
        name="hawk_layer",
    )(x2d, g, in_w, cw, cb, wri, rb, ib, lam, out_w)


def _kv_proj_kernel(x_ref, g_ref, w_ref, kg_ref, cos_ref, sin_ref, k_ref, v_ref, km_ref):
    tm = x_ref.shape[0]
    d_attn = N_HEADS * HEAD_DIM
    lane = lax.broadcasted_iota(I32, (SUB_ROWS, HEAD_DIM), 1)
    ksum = {}
    for t in range(tm // SUB_ROWS):
        rows = pl.ds(t * SUB_ROWS, SUB_ROWS)
        h = _rms_norm(x_ref[rows, :], g_ref[...]).astype(BF16)
        kv = jnp.dot(h, w_ref[...], preferred_element_type=F32)
        ks = _head_norm_rope(kv[:, :d_attn], kg_ref[...], cos_ref[rows, :], sin_ref[rows, :])
        for hd in range(N_HEADS):
            k_ref[0, hd, rows, :] = ks[hd].astype(BF16)
            vh = kv[:, d_attn + hd * HEAD_DIM:d_attn + (hd + 1) * HEAD_DIM]
            lo = jnp.where(lane < HALF, vh, jnp.where(lane == L_LANE, 1.0, 0.0))
            hi = jnp.where(lane < HALF, pltpu.roll(vh, HALF, 1), 0.0)
            v_ref[0, hd, rows, :] = jnp.concatenate([lo, hi], axis=1).astype(BF16)
            key = (hd, (t * SUB_ROWS) // MOBA_BLOCK)
            part = jnp.sum(ks[hd], axis=0, keepdims=True)
            ksum[key] = part if key not in ksum else ksum[key] + part
    for (hd, r), total in ksum.items():
        km_ref[0, r, pl.ds(hd, 1), :] = total * (1.0 / MOBA_BLOCK)


def _kv_proj(x2d, B, S, g, w, kg, cosf, sinf):
    T, D = x2d.shape
    tm = ROW_TILE
    ns = S // tm
    nb = S // MOBA_BLOCK
    rpt = tm // MOBA_BLOCK
    return pl.pallas_call(
        _kv_proj_kernel,
        grid=(B, ns),
        in_specs=[pl.BlockSpec((tm, D), lambda b, s: (b * ns + s, 0)),
                  pl.BlockSpec((1, D), lambda b, s: (0, 0)),
                  pl.BlockSpec((D, 2 * N_HEADS * HEAD_DIM), lambda b, s: (0, 0)),
                  pl.BlockSpec((1, HEAD_DIM), lambda b, s: (0, 0)),
                  pl.BlockSpec((tm, HEAD_DIM), lambda b, s: (s, 0)),
                  pl.BlockSpec((tm, HEAD_DIM), lambda b, s: (s, 0))],
        out_specs=[pl.BlockSpec((1, N_HEADS, tm, HEAD_DIM), lambda b, s: (b, 0, s, 0)),
                   pl.BlockSpec((1, N_HEADS, tm, VA_W), lambda b, s: (b, 0, s, 0)),
                   pl.BlockSpec((1, rpt, N_HEADS, HEAD_DIM), lambda b, s: (b, s, 0, 0))],
        out_shape=[jax.ShapeDtypeStruct((B, N_HEADS, S, HEAD_DIM), BF16),
                   jax.ShapeDtypeStruct((B, N_HEADS, S, VA_W), BF16),
                   jax.ShapeDtypeStruct((B, nb, N_HEADS, HEAD_DIM), F32)],
        compiler_params=_cparams(("arbitrary", "arbitrary")),
        name="kv_proj",
    )(x2d, g, w, kg, cosf, sinf)


def _q_proj_kernel(x_ref, g_ref, w_ref, qg_ref, cos_ref, sin_ref, q_ref, sg_ref):
    d_attn = N_HEADS * HEAD_DIM
    for t in range(x_ref.shape[0] // SUB_ROWS):
        rows = pl.ds(t * SUB_ROWS, SUB_ROWS)
        h = _rms_norm(x_ref[rows, :], g_ref[...]).astype(BF16)
        u = jnp.dot(h, w_ref[...], preferred_element_type=F32)
        qs = _head_norm_rope(u[:, :d_attn], qg_ref[...], cos_ref[rows, :], sin_ref[rows, :])
        for hd in range(N_HEADS):
            q_ref[0, hd, rows, :] = qs[hd]
        sg_ref[rows, :] = _silu(u[:, d_attn:])


def _q_proj(x2d, B, S, g, w, qg, cosf, sinf):
    T, D = x2d.shape
    tm = ROW_TILE
    ns = S // tm
    d_attn = N_HEADS * HEAD_DIM
    return pl.pallas_call(
        _q_proj_kernel,
        grid=(B, ns),
        in_specs=[pl.BlockSpec((tm, D), lambda b, s: (b * ns + s, 0)),
                  pl.BlockSpec((1, D), lambda b, s: (0, 0)),
                  pl.BlockSpec((D, 2 * d_attn), lambda b, s: (0, 0)),
                  pl.BlockSpec((1, HEAD_DIM), lambda b, s: (0, 0)),
                  pl.BlockSpec((tm, HEAD_DIM), lambda b, s: (s, 0)),
                  pl.BlockSpec((tm, HEAD_DIM), lambda b, s: (s, 0))],
        out_specs=[pl.BlockSpec((1, N_HEADS, tm, HEAD_DIM), lambda b, s: (b, 0, s, 0)),
                   pl.BlockSpec((tm, d_attn), lambda b, s: (b * ns + s, 0))],
        out_shape=[jax.ShapeDtypeStruct((B, N_HEADS, S, HEAD_DIM), F32),
                   jax.ShapeDtypeStruct((T, d_attn), F32)],
        compiler_params=_cparams(("arbitrary", "arbitrary")),
        name="q_proj",
    )(x2d, g, w, qg, cosf, sinf)


def _moba_route_kernel(q_ref, km_ref, rt_ref, cnt_ref, run_ref):
    step = pl.program_id(2)
    nb = km_ref.shape[2]

    @pl.when(step == 0)
    def _():
        run_ref[...] = jnp.zeros_like(run_ref)

    km = km_ref[0, 0].astype(BF16)
    blk = lax.broadcasted_iota(I32, (nb, MOBA_BLOCK), 0)
    qa = lax.broadcasted_iota(I32, (MOBA_BLOCK, MOBA_BLOCK), 0)
    qb = lax.broadcasted_iota(I32, (MOBA_BLOCK, MOBA_BLOCK), 1)
    earlier = jnp.where(qa < qb, 1.0, 0.0).astype(BF16)
    run = run_ref[...]
    for u in range(ROUTE_TILES):
        j = step * ROUTE_TILES + u
        g = _nt_dot(km, q_ref[0, 0, pl.ds(u * MOBA_BLOCK, MOBA_BLOCK), :].astype(BF16))
        g = jnp.where(blk < j, g, -jnp.inf)
        hits, ids = [], []
        for _ in range(MOBA_TOPK):
            mx = jnp.max(g, axis=0, keepdims=True)
            idx = jnp.min(jnp.where(g == mx, blk, nb), axis=0, keepdims=True)
            hit = blk == idx
            ok = mx > -jnp.inf
            hits.append(jnp.where(hit & ok, 1.0, 0.0))
            ids.append(jnp.where(ok, idx, -1))
            g = jnp.where(hit, -jnp.inf, g)
        sel = hits[0] + hits[1] + hits[2]

        rank = jnp.dot(sel.astype(BF16), earlier, preferred_element_type=F32)
        dest = run + rank
        run = run + jnp.sum(sel, axis=1, keepdims=True)
        for r in range(MOBA_TOPK):
            loc = jnp.sum(hits[r] * dest, axis=0, keepdims=True).astype(I32)
            rt_ref[u, pl.ds(r, 1), :] = jnp.where(ids[r] >= 0, ids[r] * ROUTE_BLK_UNIT + loc, -1)
        rt_ref[u, pl.ds(MOBA_TOPK, SUBLANES - MOBA_TOPK), :] = jnp.full(
            (SUBLANES - MOBA_TOPK, MOBA_BLOCK), -1, I32)
    run_ref[...] = run
    cnt_ref[0] = run.astype(I32)


def _moba_route(q, kmean, b):
    _, H, S, Dh = q.shape
    nb = S // MOBA_BLOCK
    rows = ROUTE_TILES * MOBA_BLOCK
    nsteps = nb // ROUTE_TILES
    return pl.pallas_call(
        _moba_route_kernel,
        grid=(1, H, nsteps),
        in_specs=[pl.BlockSpec((1, 1, rows, Dh), lambda _, h, j: (b, h, j, 0)),
                  pl.BlockSpec((1, 1, nb, Dh), lambda _, h, j: (b, h, 0, 0))],
        out_specs=[pl.BlockSpec((ROUTE_TILES, SUBLANES, MOBA_BLOCK), lambda _, h, j: (h * nsteps + j, 0, 0)),
                   pl.BlockSpec((1, nb, 1), lambda _, h, j: (h, 0, 0))],
        out_shape=[jax.ShapeDtypeStruct((H * nb, SUBLANES, MOBA_BLOCK), I32),
                   jax.ShapeDtypeStruct((H, nb, 1), I32)],
        scratch_shapes=[pltpu.VMEM((nb, 1), F32)],
        compiler_params=_cparams(("arbitrary", "arbitrary", "arbitrary")),
        name="moba_route",
    )(q, kmean)


def _moba_layout_kernel(cnt_ref, rt_ref, pos_ref, tblk_ref, tvalid_ref, nt_ref, base_ref, *, max_tiles):
    g = pl.program_id(0)
    nb = rt_ref.shape[0]
    slab_rows = max_tiles * GROUP_TILE
    null_row0 = (max_tiles - 1) * GROUP_TILE

    def per_block(n, first_tile):
        count = cnt_ref[g * nb + n]
        tiles = (count + (GROUP_TILE - 1)) // GROUP_TILE
        base_ref[n] = first_tile * GROUP_TILE

        def per_tile(i, carry):
            tblk_ref[g * max_tiles + first_tile + i] = n
            tvalid_ref[g * max_tiles + first_tile + i] = jnp.minimum(count - i * GROUP_TILE, GROUP_TILE)
            return carry

        lax.fori_loop(0, tiles, per_tile, 0)
        return first_tile + tiles

    used = lax.fori_loop(0, nb, per_block, 0)
    nt_ref[g] = used

    def unused_tile(t, carry):
        tblk_ref[g * max_tiles + t] = nb - 1
        tvalid_ref[g * max_tiles + t] = 0
        return carry

    lax.fori_loop(used, max_tiles, unused_tile, 0)

    code = rt_ref[...].reshape(nb * SUBLANES, MOBA_BLOCK)
    blk = lax.shift_right_arithmetic(code, ROUTE_BLK_SHIFT)
    base = jnp.zeros_like(code)
    for n in range(nb):
        base = jnp.where(blk == n, base_ref[n], base)
    lane = lax.broadcasted_iota(I32, code.shape, 1)
    row = jnp.where(code >= 0, base + (code & (ROUTE_BLK_UNIT - 1)), null_row0 + lane)
    pos_ref[...] = (row + g * slab_rows).reshape(pos_ref.shape)


def _moba_layout(rt, counts, nb, max_tiles):
    G = rt.shape[0] // nb
    smem = pl.BlockSpec(memory_space=pltpu.SMEM)
    return pl.pallas_call(
        functools.partial(_moba_layout_kernel, max_tiles=max_tiles),
        grid_spec=pltpu.PrefetchScalarGridSpec(
            num_scalar_prefetch=1,
            grid=(G,),
            in_specs=[pl.BlockSpec((nb, SUBLANES, MOBA_BLOCK), lambda g, cnt: (g, 0, 0))],
            out_specs=[pl.BlockSpec((nb, SUBLANES, MOBA_BLOCK), lambda g, cnt: (g, 0, 0)), smem, smem, smem],
            scratch_shapes=[pltpu.SMEM((nb,), I32)],
        ),
        out_shape=[jax.ShapeDtypeStruct((G * nb, SUBLANES, MOBA_BLOCK), I32),
                   jax.ShapeDtypeStruct((G * max_tiles,), I32),
                   jax.ShapeDtypeStruct((G * max_tiles,), I32),
                   jax.ShapeDtypeStruct((G,), I32)],
        compiler_params=_cparams(("arbitrary",)),
        name="moba_layout",
    )(counts, rt)


def _sc_mesh():
    return plsc.VectorSubcoreMesh(core_axis_name="c", subcore_axis_name="s")


def _sc_load_indices(pos_hbm, idx_v, t0, n_tiles):
    for c in range(SC_HALVES):
        pltpu.sync_copy(pos_hbm.at[pl.ds(t0, n_tiles), pl.ds(0, MOBA_TOPK), pl.ds(c * IDX_CHUNK, IDX_CHUNK)],
                        idx_v.at[c])


def _sc_scatter_rows(src, src_tile0, pos, n_out_rows):
    D = src.shape[1]
    n_tiles = pos.shape[0]
    per_w = n_tiles // SC_WORKERS
    assert per_w * SC_WORKERS == n_tiles

    @functools.partial(
        pl.kernel, mesh=_sc_mesh(),
        out_type=jax.ShapeDtypeStruct((n_out_rows, D), src.dtype),
        scratch_types=[pltpu.VMEM((SC_HALVES, per_w, MOBA_TOPK, IDX_CHUNK), I32),
                       pltpu.VMEM((SC_HALVES, IDX_CHUNK, D), src.dtype),
                       pltpu.SemaphoreType.DMA((SC_HALVES,)),
                       pltpu.SemaphoreType.DMA((SC_HALVES,))],
        name="sc_scatter_rows",
    )
    def k(src_hbm, pos_hbm, out_hbm, idx_v, rows_v, load_sem, scat_sem):
        t0 = (lax.axis_index("s") * SC_CORES + lax.axis_index("c")) * per_w
        _sc_load_indices(pos_hbm, idx_v, t0, per_w)

        def load(i, c):
            row0 = pl.multiple_of((src_tile0 + t0 + i) * MOBA_BLOCK + c * IDX_CHUNK, IDX_CHUNK)
            return pltpu.make_async_copy(src_hbm.at[pl.ds(row0, IDX_CHUNK)], rows_v.at[c], load_sem.at[c])

        def scatters(i, c):
            return [pltpu.make_async_copy(rows_v.at[c], out_hbm.at[idx_v.at[c, i, r]], scat_sem.at[c])
                    for r in range(MOBA_TOPK)]

        load(0, 0).start()

        @pl.loop(0, per_w)
        def _(i):
            for c in range(SC_HALVES):
                pi, pc = (i, 0) if c else (i - 1, 1)
                ni, nc = (i + 1, 0) if c else (i, 1)
                load(i, c).wait()
                for cp in scatters(i, c):
                    cp.start()

                @pl.when(pi >= 0)
                def _():
                    for cp in scatters(pi, pc):
                        cp.wait()

                @pl.when(ni < per_w)
                def _():
                    load(ni, nc).start()

        for cp in scatters(per_w - 1, SC_HALVES - 1):
            cp.wait()

    return k(src, pos)


def _sc_gather_rows(table, pos):
    R, D = table.shape
    n_tiles = pos.shape[0]
    N = n_tiles * MOBA_BLOCK
    per_w = n_tiles // SC_WORKERS
    assert per_w * SC_WORKERS == n_tiles

    items = SC_HALVES * MOBA_TOPK
    n_buf = 2
    assert items % n_buf == 0

    @functools.partial(
        pl.kernel, mesh=_sc_mesh(),
        out_type=jax.ShapeDtypeStruct((MOBA_TOPK, N, D), table.dtype),
        scratch_types=[pltpu.VMEM((SC_HALVES, per_w, MOBA_TOPK, IDX_CHUNK), I32),
                       pltpu.VMEM((n_buf, IDX_CHUNK, D), table.dtype),
                       pltpu.SemaphoreType.DMA((n_buf,)),
                       pltpu.SemaphoreType.DMA((n_buf,))],
        name="sc_gather_rows",
    )
    def k(table_hbm, pos_hbm, out_hbm, idx_v, rows_v, gat_sem, put_sem):
        t0 = (lax.axis_index("s") * SC_CORES + lax.axis_index("c")) * per_w
        _sc_load_indices(pos_hbm, idx_v, t0, per_w)

        def gather(i, j):
            c, r = divmod(j, MOBA_TOPK)
            return pltpu.make_async_copy(table_hbm.at[idx_v.at[c, i, r]], rows_v.at[j % n_buf], gat_sem.at[j % n_buf])

        def put(i, j):
            c, r = divmod(j, MOBA_TOPK)
            row0 = pl.multiple_of((t0 + i) * MOBA_BLOCK + c * IDX_CHUNK, IDX_CHUNK)
            return pltpu.make_async_copy(rows_v.at[j % n_buf], out_hbm.at[r, pl.ds(row0, IDX_CHUNK)],
                                         put_sem.at[j % n_buf])

        gather(0, 0).start()

        @pl.loop(0, per_w)
        def _(i):
            for j in range(items):
                pi, pj = (i, j - 1) if j else (i - 1, items - 1)
                ni, nj = (i, j + 1) if j + 1 < items else (i + 1, 0)
                gather(i, j).wait()
                put(i, j).start()

                @pl.when(pi >= 0)
                def _():
                    put(pi, pj).wait()

                @pl.when(ni < per_w)
                def _():
                    gather(ni, nj).start()

        put(per_w - 1, items - 1).wait()

    return k(table, pos)


def _bits(x):
    return lax.bitcast_convert_type(x, I32)


def _pack_partial(part, m):
    lane = lax.broadcasted_iota(I32, (part.shape[0], HEAD_DIM), 1)
    lo = _bits(part[:, :HEAD_DIM]) + BF16_HALF_ULP
    hi = _bits(jnp.where(lane == L_LANE, m, part[:, HEAD_DIM:])) + BF16_HALF_ULP
    return (hi & HI16_MASK) | lax.shift_right_logical(lo, 16)


def _null_partial(rows):
    lane = lax.broadcasted_iota(I32, (rows, HEAD_DIM), 1)
    return jnp.where(lane == L_LANE, NEG_INF_BITS & HI16_MASK, 0)


def _unpack_partial(word):
    lo = lax.bitcast_convert_type(lax.shift_left(word, 16), F32)
    hi = lax.bitcast_convert_type(word & HI16_MASK, F32)
    return lo, hi


def _moba_group_kernel(tblk_ref, tvalid_ref, nt_ref, qg_ref, k_ref, v_ref, o_ref, *, max_tiles):
    g, step = pl.program_id(0), pl.program_id(1)
    t0 = g * max_tiles + step * TILES_PER_STEP
    scale = HEAD_DIM ** -0.5

    @pl.when(tvalid_ref[t0] > 0)
    def _():
        for u in range(TILES_PER_STEP):
            rows = pl.ds(u * GROUP_TILE, GROUP_TILE)
            q = qg_ref[rows, :].astype(BF16)
            n0 = pl.multiple_of(tblk_ref[t0 + u] * MOBA_BLOCK, MOBA_BLOCK)
            s = _nt_dot(q, k_ref[0, 0, pl.ds(n0, MOBA_BLOCK), :])
            m = (jnp.max(s, axis=1, keepdims=True) * scale).astype(BF16).astype(F32)
            p = jnp.exp2(s * (scale * LOG2_E) - m * LOG2_E).astype(BF16)
            part = jnp.dot(p, v_ref[0, 0, pl.ds(n0, MOBA_BLOCK), :], preferred_element_type=F32)
            o_ref[rows, :] = _pack_partial(part, m)

    @pl.when(step == pl.num_programs(1) - 1)
    def _():
        o_ref[pl.ds((TILES_PER_STEP - 1) * GROUP_TILE, GROUP_TILE), :] = _null_partial(GROUP_TILE)


def _moba_group(qg, k, vaug, b, tblk, tvalid, ntiles, max_tiles):
    _, H, S, Dh = k.shape
    G = H
    n_steps = max_tiles // TILES_PER_STEP
    step_rows = TILES_PER_STEP * GROUP_TILE

    def tile_idx(g, s, tblk, tvalid, nt):
        return (g * n_steps + jnp.where(s * TILES_PER_STEP < nt[g], s, n_steps - 1), 0)

    return pl.pallas_call(
        functools.partial(_moba_group_kernel, max_tiles=max_tiles),
        grid_spec=pltpu.PrefetchScalarGridSpec(
            num_scalar_prefetch=3,
            grid=(G, n_steps),
            in_specs=[pl.BlockSpec((step_rows, Dh), tile_idx),
                      pl.BlockSpec((1, 1, S, Dh), lambda g, s, *_: (b, g, 0, 0)),
                      pl.BlockSpec((1, 1, S, VA_W), lambda g, s, *_: (b, g, 0, 0))],
            out_specs=pl.BlockSpec((step_rows, LANES), tile_idx),
        ),
        out_shape=jax.ShapeDtypeStruct((G * max_tiles * GROUP_TILE, LANES), I32),
        compiler_params=_cparams(("arbitrary", "arbitrary")),
        name="moba_group",
    )(tblk, tvalid, ntiles, qg, k, vaug)


def _moba_merge_kernel(*refs, n_skip):
    q_ref, k_ref, v_ref, sg_ref, x_ref, outw_ref = refs[n_skip:n_skip + 6]
    og_refs, out_ref, y_ref = refs[n_skip + 6:-2], refs[-2], refs[-1]
    scale = HEAD_DIM ** -0.5
    qi = lax.broadcasted_iota(I32, (MOBA_BLOCK, MOBA_BLOCK), 0)
    ki = lax.broadcasted_iota(I32, (MOBA_BLOCK, MOBA_BLOCK), 1)
    lane = lax.broadcasted_iota(I32, (MOBA_BLOCK, HEAD_DIM), 1)
    spread = jnp.where(lax.broadcasted_iota(I32, (HEAD_DIM, HEAD_DIM), 0) == L_LANE, 1.0, 0.0).astype(BF16)
    for u, t in [(u, t) for t in range(MERGE_TILES) for u in range(len(og_refs))]:
        og_ref = og_refs[u]
        rows = pl.ds(t * MOBA_BLOCK, MOBA_BLOCK)
        q = q_ref[0, u, rows, :].astype(BF16)
        s = _nt_dot(q, k_ref[0, u, rows, :])
        s = jnp.where(ki <= qi, s, NEG_INF)
        m_raw = jnp.max(s, axis=1, keepdims=True)
        p = jnp.exp2((s - m_raw) * (scale * LOG2_E)).astype(BF16)
        own = jnp.dot(p, v_ref[0, u, rows, :], preferred_element_type=F32)
        parts = [_unpack_partial(og_ref[r, rows, :]) for r in range(MOBA_TOPK)]
        ms = [jnp.dot(jnp.where(lane == L_LANE, hi, 0.0).astype(BF16), spread, preferred_element_type=F32)
              for _, hi in parts]
        m_own = jnp.broadcast_to(m_raw * scale, lane.shape)
        m_all = m_own
        for m in ms:
            m_all = jnp.maximum(m_all, m)
        w_own = jnp.exp2((m_own - m_all) * LOG2_E)
        tot_lo = w_own * own[:, :HEAD_DIM]
        tot_hi = w_own * own[:, HEAD_DIM:]
        for (lo, hi), m in zip(parts, ms):
            w = jnp.exp2((m - m_all) * LOG2_E)
            tot_lo = tot_lo + w * lo
            tot_hi = tot_hi + w * hi
        acc = jnp.where(lane < HALF, tot_lo, pltpu.roll(tot_hi, HALF, 1))
        cols = pl.ds(u * HEAD_DIM, HEAD_DIM)
        y_ref[rows, cols] = ((acc / tot_lo[:, L_LANE:L_LANE + 1]) * sg_ref[rows, cols]).astype(BF16)

    out_ref[...] = x_ref[...] + jnp.dot(y_ref[...], outw_ref[...], preferred_element_type=F32)


def _moba_merge(q, k, vaug, sg, og, x2d, out_w, b, o_prev):
    B, H, S, Dh = q.shape
    D = x2d.shape[1]
    hps = HEADS_PER_STEP
    assert hps == H
    rows = MERGE_TILES * MOBA_BLOCK
    ns = S // rows
    og_spec = lambda u: pl.BlockSpec((MOBA_TOPK, rows, LANES),
                                     lambda _, hp, j: (0, (hp * hps + u) * ns + j, 0))
    prev_specs = [] if o_prev is None else [pl.BlockSpec(memory_space=pl.ANY)]
    prev_args = [] if o_prev is None else [o_prev]
    return pl.pallas_call(
        functools.partial(_moba_merge_kernel, n_skip=len(prev_args)),
        grid=(1, H // hps, ns),
        in_specs=prev_specs
                 + [pl.BlockSpec((1, hps, rows, Dh), lambda _, hp, j: (b, hp, j, 0)),
                    pl.BlockSpec((1, hps, rows, Dh), lambda _, hp, j: (b, hp, j, 0)),
                    pl.BlockSpec((1, hps, rows, VA_W), lambda _, hp, j: (b, hp, j, 0)),
                    pl.BlockSpec((rows, hps * Dh), lambda _, hp, j: (b * ns + j, hp)),
                    pl.BlockSpec((rows, D), lambda _, hp, j: (b * ns + j, 0)),
                    pl.BlockSpec((H * Dh, D), lambda _, hp, j: (0, 0))]
                 + [og_spec(u) for u in range(hps)],
        out_specs=pl.BlockSpec((rows, D), lambda _, hp, j: (b * ns + j, 0)),
        out_shape=jax.ShapeDtypeStruct((B * S, D), F32),
        scratch_shapes=[pltpu.VMEM((rows, H * Dh), BF16)],
        input_output_aliases={} if o_prev is None else {0: 0},
        compiler_params=_cparams(("arbitrary", "arbitrary", "arbitrary")),
        name="moba_merge",
    )(*prev_args, q, k, vaug, sg, x2d, out_w, *([og] * hps))


def _moba_layer(q, k, vaug, kmean, sg, x2d, out_w):
    B, H, S, Dh = q.shape
    nb = S // MOBA_BLOCK
    max_tiles = (MOBA_TOPK * S) // GROUP_TILE + nb + 1
    max_tiles = -(-max_tiles // TILES_PER_STEP) * TILES_PER_STEP
    slab_rows = max_tiles * GROUP_TILE

    q_rows = q.reshape(B * H * S, Dh)
    o = None
    for b in range(B):
        rt, counts = _moba_route(q, kmean, b)
        pos, tblk, tvalid, ntiles = _moba_layout(rt, counts.reshape(-1), nb, max_tiles)
        qg = _sc_scatter_rows(q_rows, b * H * nb, pos, H * slab_rows)
        parts = _moba_group(qg, k, vaug, b, tblk, tvalid, ntiles, max_tiles)
        og = _sc_gather_rows(parts, pos)
        o = _moba_merge(q, k, vaug, sg, og, x2d, out_w, b, o)
    return o


def _rope_tables(S):
    inv = np.float32(ROPE_THETA) ** (-np.arange(HALF, dtype=np.float32) / np.float32(HALF))
    ang = np.arange(S, dtype=np.float32)[:, None] * inv[None, :].astype(np.float32)
    cos, sin = np.cos(ang).astype(np.float32), np.sin(ang).astype(np.float32)
    return (jnp.asarray(np.concatenate([cos, cos], axis=-1)),
            jnp.asarray(np.concatenate([-sin, sin], axis=-1)))


def kernel(x, a_norm, a_in_w, a_conv_w, a_conv_b, a_r_w, a_r_b, a_i_w, a_i_b, a_lambda, a_out_w,
           kv_norm, kv_w, k_norm, b_norm, b_in_w, q_norm, b_out_w):
    B, S, D = x.shape
    assert S % ROW_TILE == 0 and ROW_TILE % MOBA_BLOCK == 0
    assert D == N_RG_BLOCKS * RG_BW == N_HEADS * HEAD_DIM
    assert (B * N_HEADS * S) % (IDX_CHUNK * SC_WORKERS) == 0
    xs = x.reshape(B * S, D)
    row = lambda v: v.reshape(1, -1)

    for l in range(a_in_w.shape[0]):
        wri = jnp.concatenate([a_r_w[l], a_i_w[l]], axis=-1).astype(BF16)
        xs = _hawk_layer(xs, B, S, row(a_norm[l]), a_in_w[l].astype(BF16), a_conv_w[l], row(a_conv_b[l]), wri,
                         row(a_r_b[l]), row(a_i_b[l]), row(a_lambda[l]), a_out_w[l].astype(BF16))

    cosf, sinf = _rope_tables(S)
    k, vaug, kmean = _kv_proj(xs, B, S, row(kv_norm), kv_w.astype(BF16), row(k_norm), cosf, sinf)
    kmean = jnp.transpose(kmean, (0, 2, 1, 3))

    for jl in range(b_in_w.shape[0]):
        q, sg = _q_proj(xs, B, S, row(b_norm[jl]), b_in_w[jl].astype(BF16), row(q_norm[jl]), cosf, sinf)
        xs = _moba_layer(q, k, vaug, kmean, sg, xs, b_out_w[jl].astype(BF16))
    return xs.reshape(B, S, D)
```

```python
import functools

import jax
import jax.numpy as jnp
import numpy as np
from jax import lax
from jax.experimental import pallas as pl
from jax.experimental.pallas import tpu as pltpu
from jax.experimental.pallas import tpu_sc as plsc

N_HEADS = 8
HEAD_DIM = 128
MOBA_BLOCK = 256
MOBA_TOPK = 3
CONV_WIDTH = 4
N_RG_BLOCKS = 8
RG_BW = 128
RG_C = 8.0
ROPE_THETA = 10000.0
EPS = 1e-6
NEG_INF = -1e30
LOG2_E = 1.4426950408889634
TINY = 1e-30

SUBLANES = 8
LANES = 128
ROW_TILE = 512
SUB_ROWS = 128
VMEM_LIMIT = 56 * 1024 * 1024

GROUP_TILE = MOBA_BLOCK
TILES_PER_STEP = 32
HEADS_PER_STEP = 8
MERGE_TILES = 2
ROUTE_TILES = 8
ROUTE_BLK_SHIFT = 16
ROUTE_BLK_UNIT = 1 << ROUTE_BLK_SHIFT
HALF = HEAD_DIM // 2
VA_W = 2 * HEAD_DIM
L_LANE = HALF
HI16_MASK = -65536
BF16_HALF_ULP = 0x8000
NEG_INF_BITS = int(np.float32(NEG_INF).view(np.int32))

SC_CORES = 2
SC_SUBCORES = 16
SC_WORKERS = SC_CORES * SC_SUBCORES
IDX_CHUNK = 128
SC_HALVES = MOBA_BLOCK // IDX_CHUNK

F32 = jnp.float32
BF16 = jnp.bfloat16
I32 = jnp.int32


def _cparams(sem):
    return pltpu.CompilerParams(dimension_semantics=sem, vmem_limit_bytes=VMEM_LIMIT)


def _rms_norm(x, g):
    ms = jnp.mean(x * x, axis=-1, keepdims=True)
    return (x * lax.rsqrt(ms + EPS)) * g


def _silu(x):
    hx = 0.5 * x
    return hx * jnp.tanh(hx) + hx


def _sigmoid(x):
    return 0.5 * jnp.tanh(0.5 * x) + 0.5


def _nt_dot(a, b):
    return lax.dot_general(a, b, (((1,), (1,)), ((), ())), preferred_element_type=F32)


def _head_norm_rope(z, g, cosf, sinf):
    outs = []
    for h in range(N_HEADS):
        zh = z[:, h * HEAD_DIM:(h + 1) * HEAD_DIM]
        zh = _rms_norm(zh, g)
        outs.append(zh * cosf + pltpu.roll(zh, HEAD_DIM // 2, 1) * sinf)
    return outs


def _hawk_kernel(x_ref, g_ref, inw_ref, cw_ref, cb_ref, wri_ref, rb_ref, ib_ref, lam_ref, outw_ref,
                 o_ref, gate_ref, xpad_ref, a_ref, b_ref, hc_ref):
    tm, C = gate_ref.shape
    s = pl.program_id(1)

    @pl.when(s == 0)
    def _():
        xpad_ref[pl.ds(0, SUBLANES), :] = jnp.zeros((SUBLANES, xpad_ref.shape[1]), F32)
        hc_ref[...] = jnp.zeros_like(hc_ref)

    h = _rms_norm(x_ref[...], g_ref[...]).astype(BF16)
    gate_ref[...] = jnp.dot(h, inw_ref[:, C:], preferred_element_type=F32)

    xpad_ref[pl.ds(SUBLANES, tm), :] = jnp.dot(h, inw_ref[:, :C], preferred_element_type=F32)
    cw = cw_ref[...]
    xc = xpad_ref[pl.ds(SUBLANES - (CONV_WIDTH - 1), tm), :] * cw[0:1, :]
    for k in range(1, CONV_WIDTH):
        xc = xc + xpad_ref[pl.ds(SUBLANES - (CONV_WIDTH - 1) + k, tm), :] * cw[k:k + 1, :]
    xc = xc + cb_ref[...]
    xpad_ref[pl.ds(0, SUBLANES), :] = xpad_ref[pl.ds(tm, SUBLANES), :]

    lam = lam_ref[...]
    sp = jnp.maximum(-lam, 0.0) + jnp.log1p(jnp.exp(-jnp.abs(lam)))
    log2a_per_r = (-RG_C * LOG2_E) * sp
    xc16 = xc.astype(BF16)
    for g in range(N_RG_BLOCKS):
        lo = g * RG_BW
        z = jnp.dot(xc16[:, lo:lo + RG_BW], wri_ref[g], preferred_element_type=F32)
        r = _sigmoid(z[:, :RG_BW] + rb_ref[:, lo:lo + RG_BW])
        i = _sigmoid(z[:, RG_BW:] + ib_ref[:, lo:lo + RG_BW])
        a = jnp.exp2(r * log2a_per_r[:, lo:lo + RG_BW])
        om = 1.0 - a * a
        mult = om * lax.rsqrt(jnp.maximum(om, TINY))
        a_ref[:, lo:lo + RG_BW] = a
        b_ref[:, lo:lo + RG_BW] = mult * (i * xc[:, lo:lo + RG_BW])

    row = lax.broadcasted_iota(I32, (SUBLANES, a_ref.shape[1]), 0)

    def scan_body(c, hc):
        r0 = pl.multiple_of(c * SUBLANES, SUBLANES)
        a = a_ref[pl.ds(r0, SUBLANES), :]
        b = b_ref[pl.ds(r0, SUBLANES), :]
        for k in (1, 2, 4):
            keep = row >= k
            b = jnp.where(keep, a * pltpu.roll(b, k, 0) + b, b)
            a = jnp.where(keep, a * pltpu.roll(a, k, 0), a)
        h = a * hc + b
        b_ref[pl.ds(r0, SUBLANES), :] = h
        return h[SUBLANES - 1:SUBLANES, :]

    hc = lax.fori_loop(0, tm // SUBLANES, scan_body, hc_ref[0:1, :])
    hc_ref[0:1, :] = hc

    y = (b_ref[...] * _silu(gate_ref[...])).astype(BF16)
    o_ref[...] = x_ref[...] + jnp.dot(y, outw_ref[...], preferred_element_type=F32)


def _hawk_layer(x2d, B, S, g, in_w, cw, cb, wri, rb, ib, lam, out_w):
    T, D = x2d.shape
    C = in_w.shape[1] // 2
    tm = ROW_TILE
    ns = S // tm
    vec = lambda n: pl.BlockSpec((1, n), lambda b, s: (0, 0))
    return pl.pallas_call(
        _hawk_kernel,
        grid=(B, ns),
        in_specs=[pl.BlockSpec((tm, D), lambda b, s: (b * ns + s, 0)),
                  vec(D),
                  pl.BlockSpec((D, 2 * C), lambda b, s: (0, 0)),
                  pl.BlockSpec((CONV_WIDTH, C), lambda b, s: (0, 0)),
                  vec(C),
                  pl.BlockSpec((N_RG_BLOCKS, RG_BW, 2 * RG_BW), lambda b, s: (0, 0, 0)),
                  vec(C), vec(C), vec(C),
                  pl.BlockSpec((C, D), lambda b, s: (0, 0))],
        out_specs=pl.BlockSpec((tm, D), lambda b, s: (b * ns + s, 0)),
        out_shape=jax.ShapeDtypeStruct((T, D), F32),
        scratch_shapes=[pltpu.VMEM((tm, C), F32),
                        pltpu.VMEM((tm + SUBLANES, C), F32),
                        pltpu.VMEM((tm, C), F32),
                        pltpu.VMEM((tm, C), F32),
                        pltpu.VMEM((SUBLANES, C), F32)],
        compiler_params=_cparams(("arbitrary", "arbitrary")),
        name="hawk_layer",
    )(x2d, g, in_w, cw, cb, wri, rb, ib, lam, out_w)


def _kv_proj_kernel(x_ref, g_ref, w_ref, kg_ref, cos_ref, sin_ref, k_ref, v_ref, km_ref):
    tm = x_ref.shape[0]
    d_attn = N_HEADS * HEAD_DIM
    lane = lax.broadcasted_iota(I32, (SUB_ROWS, HEAD_DIM), 1)
    ksum = {}
    for t in range(tm // SUB_ROWS):
        rows = pl.ds(t * SUB_ROWS, SUB_ROWS)
        h = _rms_norm(x_ref[rows, :], g_ref[...]).astype(BF16)
        kv = jnp.dot(h, w_ref[...], preferred_element_type=F32)
        ks = _head_norm_rope(kv[:, :d_attn], kg_ref[...], cos_ref[rows, :], sin_ref[rows, :])
        for hd in range(N_HEADS):
            k_ref[0, hd, rows, :] = ks[hd].astype(BF16)
            vh = kv[:, d_attn + hd * HEAD_DIM:d_attn + (hd + 1) * HEAD_DIM]
            lo = jnp.where(lane < HALF, vh, jnp.where(lane == L_LANE, 1.0, 0.0))
            hi = jnp.where(lane < HALF, pltpu.roll(vh, HALF, 1), 0.0)
            v_ref[0, hd, rows, :] = jnp.concatenate([lo, hi], axis=1).astype(BF16)
            key = (hd, (t * SUB_ROWS) // MOBA_BLOCK)
            part = jnp.sum(ks[hd], axis=0, keepdims=True)
            ksum[key] = part if key not in ksum else ksum[key] + part
    for (hd, r), total in ksum.items():
        km_ref[0, r, pl.ds(hd, 1), :] = total * (1.0 / MOBA_BLOCK)


def _kv_proj(x2d, B, S, g, w, kg, cosf, sinf):
    T, D = x2d.shape
    tm = ROW_TILE
    ns = S // tm
    nb = S // MOBA_BLOCK
    rpt = tm // MOBA_BLOCK
    return pl.pallas_call(
        _kv_proj_kernel,
        grid=(B, ns),
        in_specs=[pl.BlockSpec((tm, D), lambda b, s: (b * ns + s, 0)),
                  pl.BlockSpec((1, D), lambda b, s: (0, 0)),
                  pl.BlockSpec((D, 2 * N_HEADS * HEAD_DIM), lambda b, s: (0, 0)),
                  pl.BlockSpec((1, HEAD_DIM), lambda b, s: (0, 0)),
                  pl.BlockSpec((tm, HEAD_DIM), lambda b, s: (s, 0)),
                  pl.BlockSpec((tm, HEAD_DIM), lambda b, s: (s, 0))],
        out_specs=[pl.BlockSpec((1, N_HEADS, tm, HEAD_DIM), lambda b, s: (b, 0, s, 0)),
                   pl.BlockSpec((1, N_HEADS, tm, VA_W), lambda b, s: (b, 0, s, 0)),
                   pl.BlockSpec((1, rpt, N_HEADS, HEAD_DIM), lambda b, s: (b, s, 0, 0))],
        out_shape=[jax.ShapeDtypeStruct((B, N_HEADS, S, HEAD_DIM), BF16),
                   jax.ShapeDtypeStruct((B, N_HEADS, S, VA_W), BF16),
                   jax.ShapeDtypeStruct((B, nb, N_HEADS, HEAD_DIM), F32)],
        compiler_params=_cparams(("arbitrary", "arbitrary")),
        name="kv_proj",
    )(x2d, g, w, kg, cosf, sinf)


def _q_proj_kernel(x_ref, g_ref, w_ref, qg_ref, cos_ref, sin_ref, q_ref, sg_ref):
    d_attn = N_HEADS * HEAD_DIM
    for t in range(x_ref.shape[0] // SUB_ROWS):
        rows = pl.ds(t * SUB_ROWS, SUB_ROWS)
        h = _rms_norm(x_ref[rows, :], g_ref[...]).astype(BF16)
        u = jnp.dot(h, w_ref[...], preferred_element_type=F32)
        qs = _head_norm_rope(u[:, :d_attn], qg_ref[...], cos_ref[rows, :], sin_ref[rows, :])
        for hd in range(N_HEADS):
            q_ref[0, hd, rows, :] = qs[hd]
        sg_ref[rows, :] = _silu(u[:, d_attn:])


def _q_proj(x2d, B, S, g, w, qg, cosf, sinf):
    T, D = x2d.shape
    tm = ROW_TILE
    ns = S // tm
    d_attn = N_HEADS * HEAD_DIM
    return pl.pallas_call(
        _q_proj_kernel,
        grid=(B, ns),
        in_specs=[pl.BlockSpec((tm, D), lambda b, s: (b * ns + s, 0)),
                  pl.BlockSpec((1, D), lambda b, s: (0, 0)),
                  pl.BlockSpec((D, 2 * d_attn), lambda b, s: (0, 0)),
                  pl.BlockSpec((1, HEAD_DIM), lambda b, s: (0, 0)),
                  pl.BlockSpec((tm, HEAD_DIM), lambda b, s: (s, 0)),
                  pl.BlockSpec((tm, HEAD_DIM), lambda b, s: (s, 0))],
        out_specs=[pl.BlockSpec((1, N_HEADS, tm, HEAD_DIM), lambda b, s: (b, 0, s, 0)),
                   pl.BlockSpec((tm, d_attn), lambda b, s: (b * ns + s, 0))],
        out_shape=[jax.ShapeDtypeStruct((B, N_HEADS, S, HEAD_DIM), F32),
                   jax.ShapeDtypeStruct((T, d_attn), F32)],
        compiler_params=_cparams(("arbitrary", "arbitrary")),
        name="q_proj",
    )(x2d, g, w, qg, cosf, sinf)


def _moba_route_kernel(q_ref, km_ref, rt_ref, cnt_ref, run_ref):
    step = pl.program_id(2)
    nb = km_ref.shape[2]

    @pl.when(step == 0)
    def _():
        run_ref[...] = jnp.zeros_like(run_ref)

    km = km_ref[0, 0].astype(BF16)
    blk = lax.broadcasted_iota(I32, (nb, MOBA_BLOCK), 0)
    qa = lax.broadcasted_iota(I32, (MOBA_BLOCK, MOBA_BLOCK), 0)
    qb = lax.broadcasted_iota(I32, (MOBA_BLOCK, MOBA_BLOCK), 1)
    earlier = jnp.where(qa < qb, 1.0, 0.0).astype(BF16)
    run = run_ref[...]
    for u in range(ROUTE_TILES):
        j = step * ROUTE_TILES + u
        g = _nt_dot(km, q_ref[0, 0, pl.ds(u * MOBA_BLOCK, MOBA_BLOCK), :].astype(BF16))
        g = jnp.where(blk < j, g, -jnp.inf)
        hits, ids = [], []
        for _ in range(MOBA_TOPK):
            mx = jnp.max(g, axis=0, keepdims=True)
            idx = jnp.min(jnp.where(g == mx, blk, nb), axis=0, keepdims=True)
            hit = blk == idx
            ok = mx > -jnp.inf
            hits.append(jnp.where(hit & ok, 1.0, 0.0))
            ids.append(jnp.where(ok, idx, -1))
            g = jnp.where(hit, -jnp.inf, g)
        sel = hits[0] + hits[1] + hits[2]

        rank = jnp.dot(sel.astype(BF16), earlier, preferred_element_type=F32)
        dest = run + rank
        run = run + jnp.sum(sel, axis=1, keepdims=True)
        for r in range(MOBA_TOPK):
            loc = jnp.sum(hits[r] * dest, axis=0, keepdims=True).astype(I32)
            rt_ref[u, pl.ds(r, 1), :] = jnp.where(ids[r] >= 0, ids[r] * ROUTE_BLK_UNIT + loc, -1)
        rt_ref[u, pl.ds(MOBA_TOPK, SUBLANES - MOBA_TOPK), :] = jnp.full(
            (SUBLANES - MOBA_TOPK, MOBA_BLOCK), -1, I32)
    run_ref[...] = run
    cnt_ref[0] = run.astype(I32)


def _moba_route(q, kmean, b):
    _, H, S, Dh = q.shape
    nb = S // MOBA_BLOCK
    rows = ROUTE_TILES * MOBA_BLOCK
    nsteps = nb // ROUTE_TILES
    return pl.pallas_call(
        _moba_route_kernel,
        grid=(1, H, nsteps),
        in_specs=[pl.BlockSpec((1, 1, rows, Dh), lambda _, h, j: (b, h, j, 0)),
                  pl.BlockSpec((1, 1, nb, Dh), lambda _, h, j: (b, h, 0, 0))],
        out_specs=[pl.BlockSpec((ROUTE_TILES, SUBLANES, MOBA_BLOCK), lambda _, h, j: (h * nsteps + j, 0, 0)),
                   pl.BlockSpec((1, nb, 1), lambda _, h, j: (h, 0, 0))],
        out_shape=[jax.ShapeDtypeStruct((H * nb, SUBLANES, MOBA_BLOCK), I32),
                   jax.ShapeDtypeStruct((H, nb, 1), I32)],
        scratch_shapes=[pltpu.VMEM((nb, 1), F32)],
        compiler_params=_cparams(("arbitrary", "arbitrary", "arbitrary")),
        name="moba_route",
    )(q, kmean)


def _moba_layout_kernel(cnt_ref, rt_ref, pos_ref, tblk_ref, tvalid_ref, nt_ref, base_ref, *, max_tiles):
    g = pl.program_id(0)
    nb = rt_ref.shape[0]
    slab_rows = max_tiles * GROUP_TILE
    null_row0 = (max_tiles - 1) * GROUP_TILE

    def per_block(n, first_tile):
        count = cnt_ref[g * nb + n]
        tiles = (count + (GROUP_TILE - 1)) // GROUP_TILE
        base_ref[n] = first_tile * GROUP_TILE

        def per_tile(i, carry):
            tblk_ref[g * max_tiles + first_tile + i] = n
            tvalid_ref[g * max_tiles + first_tile + i] = jnp.minimum(count - i * GROUP_TILE, GROUP_TILE)
            return carry

        lax.fori_loop(0, tiles, per_tile, 0)
        return first_tile + tiles

    used = lax.fori_loop(0, nb, per_block, 0)
    nt_ref[g] = used

    def unused_tile(t, carry):
        tblk_ref[g * max_tiles + t] = nb - 1
        tvalid_ref[g * max_tiles + t] = 0
        return carry

    lax.fori_loop(used, max_tiles, unused_tile, 0)

    code = rt_ref[...].reshape(nb * SUBLANES, MOBA_BLOCK)
    blk = lax.shift_right_arithmetic(code, ROUTE_BLK_SHIFT)
    base = jnp.zeros_like(code)
    for n in range(nb):
        base = jnp.where(blk == n, base_ref[n], base)
    lane = lax.broadcasted_iota(I32, code.shape, 1)
    row = jnp.where(code >= 0, base + (code & (ROUTE_BLK_UNIT - 1)), null_row0 + lane)
    pos_ref[...] = (row + g * slab_rows).reshape(pos_ref.shape)


def _moba_layout(rt, counts, nb, max_tiles):
    G = rt.shape[0] // nb
    smem = pl.BlockSpec(memory_space=pltpu.SMEM)
    return pl.pallas_call(
        functools.partial(_moba_layout_kernel, max_tiles=max_tiles),
        grid_spec=pltpu.PrefetchScalarGridSpec(
            num_scalar_prefetch=1,
            grid=(G,),
            in_specs=[pl.BlockSpec((nb, SUBLANES, MOBA_BLOCK), lambda g, cnt: (g, 0, 0))],
            out_specs=[pl.BlockSpec((nb, SUBLANES, MOBA_BLOCK), lambda g, cnt: (g, 0, 0)), smem, smem, smem],
            scratch_shapes=[pltpu.SMEM((nb,), I32)],
        ),
        out_shape=[jax.ShapeDtypeStruct((G * nb, SUBLANES, MOBA_BLOCK), I32),
                   jax.ShapeDtypeStruct((G * max_tiles,), I32),
                   jax.ShapeDtypeStruct((G * max_tiles,), I32),
                   jax.ShapeDtypeStruct((G,), I32)],
        compiler_params=_cparams(("arbitrary",)),
        name="moba_layout",
    )(counts, rt)


def _sc_mesh():
    return plsc.VectorSubcoreMesh(core_axis_name="c", subcore_axis_name="s")


def _sc_load_indices(pos_hbm, idx_v, t0, n_tiles):
    for c in range(SC_HALVES):
        pltpu.sync_copy(pos_hbm.at[pl.ds(t0, n_tiles), pl.ds(0, MOBA_TOPK), pl.ds(c * IDX_CHUNK, IDX_CHUNK)],
                        idx_v.at[c])


def _sc_scatter_rows(src, src_tile0, pos, n_out_rows):
    D = src.shape[1]
    n_tiles = pos.shape[0]
    per_w = n_tiles // SC_WORKERS
    assert per_w * SC_WORKERS == n_tiles

    @functools.partial(
        pl.kernel, mesh=_sc_mesh(),
        out_type=jax.ShapeDtypeStruct((n_out_rows, D), src.dtype),
        scratch_types=[pltpu.VMEM((SC_HALVES, per_w, MOBA_TOPK, IDX_CHUNK), I32),
                       pltpu.VMEM((SC_HALVES, IDX_CHUNK, D), src.dtype),
                       pltpu.SemaphoreType.DMA((SC_HALVES,)),
                       pltpu.SemaphoreType.DMA((SC_HALVES,))],
        name="sc_scatter_rows",
    )
    def k(src_hbm, pos_hbm, out_hbm, idx_v, rows_v, load_sem, scat_sem):
        t0 = (lax.axis_index("s") * SC_CORES + lax.axis_index("c")) * per_w
        _sc_load_indices(pos_hbm, idx_v, t0, per_w)

        def load(i, c):
            row0 = pl.multiple_of((src_tile0 + t0 + i) * MOBA_BLOCK + c * IDX_CHUNK, IDX_CHUNK)
            return pltpu.make_async_copy(src_hbm.at[pl.ds(row0, IDX_CHUNK)], rows_v.at[c], load_sem.at[c])

        def scatters(i, c):
            return [pltpu.make_async_copy(rows_v.at[c], out_hbm.at[idx_v.at[c, i, r]], scat_sem.at[c])
                    for r in range(MOBA_TOPK)]

        load(0, 0).start()

        @pl.loop(0, per_w)
        def _(i):
            for c in range(SC_HALVES):
                pi, pc = (i, 0) if c else (i - 1, 1)
                ni, nc = (i + 1, 0) if c else (i, 1)
                load(i, c).wait()
                for cp in scatters(i, c):
                    cp.start()

                @pl.when(pi >= 0)
                def _():
                    for cp in scatters(pi, pc):
                        cp.wait()

                @pl.when(ni < per_w)
                def _():
                    load(ni, nc).start()

        for cp in scatters(per_w - 1, SC_HALVES - 1):
            cp.wait()

    return k(src, pos)


def _sc_gather_rows(table, pos):
    R, D = table.shape
    n_tiles = pos.shape[0]
    N = n_tiles * MOBA_BLOCK
    per_w = n_tiles // SC_WORKERS
    assert per_w * SC_WORKERS == n_tiles

    items = SC_HALVES * MOBA_TOPK
    n_buf = 2
    assert items % n_buf == 0

    @functools.partial(
        pl.kernel, mesh=_sc_mesh(),
        out_type=jax.ShapeDtypeStruct((MOBA_TOPK, N, D), table.dtype),
        scratch_types=[pltpu.VMEM((SC_HALVES, per_w, MOBA_TOPK, IDX_CHUNK), I32),
                       pltpu.VMEM((n_buf, IDX_CHUNK, D), table.dtype),
                       pltpu.SemaphoreType.DMA((n_buf,)),
                       pltpu.SemaphoreType.DMA((n_buf,))],
        name="sc_gather_rows",
    )
    def k(table_hbm, pos_hbm, out_hbm, idx_v, rows_v, gat_sem, put_sem):
        t0 = (lax.axis_index("s") * SC_CORES + lax.axis_index("c")) * per_w
        _sc_load_indices(pos_hbm, idx_v, t0, per_w)

        def gather(i, j):
            c, r = divmod(j, MOBA_TOPK)
            return pltpu.make_async_copy(table_hbm.at[idx_v.at[c, i, r]], rows_v.at[j % n_buf], gat_sem.at[j % n_buf])

        def put(i, j):
            c, r = divmod(j, MOBA_TOPK)
            row0 = pl.multiple_of((t0 + i) * MOBA_BLOCK + c * IDX_CHUNK, IDX_CHUNK)
            return pltpu.make_async_copy(rows_v.at[j % n_buf], out_hbm.at[r, pl.ds(row0, IDX_CHUNK)],
                                         put_sem.at[j % n_buf])

        gather(0, 0).start()

        @pl.loop(0, per_w)
        def _(i):
            for j in range(items):
                pi, pj = (i, j - 1) if j else (i - 1, items - 1)
                ni, nj = (i, j + 1) if j + 1 < items else (i + 1, 0)
                gather(i, j).wait()
                put(i, j).start()

                @pl.when(pi >= 0)
                def _():
                    put(pi, pj).wait()

                @pl.when(ni < per_w)
                def _():
                    gather(ni, nj).start()

        put(per_w - 1, items - 1).wait()

    return k(table, pos)


def _bits(x):
    return lax.bitcast_convert_type(x, I32)


def _pack_partial(part, m):
    lane = lax.broadcasted_iota(I32, (part.shape[0], HEAD_DIM), 1)
    lo = _bits(part[:, :HEAD_DIM]) + BF16_HALF_ULP
    hi = _bits(jnp.where(lane == L_LANE, m, part[:, HEAD_DIM:])) + BF16_HALF_ULP
    return (hi & HI16_MASK) | lax.shift_right_logical(lo, 16)


def _null_partial(rows):
    lane = lax.broadcasted_iota(I32, (rows, HEAD_DIM), 1)
    return jnp.where(lane == L_LANE, NEG_INF_BITS & HI16_MASK, 0)


def _unpack_partial(word):
    lo = lax.bitcast_convert_type(lax.shift_left(word, 16), F32)
    hi = lax.bitcast_convert_type(word & HI16_MASK, F32)
    return lo, hi


def _moba_group_kernel(tblk_ref, tvalid_ref, nt_ref, qg_ref, k_ref, v_ref, o_ref, *, max_tiles):
    g, step = pl.program_id(0), pl.program_id(1)
    t0 = g * max_tiles + step * TILES_PER_STEP
    scale = HEAD_DIM ** -0.5

    @pl.when(tvalid_ref[t0] > 0)
    def _():
        for u in range(TILES_PER_STEP):
            rows = pl.ds(u * GROUP_TILE, GROUP_TILE)
            q = qg_ref[rows, :].astype(BF16)
            n0 = pl.multiple_of(tblk_ref[t0 + u] * MOBA_BLOCK, MOBA_BLOCK)
            s = _nt_dot(q, k_ref[0, 0, pl.ds(n0, MOBA_BLOCK), :])
            m = (jnp.max(s, axis=1, keepdims=True) * (scale * LOG2_E)).astype(BF16).astype(F32)
            p = jnp.exp2(s * (scale * LOG2_E) - m).astype(BF16)
            part = jnp.dot(p, v_ref[0, 0, pl.ds(n0, MOBA_BLOCK), :], preferred_element_type=F32)
            o_ref[rows, :] = _pack_partial(part, m)

    @pl.when(step == pl.num_programs(1) - 1)
    def _():
        o_ref[pl.ds((TILES_PER_STEP - 1) * GROUP_TILE, GROUP_TILE), :] = _null_partial(GROUP_TILE)


def _moba_group(qg, k, vaug, b, tblk, tvalid, ntiles, max_tiles):
    _, H, S, Dh = k.shape
    G = H
    n_steps = max_tiles // TILES_PER_STEP
    step_rows = TILES_PER_STEP * GROUP_TILE

    def tile_idx(g, s, tblk, tvalid, nt):
        return (g * n_steps + jnp.where(s * TILES_PER_STEP < nt[g], s, n_steps - 1), 0)

    return pl.pallas_call(
        functools.partial(_moba_group_kernel, max_tiles=max_tiles),
        grid_spec=pltpu.PrefetchScalarGridSpec(
            num_scalar_prefetch=3,
            grid=(G, n_steps),
            in_specs=[pl.BlockSpec((step_rows, Dh), tile_idx),
                      pl.BlockSpec((1, 1, S, Dh), lambda g, s, *_: (b, g, 0, 0)),
                      pl.BlockSpec((1, 1, S, VA_W), lambda g, s, *_: (b, g, 0, 0))],
            out_specs=pl.BlockSpec((step_rows, LANES), tile_idx),
        ),
        out_shape=jax.ShapeDtypeStruct((G * max_tiles * GROUP_TILE, LANES), I32),
        compiler_params=_cparams(("arbitrary", "arbitrary")),
        name="moba_group",
    )(tblk, tvalid, ntiles, qg, k, vaug)


def _moba_merge_kernel(*refs, n_skip):
    q_ref, k_ref, v_ref, sg_ref, x_ref, outw_ref = refs[n_skip:n_skip + 6]
    og_refs, out_ref, y_ref = refs[n_skip + 6:-2], refs[-2], refs[-1]
    scale = HEAD_DIM ** -0.5
    qi = lax.broadcasted_iota(I32, (MOBA_BLOCK, MOBA_BLOCK), 0)
    ki = lax.broadcasted_iota(I32, (MOBA_BLOCK, MOBA_BLOCK), 1)
    lane = lax.broadcasted_iota(I32, (MOBA_BLOCK, HEAD_DIM), 1)
    spread = jnp.where(lax.broadcasted_iota(I32, (HEAD_DIM, HEAD_DIM), 0) == L_LANE, 1.0, 0.0).astype(BF16)
    for u, t in [(u, t) for t in range(MERGE_TILES) for u in range(len(og_refs))]:
        og_ref = og_refs[u]
        rows = pl.ds(t * MOBA_BLOCK, MOBA_BLOCK)
        q = q_ref[0, u, rows, :].astype(BF16)
        s = _nt_dot(q, k_ref[0, u, rows, :])
        s = jnp.where(ki <= qi, s, NEG_INF)
        m_raw = jnp.max(s, axis=1, keepdims=True)
        p = jnp.exp2((s - m_raw) * (scale * LOG2_E)).astype(BF16)
        own = jnp.dot(p, v_ref[0, u, rows, :], preferred_element_type=F32)
        parts = [_unpack_partial(og_ref[r, rows, :]) for r in range(MOBA_TOPK)]
        ms = [jnp.dot(jnp.where(lane == L_LANE, hi, 0.0).astype(BF16), spread, preferred_element_type=F32)
              for _, hi in parts]
        m_own = jnp.broadcast_to(m_raw * (scale * LOG2_E), lane.shape)
        m_all = m_own
        for m in ms:
            m_all = jnp.maximum(m_all, m)
        w_own = jnp.exp2(m_own - m_all)
        tot_lo = w_own * own[:, :HEAD_DIM]
        tot_hi = w_own * own[:, HEAD_DIM:]
        for (lo, hi), m in zip(parts, ms):
            w = jnp.exp2(m - m_all)
            tot_lo = tot_lo + w * lo
            tot_hi = tot_hi + w * hi
        acc = jnp.where(lane < HALF, tot_lo, pltpu.roll(tot_hi, HALF, 1))
        cols = pl.ds(u * HEAD_DIM, HEAD_DIM)
        y_ref[rows, cols] = ((acc / tot_lo[:, L_LANE:L_LANE + 1]) * sg_ref[rows, cols]).astype(BF16)

    out_ref[...] = x_ref[...] + jnp.dot(y_ref[...], outw_ref[...], preferred_element_type=F32)


def _moba_merge(q, k, vaug, sg, og, x2d, out_w, b, o_prev):
    B, H, S, Dh = q.shape
    D = x2d.shape[1]
    hps = HEADS_PER_STEP
    assert hps == H
    rows = MERGE_TILES * MOBA_BLOCK
    ns = S // rows
    og_spec = lambda u: pl.BlockSpec((MOBA_TOPK, rows, LANES),
                                     lambda _, hp, j: (0, (hp * hps + u) * ns + j, 0))
    prev_specs = [] if o_prev is None else [pl.BlockSpec(memory_space=pl.ANY)]
    prev_args = [] if o_prev is None else [o_prev]
    return pl.pallas_call(
        functools.partial(_moba_merge_kernel, n_skip=len(prev_args)),
        grid=(1, H // hps, ns),
        in_specs=prev_specs
                 + [pl.BlockSpec((1, hps, rows, Dh), lambda _, hp, j: (b, hp, j, 0)),
                    pl.BlockSpec((1, hps, rows, Dh), lambda _, hp, j: (b, hp, j, 0)),
                    pl.BlockSpec((1, hps, rows, VA_W), lambda _, hp, j: (b, hp, j, 0)),
                    pl.BlockSpec((rows, hps * Dh), lambda _, hp, j: (b * ns + j, hp)),
                    pl.BlockSpec((rows, D), lambda _, hp, j: (b * ns + j, 0)),
                    pl.BlockSpec((H * Dh, D), lambda _, hp, j: (0, 0))]
                 + [og_spec(u) for u in range(hps)],
        out_specs=pl.BlockSpec((rows, D), lambda _, hp, j: (b * ns + j, 0)),
        out_shape=jax.ShapeDtypeStruct((B * S, D), F32),
        scratch_shapes=[pltpu.VMEM((rows, H * Dh), BF16)],
        input_output_aliases={} if o_prev is None else {0: 0},
        compiler_params=_cparams(("arbitrary", "arbitrary", "arbitrary")),
        name="moba_merge",
    )(*prev_args, q, k, vaug, sg, x2d, out_w, *([og] * hps))


def _moba_layer(q, k, vaug, kmean, sg, x2d, out_w):
    B, H, S, Dh = q.shape
    nb = S // MOBA_BLOCK
    max_tiles = (MOBA_TOPK * S) // GROUP_TILE + nb + 1
    max_tiles = -(-max_tiles // TILES_PER_STEP) * TILES_PER_STEP
    slab_rows = max_tiles * GROUP_TILE

    q_rows = q.reshape(B * H * S, Dh)
    o = None
    for b in range(B):
        rt, counts = _moba_route(q, kmean, b)
        pos, tblk, tvalid, ntiles = _moba_layout(rt, counts.reshape(-1), nb, max_tiles)
        qg = _sc_scatter_rows(q_rows, b * H * nb, pos, H * slab_rows)
        parts = _moba_group(qg, k, vaug, b, tblk, tvalid, ntiles, max_tiles)
        og = _sc_gather_rows(parts, pos)
        o = _moba_merge(q, k, vaug, sg, og, x2d, out_w, b, o)
    return o


def _rope_tables(S):
    inv = np.float32(ROPE_THETA) ** (-np.arange(HALF, dtype=np.float32) / np.float32(HALF))
    ang = np.arange(S, dtype=np.float32)[:, None] * inv[None, :].astype(np.float32)
    cos, sin = np.cos(ang).astype(np.float32), np.sin(ang).astype(np.float32)
    return (jnp.asarray(np.concatenate([cos, cos], axis=-1)),
            jnp.asarray(np.concatenate([-sin, sin], axis=-1)))


def kernel(x, a_norm, a_in_w, a_conv_w, a_conv_b, a_r_w, a_r_b, a_i_w, a_i_b, a_lambda, a_out_w,
           kv_norm, kv_w, k_norm, b_norm, b_in_w, q_norm, b_out_w):
    B, S, D = x.shape
    assert S % ROW_TILE == 0 and ROW_TILE % MOBA_BLOCK == 0
    assert D == N_RG_BLOCKS * RG_BW == N_HEADS * HEAD_DIM
    assert (B * N_HEADS * S) % (IDX_CHUNK * SC_WORKERS) == 0
    xs = x.reshape(B * S, D)
    row = lambda v: v.reshape(1, -1)

    for l in range(a_in_w.shape[0]):
        wri = jnp.concatenate([a_r_w[l], a_i_w[l]], axis=-1).astype(BF16)
        xs = _hawk_layer(xs, B, S, row(a_norm[l]), a_in_w[l].astype(BF16), a_conv_w[l], row(a_conv_b[l]), wri,
                         row(a_r_b[l]), row(a_i_b[l]), row(a_lambda[l]), a_out_w[l].astype(BF16))

    cosf, sinf = _rope_tables(S)
    k, vaug, kmean = _kv_proj(xs, B, S, row(kv_norm), kv_w.astype(BF16), row(k_norm), cosf, sinf)
    kmean = jnp.transpose(kmean, (0, 2, 1, 3))

    for jl in range(b_in_w.shape[0]):
        q, sg = _q_proj(xs, B, S, row(b_norm[jl]), b_in_w[jl].astype(BF16), row(q_norm[jl]), cosf, sinf)
        xs = _moba_layer(q, k, vaug, kmean, sg, xs, b_out_w[jl].astype(BF16))
    return xs.reshape(B, S, D)
```

```python
import functools

import jax
import jax.numpy as jnp
import numpy as np
from jax import lax
from jax.experimental import pallas as pl
from jax.experimental.pallas import tpu as pltpu
from jax.experimental.pallas import tpu_sc as plsc

N_HEADS = 8
HEAD_DIM = 128
MOBA_BLOCK = 256
MOBA_TOPK = 3
CONV_WIDTH = 4
N_RG_BLOCKS = 8
RG_BW = 128
RG_C = 8.0
ROPE_THETA = 10000.0
EPS = 1e-6
NEG_INF = -1e30
LOG2_E = 1.4426950408889634
TINY = 1e-30

SUBLANES = 8
LANES = 128
ROW_TILE = 512
SUB_ROWS = 128
VMEM_LIMIT = 56 * 1024 * 1024

GROUP_TILE = MOBA_BLOCK
TILES_PER_STEP = 32
V_BUILD_ROWS = 1024
HEADS_PER_STEP = 8
MERGE_TILES = 2
ROUTE_TILES = 8
ROUTE_BLK_SHIFT = 16
ROUTE_BLK_UNIT = 1 << ROUTE_BLK_SHIFT
HALF = HEAD_DIM // 2
VA_W = 2 * HEAD_DIM
L_LANE = HALF
HI16_MASK = -65536
BF16_HALF_ULP = 0x8000
NEG_INF_BITS = int(np.float32(NEG_INF).view(np.int32))

SC_CORES = 2
SC_SUBCORES = 16
SC_WORKERS = SC_CORES * SC_SUBCORES
IDX_CHUNK = 128
SC_HALVES = MOBA_BLOCK // IDX_CHUNK

F32 = jnp.float32
BF16 = jnp.bfloat16
I32 = jnp.int32


def _cparams(sem):
    return pltpu.CompilerParams(dimension_semantics=sem, vmem_limit_bytes=VMEM_LIMIT)


def _rms_norm(x, g):
    ms = jnp.mean(x * x, axis=-1, keepdims=True)
    return (x * lax.rsqrt(ms + EPS)) * g


def _silu(x):
    hx = 0.5 * x
    return hx * jnp.tanh(hx) + hx


def _sigmoid(x):
    return 0.5 * jnp.tanh(0.5 * x) + 0.5


def _nt_dot(a, b):
    return lax.dot_general(a, b, (((1,), (1,)), ((), ())), preferred_element_type=F32)


def _head_norm_rope(z, g, cosf, sinf):
    outs = []
    for h in range(N_HEADS):
        zh = z[:, h * HEAD_DIM:(h + 1) * HEAD_DIM]
        zh = _rms_norm(zh, g)
        outs.append(zh * cosf + pltpu.roll(zh, HEAD_DIM // 2, 1) * sinf)
    return outs


def _hawk_kernel(x_ref, g_ref, inw_ref, cw_ref, cb_ref, wri_ref, rb_ref, ib_ref, lam_ref, outw_ref,
                 o_ref, gate_ref, xpad_ref, a_ref, b_ref, hc_ref):
    tm, C = gate_ref.shape
    s = pl.program_id(1)

    @pl.when(s == 0)
    def _():
        xpad_ref[pl.ds(0, SUBLANES), :] = jnp.zeros((SUBLANES, xpad_ref.shape[1]), F32)
        hc_ref[...] = jnp.zeros_like(hc_ref)

    h = _rms_norm(x_ref[...], g_ref[...]).astype(BF16)
    gate_ref[...] = jnp.dot(h, inw_ref[:, C:], preferred_element_type=F32)

    xpad_ref[pl.ds(SUBLANES, tm), :] = jnp.dot(h, inw_ref[:, :C], preferred_element_type=F32)
    cw = cw_ref[...]
    xc = xpad_ref[pl.ds(SUBLANES - (CONV_WIDTH - 1), tm), :] * cw[0:1, :]
    for k in range(1, CONV_WIDTH):
        xc = xc + xpad_ref[pl.ds(SUBLANES - (CONV_WIDTH - 1) + k, tm), :] * cw[k:k + 1, :]
    xc = xc + cb_ref[...]
    xpad_ref[pl.ds(0, SUBLANES), :] = xpad_ref[pl.ds(tm, SUBLANES), :]

    lam = lam_ref[...]
    sp = jnp.maximum(-lam, 0.0) + jnp.log1p(jnp.exp(-jnp.abs(lam)))
    log2a_per_r = (-RG_C * LOG2_E) * sp
    xc16 = xc.astype(BF16)
    for g in range(N_RG_BLOCKS):
        lo = g * RG_BW
        z = jnp.dot(xc16[:, lo:lo + RG_BW], wri_ref[g], preferred_element_type=F32)
        r = _sigmoid(z[:, :RG_BW] + rb_ref[:, lo:lo + RG_BW])
        i = _sigmoid(z[:, RG_BW:] + ib_ref[:, lo:lo + RG_BW])
        a = jnp.exp2(r * log2a_per_r[:, lo:lo + RG_BW])
        om = 1.0 - a * a
        mult = om * lax.rsqrt(jnp.maximum(om, TINY))
        a_ref[:, lo:lo + RG_BW] = a
        b_ref[:, lo:lo + RG_BW] = mult * (i * xc[:, lo:lo + RG_BW])

    row = lax.broadcasted_iota(I32, (SUBLANES, a_ref.shape[1]), 0)

    def scan_body(c, hc):
        r0 = pl.multiple_of(c * SUBLANES, SUBLANES)
        a = a_ref[pl.ds(r0, SUBLANES), :]
        b = b_ref[pl.ds(r0, SUBLANES), :]
        for k in (1, 2, 4):
            keep = row >= k
            b = jnp.where(keep, a * pltpu.roll(b, k, 0) + b, b)
            a = jnp.where(keep, a * pltpu.roll(a, k, 0), a)
        h = a * hc + b
        b_ref[pl.ds(r0, SUBLANES), :] = h
        return h[SUBLANES - 1:SUBLANES, :]

    hc = lax.fori_loop(0, tm // SUBLANES, scan_body, hc_ref[0:1, :])
    hc_ref[0:1, :] = hc

    y = (b_ref[...] * _silu(gate_ref[...])).astype(BF16)
    o_ref[...] = x_ref[...] + jnp.dot(y, outw_ref[...], preferred_element_type=F32)


def _hawk_layer(x2d, B, S, g, in_w, cw, cb, wri, rb, ib, lam, out_w):
    T, D = x2d.shape
    C = in_w.shape[1] // 2
    tm = ROW_TILE
    ns = S // tm
    vec = lambda n: pl.BlockSpec((1, n), lambda b, s: (0, 0))
    return pl.pallas_call(
        _hawk_kernel,
        grid=(B, ns),
        in_specs=[pl.BlockSpec((tm, D), lambda b, s: (b * ns + s, 0)),
                  vec(D),
                  pl.BlockSpec((D, 2 * C), lambda b, s: (0, 0)),
                  pl.BlockSpec((CONV_WIDTH, C), lambda b, s: (0, 0)),
                  vec(C),
                  pl.BlockSpec((N_RG_BLOCKS, RG_BW, 2 * RG_BW), lambda b, s: (0, 0, 0)),
                  vec(C), vec(C), vec(C),
                  pl.BlockSpec((C, D), lambda b, s: (0, 0))],
        out_specs=pl.BlockSpec((tm, D), lambda b, s: (b * ns + s, 0)),
        out_shape=jax.ShapeDtypeStruct((T, D), F32),
        scratch_shapes=[pltpu.VMEM((tm, C), F32),
                        pltpu.VMEM((tm + SUBLANES, C), F32),
                        pltpu.VMEM((tm, C), F32),
                        pltpu.VMEM((tm, C), F32),
                        pltpu.VMEM((SUBLANES, C), F32)],
        compiler_params=_cparams(("arbitrary", "arbitrary")),
        name="hawk_layer",
    )(x2d, g, in_w, cw, cb, wri, rb, ib, lam, out_w)


def _emit_q(u, qg, cosf, sinf, rows, q_ref, q16_ref, sg_ref):
    d_attn = N_HEADS * HEAD_DIM
    qs = _head_norm_rope(u[:, :d_attn], qg, cosf, sinf)
    for hd in range(N_HEADS):
        q_ref[0, hd, rows, :] = qs[hd]
        q16_ref[0, hd, rows, :] = qs[hd].astype(BF16)
    sg_ref[rows, :] = _silu(u[:, d_attn:])


def _kvq_proj_kernel(x_ref, gkv_ref, wkv_ref, kg_ref, gq_ref, wq_ref, qg_ref, cos_ref, sin_ref,
                     k_ref, v_ref, km_ref, q_ref, q16_ref, sg_ref):
    tm = x_ref.shape[0]
    d_attn = N_HEADS * HEAD_DIM
    ksum = {}
    for t in range(tm // SUB_ROWS):
        rows = pl.ds(t * SUB_ROWS, SUB_ROWS)
        x = x_ref[rows, :]
        xn = x * lax.rsqrt(jnp.mean(x * x, axis=-1, keepdims=True) + EPS)
        kv = jnp.dot((xn * gkv_ref[...]).astype(BF16), wkv_ref[...], preferred_element_type=F32)
        u = jnp.dot((xn * gq_ref[...]).astype(BF16), wq_ref[...], preferred_element_type=F32)
        cosf, sinf = cos_ref[rows, :], sin_ref[rows, :]
        ks = _head_norm_rope(kv[:, :d_attn], kg_ref[...], cosf, sinf)
        for hd in range(N_HEADS):
            k_ref[0, hd, rows, :] = ks[hd].astype(BF16)
            v_ref[0, hd, rows, :] = kv[:, d_attn + hd * HEAD_DIM:d_attn + (hd + 1) * HEAD_DIM].astype(BF16)
            key = (hd, (t * SUB_ROWS) // MOBA_BLOCK)
            part = jnp.sum(ks[hd], axis=0, keepdims=True)
            ksum[key] = part if key not in ksum else ksum[key] + part
        _emit_q(u, qg_ref[...], cosf, sinf, rows, q_ref, q16_ref, sg_ref)
    for (hd, r), total in ksum.items():
        km_ref[0, r, pl.ds(hd, 1), :] = total * (1.0 / MOBA_BLOCK)


def _kvq_proj(x2d, B, S, gkv, wkv, kg, gq, wq, qg, cosf, sinf):
    T, D = x2d.shape
    tm = ROW_TILE
    ns = S // tm
    nb = S // MOBA_BLOCK
    rpt = tm // MOBA_BLOCK
    d_attn = N_HEADS * HEAD_DIM
    vec = lambda n: pl.BlockSpec((1, n), lambda b, s: (0, 0))
    wspec = pl.BlockSpec((D, 2 * d_attn), lambda b, s: (0, 0))
    rope = pl.BlockSpec((tm, HEAD_DIM), lambda b, s: (s, 0))
    heads = pl.BlockSpec((1, N_HEADS, tm, HEAD_DIM), lambda b, s: (b, 0, s, 0))
    hshape = lambda dt: jax.ShapeDtypeStruct((B, N_HEADS, S, HEAD_DIM), dt)
    return pl.pallas_call(
        _kvq_proj_kernel,
        grid=(B, ns),
        in_specs=[pl.BlockSpec((tm, D), lambda b, s: (b * ns + s, 0)),
                  vec(D), wspec, vec(HEAD_DIM), vec(D), wspec, vec(HEAD_DIM), rope, rope],
        out_specs=[heads, heads,
                   pl.BlockSpec((1, rpt, N_HEADS, HEAD_DIM), lambda b, s: (b, s, 0, 0)),
                   heads, heads,
                   pl.BlockSpec((tm, d_attn), lambda b, s: (b * ns + s, 0))],
        out_shape=[hshape(BF16), hshape(BF16),
                   jax.ShapeDtypeStruct((B, nb, N_HEADS, HEAD_DIM), F32),
                   hshape(F32), hshape(BF16),
                   jax.ShapeDtypeStruct((T, d_attn), F32)],
        compiler_params=_cparams(("arbitrary", "arbitrary")),
        name="kvq_proj",
    )(x2d, gkv, wkv, kg, gq, wq, qg, cosf, sinf)


def _q_proj_kernel(x_ref, g_ref, w_ref, qg_ref, cos_ref, sin_ref, q_ref, q16_ref, sg_ref):
    for t in range(x_ref.shape[0] // SUB_ROWS):
        rows = pl.ds(t * SUB_ROWS, SUB_ROWS)
        h = _rms_norm(x_ref[rows, :], g_ref[...]).astype(BF16)
        u = jnp.dot(h, w_ref[...], preferred_element_type=F32)
        _emit_q(u, qg_ref[...], cos_ref[rows, :], sin_ref[rows, :], rows, q_ref, q16_ref, sg_ref)


def _q_proj(x2d, B, S, g, w, qg, cosf, sinf):
    T, D = x2d.shape
    tm = ROW_TILE
    ns = S // tm
    d_attn = N_HEADS * HEAD_DIM
    heads = pl.BlockSpec((1, N_HEADS, tm, HEAD_DIM), lambda b, s: (b, 0, s, 0))
    return pl.pallas_call(
        _q_proj_kernel,
        grid=(B, ns),
        in_specs=[pl.BlockSpec((tm, D), lambda b, s: (b * ns + s, 0)),
                  pl.BlockSpec((1, D), lambda b, s: (0, 0)),
                  pl.BlockSpec((D, 2 * d_attn), lambda b, s: (0, 0)),
                  pl.BlockSpec((1, HEAD_DIM), lambda b, s: (0, 0)),
                  pl.BlockSpec((tm, HEAD_DIM), lambda b, s: (s, 0)),
                  pl.BlockSpec((tm, HEAD_DIM), lambda b, s: (s, 0))],
        out_specs=[heads, heads, pl.BlockSpec((tm, d_attn), lambda b, s: (b * ns + s, 0))],
        out_shape=[jax.ShapeDtypeStruct((B, N_HEADS, S, HEAD_DIM), F32),
                   jax.ShapeDtypeStruct((B, N_HEADS, S, HEAD_DIM), BF16),
                   jax.ShapeDtypeStruct((T, d_attn), F32)],
        compiler_params=_cparams(("arbitrary", "arbitrary")),
        name="q_proj",
    )(x2d, g, w, qg, cosf, sinf)


def _moba_route_kernel(q_ref, km_ref, rt_ref, cnt_ref, run_ref):
    step = pl.program_id(2)
    nb = km_ref.shape[2]

    @pl.when(step == 0)
    def _():
        run_ref[...] = jnp.zeros_like(run_ref)

    km = km_ref[0, 0].astype(BF16)
    blk = lax.broadcasted_iota(I32, (nb, MOBA_BLOCK), 0)
    qa = lax.broadcasted_iota(I32, (MOBA_BLOCK, MOBA_BLOCK), 0)
    qb = lax.broadcasted_iota(I32, (MOBA_BLOCK, MOBA_BLOCK), 1)
    earlier = jnp.where(qa < qb, 1.0, 0.0).astype(BF16)
    run = run_ref[...]
    for u in range(ROUTE_TILES):
        j = step * ROUTE_TILES + u
        g = _nt_dot(km, q_ref[0, 0, pl.ds(u * MOBA_BLOCK, MOBA_BLOCK), :].astype(BF16))
        g = jnp.where(blk < j, g, -jnp.inf)
        hits, ids = [], []
        for _ in range(MOBA_TOPK):
            mx = jnp.max(g, axis=0, keepdims=True)
            idx = jnp.min(jnp.where(g == mx, blk, nb), axis=0, keepdims=True)
            hit = blk == idx
            ok = mx > -jnp.inf
            hits.append(jnp.where(hit & ok, 1.0, 0.0))
            ids.append(jnp.where(ok, idx, -1))
            g = jnp.where(hit, -jnp.inf, g)
        sel = hits[0] + hits[1] + hits[2]

        rank = jnp.dot(sel.astype(BF16), earlier, preferred_element_type=F32)
        dest = run + rank
        run = run + jnp.sum(sel, axis=1, keepdims=True)
        for r in range(MOBA_TOPK):
            loc = jnp.sum(hits[r] * dest, axis=0, keepdims=True).astype(I32)
            rt_ref[u, pl.ds(r, 1), :] = jnp.where(ids[r] >= 0, ids[r] * ROUTE_BLK_UNIT + loc, -1)
        rt_ref[u, pl.ds(MOBA_TOPK, SUBLANES - MOBA_TOPK), :] = jnp.full(
            (SUBLANES - MOBA_TOPK, MOBA_BLOCK), -1, I32)
    run_ref[...] = run
    cnt_ref[0] = run.astype(I32)


def _moba_route(q, kmean, b):
    _, H, S, Dh = q.shape
    nb = S // MOBA_BLOCK
    rows = ROUTE_TILES * MOBA_BLOCK
    nsteps = nb // ROUTE_TILES
    return pl.pallas_call(
        _moba_route_kernel,
        grid=(1, H, nsteps),
        in_specs=[pl.BlockSpec((1, 1, rows, Dh), lambda _, h, j: (b, h, j, 0)),
                  pl.BlockSpec((1, 1, nb, Dh), lambda _, h, j: (b, h, 0, 0))],
        out_specs=[pl.BlockSpec((ROUTE_TILES, SUBLANES, MOBA_BLOCK), lambda _, h, j: (h * nsteps + j, 0, 0)),
                   pl.BlockSpec((1, nb, 1), lambda _, h, j: (h, 0, 0))],
        out_shape=[jax.ShapeDtypeStruct((H * nb, SUBLANES, MOBA_BLOCK), I32),
                   jax.ShapeDtypeStruct((H, nb, 1), I32)],
        scratch_shapes=[pltpu.VMEM((nb, 1), F32)],
        compiler_params=_cparams(("arbitrary", "arbitrary", "arbitrary")),
        name="moba_route",
    )(q, kmean)


def _moba_layout_kernel(cnt_ref, rt_ref, pos_ref, tblk_ref, tvalid_ref, nt_ref, base_ref, *, max_tiles):
    g = pl.program_id(0)
    nb = rt_ref.shape[0]
    slab_rows = max_tiles * GROUP_TILE
    null_row0 = (max_tiles - 1) * GROUP_TILE

    def per_block(n, first_tile):
        count = cnt_ref[g * nb + n]
        tiles = (count + (GROUP_TILE - 1)) // GROUP_TILE
        base_ref[n] = first_tile * GROUP_TILE

        def per_tile(i, carry):
            tblk_ref[g * max_tiles + first_tile + i] = n
            tvalid_ref[g * max_tiles + first_tile + i] = jnp.minimum(count - i * GROUP_TILE, GROUP_TILE)
            return carry

        lax.fori_loop(0, tiles, per_tile, 0)
        return first_tile + tiles

    used = lax.fori_loop(0, nb, per_block, 0)
    nt_ref[g] = used

    def unused_tile(t, carry):
        tblk_ref[g * max_tiles + t] = nb - 1
        tvalid_ref[g * max_tiles + t] = 0
        return carry

    lax.fori_loop(used, max_tiles, unused_tile, 0)

    code = rt_ref[...].reshape(nb * SUBLANES, MOBA_BLOCK)
    blk = lax.shift_right_arithmetic(code, ROUTE_BLK_SHIFT)
    base = jnp.zeros_like(code)
    for n in range(nb):
        base = jnp.where(blk == n, base_ref[n], base)
    lane = lax.broadcasted_iota(I32, code.shape, 1)
    row = jnp.where(code >= 0, base + (code & (ROUTE_BLK_UNIT - 1)), null_row0 + lane)
    pos_ref[...] = (row + g * slab_rows).reshape(pos_ref.shape)


def _moba_layout(rt, counts, nb, max_tiles):
    G = rt.shape[0] // nb
    smem = pl.BlockSpec(memory_space=pltpu.SMEM)
    return pl.pallas_call(
        functools.partial(_moba_layout_kernel, max_tiles=max_tiles),
        grid_spec=pltpu.PrefetchScalarGridSpec(
            num_scalar_prefetch=1,
            grid=(G,),
            in_specs=[pl.BlockSpec((nb, SUBLANES, MOBA_BLOCK), lambda g, cnt: (g, 0, 0))],
            out_specs=[pl.BlockSpec((nb, SUBLANES, MOBA_BLOCK), lambda g, cnt: (g, 0, 0)), smem, smem, smem],
            scratch_shapes=[pltpu.SMEM((nb,), I32)],
        ),
        out_shape=[jax.ShapeDtypeStruct((G * nb, SUBLANES, MOBA_BLOCK), I32),
                   jax.ShapeDtypeStruct((G * max_tiles,), I32),
                   jax.ShapeDtypeStruct((G * max_tiles,), I32),
                   jax.ShapeDtypeStruct((G,), I32)],
        compiler_params=_cparams(("arbitrary",)),
        name="moba_layout",
    )(counts, rt)


def _sc_mesh():
    return plsc.VectorSubcoreMesh(core_axis_name="c", subcore_axis_name="s")


def _sc_load_indices(pos_hbm, idx_v, t0, n_tiles):
    for c in range(SC_HALVES):
        pltpu.sync_copy(pos_hbm.at[pl.ds(t0, n_tiles), pl.ds(0, MOBA_TOPK), pl.ds(c * IDX_CHUNK, IDX_CHUNK)],
                        idx_v.at[c])


def _sc_scatter_rows(src, src_tile0, pos, n_out_rows):
    D = src.shape[1]
    n_tiles = pos.shape[0]
    per_w = n_tiles // SC_WORKERS
    assert per_w * SC_WORKERS == n_tiles

    @functools.partial(
        pl.kernel, mesh=_sc_mesh(),
        out_type=jax.ShapeDtypeStruct((n_out_rows, D), src.dtype),
        scratch_types=[pltpu.VMEM((SC_HALVES, per_w, MOBA_TOPK, IDX_CHUNK), I32),
                       pltpu.VMEM((SC_HALVES, IDX_CHUNK, D), src.dtype),
                       pltpu.SemaphoreType.DMA((SC_HALVES,)),
                       pltpu.SemaphoreType.DMA((SC_HALVES,))],
        name="sc_scatter_rows",
    )
    def k(src_hbm, pos_hbm, out_hbm, idx_v, rows_v, load_sem, scat_sem):
        t0 = (lax.axis_index("s") * SC_CORES + lax.axis_index("c")) * per_w
        _sc_load_indices(pos_hbm, idx_v, t0, per_w)

        def load(i, c):
            row0 = pl.multiple_of((src_tile0 + t0 + i) * MOBA_BLOCK + c * IDX_CHUNK, IDX_CHUNK)
            return pltpu.make_async_copy(src_hbm.at[pl.ds(row0, IDX_CHUNK)], rows_v.at[c], load_sem.at[c])

        def scatters(i, c):
            return [pltpu.make_async_copy(rows_v.at[c], out_hbm.at[idx_v.at[c, i, r]], scat_sem.at[c])
                    for r in range(MOBA_TOPK)]

        load(0, 0).start()

        @pl.loop(0, per_w)
        def _(i):
            for c in range(SC_HALVES):
                pi, pc = (i, 0) if c else (i - 1, 1)
                ni, nc = (i + 1, 0) if c else (i, 1)
                load(i, c).wait()
                for cp in scatters(i, c):
                    cp.start()

                @pl.when(pi >= 0)
                def _():
                    for cp in scatters(pi, pc):
                        cp.wait()

                @pl.when(ni < per_w)
                def _():
                    load(ni, nc).start()

        for cp in scatters(per_w - 1, SC_HALVES - 1):
            cp.wait()

    return k(src, pos)


def _sc_gather_rows(table, pos):
    R, D = table.shape
    n_tiles = pos.shape[0]
    N = n_tiles * MOBA_BLOCK
    per_w = n_tiles // SC_WORKERS
    assert per_w * SC_WORKERS == n_tiles

    items = SC_HALVES * MOBA_TOPK
    n_buf = 2
    assert items % n_buf == 0

    @functools.partial(
        pl.kernel, mesh=_sc_mesh(),
        out_type=jax.ShapeDtypeStruct((MOBA_TOPK, N, D), table.dtype),
        scratch_types=[pltpu.VMEM((SC_HALVES, per_w, MOBA_TOPK, IDX_CHUNK), I32),
                       pltpu.VMEM((n_buf, IDX_CHUNK, D), table.dtype),
                       pltpu.SemaphoreType.DMA((n_buf,)),
                       pltpu.SemaphoreType.DMA((n_buf,))],
        name="sc_gather_rows",
    )
    def k(table_hbm, pos_hbm, out_hbm, idx_v, rows_v, gat_sem, put_sem):
        t0 = (lax.axis_index("s") * SC_CORES + lax.axis_index("c")) * per_w
        _sc_load_indices(pos_hbm, idx_v, t0, per_w)

        def gather(i, j):
            c, r = divmod(j, MOBA_TOPK)
            return pltpu.make_async_copy(table_hbm.at[idx_v.at[c, i, r]], rows_v.at[j % n_buf], gat_sem.at[j % n_buf])

        def put(i, j):
            c, r = divmod(j, MOBA_TOPK)
            row0 = pl.multiple_of((t0 + i) * MOBA_BLOCK + c * IDX_CHUNK, IDX_CHUNK)
            return pltpu.make_async_copy(rows_v.at[j % n_buf], out_hbm.at[r, pl.ds(row0, IDX_CHUNK)],
                                         put_sem.at[j % n_buf])

        gather(0, 0).start()

        @pl.loop(0, per_w)
        def _(i):
            for j in range(items):
                pi, pj = (i, j - 1) if j else (i - 1, items - 1)
                ni, nj = (i, j + 1) if j + 1 < items else (i + 1, 0)
                gather(i, j).wait()
                put(i, j).start()

                @pl.when(pi >= 0)
                def _():
                    put(pi, pj).wait()

                @pl.when(ni < per_w)
                def _():
                    gather(ni, nj).start()

        put(per_w - 1, items - 1).wait()

    return k(table, pos)


def _bits(x):
    return lax.bitcast_convert_type(x, I32)


def _pack_partial(part, m):
    lane = lax.broadcasted_iota(I32, (part.shape[0], HEAD_DIM), 1)
    lo = _bits(part[:, :HEAD_DIM]) + BF16_HALF_ULP
    hi = _bits(jnp.where(lane == L_LANE, m, part[:, HEAD_DIM:])) + BF16_HALF_ULP
    return (hi & HI16_MASK) | lax.shift_right_logical(lo, 16)


def _null_partial(rows):
    lane = lax.broadcasted_iota(I32, (rows, HEAD_DIM), 1)
    return jnp.where(lane == L_LANE, NEG_INF_BITS & HI16_MASK, 0)


def _unpack_partial(word):
    lo = lax.bitcast_convert_type(lax.shift_left(word, 16), F32)
    hi = lax.bitcast_convert_type(word & HI16_MASK, F32)
    return lo, hi


def _augment_v(v):
    vf = v.astype(F32)
    lane = lax.broadcasted_iota(I32, vf.shape, 1)
    lo = jnp.where(lane < HALF, vf, jnp.where(lane == L_LANE, 1.0, 0.0))
    hi = jnp.where(lane < HALF, pltpu.roll(vf, HALF, 1), 0.0)
    return jnp.concatenate([lo, hi], axis=1).astype(BF16)


def _moba_group_kernel(tblk_ref, tvalid_ref, nt_ref, qg_ref, k_ref, v_ref, o_ref, va_ref, *, max_tiles):
    g, step = pl.program_id(0), pl.program_id(1)
    t0 = g * max_tiles + step * TILES_PER_STEP
    scale = HEAD_DIM ** -0.5

    @pl.when(step == 0)
    def _():
        def build(c, carry):
            rows = pl.ds(pl.multiple_of(c * V_BUILD_ROWS, V_BUILD_ROWS), V_BUILD_ROWS)
            va_ref[rows, :] = _augment_v(v_ref[0, 0, rows, :])
            return carry

        lax.fori_loop(0, va_ref.shape[0] // V_BUILD_ROWS, build, 0)

    @pl.when(tvalid_ref[t0] > 0)
    def _():
        for u in range(TILES_PER_STEP):
            rows = pl.ds(u * GROUP_TILE, GROUP_TILE)
            q = qg_ref[rows, :].astype(BF16)
            n0 = pl.multiple_of(tblk_ref[t0 + u] * MOBA_BLOCK, MOBA_BLOCK)
            s = _nt_dot(q, k_ref[0, 0, pl.ds(n0, MOBA_BLOCK), :])
            m = (jnp.max(s, axis=1, keepdims=True) * (scale * LOG2_E)).astype(BF16).astype(F32)
            p = jnp.exp2(s * (scale * LOG2_E) - m).astype(BF16)
            part = jnp.dot(p, va_ref[pl.ds(n0, MOBA_BLOCK), :], preferred_element_type=F32)
            o_ref[rows, :] = _pack_partial(part, m)

    @pl.when(step == pl.num_programs(1) - 1)
    def _():
        o_ref[pl.ds((TILES_PER_STEP - 1) * GROUP_TILE, GROUP_TILE), :] = _null_partial(GROUP_TILE)


def _moba_group(qg, k, v, b, tblk, tvalid, ntiles, max_tiles):
    _, H, S, Dh = k.shape
    G = H
    n_steps = max_tiles // TILES_PER_STEP
    step_rows = TILES_PER_STEP * GROUP_TILE

    def tile_idx(g, s, tblk, tvalid, nt):
        return (g * n_steps + jnp.where(s * TILES_PER_STEP < nt[g], s, n_steps - 1), 0)

    return pl.pallas_call(
        functools.partial(_moba_group_kernel, max_tiles=max_tiles),
        grid_spec=pltpu.PrefetchScalarGridSpec(
            num_scalar_prefetch=3,
            grid=(G, n_steps),
            in_specs=[pl.BlockSpec((step_rows, Dh), tile_idx),
                      pl.BlockSpec((1, 1, S, Dh), lambda g, s, *_: (b, g, 0, 0)),
                      pl.BlockSpec((1, 1, S, Dh), lambda g, s, *_: (b, g, 0, 0))],
            out_specs=pl.BlockSpec((step_rows, LANES), tile_idx),
            scratch_shapes=[pltpu.VMEM((S, VA_W), BF16)],
        ),
        out_shape=jax.ShapeDtypeStruct((G * max_tiles * GROUP_TILE, LANES), I32),
        compiler_params=_cparams(("arbitrary", "arbitrary")),
        name="moba_group",
    )(tblk, tvalid, ntiles, qg, k, v)


def _moba_merge_kernel(*refs, n_skip):
    q_ref, k_ref, v_ref, sg_ref, x_ref, outw_ref = refs[n_skip:n_skip + 6]
    og_refs, out_ref, y_ref = refs[n_skip + 6:-2], refs[-2], refs[-1]
    scale = HEAD_DIM ** -0.5
    qi = lax.broadcasted_iota(I32, (MOBA_BLOCK, MOBA_BLOCK), 0)
    ki = lax.broadcasted_iota(I32, (MOBA_BLOCK, MOBA_BLOCK), 1)
    lane = lax.broadcasted_iota(I32, (MOBA_BLOCK, HEAD_DIM), 1)
    spread = jnp.where(lax.broadcasted_iota(I32, (HEAD_DIM, HEAD_DIM), 0) == L_LANE, 1.0, 0.0).astype(BF16)
    for u, t in [(u, t) for t in range(MERGE_TILES) for u in range(len(og_refs))]:
        og_ref = og_refs[u]
        rows = pl.ds(t * MOBA_BLOCK, MOBA_BLOCK)
        q = q_ref[0, u, rows, :].astype(BF16)
        s = _nt_dot(q, k_ref[0, u, rows, :])
        s = jnp.where(ki <= qi, s, NEG_INF)
        m_raw = jnp.max(s, axis=1, keepdims=True)
        p = jnp.exp2((s - m_raw) * (scale * LOG2_E)).astype(BF16)
        own = jnp.dot(p, _augment_v(v_ref[0, u, rows, :]), preferred_element_type=F32)
        parts = [_unpack_partial(og_ref[r, rows, :]) for r in range(MOBA_TOPK)]
        ms = [jnp.dot(jnp.where(lane == L_LANE, hi, 0.0).astype(BF16), spread, preferred_element_type=F32)
              for _, hi in parts]
        m_own = jnp.broadcast_to(m_raw * (scale * LOG2_E), lane.shape)
        m_all = m_own
        for m in ms:
            m_all = jnp.maximum(m_all, m)
        w_own = jnp.exp2(m_own - m_all)
        tot_lo = w_own * own[:, :HEAD_DIM]
        tot_hi = w_own * own[:, HEAD_DIM:]
        for (lo, hi), m in zip(parts, ms):
            w = jnp.exp2(m - m_all)
            tot_lo = tot_lo + w * lo
            tot_hi = tot_hi + w * hi
        acc = jnp.where(lane < HALF, tot_lo, pltpu.roll(tot_hi, HALF, 1))
        cols = pl.ds(u * HEAD_DIM, HEAD_DIM)
        y_ref[rows, cols] = ((acc / tot_lo[:, L_LANE:L_LANE + 1]) * sg_ref[rows, cols]).astype(BF16)

    out_ref[...] = x_ref[...] + jnp.dot(y_ref[...], outw_ref[...], preferred_element_type=F32)


def _moba_merge(q, k, v, sg, og, x2d, out_w, b, o_prev):
    B, H, S, Dh = q.shape
    D = x2d.shape[1]
    hps = HEADS_PER_STEP
    assert hps == H
    rows = MERGE_TILES * MOBA_BLOCK
    ns = S // rows
    og_spec = lambda u: pl.BlockSpec((MOBA_TOPK, rows, LANES),
                                     lambda _, hp, j: (0, (hp * hps + u) * ns + j, 0))
    prev_specs = [] if o_prev is None else [pl.BlockSpec(memory_space=pl.ANY)]
    prev_args = [] if o_prev is None else [o_prev]
    return pl.pallas_call(
        functools.partial(_moba_merge_kernel, n_skip=len(prev_args)),
        grid=(1, H // hps, ns),
        in_specs=prev_specs
                 + [pl.BlockSpec((1, hps, rows, Dh), lambda _, hp, j: (b, hp, j, 0)),
                    pl.BlockSpec((1, hps, rows, Dh), lambda _, hp, j: (b, hp, j, 0)),
                    pl.BlockSpec((1, hps, rows, Dh), lambda _, hp, j: (b, hp, j, 0)),
                    pl.BlockSpec((rows, hps * Dh), lambda _, hp, j: (b * ns + j, hp)),
                    pl.BlockSpec((rows, D), lambda _, hp, j: (b * ns + j, 0)),
                    pl.BlockSpec((H * Dh, D), lambda _, hp, j: (0, 0))]
                 + [og_spec(u) for u in range(hps)],
        out_specs=pl.BlockSpec((rows, D), lambda _, hp, j: (b * ns + j, 0)),
        out_shape=jax.ShapeDtypeStruct((B * S, D), F32),
        scratch_shapes=[pltpu.VMEM((rows, H * Dh), BF16)],
        input_output_aliases={} if o_prev is None else {0: 0},
        compiler_params=_cparams(("arbitrary", "arbitrary", "arbitrary")),
        name="moba_merge",
    )(*prev_args, q, k, v, sg, x2d, out_w, *([og] * hps))


def _moba_layer(q, q16, k, v, kmean, sg, x2d, out_w):
    B, H, S, Dh = q.shape
    nb = S // MOBA_BLOCK
    max_tiles = (MOBA_TOPK * S) // GROUP_TILE + nb + 1
    max_tiles = -(-max_tiles // TILES_PER_STEP) * TILES_PER_STEP
    slab_rows = max_tiles * GROUP_TILE

    q_rows = q.reshape(B * H * S, Dh)
    o = None
    for b in range(B):
        rt, counts = _moba_route(q16, kmean, b)
        pos, tblk, tvalid, ntiles = _moba_layout(rt, counts.reshape(-1), nb, max_tiles)
        qg = _sc_scatter_rows(q_rows, b * H * nb, pos, H * slab_rows)
        parts = _moba_group(qg, k, v, b, tblk, tvalid, ntiles, max_tiles)
        og = _sc_gather_rows(parts, pos)
        o = _moba_merge(q16, k, v, sg, og, x2d, out_w, b, o)
    return o


def _rope_tables(S):
    inv = np.float32(ROPE_THETA) ** (-np.arange(HALF, dtype=np.float32) / np.float32(HALF))
    ang = np.arange(S, dtype=np.float32)[:, None] * inv[None, :].astype(np.float32)
    cos, sin = np.cos(ang).astype(np.float32), np.sin(ang).astype(np.float32)
    return (jnp.asarray(np.concatenate([cos, cos], axis=-1)),
            jnp.asarray(np.concatenate([-sin, sin], axis=-1)))


def kernel(x, a_norm, a_in_w, a_conv_w, a_conv_b, a_r_w, a_r_b, a_i_w, a_i_b, a_lambda, a_out_w,
           kv_norm, kv_w, k_norm, b_norm, b_in_w, q_norm, b_out_w):
    B, S, D = x.shape
    assert S % ROW_TILE == 0 and ROW_TILE % MOBA_BLOCK == 0
    assert D == N_RG_BLOCKS * RG_BW == N_HEADS * HEAD_DIM
    assert (B * N_HEADS * S) % (IDX_CHUNK * SC_WORKERS) == 0
    xs = x.reshape(B * S, D)
    row = lambda v: v.reshape(1, -1)

    for l in range(a_in_w.shape[0]):
        wri = jnp.concatenate([a_r_w[l], a_i_w[l]], axis=-1).astype(BF16)
        xs = _hawk_layer(xs, B, S, row(a_norm[l]), a_in_w[l].astype(BF16), a_conv_w[l], row(a_conv_b[l]), wri,
                         row(a_r_b[l]), row(a_i_b[l]), row(a_lambda[l]), a_out_w[l].astype(BF16))

    cosf, sinf = _rope_tables(S)
    for jl in range(b_in_w.shape[0]):
        q_args = (row(b_norm[jl]), b_in_w[jl].astype(BF16), row(q_norm[jl]))
        if jl == 0:
            k, v, kmean, q, q16, sg = _kvq_proj(xs, B, S, row(kv_norm), kv_w.astype(BF16), row(k_norm),
                                                *q_args, cosf, sinf)
            kmean = jnp.transpose(kmean, (0, 2, 1, 3))
        else:
            q, q16, sg = _q_proj(xs, B, S, *q_args, cosf, sinf)
        xs = _moba_layer(q, q16, k, v, kmean, sg, xs, b_out_w[jl].astype(BF16))
    return xs.reshape(B, S, D)
```

```python
import functools

import jax
import jax.numpy as jnp
import numpy as np
from jax import lax
from jax.experimental import pallas as pl
from jax.experimental.pallas import tpu as pltpu
from jax.experimental.pallas import tpu_sc as plsc

N_HEADS = 8
HEAD_DIM = 128
MOBA_BLOCK = 256
MOBA_TOPK = 3
CONV_WIDTH = 4
N_RG_BLOCKS = 8
RG_BW = 128
RG_C = 8.0
ROPE_THETA = 10000.0
EPS = 1e-6
NEG_INF = -1e30
LOG2_E = 1.4426950408889634
TINY = 1e-30

SUBLANES = 8
LANES = 128
ROW_TILE = 512
SUB_ROWS = 128
VMEM_LIMIT = 56 * 1024 * 1024

GROUP_TILE = MOBA_BLOCK
TILES_PER_STEP = 32
V_BUILD_ROWS = 1024
HEADS_PER_STEP = 8
MERGE_TILES = 2
ROUTE_TILES = 8
ROUTE_BLK_SHIFT = 16
ROUTE_BLK_UNIT = 1 << ROUTE_BLK_SHIFT
HALF = HEAD_DIM // 2
VA_W = 2 * HEAD_DIM
L_LANE = HALF
HI16_MASK = -65536
BF16_HALF_ULP = 0x8000
NEG_INF_BITS = int(np.float32(NEG_INF).view(np.int32))

SC_CORES = 2
SC_SUBCORES = 16
SC_WORKERS = SC_CORES * SC_SUBCORES
IDX_CHUNK = 128
SC_HALVES = MOBA_BLOCK // IDX_CHUNK

F32 = jnp.float32
BF16 = jnp.bfloat16
I32 = jnp.int32


def _cparams(sem):
    return pltpu.CompilerParams(dimension_semantics=sem, vmem_limit_bytes=VMEM_LIMIT)


def _rms_norm(x, g):
    ms = jnp.mean(x * x, axis=-1, keepdims=True)
    return (x * lax.rsqrt(ms + EPS)) * g


def _silu(x):
    hx = 0.5 * x
    return hx * jnp.tanh(hx) + hx


def _sigmoid(x):
    return 0.5 * jnp.tanh(0.5 * x) + 0.5


def _nt_dot(a, b):
    return lax.dot_general(a, b, (((1,), (1,)), ((), ())), preferred_element_type=F32)


def _head_norm_rope(z, g, cosf, sinf):
    outs = []
    for h in range(N_HEADS):
        zh = z[:, h * HEAD_DIM:(h + 1) * HEAD_DIM]
        zh = _rms_norm(zh, g)
        outs.append(zh * cosf + pltpu.roll(zh, HEAD_DIM // 2, 1) * sinf)
    return outs


def _hawk_kernel(x_ref, g_ref, inw_ref, cw_ref, cb_ref, wri_ref, rb_ref, ib_ref, lam_ref, outw_ref,
                 o_ref, gate_ref, xpad_ref, a_ref, b_ref, hc_ref):
    tm, C = gate_ref.shape
    s = pl.program_id(1)

    @pl.when(s == 0)
    def _():
        xpad_ref[pl.ds(0, SUBLANES), :] = jnp.zeros((SUBLANES, xpad_ref.shape[1]), F32)
        hc_ref[...] = jnp.zeros_like(hc_ref)

    h = _rms_norm(x_ref[...], g_ref[...]).astype(BF16)
    gate_ref[...] = jnp.dot(h, inw_ref[:, C:], preferred_element_type=F32)

    xpad_ref[pl.ds(SUBLANES, tm), :] = jnp.dot(h, inw_ref[:, :C], preferred_element_type=F32)
    cw = cw_ref[...]
    xc = xpad_ref[pl.ds(SUBLANES - (CONV_WIDTH - 1), tm), :] * cw[0:1, :]
    for k in range(1, CONV_WIDTH):
        xc = xc + xpad_ref[pl.ds(SUBLANES - (CONV_WIDTH - 1) + k, tm), :] * cw[k:k + 1, :]
    xc = xc + cb_ref[...]
    xpad_ref[pl.ds(0, SUBLANES), :] = xpad_ref[pl.ds(tm, SUBLANES), :]

    lam = lam_ref[...]
    sp = jnp.maximum(-lam, 0.0) + jnp.log1p(jnp.exp(-jnp.abs(lam)))
    log2a_per_r = (-RG_C * LOG2_E) * sp
    xc16 = xc.astype(BF16)
    for g in range(N_RG_BLOCKS):
        lo = g * RG_BW
        z = jnp.dot(xc16[:, lo:lo + RG_BW], wri_ref[g], preferred_element_type=F32)
        r = _sigmoid(z[:, :RG_BW] + rb_ref[:, lo:lo + RG_BW])
        i = _sigmoid(z[:, RG_BW:] + ib_ref[:, lo:lo + RG_BW])
        a = jnp.exp2(r * log2a_per_r[:, lo:lo + RG_BW])
        om = 1.0 - a * a
        mult = om * lax.rsqrt(jnp.maximum(om, TINY))
        a_ref[:, lo:lo + RG_BW] = a
        b_ref[:, lo:lo + RG_BW] = mult * (i * xc[:, lo:lo + RG_BW])

    row = lax.broadcasted_iota(I32, (SUBLANES, a_ref.shape[1]), 0)

    def scan_body(c, hc):
        r0 = pl.multiple_of(c * SUBLANES, SUBLANES)
        a = a_ref[pl.ds(r0, SUBLANES), :]
        b = b_ref[pl.ds(r0, SUBLANES), :]
        for k in (1, 2, 4):
            keep = row >= k
            b = jnp.where(keep, a * pltpu.roll(b, k, 0) + b, b)
            a = jnp.where(keep, a * pltpu.roll(a, k, 0), a)
        h = a * hc + b
        b_ref[pl.ds(r0, SUBLANES), :] = h
        return h[SUBLANES - 1:SUBLANES, :]

    hc = lax.fori_loop(0, tm // SUBLANES, scan_body, hc_ref[0:1, :])
    hc_ref[0:1, :] = hc

    y = (b_ref[...] * _silu(gate_ref[...])).astype(BF16)
    o_ref[...] = x_ref[...] + jnp.dot(y, outw_ref[...], preferred_element_type=F32)


def _hawk_layer(x2d, B, S, g, in_w, cw, cb, wri, rb, ib, lam, out_w):
    T, D = x2d.shape
    C = in_w.shape[1] // 2
    tm = ROW_TILE
    ns = S // tm
    vec = lambda n: pl.BlockSpec((1, n), lambda b, s: (0, 0))
    return pl.pallas_call(
        _hawk_kernel,
        grid=(B, ns),
        in_specs=[pl.BlockSpec((tm, D), lambda b, s: (b * ns + s, 0)),
                  vec(D),
                  pl.BlockSpec((D, 2 * C), lambda b, s: (0, 0)),
                  pl.BlockSpec((CONV_WIDTH, C), lambda b, s: (0, 0)),
                  vec(C),
                  pl.BlockSpec((N_RG_BLOCKS, RG_BW, 2 * RG_BW), lambda b, s: (0, 0, 0)),
                  vec(C), vec(C), vec(C),
                  pl.BlockSpec((C, D), lambda b, s: (0, 0))],
        out_specs=pl.BlockSpec((tm, D), lambda b, s: (b * ns + s, 0)),
        out_shape=jax.ShapeDtypeStruct((T, D), F32),
        scratch_shapes=[pltpu.VMEM((tm, C), F32),
                        pltpu.VMEM((tm + SUBLANES, C), F32),
                        pltpu.VMEM((tm, C), F32),
                        pltpu.VMEM((tm, C), F32),
                        pltpu.VMEM((SUBLANES, C), F32)],
        compiler_params=_cparams(("arbitrary", "arbitrary")),
        name="hawk_layer",
    )(x2d, g, in_w, cw, cb, wri, rb, ib, lam, out_w)


def _emit_q(u, qg, cosf, sinf, t, qp_ref, q16_ref, sg_ref):
    assert SUB_ROWS == IDX_CHUNK
    d_attn = N_HEADS * HEAD_DIM
    rows = pl.ds(t * SUB_ROWS, SUB_ROWS)
    prow = pl.ds((t // 2) * SUB_ROWS, SUB_ROWS)
    qs = _head_norm_rope(u[:, :d_attn], qg, cosf, sinf)
    for hd in range(N_HEADS):
        q16 = qs[hd].astype(BF16)
        q16_ref[0, hd, rows, :] = q16
        r = _bits(q16.astype(F32))
        swapped = pltpu.roll(r, HALF, 1)
        if t % 2 == 0:
            w = (swapped & HI16_MASK) | lax.shift_right_logical(r, 16)
            qp_ref[0, hd, prow, pl.ds(0, HALF)] = w[:, :HALF]
        else:
            w = (r & HI16_MASK) | lax.shift_right_logical(swapped, 16)
            qp_ref[0, hd, prow, pl.ds(HALF, HALF)] = w[:, HALF:]
    sg_ref[rows, :] = _silu(u[:, d_attn:])


def _kvq_proj_kernel(x_ref, gkv_ref, wkv_ref, kg_ref, gq_ref, wq_ref, qg_ref, cos_ref, sin_ref,
                     k_ref, v_ref, km_ref, q_ref, q16_ref, sg_ref):
    tm = x_ref.shape[0]
    d_attn = N_HEADS * HEAD_DIM
    ksum = {}
    for t in range(tm // SUB_ROWS):
        rows = pl.ds(t * SUB_ROWS, SUB_ROWS)
        x = x_ref[rows, :]
        xn = x * lax.rsqrt(jnp.mean(x * x, axis=-1, keepdims=True) + EPS)
        kv = jnp.dot((xn * gkv_ref[...]).astype(BF16), wkv_ref[...], preferred_element_type=F32)
        u = jnp.dot((xn * gq_ref[...]).astype(BF16), wq_ref[...], preferred_element_type=F32)
        cosf, sinf = cos_ref[rows, :], sin_ref[rows, :]
        ks = _head_norm_rope(kv[:, :d_attn], kg_ref[...], cosf, sinf)
        for hd in range(N_HEADS):
            k_ref[0, hd, rows, :] = ks[hd].astype(BF16)
            v_ref[0, hd, rows, :] = kv[:, d_attn + hd * HEAD_DIM:d_attn + (hd + 1) * HEAD_DIM].astype(BF16)
            key = (hd, (t * SUB_ROWS) // MOBA_BLOCK)
            part = jnp.sum(ks[hd], axis=0, keepdims=True)
            ksum[key] = part if key not in ksum else ksum[key] + part
        _emit_q(u, qg_ref[...], cosf, sinf, t, q_ref, q16_ref, sg_ref)
    for (hd, r), total in ksum.items():
        km_ref[0, r, pl.ds(hd, 1), :] = total * (1.0 / MOBA_BLOCK)


def _kvq_proj(x2d, B, S, gkv, wkv, kg, gq, wq, qg, cosf, sinf):
    T, D = x2d.shape
    tm = ROW_TILE
    ns = S // tm
    nb = S // MOBA_BLOCK
    rpt = tm // MOBA_BLOCK
    d_attn = N_HEADS * HEAD_DIM
    vec = lambda n: pl.BlockSpec((1, n), lambda b, s: (0, 0))
    wspec = pl.BlockSpec((D, 2 * d_attn), lambda b, s: (0, 0))
    rope = pl.BlockSpec((tm, HEAD_DIM), lambda b, s: (s, 0))
    heads = pl.BlockSpec((1, N_HEADS, tm, HEAD_DIM), lambda b, s: (b, 0, s, 0))
    hshape = lambda dt: jax.ShapeDtypeStruct((B, N_HEADS, S, HEAD_DIM), dt)
    return pl.pallas_call(
        _kvq_proj_kernel,
        grid=(B, ns),
        in_specs=[pl.BlockSpec((tm, D), lambda b, s: (b * ns + s, 0)),
                  vec(D), wspec, vec(HEAD_DIM), vec(D), wspec, vec(HEAD_DIM), rope, rope],
        out_specs=[heads, heads,
                   pl.BlockSpec((1, rpt, N_HEADS, HEAD_DIM), lambda b, s: (b, s, 0, 0)),
                   pl.BlockSpec((1, N_HEADS, tm // 2, LANES), lambda b, s: (b, 0, s, 0)), heads,
                   pl.BlockSpec((tm, d_attn), lambda b, s: (b * ns + s, 0))],
        out_shape=[hshape(BF16), hshape(BF16),
                   jax.ShapeDtypeStruct((B, nb, N_HEADS, HEAD_DIM), F32),
                   jax.ShapeDtypeStruct((B, N_HEADS, S // 2, LANES), I32), hshape(BF16),
                   jax.ShapeDtypeStruct((T, d_attn), F32)],
        compiler_params=_cparams(("arbitrary", "arbitrary")),
        name="kvq_proj",
    )(x2d, gkv, wkv, kg, gq, wq, qg, cosf, sinf)


def _q_proj_kernel(x_ref, g_ref, w_ref, qg_ref, cos_ref, sin_ref, q_ref, q16_ref, sg_ref):
    for t in range(x_ref.shape[0] // SUB_ROWS):
        rows = pl.ds(t * SUB_ROWS, SUB_ROWS)
        h = _rms_norm(x_ref[rows, :], g_ref[...]).astype(BF16)
        u = jnp.dot(h, w_ref[...], preferred_element_type=F32)
        _emit_q(u, qg_ref[...], cos_ref[rows, :], sin_ref[rows, :], t, q_ref, q16_ref, sg_ref)


def _q_proj(x2d, B, S, g, w, qg, cosf, sinf):
    T, D = x2d.shape
    tm = ROW_TILE
    ns = S // tm
    d_attn = N_HEADS * HEAD_DIM
    heads = pl.BlockSpec((1, N_HEADS, tm, HEAD_DIM), lambda b, s: (b, 0, s, 0))
    return pl.pallas_call(
        _q_proj_kernel,
        grid=(B, ns),
        in_specs=[pl.BlockSpec((tm, D), lambda b, s: (b * ns + s, 0)),
                  pl.BlockSpec((1, D), lambda b, s: (0, 0)),
                  pl.BlockSpec((D, 2 * d_attn), lambda b, s: (0, 0)),
                  pl.BlockSpec((1, HEAD_DIM), lambda b, s: (0, 0)),
                  pl.BlockSpec((tm, HEAD_DIM), lambda b, s: (s, 0)),
                  pl.BlockSpec((tm, HEAD_DIM), lambda b, s: (s, 0))],
        out_specs=[pl.BlockSpec((1, N_HEADS, tm // 2, LANES), lambda b, s: (b, 0, s, 0)), heads,
                   pl.BlockSpec((tm, d_attn), lambda b, s: (b * ns + s, 0))],
        out_shape=[jax.ShapeDtypeStruct((B, N_HEADS, S // 2, LANES), I32),
                   jax.ShapeDtypeStruct((B, N_HEADS, S, HEAD_DIM), BF16),
                   jax.ShapeDtypeStruct((T, d_attn), F32)],
        compiler_params=_cparams(("arbitrary", "arbitrary")),
        name="q_proj",
    )(x2d, g, w, qg, cosf, sinf)


def _moba_route_kernel(q_ref, km_ref, rt_ref, cnt_ref, run_ref):
    step = pl.program_id(2)
    nb = km_ref.shape[2]

    @pl.when(step == 0)
    def _():
        run_ref[...] = jnp.zeros_like(run_ref)

    km = km_ref[0, 0].astype(BF16)
    blk = lax.broadcasted_iota(I32, (nb, MOBA_BLOCK), 0)
    qa = lax.broadcasted_iota(I32, (MOBA_BLOCK, MOBA_BLOCK), 0)
    qb = lax.broadcasted_iota(I32, (MOBA_BLOCK, MOBA_BLOCK), 1)
    earlier = jnp.where(qa < qb, 1.0, 0.0).astype(BF16)
    run = run_ref[...]
    for u in range(ROUTE_TILES):
        j = step * ROUTE_TILES + u
        g = _nt_dot(km, q_ref[0, 0, pl.ds(u * MOBA_BLOCK, MOBA_BLOCK), :].astype(BF16))
        g = jnp.where(blk < j, g, -jnp.inf)
        hits, ids = [], []
        for _ in range(MOBA_TOPK):
            mx = jnp.max(g, axis=0, keepdims=True)
            idx = jnp.min(jnp.where(g == mx, blk, nb), axis=0, keepdims=True)
            hit = blk == idx
            ok = mx > -jnp.inf
            hits.append(jnp.where(hit & ok, 1.0, 0.0))
            ids.append(jnp.where(ok, idx, -1))
            g = jnp.where(hit, -jnp.inf, g)
        sel = hits[0] + hits[1] + hits[2]

        rank = jnp.dot(sel.astype(BF16), earlier, preferred_element_type=F32)
        dest = run + rank
        run = run + jnp.sum(sel, axis=1, keepdims=True)
        for r in range(MOBA_TOPK):
            loc = jnp.sum(hits[r] * dest, axis=0, keepdims=True).astype(I32)
            rt_ref[u, pl.ds(r, 1), :] = jnp.where(ids[r] >= 0, ids[r] * ROUTE_BLK_UNIT + loc, -1)
        rt_ref[u, pl.ds(MOBA_TOPK, SUBLANES - MOBA_TOPK), :] = jnp.full(
            (SUBLANES - MOBA_TOPK, MOBA_BLOCK), -1, I32)
    run_ref[...] = run
    cnt_ref[0] = run.astype(I32)


def _moba_route(q, kmean, b):
    _, H, S, Dh = q.shape
    nb = S // MOBA_BLOCK
    rows = ROUTE_TILES * MOBA_BLOCK
    nsteps = nb // ROUTE_TILES
    return pl.pallas_call(
        _moba_route_kernel,
        grid=(1, H, nsteps),
        in_specs=[pl.BlockSpec((1, 1, rows, Dh), lambda _, h, j: (b, h, j, 0)),
                  pl.BlockSpec((1, 1, nb, Dh), lambda _, h, j: (b, h, 0, 0))],
        out_specs=[pl.BlockSpec((ROUTE_TILES, SUBLANES, MOBA_BLOCK), lambda _, h, j: (h * nsteps + j, 0, 0)),
                   pl.BlockSpec((1, nb, 1), lambda _, h, j: (h, 0, 0))],
        out_shape=[jax.ShapeDtypeStruct((H * nb, SUBLANES, MOBA_BLOCK), I32),
                   jax.ShapeDtypeStruct((H, nb, 1), I32)],
        scratch_shapes=[pltpu.VMEM((nb, 1), F32)],
        compiler_params=_cparams(("arbitrary", "arbitrary", "arbitrary")),
        name="moba_route",
    )(q, kmean)


def _moba_layout_kernel(cnt_ref, rt_ref, pos_ref, qpos_ref, tblk_ref, tvalid_ref, nt_ref, base_ref, *, max_tiles):
    g = pl.program_id(0)
    nb = rt_ref.shape[0]
    slab_rows = max_tiles * GROUP_TILE
    null_row0 = (max_tiles - 1) * GROUP_TILE

    def per_block(n, first_tile):
        count = cnt_ref[g * nb + n]
        tiles = (count + (GROUP_TILE - 1)) // GROUP_TILE
        base_ref[n] = first_tile * GROUP_TILE

        def per_tile(i, carry):
            tblk_ref[g * max_tiles + first_tile + i] = n
            tvalid_ref[g * max_tiles + first_tile + i] = jnp.minimum(count - i * GROUP_TILE, GROUP_TILE)
            return carry

        lax.fori_loop(0, tiles, per_tile, 0)
        return first_tile + tiles

    used = lax.fori_loop(0, nb, per_block, 0)
    nt_ref[g] = used

    def unused_tile(t, carry):
        tblk_ref[g * max_tiles + t] = nb - 1
        tvalid_ref[g * max_tiles + t] = 0
        return carry

    lax.fori_loop(used, max_tiles, unused_tile, 0)

    code = rt_ref[...].reshape(nb * SUBLANES, MOBA_BLOCK)
    blk = lax.shift_right_arithmetic(code, ROUTE_BLK_SHIFT)
    base = jnp.zeros_like(code)
    for n in range(nb):
        base = jnp.where(blk == n, base_ref[n], base)
    lane = lax.broadcasted_iota(I32, code.shape, 1)
    row = jnp.where(code >= 0, base + (code & (ROUTE_BLK_UNIT - 1)), null_row0 + lane) + g * slab_rows
    pos_ref[...] = row.reshape(pos_ref.shape)
    p = row & (GROUP_TILE - 1)
    assert GROUP_TILE == 2 * IDX_CHUNK
    slot = ((p & (IDX_CHUNK - 1)) << 1) | lax.shift_right_logical(p, IDX_CHUNK.bit_length() - 1)
    qpos_ref[...] = ((row - p) | slot).reshape(qpos_ref.shape)


def _moba_layout(rt, counts, nb, max_tiles):
    G = rt.shape[0] // nb
    smem = pl.BlockSpec(memory_space=pltpu.SMEM)
    return pl.pallas_call(
        functools.partial(_moba_layout_kernel, max_tiles=max_tiles),
        grid_spec=pltpu.PrefetchScalarGridSpec(
            num_scalar_prefetch=1,
            grid=(G,),
            in_specs=[pl.BlockSpec((nb, SUBLANES, MOBA_BLOCK), lambda g, cnt: (g, 0, 0))],
            out_specs=[pl.BlockSpec((nb, SUBLANES, MOBA_BLOCK), lambda g, cnt: (g, 0, 0)),
                       pl.BlockSpec((nb, SUBLANES, MOBA_BLOCK), lambda g, cnt: (g, 0, 0)), smem, smem, smem],
            scratch_shapes=[pltpu.SMEM((nb,), I32)],
        ),
        out_shape=[jax.ShapeDtypeStruct((G * nb, SUBLANES, MOBA_BLOCK), I32),
                   jax.ShapeDtypeStruct((G * nb, SUBLANES, MOBA_BLOCK), I32),
                   jax.ShapeDtypeStruct((G * max_tiles,), I32),
                   jax.ShapeDtypeStruct((G * max_tiles,), I32),
                   jax.ShapeDtypeStruct((G,), I32)],
        compiler_params=_cparams(("arbitrary",)),
        name="moba_layout",
    )(counts, rt)


def _sc_mesh():
    return plsc.VectorSubcoreMesh(core_axis_name="c", subcore_axis_name="s")


def _sc_load_indices(pos_hbm, idx_v, t0, n_tiles):
    for c in range(SC_HALVES):
        pltpu.sync_copy(pos_hbm.at[pl.ds(t0, n_tiles), pl.ds(0, MOBA_TOPK), pl.ds(c * IDX_CHUNK, IDX_CHUNK)],
                        idx_v.at[c])


def _sc_scatter_rows(src, src_tile0, pos, n_out_rows):
    D = src.shape[-1]
    n_tiles = pos.shape[0]
    per_w = n_tiles // SC_WORKERS
    assert per_w * SC_WORKERS == n_tiles and src.shape[1:3] == (IDX_CHUNK, SC_HALVES)

    @functools.partial(
        pl.kernel, mesh=_sc_mesh(),
        out_type=jax.ShapeDtypeStruct((n_out_rows, D), src.dtype),
        scratch_types=[pltpu.VMEM((SC_HALVES, per_w, MOBA_TOPK, IDX_CHUNK), I32),
                       pltpu.VMEM((SC_HALVES, IDX_CHUNK, D), src.dtype),
                       pltpu.SemaphoreType.DMA((SC_HALVES,)),
                       pltpu.SemaphoreType.DMA((SC_HALVES,))],
        compiler_params=pltpu.CompilerParams(use_tc_tiling_on_sc=False),
        name="sc_scatter_rows",
    )
    def k(src_hbm, pos_hbm, out_hbm, idx_v, rows_v, load_sem, scat_sem):
        t0 = (lax.axis_index("s") * SC_CORES + lax.axis_index("c")) * per_w
        _sc_load_indices(pos_hbm, idx_v, t0, per_w)

        def load(i, c):
            return pltpu.make_async_copy(src_hbm.at[src_tile0 + t0 + i, :, c, :], rows_v.at[c], load_sem.at[c])

        def scatters(i, c):
            return [pltpu.make_async_copy(rows_v.at[c], out_hbm.at[idx_v.at[c, i, r]], scat_sem.at[c])
                    for r in range(MOBA_TOPK)]

        load(0, 0).start()

        @pl.loop(0, per_w)
        def _(i):
            for c in range(SC_HALVES):
                pi, pc = (i, 0) if c else (i - 1, 1)
                ni, nc = (i + 1, 0) if c else (i, 1)
                load(i, c).wait()
                for cp in scatters(i, c):
                    cp.start()

                @pl.when(pi >= 0)
                def _():
                    for cp in scatters(pi, pc):
                        cp.wait()

                @pl.when(ni < per_w)
                def _():
                    load(ni, nc).start()

        for cp in scatters(per_w - 1, SC_HALVES - 1):
            cp.wait()

    return k(src, pos)


def _sc_gather_rows(table, pos):
    R, D = table.shape
    n_tiles = pos.shape[0]
    N = n_tiles * MOBA_BLOCK
    per_w = n_tiles // SC_WORKERS
    assert per_w * SC_WORKERS == n_tiles

    items = SC_HALVES * MOBA_TOPK
    n_buf = 2
    assert items % n_buf == 0

    @functools.partial(
        pl.kernel, mesh=_sc_mesh(),
        out_type=jax.ShapeDtypeStruct((MOBA_TOPK, N, D), table.dtype),
        scratch_types=[pltpu.VMEM((SC_HALVES, per_w, MOBA_TOPK, IDX_CHUNK), I32),
                       pltpu.VMEM((n_buf, IDX_CHUNK, D), table.dtype),
                       pltpu.SemaphoreType.DMA((n_buf,)),
                       pltpu.SemaphoreType.DMA((n_buf,))],
        name="sc_gather_rows",
    )
    def k(table_hbm, pos_hbm, out_hbm, idx_v, rows_v, gat_sem, put_sem):
        t0 = (lax.axis_index("s") * SC_CORES + lax.axis_index("c")) * per_w
        _sc_load_indices(pos_hbm, idx_v, t0, per_w)

        def gather(i, j):
            c, r = divmod(j, MOBA_TOPK)
            return pltpu.make_async_copy(table_hbm.at[idx_v.at[c, i, r]], rows_v.at[j % n_buf], gat_sem.at[j % n_buf])

        def put(i, j):
            c, r = divmod(j, MOBA_TOPK)
            row0 = pl.multiple_of((t0 + i) * MOBA_BLOCK + c * IDX_CHUNK, IDX_CHUNK)
            return pltpu.make_async_copy(rows_v.at[j % n_buf], out_hbm.at[r, pl.ds(row0, IDX_CHUNK)],
                                         put_sem.at[j % n_buf])

        gather(0, 0).start()

        @pl.loop(0, per_w)
        def _(i):
            for j in range(items):
                pi, pj = (i, j - 1) if j else (i - 1, items - 1)
                ni, nj = (i, j + 1) if j + 1 < items else (i + 1, 0)
                gather(i, j).wait()
                put(i, j).start()

                @pl.when(pi >= 0)
                def _():
                    put(pi, pj).wait()

                @pl.when(ni < per_w)
                def _():
                    gather(ni, nj).start()

        put(per_w - 1, items - 1).wait()

    return k(table, pos)


def _bits(x):
    return lax.bitcast_convert_type(x, I32)


def _pack_partial(part, m):
    lane = lax.broadcasted_iota(I32, (part.shape[0], HEAD_DIM), 1)
    lo = _bits(part[:, :HEAD_DIM]) + BF16_HALF_ULP
    hi = _bits(jnp.where(lane == L_LANE, m, part[:, HEAD_DIM:])) + BF16_HALF_ULP
    return (hi & HI16_MASK) | lax.shift_right_logical(lo, 16)


def _null_partial(rows):
    lane = lax.broadcasted_iota(I32, (rows, HEAD_DIM), 1)
    return jnp.where(lane == L_LANE, NEG_INF_BITS & HI16_MASK, 0)


def _unpack_partial(word):
    lo = lax.bitcast_convert_type(lax.shift_left(word, 16), F32)
    hi = lax.bitcast_convert_type(word & HI16_MASK, F32)
    return lo, hi


def _augment_v(v):
    vf = v.astype(F32)
    lane = lax.broadcasted_iota(I32, vf.shape, 1)
    lo = jnp.where(lane < HALF, vf, jnp.where(lane == L_LANE, 1.0, 0.0))
    hi = jnp.where(lane < HALF, pltpu.roll(vf, HALF, 1), 0.0)
    return jnp.concatenate([lo, hi], axis=1).astype(BF16)


def _moba_group_kernel(tblk_ref, tvalid_ref, nt_ref, qg_ref, k_ref, v_ref, o_ref, va_ref, k2_ref, *, max_tiles):
    g, step = pl.program_id(0), pl.program_id(1)
    t0 = g * max_tiles + step * TILES_PER_STEP
    scale = HEAD_DIM ** -0.5

    @pl.when(step == 0)
    def _():
        def build(c, carry):
            rows = pl.ds(pl.multiple_of(c * V_BUILD_ROWS, V_BUILD_ROWS), V_BUILD_ROWS)
            va_ref[rows, :] = _augment_v(v_ref[0, 0, rows, :])
            kf = k_ref[0, 0, rows, :].astype(F32)
            lane = lax.broadcasted_iota(I32, kf.shape, 1)
            swapped = pltpu.roll(kf, HALF, 1)
            k2_ref[rows, :] = jnp.concatenate([jnp.where(lane < HALF, kf, swapped),
                                               jnp.where(lane < HALF, swapped, kf)], axis=1).astype(BF16)
            return carry

        lax.fori_loop(0, va_ref.shape[0] // V_BUILD_ROWS, build, 0)

    @pl.when(tvalid_ref[t0] > 0)
    def _():
        half_rows = GROUP_TILE // 2
        lane = lax.broadcasted_iota(I32, (half_rows, HEAD_DIM), 1)
        for u in range(TILES_PER_STEP):
            word = qg_ref[pl.ds(u * half_rows, half_rows), :]
            first = lax.bitcast_convert_type(lax.shift_left(word, 16), F32)
            second = lax.bitcast_convert_type(word & HI16_MASK, F32)
            n0 = pl.multiple_of(tblk_ref[t0 + u] * MOBA_BLOCK, MOBA_BLOCK)
            left, right = lane < HALF, lane >= HALF
            q = jnp.concatenate(
                [jnp.concatenate([jnp.where(left, first, 0.0), jnp.where(left, second, 0.0)], axis=1),
                 jnp.concatenate([jnp.where(right, first, 0.0), jnp.where(right, second, 0.0)], axis=1)],
                axis=0).astype(BF16)
            s = _nt_dot(q, k2_ref[pl.ds(n0, MOBA_BLOCK), :])
            m = (jnp.max(s, axis=1, keepdims=True) * (scale * LOG2_E)).astype(BF16).astype(F32)
            p = jnp.exp2(s * (scale * LOG2_E) - m).astype(BF16)
            part = jnp.dot(p, va_ref[pl.ds(n0, MOBA_BLOCK), :], preferred_element_type=F32)
            o_ref[pl.ds(u * GROUP_TILE, GROUP_TILE), :] = _pack_partial(part, m)

    @pl.when(step == pl.num_programs(1) - 1)
    def _():
        o_ref[pl.ds((TILES_PER_STEP - 1) * GROUP_TILE, GROUP_TILE), :] = _null_partial(GROUP_TILE)


def _moba_group(qg, k, v, b, tblk, tvalid, ntiles, max_tiles):
    _, H, S, Dh = k.shape
    G = H
    n_steps = max_tiles // TILES_PER_STEP
    step_rows = TILES_PER_STEP * GROUP_TILE

    def tile_idx(g, s, tblk, tvalid, nt):
        return (g * n_steps + jnp.where(s * TILES_PER_STEP < nt[g], s, n_steps - 1), 0)

    return pl.pallas_call(
        functools.partial(_moba_group_kernel, max_tiles=max_tiles),
        grid_spec=pltpu.PrefetchScalarGridSpec(
            num_scalar_prefetch=3,
            grid=(G, n_steps),
            in_specs=[pl.BlockSpec((step_rows // 2, LANES), tile_idx),
                      pl.BlockSpec((1, 1, S, Dh), lambda g, s, *_: (b, g, 0, 0)),
                      pl.BlockSpec((1, 1, S, Dh), lambda g, s, *_: (b, g, 0, 0))],
            out_specs=pl.BlockSpec((step_rows, LANES), tile_idx),
            scratch_shapes=[pltpu.VMEM((S, VA_W), BF16), pltpu.VMEM((S, 2 * Dh), BF16)],
        ),
        out_shape=jax.ShapeDtypeStruct((G * max_tiles * GROUP_TILE, LANES), I32),
        compiler_params=_cparams(("arbitrary", "arbitrary")),
        name="moba_group",
    )(tblk, tvalid, ntiles, qg, k, v)


def _moba_merge_kernel(*refs, n_skip):
    q_ref, k_ref, v_ref, sg_ref, x_ref, outw_ref = refs[n_skip:n_skip + 6]
    og_refs, out_ref, y_ref = refs[n_skip + 6:-2], refs[-2], refs[-1]
    scale = HEAD_DIM ** -0.5
    qi = lax.broadcasted_iota(I32, (MOBA_BLOCK, MOBA_BLOCK), 0)
    ki = lax.broadcasted_iota(I32, (MOBA_BLOCK, MOBA_BLOCK), 1)
    lane = lax.broadcasted_iota(I32, (MOBA_BLOCK, HEAD_DIM), 1)
    spread = jnp.where(lax.broadcasted_iota(I32, (HEAD_DIM, HEAD_DIM), 0) == L_LANE, 1.0, 0.0).astype(BF16)
    for u, t in [(u, t) for t in range(MERGE_TILES) for u in range(len(og_refs))]:
        og_ref = og_refs[u]
        rows = pl.ds(t * MOBA_BLOCK, MOBA_BLOCK)
        q = q_ref[0, u, rows, :].astype(BF16)
        s = _nt_dot(q, k_ref[0, u, rows, :])
        s = jnp.where(ki <= qi, s, NEG_INF)
        m_raw = jnp.max(s, axis=1, keepdims=True)
        p = jnp.exp2((s - m_raw) * (scale * LOG2_E)).astype(BF16)
        own = jnp.dot(p, _augment_v(v_ref[0, u, rows, :]), preferred_element_type=F32)
        parts = [_unpack_partial(og_ref[r, rows, :]) for r in range(MOBA_TOPK)]
        ms = [jnp.dot(jnp.where(lane == L_LANE, hi, 0.0).astype(BF16), spread, preferred_element_type=F32)
              for _, hi in parts]
        m_own = jnp.broadcast_to(m_raw * (scale * LOG2_E), lane.shape)
        m_all = m_own
        for m in ms:
            m_all = jnp.maximum(m_all, m)
        w_own = jnp.exp2(m_own - m_all)
        tot_lo = w_own * own[:, :HEAD_DIM]
        tot_hi = w_own * own[:, HEAD_DIM:]
        for (lo, hi), m in zip(parts, ms):
            w = jnp.exp2(m - m_all)
            tot_lo = tot_lo + w * lo
            tot_hi = tot_hi + w * hi
        acc = jnp.where(lane < HALF, tot_lo, pltpu.roll(tot_hi, HALF, 1))
        cols = pl.ds(u * HEAD_DIM, HEAD_DIM)
        y_ref[rows, cols] = ((acc / tot_lo[:, L_LANE:L_LANE + 1]) * sg_ref[rows, cols]).astype(BF16)

    out_ref[...] = x_ref[...] + jnp.dot(y_ref[...], outw_ref[...], preferred_element_type=F32)


def _moba_merge(q, k, v, sg, og, x2d, out_w, b, o_prev):
    B, H, S, Dh = q.shape
    D = x2d.shape[1]
    hps = HEADS_PER_STEP
    assert hps == H
    rows = MERGE_TILES * MOBA_BLOCK
    ns = S // rows
    og_spec = lambda u: pl.BlockSpec((MOBA_TOPK, rows, LANES),
                                     lambda _, hp, j: (0, (hp * hps + u) * ns + j, 0))
    prev_specs = [] if o_prev is None else [pl.BlockSpec(memory_space=pl.ANY)]
    prev_args = [] if o_prev is None else [o_prev]
    return pl.pallas_call(
        functools.partial(_moba_merge_kernel, n_skip=len(prev_args)),
        grid=(1, H // hps, ns),
        in_specs=prev_specs
                 + [pl.BlockSpec((1, hps, rows, Dh), lambda _, hp, j: (b, hp, j, 0)),
                    pl.BlockSpec((1, hps, rows, Dh), lambda _, hp, j: (b, hp, j, 0)),
                    pl.BlockSpec((1, hps, rows, Dh), lambda _, hp, j: (b, hp, j, 0)),
                    pl.BlockSpec((rows, hps * Dh), lambda _, hp, j: (b * ns + j, hp)),
                    pl.BlockSpec((rows, D), lambda _, hp, j: (b * ns + j, 0)),
                    pl.BlockSpec((H * Dh, D), lambda _, hp, j: (0, 0))]
                 + [og_spec(u) for u in range(hps)],
        out_specs=pl.BlockSpec((rows, D), lambda _, hp, j: (b * ns + j, 0)),
        out_shape=jax.ShapeDtypeStruct((B * S, D), F32),
        scratch_shapes=[pltpu.VMEM((rows, H * Dh), BF16)],
        input_output_aliases={} if o_prev is None else {0: 0},
        compiler_params=_cparams(("arbitrary", "arbitrary", "arbitrary")),
        name="moba_merge",
    )(*prev_args, q, k, v, sg, x2d, out_w, *([og] * hps))


def _moba_layer(qp, q16, k, v, kmean, sg, x2d, out_w):
    B, H, S, Dh = q16.shape
    nb = S // MOBA_BLOCK
    max_tiles = (MOBA_TOPK * S) // GROUP_TILE + nb + 1
    max_tiles = -(-max_tiles // TILES_PER_STEP) * TILES_PER_STEP
    slab_rows = max_tiles * GROUP_TILE

    q_rows = qp.reshape(B * H * nb, IDX_CHUNK, SC_HALVES, HALF)
    o = None
    for b in range(B):
        rt, counts = _moba_route(q16, kmean, b)
        pos, qpos, tblk, tvalid, ntiles = _moba_layout(rt, counts.reshape(-1), nb, max_tiles)
        qg = _sc_scatter_rows(q_rows, b * H * nb, qpos, H * slab_rows)
        qg = qg.reshape(H * slab_rows // 2, LANES)
        parts = _moba_group(qg, k, v, b, tblk, tvalid, ntiles, max_tiles)
        og = _sc_gather_rows(parts, pos)
        o = _moba_merge(q16, k, v, sg, og, x2d, out_w, b, o)
    return o


def _rope_tables(S):
    inv = np.float32(ROPE_THETA) ** (-np.arange(HALF, dtype=np.float32) / np.float32(HALF))
    ang = np.arange(S, dtype=np.float32)[:, None] * inv[None, :].astype(np.float32)
    cos, sin = np.cos(ang).astype(np.float32), np.sin(ang).astype(np.float32)
    return (jnp.asarray(np.concatenate([cos, cos], axis=-1)),
            jnp.asarray(np.concatenate([-sin, sin], axis=-1)))


def kernel(x, a_norm, a_in_w, a_conv_w, a_conv_b, a_r_w, a_r_b, a_i_w, a_i_b, a_lambda, a_out_w,
           kv_norm, kv_w, k_norm, b_norm, b_in_w, q_norm, b_out_w):
    B, S, D = x.shape
    assert S % ROW_TILE == 0 and ROW_TILE % MOBA_BLOCK == 0
    assert D == N_RG_BLOCKS * RG_BW == N_HEADS * HEAD_DIM
    assert (B * N_HEADS * S) % (IDX_CHUNK * SC_WORKERS) == 0
    xs = x.reshape(B * S, D)
    row = lambda v: v.reshape(1, -1)

    for l in range(a_in_w.shape[0]):
        wri = jnp.concatenate([a_r_w[l], a_i_w[l]], axis=-1).astype(BF16)
        xs = _hawk_layer(xs, B, S, row(a_norm[l]), a_in_w[l].astype(BF16), a_conv_w[l], row(a_conv_b[l]), wri,
                         row(a_r_b[l]), row(a_i_b[l]), row(a_lambda[l]), a_out_w[l].astype(BF16))

    cosf, sinf = _rope_tables(S)
    for jl in range(b_in_w.shape[0]):
        q_args = (row(b_norm[jl]), b_in_w[jl].astype(BF16), row(q_norm[jl]))
        if jl == 0:
            k, v, kmean, qp, q16, sg = _kvq_proj(xs, B, S, row(kv_norm), kv_w.astype(BF16), row(k_norm),
                                                *q_args, cosf, sinf)
            kmean = jnp.transpose(kmean, (0, 2, 1, 3))
        else:
            qp, q16, sg = _q_proj(xs, B, S, *q_args, cosf, sinf)
        xs = _moba_layer(qp, q16, k, v, kmean, sg, xs, b_out_w[jl].astype(BF16))
    return xs.reshape(B, S, D)
```

```python
import functools

import jax
import jax.numpy as jnp
import numpy as np
from jax import lax
from jax.experimental import pallas as pl
from jax.experimental.pallas import tpu as pltpu
from jax.experimental.pallas import tpu_sc as plsc

N_HEADS = 8
HEAD_DIM = 128
MOBA_BLOCK = 256
MOBA_TOPK = 3
CONV_WIDTH = 4
N_RG_BLOCKS = 8
RG_BW = 128
RG_C = 8.0
ROPE_THETA = 10000.0
EPS = 1e-6
NEG_INF = -1e30
LOG2_E = 1.4426950408889634
TINY = 1e-30

SUBLANES = 8
LANES = 128
ROW_TILE = 512
SUB_ROWS = 128
VMEM_LIMIT = 56 * 1024 * 1024

GROUP_TILE = MOBA_BLOCK
TILES_PER_STEP = 32
V_BUILD_ROWS = 1024
HEADS_PER_STEP = 8
MERGE_TILES = 2
ROUTE_TILES = 8
ROUTE_BLK_SHIFT = 16
ROUTE_BLK_UNIT = 1 << ROUTE_BLK_SHIFT
HALF = HEAD_DIM // 2
VA_W = 2 * HEAD_DIM
L_LANE = HALF
HI16_MASK = -65536
BF16_HALF_ULP = 0x8000
NEG_INF_BITS = int(np.float32(NEG_INF).view(np.int32))

SC_CORES = 2
SC_SUBCORES = 16
SC_WORKERS = SC_CORES * SC_SUBCORES
IDX_CHUNK = 128
SC_HALVES = MOBA_BLOCK // IDX_CHUNK

F32 = jnp.float32
BF16 = jnp.bfloat16
I32 = jnp.int32


def _cparams(sem):
    return pltpu.CompilerParams(dimension_semantics=sem, vmem_limit_bytes=VMEM_LIMIT)


def _rms_norm(x, g):
    ms = jnp.mean(x * x, axis=-1, keepdims=True)
    return (x * lax.rsqrt(ms + EPS)) * g


def _silu(x):
    hx = 0.5 * x
    return hx * jnp.tanh(hx) + hx


def _sigmoid(x):
    return 0.5 * jnp.tanh(0.5 * x) + 0.5


def _nt_dot(a, b):
    return lax.dot_general(a, b, (((1,), (1,)), ((), ())), preferred_element_type=F32)


def _head_norm_rope(z, g, cosf, sinf):
    outs = []
    for h in range(N_HEADS):
        zh = z[:, h * HEAD_DIM:(h + 1) * HEAD_DIM]
        zh = _rms_norm(zh, g)
        outs.append(zh * cosf + pltpu.roll(zh, HEAD_DIM // 2, 1) * sinf)
    return outs


def _hawk_kernel(x_ref, g_ref, inw_ref, cw_ref, cb_ref, wri_ref, rb_ref, ib_ref, lam_ref, outw_ref,
                 o_ref, gate_ref, xpad_ref, a_ref, b_ref, hc_ref):
    tm, C = gate_ref.shape
    s = pl.program_id(1)

    @pl.when(s == 0)
    def _():
        xpad_ref[pl.ds(0, SUBLANES), :] = jnp.zeros((SUBLANES, xpad_ref.shape[1]), F32)
        hc_ref[...] = jnp.zeros_like(hc_ref)

    h = _rms_norm(x_ref[...], g_ref[...]).astype(BF16)
    gate_ref[...] = jnp.dot(h, inw_ref[:, C:], preferred_element_type=F32)

    xpad_ref[pl.ds(SUBLANES, tm), :] = jnp.dot(h, inw_ref[:, :C], preferred_element_type=F32)
    cw = cw_ref[...]
    xc = xpad_ref[pl.ds(SUBLANES - (CONV_WIDTH - 1), tm), :] * cw[0:1, :]
    for k in range(1, CONV_WIDTH):
        xc = xc + xpad_ref[pl.ds(SUBLANES - (CONV_WIDTH - 1) + k, tm), :] * cw[k:k + 1, :]
    xc = xc + cb_ref[...]
    xpad_ref[pl.ds(0, SUBLANES), :] = xpad_ref[pl.ds(tm, SUBLANES), :]

    lam = lam_ref[...]
    sp = jnp.maximum(-lam, 0.0) + jnp.log1p(jnp.exp(-jnp.abs(lam)))
    log2a_per_r = (-RG_C * LOG2_E) * sp
    xc16 = xc.astype(BF16)
    for g in range(N_RG_BLOCKS):
        lo = g * RG_BW
        z = jnp.dot(xc16[:, lo:lo + RG_BW], wri_ref[g], preferred_element_type=F32)
        r = _sigmoid(z[:, :RG_BW] + rb_ref[:, lo:lo + RG_BW])
        i = _sigmoid(z[:, RG_BW:] + ib_ref[:, lo:lo + RG_BW])
        a = jnp.exp2(r * log2a_per_r[:, lo:lo + RG_BW])
        om = 1.0 - a * a
        mult = om * lax.rsqrt(jnp.maximum(om, TINY))
        a_ref[:, lo:lo + RG_BW] = a
        b_ref[:, lo:lo + RG_BW] = mult * (i * xc[:, lo:lo + RG_BW])

    row = lax.broadcasted_iota(I32, (SUBLANES, a_ref.shape[1]), 0)

    def scan_body(c, hc):
        r0 = pl.multiple_of(c * SUBLANES, SUBLANES)
        a = a_ref[pl.ds(r0, SUBLANES), :]
        b = b_ref[pl.ds(r0, SUBLANES), :]
        for k in (1, 2, 4):
            keep = row >= k
            b = jnp.where(keep, a * pltpu.roll(b, k, 0) + b, b)
            a = jnp.where(keep, a * pltpu.roll(a, k, 0), a)
        h = a * hc + b
        b_ref[pl.ds(r0, SUBLANES), :] = h
        return h[SUBLANES - 1:SUBLANES, :]

    hc = lax.fori_loop(0, tm // SUBLANES, scan_body, hc_ref[0:1, :])
    hc_ref[0:1, :] = hc

    y = (b_ref[...] * _silu(gate_ref[...])).astype(BF16)
    o_ref[...] = x_ref[...] + jnp.dot(y, outw_ref[...], preferred_element_type=F32)


def _hawk_layer(x2d, b_in, S, g, in_w, cw, cb, wri, rb, ib, lam, out_w):
    D = x2d.shape[1]
    C = in_w.shape[1] // 2
    tm = ROW_TILE
    ns = S // tm
    vec = lambda n: pl.BlockSpec((1, n), lambda b, s: (0, 0))
    return pl.pallas_call(
        _hawk_kernel,
        grid=(1, ns),
        in_specs=[pl.BlockSpec((tm, D), lambda b, s: (b_in * ns + s, 0)),
                  vec(D),
                  pl.BlockSpec((D, 2 * C), lambda b, s: (0, 0)),
                  pl.BlockSpec((CONV_WIDTH, C), lambda b, s: (0, 0)),
                  vec(C),
                  pl.BlockSpec((N_RG_BLOCKS, RG_BW, 2 * RG_BW), lambda b, s: (0, 0, 0)),
                  vec(C), vec(C), vec(C),
                  pl.BlockSpec((C, D), lambda b, s: (0, 0))],
        out_specs=pl.BlockSpec((tm, D), lambda b, s: (s, 0)),
        out_shape=jax.ShapeDtypeStruct((S, D), F32),
        scratch_shapes=[pltpu.VMEM((tm, C), F32),
                        pltpu.VMEM((tm + SUBLANES, C), F32),
                        pltpu.VMEM((tm, C), F32),
                        pltpu.VMEM((tm, C), F32),
                        pltpu.VMEM((SUBLANES, C), F32)],
        compiler_params=_cparams(("arbitrary", "arbitrary")),
        name="hawk_layer",
    )(x2d, g, in_w, cw, cb, wri, rb, ib, lam, out_w)


def _emit_q(u, qg, cosf, sinf, t, qp_ref, q16_ref, sg_ref):
    assert SUB_ROWS == IDX_CHUNK
    d_attn = N_HEADS * HEAD_DIM
    rows = pl.ds(t * SUB_ROWS, SUB_ROWS)
    prow = pl.ds((t // 2) * SUB_ROWS, SUB_ROWS)
    qs = _head_norm_rope(u[:, :d_attn], qg, cosf, sinf)
    for hd in range(N_HEADS):
        q16 = qs[hd].astype(BF16)
        q16_ref[0, hd, rows, :] = q16
        r = _bits(q16.astype(F32))
        swapped = pltpu.roll(r, HALF, 1)
        if t % 2 == 0:
            w = (swapped & HI16_MASK) | lax.shift_right_logical(r, 16)
            qp_ref[0, hd, prow, pl.ds(0, HALF)] = w[:, :HALF]
        else:
            w = (r & HI16_MASK) | lax.shift_right_logical(swapped, 16)
            qp_ref[0, hd, prow, pl.ds(HALF, HALF)] = w[:, HALF:]
    sg_ref[rows, :] = _silu(u[:, d_attn:])


def _kvq_proj_kernel(x_ref, gkv_ref, wkv_ref, kg_ref, gq_ref, wq_ref, qg_ref, cos_ref, sin_ref,
                     k_ref, v_ref, km_ref, q_ref, q16_ref, sg_ref):
    tm = x_ref.shape[0]
    d_attn = N_HEADS * HEAD_DIM
    ksum = {}
    for t in range(tm // SUB_ROWS):
        rows = pl.ds(t * SUB_ROWS, SUB_ROWS)
        x = x_ref[rows, :]
        xn = x * lax.rsqrt(jnp.mean(x * x, axis=-1, keepdims=True) + EPS)
        kv = jnp.dot((xn * gkv_ref[...]).astype(BF16), wkv_ref[...], preferred_element_type=F32)
        u = jnp.dot((xn * gq_ref[...]).astype(BF16), wq_ref[...], preferred_element_type=F32)
        cosf, sinf = cos_ref[rows, :], sin_ref[rows, :]
        ks = _head_norm_rope(kv[:, :d_attn], kg_ref[...], cosf, sinf)
        for hd in range(N_HEADS):
            k_ref[0, hd, rows, :] = ks[hd].astype(BF16)
            v_ref[0, hd, rows, :] = kv[:, d_attn + hd * HEAD_DIM:d_attn + (hd + 1) * HEAD_DIM].astype(BF16)
            key = (hd, (t * SUB_ROWS) // MOBA_BLOCK)
            part = jnp.sum(ks[hd], axis=0, keepdims=True)
            ksum[key] = part if key not in ksum else ksum[key] + part
        _emit_q(u, qg_ref[...], cosf, sinf, t, q_ref, q16_ref, sg_ref)
    for (hd, r), total in ksum.items():
        km_ref[0, r, pl.ds(hd, 1), :] = total * (1.0 / MOBA_BLOCK)


def _kvq_proj(x2d, B, S, gkv, wkv, kg, gq, wq, qg, cosf, sinf):
    T, D = x2d.shape
    tm = ROW_TILE
    ns = S // tm
    nb = S // MOBA_BLOCK
    rpt = tm // MOBA_BLOCK
    d_attn = N_HEADS * HEAD_DIM
    vec = lambda n: pl.BlockSpec((1, n), lambda b, s: (0, 0))
    wspec = pl.BlockSpec((D, 2 * d_attn), lambda b, s: (0, 0))
    rope = pl.BlockSpec((tm, HEAD_DIM), lambda b, s: (s, 0))
    heads = pl.BlockSpec((1, N_HEADS, tm, HEAD_DIM), lambda b, s: (b, 0, s, 0))
    hshape = lambda dt: jax.ShapeDtypeStruct((B, N_HEADS, S, HEAD_DIM), dt)
    return pl.pallas_call(
        _kvq_proj_kernel,
        grid=(B, ns),
        in_specs=[pl.BlockSpec((tm, D), lambda b, s: (b * ns + s, 0)),
                  vec(D), wspec, vec(HEAD_DIM), vec(D), wspec, vec(HEAD_DIM), rope, rope],
        out_specs=[heads, heads,
                   pl.BlockSpec((1, rpt, N_HEADS, HEAD_DIM), lambda b, s: (b, s, 0, 0)),
                   pl.BlockSpec((1, N_HEADS, tm // 2, LANES), lambda b, s: (b, 0, s, 0)), heads,
                   pl.BlockSpec((tm, d_attn), lambda b, s: (b * ns + s, 0))],
        out_shape=[hshape(BF16), hshape(BF16),
                   jax.ShapeDtypeStruct((B, nb, N_HEADS, HEAD_DIM), F32),
                   jax.ShapeDtypeStruct((B, N_HEADS, S // 2, LANES), I32), hshape(BF16),
                   jax.ShapeDtypeStruct((T, d_attn), F32)],
        compiler_params=_cparams(("arbitrary", "arbitrary")),
        name="kvq_proj",
    )(x2d, gkv, wkv, kg, gq, wq, qg, cosf, sinf)


def _q_proj_kernel(x_ref, g_ref, w_ref, qg_ref, cos_ref, sin_ref, q_ref, q16_ref, sg_ref):
    for t in range(x_ref.shape[0] // SUB_ROWS):
        rows = pl.ds(t * SUB_ROWS, SUB_ROWS)
        h = _rms_norm(x_ref[rows, :], g_ref[...]).astype(BF16)
        u = jnp.dot(h, w_ref[...], preferred_element_type=F32)
        _emit_q(u, qg_ref[...], cos_ref[rows, :], sin_ref[rows, :], t, q_ref, q16_ref, sg_ref)


def _q_proj(x2d, B, S, g, w, qg, cosf, sinf):
    T, D = x2d.shape
    tm = ROW_TILE
    ns = S // tm
    d_attn = N_HEADS * HEAD_DIM
    heads = pl.BlockSpec((1, N_HEADS, tm, HEAD_DIM), lambda b, s: (b, 0, s, 0))
    return pl.pallas_call(
        _q_proj_kernel,
        grid=(B, ns),
        in_specs=[pl.BlockSpec((tm, D), lambda b, s: (b * ns + s, 0)),
                  pl.BlockSpec((1, D), lambda b, s: (0, 0)),
                  pl.BlockSpec((D, 2 * d_attn), lambda b, s: (0, 0)),
                  pl.BlockSpec((1, HEAD_DIM), lambda b, s: (0, 0)),
                  pl.BlockSpec((tm, HEAD_DIM), lambda b, s: (s, 0)),
                  pl.BlockSpec((tm, HEAD_DIM), lambda b, s: (s, 0))],
        out_specs=[pl.BlockSpec((1, N_HEADS, tm // 2, LANES), lambda b, s: (b, 0, s, 0)), heads,
                   pl.BlockSpec((tm, d_attn), lambda b, s: (b * ns + s, 0))],
        out_shape=[jax.ShapeDtypeStruct((B, N_HEADS, S // 2, LANES), I32),
                   jax.ShapeDtypeStruct((B, N_HEADS, S, HEAD_DIM), BF16),
                   jax.ShapeDtypeStruct((T, d_attn), F32)],
        compiler_params=_cparams(("arbitrary", "arbitrary")),
        name="q_proj",
    )(x2d, g, w, qg, cosf, sinf)


def _moba_route_kernel(q_ref, km_ref, rt_ref, cnt_ref, run_ref):
    step = pl.program_id(2)
    nb = km_ref.shape[2]

    @pl.when(step == 0)
    def _():
        run_ref[...] = jnp.zeros_like(run_ref)

    km = km_ref[0, 0].astype(BF16)
    blk = lax.broadcasted_iota(I32, (nb, MOBA_BLOCK), 0)
    qa = lax.broadcasted_iota(I32, (MOBA_BLOCK, MOBA_BLOCK), 0)
    qb = lax.broadcasted_iota(I32, (MOBA_BLOCK, MOBA_BLOCK), 1)
    earlier = jnp.where(qa < qb, 1.0, 0.0).astype(BF16)
    run = run_ref[...]
    for u in range(ROUTE_TILES):
        j = step * ROUTE_TILES + u
        g = _nt_dot(km, q_ref[0, 0, pl.ds(u * MOBA_BLOCK, MOBA_BLOCK), :].astype(BF16))
        g = jnp.where(blk < j, g, -jnp.inf)
        hits, ids = [], []
        for _ in range(MOBA_TOPK):
            mx = jnp.max(g, axis=0, keepdims=True)
            idx = jnp.min(jnp.where(g == mx, blk, nb), axis=0, keepdims=True)
            hit = blk == idx
            ok = mx > -jnp.inf
            hits.append(jnp.where(hit & ok, 1.0, 0.0))
            ids.append(jnp.where(ok, idx, -1))
            g = jnp.where(hit, -jnp.inf, g)
        sel = hits[0] + hits[1] + hits[2]

        rank = jnp.dot(sel.astype(BF16), earlier, preferred_element_type=F32)
        dest = run + rank
        run = run + jnp.sum(sel, axis=1, keepdims=True)
        for r in range(MOBA_TOPK):
            loc = jnp.sum(hits[r] * dest, axis=0, keepdims=True).astype(I32)
            rt_ref[u, pl.ds(r, 1), :] = jnp.where(ids[r] >= 0, ids[r] * ROUTE_BLK_UNIT + loc, -1)
        rt_ref[u, pl.ds(MOBA_TOPK, SUBLANES - MOBA_TOPK), :] = jnp.full(
            (SUBLANES - MOBA_TOPK, MOBA_BLOCK), -1, I32)
    run_ref[...] = run
    cnt_ref[0] = run.astype(I32)


def _moba_route(q, kmean, b):
    _, H, S, Dh = q.shape
    nb = S // MOBA_BLOCK
    rows = ROUTE_TILES * MOBA_BLOCK
    nsteps = nb // ROUTE_TILES
    return pl.pallas_call(
        _moba_route_kernel,
        grid=(1, H, nsteps),
        in_specs=[pl.BlockSpec((1, 1, rows, Dh), lambda _, h, j: (b, h, j, 0)),
                  pl.BlockSpec((1, 1, nb, Dh), lambda _, h, j: (b, h, 0, 0))],
        out_specs=[pl.BlockSpec((ROUTE_TILES, SUBLANES, MOBA_BLOCK), lambda _, h, j: (h * nsteps + j, 0, 0)),
                   pl.BlockSpec((1, nb, 1), lambda _, h, j: (h, 0, 0))],
        out_shape=[jax.ShapeDtypeStruct((H * nb, SUBLANES, MOBA_BLOCK), I32),
                   jax.ShapeDtypeStruct((H, nb, 1), I32)],
        scratch_shapes=[pltpu.VMEM((nb, 1), F32)],
        compiler_params=_cparams(("arbitrary", "arbitrary", "arbitrary")),
        name="moba_route",
    )(q, kmean)


def _moba_layout_kernel(cnt_ref, rt_ref, pos_ref, qpos_ref, tblk_ref, tvalid_ref, nt_ref, base_ref, *, max_tiles):
    g = pl.program_id(0)
    nb = rt_ref.shape[0]
    slab_rows = max_tiles * GROUP_TILE
    null_row0 = (max_tiles - 1) * GROUP_TILE

    def per_block(n, first_tile):
        count = cnt_ref[g * nb + n]
        tiles = (count + (GROUP_TILE - 1)) // GROUP_TILE
        base_ref[n] = first_tile * GROUP_TILE

        def per_tile(i, carry):
            tblk_ref[g * max_tiles + first_tile + i] = n
            tvalid_ref[g * max_tiles + first_tile + i] = jnp.minimum(count - i * GROUP_TILE, GROUP_TILE)
            return carry

        lax.fori_loop(0, tiles, per_tile, 0)
        return first_tile + tiles

    used = lax.fori_loop(0, nb, per_block, 0)
    nt_ref[g] = used

    def unused_tile(t, carry):
        tblk_ref[g * max_tiles + t] = nb - 1
        tvalid_ref[g * max_tiles + t] = 0
        return carry

    lax.fori_loop(used, max_tiles, unused_tile, 0)

    code = rt_ref[...].reshape(nb * SUBLANES, MOBA_BLOCK)
    blk = lax.shift_right_arithmetic(code, ROUTE_BLK_SHIFT)
    base = jnp.zeros_like(code)
    for n in range(nb):
        base = jnp.where(blk == n, base_ref[n], base)
    lane = lax.broadcasted_iota(I32, code.shape, 1)
    row = jnp.where(code >= 0, base + (code & (ROUTE_BLK_UNIT - 1)), null_row0 + lane) + g * slab_rows
    pos_ref[...] = row.reshape(pos_ref.shape)
    p = row & (GROUP_TILE - 1)
    assert GROUP_TILE == 2 * IDX_CHUNK
    slot = ((p & (IDX_CHUNK - 1)) << 1) | lax.shift_right_logical(p, IDX_CHUNK.bit_length() - 1)
    qpos_ref[...] = ((row - p) | slot).reshape(qpos_ref.shape)


def _moba_layout(rt, counts, nb, max_tiles):
    G = rt.shape[0] // nb
    smem = pl.BlockSpec(memory_space=pltpu.SMEM)
    return pl.pallas_call(
        functools.partial(_moba_layout_kernel, max_tiles=max_tiles),
        grid_spec=pltpu.PrefetchScalarGridSpec(
            num_scalar_prefetch=1,
            grid=(G,),
            in_specs=[pl.BlockSpec((nb, SUBLANES, MOBA_BLOCK), lambda g, cnt: (g, 0, 0))],
            out_specs=[pl.BlockSpec((nb, SUBLANES, MOBA_BLOCK), lambda g, cnt: (g, 0, 0)),
                       pl.BlockSpec((nb, SUBLANES, MOBA_BLOCK), lambda g, cnt: (g, 0, 0)), smem, smem, smem],
            scratch_shapes=[pltpu.SMEM((nb,), I32)],
        ),
        out_shape=[jax.ShapeDtypeStruct((G * nb, SUBLANES, MOBA_BLOCK), I32),
                   jax.ShapeDtypeStruct((G * nb, SUBLANES, MOBA_BLOCK), I32),
                   jax.ShapeDtypeStruct((G * max_tiles,), I32),
                   jax.ShapeDtypeStruct((G * max_tiles,), I32),
                   jax.ShapeDtypeStruct((G,), I32)],
        compiler_params=_cparams(("arbitrary",)),
        name="moba_layout",
    )(counts, rt)


def _sc_mesh():
    return plsc.VectorSubcoreMesh(core_axis_name="c", subcore_axis_name="s")


def _sc_load_indices(pos_hbm, idx_v, t0, n_tiles):
    for c in range(SC_HALVES):
        pltpu.sync_copy(pos_hbm.at[pl.ds(t0, n_tiles), pl.ds(0, MOBA_TOPK), pl.ds(c * IDX_CHUNK, IDX_CHUNK)],
                        idx_v.at[c])


def _sc_scatter_rows(src, src_tile0, pos, n_out_rows):
    D = src.shape[-1]
    n_tiles = pos.shape[0]
    per_w = n_tiles // SC_WORKERS
    assert per_w * SC_WORKERS == n_tiles and src.shape[1:3] == (IDX_CHUNK, SC_HALVES)

    @functools.partial(
        pl.kernel, mesh=_sc_mesh(),
        out_type=jax.ShapeDtypeStruct((n_out_rows, D), src.dtype),
        scratch_types=[pltpu.VMEM((SC_HALVES, per_w, MOBA_TOPK, IDX_CHUNK), I32),
                       pltpu.VMEM((SC_HALVES, IDX_CHUNK, D), src.dtype),
                       pltpu.SemaphoreType.DMA((SC_HALVES,)),
                       pltpu.SemaphoreType.DMA((SC_HALVES,))],
        compiler_params=pltpu.CompilerParams(use_tc_tiling_on_sc=False),
        name="sc_scatter_rows",
    )
    def k(src_hbm, pos_hbm, out_hbm, idx_v, rows_v, load_sem, scat_sem):
        t0 = (lax.axis_index("s") * SC_CORES + lax.axis_index("c")) * per_w
        _sc_load_indices(pos_hbm, idx_v, t0, per_w)

        def load(i, c):
            return pltpu.make_async_copy(src_hbm.at[src_tile0 + t0 + i, :, c, :], rows_v.at[c], load_sem.at[c])

        def scatters(i, c):
            return [pltpu.make_async_copy(rows_v.at[c], out_hbm.at[idx_v.at[c, i, r]], scat_sem.at[c])
                    for r in range(MOBA_TOPK)]

        load(0, 0).start()

        @pl.loop(0, per_w)
        def _(i):
            for c in range(SC_HALVES):
                pi, pc = (i, 0) if c else (i - 1, 1)
                ni, nc = (i + 1, 0) if c else (i, 1)
                load(i, c).wait()
                for cp in scatters(i, c):
                    cp.start()

                @pl.when(pi >= 0)
                def _():
                    for cp in scatters(pi, pc):
                        cp.wait()

                @pl.when(ni < per_w)
                def _():
                    load(ni, nc).start()

        for cp in scatters(per_w - 1, SC_HALVES - 1):
            cp.wait()

    return k(src, pos)


def _sc_gather_rows(table, pos):
    R, D = table.shape
    n_tiles = pos.shape[0]
    N = n_tiles * MOBA_BLOCK
    per_w = n_tiles // SC_WORKERS
    assert per_w * SC_WORKERS == n_tiles

    items = SC_HALVES * MOBA_TOPK
    n_buf = 2
    assert items % n_buf == 0

    @functools.partial(
        pl.kernel, mesh=_sc_mesh(),
        out_type=jax.ShapeDtypeStruct((MOBA_TOPK, N, D), table.dtype),
        scratch_types=[pltpu.VMEM((SC_HALVES, per_w, MOBA_TOPK, IDX_CHUNK), I32),
                       pltpu.VMEM((n_buf, IDX_CHUNK, D), table.dtype),
                       pltpu.SemaphoreType.DMA((n_buf,)),
                       pltpu.SemaphoreType.DMA((n_buf,))],
        name="sc_gather_rows",
    )
    def k(table_hbm, pos_hbm, out_hbm, idx_v, rows_v, gat_sem, put_sem):
        t0 = (lax.axis_index("s") * SC_CORES + lax.axis_index("c")) * per_w
        _sc_load_indices(pos_hbm, idx_v, t0, per_w)

        def gather(i, j):
            c, r = divmod(j, MOBA_TOPK)
            return pltpu.make_async_copy(table_hbm.at[idx_v.at[c, i, r]], rows_v.at[j % n_buf], gat_sem.at[j % n_buf])

        def put(i, j):
            c, r = divmod(j, MOBA_TOPK)
            row0 = pl.multiple_of((t0 + i) * MOBA_BLOCK + c * IDX_CHUNK, IDX_CHUNK)
            return pltpu.make_async_copy(rows_v.at[j % n_buf], out_hbm.at[r, pl.ds(row0, IDX_CHUNK)],
                                         put_sem.at[j % n_buf])

        gather(0, 0).start()

        @pl.loop(0, per_w)
        def _(i):
            for j in range(items):
                pi, pj = (i, j - 1) if j else (i - 1, items - 1)
                ni, nj = (i, j + 1) if j + 1 < items else (i + 1, 0)
                gather(i, j).wait()
                put(i, j).start()

                @pl.when(pi >= 0)
                def _():
                    put(pi, pj).wait()

                @pl.when(ni < per_w)
                def _():
                    gather(ni, nj).start()

        put(per_w - 1, items - 1).wait()

    return k(table, pos)


def _bits(x):
    return lax.bitcast_convert_type(x, I32)


def _pack_partial(part, m):
    lane = lax.broadcasted_iota(I32, (part.shape[0], HEAD_DIM), 1)
    lo = _bits(part[:, :HEAD_DIM]) + BF16_HALF_ULP
    hi = _bits(jnp.where(lane == L_LANE, m, part[:, HEAD_DIM:])) + BF16_HALF_ULP
    return (hi & HI16_MASK) | lax.shift_right_logical(lo, 16)


def _null_partial(rows):
    lane = lax.broadcasted_iota(I32, (rows, HEAD_DIM), 1)
    return jnp.where(lane == L_LANE, NEG_INF_BITS & HI16_MASK, 0)


def _unpack_partial(word):
    lo = lax.bitcast_convert_type(lax.shift_left(word, 16), F32)
    hi = lax.bitcast_convert_type(word & HI16_MASK, F32)
    return lo, hi


def _augment_v(v):
    vf = v.astype(F32)
    lane = lax.broadcasted_iota(I32, vf.shape, 1)
    lo = jnp.where(lane < HALF, vf, jnp.where(lane == L_LANE, 1.0, 0.0))
    hi = jnp.where(lane < HALF, pltpu.roll(vf, HALF, 1), 0.0)
    return jnp.concatenate([lo, hi], axis=1).astype(BF16)


def _moba_group_kernel(tblk_ref, tvalid_ref, nt_ref, qg_ref, k_ref, v_ref, o_ref, va_ref, k2_ref, *, max_tiles):
    g, step = pl.program_id(0), pl.program_id(1)
    t0 = g * max_tiles + step * TILES_PER_STEP
    scale = HEAD_DIM ** -0.5

    @pl.when(step == 0)
    def _():
        def build(c, carry):
            rows = pl.ds(pl.multiple_of(c * V_BUILD_ROWS, V_BUILD_ROWS), V_BUILD_ROWS)
            va_ref[rows, :] = _augment_v(v_ref[0, 0, rows, :])
            kf = k_ref[0, 0, rows, :].astype(F32)
            lane = lax.broadcasted_iota(I32, kf.shape, 1)
            swapped = pltpu.roll(kf, HALF, 1)
            k2_ref[rows, :] = jnp.concatenate([jnp.where(lane < HALF, kf, swapped),
                                               jnp.where(lane < HALF, swapped, kf)], axis=1).astype(BF16)
            return carry

        lax.fori_loop(0, va_ref.shape[0] // V_BUILD_ROWS, build, 0)

    @pl.when(tvalid_ref[t0] > 0)
    def _():
        half_rows = GROUP_TILE // 2
        lane = lax.broadcasted_iota(I32, (half_rows, HEAD_DIM), 1)
        for u in range(TILES_PER_STEP):
            word = qg_ref[pl.ds(u * half_rows, half_rows), :]
            first = lax.bitcast_convert_type(lax.shift_left(word, 16), F32)
            second = lax.bitcast_convert_type(word & HI16_MASK, F32)
            n0 = pl.multiple_of(tblk_ref[t0 + u] * MOBA_BLOCK, MOBA_BLOCK)
            left, right = lane < HALF, lane >= HALF
            q = jnp.concatenate(
                [jnp.concatenate([jnp.where(left, first, 0.0), jnp.where(left, second, 0.0)], axis=1),
                 jnp.concatenate([jnp.where(right, first, 0.0), jnp.where(right, second, 0.0)], axis=1)],
                axis=0).astype(BF16)
            s = _nt_dot(q, k2_ref[pl.ds(n0, MOBA_BLOCK), :])
            m = (jnp.max(s, axis=1, keepdims=True) * (scale * LOG2_E)).astype(BF16).astype(F32)
            p = jnp.exp2(s * (scale * LOG2_E) - m).astype(BF16)
            part = jnp.dot(p, va_ref[pl.ds(n0, MOBA_BLOCK), :], preferred_element_type=F32)
            o_ref[pl.ds(u * GROUP_TILE, GROUP_TILE), :] = _pack_partial(part, m)

    @pl.when(step == pl.num_programs(1) - 1)
    def _():
        o_ref[pl.ds((TILES_PER_STEP - 1) * GROUP_TILE, GROUP_TILE), :] = _null_partial(GROUP_TILE)


def _moba_group(qg, k, v, b, tblk, tvalid, ntiles, max_tiles):
    _, H, S, Dh = k.shape
    G = H
    n_steps = max_tiles // TILES_PER_STEP
    step_rows = TILES_PER_STEP * GROUP_TILE

    def tile_idx(g, s, tblk, tvalid, nt):
        return (g * n_steps + jnp.where(s * TILES_PER_STEP < nt[g], s, n_steps - 1), 0)

    return pl.pallas_call(
        functools.partial(_moba_group_kernel, max_tiles=max_tiles),
        grid_spec=pltpu.PrefetchScalarGridSpec(
            num_scalar_prefetch=3,
            grid=(G, n_steps),
            in_specs=[pl.BlockSpec((step_rows // 2, LANES), tile_idx),
                      pl.BlockSpec((1, 1, S, Dh), lambda g, s, *_: (b, g, 0, 0)),
                      pl.BlockSpec((1, 1, S, Dh), lambda g, s, *_: (b, g, 0, 0))],
            out_specs=pl.BlockSpec((step_rows, LANES), tile_idx),
            scratch_shapes=[pltpu.VMEM((S, VA_W), BF16), pltpu.VMEM((S, 2 * Dh), BF16)],
        ),
        out_shape=jax.ShapeDtypeStruct((G * max_tiles * GROUP_TILE, LANES), I32),
        compiler_params=_cparams(("arbitrary", "arbitrary")),
        name="moba_group",
    )(tblk, tvalid, ntiles, qg, k, v)


def _moba_merge_kernel(*refs, n_skip):
    q_ref, k_ref, v_ref, sg_ref, x_ref, outw_ref = refs[n_skip:n_skip + 6]
    og_refs, out_ref, y_ref = refs[n_skip + 6:-2], refs[-2], refs[-1]
    scale = HEAD_DIM ** -0.5
    qi = lax.broadcasted_iota(I32, (MOBA_BLOCK, MOBA_BLOCK), 0)
    ki = lax.broadcasted_iota(I32, (MOBA_BLOCK, MOBA_BLOCK), 1)
    lane = lax.broadcasted_iota(I32, (MOBA_BLOCK, HEAD_DIM), 1)
    spread = jnp.where(lax.broadcasted_iota(I32, (HEAD_DIM, HEAD_DIM), 0) == L_LANE, 1.0, 0.0).astype(BF16)
    for u, t in [(u, t) for t in range(MERGE_TILES) for u in range(len(og_refs))]:
        og_ref = og_refs[u]
        rows = pl.ds(t * MOBA_BLOCK, MOBA_BLOCK)
        q = q_ref[0, u, rows, :].astype(BF16)
        s = _nt_dot(q, k_ref[0, u, rows, :])
        s = jnp.where(ki <= qi, s, NEG_INF)
        m_raw = jnp.max(s, axis=1, keepdims=True)
        p = jnp.exp2((s - m_raw) * (scale * LOG2_E)).astype(BF16)
        own = jnp.dot(p, _augment_v(v_ref[0, u, rows, :]), preferred_element_type=F32)
        parts = [_unpack_partial(og_ref[r, rows, :]) for r in range(MOBA_TOPK)]
        ms = [jnp.dot(jnp.where(lane == L_LANE, hi, 0.0).astype(BF16), spread, preferred_element_type=F32)
              for _, hi in parts]
        m_own = jnp.broadcast_to(m_raw * (scale * LOG2_E), lane.shape)
        m_all = m_own
        for m in ms:
            m_all = jnp.maximum(m_all, m)
        w_own = jnp.exp2(m_own - m_all)
        tot_lo = w_own * own[:, :HEAD_DIM]
        tot_hi = w_own * own[:, HEAD_DIM:]
        for (lo, hi), m in zip(parts, ms):
            w = jnp.exp2(m - m_all)
            tot_lo = tot_lo + w * lo
            tot_hi = tot_hi + w * hi
        acc = jnp.where(lane < HALF, tot_lo, pltpu.roll(tot_hi, HALF, 1))
        cols = pl.ds(u * HEAD_DIM, HEAD_DIM)
        y_ref[rows, cols] = ((acc / tot_lo[:, L_LANE:L_LANE + 1]) * sg_ref[rows, cols]).astype(BF16)

    out_ref[...] = x_ref[...] + jnp.dot(y_ref[...], outw_ref[...], preferred_element_type=F32)


def _moba_merge(q, k, v, sg, og, x2d, out_w, out_batch, n_out_batch, o_prev):
    _, H, S, Dh = q.shape
    D = x2d.shape[1]
    b = 0
    hps = HEADS_PER_STEP
    assert hps == H
    rows = MERGE_TILES * MOBA_BLOCK
    ns = S // rows
    og_spec = lambda u: pl.BlockSpec((MOBA_TOPK, rows, LANES),
                                     lambda _, hp, j: (0, (hp * hps + u) * ns + j, 0))
    prev_specs = [] if o_prev is None else [pl.BlockSpec(memory_space=pl.ANY)]
    prev_args = [] if o_prev is None else [o_prev]
    return pl.pallas_call(
        functools.partial(_moba_merge_kernel, n_skip=len(prev_args)),
        grid=(1, H // hps, ns),
        in_specs=prev_specs
                 + [pl.BlockSpec((1, hps, rows, Dh), lambda _, hp, j: (b, hp, j, 0)),
                    pl.BlockSpec((1, hps, rows, Dh), lambda _, hp, j: (b, hp, j, 0)),
                    pl.BlockSpec((1, hps, rows, Dh), lambda _, hp, j: (b, hp, j, 0)),
                    pl.BlockSpec((rows, hps * Dh), lambda _, hp, j: (b * ns + j, hp)),
                    pl.BlockSpec((rows, D), lambda _, hp, j: (b * ns + j, 0)),
                    pl.BlockSpec((H * Dh, D), lambda _, hp, j: (0, 0))]
                 + [og_spec(u) for u in range(hps)],
        out_specs=pl.BlockSpec((rows, D), lambda _, hp, j: (out_batch * ns + j, 0)),
        out_shape=jax.ShapeDtypeStruct((n_out_batch * S, D), F32),
        scratch_shapes=[pltpu.VMEM((rows, H * Dh), BF16)],
        input_output_aliases={} if o_prev is None else {0: 0},
        compiler_params=_cparams(("arbitrary", "arbitrary", "arbitrary")),
        name="moba_merge",
    )(*prev_args, q, k, v, sg, x2d, out_w, *([og] * hps))


def _moba_layer(qp, q16, k, v, kmean, sg, x2d, out_w, out_batch, n_out_batch, o_prev):
    _, H, S, Dh = q16.shape
    nb = S // MOBA_BLOCK
    max_tiles = (MOBA_TOPK * S) // GROUP_TILE + nb + 1
    max_tiles = -(-max_tiles // TILES_PER_STEP) * TILES_PER_STEP
    slab_rows = max_tiles * GROUP_TILE

    q_rows = qp.reshape(H * nb, IDX_CHUNK, SC_HALVES, HALF)
    rt, counts = _moba_route(q16, kmean, 0)
    pos, qpos, tblk, tvalid, ntiles = _moba_layout(rt, counts.reshape(-1), nb, max_tiles)
    qg = _sc_scatter_rows(q_rows, 0, qpos, H * slab_rows)
    qg = qg.reshape(H * slab_rows // 2, LANES)
    parts = _moba_group(qg, k, v, 0, tblk, tvalid, ntiles, max_tiles)
    og = _sc_gather_rows(parts, pos)
    return _moba_merge(q16, k, v, sg, og, x2d, out_w, out_batch, n_out_batch, o_prev)


def _rope_tables(S):
    inv = np.float32(ROPE_THETA) ** (-np.arange(HALF, dtype=np.float32) / np.float32(HALF))
    ang = np.arange(S, dtype=np.float32)[:, None] * inv[None, :].astype(np.float32)
    cos, sin = np.cos(ang).astype(np.float32), np.sin(ang).astype(np.float32)
    return (jnp.asarray(np.concatenate([cos, cos], axis=-1)),
            jnp.asarray(np.concatenate([-sin, sin], axis=-1)))


def kernel(x, a_norm, a_in_w, a_conv_w, a_conv_b, a_r_w, a_r_b, a_i_w, a_i_b, a_lambda, a_out_w,
           kv_norm, kv_w, k_norm, b_norm, b_in_w, q_norm, b_out_w):
    B, S, D = x.shape
    assert S % ROW_TILE == 0 and ROW_TILE % MOBA_BLOCK == 0
    assert D == N_RG_BLOCKS * RG_BW == N_HEADS * HEAD_DIM
    assert (B * N_HEADS * S) % (IDX_CHUNK * SC_WORKERS) == 0
    n_a, n_b = a_in_w.shape[0], b_in_w.shape[0]
    assert n_a >= 1 and n_b >= 1
    x2d = x.reshape(B * S, D)
    row = lambda v: v.reshape(1, -1)
    cosf, sinf = _rope_tables(S)

    out = None
    for b in range(B):
        xs, b_in = x2d, b
        for l in range(n_a):
            wri = jnp.concatenate([a_r_w[l], a_i_w[l]], axis=-1).astype(BF16)
            xs = _hawk_layer(xs, b_in, S, row(a_norm[l]), a_in_w[l].astype(BF16), a_conv_w[l], row(a_conv_b[l]),
                             wri, row(a_r_b[l]), row(a_i_b[l]), row(a_lambda[l]), a_out_w[l].astype(BF16))
            b_in = 0
        for jl in range(n_b):
            q_args = (row(b_norm[jl]), b_in_w[jl].astype(BF16), row(q_norm[jl]))
            if jl == 0:
                k, v, kmean, qp, q16, sg = _kvq_proj(xs, 1, S, row(kv_norm), kv_w.astype(BF16), row(k_norm),
                                                    *q_args, cosf, sinf)
                kmean = jnp.transpose(kmean, (0, 2, 1, 3))
            else:
                qp, q16, sg = _q_proj(xs, 1, S, *q_args, cosf, sinf)
            w_out = b_out_w[jl].astype(BF16)
            if jl == n_b - 1:
                out = xs = _moba_layer(qp, q16, k, v, kmean, sg, xs, w_out, b, B, out)
            else:
                xs = _moba_layer(qp, q16, k, v, kmean, sg, xs, w_out, 0, 1, None)
    return out.reshape(B, S, D)
```

```python
import functools

import jax
import jax.numpy as jnp
import numpy as np
from jax import lax
from jax.experimental import pallas as pl
from jax.experimental.pallas import tpu as pltpu
from jax.experimental.pallas import tpu_sc as plsc

N_HEADS = 8
HEAD_DIM = 128
MOBA_BLOCK = 256
MOBA_TOPK = 3
CONV_WIDTH = 4
N_RG_BLOCKS = 8
RG_BW = 128
RG_C = 8.0
ROPE_THETA = 10000.0
EPS = 1e-6
NEG_INF = -1e30
LOG2_E = 1.4426950408889634
TINY = 1e-30

SUBLANES = 8
LANES = 128
ROW_TILE = 512
SUB_ROWS = 128
VMEM_LIMIT = 56 * 1024 * 1024

GROUP_TILE = MOBA_BLOCK
TILES_PER_STEP = 32
V_BUILD_ROWS = 1024
HEADS_PER_STEP = 8
MERGE_TILES = 2
ROUTE_TILES = 8
ROUTE_BLK_SHIFT = 16
ROUTE_BLK_UNIT = 1 << ROUTE_BLK_SHIFT
HALF = HEAD_DIM // 2
VA_W = 2 * HEAD_DIM
L_LANE = HALF
HI16_MASK = -65536
BF16_HALF_ULP = 0x8000
NEG_INF_BITS = int(np.float32(NEG_INF).view(np.int32))

SC_CORES = 2
SC_SUBCORES = 16
SC_WORKERS = SC_CORES * SC_SUBCORES
IDX_CHUNK = 128
SC_HALVES = MOBA_BLOCK // IDX_CHUNK

F32 = jnp.float32
BF16 = jnp.bfloat16
I32 = jnp.int32


def _cparams(sem):
    return pltpu.CompilerParams(dimension_semantics=sem, vmem_limit_bytes=VMEM_LIMIT)


def _rms_norm(x, g):
    ms = jnp.mean(x * x, axis=-1, keepdims=True)
    return (x * lax.rsqrt(ms + EPS)) * g


def _silu(x):
    hx = 0.5 * x
    return hx * jnp.tanh(hx) + hx


def _nt_dot(a, b):
    return lax.dot_general(a, b, (((1,), (1,)), ((), ())), preferred_element_type=F32)


def _head_norm_rope(z, g, cosf, sinf):
    outs = []
    for h in range(N_HEADS):
        zh = z[:, h * HEAD_DIM:(h + 1) * HEAD_DIM]
        zh = _rms_norm(zh, g)
        outs.append(zh * cosf + pltpu.roll(zh, HEAD_DIM // 2, 1) * sinf)
    return outs


def _hawk_kernel(x_ref, g_ref, inw_ref, cw_ref, cb_ref, wri_ref, rb_ref, ib_ref, lam_ref, outw_ref,
                 o_ref, gate_ref, xpad_ref, a_ref, b_ref, hc_ref):
    tm, C = gate_ref.shape
    s = pl.program_id(1)

    @pl.when(s == 0)
    def _():
        xpad_ref[pl.ds(0, SUBLANES), :] = jnp.zeros((SUBLANES, xpad_ref.shape[1]), F32)
        hc_ref[...] = jnp.zeros_like(hc_ref)

    h = _rms_norm(x_ref[...], g_ref[...]).astype(BF16)
    gate_ref[...] = jnp.dot(h, inw_ref[:, C:], preferred_element_type=F32)

    xpad_ref[pl.ds(SUBLANES, tm), :] = jnp.dot(h, inw_ref[:, :C], preferred_element_type=F32)
    cw = cw_ref[...]
    xc = xpad_ref[pl.ds(SUBLANES - (CONV_WIDTH - 1), tm), :] * cw[0:1, :]
    for k in range(1, CONV_WIDTH):
        xc = xc + xpad_ref[pl.ds(SUBLANES - (CONV_WIDTH - 1) + k, tm), :] * cw[k:k + 1, :]
    xc = xc + cb_ref[...]
    xpad_ref[pl.ds(0, SUBLANES), :] = xpad_ref[pl.ds(tm, SUBLANES), :]

    lam = lam_ref[...]
    sp = jnp.maximum(-lam, 0.0) + jnp.log1p(jnp.exp(-jnp.abs(lam)))
    half_log2a = (-0.5 * RG_C * LOG2_E) * sp
    xc16 = xc.astype(BF16)
    for g in range(N_RG_BLOCKS):
        lo = g * RG_BW
        z = jnp.dot(xc16[:, lo:lo + RG_BW], wri_ref[g], preferred_element_type=F32)
        tr = jnp.tanh(z[:, :RG_BW] + rb_ref[:, lo:lo + RG_BW])
        ti = jnp.tanh(z[:, RG_BW:] + ib_ref[:, lo:lo + RG_BW])
        a = jnp.exp2(tr * half_log2a[:, lo:lo + RG_BW] + half_log2a[:, lo:lo + RG_BW])
        om = 1.0 - a * a
        mult = om * lax.rsqrt(jnp.maximum(om, TINY))
        hx = 0.5 * xc[:, lo:lo + RG_BW]
        a_ref[:, lo:lo + RG_BW] = a
        b_ref[:, lo:lo + RG_BW] = mult * (hx * ti + hx)

    row = lax.broadcasted_iota(I32, (SUBLANES, a_ref.shape[1]), 0)

    def scan_body(c, hc):
        r0 = pl.multiple_of(c * SUBLANES, SUBLANES)
        a = a_ref[pl.ds(r0, SUBLANES), :]
        b = b_ref[pl.ds(r0, SUBLANES), :]
        for k in (1, 2, 4):
            keep = row >= k
            b = jnp.where(keep, a * pltpu.roll(b, k, 0) + b, b)
            a = jnp.where(keep, a * pltpu.roll(a, k, 0), a)
        h = a * hc + b
        b_ref[pl.ds(r0, SUBLANES), :] = h
        return h[SUBLANES - 1:SUBLANES, :]

    hc = lax.fori_loop(0, tm // SUBLANES, scan_body, hc_ref[0:1, :])
    hc_ref[0:1, :] = hc

    gh = gate_ref[...]
    y = (b_ref[...] * (gh * jnp.tanh(gh) + gh)).astype(BF16)
    o_ref[...] = x_ref[...] + jnp.dot(y, outw_ref[...], preferred_element_type=F32)


def _hawk_layer(x2d, b_in, S, g, in_w, cw, cb, wri, rb, ib, lam, out_w):
    D = x2d.shape[1]
    C = in_w.shape[1] // 2
    tm = ROW_TILE
    ns = S // tm
    vec = lambda n: pl.BlockSpec((1, n), lambda b, s: (0, 0))
    return pl.pallas_call(
        _hawk_kernel,
        grid=(1, ns),
        in_specs=[pl.BlockSpec((tm, D), lambda b, s: (b_in * ns + s, 0)),
                  vec(D),
                  pl.BlockSpec((D, 2 * C), lambda b, s: (0, 0)),
                  pl.BlockSpec((CONV_WIDTH, C), lambda b, s: (0, 0)),
                  vec(C),
                  pl.BlockSpec((N_RG_BLOCKS, RG_BW, 2 * RG_BW), lambda b, s: (0, 0, 0)),
                  vec(C), vec(C), vec(C),
                  pl.BlockSpec((C, D), lambda b, s: (0, 0))],
        out_specs=pl.BlockSpec((tm, D), lambda b, s: (s, 0)),
        out_shape=jax.ShapeDtypeStruct((S, D), F32),
        scratch_shapes=[pltpu.VMEM((tm, C), F32),
                        pltpu.VMEM((tm + SUBLANES, C), F32),
                        pltpu.VMEM((tm, C), F32),
                        pltpu.VMEM((tm, C), F32),
                        pltpu.VMEM((SUBLANES, C), F32)],
        compiler_params=_cparams(("arbitrary", "arbitrary")),
        name="hawk_layer",
    )(x2d, g, in_w, cw, cb, wri, rb, ib, lam, out_w)


def _emit_q(u, qg, cosf, sinf, t, qp_ref, q16_ref, sg_ref):
    assert SUB_ROWS == IDX_CHUNK
    d_attn = N_HEADS * HEAD_DIM
    rows = pl.ds(t * SUB_ROWS, SUB_ROWS)
    prow = pl.ds((t // 2) * SUB_ROWS, SUB_ROWS)
    qs = _head_norm_rope(u[:, :d_attn], qg, cosf, sinf)
    for hd in range(N_HEADS):
        q16 = qs[hd].astype(BF16)
        q16_ref[0, hd, rows, :] = q16
        r = _bits(q16.astype(F32))
        swapped = pltpu.roll(r, HALF, 1)
        if t % 2 == 0:
            w = (swapped & HI16_MASK) | lax.shift_right_logical(r, 16)
            qp_ref[0, hd, prow, pl.ds(0, HALF)] = w[:, :HALF]
        else:
            w = (r & HI16_MASK) | lax.shift_right_logical(swapped, 16)
            qp_ref[0, hd, prow, pl.ds(HALF, HALF)] = w[:, HALF:]
    sg_ref[rows, :] = _silu(u[:, d_attn:])


def _kvq_proj_kernel(x_ref, gkv_ref, wkv_ref, kg_ref, gq_ref, wq_ref, qg_ref, cos_ref, sin_ref,
                     k_ref, v_ref, km_ref, q_ref, q16_ref, sg_ref):
    tm = x_ref.shape[0]
    d_attn = N_HEADS * HEAD_DIM
    ksum = {}
    for t in range(tm // SUB_ROWS):
        rows = pl.ds(t * SUB_ROWS, SUB_ROWS)
        x = x_ref[rows, :]
        xn = x * lax.rsqrt(jnp.mean(x * x, axis=-1, keepdims=True) + EPS)
        kv = jnp.dot((xn * gkv_ref[...]).astype(BF16), wkv_ref[...], preferred_element_type=F32)
        u = jnp.dot((xn * gq_ref[...]).astype(BF16), wq_ref[...], preferred_element_type=F32)
        cosf, sinf = cos_ref[rows, :], sin_ref[rows, :]
        ks = _head_norm_rope(kv[:, :d_attn], kg_ref[...], cosf, sinf)
        for hd in range(N_HEADS):
            k_ref[0, hd, rows, :] = ks[hd].astype(BF16)
            v_ref[0, hd, rows, :] = kv[:, d_attn + hd * HEAD_DIM:d_attn + (hd + 1) * HEAD_DIM].astype(BF16)
            key = (hd, (t * SUB_ROWS) // MOBA_BLOCK)
            part = jnp.sum(ks[hd], axis=0, keepdims=True)
            ksum[key] = part if key not in ksum else ksum[key] + part
        _emit_q(u, qg_ref[...], cosf, sinf, t, q_ref, q16_ref, sg_ref)
    for (hd, r), total in ksum.items():
        km_ref[0, r, pl.ds(hd, 1), :] = total * (1.0 / MOBA_BLOCK)


def _kvq_proj(x2d, B, S, gkv, wkv, kg, gq, wq, qg, cosf, sinf):
    T, D = x2d.shape
    tm = ROW_TILE
    ns = S // tm
    nb = S // MOBA_BLOCK
    rpt = tm // MOBA_BLOCK
    d_attn = N_HEADS * HEAD_DIM
    vec = lambda n: pl.BlockSpec((1, n), lambda b, s: (0, 0))
    wspec = pl.BlockSpec((D, 2 * d_attn), lambda b, s: (0, 0))
    rope = pl.BlockSpec((tm, HEAD_DIM), lambda b, s: (s, 0))
    heads = pl.BlockSpec((1, N_HEADS, tm, HEAD_DIM), lambda b, s: (b, 0, s, 0))
    hshape = lambda dt: jax.ShapeDtypeStruct((B, N_HEADS, S, HEAD_DIM), dt)
    return pl.pallas_call(
        _kvq_proj_kernel,
        grid=(B, ns),
        in_specs=[pl.BlockSpec((tm, D), lambda b, s: (b * ns + s, 0)),
                  vec(D), wspec, vec(HEAD_DIM), vec(D), wspec, vec(HEAD_DIM), rope, rope],
        out_specs=[heads, heads,
                   pl.BlockSpec((1, rpt, N_HEADS, HEAD_DIM), lambda b, s: (b, s, 0, 0)),
                   pl.BlockSpec((1, N_HEADS, tm // 2, LANES), lambda b, s: (b, 0, s, 0)), heads,
                   pl.BlockSpec((tm, d_attn), lambda b, s: (b * ns + s, 0))],
        out_shape=[hshape(BF16), hshape(BF16),
                   jax.ShapeDtypeStruct((B, nb, N_HEADS, HEAD_DIM), F32),
                   jax.ShapeDtypeStruct((B, N_HEADS, S // 2, LANES), I32), hshape(BF16),
                   jax.ShapeDtypeStruct((T, d_attn), F32)],
        compiler_params=_cparams(("arbitrary", "arbitrary")),
        name="kvq_proj",
    )(x2d, gkv, wkv, kg, gq, wq, qg, cosf, sinf)


def _q_proj_kernel(x_ref, g_ref, w_ref, qg_ref, cos_ref, sin_ref, q_ref, q16_ref, sg_ref):
    for t in range(x_ref.shape[0] // SUB_ROWS):
        rows = pl.ds(t * SUB_ROWS, SUB_ROWS)
        h = _rms_norm(x_ref[rows, :], g_ref[...]).astype(BF16)
        u = jnp.dot(h, w_ref[...], preferred_element_type=F32)
        _emit_q(u, qg_ref[...], cos_ref[rows, :], sin_ref[rows, :], t, q_ref, q16_ref, sg_ref)


def _q_proj(x2d, B, S, g, w, qg, cosf, sinf):
    T, D = x2d.shape
    tm = ROW_TILE
    ns = S // tm
    d_attn = N_HEADS * HEAD_DIM
    heads = pl.BlockSpec((1, N_HEADS, tm, HEAD_DIM), lambda b, s: (b, 0, s, 0))
    return pl.pallas_call(
        _q_proj_kernel,
        grid=(B, ns),
        in_specs=[pl.BlockSpec((tm, D), lambda b, s: (b * ns + s, 0)),
                  pl.BlockSpec((1, D), lambda b, s: (0, 0)),
                  pl.BlockSpec((D, 2 * d_attn), lambda b, s: (0, 0)),
                  pl.BlockSpec((1, HEAD_DIM), lambda b, s: (0, 0)),
                  pl.BlockSpec((tm, HEAD_DIM), lambda b, s: (s, 0)),
                  pl.BlockSpec((tm, HEAD_DIM), lambda b, s: (s, 0))],
        out_specs=[pl.BlockSpec((1, N_HEADS, tm // 2, LANES), lambda b, s: (b, 0, s, 0)), heads,
                   pl.BlockSpec((tm, d_attn), lambda b, s: (b * ns + s, 0))],
        out_shape=[jax.ShapeDtypeStruct((B, N_HEADS, S // 2, LANES), I32),
                   jax.ShapeDtypeStruct((B, N_HEADS, S, HEAD_DIM), BF16),
                   jax.ShapeDtypeStruct((T, d_attn), F32)],
        compiler_params=_cparams(("arbitrary", "arbitrary")),
        name="q_proj",
    )(x2d, g, w, qg, cosf, sinf)


def _moba_route_kernel(q_ref, km_ref, rt_ref, cnt_ref, run_ref):
    step = pl.program_id(2)
    nb = km_ref.shape[2]

    @pl.when(step == 0)
    def _():
        run_ref[...] = jnp.zeros_like(run_ref)

    km = km_ref[0, 0].astype(BF16)
    blk = lax.broadcasted_iota(I32, (nb, MOBA_BLOCK), 0)
    qa = lax.broadcasted_iota(I32, (MOBA_BLOCK, MOBA_BLOCK), 0)
    qb = lax.broadcasted_iota(I32, (MOBA_BLOCK, MOBA_BLOCK), 1)
    earlier = jnp.where(qa < qb, 1.0, 0.0).astype(BF16)
    run = run_ref[...]
    for u in range(ROUTE_TILES):
        j = step * ROUTE_TILES + u
        g = _nt_dot(km, q_ref[0, 0, pl.ds(u * MOBA_BLOCK, MOBA_BLOCK), :].astype(BF16))
        g = jnp.where(blk < j, g, -jnp.inf)
        hits, ids = [], []
        for _ in range(MOBA_TOPK):
            mx = jnp.max(g, axis=0, keepdims=True)
            idx = jnp.min(jnp.where(g == mx, blk, nb), axis=0, keepdims=True)
            hit = blk == idx
            ok = mx > -jnp.inf
            hits.append(jnp.where(hit & ok, 1.0, 0.0))
            ids.append(jnp.where(ok, idx, -1))
            g = jnp.where(hit, -jnp.inf, g)
        sel = hits[0] + hits[1] + hits[2]

        rank = jnp.dot(sel.astype(BF16), earlier, preferred_element_type=F32)
        dest = run + rank
        run = run + jnp.sum(sel, axis=1, keepdims=True)
        for r in range(MOBA_TOPK):
            loc = jnp.sum(hits[r] * dest, axis=0, keepdims=True).astype(I32)
            rt_ref[u, pl.ds(r, 1), :] = jnp.where(ids[r] >= 0, ids[r] * ROUTE_BLK_UNIT + loc, -1)
        rt_ref[u, pl.ds(MOBA_TOPK, SUBLANES - MOBA_TOPK), :] = jnp.full(
            (SUBLANES - MOBA_TOPK, MOBA_BLOCK), -1, I32)
    run_ref[...] = run
    cnt_ref[0] = run.astype(I32)


def _moba_route(q, kmean, b):
    _, H, S, Dh = q.shape
    nb = S // MOBA_BLOCK
    rows = ROUTE_TILES * MOBA_BLOCK
    nsteps = nb // ROUTE_TILES
    return pl.pallas_call(
        _moba_route_kernel,
        grid=(1, H, nsteps),
        in_specs=[pl.BlockSpec((1, 1, rows, Dh), lambda _, h, j: (b, h, j, 0)),
                  pl.BlockSpec((1, 1, nb, Dh), lambda _, h, j: (b, h, 0, 0))],
        out_specs=[pl.BlockSpec((ROUTE_TILES, SUBLANES, MOBA_BLOCK), lambda _, h, j: (h * nsteps + j, 0, 0)),
                   pl.BlockSpec((1, nb, 1), lambda _, h, j: (h, 0, 0))],
        out_shape=[jax.ShapeDtypeStruct((H * nb, SUBLANES, MOBA_BLOCK), I32),
                   jax.ShapeDtypeStruct((H, nb, 1), I32)],
        scratch_shapes=[pltpu.VMEM((nb, 1), F32)],
        compiler_params=_cparams(("arbitrary", "arbitrary", "arbitrary")),
        name="moba_route",
    )(q, kmean)


def _moba_layout_kernel(cnt_ref, rt_ref, pos_ref, qpos_ref, tblk_ref, tvalid_ref, nt_ref, base_ref, *, max_tiles):
    g = pl.program_id(0)
    nb = rt_ref.shape[0]
    slab_rows = max_tiles * GROUP_TILE
    null_row0 = (max_tiles - 1) * GROUP_TILE

    def per_block(n, first_tile):
        count = cnt_ref[g * nb + n]
        tiles = (count + (GROUP_TILE - 1)) // GROUP_TILE
        base_ref[n] = first_tile * GROUP_TILE

        def per_tile(i, carry):
            tblk_ref[g * max_tiles + first_tile + i] = n
            tvalid_ref[g * max_tiles + first_tile + i] = jnp.minimum(count - i * GROUP_TILE, GROUP_TILE)
            return carry

        lax.fori_loop(0, tiles, per_tile, 0)
        return first_tile + tiles

    used = lax.fori_loop(0, nb, per_block, 0)
    nt_ref[g] = used

    def unused_tile(t, carry):
        tblk_ref[g * max_tiles + t] = nb - 1
        tvalid_ref[g * max_tiles + t] = 0
        return carry

    lax.fori_loop(used, max_tiles, unused_tile, 0)

    code = rt_ref[...].reshape(nb * SUBLANES, MOBA_BLOCK)
    blk = lax.shift_right_arithmetic(code, ROUTE_BLK_SHIFT)
    base = jnp.zeros_like(code)
    for n in range(nb):
        base = jnp.where(blk == n, base_ref[n], base)
    lane = lax.broadcasted_iota(I32, code.shape, 1)
    row = jnp.where(code >= 0, base + (code & (ROUTE_BLK_UNIT - 1)), null_row0 + lane) + g * slab_rows
    pos_ref[...] = row.reshape(pos_ref.shape)
    p = row & (GROUP_TILE - 1)
    assert GROUP_TILE == 2 * IDX_CHUNK
    slot = ((p & (IDX_CHUNK - 1)) << 1) | lax.shift_right_logical(p, IDX_CHUNK.bit_length() - 1)
    qpos_ref[...] = ((row - p) | slot).reshape(qpos_ref.shape)


def _moba_layout(rt, counts, nb, max_tiles):
    G = rt.shape[0] // nb
    smem = pl.BlockSpec(memory_space=pltpu.SMEM)
    return pl.pallas_call(
        functools.partial(_moba_layout_kernel, max_tiles=max_tiles),
        grid_spec=pltpu.PrefetchScalarGridSpec(
            num_scalar_prefetch=1,
            grid=(G,),
            in_specs=[pl.BlockSpec((nb, SUBLANES, MOBA_BLOCK), lambda g, cnt: (g, 0, 0))],
            out_specs=[pl.BlockSpec((nb, SUBLANES, MOBA_BLOCK), lambda g, cnt: (g, 0, 0)),
                       pl.BlockSpec((nb, SUBLANES, MOBA_BLOCK), lambda g, cnt: (g, 0, 0)), smem, smem, smem],
            scratch_shapes=[pltpu.SMEM((nb,), I32)],
        ),
        out_shape=[jax.ShapeDtypeStruct((G * nb, SUBLANES, MOBA_BLOCK), I32),
                   jax.ShapeDtypeStruct((G * nb, SUBLANES, MOBA_BLOCK), I32),
                   jax.ShapeDtypeStruct((G * max_tiles,), I32),
                   jax.ShapeDtypeStruct((G * max_tiles,), I32),
                   jax.ShapeDtypeStruct((G,), I32)],
        compiler_params=_cparams(("arbitrary",)),
        name="moba_layout",
    )(counts, rt)


def _sc_mesh():
    return plsc.VectorSubcoreMesh(core_axis_name="c", subcore_axis_name="s")


def _sc_load_indices(pos_hbm, idx_v, t0, n_tiles):
    for c in range(SC_HALVES):
        pltpu.sync_copy(pos_hbm.at[pl.ds(t0, n_tiles), pl.ds(0, MOBA_TOPK), pl.ds(c * IDX_CHUNK, IDX_CHUNK)],
                        idx_v.at[c])


def _sc_scatter_rows(src, src_tile0, pos, n_out_rows):
    D = src.shape[-1]
    n_tiles = pos.shape[0]
    per_w = n_tiles // SC_WORKERS
    assert per_w * SC_WORKERS == n_tiles and src.shape[1:3] == (IDX_CHUNK, SC_HALVES)

    @functools.partial(
        pl.kernel, mesh=_sc_mesh(),
        out_type=jax.ShapeDtypeStruct((n_out_rows, D), src.dtype),
        scratch_types=[pltpu.VMEM((SC_HALVES, per_w, MOBA_TOPK, IDX_CHUNK), I32),
                       pltpu.VMEM((SC_HALVES, IDX_CHUNK, D), src.dtype),
                       pltpu.SemaphoreType.DMA((SC_HALVES,)),
                       pltpu.SemaphoreType.DMA((SC_HALVES,))],
        compiler_params=pltpu.CompilerParams(use_tc_tiling_on_sc=False),
        name="sc_scatter_rows",
    )
    def k(src_hbm, pos_hbm, out_hbm, idx_v, rows_v, load_sem, scat_sem):
        t0 = (lax.axis_index("s") * SC_CORES + lax.axis_index("c")) * per_w
        _sc_load_indices(pos_hbm, idx_v, t0, per_w)

        def load(i, c):
            return pltpu.make_async_copy(src_hbm.at[src_tile0 + t0 + i, :, c, :], rows_v.at[c], load_sem.at[c])

        def scatters(i, c):
            return [pltpu.make_async_copy(rows_v.at[c], out_hbm.at[idx_v.at[c, i, r]], scat_sem.at[c])
                    for r in range(MOBA_TOPK)]

        load(0, 0).start()

        @pl.loop(0, per_w)
        def _(i):
            for c in range(SC_HALVES):
                pi, pc = (i, 0) if c else (i - 1, 1)
                ni, nc = (i + 1, 0) if c else (i, 1)
                load(i, c).wait()
                for cp in scatters(i, c):
                    cp.start()

                @pl.when(pi >= 0)
                def _():
                    for cp in scatters(pi, pc):
                        cp.wait()

                @pl.when(ni < per_w)
                def _():
                    load(ni, nc).start()

        for cp in scatters(per_w - 1, SC_HALVES - 1):
            cp.wait()

    return k(src, pos)


def _sc_gather_rows(table, pos):
    R, D = table.shape
    n_tiles = pos.shape[0]
    N = n_tiles * MOBA_BLOCK
    per_w = n_tiles // SC_WORKERS
    assert per_w * SC_WORKERS == n_tiles

    items = SC_HALVES * MOBA_TOPK
    n_buf = 2
    assert items % n_buf == 0

    @functools.partial(
        pl.kernel, mesh=_sc_mesh(),
        out_type=jax.ShapeDtypeStruct((MOBA_TOPK, N, D), table.dtype),
        scratch_types=[pltpu.VMEM((SC_HALVES, per_w, MOBA_TOPK, IDX_CHUNK), I32),
                       pltpu.VMEM((n_buf, IDX_CHUNK, D), table.dtype),
                       pltpu.SemaphoreType.DMA((n_buf,)),
                       pltpu.SemaphoreType.DMA((n_buf,))],
        name="sc_gather_rows",
    )
    def k(table_hbm, pos_hbm, out_hbm, idx_v, rows_v, gat_sem, put_sem):
        t0 = (lax.axis_index("s") * SC_CORES + lax.axis_index("c")) * per_w
        _sc_load_indices(pos_hbm, idx_v, t0, per_w)

        def gather(i, j):
            c, r = divmod(j, MOBA_TOPK)
            return pltpu.make_async_copy(table_hbm.at[idx_v.at[c, i, r]], rows_v.at[j % n_buf], gat_sem.at[j % n_buf])

        def put(i, j):
            c, r = divmod(j, MOBA_TOPK)
            row0 = pl.multiple_of((t0 + i) * MOBA_BLOCK + c * IDX_CHUNK, IDX_CHUNK)
            return pltpu.make_async_copy(rows_v.at[j % n_buf], out_hbm.at[r, pl.ds(row0, IDX_CHUNK)],
                                         put_sem.at[j % n_buf])

        gather(0, 0).start()

        @pl.loop(0, per_w)
        def _(i):
            for j in range(items):
                pi, pj = (i, j - 1) if j else (i - 1, items - 1)
                ni, nj = (i, j + 1) if j + 1 < items else (i + 1, 0)
                gather(i, j).wait()
                put(i, j).start()

                @pl.when(pi >= 0)
                def _():
                    put(pi, pj).wait()

                @pl.when(ni < per_w)
                def _():
                    gather(ni, nj).start()

        put(per_w - 1, items - 1).wait()

    return k(table, pos)


def _bits(x):
    return lax.bitcast_convert_type(x, I32)


def _pack_partial(part, m):
    lane = lax.broadcasted_iota(I32, (part.shape[0], HEAD_DIM), 1)
    lo = _bits(part[:, :HEAD_DIM]) + BF16_HALF_ULP
    hi = _bits(jnp.where(lane == L_LANE, m, part[:, HEAD_DIM:])) + BF16_HALF_ULP
    return (hi & HI16_MASK) | lax.shift_right_logical(lo, 16)


def _null_partial(rows):
    lane = lax.broadcasted_iota(I32, (rows, HEAD_DIM), 1)
    return jnp.where(lane == L_LANE, NEG_INF_BITS & HI16_MASK, 0)


def _unpack_partial(word):
    lo = lax.bitcast_convert_type(lax.shift_left(word, 16), F32)
    hi = lax.bitcast_convert_type(word & HI16_MASK, F32)
    return lo, hi


def _augment_v(v):
    vf = v.astype(F32)
    lane = lax.broadcasted_iota(I32, vf.shape, 1)
    lo = jnp.where(lane < HALF, vf, jnp.where(lane == L_LANE, 1.0, 0.0))
    hi = jnp.where(lane < HALF, pltpu.roll(vf, HALF, 1), 0.0)
    return jnp.concatenate([lo, hi], axis=1).astype(BF16)


def _moba_group_kernel(tblk_ref, tvalid_ref, nt_ref, qg_ref, k_ref, v_ref, o_ref, va_ref, k2_ref, *, max_tiles):
    g, step = pl.program_id(0), pl.program_id(1)
    t0 = g * max_tiles + step * TILES_PER_STEP
    scale = HEAD_DIM ** -0.5

    @pl.when(step == 0)
    def _():
        def build(c, carry):
            rows = pl.ds(pl.multiple_of(c * V_BUILD_ROWS, V_BUILD_ROWS), V_BUILD_ROWS)
            va_ref[rows, :] = _augment_v(v_ref[0, 0, rows, :])
            kf = k_ref[0, 0, rows, :].astype(F32)
            lane = lax.broadcasted_iota(I32, kf.shape, 1)
            swapped = pltpu.roll(kf, HALF, 1)
            k2_ref[rows, :] = jnp.concatenate([jnp.where(lane < HALF, kf, swapped),
                                               jnp.where(lane < HALF, swapped, kf)], axis=1).astype(BF16)
            return carry

        lax.fori_loop(0, va_ref.shape[0] // V_BUILD_ROWS, build, 0)

    @pl.when(tvalid_ref[t0] > 0)
    def _():
        half_rows = GROUP_TILE // 2
        lane = lax.broadcasted_iota(I32, (half_rows, HEAD_DIM), 1)
        for u in range(TILES_PER_STEP):
            word = qg_ref[pl.ds(u * half_rows, half_rows), :]
            first = lax.bitcast_convert_type(lax.shift_left(word, 16), F32)
            second = lax.bitcast_convert_type(word & HI16_MASK, F32)
            n0 = pl.multiple_of(tblk_ref[t0 + u] * MOBA_BLOCK, MOBA_BLOCK)
            left, right = lane < HALF, lane >= HALF
            q = jnp.concatenate(
                [jnp.concatenate([jnp.where(left, first, 0.0), jnp.where(left, second, 0.0)], axis=1),
                 jnp.concatenate([jnp.where(right, first, 0.0), jnp.where(right, second, 0.0)], axis=1)],
                axis=0).astype(BF16)
            s = _nt_dot(q, k2_ref[pl.ds(n0, MOBA_BLOCK), :])
            m = (jnp.max(s, axis=1, keepdims=True) * (scale * LOG2_E)).astype(BF16).astype(F32)
            p = jnp.exp2(s * (scale * LOG2_E) - m).astype(BF16)
            part = jnp.dot(p, va_ref[pl.ds(n0, MOBA_BLOCK), :], preferred_element_type=F32)
            o_ref[pl.ds(u * GROUP_TILE, GROUP_TILE), :] = _pack_partial(part, m)

    @pl.when(step == pl.num_programs(1) - 1)
    def _():
        o_ref[pl.ds((TILES_PER_STEP - 1) * GROUP_TILE, GROUP_TILE), :] = _null_partial(GROUP_TILE)


def _moba_group(qg, k, v, b, tblk, tvalid, ntiles, max_tiles):
    _, H, S, Dh = k.shape
    G = H
    n_steps = max_tiles // TILES_PER_STEP
    step_rows = TILES_PER_STEP * GROUP_TILE

    def tile_idx(g, s, tblk, tvalid, nt):
        return (g * n_steps + jnp.where(s * TILES_PER_STEP < nt[g], s, n_steps - 1), 0)

    return pl.pallas_call(
        functools.partial(_moba_group_kernel, max_tiles=max_tiles),
        grid_spec=pltpu.PrefetchScalarGridSpec(
            num_scalar_prefetch=3,
            grid=(G, n_steps),
            in_specs=[pl.BlockSpec((step_rows // 2, LANES), tile_idx),
                      pl.BlockSpec((1, 1, S, Dh), lambda g, s, *_: (b, g, 0, 0)),
                      pl.BlockSpec((1, 1, S, Dh), lambda g, s, *_: (b, g, 0, 0))],
            out_specs=pl.BlockSpec((step_rows, LANES), tile_idx),
            scratch_shapes=[pltpu.VMEM((S, VA_W), BF16), pltpu.VMEM((S, 2 * Dh), BF16)],
        ),
        out_shape=jax.ShapeDtypeStruct((G * max_tiles * GROUP_TILE, LANES), I32),
        compiler_params=_cparams(("arbitrary", "arbitrary")),
        name="moba_group",
    )(tblk, tvalid, ntiles, qg, k, v)


def _moba_merge_kernel(*refs, n_skip):
    q_ref, k_ref, v_ref, sg_ref, x_ref, outw_ref = refs[n_skip:n_skip + 6]
    og_refs, out_ref, y_ref = refs[n_skip + 6:-2], refs[-2], refs[-1]
    scale = HEAD_DIM ** -0.5
    qi = lax.broadcasted_iota(I32, (MOBA_BLOCK, MOBA_BLOCK), 0)
    ki = lax.broadcasted_iota(I32, (MOBA_BLOCK, MOBA_BLOCK), 1)
    lane = lax.broadcasted_iota(I32, (MOBA_BLOCK, HEAD_DIM), 1)
    spread = jnp.where(lax.broadcasted_iota(I32, (HEAD_DIM, HEAD_DIM), 0) == L_LANE, 1.0, 0.0).astype(BF16)
    for u, t in [(u, t) for t in range(MERGE_TILES) for u in range(len(og_refs))]:
        og_ref = og_refs[u]
        rows = pl.ds(t * MOBA_BLOCK, MOBA_BLOCK)
        q = q_ref[0, u, rows, :].astype(BF16)
        s = _nt_dot(q, k_ref[0, u, rows, :])
        s = jnp.where(ki <= qi, s, NEG_INF)
        m_raw = jnp.max(s, axis=1, keepdims=True)
        p = jnp.exp2((s - m_raw) * (scale * LOG2_E)).astype(BF16)
        own = jnp.dot(p, _augment_v(v_ref[0, u, rows, :]), preferred_element_type=F32)
        parts = [_unpack_partial(og_ref[r, rows, :]) for r in range(MOBA_TOPK)]
        ms = [jnp.dot(jnp.where(lane == L_LANE, hi, 0.0).astype(BF16), spread, preferred_element_type=F32)
              for _, hi in parts]
        m_own = jnp.broadcast_to(m_raw * (scale * LOG2_E), lane.shape)
        m_all = m_own
        for m in ms:
            m_all = jnp.maximum(m_all, m)
        w_own = jnp.exp2(m_own - m_all)
        tot_lo = w_own * own[:, :HEAD_DIM]
        tot_hi = w_own * own[:, HEAD_DIM:]
        for (lo, hi), m in zip(parts, ms):
            w = jnp.exp2(m - m_all)
            tot_lo = tot_lo + w * lo
            tot_hi = tot_hi + w * hi
        acc = jnp.where(lane < HALF, tot_lo, pltpu.roll(tot_hi, HALF, 1))
        cols = pl.ds(u * HEAD_DIM, HEAD_DIM)
        y_ref[rows, cols] = ((acc / tot_lo[:, L_LANE:L_LANE + 1]) * sg_ref[rows, cols]).astype(BF16)

    out_ref[...] = x_ref[...] + jnp.dot(y_ref[...], outw_ref[...], preferred_element_type=F32)


def _moba_merge(q, k, v, sg, og, x2d, out_w, out_batch, n_out_batch, o_prev):
    _, H, S, Dh = q.shape
    D = x2d.shape[1]
    b = 0
    hps = HEADS_PER_STEP
    assert hps == H
    rows = MERGE_TILES * MOBA_BLOCK
    ns = S // rows
    og_spec = lambda u: pl.BlockSpec((MOBA_TOPK, rows, LANES),
                                     lambda _, hp, j: (0, (hp * hps + u) * ns + j, 0))
    prev_specs = [] if o_prev is None else [pl.BlockSpec(memory_space=pl.ANY)]
    prev_args = [] if o_prev is None else [o_prev]
    return pl.pallas_call(
        functools.partial(_moba_merge_kernel, n_skip=len(prev_args)),
        grid=(1, H // hps, ns),
        in_specs=prev_specs
                 + [pl.BlockSpec((1, hps, rows, Dh), lambda _, hp, j: (b, hp, j, 0)),
                    pl.BlockSpec((1, hps, rows, Dh), lambda _, hp, j: (b, hp, j, 0)),
                    pl.BlockSpec((1, hps, rows, Dh), lambda _, hp, j: (b, hp, j, 0)),
                    pl.BlockSpec((rows, hps * Dh), lambda _, hp, j: (b * ns + j, hp)),
                    pl.BlockSpec((rows, D), lambda _, hp, j: (b * ns + j, 0)),
                    pl.BlockSpec((H * Dh, D), lambda _, hp, j: (0, 0))]
                 + [og_spec(u) for u in range(hps)],
        out_specs=pl.BlockSpec((rows, D), lambda _, hp, j: (out_batch * ns + j, 0)),
        out_shape=jax.ShapeDtypeStruct((n_out_batch * S, D), F32),
        scratch_shapes=[pltpu.VMEM((rows, H * Dh), BF16)],
        input_output_aliases={} if o_prev is None else {0: 0},
        compiler_params=_cparams(("arbitrary", "arbitrary", "arbitrary")),
        name="moba_merge",
    )(*prev_args, q, k, v, sg, x2d, out_w, *([og] * hps))


def _moba_layer(qp, q16, k, v, kmean, sg, x2d, out_w, out_batch, n_out_batch, o_prev):
    _, H, S, Dh = q16.shape
    nb = S // MOBA_BLOCK
    max_tiles = (MOBA_TOPK * S) // GROUP_TILE + nb + 1
    max_tiles = -(-max_tiles // TILES_PER_STEP) * TILES_PER_STEP
    slab_rows = max_tiles * GROUP_TILE

    q_rows = qp.reshape(H * nb, IDX_CHUNK, SC_HALVES, HALF)
    rt, counts = _moba_route(q16, kmean, 0)
    pos, qpos, tblk, tvalid, ntiles = _moba_layout(rt, counts.reshape(-1), nb, max_tiles)
    qg = _sc_scatter_rows(q_rows, 0, qpos, H * slab_rows)
    qg = qg.reshape(H * slab_rows // 2, LANES)
    parts = _moba_group(qg, k, v, 0, tblk, tvalid, ntiles, max_tiles)
    og = _sc_gather_rows(parts, pos)
    return _moba_merge(q16, k, v, sg, og, x2d, out_w, out_batch, n_out_batch, o_prev)


def _rope_tables(S):
    inv = np.float32(ROPE_THETA) ** (-np.arange(HALF, dtype=np.float32) / np.float32(HALF))
    ang = np.arange(S, dtype=np.float32)[:, None] * inv[None, :].astype(np.float32)
    cos, sin = np.cos(ang).astype(np.float32), np.sin(ang).astype(np.float32)
    return (jnp.asarray(np.concatenate([cos, cos], axis=-1)),
            jnp.asarray(np.concatenate([-sin, sin], axis=-1)))


def kernel(x, a_norm, a_in_w, a_conv_w, a_conv_b, a_r_w, a_r_b, a_i_w, a_i_b, a_lambda, a_out_w,
           kv_norm, kv_w, k_norm, b_norm, b_in_w, q_norm, b_out_w):
    B, S, D = x.shape
    assert S % ROW_TILE == 0 and ROW_TILE % MOBA_BLOCK == 0
    assert D == N_RG_BLOCKS * RG_BW == N_HEADS * HEAD_DIM
    assert (B * N_HEADS * S) % (IDX_CHUNK * SC_WORKERS) == 0
    n_a, n_b = a_in_w.shape[0], b_in_w.shape[0]
    assert n_a >= 1 and n_b >= 1
    x2d = x.reshape(B * S, D)
    row = lambda v: v.reshape(1, -1)
    cosf, sinf = _rope_tables(S)

    out = None
    for b in range(B):
        xs, b_in = x2d, b
        for l in range(n_a):
            c_rnn = a_in_w.shape[2] // 2
            wri = (0.5 * jnp.concatenate([a_r_w[l], a_i_w[l]], axis=-1)).astype(BF16)
            in_w = jnp.concatenate([a_in_w[l][:, :c_rnn], 0.5 * a_in_w[l][:, c_rnn:]], axis=-1).astype(BF16)
            xs = _hawk_layer(xs, b_in, S, row(a_norm[l]), in_w, a_conv_w[l], row(a_conv_b[l]),
                             wri, row(0.5 * a_r_b[l]), row(0.5 * a_i_b[l]), row(a_lambda[l]),
                             a_out_w[l].astype(BF16))
            b_in = 0
        for jl in range(n_b):
            q_args = (row(b_norm[jl]), b_in_w[jl].astype(BF16), row(q_norm[jl]))
            if jl == 0:
                k, v, kmean, qp, q16, sg = _kvq_proj(xs, 1, S, row(kv_norm), kv_w.astype(BF16), row(k_norm),
                                                    *q_args, cosf, sinf)
                kmean = jnp.transpose(kmean, (0, 2, 1, 3))
            else:
                qp, q16, sg = _q_proj(xs, 1, S, *q_args, cosf, sinf)
            w_out = b_out_w[jl].astype(BF16)
            if jl == n_b - 1:
                out = xs = _moba_layer(qp, q16, k, v, kmean, sg, xs, w_out, b, B, out)
            else:
                xs = _moba_layer(qp, q16, k, v, kmean, sg, xs, w_out, 0, 1, None)
    return out.reshape(B, S, D)
```

```python
import functools

import jax
import jax.numpy as jnp
import numpy as np
from jax import lax
from jax.experimental import pallas as pl
from jax.experimental.pallas import tpu as pltpu
from jax.experimental.pallas import tpu_sc as plsc

N_HEADS = 8
HEAD_DIM = 128
MOBA_BLOCK = 256
MOBA_TOPK = 3
CONV_WIDTH = 4
N_RG_BLOCKS = 8
RG_BW = 128
RG_C = 8.0
ROPE_THETA = 10000.0
EPS = 1e-6
NEG_INF = -1e30
LOG2_E = 1.4426950408889634
TINY = 1e-30

SUBLANES = 8
LANES = 128
ROW_TILE = 512
SUB_ROWS = 128
VMEM_LIMIT = 56 * 1024 * 1024

GROUP_TILE = MOBA_BLOCK
TILES_PER_STEP = 32
V_BUILD_ROWS = 1024
HEADS_PER_STEP = 8
MERGE_TILES = 2
ROUTE_TILES = 16
ROUTE_BLK_SHIFT = 16
ROUTE_BLK_UNIT = 1 << ROUTE_BLK_SHIFT
HALF = HEAD_DIM // 2
VA_W = 2 * HEAD_DIM
L_LANE = HALF
HI16_MASK = -65536
BF16_HALF_ULP = 0x8000
NEG_INF_BITS = int(np.float32(NEG_INF).view(np.int32))

SC_CORES = 2
SC_SUBCORES = 16
SC_WORKERS = SC_CORES * SC_SUBCORES
IDX_CHUNK = 128
SC_HALVES = MOBA_BLOCK // IDX_CHUNK

F32 = jnp.float32
BF16 = jnp.bfloat16
I32 = jnp.int32


def _cparams(sem):
    return pltpu.CompilerParams(dimension_semantics=sem, vmem_limit_bytes=VMEM_LIMIT)


def _rms_norm(x, g):
    ms = jnp.mean(x * x, axis=-1, keepdims=True)
    return (x * lax.rsqrt(ms + EPS)) * g


def _silu(x):
    hx = 0.5 * x
    return hx * jnp.tanh(hx) + hx


def _nt_dot(a, b):
    return lax.dot_general(a, b, (((1,), (1,)), ((), ())), preferred_element_type=F32)


def _head_norm_rope(z, g, cosf, sinf):
    outs = []
    for h in range(N_HEADS):
        zh = z[:, h * HEAD_DIM:(h + 1) * HEAD_DIM]
        zh = _rms_norm(zh, g)
        outs.append(zh * cosf + pltpu.roll(zh, HEAD_DIM // 2, 1) * sinf)
    return outs


def _hawk_kernel(x_ref, g_ref, inw_ref, cw_ref, cb_ref, wri_ref, rb_ref, ib_ref, lam_ref, outw_ref,
                 o_ref, gate_ref, xpad_ref, a_ref, b_ref, hc_ref):
    tm, C = gate_ref.shape
    s = pl.program_id(1)

    @pl.when(s == 0)
    def _():
        xpad_ref[pl.ds(0, SUBLANES), :] = jnp.zeros((SUBLANES, xpad_ref.shape[1]), F32)
        hc_ref[...] = jnp.zeros_like(hc_ref)

    h = _rms_norm(x_ref[...], g_ref[...]).astype(BF16)
    gate_ref[...] = jnp.dot(h, inw_ref[:, C:], preferred_element_type=F32)

    xpad_ref[pl.ds(SUBLANES, tm), :] = jnp.dot(h, inw_ref[:, :C], preferred_element_type=F32)
    cw = cw_ref[...]
    xc = xpad_ref[pl.ds(SUBLANES - (CONV_WIDTH - 1), tm), :] * cw[0:1, :]
    for k in range(1, CONV_WIDTH):
        xc = xc + xpad_ref[pl.ds(SUBLANES - (CONV_WIDTH - 1) + k, tm), :] * cw[k:k + 1, :]
    xc = xc + cb_ref[...]
    xpad_ref[pl.ds(0, SUBLANES), :] = xpad_ref[pl.ds(tm, SUBLANES), :]

    lam = lam_ref[...]
    sp = jnp.maximum(-lam, 0.0) + jnp.log1p(jnp.exp(-jnp.abs(lam)))
    half_log2a = (-0.5 * RG_C * LOG2_E) * sp
    xc16 = xc.astype(BF16)
    for g in range(N_RG_BLOCKS):
        lo = g * RG_BW
        z = jnp.dot(xc16[:, lo:lo + RG_BW], wri_ref[g], preferred_element_type=F32)
        tr = jnp.tanh(z[:, :RG_BW] + rb_ref[:, lo:lo + RG_BW])
        ti = jnp.tanh(z[:, RG_BW:] + ib_ref[:, lo:lo + RG_BW])
        a = jnp.exp2(tr * half_log2a[:, lo:lo + RG_BW] + half_log2a[:, lo:lo + RG_BW])
        om = 1.0 - a * a
        mult = om * lax.rsqrt(jnp.maximum(om, TINY))
        hx = 0.5 * xc[:, lo:lo + RG_BW]
        a_ref[:, lo:lo + RG_BW] = a
        b_ref[:, lo:lo + RG_BW] = mult * (hx * ti + hx)

    row = lax.broadcasted_iota(I32, (SUBLANES, a_ref.shape[1]), 0)

    def scan_body(c, hc):
        r0 = pl.multiple_of(c * SUBLANES, SUBLANES)
        a = a_ref[pl.ds(r0, SUBLANES), :]
        b = b_ref[pl.ds(r0, SUBLANES), :]
        for k in (1, 2, 4):
            keep = row >= k
            b = jnp.where(keep, a * pltpu.roll(b, k, 0) + b, b)
            a = jnp.where(keep, a * pltpu.roll(a, k, 0), a)
        h = a * hc + b
        b_ref[pl.ds(r0, SUBLANES), :] = h
        return h[SUBLANES - 1:SUBLANES, :]

    hc = lax.fori_loop(0, tm // SUBLANES, scan_body, hc_ref[0:1, :])
    hc_ref[0:1, :] = hc

    gh = gate_ref[...]
    y = (b_ref[...] * (gh * jnp.tanh(gh) + gh)).astype(BF16)
    o_ref[...] = x_ref[...] + jnp.dot(y, outw_ref[...], preferred_element_type=F32)


def _hawk_layer(x2d, b_in, S, g, in_w, cw, cb, wri, rb, ib, lam, out_w):
    D = x2d.shape[1]
    C = in_w.shape[1] // 2
    tm = ROW_TILE
    ns = S // tm
    vec = lambda n: pl.BlockSpec((1, n), lambda b, s: (0, 0))
    return pl.pallas_call(
        _hawk_kernel,
        grid=(1, ns),
        in_specs=[pl.BlockSpec((tm, D), lambda b, s: (b_in * ns + s, 0)),
                  vec(D),
                  pl.BlockSpec((D, 2 * C), lambda b, s: (0, 0)),
                  pl.BlockSpec((CONV_WIDTH, C), lambda b, s: (0, 0)),
                  vec(C),
                  pl.BlockSpec((N_RG_BLOCKS, RG_BW, 2 * RG_BW), lambda b, s: (0, 0, 0)),
                  vec(C), vec(C), vec(C),
                  pl.BlockSpec((C, D), lambda b, s: (0, 0))],
        out_specs=pl.BlockSpec((tm, D), lambda b, s: (s, 0)),
        out_shape=jax.ShapeDtypeStruct((S, D), F32),
        scratch_shapes=[pltpu.VMEM((tm, C), F32),
                        pltpu.VMEM((tm + SUBLANES, C), F32),
                        pltpu.VMEM((tm, C), F32),
                        pltpu.VMEM((tm, C), F32),
                        pltpu.VMEM((SUBLANES, C), F32)],
        compiler_params=_cparams(("arbitrary", "arbitrary")),
        name="hawk_layer",
    )(x2d, g, in_w, cw, cb, wri, rb, ib, lam, out_w)


def _emit_q(u, qg, cosf, sinf, t, qp_ref, q16_ref, sg_ref):
    assert SUB_ROWS == IDX_CHUNK
    d_attn = N_HEADS * HEAD_DIM
    rows = pl.ds(t * SUB_ROWS, SUB_ROWS)
    prow = pl.ds((t // 2) * SUB_ROWS, SUB_ROWS)
    qs = _head_norm_rope(u[:, :d_attn], qg, cosf, sinf)
    for hd in range(N_HEADS):
        q16 = qs[hd].astype(BF16)
        q16_ref[0, hd, rows, :] = q16
        r = _bits(q16.astype(F32))
        swapped = pltpu.roll(r, HALF, 1)
        if t % 2 == 0:
            w = (swapped & HI16_MASK) | lax.shift_right_logical(r, 16)
            qp_ref[0, hd, prow, pl.ds(0, HALF)] = w[:, :HALF]
        else:
            w = (r & HI16_MASK) | lax.shift_right_logical(swapped, 16)
            qp_ref[0, hd, prow, pl.ds(HALF, HALF)] = w[:, HALF:]
    sg_ref[rows, :] = _silu(u[:, d_attn:])


def _kvq_proj_kernel(x_ref, gkv_ref, wkv_ref, kg_ref, gq_ref, wq_ref, qg_ref, cos_ref, sin_ref,
                     k_ref, v_ref, km_ref, q_ref, q16_ref, sg_ref):
    tm = x_ref.shape[0]
    d_attn = N_HEADS * HEAD_DIM
    ksum = {}
    for t in range(tm // SUB_ROWS):
        rows = pl.ds(t * SUB_ROWS, SUB_ROWS)
        x = x_ref[rows, :]
        xn = x * lax.rsqrt(jnp.mean(x * x, axis=-1, keepdims=True) + EPS)
        kv = jnp.dot((xn * gkv_ref[...]).astype(BF16), wkv_ref[...], preferred_element_type=F32)
        u = jnp.dot((xn * gq_ref[...]).astype(BF16), wq_ref[...], preferred_element_type=F32)
        cosf, sinf = cos_ref[rows, :], sin_ref[rows, :]
        ks = _head_norm_rope(kv[:, :d_attn], kg_ref[...], cosf, sinf)
        for hd in range(N_HEADS):
            k_ref[0, hd, rows, :] = ks[hd].astype(BF16)
            v_ref[0, hd, rows, :] = kv[:, d_attn + hd * HEAD_DIM:d_attn + (hd + 1) * HEAD_DIM].astype(BF16)
            key = (hd, (t * SUB_ROWS) // MOBA_BLOCK)
            part = jnp.sum(ks[hd], axis=0, keepdims=True)
            ksum[key] = part if key not in ksum else ksum[key] + part
        _emit_q(u, qg_ref[...], cosf, sinf, t, q_ref, q16_ref, sg_ref)
    for (hd, r), total in ksum.items():
        km_ref[0, r, pl.ds(hd, 1), :] = total * (1.0 / MOBA_BLOCK)


def _kvq_proj(x2d, B, S, gkv, wkv, kg, gq, wq, qg, cosf, sinf):
    T, D = x2d.shape
    tm = ROW_TILE
    ns = S // tm
    nb = S // MOBA_BLOCK
    rpt = tm // MOBA_BLOCK
    d_attn = N_HEADS * HEAD_DIM
    vec = lambda n: pl.BlockSpec((1, n), lambda b, s: (0, 0))
    wspec = pl.BlockSpec((D, 2 * d_attn), lambda b, s: (0, 0))
    rope = pl.BlockSpec((tm, HEAD_DIM), lambda b, s: (s, 0))
    heads = pl.BlockSpec((1, N_HEADS, tm, HEAD_DIM), lambda b, s: (b, 0, s, 0))
    hshape = lambda dt: jax.ShapeDtypeStruct((B, N_HEADS, S, HEAD_DIM), dt)
    return pl.pallas_call(
        _kvq_proj_kernel,
        grid=(B, ns),
        in_specs=[pl.BlockSpec((tm, D), lambda b, s: (b * ns + s, 0)),
                  vec(D), wspec, vec(HEAD_DIM), vec(D), wspec, vec(HEAD_DIM), rope, rope],
        out_specs=[heads, heads,
                   pl.BlockSpec((1, rpt, N_HEADS, HEAD_DIM), lambda b, s: (b, s, 0, 0)),
                   pl.BlockSpec((1, N_HEADS, tm // 2, LANES), lambda b, s: (b, 0, s, 0)), heads,
                   pl.BlockSpec((tm, d_attn), lambda b, s: (b * ns + s, 0))],
        out_shape=[hshape(BF16), hshape(BF16),
                   jax.ShapeDtypeStruct((B, nb, N_HEADS, HEAD_DIM), F32),
                   jax.ShapeDtypeStruct((B, N_HEADS, S // 2, LANES), I32), hshape(BF16),
                   jax.ShapeDtypeStruct((T, d_attn), F32)],
        compiler_params=_cparams(("arbitrary", "arbitrary")),
        name="kvq_proj",
    )(x2d, gkv, wkv, kg, gq, wq, qg, cosf, sinf)


def _q_proj_kernel(x_ref, g_ref, w_ref, qg_ref, cos_ref, sin_ref, q_ref, q16_ref, sg_ref):
    for t in range(x_ref.shape[0] // SUB_ROWS):
        rows = pl.ds(t * SUB_ROWS, SUB_ROWS)
        h = _rms_norm(x_ref[rows, :], g_ref[...]).astype(BF16)
        u = jnp.dot(h, w_ref[...], preferred_element_type=F32)
        _emit_q(u, qg_ref[...], cos_ref[rows, :], sin_ref[rows, :], t, q_ref, q16_ref, sg_ref)


def _q_proj(x2d, B, S, g, w, qg, cosf, sinf):
    T, D = x2d.shape
    tm = ROW_TILE
    ns = S // tm
    d_attn = N_HEADS * HEAD_DIM
    heads = pl.BlockSpec((1, N_HEADS, tm, HEAD_DIM), lambda b, s: (b, 0, s, 0))
    return pl.pallas_call(
        _q_proj_kernel,
        grid=(B, ns),
        in_specs=[pl.BlockSpec((tm, D), lambda b, s: (b * ns + s, 0)),
                  pl.BlockSpec((1, D), lambda b, s: (0, 0)),
                  pl.BlockSpec((D, 2 * d_attn), lambda b, s: (0, 0)),
                  pl.BlockSpec((1, HEAD_DIM), lambda b, s: (0, 0)),
                  pl.BlockSpec((tm, HEAD_DIM), lambda b, s: (s, 0)),
                  pl.BlockSpec((tm, HEAD_DIM), lambda b, s: (s, 0))],
        out_specs=[pl.BlockSpec((1, N_HEADS, tm // 2, LANES), lambda b, s: (b, 0, s, 0)), heads,
                   pl.BlockSpec((tm, d_attn), lambda b, s: (b * ns + s, 0))],
        out_shape=[jax.ShapeDtypeStruct((B, N_HEADS, S // 2, LANES), I32),
                   jax.ShapeDtypeStruct((B, N_HEADS, S, HEAD_DIM), BF16),
                   jax.ShapeDtypeStruct((T, d_attn), F32)],
        compiler_params=_cparams(("arbitrary", "arbitrary")),
        name="q_proj",
    )(x2d, g, w, qg, cosf, sinf)


def _moba_route_kernel(q_ref, km_ref, rt_ref, cnt_ref, run_ref):
    step = pl.program_id(2)
    nb = km_ref.shape[2]

    @pl.when(step == 0)
    def _():
        run_ref[...] = jnp.zeros_like(run_ref)

    km = km_ref[0, 0].astype(BF16)
    blk = lax.broadcasted_iota(I32, (nb, MOBA_BLOCK), 0)
    qa = lax.broadcasted_iota(I32, (MOBA_BLOCK, MOBA_BLOCK), 0)
    qb = lax.broadcasted_iota(I32, (MOBA_BLOCK, MOBA_BLOCK), 1)
    earlier = jnp.where(qa < qb, 1.0, 0.0).astype(BF16)
    run = run_ref[...]
    for u in range(ROUTE_TILES):
        j = step * ROUTE_TILES + u
        g = _nt_dot(km, q_ref[0, 0, pl.ds(u * MOBA_BLOCK, MOBA_BLOCK), :].astype(BF16))
        g = jnp.where(blk < j, g, -jnp.inf)
        hits, ids = [], []
        for _ in range(MOBA_TOPK):
            mx = jnp.max(g, axis=0, keepdims=True)
            idx = jnp.min(jnp.where(g == mx, blk, nb), axis=0, keepdims=True)
            hit = blk == idx
            ok = mx > -jnp.inf
            hits.append(jnp.where(hit & ok, 1.0, 0.0))
            ids.append(jnp.where(ok, idx, -1))
            g = jnp.where(hit, -jnp.inf, g)
        sel = hits[0] + hits[1] + hits[2]

        rank = jnp.dot(sel.astype(BF16), earlier, preferred_element_type=F32)
        dest = run + rank
        run = run + jnp.sum(sel, axis=1, keepdims=True)
        for r in range(MOBA_TOPK):
            loc = jnp.sum(hits[r] * dest, axis=0, keepdims=True).astype(I32)
            rt_ref[u, pl.ds(r, 1), :] = jnp.where(ids[r] >= 0, ids[r] * ROUTE_BLK_UNIT + loc, -1)
        rt_ref[u, pl.ds(MOBA_TOPK, SUBLANES - MOBA_TOPK), :] = jnp.full(
            (SUBLANES - MOBA_TOPK, MOBA_BLOCK), -1, I32)
    run_ref[...] = run
    cnt_ref[0] = run.astype(I32)


def _moba_route(q, kmean, b):
    _, H, S, Dh = q.shape
    nb = S // MOBA_BLOCK
    rows = ROUTE_TILES * MOBA_BLOCK
    nsteps = nb // ROUTE_TILES
    return pl.pallas_call(
        _moba_route_kernel,
        grid=(1, H, nsteps),
        in_specs=[pl.BlockSpec((1, 1, rows, Dh), lambda _, h, j: (b, h, j, 0)),
                  pl.BlockSpec((1, 1, nb, Dh), lambda _, h, j: (b, h, 0, 0))],
        out_specs=[pl.BlockSpec((ROUTE_TILES, SUBLANES, MOBA_BLOCK), lambda _, h, j: (h * nsteps + j, 0, 0)),
                   pl.BlockSpec((1, nb, 1), lambda _, h, j: (h, 0, 0))],
        out_shape=[jax.ShapeDtypeStruct((H * nb, SUBLANES, MOBA_BLOCK), I32),
                   jax.ShapeDtypeStruct((H, nb, 1), I32)],
        scratch_shapes=[pltpu.VMEM((nb, 1), F32)],
        compiler_params=_cparams(("arbitrary", "arbitrary", "arbitrary")),
        name="moba_route",
    )(q, kmean)


def _moba_layout_kernel(cnt_ref, rt_ref, pos_ref, qpos_ref, tblk_ref, tvalid_ref, nt_ref, base_ref, *, max_tiles):
    g = pl.program_id(0)
    nb = rt_ref.shape[0]
    slab_rows = max_tiles * GROUP_TILE
    null_row0 = (max_tiles - 1) * GROUP_TILE

    def per_block(n, first_tile):
        count = cnt_ref[g * nb + n]
        tiles = (count + (GROUP_TILE - 1)) // GROUP_TILE
        base_ref[n] = first_tile * GROUP_TILE

        def per_tile(i, carry):
            tblk_ref[g * max_tiles + first_tile + i] = n
            tvalid_ref[g * max_tiles + first_tile + i] = jnp.minimum(count - i * GROUP_TILE, GROUP_TILE)
            return carry

        lax.fori_loop(0, tiles, per_tile, 0)
        return first_tile + tiles

    used = lax.fori_loop(0, nb, per_block, 0)
    nt_ref[g] = used

    def unused_tile(t, carry):
        tblk_ref[g * max_tiles + t] = nb - 1
        tvalid_ref[g * max_tiles + t] = 0
        return carry

    lax.fori_loop(used, max_tiles, unused_tile, 0)

    code = rt_ref[...].reshape(nb * SUBLANES, MOBA_BLOCK)
    blk = lax.shift_right_arithmetic(code, ROUTE_BLK_SHIFT)
    base = jnp.zeros_like(code)
    for n in range(nb):
        base = jnp.where(blk == n, base_ref[n], base)
    lane = lax.broadcasted_iota(I32, code.shape, 1)
    row = jnp.where(code >= 0, base + (code & (ROUTE_BLK_UNIT - 1)), null_row0 + lane) + g * slab_rows
    pos_ref[...] = row.reshape(pos_ref.shape)
    p = row & (GROUP_TILE - 1)
    assert GROUP_TILE == 2 * IDX_CHUNK
    slot = ((p & (IDX_CHUNK - 1)) << 1) | lax.shift_right_logical(p, IDX_CHUNK.bit_length() - 1)
    qpos_ref[...] = ((row - p) | slot).reshape(qpos_ref.shape)


def _moba_layout(rt, counts, nb, max_tiles):
    G = rt.shape[0] // nb
    smem = pl.BlockSpec(memory_space=pltpu.SMEM)
    return pl.pallas_call(
        functools.partial(_moba_layout_kernel, max_tiles=max_tiles),
        grid_spec=pltpu.PrefetchScalarGridSpec(
            num_scalar_prefetch=1,
            grid=(G,),
            in_specs=[pl.BlockSpec((nb, SUBLANES, MOBA_BLOCK), lambda g, cnt: (g, 0, 0))],
            out_specs=[pl.BlockSpec((nb, SUBLANES, MOBA_BLOCK), lambda g, cnt: (g, 0, 0)),
                       pl.BlockSpec((nb, SUBLANES, MOBA_BLOCK), lambda g, cnt: (g, 0, 0)), smem, smem, smem],
            scratch_shapes=[pltpu.SMEM((nb,), I32)],
        ),
        out_shape=[jax.ShapeDtypeStruct((G * nb, SUBLANES, MOBA_BLOCK), I32),
                   jax.ShapeDtypeStruct((G * nb, SUBLANES, MOBA_BLOCK), I32),
                   jax.ShapeDtypeStruct((G * max_tiles,), I32),
                   jax.ShapeDtypeStruct((G * max_tiles,), I32),
                   jax.ShapeDtypeStruct((G,), I32)],
        compiler_params=_cparams(("arbitrary",)),
        name="moba_layout",
    )(counts, rt)


def _sc_mesh():
    return plsc.VectorSubcoreMesh(core_axis_name="c", subcore_axis_name="s")


def _sc_load_indices(pos_hbm, idx_v, t0, n_tiles):
    for c in range(SC_HALVES):
        pltpu.sync_copy(pos_hbm.at[pl.ds(t0, n_tiles), pl.ds(0, MOBA_TOPK), pl.ds(c * IDX_CHUNK, IDX_CHUNK)],
                        idx_v.at[c])


def _sc_scatter_rows(src, src_tile0, pos, n_out_rows):
    D = src.shape[-1]
    n_tiles = pos.shape[0]
    per_w = n_tiles // SC_WORKERS
    assert per_w * SC_WORKERS == n_tiles and src.shape[1:3] == (IDX_CHUNK, SC_HALVES)

    @functools.partial(
        pl.kernel, mesh=_sc_mesh(),
        out_type=jax.ShapeDtypeStruct((n_out_rows, D), src.dtype),
        scratch_types=[pltpu.VMEM((SC_HALVES, per_w, MOBA_TOPK, IDX_CHUNK), I32),
                       pltpu.VMEM((SC_HALVES, IDX_CHUNK, D), src.dtype),
                       pltpu.SemaphoreType.DMA((SC_HALVES,)),
                       pltpu.SemaphoreType.DMA((SC_HALVES,))],
        compiler_params=pltpu.CompilerParams(use_tc_tiling_on_sc=False),
        name="sc_scatter_rows",
    )
    def k(src_hbm, pos_hbm, out_hbm, idx_v, rows_v, load_sem, scat_sem):
        t0 = (lax.axis_index("s") * SC_CORES + lax.axis_index("c")) * per_w
        _sc_load_indices(pos_hbm, idx_v, t0, per_w)

        def load(i, c):
            return pltpu.make_async_copy(src_hbm.at[src_tile0 + t0 + i, :, c, :], rows_v.at[c], load_sem.at[c])

        def scatters(i, c):
            return [pltpu.make_async_copy(rows_v.at[c], out_hbm.at[idx_v.at[c, i, r]], scat_sem.at[c])
                    for r in range(MOBA_TOPK)]

        load(0, 0).start()

        @pl.loop(0, per_w)
        def _(i):
            for c in range(SC_HALVES):
                pi, pc = (i, 0) if c else (i - 1, 1)
                ni, nc = (i + 1, 0) if c else (i, 1)
                load(i, c).wait()
                for cp in scatters(i, c):
                    cp.start()

                @pl.when(pi >= 0)
                def _():
                    for cp in scatters(pi, pc):
                        cp.wait()

                @pl.when(ni < per_w)
                def _():
                    load(ni, nc).start()

        for cp in scatters(per_w - 1, SC_HALVES - 1):
            cp.wait()

    return k(src, pos)


def _sc_gather_rows(table, pos):
    R, D = table.shape
    n_tiles = pos.shape[0]
    N = n_tiles * MOBA_BLOCK
    per_w = n_tiles // SC_WORKERS
    assert per_w * SC_WORKERS == n_tiles

    items = SC_HALVES * MOBA_TOPK
    n_buf = 2
    assert items % n_buf == 0

    @functools.partial(
        pl.kernel, mesh=_sc_mesh(),
        out_type=jax.ShapeDtypeStruct((MOBA_TOPK, N, D), table.dtype),
        scratch_types=[pltpu.VMEM((SC_HALVES, per_w, MOBA_TOPK, IDX_CHUNK), I32),
                       pltpu.VMEM((n_buf, IDX_CHUNK, D), table.dtype),
                       pltpu.SemaphoreType.DMA((n_buf,)),
                       pltpu.SemaphoreType.DMA((n_buf,))],
        name="sc_gather_rows",
    )
    def k(table_hbm, pos_hbm, out_hbm, idx_v, rows_v, gat_sem, put_sem):
        t0 = (lax.axis_index("s") * SC_CORES + lax.axis_index("c")) * per_w
        _sc_load_indices(pos_hbm, idx_v, t0, per_w)

        def gather(i, j):
            c, r = divmod(j, MOBA_TOPK)
            return pltpu.make_async_copy(table_hbm.at[idx_v.at[c, i, r]], rows_v.at[j % n_buf], gat_sem.at[j % n_buf])

        def put(i, j):
            c, r = divmod(j, MOBA_TOPK)
            row0 = pl.multiple_of((t0 + i) * MOBA_BLOCK + c * IDX_CHUNK, IDX_CHUNK)
            return pltpu.make_async_copy(rows_v.at[j % n_buf], out_hbm.at[r, pl.ds(row0, IDX_CHUNK)],
                                         put_sem.at[j % n_buf])

        gather(0, 0).start()

        @pl.loop(0, per_w)
        def _(i):
            for j in range(items):
                pi, pj = (i, j - 1) if j else (i - 1, items - 1)
                ni, nj = (i, j + 1) if j + 1 < items else (i + 1, 0)
                gather(i, j).wait()
                put(i, j).start()

                @pl.when(pi >= 0)
                def _():
                    put(pi, pj).wait()

                @pl.when(ni < per_w)
                def _():
                    gather(ni, nj).start()

        put(per_w - 1, items - 1).wait()

    return k(table, pos)


def _bits(x):
    return lax.bitcast_convert_type(x, I32)


def _pack_partial(part, m):
    lane = lax.broadcasted_iota(I32, (part.shape[0], HEAD_DIM), 1)
    lo = _bits(part[:, :HEAD_DIM]) + BF16_HALF_ULP
    hi = _bits(jnp.where(lane == L_LANE, m, part[:, HEAD_DIM:])) + BF16_HALF_ULP
    return (hi & HI16_MASK) | lax.shift_right_logical(lo, 16)


def _null_partial(rows):
    lane = lax.broadcasted_iota(I32, (rows, HEAD_DIM), 1)
    return jnp.where(lane == L_LANE, NEG_INF_BITS & HI16_MASK, 0)


def _unpack_partial(word):
    lo = lax.bitcast_convert_type(lax.shift_left(word, 16), F32)
    hi = lax.bitcast_convert_type(word & HI16_MASK, F32)
    return lo, hi


def _augment_v(v):
    vf = v.astype(F32)
    lane = lax.broadcasted_iota(I32, vf.shape, 1)
    lo = jnp.where(lane < HALF, vf, jnp.where(lane == L_LANE, 1.0, 0.0))
    hi = jnp.where(lane < HALF, pltpu.roll(vf, HALF, 1), 0.0)
    return jnp.concatenate([lo, hi], axis=1).astype(BF16)


def _moba_group_kernel(tblk_ref, tvalid_ref, nt_ref, qg_ref, k_ref, v_ref, o_ref, va_ref, k2_ref, *, max_tiles):
    g, step = pl.program_id(0), pl.program_id(1)
    t0 = g * max_tiles + step * TILES_PER_STEP
    scale = HEAD_DIM ** -0.5

    @pl.when(step == 0)
    def _():
        def build(c, carry):
            rows = pl.ds(pl.multiple_of(c * V_BUILD_ROWS, V_BUILD_ROWS), V_BUILD_ROWS)
            va_ref[rows, :] = _augment_v(v_ref[0, 0, rows, :])
            kf = k_ref[0, 0, rows, :].astype(F32)
            lane = lax.broadcasted_iota(I32, kf.shape, 1)
            swapped = pltpu.roll(kf, HALF, 1)
            k2_ref[rows, :] = jnp.concatenate([jnp.where(lane < HALF, kf, swapped),
                                               jnp.where(lane < HALF, swapped, kf)], axis=1).astype(BF16)
            return carry

        lax.fori_loop(0, va_ref.shape[0] // V_BUILD_ROWS, build, 0)

    @pl.when(tvalid_ref[t0] > 0)
    def _():
        half_rows = GROUP_TILE // 2
        lane = lax.broadcasted_iota(I32, (half_rows, HEAD_DIM), 1)
        for u in range(TILES_PER_STEP):
            word = qg_ref[pl.ds(u * half_rows, half_rows), :]
            first = lax.bitcast_convert_type(lax.shift_left(word, 16), F32)
            second = lax.bitcast_convert_type(word & HI16_MASK, F32)
            n0 = pl.multiple_of(tblk_ref[t0 + u] * MOBA_BLOCK, MOBA_BLOCK)
            left, right = lane < HALF, lane >= HALF
            q = jnp.concatenate(
                [jnp.concatenate([jnp.where(left, first, 0.0), jnp.where(left, second, 0.0)], axis=1),
                 jnp.concatenate([jnp.where(right, first, 0.0), jnp.where(right, second, 0.0)], axis=1)],
                axis=0).astype(BF16)
            s = _nt_dot(q, k2_ref[pl.ds(n0, MOBA_BLOCK), :])
            m = (jnp.max(s, axis=1, keepdims=True) * (scale * LOG2_E)).astype(BF16).astype(F32)
            p = jnp.exp2(s * (scale * LOG2_E) - m).astype(BF16)
            part = jnp.dot(p, va_ref[pl.ds(n0, MOBA_BLOCK), :], preferred_element_type=F32)
            o_ref[pl.ds(u * GROUP_TILE, GROUP_TILE), :] = _pack_partial(part, m)

    @pl.when(step == pl.num_programs(1) - 1)
    def _():
        o_ref[pl.ds((TILES_PER_STEP - 1) * GROUP_TILE, GROUP_TILE), :] = _null_partial(GROUP_TILE)


def _moba_group(qg, k, v, b, tblk, tvalid, ntiles, max_tiles):
    _, H, S, Dh = k.shape
    G = H
    n_steps = max_tiles // TILES_PER_STEP
    step_rows = TILES_PER_STEP * GROUP_TILE

    def tile_idx(g, s, tblk, tvalid, nt):
        return (g * n_steps + jnp.where(s * TILES_PER_STEP < nt[g], s, n_steps - 1), 0)

    return pl.pallas_call(
        functools.partial(_moba_group_kernel, max_tiles=max_tiles),
        grid_spec=pltpu.PrefetchScalarGridSpec(
            num_scalar_prefetch=3,
            grid=(G, n_steps),
            in_specs=[pl.BlockSpec((step_rows // 2, LANES), tile_idx),
                      pl.BlockSpec((1, 1, S, Dh), lambda g, s, *_: (b, g, 0, 0)),
                      pl.BlockSpec((1, 1, S, Dh), lambda g, s, *_: (b, g, 0, 0))],
            out_specs=pl.BlockSpec((step_rows, LANES), tile_idx),
            scratch_shapes=[pltpu.VMEM((S, VA_W), BF16), pltpu.VMEM((S, 2 * Dh), BF16)],
        ),
        out_shape=jax.ShapeDtypeStruct((G * max_tiles * GROUP_TILE, LANES), I32),
        compiler_params=_cparams(("arbitrary", "arbitrary")),
        name="moba_group",
    )(tblk, tvalid, ntiles, qg, k, v)


def _moba_merge_kernel(*refs, n_skip):
    q_ref, k_ref, v_ref, sg_ref, x_ref, outw_ref = refs[n_skip:n_skip + 6]
    og_refs, out_ref, y_ref = refs[n_skip + 6:-2], refs[-2], refs[-1]
    scale = HEAD_DIM ** -0.5
    qi = lax.broadcasted_iota(I32, (MOBA_BLOCK, MOBA_BLOCK), 0)
    ki = lax.broadcasted_iota(I32, (MOBA_BLOCK, MOBA_BLOCK), 1)
    lane = lax.broadcasted_iota(I32, (MOBA_BLOCK, HEAD_DIM), 1)
    spread = jnp.where(lax.broadcasted_iota(I32, (HEAD_DIM, HEAD_DIM), 0) == L_LANE, 1.0, 0.0).astype(BF16)
    for u, t in [(u, t) for t in range(MERGE_TILES) for u in range(len(og_refs))]:
        og_ref = og_refs[u]
        rows = pl.ds(t * MOBA_BLOCK, MOBA_BLOCK)
        q = q_ref[0, u, rows, :].astype(BF16)
        s = _nt_dot(q, k_ref[0, u, rows, :])
        s = jnp.where(ki <= qi, s, NEG_INF)
        m_raw = jnp.max(s, axis=1, keepdims=True)
        p = jnp.exp2((s - m_raw) * (scale * LOG2_E)).astype(BF16)
        own = jnp.dot(p, _augment_v(v_ref[0, u, rows, :]), preferred_element_type=F32)
        parts = [_unpack_partial(og_ref[r, rows, :]) for r in range(MOBA_TOPK)]
        ms = [jnp.dot(jnp.where(lane == L_LANE, hi, 0.0).astype(BF16), spread, preferred_element_type=F32)
              for _, hi in parts]
        m_own = jnp.broadcast_to(m_raw * (scale * LOG2_E), lane.shape)
        m_all = m_own
        for m in ms:
            m_all = jnp.maximum(m_all, m)
        w_own = jnp.exp2(m_own - m_all)
        tot_lo = w_own * own[:, :HEAD_DIM]
        tot_hi = w_own * own[:, HEAD_DIM:]
        for (lo, hi), m in zip(parts, ms):
            w = jnp.exp2(m - m_all)
            tot_lo = tot_lo + w * lo
            tot_hi = tot_hi + w * hi
        acc = jnp.where(lane < HALF, tot_lo, pltpu.roll(tot_hi, HALF, 1))
        cols = pl.ds(u * HEAD_DIM, HEAD_DIM)
        y_ref[rows, cols] = ((acc / tot_lo[:, L_LANE:L_LANE + 1]) * sg_ref[rows, cols]).astype(BF16)

    out_ref[...] = x_ref[...] + jnp.dot(y_ref[...], outw_ref[...], preferred_element_type=F32)


def _moba_merge(q, k, v, sg, og, x2d, out_w, out_batch, n_out_batch, o_prev):
    _, H, S, Dh = q.shape
    D = x2d.shape[1]
    b = 0
    hps = HEADS_PER_STEP
    assert hps == H
    rows = MERGE_TILES * MOBA_BLOCK
    ns = S // rows
    og_spec = lambda u: pl.BlockSpec((MOBA_TOPK, rows, LANES),
                                     lambda _, hp, j: (0, (hp * hps + u) * ns + j, 0))
    prev_specs = [] if o_prev is None else [pl.BlockSpec(memory_space=pl.ANY)]
    prev_args = [] if o_prev is None else [o_prev]
    return pl.pallas_call(
        functools.partial(_moba_merge_kernel, n_skip=len(prev_args)),
        grid=(1, H // hps, ns),
        in_specs=prev_specs
                 + [pl.BlockSpec((1, hps, rows, Dh), lambda _, hp, j: (b, hp, j, 0)),
                    pl.BlockSpec((1, hps, rows, Dh), lambda _, hp, j: (b, hp, j, 0)),
                    pl.BlockSpec((1, hps, rows, Dh), lambda _, hp, j: (b, hp, j, 0)),
                    pl.BlockSpec((rows, hps * Dh), lambda _, hp, j: (b * ns + j, hp)),
                    pl.BlockSpec((rows, D), lambda _, hp, j: (b * ns + j, 0)),
                    pl.BlockSpec((H * Dh, D), lambda _, hp, j: (0, 0))]
                 + [og_spec(u) for u in range(hps)],
        out_specs=pl.BlockSpec((rows, D), lambda _, hp, j: (out_batch * ns + j, 0)),
        out_shape=jax.ShapeDtypeStruct((n_out_batch * S, D), F32),
        scratch_shapes=[pltpu.VMEM((rows, H * Dh), BF16)],
        input_output_aliases={} if o_prev is None else {0: 0},
        compiler_params=_cparams(("arbitrary", "arbitrary", "arbitrary")),
        name="moba_merge",
    )(*prev_args, q, k, v, sg, x2d, out_w, *([og] * hps))


def _moba_layer(qp, q16, k, v, kmean, sg, x2d, out_w, out_batch, n_out_batch, o_prev):
    _, H, S, Dh = q16.shape
    nb = S // MOBA_BLOCK
    max_tiles = (MOBA_TOPK * S) // GROUP_TILE + nb + 1
    max_tiles = -(-max_tiles // TILES_PER_STEP) * TILES_PER_STEP
    slab_rows = max_tiles * GROUP_TILE

    q_rows = qp.reshape(H * nb, IDX_CHUNK, SC_HALVES, HALF)
    rt, counts = _moba_route(q16, kmean, 0)
    pos, qpos, tblk, tvalid, ntiles = _moba_layout(rt, counts.reshape(-1), nb, max_tiles)
    qg = _sc_scatter_rows(q_rows, 0, qpos, H * slab_rows)
    qg = qg.reshape(H * slab_rows // 2, LANES)
    parts = _moba_group(qg, k, v, 0, tblk, tvalid, ntiles, max_tiles)
    og = _sc_gather_rows(parts, pos)
    return _moba_merge(q16, k, v, sg, og, x2d, out_w, out_batch, n_out_batch, o_prev)


def _rope_tables(S):
    inv = np.float32(ROPE_THETA) ** (-np.arange(HALF, dtype=np.float32) / np.float32(HALF))
    ang = np.arange(S, dtype=np.float32)[:, None] * inv[None, :].astype(np.float32)
    cos, sin = np.cos(ang).astype(np.float32), np.sin(ang).astype(np.float32)
    return (jnp.asarray(np.concatenate([cos, cos], axis=-1)),
            jnp.asarray(np.concatenate([-sin, sin], axis=-1)))


def kernel(x, a_norm, a_in_w, a_conv_w, a_conv_b, a_r_w, a_r_b, a_i_w, a_i_b, a_lambda, a_out_w,
           kv_norm, kv_w, k_norm, b_norm, b_in_w, q_norm, b_out_w):
    B, S, D = x.shape
    assert S % ROW_TILE == 0 and ROW_TILE % MOBA_BLOCK == 0
    assert D == N_RG_BLOCKS * RG_BW == N_HEADS * HEAD_DIM
    assert (B * N_HEADS * S) % (IDX_CHUNK * SC_WORKERS) == 0
    n_a, n_b = a_in_w.shape[0], b_in_w.shape[0]
    assert n_a >= 1 and n_b >= 1
    x2d = x.reshape(B * S, D)
    row = lambda v: v.reshape(1, -1)
    cosf, sinf = _rope_tables(S)

    out = None
    for b in range(B):
        xs, b_in = x2d, b
        for l in range(n_a):
            c_rnn = a_in_w.shape[2] // 2
            wri = (0.5 * jnp.concatenate([a_r_w[l], a_i_w[l]], axis=-1)).astype(BF16)
            in_w = jnp.concatenate([a_in_w[l][:, :c_rnn], 0.5 * a_in_w[l][:, c_rnn:]], axis=-1).astype(BF16)
            xs = _hawk_layer(xs, b_in, S, row(a_norm[l]), in_w, a_conv_w[l], row(a_conv_b[l]),
                             wri, row(0.5 * a_r_b[l]), row(0.5 * a_i_b[l]), row(a_lambda[l]),
                             a_out_w[l].astype(BF16))
            b_in = 0
        for jl in range(n_b):
            q_args = (row(b_norm[jl]), b_in_w[jl].astype(BF16), row(q_norm[jl]))
            if jl == 0:
                k, v, kmean, qp, q16, sg = _kvq_proj(xs, 1, S, row(kv_norm), kv_w.astype(BF16), row(k_norm),
                                                    *q_args, cosf, sinf)
                kmean = jnp.transpose(kmean, (0, 2, 1, 3))
            else:
                qp, q16, sg = _q_proj(xs, 1, S, *q_args, cosf, sinf)
            w_out = b_out_w[jl].astype(BF16)
            if jl == n_b - 1:
                out = xs = _moba_layer(qp, q16, k, v, kmean, sg, xs, w_out, b, B, out)
            else:
                xs = _moba_layer(qp, q16, k, v, kmean, sg, xs, w_out, 0, 1, None)
    return out.reshape(B, S, D)
```

```python
import functools

import jax
import jax.numpy as jnp
import numpy as np
from jax import lax
from jax.experimental import pallas as pl
from jax.experimental.pallas import tpu as pltpu
from jax.experimental.pallas import tpu_sc as plsc

N_HEADS = 8
HEAD_DIM = 128
MOBA_BLOCK = 256
MOBA_TOPK = 3
CONV_WIDTH = 4
N_RG_BLOCKS = 8
RG_BW = 128
RG_C = 8.0
ROPE_THETA = 10000.0
EPS = 1e-6
NEG_INF = -1e30
LOG2_E = 1.4426950408889634
TINY = 1e-30

SUBLANES = 8
LANES = 128
ROW_TILE = 512
SUB_ROWS = 128
VMEM_LIMIT = 56 * 1024 * 1024

GROUP_TILE = MOBA_BLOCK
TILES_PER_STEP = 32
V_BUILD_ROWS = 1024
HEADS_PER_STEP = 8
MERGE_TILES = 2
ROUTE_TILES = 16
ROUTE_BLK_SHIFT = 16
ROUTE_BLK_UNIT = 1 << ROUTE_BLK_SHIFT
HALF = HEAD_DIM // 2
VA_W = 2 * HEAD_DIM
L_LANE = HALF
HI16_MASK = -65536
BF16_HALF_ULP = 0x8000
NEG_INF_BITS = int(np.float32(NEG_INF).view(np.int32))

SC_CORES = 2
SC_SUBCORES = 16
SC_WORKERS = SC_CORES * SC_SUBCORES
IDX_CHUNK = 128
SC_HALVES = MOBA_BLOCK // IDX_CHUNK

F32 = jnp.float32
BF16 = jnp.bfloat16
I32 = jnp.int32


def _cparams(sem):
    return pltpu.CompilerParams(dimension_semantics=sem, vmem_limit_bytes=VMEM_LIMIT)


def _rms_norm(x, g):
    ms = jnp.mean(x * x, axis=-1, keepdims=True)
    return (x * lax.rsqrt(ms + EPS)) * g


def _silu(x):
    hx = 0.5 * x
    return hx * jnp.tanh(hx) + hx


def _nt_dot(a, b):
    return lax.dot_general(a, b, (((1,), (1,)), ((), ())), preferred_element_type=F32)


def _head_norm_rope(z, g, cosf, sinf):
    outs = []
    for h in range(N_HEADS):
        zh = z[:, h * HEAD_DIM:(h + 1) * HEAD_DIM]
        zh = _rms_norm(zh, g)
        outs.append(zh * cosf + pltpu.roll(zh, HEAD_DIM // 2, 1) * sinf)
    return outs


def _hawk_kernel(x_ref, g_ref, inw_ref, cw_ref, cb_ref, wri_ref, rb_ref, ib_ref, lam_ref, outw_ref,
                 o_ref, gate_ref, xpad_ref, a_ref, b_ref, hc_ref):
    tm, C = gate_ref.shape
    s = pl.program_id(1)

    @pl.when(s == 0)
    def _():
        xpad_ref[pl.ds(0, SUBLANES), :] = jnp.zeros((SUBLANES, xpad_ref.shape[1]), F32)
        hc_ref[...] = jnp.zeros_like(hc_ref)

    h = _rms_norm(x_ref[...], g_ref[...]).astype(BF16)
    gate_ref[...] = jnp.dot(h, inw_ref[:, C:], preferred_element_type=F32)

    xpad_ref[pl.ds(SUBLANES, tm), :] = jnp.dot(h, inw_ref[:, :C], preferred_element_type=F32)
    cw = cw_ref[...]
    xc = xpad_ref[pl.ds(SUBLANES - (CONV_WIDTH - 1), tm), :] * cw[0:1, :]
    for k in range(1, CONV_WIDTH):
        xc = xc + xpad_ref[pl.ds(SUBLANES - (CONV_WIDTH - 1) + k, tm), :] * cw[k:k + 1, :]
    xc = xc + cb_ref[...]
    xpad_ref[pl.ds(0, SUBLANES), :] = xpad_ref[pl.ds(tm, SUBLANES), :]

    lam = lam_ref[...]
    sp = jnp.maximum(-lam, 0.0) + jnp.log1p(jnp.exp(-jnp.abs(lam)))
    half_log2a = (-0.5 * RG_C * LOG2_E) * sp
    xc16 = xc.astype(BF16)
    for g in range(N_RG_BLOCKS):
        lo = g * RG_BW
        z = jnp.dot(xc16[:, lo:lo + RG_BW], wri_ref[g], preferred_element_type=F32)
        tr = jnp.tanh(z[:, :RG_BW] + rb_ref[:, lo:lo + RG_BW])
        ti = jnp.tanh(z[:, RG_BW:] + ib_ref[:, lo:lo + RG_BW])
        a = jnp.exp2(tr * half_log2a[:, lo:lo + RG_BW] + half_log2a[:, lo:lo + RG_BW])
        om = 1.0 - a * a
        mult = om * lax.rsqrt(jnp.maximum(om, TINY))
        hx = 0.5 * xc[:, lo:lo + RG_BW]
        a_ref[:, lo:lo + RG_BW] = a
        b_ref[:, lo:lo + RG_BW] = mult * (hx * ti + hx)

    row = lax.broadcasted_iota(I32, (SUBLANES, a_ref.shape[1]), 0)

    def scan_body(c, hc):
        r0 = pl.multiple_of(c * SUBLANES, SUBLANES)
        a = a_ref[pl.ds(r0, SUBLANES), :]
        b = b_ref[pl.ds(r0, SUBLANES), :]
        for k in (1, 2, 4):
            keep = row >= k
            b = jnp.where(keep, a * pltpu.roll(b, k, 0) + b, b)
            a = jnp.where(keep, a * pltpu.roll(a, k, 0), a)
        h = a * hc + b
        b_ref[pl.ds(r0, SUBLANES), :] = h
        return h[SUBLANES - 1:SUBLANES, :]

    hc = lax.fori_loop(0, tm // SUBLANES, scan_body, hc_ref[0:1, :])
    hc_ref[0:1, :] = hc

    gh = gate_ref[...]
    y = (b_ref[...] * (gh * jnp.tanh(gh) + gh)).astype(BF16)
    o_ref[...] = x_ref[...] + jnp.dot(y, outw_ref[...], preferred_element_type=F32)


def _hawk_layer(x2d, b_in, S, g, in_w, cw, cb, wri, rb, ib, lam, out_w):
    D = x2d.shape[1]
    C = in_w.shape[1] // 2
    tm = ROW_TILE
    ns = S // tm
    vec = lambda n: pl.BlockSpec((1, n), lambda b, s: (0, 0))
    return pl.pallas_call(
        _hawk_kernel,
        grid=(1, ns),
        in_specs=[pl.BlockSpec((tm, D), lambda b, s: (b_in * ns + s, 0)),
                  vec(D),
                  pl.BlockSpec((D, 2 * C), lambda b, s: (0, 0)),
                  pl.BlockSpec((CONV_WIDTH, C), lambda b, s: (0, 0)),
                  vec(C),
                  pl.BlockSpec((N_RG_BLOCKS, RG_BW, 2 * RG_BW), lambda b, s: (0, 0, 0)),
                  vec(C), vec(C), vec(C),
                  pl.BlockSpec((C, D), lambda b, s: (0, 0))],
        out_specs=pl.BlockSpec((tm, D), lambda b, s: (s, 0)),
        out_shape=jax.ShapeDtypeStruct((S, D), F32),
        scratch_shapes=[pltpu.VMEM((tm, C), F32),
                        pltpu.VMEM((tm + SUBLANES, C), F32),
                        pltpu.VMEM((tm, C), F32),
                        pltpu.VMEM((tm, C), F32),
                        pltpu.VMEM((SUBLANES, C), F32)],
        compiler_params=_cparams(("arbitrary", "arbitrary")),
        name="hawk_layer",
    )(x2d, g, in_w, cw, cb, wri, rb, ib, lam, out_w)


def _emit_q(u, qg, cosf, sinf, t, qp_ref, q16_ref, sg_ref):
    assert SUB_ROWS == IDX_CHUNK
    d_attn = N_HEADS * HEAD_DIM
    rows = pl.ds(t * SUB_ROWS, SUB_ROWS)
    prow = pl.ds((t // 2) * SUB_ROWS, SUB_ROWS)
    qs = _head_norm_rope(u[:, :d_attn], qg, cosf, sinf)
    for hd in range(N_HEADS):
        q16 = qs[hd].astype(BF16)
        q16_ref[0, hd, rows, :] = q16
        r = _bits(q16.astype(F32))
        swapped = pltpu.roll(r, HALF, 1)
        if t % 2 == 0:
            w = (swapped & HI16_MASK) | lax.shift_right_logical(r, 16)
            qp_ref[0, hd, prow, pl.ds(0, HALF)] = w[:, :HALF]
        else:
            w = (r & HI16_MASK) | lax.shift_right_logical(swapped, 16)
            qp_ref[0, hd, prow, pl.ds(HALF, HALF)] = w[:, HALF:]
    sg_ref[rows, :] = _silu(u[:, d_attn:])


def _kvq_proj_kernel(x_ref, gkv_ref, wkv_ref, kg_ref, gq_ref, wq_ref, qg_ref, cos_ref, sin_ref,
                     k_ref, v_ref, km_ref, q_ref, q16_ref, sg_ref):
    tm = x_ref.shape[0]
    d_attn = N_HEADS * HEAD_DIM
    ksum = {}
    for t in range(tm // SUB_ROWS):
        rows = pl.ds(t * SUB_ROWS, SUB_ROWS)
        x = x_ref[rows, :]
        xn = x * lax.rsqrt(jnp.mean(x * x, axis=-1, keepdims=True) + EPS)
        kv = jnp.dot((xn * gkv_ref[...]).astype(BF16), wkv_ref[...], preferred_element_type=F32)
        u = jnp.dot((xn * gq_ref[...]).astype(BF16), wq_ref[...], preferred_element_type=F32)
        cosf, sinf = cos_ref[rows, :], sin_ref[rows, :]
        ks = _head_norm_rope(kv[:, :d_attn], kg_ref[...], cosf, sinf)
        for hd in range(N_HEADS):
            k_ref[0, hd, rows, :] = ks[hd].astype(BF16)
            v_ref[0, hd, rows, :] = kv[:, d_attn + hd * HEAD_DIM:d_attn + (hd + 1) * HEAD_DIM].astype(BF16)
            key = (hd, (t * SUB_ROWS) // MOBA_BLOCK)
            part = jnp.sum(ks[hd], axis=0, keepdims=True)
            ksum[key] = part if key not in ksum else ksum[key] + part
        _emit_q(u, qg_ref[...], cosf, sinf, t, q_ref, q16_ref, sg_ref)
    for (hd, r), total in ksum.items():
        km_ref[0, r, pl.ds(hd, 1), :] = total * (1.0 / MOBA_BLOCK)


def _kvq_proj(x2d, B, S, gkv, wkv, kg, gq, wq, qg, cosf, sinf):
    T, D = x2d.shape
    tm = ROW_TILE
    ns = S // tm
    nb = S // MOBA_BLOCK
    rpt = tm // MOBA_BLOCK
    d_attn = N_HEADS * HEAD_DIM
    vec = lambda n: pl.BlockSpec((1, n), lambda b, s: (0, 0))
    wspec = pl.BlockSpec((D, 2 * d_attn), lambda b, s: (0, 0))
    rope = pl.BlockSpec((tm, HEAD_DIM), lambda b, s: (s, 0))
    heads = pl.BlockSpec((1, N_HEADS, tm, HEAD_DIM), lambda b, s: (b, 0, s, 0))
    hshape = lambda dt: jax.ShapeDtypeStruct((B, N_HEADS, S, HEAD_DIM), dt)
    return pl.pallas_call(
        _kvq_proj_kernel,
        grid=(B, ns),
        in_specs=[pl.BlockSpec((tm, D), lambda b, s: (b * ns + s, 0)),
                  vec(D), wspec, vec(HEAD_DIM), vec(D), wspec, vec(HEAD_DIM), rope, rope],
        out_specs=[heads, heads,
                   pl.BlockSpec((1, rpt, N_HEADS, HEAD_DIM), lambda b, s: (b, s, 0, 0)),
                   pl.BlockSpec((1, N_HEADS, tm // 2, LANES), lambda b, s: (b, 0, s, 0)), heads,
                   pl.BlockSpec((tm, d_attn), lambda b, s: (b * ns + s, 0))],
        out_shape=[hshape(BF16), hshape(BF16),
                   jax.ShapeDtypeStruct((B, nb, N_HEADS, HEAD_DIM), F32),
                   jax.ShapeDtypeStruct((B, N_HEADS, S // 2, LANES), I32), hshape(BF16),
                   jax.ShapeDtypeStruct((T, d_attn), F32)],
        compiler_params=_cparams(("arbitrary", "arbitrary")),
        name="kvq_proj",
    )(x2d, gkv, wkv, kg, gq, wq, qg, cosf, sinf)


def _q_proj_kernel(x_ref, g_ref, w_ref, qg_ref, cos_ref, sin_ref, q_ref, q16_ref, sg_ref):
    for t in range(x_ref.shape[0] // SUB_ROWS):
        rows = pl.ds(t * SUB_ROWS, SUB_ROWS)
        h = _rms_norm(x_ref[rows, :], g_ref[...]).astype(BF16)
        u = jnp.dot(h, w_ref[...], preferred_element_type=F32)
        _emit_q(u, qg_ref[...], cos_ref[rows, :], sin_ref[rows, :], t, q_ref, q16_ref, sg_ref)


def _q_proj(x2d, B, S, g, w, qg, cosf, sinf):
    T, D = x2d.shape
    tm = ROW_TILE
    ns = S // tm
    d_attn = N_HEADS * HEAD_DIM
    heads = pl.BlockSpec((1, N_HEADS, tm, HEAD_DIM), lambda b, s: (b, 0, s, 0))
    return pl.pallas_call(
        _q_proj_kernel,
        grid=(B, ns),
        in_specs=[pl.BlockSpec((tm, D), lambda b, s: (b * ns + s, 0)),
                  pl.BlockSpec((1, D), lambda b, s: (0, 0)),
                  pl.BlockSpec((D, 2 * d_attn), lambda b, s: (0, 0)),
                  pl.BlockSpec((1, HEAD_DIM), lambda b, s: (0, 0)),
                  pl.BlockSpec((tm, HEAD_DIM), lambda b, s: (s, 0)),
                  pl.BlockSpec((tm, HEAD_DIM), lambda b, s: (s, 0))],
        out_specs=[pl.BlockSpec((1, N_HEADS, tm // 2, LANES), lambda b, s: (b, 0, s, 0)), heads,
                   pl.BlockSpec((tm, d_attn), lambda b, s: (b * ns + s, 0))],
        out_shape=[jax.ShapeDtypeStruct((B, N_HEADS, S // 2, LANES), I32),
                   jax.ShapeDtypeStruct((B, N_HEADS, S, HEAD_DIM), BF16),
                   jax.ShapeDtypeStruct((T, d_attn), F32)],
        compiler_params=_cparams(("arbitrary", "arbitrary")),
        name="q_proj",
    )(x2d, g, w, qg, cosf, sinf)


def _moba_route_kernel(q_ref, km_ref, rt_ref, cnt_ref, run_ref):
    step = pl.program_id(2)
    nb = km_ref.shape[2]

    @pl.when(step == 0)
    def _():
        run_ref[...] = jnp.zeros_like(run_ref)

    km = km_ref[0, 0].astype(BF16)
    blk = lax.broadcasted_iota(I32, (nb, MOBA_BLOCK), 0)
    qa = lax.broadcasted_iota(I32, (MOBA_BLOCK, MOBA_BLOCK), 0)
    qb = lax.broadcasted_iota(I32, (MOBA_BLOCK, MOBA_BLOCK), 1)
    earlier = jnp.where(qa < qb, 1.0, 0.0).astype(BF16)
    run = run_ref[...]
    for u in range(ROUTE_TILES):
        j = step * ROUTE_TILES + u
        g = _nt_dot(km, q_ref[0, 0, pl.ds(u * MOBA_BLOCK, MOBA_BLOCK), :].astype(BF16))
        g = jnp.where(blk < j, g, -jnp.inf)
        hits, ids = [], []
        for _ in range(MOBA_TOPK):
            mx = jnp.max(g, axis=0, keepdims=True)
            idx = jnp.min(jnp.where(g == mx, blk, nb), axis=0, keepdims=True)
            hit = blk == idx
            ok = mx > -jnp.inf
            hits.append(jnp.where(hit & ok, 1.0, 0.0))
            ids.append(jnp.where(ok, idx, -1))
            g = jnp.where(hit, -jnp.inf, g)
        sel = hits[0] + hits[1] + hits[2]

        rank = jnp.dot(sel.astype(BF16), earlier, preferred_element_type=F32)
        dest = run + rank
        run = run + jnp.sum(sel, axis=1, keepdims=True)
        for r in range(MOBA_TOPK):
            loc = jnp.sum(hits[r] * dest, axis=0, keepdims=True).astype(I32)
            rt_ref[u, pl.ds(r, 1), :] = jnp.where(ids[r] >= 0, ids[r] * ROUTE_BLK_UNIT + loc, -1)
        rt_ref[u, pl.ds(MOBA_TOPK, SUBLANES - MOBA_TOPK), :] = jnp.full(
            (SUBLANES - MOBA_TOPK, MOBA_BLOCK), -1, I32)
    run_ref[...] = run
    cnt_ref[0] = run.astype(I32)


def _moba_route(q, kmean, b):
    _, H, S, Dh = q.shape
    nb = S // MOBA_BLOCK
    rows = ROUTE_TILES * MOBA_BLOCK
    nsteps = nb // ROUTE_TILES
    return pl.pallas_call(
        _moba_route_kernel,
        grid=(1, H, nsteps),
        in_specs=[pl.BlockSpec((1, 1, rows, Dh), lambda _, h, j: (b, h, j, 0)),
                  pl.BlockSpec((1, 1, nb, Dh), lambda _, h, j: (b, h, 0, 0))],
        out_specs=[pl.BlockSpec((ROUTE_TILES, SUBLANES, MOBA_BLOCK), lambda _, h, j: (h * nsteps + j, 0, 0)),
                   pl.BlockSpec((1, nb, 1), lambda _, h, j: (h, 0, 0))],
        out_shape=[jax.ShapeDtypeStruct((H * nb, SUBLANES, MOBA_BLOCK), I32),
                   jax.ShapeDtypeStruct((H, nb, 1), I32)],
        scratch_shapes=[pltpu.VMEM((nb, 1), F32)],
        compiler_params=_cparams(("arbitrary", "arbitrary", "arbitrary")),
        name="moba_route",
    )(q, kmean)


def _moba_layout_kernel(cnt_ref, rt_ref, pos_ref, qpos_ref, tblk_ref, tvalid_ref, nt_ref, base_ref, *, max_tiles):
    g = pl.program_id(0)
    nb = rt_ref.shape[0]
    slab_rows = max_tiles * GROUP_TILE
    null_row0 = (max_tiles - 1) * GROUP_TILE

    def per_block(n, first_tile):
        count = cnt_ref[g * nb + n]
        tiles = (count + (GROUP_TILE - 1)) // GROUP_TILE
        base_ref[n] = first_tile * GROUP_TILE

        def per_tile(i, carry):
            tblk_ref[g * max_tiles + first_tile + i] = n
            tvalid_ref[g * max_tiles + first_tile + i] = jnp.minimum(count - i * GROUP_TILE, GROUP_TILE)
            return carry

        lax.fori_loop(0, tiles, per_tile, 0)
        return first_tile + tiles

    used = lax.fori_loop(0, nb, per_block, 0)
    nt_ref[g] = used

    def unused_tile(t, carry):
        tblk_ref[g * max_tiles + t] = nb - 1
        tvalid_ref[g * max_tiles + t] = 0
        return carry

    lax.fori_loop(used, max_tiles, unused_tile, 0)

    code = rt_ref[...].reshape(nb * SUBLANES, MOBA_BLOCK)
    blk = lax.shift_right_arithmetic(code, ROUTE_BLK_SHIFT)
    base = jnp.zeros_like(code)
    for n in range(nb):
        base = jnp.where(blk == n, base_ref[n], base)
    lane = lax.broadcasted_iota(I32, code.shape, 1)
    row = jnp.where(code >= 0, base + (code & (ROUTE_BLK_UNIT - 1)), null_row0 + lane) + g * slab_rows
    pos_ref[...] = row.reshape(pos_ref.shape)
    p = row & (GROUP_TILE - 1)
    assert GROUP_TILE == 2 * IDX_CHUNK
    slot = ((p & (IDX_CHUNK - 1)) << 1) | lax.shift_right_logical(p, IDX_CHUNK.bit_length() - 1)
    qpos_ref[...] = ((row - p) | slot).reshape(qpos_ref.shape)


def _moba_layout(rt, counts, nb, max_tiles):
    G = rt.shape[0] // nb
    smem = pl.BlockSpec(memory_space=pltpu.SMEM)
    return pl.pallas_call(
        functools.partial(_moba_layout_kernel, max_tiles=max_tiles),
        grid_spec=pltpu.PrefetchScalarGridSpec(
            num_scalar_prefetch=1,
            grid=(G,),
            in_specs=[pl.BlockSpec((nb, SUBLANES, MOBA_BLOCK), lambda g, cnt: (g, 0, 0))],
            out_specs=[pl.BlockSpec((nb, SUBLANES, MOBA_BLOCK), lambda g, cnt: (g, 0, 0)),
                       pl.BlockSpec((nb, SUBLANES, MOBA_BLOCK), lambda g, cnt: (g, 0, 0)), smem, smem, smem],
            scratch_shapes=[pltpu.SMEM((nb,), I32)],
        ),
        out_shape=[jax.ShapeDtypeStruct((G * nb, SUBLANES, MOBA_BLOCK), I32),
                   jax.ShapeDtypeStruct((G * nb, SUBLANES, MOBA_BLOCK), I32),
                   jax.ShapeDtypeStruct((G * max_tiles,), I32),
                   jax.ShapeDtypeStruct((G * max_tiles,), I32),
                   jax.ShapeDtypeStruct((G,), I32)],
        compiler_params=_cparams(("arbitrary",)),
        name="moba_layout",
    )(counts, rt)


def _sc_mesh():
    return plsc.VectorSubcoreMesh(core_axis_name="c", subcore_axis_name="s")


def _sc_load_indices(pos_hbm, idx_v, t0, n_tiles):
    for c in range(SC_HALVES):
        pltpu.sync_copy(pos_hbm.at[pl.ds(t0, n_tiles), pl.ds(0, MOBA_TOPK), pl.ds(c * IDX_CHUNK, IDX_CHUNK)],
                        idx_v.at[c])


def _sc_scatter_rows(src, src_tile0, pos, n_out_rows):
    D = src.shape[-1]
    n_tiles = pos.shape[0]
    per_w = n_tiles // SC_WORKERS
    assert per_w * SC_WORKERS == n_tiles and src.shape[1:3] == (IDX_CHUNK, SC_HALVES)

    @functools.partial(
        pl.kernel, mesh=_sc_mesh(),
        out_type=jax.ShapeDtypeStruct((n_out_rows, D), src.dtype),
        scratch_types=[pltpu.VMEM((SC_HALVES, per_w, MOBA_TOPK, IDX_CHUNK), I32),
                       pltpu.VMEM((SC_HALVES, IDX_CHUNK, D), src.dtype),
                       pltpu.SemaphoreType.DMA((SC_HALVES,)),
                       pltpu.SemaphoreType.DMA((SC_HALVES,))],
        compiler_params=pltpu.CompilerParams(use_tc_tiling_on_sc=False),
        name="sc_scatter_rows",
    )
    def k(src_hbm, pos_hbm, out_hbm, idx_v, rows_v, load_sem, scat_sem):
        t0 = (lax.axis_index("s") * SC_CORES + lax.axis_index("c")) * per_w
        _sc_load_indices(pos_hbm, idx_v, t0, per_w)

        def load(i, c):
            return pltpu.make_async_copy(src_hbm.at[src_tile0 + t0 + i, :, c, :], rows_v.at[c], load_sem.at[c])

        def scatters(i, c):
            return [pltpu.make_async_copy(rows_v.at[c], out_hbm.at[idx_v.at[c, i, r]], scat_sem.at[c])
                    for r in range(MOBA_TOPK)]

        load(0, 0).start()

        @pl.loop(0, per_w)
        def _(i):
            for c in range(SC_HALVES):
                pi, pc = (i, 0) if c else (i - 1, 1)
                ni, nc = (i + 1, 0) if c else (i, 1)
                load(i, c).wait()
                for cp in scatters(i, c):
                    cp.start()

                @pl.when(pi >= 0)
                def _():
                    for cp in scatters(pi, pc):
                        cp.wait()

                @pl.when(ni < per_w)
                def _():
                    load(ni, nc).start()

        for cp in scatters(per_w - 1, SC_HALVES - 1):
            cp.wait()

    return k(src, pos)


def _sc_gather_rows(table, pos):
    R, D = table.shape
    n_tiles = pos.shape[0]
    N = n_tiles * MOBA_BLOCK
    per_w = n_tiles // SC_WORKERS
    assert per_w * SC_WORKERS == n_tiles

    items = SC_HALVES * MOBA_TOPK
    n_buf = 2
    assert items % n_buf == 0

    @functools.partial(
        pl.kernel, mesh=_sc_mesh(),
        out_type=jax.ShapeDtypeStruct((MOBA_TOPK, N, D), table.dtype),
        scratch_types=[pltpu.VMEM((SC_HALVES, per_w, MOBA_TOPK, IDX_CHUNK), I32),
                       pltpu.VMEM((n_buf, IDX_CHUNK, D), table.dtype),
                       pltpu.SemaphoreType.DMA((n_buf,)),
                       pltpu.SemaphoreType.DMA((n_buf,))],
        name="sc_gather_rows",
    )
    def k(table_hbm, pos_hbm, out_hbm, idx_v, rows_v, gat_sem, put_sem):
        t0 = (lax.axis_index("s") * SC_CORES + lax.axis_index("c")) * per_w
        _sc_load_indices(pos_hbm, idx_v, t0, per_w)

        def gather(i, j):
            c, r = divmod(j, MOBA_TOPK)
            return pltpu.make_async_copy(table_hbm.at[idx_v.at[c, i, r]], rows_v.at[j % n_buf], gat_sem.at[j % n_buf])

        def put(i, j):
            c, r = divmod(j, MOBA_TOPK)
            row0 = pl.multiple_of((t0 + i) * MOBA_BLOCK + c * IDX_CHUNK, IDX_CHUNK)
            return pltpu.make_async_copy(rows_v.at[j % n_buf], out_hbm.at[r, pl.ds(row0, IDX_CHUNK)],
                                         put_sem.at[j % n_buf])

        gather(0, 0).start()

        @pl.loop(0, per_w)
        def _(i):
            for j in range(items):
                pi, pj = (i, j - 1) if j else (i - 1, items - 1)
                ni, nj = (i, j + 1) if j + 1 < items else (i + 1, 0)
                gather(i, j).wait()
                put(i, j).start()

                @pl.when(pi >= 0)
                def _():
                    put(pi, pj).wait()

                @pl.when(ni < per_w)
                def _():
                    gather(ni, nj).start()

        put(per_w - 1, items - 1).wait()

    return k(table, pos)


def _bits(x):
    return lax.bitcast_convert_type(x, I32)


def _pack_partial(part, m):
    lane = lax.broadcasted_iota(I32, (part.shape[0], HEAD_DIM), 1)
    lo = _bits(part[:, :HEAD_DIM]) + BF16_HALF_ULP
    hi = _bits(jnp.where(lane == L_LANE, m, part[:, HEAD_DIM:])) + BF16_HALF_ULP
    return (hi & HI16_MASK) | lax.shift_right_logical(lo, 16)


def _null_partial(rows):
    lane = lax.broadcasted_iota(I32, (rows, HEAD_DIM), 1)
    return jnp.where(lane == L_LANE, NEG_INF_BITS & HI16_MASK, 0)


def _unpack_partial(word):
    lo = lax.bitcast_convert_type(lax.shift_left(word, 16), F32)
    hi = lax.bitcast_convert_type(word & HI16_MASK, F32)
    return lo, hi


def _augment_v(v):
    vf = v.astype(F32)
    lane = lax.broadcasted_iota(I32, vf.shape, 1)
    lo = jnp.where(lane < HALF, vf, jnp.where(lane == L_LANE, 1.0, 0.0))
    hi = jnp.where(lane < HALF, pltpu.roll(vf, HALF, 1), 0.0)
    return jnp.concatenate([lo, hi], axis=1).astype(BF16)


def _moba_group_kernel(tblk_ref, tvalid_ref, nt_ref, qg_ref, k_ref, v_ref, o_ref, va_ref, k2_ref, *, max_tiles):
    g, step = pl.program_id(0), pl.program_id(1)
    t0 = g * max_tiles + step * TILES_PER_STEP
    scale = HEAD_DIM ** -0.5

    @pl.when(step == 0)
    def _():
        def build(c, carry):
            rows = pl.ds(pl.multiple_of(c * V_BUILD_ROWS, V_BUILD_ROWS), V_BUILD_ROWS)
            va_ref[rows, :] = _augment_v(v_ref[0, 0, rows, :])
            kf = k_ref[0, 0, rows, :].astype(F32)
            lane = lax.broadcasted_iota(I32, kf.shape, 1)
            swapped = pltpu.roll(kf, HALF, 1)
            k2_ref[rows, :] = jnp.concatenate([jnp.where(lane < HALF, kf, swapped),
                                               jnp.where(lane < HALF, swapped, kf)], axis=1).astype(BF16)
            return carry

        lax.fori_loop(0, va_ref.shape[0] // V_BUILD_ROWS, build, 0)

    @pl.when(tvalid_ref[t0] > 0)
    def _():
        half_rows = GROUP_TILE // 2
        lane = lax.broadcasted_iota(I32, (half_rows, HEAD_DIM), 1)
        for u in range(TILES_PER_STEP):
            word = qg_ref[pl.ds(u * half_rows, half_rows), :]
            first = lax.bitcast_convert_type(lax.shift_left(word, 16), F32)
            second = lax.bitcast_convert_type(word & HI16_MASK, F32)
            n0 = pl.multiple_of(tblk_ref[t0 + u] * MOBA_BLOCK, MOBA_BLOCK)
            left, right = lane < HALF, lane >= HALF
            q = jnp.concatenate(
                [jnp.concatenate([jnp.where(left, first, 0.0), jnp.where(left, second, 0.0)], axis=1),
                 jnp.concatenate([jnp.where(right, first, 0.0), jnp.where(right, second, 0.0)], axis=1)],
                axis=0).astype(BF16)
            s = _nt_dot(q, k2_ref[pl.ds(n0, MOBA_BLOCK), :])
            m = (jnp.max(s, axis=1, keepdims=True) * (scale * LOG2_E)).astype(BF16).astype(F32)
            p = jnp.exp2(s * (scale * LOG2_E) - m).astype(BF16)
            part = jnp.dot(p, va_ref[pl.ds(n0, MOBA_BLOCK), :], preferred_element_type=F32)
            o_ref[pl.ds(u * GROUP_TILE, GROUP_TILE), :] = _pack_partial(part, m)

    @pl.when(step == pl.num_programs(1) - 1)
    def _():
        o_ref[pl.ds((TILES_PER_STEP - 1) * GROUP_TILE, GROUP_TILE), :] = _null_partial(GROUP_TILE)


def _moba_group(qg, k, v, b, tblk, tvalid, ntiles, max_tiles):
    _, H, S, Dh = k.shape
    G = H
    n_steps = max_tiles // TILES_PER_STEP
    step_rows = TILES_PER_STEP * GROUP_TILE

    def tile_idx(g, s, tblk, tvalid, nt):
        return (g * n_steps + jnp.where(s * TILES_PER_STEP < nt[g], s, n_steps - 1), 0)

    return pl.pallas_call(
        functools.partial(_moba_group_kernel, max_tiles=max_tiles),
        grid_spec=pltpu.PrefetchScalarGridSpec(
            num_scalar_prefetch=3,
            grid=(G, n_steps),
            in_specs=[pl.BlockSpec((step_rows // 2, LANES), tile_idx),
                      pl.BlockSpec((1, 1, S, Dh), lambda g, s, *_: (b, g, 0, 0)),
                      pl.BlockSpec((1, 1, S, Dh), lambda g, s, *_: (b, g, 0, 0))],
            out_specs=pl.BlockSpec((step_rows, LANES), tile_idx),
            scratch_shapes=[pltpu.VMEM((S, VA_W), BF16), pltpu.VMEM((S, 2 * Dh), BF16)],
        ),
        out_shape=jax.ShapeDtypeStruct((G * max_tiles * GROUP_TILE, LANES), I32),
        compiler_params=_cparams(("arbitrary", "arbitrary")),
        name="moba_group",
    )(tblk, tvalid, ntiles, qg, k, v)


def _moba_merge_kernel(*refs, n_skip):
    q_ref, k_ref, v_ref, sg_ref, x_ref, outw_ref = refs[n_skip:n_skip + 6]
    og_refs, out_ref, y_ref = refs[n_skip + 6:-2], refs[-2], refs[-1]
    scale = HEAD_DIM ** -0.5
    qi = lax.broadcasted_iota(I32, (MOBA_BLOCK, MOBA_BLOCK), 0)
    ki = lax.broadcasted_iota(I32, (MOBA_BLOCK, MOBA_BLOCK), 1)
    lane = lax.broadcasted_iota(I32, (MOBA_BLOCK, HEAD_DIM), 1)
    spread = jnp.where(lax.broadcasted_iota(I32, (HEAD_DIM, HEAD_DIM), 0) == L_LANE, 1.0, 0.0).astype(BF16)
    for u, t in [(u, t) for t in range(MERGE_TILES) for u in range(len(og_refs))]:
        og_ref = og_refs[u]
        rows = pl.ds(t * MOBA_BLOCK, MOBA_BLOCK)
        q = q_ref[0, u, rows, :].astype(BF16)
        s = _nt_dot(q, k_ref[0, u, rows, :])
        s = jnp.where(ki <= qi, s, NEG_INF)
        m_raw = jnp.max(s, axis=1, keepdims=True)
        p = jnp.exp2((s - m_raw) * (scale * LOG2_E)).astype(BF16)
        own = jnp.dot(p, _augment_v(v_ref[0, u, rows, :]), preferred_element_type=F32)
        parts = [_unpack_partial(og_ref[r, rows, :]) for r in range(MOBA_TOPK)]
        ms = [jnp.dot(jnp.where(lane == L_LANE, hi, 0.0).astype(BF16), spread, preferred_element_type=F32)
              for _, hi in parts]
        m_own = jnp.broadcast_to(m_raw * (scale * LOG2_E), lane.shape)
        m_all = m_own
        for m in ms:
            m_all = jnp.maximum(m_all, m)
        w_own = jnp.exp2(m_own - m_all)
        tot_lo = w_own * own[:, :HEAD_DIM]
        tot_hi = w_own * own[:, HEAD_DIM:]
        for (lo, hi), m in zip(parts, ms):
            w = jnp.exp2(m - m_all)
            tot_lo = tot_lo + w * lo
            tot_hi = tot_hi + w * hi
        acc = jnp.where(lane < HALF, tot_lo, pltpu.roll(tot_hi, HALF, 1))
        cols = pl.ds(u * HEAD_DIM, HEAD_DIM)
        y_ref[rows, cols] = ((acc / tot_lo[:, L_LANE:L_LANE + 1]) * sg_ref[rows, cols]).astype(BF16)

    out_ref[...] = x_ref[...] + jnp.dot(y_ref[...], outw_ref[...], preferred_element_type=F32)


def _moba_merge(q, k, v, sg, og, x2d, out_w, out_batch, n_out_batch, o_prev):
    _, H, S, Dh = q.shape
    D = x2d.shape[1]
    b = 0
    hps = HEADS_PER_STEP
    assert hps == H
    rows = MERGE_TILES * MOBA_BLOCK
    ns = S // rows
    og_spec = lambda u: pl.BlockSpec((MOBA_TOPK, rows, LANES),
                                     lambda _, hp, j: (0, (hp * hps + u) * ns + j, 0))
    prev_specs = [] if o_prev is None else [pl.BlockSpec(memory_space=pl.ANY)]
    prev_args = [] if o_prev is None else [o_prev]
    return pl.pallas_call(
        functools.partial(_moba_merge_kernel, n_skip=len(prev_args)),
        grid=(1, H // hps, ns),
        in_specs=prev_specs
                 + [pl.BlockSpec((1, hps, rows, Dh), lambda _, hp, j: (b, hp, j, 0)),
                    pl.BlockSpec((1, hps, rows, Dh), lambda _, hp, j: (b, hp, j, 0)),
                    pl.BlockSpec((1, hps, rows, Dh), lambda _, hp, j: (b, hp, j, 0)),
                    pl.BlockSpec((rows, hps * Dh), lambda _, hp, j: (b * ns + j, hp)),
                    pl.BlockSpec((rows, D), lambda _, hp, j: (b * ns + j, 0)),
                    pl.BlockSpec((H * Dh, D), lambda _, hp, j: (0, 0))]
                 + [og_spec(u) for u in range(hps)],
        out_specs=pl.BlockSpec((rows, D), lambda _, hp, j: (out_batch * ns + j, 0)),
        out_shape=jax.ShapeDtypeStruct((n_out_batch * S, D), F32),
        scratch_shapes=[pltpu.VMEM((rows, H * Dh), BF16)],
        input_output_aliases={} if o_prev is None else {0: 0},
        compiler_params=_cparams(("arbitrary", "arbitrary", "arbitrary")),
        name="moba_merge",
    )(*prev_args, q, k, v, sg, x2d, out_w, *([og] * hps))


def _moba_layer(qp, q16, k, v, kmean, sg, x2d, out_w, out_batch, n_out_batch, o_prev):
    _, H, S, Dh = q16.shape
    nb = S // MOBA_BLOCK
    max_tiles = (MOBA_TOPK * S) // GROUP_TILE + nb + 1
    max_tiles = -(-max_tiles // TILES_PER_STEP) * TILES_PER_STEP
    slab_rows = max_tiles * GROUP_TILE

    q_rows = qp.reshape(H * nb, IDX_CHUNK, SC_HALVES, HALF)
    rt, counts = _moba_route(q16, kmean, 0)
    pos, qpos, tblk, tvalid, ntiles = _moba_layout(rt, counts.reshape(-1), nb, max_tiles)
    qg = _sc_scatter_rows(q_rows, 0, qpos, H * slab_rows)
    qg = qg.reshape(H * slab_rows // 2, LANES)
    parts = _moba_group(qg, k, v, 0, tblk, tvalid, ntiles, max_tiles)
    og = _sc_gather_rows(parts, pos)
    return _moba_merge(q16, k, v, sg, og, x2d, out_w, out_batch, n_out_batch, o_prev)


def _rope_tables(S):
    inv = np.float32(ROPE_THETA) ** (-np.arange(HALF, dtype=np.float32) / np.float32(HALF))
    ang = np.arange(S, dtype=np.float32)[:, None] * inv[None, :].astype(np.float32)
    cos, sin = np.cos(ang).astype(np.float32), np.sin(ang).astype(np.float32)
    return (jnp.asarray(np.concatenate([cos, cos], axis=-1)),
            jnp.asarray(np.concatenate([-sin, sin], axis=-1)))


def kernel(x, a_norm, a_in_w, a_conv_w, a_conv_b, a_r_w, a_r_b, a_i_w, a_i_b, a_lambda, a_out_w,
           kv_norm, kv_w, k_norm, b_norm, b_in_w, q_norm, b_out_w):
    B, S, D = x.shape
    assert S % ROW_TILE == 0 and ROW_TILE % MOBA_BLOCK == 0
    assert D == N_RG_BLOCKS * RG_BW == N_HEADS * HEAD_DIM
    assert S % (ROUTE_TILES * MOBA_BLOCK) == 0 and S % (MERGE_TILES * MOBA_BLOCK) == 0
    assert S % V_BUILD_ROWS == 0 and S <= ROUTE_BLK_UNIT
    assert (N_HEADS * (S // MOBA_BLOCK)) % SC_WORKERS == 0
    n_a, n_b = a_in_w.shape[0], b_in_w.shape[0]
    assert n_a >= 1 and n_b >= 1
    x2d = x.reshape(B * S, D)
    row = lambda v: v.reshape(1, -1)
    cosf, sinf = _rope_tables(S)

    out = None
    for b in range(B):
        xs, b_in = x2d, b
        for l in range(n_a):
            c_rnn = a_in_w.shape[2] // 2
            wri = (0.5 * jnp.concatenate([a_r_w[l], a_i_w[l]], axis=-1)).astype(BF16)
            in_w = jnp.concatenate([a_in_w[l][:, :c_rnn], 0.5 * a_in_w[l][:, c_rnn:]], axis=-1).astype(BF16)
            xs = _hawk_layer(xs, b_in, S, row(a_norm[l]), in_w, a_conv_w[l], row(a_conv_b[l]),
                             wri, row(0.5 * a_r_b[l]), row(0.5 * a_i_b[l]), row(a_lambda[l]),
                             a_out_w[l].astype(BF16))
            b_in = 0
        for jl in range(n_b):
            q_args = (row(b_norm[jl]), b_in_w[jl].astype(BF16), row(q_norm[jl]))
            if jl == 0:
                k, v, kmean, qp, q16, sg = _kvq_proj(xs, 1, S, row(kv_norm), kv_w.astype(BF16), row(k_norm),
                                                    *q_args, cosf, sinf)
                kmean = jnp.transpose(kmean, (0, 2, 1, 3))
            else:
                qp, q16, sg = _q_proj(xs, 1, S, *q_args, cosf, sinf)
            w_out = b_out_w[jl].astype(BF16)
            if jl == n_b - 1:
                out = xs = _moba_layer(qp, q16, k, v, kmean, sg, xs, w_out, b, B, out)
            else:
                xs = _moba_layer(qp, q16, k, v, kmean, sg, xs, w_out, 0, 1, None)
    return out.reshape(B, S, D)
```

```python
import functools

import jax
import jax.numpy as jnp
import numpy as np
from jax import lax
from jax.experimental import pallas as pl
from jax.experimental.pallas import tpu as pltpu
from jax.experimental.pallas import tpu_sc as plsc

N_HEADS = 8
HEAD_DIM = 128
MOBA_BLOCK = 256
MOBA_TOPK = 3
CONV_WIDTH = 4
N_RG_BLOCKS = 8
RG_BW = 128
RG_C = 8.0
ROPE_THETA = 10000.0
EPS = 1e-6
NEG_INF = -1e30
LOG2_E = 1.4426950408889634
TINY = 1e-30

SUBLANES = 8
LANES = 128
ROW_TILE = 512
SUB_ROWS = 128
VMEM_LIMIT = 56 * 1024 * 1024

GROUP_TILE = MOBA_BLOCK
TILES_PER_STEP = 32
V_BUILD_ROWS = 1024
HEADS_PER_STEP = 8
MERGE_TILES = 2
ROUTE_TILES = 16
ROUTE_BLK_SHIFT = 16
ROUTE_BLK_UNIT = 1 << ROUTE_BLK_SHIFT
HALF = HEAD_DIM // 2
VA_W = 2 * HEAD_DIM
L_LANE = HALF
HI16_MASK = -65536
BF16_HALF_ULP = 0x8000
NEG_INF_BITS = int(np.float32(NEG_INF).view(np.int32))

SC_CORES = 2
SC_SUBCORES = 16
SC_WORKERS = SC_CORES * SC_SUBCORES
IDX_CHUNK = 128
SC_HALVES = MOBA_BLOCK // IDX_CHUNK

F32 = jnp.float32
BF16 = jnp.bfloat16
I32 = jnp.int32


def _cparams(sem):
    return pltpu.CompilerParams(dimension_semantics=sem, vmem_limit_bytes=VMEM_LIMIT)


def _rms_norm(x, g):
    ms = jnp.mean(x * x, axis=-1, keepdims=True)
    return (x * lax.rsqrt(ms + EPS)) * g


def _silu(x):
    hx = 0.5 * x
    return hx * jnp.tanh(hx) + hx


def _nt_dot(a, b):
    return lax.dot_general(a, b, (((1,), (1,)), ((), ())), preferred_element_type=F32)


def _head_norm_rope(z, g, cosf, sinf):
    outs = []
    for h in range(N_HEADS):
        zh = z[:, h * HEAD_DIM:(h + 1) * HEAD_DIM]
        zh = _rms_norm(zh, g)
        outs.append(zh * cosf + pltpu.roll(zh, HEAD_DIM // 2, 1) * sinf)
    return outs


def _hawk_kernel(x_ref, g_ref, inw_ref, cw_ref, cb_ref, wri_ref, rb_ref, ib_ref, lam_ref, outw_ref,
                 o_ref, gate_ref, xpad_ref, a_ref, b_ref, hc_ref):
    tm, C = gate_ref.shape
    s = pl.program_id(1)

    @pl.when(s == 0)
    def _():
        xpad_ref[pl.ds(0, SUBLANES), :] = jnp.zeros((SUBLANES, xpad_ref.shape[1]), F32)
        hc_ref[...] = jnp.zeros_like(hc_ref)

    h = _rms_norm(x_ref[...], g_ref[...]).astype(BF16)
    gate_ref[...] = jnp.dot(h, inw_ref[:, C:], preferred_element_type=F32)

    xpad_ref[pl.ds(SUBLANES, tm), :] = jnp.dot(h, inw_ref[:, :C], preferred_element_type=F32)
    cw = cw_ref[...]
    xc = xpad_ref[pl.ds(SUBLANES - (CONV_WIDTH - 1), tm), :] * cw[0:1, :]
    for k in range(1, CONV_WIDTH):
        xc = xc + xpad_ref[pl.ds(SUBLANES - (CONV_WIDTH - 1) + k, tm), :] * cw[k:k + 1, :]
    xc = xc + cb_ref[...]
    xpad_ref[pl.ds(0, SUBLANES), :] = xpad_ref[pl.ds(tm, SUBLANES), :]

    lam = lam_ref[...]
    sp = jnp.maximum(-lam, 0.0) + jnp.log1p(jnp.exp(-jnp.abs(lam)))
    half_log2a = (-0.5 * RG_C * LOG2_E) * sp
    xc16 = xc.astype(BF16)
    for g in range(N_RG_BLOCKS):
        lo = g * RG_BW
        z = jnp.dot(xc16[:, lo:lo + RG_BW], wri_ref[g], preferred_element_type=F32)
        tr = jnp.tanh(z[:, :RG_BW] + rb_ref[:, lo:lo + RG_BW])
        ti = jnp.tanh(z[:, RG_BW:] + ib_ref[:, lo:lo + RG_BW])
        a = jnp.exp2(tr * half_log2a[:, lo:lo + RG_BW] + half_log2a[:, lo:lo + RG_BW])
        om = 1.0 - a * a
        mult = om * lax.rsqrt(jnp.maximum(om, TINY))
        hx = 0.5 * xc[:, lo:lo + RG_BW]
        a_ref[:, lo:lo + RG_BW] = a
        b_ref[:, lo:lo + RG_BW] = mult * (hx * ti + hx)

    row = lax.broadcasted_iota(I32, (SUBLANES, a_ref.shape[1]), 0)

    def scan_body(c, hc):
        r0 = pl.multiple_of(c * SUBLANES, SUBLANES)
        a = a_ref[pl.ds(r0, SUBLANES), :]
        b = b_ref[pl.ds(r0, SUBLANES), :]
        for k in (1, 2, 4):
            keep = row >= k
            b = jnp.where(keep, a * pltpu.roll(b, k, 0) + b, b)
            a = jnp.where(keep, a * pltpu.roll(a, k, 0), a)
        h = a * hc + b
        b_ref[pl.ds(r0, SUBLANES), :] = h
        return h[SUBLANES - 1:SUBLANES, :]

    hc = lax.fori_loop(0, tm // SUBLANES, scan_body, hc_ref[0:1, :])
    hc_ref[0:1, :] = hc

    gh = gate_ref[...]
    y = (b_ref[...] * (gh * jnp.tanh(gh) + gh)).astype(BF16)
    o_ref[...] = x_ref[...] + jnp.dot(y, outw_ref[...], preferred_element_type=F32)


def _hawk_layer(x2d, b_in, S, g, in_w, cw, cb, wri, rb, ib, lam, out_w):
    D = x2d.shape[1]
    C = in_w.shape[1] // 2
    tm = ROW_TILE
    ns = S // tm
    vec = lambda n: pl.BlockSpec((1, n), lambda b, s: (0, 0))
    return pl.pallas_call(
        _hawk_kernel,
        grid=(1, ns),
        in_specs=[pl.BlockSpec((tm, D), lambda b, s: (b_in * ns + s, 0)),
                  vec(D),
                  pl.BlockSpec((D, 2 * C), lambda b, s: (0, 0)),
                  pl.BlockSpec((CONV_WIDTH, C), lambda b, s: (0, 0)),
                  vec(C),
                  pl.BlockSpec((N_RG_BLOCKS, RG_BW, 2 * RG_BW), lambda b, s: (0, 0, 0)),
                  vec(C), vec(C), vec(C),
                  pl.BlockSpec((C, D), lambda b, s: (0, 0))],
        out_specs=pl.BlockSpec((tm, D), lambda b, s: (s, 0)),
        out_shape=jax.ShapeDtypeStruct((S, D), F32),
        scratch_shapes=[pltpu.VMEM((tm, C), F32),
                        pltpu.VMEM((tm + SUBLANES, C), F32),
                        pltpu.VMEM((tm, C), F32),
                        pltpu.VMEM((tm, C), F32),
                        pltpu.VMEM((SUBLANES, C), F32)],
        compiler_params=_cparams(("arbitrary", "arbitrary")),
        name="hawk_layer",
    )(x2d, g, in_w, cw, cb, wri, rb, ib, lam, out_w)


def _emit_q(u, qg, cosf, sinf, t, qp_ref, q16_ref, sg_ref):
    assert SUB_ROWS == IDX_CHUNK
    d_attn = N_HEADS * HEAD_DIM
    rows = pl.ds(t * SUB_ROWS, SUB_ROWS)
    prow = pl.ds((t // 2) * SUB_ROWS, SUB_ROWS)
    qs = _head_norm_rope(u[:, :d_attn], qg, cosf, sinf)
    for hd in range(N_HEADS):
        q16 = qs[hd].astype(BF16)
        q16_ref[0, hd, rows, :] = q16
        r = _bits(q16.astype(F32))
        swapped = pltpu.roll(r, HALF, 1)
        if t % 2 == 0:
            w = (swapped & HI16_MASK) | lax.shift_right_logical(r, 16)
            qp_ref[0, hd, prow, pl.ds(0, HALF)] = w[:, :HALF]
        else:
            w = (r & HI16_MASK) | lax.shift_right_logical(swapped, 16)
            qp_ref[0, hd, prow, pl.ds(HALF, HALF)] = w[:, HALF:]
    sg_ref[rows, :] = _silu(u[:, d_attn:])


def _kvq_proj_kernel(x_ref, gkv_ref, wkv_ref, kg_ref, gq_ref, wq_ref, qg_ref, cos_ref, sin_ref,
                     k_ref, v_ref, km_ref, q_ref, q16_ref, sg_ref):
    tm = x_ref.shape[0]
    d_attn = N_HEADS * HEAD_DIM
    ksum = {}
    for t in range(tm // SUB_ROWS):
        rows = pl.ds(t * SUB_ROWS, SUB_ROWS)
        x = x_ref[rows, :]
        xn = x * lax.rsqrt(jnp.mean(x * x, axis=-1, keepdims=True) + EPS)
        kv = jnp.dot((xn * gkv_ref[...]).astype(BF16), wkv_ref[...], preferred_element_type=F32)
        u = jnp.dot((xn * gq_ref[...]).astype(BF16), wq_ref[...], preferred_element_type=F32)
        cosf, sinf = cos_ref[rows, :], sin_ref[rows, :]
        ks = _head_norm_rope(kv[:, :d_attn], kg_ref[...], cosf, sinf)
        for hd in range(N_HEADS):
            k_ref[0, hd, rows, :] = ks[hd].astype(BF16)
            v_ref[0, hd, rows, :] = kv[:, d_attn + hd * HEAD_DIM:d_attn + (hd + 1) * HEAD_DIM].astype(BF16)
            key = (hd, (t * SUB_ROWS) // MOBA_BLOCK)
            part = jnp.sum(ks[hd], axis=0, keepdims=True)
            ksum[key] = part if key not in ksum else ksum[key] + part
        _emit_q(u, qg_ref[...], cosf, sinf, t, q_ref, q16_ref, sg_ref)
    for (hd, r), total in ksum.items():
        km_ref[0, r, pl.ds(hd, 1), :] = total * (1.0 / MOBA_BLOCK)


def _kvq_proj(x2d, B, S, gkv, wkv, kg, gq, wq, qg, cosf, sinf):
    T, D = x2d.shape
    tm = ROW_TILE
    ns = S // tm
    nb = S // MOBA_BLOCK
    rpt = tm // MOBA_BLOCK
    d_attn = N_HEADS * HEAD_DIM
    vec = lambda n: pl.BlockSpec((1, n), lambda b, s: (0, 0))
    wspec = pl.BlockSpec((D, 2 * d_attn), lambda b, s: (0, 0))
    rope = pl.BlockSpec((tm, HEAD_DIM), lambda b, s: (s, 0))
    heads = pl.BlockSpec((1, N_HEADS, tm, HEAD_DIM), lambda b, s: (b, 0, s, 0))
    hshape = lambda dt: jax.ShapeDtypeStruct((B, N_HEADS, S, HEAD_DIM), dt)
    return pl.pallas_call(
        _kvq_proj_kernel,
        grid=(B, ns),
        in_specs=[pl.BlockSpec((tm, D), lambda b, s: (b * ns + s, 0)),
                  vec(D), wspec, vec(HEAD_DIM), vec(D), wspec, vec(HEAD_DIM), rope, rope],
        out_specs=[heads, heads,
                   pl.BlockSpec((1, rpt, N_HEADS, HEAD_DIM), lambda b, s: (b, s, 0, 0)),
                   pl.BlockSpec((1, N_HEADS, tm // 2, LANES), lambda b, s: (b, 0, s, 0)), heads,
                   pl.BlockSpec((tm, d_attn), lambda b, s: (b * ns + s, 0))],
        out_shape=[hshape(BF16), hshape(BF16),
                   jax.ShapeDtypeStruct((B, nb, N_HEADS, HEAD_DIM), F32),
                   jax.ShapeDtypeStruct((B, N_HEADS, S // 2, LANES), I32), hshape(BF16),
                   jax.ShapeDtypeStruct((T, d_attn), F32)],
        compiler_params=_cparams(("arbitrary", "arbitrary")),
        name="kvq_proj",
    )(x2d, gkv, wkv, kg, gq, wq, qg, cosf, sinf)


def _q_proj_kernel(x_ref, g_ref, w_ref, qg_ref, cos_ref, sin_ref, q_ref, q16_ref, sg_ref):
    for t in range(x_ref.shape[0] // SUB_ROWS):
        rows = pl.ds(t * SUB_ROWS, SUB_ROWS)
        h = _rms_norm(x_ref[rows, :], g_ref[...]).astype(BF16)
        u = jnp.dot(h, w_ref[...], preferred_element_type=F32)
        _emit_q(u, qg_ref[...], cos_ref[rows, :], sin_ref[rows, :], t, q_ref, q16_ref, sg_ref)


def _q_proj(x2d, B, S, g, w, qg, cosf, sinf):
    T, D = x2d.shape
    tm = ROW_TILE
    ns = S // tm
    d_attn = N_HEADS * HEAD_DIM
    heads = pl.BlockSpec((1, N_HEADS, tm, HEAD_DIM), lambda b, s: (b, 0, s, 0))
    return pl.pallas_call(
        _q_proj_kernel,
        grid=(B, ns),
        in_specs=[pl.BlockSpec((tm, D), lambda b, s: (b * ns + s, 0)),
                  pl.BlockSpec((1, D), lambda b, s: (0, 0)),
                  pl.BlockSpec((D, 2 * d_attn), lambda b, s: (0, 0)),
                  pl.BlockSpec((1, HEAD_DIM), lambda b, s: (0, 0)),
                  pl.BlockSpec((tm, HEAD_DIM), lambda b, s: (s, 0)),
                  pl.BlockSpec((tm, HEAD_DIM), lambda b, s: (s, 0))],
        out_specs=[pl.BlockSpec((1, N_HEADS, tm // 2, LANES), lambda b, s: (b, 0, s, 0)), heads,
                   pl.BlockSpec((tm, d_attn), lambda b, s: (b * ns + s, 0))],
        out_shape=[jax.ShapeDtypeStruct((B, N_HEADS, S // 2, LANES), I32),
                   jax.ShapeDtypeStruct((B, N_HEADS, S, HEAD_DIM), BF16),
                   jax.ShapeDtypeStruct((T, d_attn), F32)],
        compiler_params=_cparams(("arbitrary", "arbitrary")),
        name="q_proj",
    )(x2d, g, w, qg, cosf, sinf)


def _moba_route_kernel(q_ref, km_ref, rt_ref, cnt_ref, run_ref):
    step = pl.program_id(2)
    nb = km_ref.shape[2]

    @pl.when(step == 0)
    def _():
        run_ref[...] = jnp.zeros_like(run_ref)

    km = km_ref[0, 0].astype(BF16)
    blk = lax.broadcasted_iota(I32, (nb, MOBA_BLOCK), 0)
    qa = lax.broadcasted_iota(I32, (MOBA_BLOCK, MOBA_BLOCK), 0)
    qb = lax.broadcasted_iota(I32, (MOBA_BLOCK, MOBA_BLOCK), 1)
    earlier = jnp.where(qa < qb, 1.0, 0.0).astype(BF16)
    run = run_ref[...]
    for u in range(ROUTE_TILES):
        j = step * ROUTE_TILES + u
        g = _nt_dot(km, q_ref[0, 0, pl.ds(u * MOBA_BLOCK, MOBA_BLOCK), :].astype(BF16))
        g = jnp.where(blk < j, g, -jnp.inf)
        hits, ids = [], []
        for _ in range(MOBA_TOPK):
            mx = jnp.max(g, axis=0, keepdims=True)
            idx = jnp.min(jnp.where(g == mx, blk, nb), axis=0, keepdims=True)
            hit = blk == idx
            ok = mx > -jnp.inf
            hits.append(jnp.where(hit & ok, 1.0, 0.0))
            ids.append(jnp.where(ok, idx, -1))
            g = jnp.where(hit, -jnp.inf, g)
        sel = hits[0] + hits[1] + hits[2]

        rank = jnp.dot(sel.astype(BF16), earlier, preferred_element_type=F32)
        dest = run + rank
        run = run + jnp.sum(sel, axis=1, keepdims=True)
        for r in range(MOBA_TOPK):
            loc = jnp.sum(hits[r] * dest, axis=0, keepdims=True).astype(I32)
            rt_ref[u, pl.ds(r, 1), :] = jnp.where(ids[r] >= 0, ids[r] * ROUTE_BLK_UNIT + loc, -1)
        rt_ref[u, pl.ds(MOBA_TOPK, SUBLANES - MOBA_TOPK), :] = jnp.full(
            (SUBLANES - MOBA_TOPK, MOBA_BLOCK), -1, I32)
    run_ref[...] = run
    cnt_ref[0] = run.astype(I32)


def _moba_route(q, kmean, b):
    _, H, S, Dh = q.shape
    nb = S // MOBA_BLOCK
    rows = ROUTE_TILES * MOBA_BLOCK
    nsteps = nb // ROUTE_TILES
    return pl.pallas_call(
        _moba_route_kernel,
        grid=(1, H, nsteps),
        in_specs=[pl.BlockSpec((1, 1, rows, Dh), lambda _, h, j: (b, h, j, 0)),
                  pl.BlockSpec((1, 1, nb, Dh), lambda _, h, j: (b, h, 0, 0))],
        out_specs=[pl.BlockSpec((ROUTE_TILES, SUBLANES, MOBA_BLOCK), lambda _, h, j: (h * nsteps + j, 0, 0)),
                   pl.BlockSpec((1, nb, 1), lambda _, h, j: (h, 0, 0))],
        out_shape=[jax.ShapeDtypeStruct((H * nb, SUBLANES, MOBA_BLOCK), I32),
                   jax.ShapeDtypeStruct((H, nb, 1), I32)],
        scratch_shapes=[pltpu.VMEM((nb, 1), F32)],
        compiler_params=_cparams(("arbitrary", "arbitrary", "arbitrary")),
        name="moba_route",
    )(q, kmean)


def _moba_layout_kernel(cnt_ref, rt_ref, pos_ref, qpos_ref, tblk_ref, tvalid_ref, nt_ref, base_ref, *, max_tiles):
    g = pl.program_id(0)
    nb = rt_ref.shape[0]
    slab_rows = max_tiles * GROUP_TILE
    null_row0 = (max_tiles - 1) * GROUP_TILE

    def per_block(n, first_tile):
        count = cnt_ref[g * nb + n]
        tiles = (count + (GROUP_TILE - 1)) // GROUP_TILE
        base_ref[n] = first_tile * GROUP_TILE

        def per_tile(i, carry):
            tblk_ref[g * max_tiles + first_tile + i] = n
            tvalid_ref[g * max_tiles + first_tile + i] = jnp.minimum(count - i * GROUP_TILE, GROUP_TILE)
            return carry

        lax.fori_loop(0, tiles, per_tile, 0)
        return first_tile + tiles

    used = lax.fori_loop(0, nb, per_block, 0)
    nt_ref[g] = used

    def unused_tile(t, carry):
        tblk_ref[g * max_tiles + t] = nb - 1
        tvalid_ref[g * max_tiles + t] = 0
        return carry

    lax.fori_loop(used, max_tiles, unused_tile, 0)

    code = rt_ref[...].reshape(nb * SUBLANES, MOBA_BLOCK)
    blk = lax.shift_right_arithmetic(code, ROUTE_BLK_SHIFT)
    base = jnp.zeros_like(code)
    for n in range(nb):
        base = jnp.where(blk == n, base_ref[n], base)
    lane = lax.broadcasted_iota(I32, code.shape, 1)
    row = jnp.where(code >= 0, base + (code & (ROUTE_BLK_UNIT - 1)), null_row0 + lane) + g * slab_rows
    pos_ref[...] = row.reshape(pos_ref.shape)
    p = row & (GROUP_TILE - 1)
    assert GROUP_TILE == 2 * IDX_CHUNK
    slot = ((p & (IDX_CHUNK - 1)) << 1) | lax.shift_right_logical(p, IDX_CHUNK.bit_length() - 1)
    qpos_ref[...] = ((row - p) | slot).reshape(qpos_ref.shape)


def _moba_layout(rt, counts, nb, max_tiles):
    G = rt.shape[0] // nb
    smem = pl.BlockSpec(memory_space=pltpu.SMEM)
    return pl.pallas_call(
        functools.partial(_moba_layout_kernel, max_tiles=max_tiles),
        grid_spec=pltpu.PrefetchScalarGridSpec(
            num_scalar_prefetch=1,
            grid=(G,),
            in_specs=[pl.BlockSpec((nb, SUBLANES, MOBA_BLOCK), lambda g, cnt: (g, 0, 0))],
            out_specs=[pl.BlockSpec((nb, SUBLANES, MOBA_BLOCK), lambda g, cnt: (g, 0, 0)),
                       pl.BlockSpec((nb, SUBLANES, MOBA_BLOCK), lambda g, cnt: (g, 0, 0)), smem, smem, smem],
            scratch_shapes=[pltpu.SMEM((nb,), I32)],
        ),
        out_shape=[jax.ShapeDtypeStruct((G * nb, SUBLANES, MOBA_BLOCK), I32),
                   jax.ShapeDtypeStruct((G * nb, SUBLANES, MOBA_BLOCK), I32),
                   jax.ShapeDtypeStruct((G * max_tiles,), I32),
                   jax.ShapeDtypeStruct((G * max_tiles,), I32),
                   jax.ShapeDtypeStruct((G,), I32)],
        compiler_params=_cparams(("arbitrary",)),
        name="moba_layout",
    )(counts, rt)


def _sc_mesh():
    return plsc.VectorSubcoreMesh(core_axis_name="c", subcore_axis_name="s")


def _sc_load_indices(pos_hbm, idx_v, t0, n_tiles):
    for c in range(SC_HALVES):
        pltpu.sync_copy(pos_hbm.at[pl.ds(t0, n_tiles), pl.ds(0, MOBA_TOPK), pl.ds(c * IDX_CHUNK, IDX_CHUNK)],
                        idx_v.at[c])


def _sc_scatter_rows(src, src_tile0, pos, n_out_rows):
    D = src.shape[-1]
    n_tiles = pos.shape[0]
    per_w = n_tiles // SC_WORKERS
    assert per_w * SC_WORKERS == n_tiles and src.shape[1:3] == (IDX_CHUNK, SC_HALVES)

    @functools.partial(
        pl.kernel, mesh=_sc_mesh(),
        out_type=jax.ShapeDtypeStruct((n_out_rows, D), src.dtype),
        scratch_types=[pltpu.VMEM((SC_HALVES, per_w, MOBA_TOPK, IDX_CHUNK), I32),
                       pltpu.VMEM((SC_HALVES, IDX_CHUNK, D), src.dtype),
                       pltpu.SemaphoreType.DMA((SC_HALVES,)),
                       pltpu.SemaphoreType.DMA((SC_HALVES,))],
        compiler_params=pltpu.CompilerParams(use_tc_tiling_on_sc=False),
        name="sc_scatter_rows",
    )
    def k(src_hbm, pos_hbm, out_hbm, idx_v, rows_v, load_sem, scat_sem):
        t0 = (lax.axis_index("s") * SC_CORES + lax.axis_index("c")) * per_w
        _sc_load_indices(pos_hbm, idx_v, t0, per_w)

        def load(i, c):
            return pltpu.make_async_copy(src_hbm.at[src_tile0 + t0 + i, :, c, :], rows_v.at[c], load_sem.at[c])

        def scatters(i, c):
            return [pltpu.make_async_copy(rows_v.at[c], out_hbm.at[idx_v.at[c, i, r]], scat_sem.at[c])
                    for r in range(MOBA_TOPK)]

        load(0, 0).start()

        @pl.loop(0, per_w)
        def _(i):
            for c in range(SC_HALVES):
                pi, pc = (i, 0) if c else (i - 1, 1)
                ni, nc = (i + 1, 0) if c else (i, 1)
                load(i, c).wait()
                for cp in scatters(i, c):
                    cp.start()

                @pl.when(pi >= 0)
                def _():
                    for cp in scatters(pi, pc):
                        cp.wait()

                @pl.when(ni < per_w)
                def _():
                    load(ni, nc).start()

        for cp in scatters(per_w - 1, SC_HALVES - 1):
            cp.wait()

    return k(src, pos)


def _sc_gather_rows(table, pos):
    R, D = table.shape
    n_tiles = pos.shape[0]
    N = n_tiles * MOBA_BLOCK
    per_w = n_tiles // SC_WORKERS
    assert per_w * SC_WORKERS == n_tiles

    items = SC_HALVES * MOBA_TOPK
    n_buf = 2
    assert items % n_buf == 0

    @functools.partial(
        pl.kernel, mesh=_sc_mesh(),
        out_type=jax.ShapeDtypeStruct((MOBA_TOPK, N, D), table.dtype),
        scratch_types=[pltpu.VMEM((SC_HALVES, per_w, MOBA_TOPK, IDX_CHUNK), I32),
                       pltpu.VMEM((n_buf, IDX_CHUNK, D), table.dtype),
                       pltpu.SemaphoreType.DMA((n_buf,)),
                       pltpu.SemaphoreType.DMA((n_buf,))],
        name="sc_gather_rows",
    )
    def k(table_hbm, pos_hbm, out_hbm, idx_v, rows_v, gat_sem, put_sem):
        t0 = (lax.axis_index("s") * SC_CORES + lax.axis_index("c")) * per_w
        _sc_load_indices(pos_hbm, idx_v, t0, per_w)

        def gather(i, j):
            c, r = divmod(j, MOBA_TOPK)
            return pltpu.make_async_copy(table_hbm.at[idx_v.at[c, i, r]], rows_v.at[j % n_buf], gat_sem.at[j % n_buf])

        def put(i, j):
            c, r = divmod(j, MOBA_TOPK)
            row0 = pl.multiple_of((t0 + i) * MOBA_BLOCK + c * IDX_CHUNK, IDX_CHUNK)
            return pltpu.make_async_copy(rows_v.at[j % n_buf], out_hbm.at[r, pl.ds(row0, IDX_CHUNK)],
                                         put_sem.at[j % n_buf])

        gather(0, 0).start()

        @pl.loop(0, per_w)
        def _(i):
            for j in range(items):
                pi, pj = (i, j - 1) if j else (i - 1, items - 1)
                ni, nj = (i, j + 1) if j + 1 < items else (i + 1, 0)
                gather(i, j).wait()
                put(i, j).start()

                @pl.when(pi >= 0)
                def _():
                    put(pi, pj).wait()

                @pl.when(ni < per_w)
                def _():
                    gather(ni, nj).start()

        put(per_w - 1, items - 1).wait()

    return k(table, pos)


def _bits(x):
    return lax.bitcast_convert_type(x, I32)


def _pack_partial(part, m):
    lane = lax.broadcasted_iota(I32, (part.shape[0], HEAD_DIM), 1)
    lo = _bits(part[:, :HEAD_DIM]) + BF16_HALF_ULP
    hi = _bits(jnp.where(lane == L_LANE, m, part[:, HEAD_DIM:])) + BF16_HALF_ULP
    return (hi & HI16_MASK) | lax.shift_right_logical(lo, 16)


def _null_partial(rows):
    lane = lax.broadcasted_iota(I32, (rows, HEAD_DIM), 1)
    return jnp.where(lane == L_LANE, NEG_INF_BITS & HI16_MASK, 0)


def _unpack_partial(word):
    lo = lax.bitcast_convert_type(lax.shift_left(word, 16), F32)
    hi = lax.bitcast_convert_type(word & HI16_MASK, F32)
    return lo, hi


def _augment_v(v):
    vf = v.astype(F32)
    lane = lax.broadcasted_iota(I32, vf.shape, 1)
    lo = jnp.where(lane < HALF, vf, jnp.where(lane == L_LANE, 1.0, 0.0))
    hi = jnp.where(lane < HALF, pltpu.roll(vf, HALF, 1), 0.0)
    return jnp.concatenate([lo, hi], axis=1).astype(BF16)


def _moba_group_kernel(tblk_ref, tvalid_ref, nt_ref, qg_ref, k_ref, v_ref, o_ref, va_ref, k2_ref, *, max_tiles):
    g, step = pl.program_id(0), pl.program_id(1)
    t0 = g * max_tiles + step * TILES_PER_STEP
    scale = HEAD_DIM ** -0.5

    @pl.when(step == 0)
    def _():
        def build(c, carry):
            rows = pl.ds(pl.multiple_of(c * V_BUILD_ROWS, V_BUILD_ROWS), V_BUILD_ROWS)
            va_ref[rows, :] = _augment_v(v_ref[0, 0, rows, :])
            kf = k_ref[0, 0, rows, :].astype(F32)
            lane = lax.broadcasted_iota(I32, kf.shape, 1)
            swapped = pltpu.roll(kf, HALF, 1)
            k2_ref[rows, :] = jnp.concatenate([jnp.where(lane < HALF, kf, swapped),
                                               jnp.where(lane < HALF, swapped, kf)], axis=1).astype(BF16)
            return carry

        lax.fori_loop(0, va_ref.shape[0] // V_BUILD_ROWS, build, 0)

    @pl.when(tvalid_ref[t0] > 0)
    def _():
        half_rows = GROUP_TILE // 2
        lane = lax.broadcasted_iota(I32, (half_rows, HEAD_DIM), 1)
        for u in range(TILES_PER_STEP):
            word = qg_ref[pl.ds(u * half_rows, half_rows), :]
            first = lax.bitcast_convert_type(lax.shift_left(word, 16), F32)
            second = lax.bitcast_convert_type(word & HI16_MASK, F32)
            n0 = pl.multiple_of(tblk_ref[t0 + u] * MOBA_BLOCK, MOBA_BLOCK)
            left, right = lane < HALF, lane >= HALF
            q = jnp.concatenate(
                [jnp.concatenate([jnp.where(left, first, 0.0), jnp.where(left, second, 0.0)], axis=1),
                 jnp.concatenate([jnp.where(right, first, 0.0), jnp.where(right, second, 0.0)], axis=1)],
                axis=0).astype(BF16)
            s = _nt_dot(q, k2_ref[pl.ds(n0, MOBA_BLOCK), :])
            m = (jnp.max(s, axis=1, keepdims=True) * (scale * LOG2_E)).astype(BF16).astype(F32)
            p = jnp.exp2(s * (scale * LOG2_E) - m).astype(BF16)
            part = jnp.dot(p, va_ref[pl.ds(n0, MOBA_BLOCK), :], preferred_element_type=F32)
            o_ref[pl.ds(u * GROUP_TILE, GROUP_TILE), :] = _pack_partial(part, m)

    @pl.when(step == pl.num_programs(1) - 1)
    def _():
        o_ref[pl.ds((TILES_PER_STEP - 1) * GROUP_TILE, GROUP_TILE), :] = _null_partial(GROUP_TILE)


def _moba_group(qg, k, v, b, tblk, tvalid, ntiles, max_tiles):
    _, H, S, Dh = k.shape
    G = H
    n_steps = max_tiles // TILES_PER_STEP
    step_rows = TILES_PER_STEP * GROUP_TILE

    def tile_idx(g, s, tblk, tvalid, nt):
        return (g * n_steps + jnp.where(s * TILES_PER_STEP < nt[g], s, n_steps - 1), 0)

    def kv_idx(g, s, *_):
        return (b, jnp.minimum(g + jnp.minimum(s, 1), G - 1), 0, 0)

    return pl.pallas_call(
        functools.partial(_moba_group_kernel, max_tiles=max_tiles),
        grid_spec=pltpu.PrefetchScalarGridSpec(
            num_scalar_prefetch=3,
            grid=(G, n_steps),
            in_specs=[pl.BlockSpec((step_rows // 2, LANES), tile_idx),
                      pl.BlockSpec((1, 1, S, Dh), kv_idx),
                      pl.BlockSpec((1, 1, S, Dh), kv_idx)],
            out_specs=pl.BlockSpec((step_rows, LANES), tile_idx),
            scratch_shapes=[pltpu.VMEM((S, VA_W), BF16), pltpu.VMEM((S, 2 * Dh), BF16)],
        ),
        out_shape=jax.ShapeDtypeStruct((G * max_tiles * GROUP_TILE, LANES), I32),
        compiler_params=_cparams(("arbitrary", "arbitrary")),
        name="moba_group",
    )(tblk, tvalid, ntiles, qg, k, v)


def _moba_merge_kernel(*refs, n_skip):
    q_ref, k_ref, v_ref, sg_ref, x_ref, outw_ref = refs[n_skip:n_skip + 6]
    og_refs, out_ref, y_ref = refs[n_skip + 6:-2], refs[-2], refs[-1]
    scale = HEAD_DIM ** -0.5
    qi = lax.broadcasted_iota(I32, (MOBA_BLOCK, MOBA_BLOCK), 0)
    ki = lax.broadcasted_iota(I32, (MOBA_BLOCK, MOBA_BLOCK), 1)
    lane = lax.broadcasted_iota(I32, (MOBA_BLOCK, HEAD_DIM), 1)
    spread = jnp.where(lax.broadcasted_iota(I32, (HEAD_DIM, HEAD_DIM), 0) == L_LANE, 1.0, 0.0).astype(BF16)
    for u, t in [(u, t) for t in range(MERGE_TILES) for u in range(len(og_refs))]:
        og_ref = og_refs[u]
        rows = pl.ds(t * MOBA_BLOCK, MOBA_BLOCK)
        q = q_ref[0, u, rows, :].astype(BF16)
        s = _nt_dot(q, k_ref[0, u, rows, :])
        s = jnp.where(ki <= qi, s, NEG_INF)
        m_raw = jnp.max(s, axis=1, keepdims=True)
        p = jnp.exp2((s - m_raw) * (scale * LOG2_E)).astype(BF16)
        own = jnp.dot(p, _augment_v(v_ref[0, u, rows, :]), preferred_element_type=F32)
        parts = [_unpack_partial(og_ref[r, rows, :]) for r in range(MOBA_TOPK)]
        ms = [jnp.dot(jnp.where(lane == L_LANE, hi, 0.0).astype(BF16), spread, preferred_element_type=F32)
              for _, hi in parts]
        m_own = jnp.broadcast_to(m_raw * (scale * LOG2_E), lane.shape)
        m_all = m_own
        for m in ms:
            m_all = jnp.maximum(m_all, m)
        w_own = jnp.exp2(m_own - m_all)
        tot_lo = w_own * own[:, :HEAD_DIM]
        tot_hi = w_own * own[:, HEAD_DIM:]
        for (lo, hi), m in zip(parts, ms):
            w = jnp.exp2(m - m_all)
            tot_lo = tot_lo + w * lo
            tot_hi = tot_hi + w * hi
        acc = jnp.where(lane < HALF, tot_lo, pltpu.roll(tot_hi, HALF, 1))
        cols = pl.ds(u * HEAD_DIM, HEAD_DIM)
        y_ref[rows, cols] = ((acc / tot_lo[:, L_LANE:L_LANE + 1]) * sg_ref[rows, cols]).astype(BF16)

    out_ref[...] = x_ref[...] + jnp.dot(y_ref[...], outw_ref[...], preferred_element_type=F32)


def _moba_merge(q, k, v, sg, og, x2d, out_w, out_batch, n_out_batch, o_prev):
    _, H, S, Dh = q.shape
    D = x2d.shape[1]
    b = 0
    hps = HEADS_PER_STEP
    assert hps == H
    rows = MERGE_TILES * MOBA_BLOCK
    ns = S // rows
    og_spec = lambda u: pl.BlockSpec((MOBA_TOPK, rows, LANES),
                                     lambda _, hp, j: (0, (hp * hps + u) * ns + j, 0))
    prev_specs = [] if o_prev is None else [pl.BlockSpec(memory_space=pl.ANY)]
    prev_args = [] if o_prev is None else [o_prev]
    return pl.pallas_call(
        functools.partial(_moba_merge_kernel, n_skip=len(prev_args)),
        grid=(1, H // hps, ns),
        in_specs=prev_specs
                 + [pl.BlockSpec((1, hps, rows, Dh), lambda _, hp, j: (b, hp, j, 0)),
                    pl.BlockSpec((1, hps, rows, Dh), lambda _, hp, j: (b, hp, j, 0)),
                    pl.BlockSpec((1, hps, rows, Dh), lambda _, hp, j: (b, hp, j, 0)),
                    pl.BlockSpec((rows, hps * Dh), lambda _, hp, j: (b * ns + j, hp)),
                    pl.BlockSpec((rows, D), lambda _, hp, j: (b * ns + j, 0)),
                    pl.BlockSpec((H * Dh, D), lambda _, hp, j: (0, 0))]
                 + [og_spec(u) for u in range(hps)],
        out_specs=pl.BlockSpec((rows, D), lambda _, hp, j: (out_batch * ns + j, 0)),
        out_shape=jax.ShapeDtypeStruct((n_out_batch * S, D), F32),
        scratch_shapes=[pltpu.VMEM((rows, H * Dh), BF16)],
        input_output_aliases={} if o_prev is None else {0: 0},
        compiler_params=_cparams(("arbitrary", "arbitrary", "arbitrary")),
        name="moba_merge",
    )(*prev_args, q, k, v, sg, x2d, out_w, *([og] * hps))


def _moba_layer(qp, q16, k, v, kmean, sg, x2d, out_w, out_batch, n_out_batch, o_prev):
    _, H, S, Dh = q16.shape
    nb = S // MOBA_BLOCK
    max_tiles = (MOBA_TOPK * S) // GROUP_TILE + nb + 1
    max_tiles = -(-max_tiles // TILES_PER_STEP) * TILES_PER_STEP
    slab_rows = max_tiles * GROUP_TILE

    q_rows = qp.reshape(H * nb, IDX_CHUNK, SC_HALVES, HALF)
    rt, counts = _moba_route(q16, kmean, 0)
    pos, qpos, tblk, tvalid, ntiles = _moba_layout(rt, counts.reshape(-1), nb, max_tiles)
    qg = _sc_scatter_rows(q_rows, 0, qpos, H * slab_rows)
    qg = qg.reshape(H * slab_rows // 2, LANES)
    parts = _moba_group(qg, k, v, 0, tblk, tvalid, ntiles, max_tiles)
    og = _sc_gather_rows(parts, pos)
    return _moba_merge(q16, k, v, sg, og, x2d, out_w, out_batch, n_out_batch, o_prev)


def _rope_tables(S):
    inv = np.float32(ROPE_THETA) ** (-np.arange(HALF, dtype=np.float32) / np.float32(HALF))
    ang = np.arange(S, dtype=np.float32)[:, None] * inv[None, :].astype(np.float32)
    cos, sin = np.cos(ang).astype(np.float32), np.sin(ang).astype(np.float32)
    return (jnp.asarray(np.concatenate([cos, cos], axis=-1)),
            jnp.asarray(np.concatenate([-sin, sin], axis=-1)))


def kernel(x, a_norm, a_in_w, a_conv_w, a_conv_b, a_r_w, a_r_b, a_i_w, a_i_b, a_lambda, a_out_w,
           kv_norm, kv_w, k_norm, b_norm, b_in_w, q_norm, b_out_w):
    B, S, D = x.shape
    assert S % ROW_TILE == 0 and ROW_TILE % MOBA_BLOCK == 0
    assert D == N_RG_BLOCKS * RG_BW == N_HEADS * HEAD_DIM
    assert S % (ROUTE_TILES * MOBA_BLOCK) == 0 and S % (MERGE_TILES * MOBA_BLOCK) == 0
    assert S % V_BUILD_ROWS == 0 and S <= ROUTE_BLK_UNIT
    assert (N_HEADS * (S // MOBA_BLOCK)) % SC_WORKERS == 0
    n_a, n_b = a_in_w.shape[0], b_in_w.shape[0]
    assert n_a >= 1 and n_b >= 1
    x2d = x.reshape(B * S, D)
    row = lambda v: v.reshape(1, -1)
    cosf, sinf = _rope_tables(S)

    out = None
    for b in range(B):
        xs, b_in = x2d, b
        for l in range(n_a):
            c_rnn = a_in_w.shape[2] // 2
            wri = (0.5 * jnp.concatenate([a_r_w[l], a_i_w[l]], axis=-1)).astype(BF16)
            in_w = jnp.concatenate([a_in_w[l][:, :c_rnn], 0.5 * a_in_w[l][:, c_rnn:]], axis=-1).astype(BF16)
            xs = _hawk_layer(xs, b_in, S, row(a_norm[l]), in_w, a_conv_w[l], row(a_conv_b[l]),
                             wri, row(0.5 * a_r_b[l]), row(0.5 * a_i_b[l]), row(a_lambda[l]),
                             a_out_w[l].astype(BF16))
            b_in = 0
        for jl in range(n_b):
            q_args = (row(b_norm[jl]), b_in_w[jl].astype(BF16), row(q_norm[jl]))
            if jl == 0:
                k, v, kmean, qp, q16, sg = _kvq_proj(xs, 1, S, row(kv_norm), kv_w.astype(BF16), row(k_norm),
                                                    *q_args, cosf, sinf)
                kmean = jnp.transpose(kmean, (0, 2, 1, 3))
            else:
                qp, q16, sg = _q_proj(xs, 1, S, *q_args, cosf, sinf)
            w_out = b_out_w[jl].astype(BF16)
            if jl == n_b - 1:
                out = xs = _moba_layer(qp, q16, k, v, kmean, sg, xs, w_out, b, B, out)
            else:
                xs = _moba_layer(qp, q16, k, v, kmean, sg, xs, w_out, 0, 1, None)
    return out.reshape(B, S, D)
```

```python
import functools

import jax
import jax.numpy as jnp
import numpy as np
from jax import lax
from jax.experimental import pallas as pl
from jax.experimental.pallas import tpu as pltpu
from jax.experimental.pallas import tpu_sc as plsc

N_HEADS = 8
HEAD_DIM = 128
MOBA_BLOCK = 256
MOBA_TOPK = 3
CONV_WIDTH = 4
N_RG_BLOCKS = 8
RG_BW = 128
RG_C = 8.0
ROPE_THETA = 10000.0
EPS = 1e-6
NEG_INF = -1e30
LOG2_E = 1.4426950408889634
TINY = 1e-30

SUBLANES = 8
LANES = 128
ROW_TILE = 512
SUB_ROWS = 128
VMEM_LIMIT = 56 * 1024 * 1024

GROUP_TILE = MOBA_BLOCK
TILES_PER_STEP = 32
V_BUILD_ROWS = 1024
HEADS_PER_STEP = 8
MERGE_TILES = 2
ROUTE_TILES = 16
ROUTE_BLK_SHIFT = 16
ROUTE_BLK_UNIT = 1 << ROUTE_BLK_SHIFT
HALF = HEAD_DIM // 2
VA_W = 2 * HEAD_DIM
L_LANE = HALF
HI16_MASK = -65536
BF16_HALF_ULP = 0x8000
NEG_INF_BITS = int(np.float32(NEG_INF).view(np.int32))

SC_CORES = 2
SC_SUBCORES = 16
SC_WORKERS = SC_CORES * SC_SUBCORES
IDX_CHUNK = 128
SC_HALVES = MOBA_BLOCK // IDX_CHUNK

F32 = jnp.float32
BF16 = jnp.bfloat16
I32 = jnp.int32


def _cparams(sem):
    return pltpu.CompilerParams(dimension_semantics=sem, vmem_limit_bytes=VMEM_LIMIT)


def _rms_norm(x, g):
    ms = jnp.mean(x * x, axis=-1, keepdims=True)
    return (x * lax.rsqrt(ms + EPS)) * g


def _silu(x):
    hx = 0.5 * x
    return hx * jnp.tanh(hx) + hx


def _nt_dot(a, b):
    return lax.dot_general(a, b, (((1,), (1,)), ((), ())), preferred_element_type=F32)


def _head_norm_rope(z, g, cosf, sinf):
    outs = []
    for h in range(N_HEADS):
        zh = z[:, h * HEAD_DIM:(h + 1) * HEAD_DIM]
        zh = _rms_norm(zh, g)
        outs.append(zh * cosf + pltpu.roll(zh, HEAD_DIM // 2, 1) * sinf)
    return outs


def _hawk_kernel(x_ref, g_ref, inw_ref, cw_ref, cb_ref, wri_ref, rb_ref, ib_ref, lam_ref, outw_ref,
                 o_ref, gate_ref, xpad_ref, a_ref, b_ref, hc_ref):
    tm, C = gate_ref.shape
    s = pl.program_id(1)

    @pl.when(s == 0)
    def _():
        xpad_ref[pl.ds(0, SUBLANES), :] = jnp.zeros((SUBLANES, xpad_ref.shape[1]), F32)
        hc_ref[...] = jnp.zeros_like(hc_ref)

    h = _rms_norm(x_ref[...], g_ref[...]).astype(BF16)
    gate_ref[...] = jnp.dot(h, inw_ref[:, C:], preferred_element_type=F32)

    xpad_ref[pl.ds(SUBLANES, tm), :] = jnp.dot(h, inw_ref[:, :C], preferred_element_type=F32)
    cw = cw_ref[...]
    xc = xpad_ref[pl.ds(SUBLANES - (CONV_WIDTH - 1), tm), :] * cw[0:1, :]
    for k in range(1, CONV_WIDTH):
        xc = xc + xpad_ref[pl.ds(SUBLANES - (CONV_WIDTH - 1) + k, tm), :] * cw[k:k + 1, :]
    xc = xc + cb_ref[...]
    xpad_ref[pl.ds(0, SUBLANES), :] = xpad_ref[pl.ds(tm, SUBLANES), :]

    lam = lam_ref[...]
    sp = jnp.maximum(-lam, 0.0) + jnp.log1p(jnp.exp(-jnp.abs(lam)))
    half_log2a = (-0.5 * RG_C * LOG2_E) * sp
    xc16 = xc.astype(BF16)
    for g in range(N_RG_BLOCKS):
        lo = g * RG_BW
        z = jnp.dot(xc16[:, lo:lo + RG_BW], wri_ref[g], preferred_element_type=F32)
        tr = jnp.tanh(z[:, :RG_BW] + rb_ref[:, lo:lo + RG_BW])
        ti = jnp.tanh(z[:, RG_BW:] + ib_ref[:, lo:lo + RG_BW])
        a = jnp.exp2(tr * half_log2a[:, lo:lo + RG_BW] + half_log2a[:, lo:lo + RG_BW])
        om = 1.0 - a * a
        mult = om * lax.rsqrt(jnp.maximum(om, TINY))
        hx = 0.5 * xc[:, lo:lo + RG_BW]
        a_ref[:, lo:lo + RG_BW] = a
        b_ref[:, lo:lo + RG_BW] = mult * (hx * ti + hx)

    row = lax.broadcasted_iota(I32, (SUBLANES, a_ref.shape[1]), 0)

    def scan_body(c, hc):
        r0 = pl.multiple_of(c * SUBLANES, SUBLANES)
        a = a_ref[pl.ds(r0, SUBLANES), :]
        b = b_ref[pl.ds(r0, SUBLANES), :]
        for k in (1, 2, 4):
            keep = row >= k
            b = jnp.where(keep, a * pltpu.roll(b, k, 0) + b, b)
            a = jnp.where(keep, a * pltpu.roll(a, k, 0), a)
        h = a * hc + b
        b_ref[pl.ds(r0, SUBLANES), :] = h
        return h[SUBLANES - 1:SUBLANES, :]

    hc = lax.fori_loop(0, tm // SUBLANES, scan_body, hc_ref[0:1, :])
    hc_ref[0:1, :] = hc

    gh = gate_ref[...]
    y = (b_ref[...] * (gh * jnp.tanh(gh) + gh)).astype(BF16)
    o_ref[...] = x_ref[...] + jnp.dot(y, outw_ref[...], preferred_element_type=F32)


def _hawk_layer(x2d, b_in, S, g, in_w, cw, cb, wri, rb, ib, lam, out_w):
    D = x2d.shape[1]
    C = in_w.shape[1] // 2
    tm = ROW_TILE
    ns = S // tm
    vec = lambda n: pl.BlockSpec((1, n), lambda b, s: (0, 0))
    return pl.pallas_call(
        _hawk_kernel,
        grid=(1, ns),
        in_specs=[pl.BlockSpec((tm, D), lambda b, s: (b_in * ns + s, 0)),
                  vec(D),
                  pl.BlockSpec((D, 2 * C), lambda b, s: (0, 0)),
                  pl.BlockSpec((CONV_WIDTH, C), lambda b, s: (0, 0)),
                  vec(C),
                  pl.BlockSpec((N_RG_BLOCKS, RG_BW, 2 * RG_BW), lambda b, s: (0, 0, 0)),
                  vec(C), vec(C), vec(C),
                  pl.BlockSpec((C, D), lambda b, s: (0, 0))],
        out_specs=pl.BlockSpec((tm, D), lambda b, s: (s, 0)),
        out_shape=jax.ShapeDtypeStruct((S, D), F32),
        scratch_shapes=[pltpu.VMEM((tm, C), F32),
                        pltpu.VMEM((tm + SUBLANES, C), F32),
                        pltpu.VMEM((tm, C), F32),
                        pltpu.VMEM((tm, C), F32),
                        pltpu.VMEM((SUBLANES, C), F32)],
        compiler_params=_cparams(("arbitrary", "arbitrary")),
        name="hawk_layer",
    )(x2d, g, in_w, cw, cb, wri, rb, ib, lam, out_w)


def _emit_q(u, qg, cosf, sinf, t, qp_ref, q16_ref, sg_ref):
    assert SUB_ROWS == IDX_CHUNK
    d_attn = N_HEADS * HEAD_DIM
    rows = pl.ds(t * SUB_ROWS, SUB_ROWS)
    prow = pl.ds((t // 2) * SUB_ROWS, SUB_ROWS)
    qs = _head_norm_rope(u[:, :d_attn], qg, cosf, sinf)
    for hd in range(N_HEADS):
        q16 = qs[hd].astype(BF16)
        q16_ref[0, hd, rows, :] = q16
        r = _bits(q16.astype(F32))
        swapped = pltpu.roll(r, HALF, 1)
        if t % 2 == 0:
            w = (swapped & HI16_MASK) | lax.shift_right_logical(r, 16)
            qp_ref[0, hd, prow, pl.ds(0, HALF)] = w[:, :HALF]
        else:
            w = (r & HI16_MASK) | lax.shift_right_logical(swapped, 16)
            qp_ref[0, hd, prow, pl.ds(HALF, HALF)] = w[:, HALF:]
    sg_ref[rows, :] = _silu(u[:, d_attn:])


def _kvq_proj_kernel(x_ref, gkv_ref, wkv_ref, kg_ref, gq_ref, wq_ref, qg_ref, cos_ref, sin_ref,
                     k_ref, v_ref, km_ref, q_ref, q16_ref, sg_ref):
    tm = x_ref.shape[0]
    d_attn = N_HEADS * HEAD_DIM
    ksum = {}
    for t in range(tm // SUB_ROWS):
        rows = pl.ds(t * SUB_ROWS, SUB_ROWS)
        x = x_ref[rows, :]
        xn = x * lax.rsqrt(jnp.mean(x * x, axis=-1, keepdims=True) + EPS)
        kv = jnp.dot((xn * gkv_ref[...]).astype(BF16), wkv_ref[...], preferred_element_type=F32)
        u = jnp.dot((xn * gq_ref[...]).astype(BF16), wq_ref[...], preferred_element_type=F32)
        cosf, sinf = cos_ref[rows, :], sin_ref[rows, :]
        ks = _head_norm_rope(kv[:, :d_attn], kg_ref[...], cosf, sinf)
        for hd in range(N_HEADS):
            k_ref[0, hd, rows, :] = ks[hd].astype(BF16)
            v_ref[0, hd, rows, :] = kv[:, d_attn + hd * HEAD_DIM:d_attn + (hd + 1) * HEAD_DIM].astype(BF16)
            key = (hd, (t * SUB_ROWS) // MOBA_BLOCK)
            part = jnp.sum(ks[hd], axis=0, keepdims=True)
            ksum[key] = part if key not in ksum else ksum[key] + part
        _emit_q(u, qg_ref[...], cosf, sinf, t, q_ref, q16_ref, sg_ref)
    for (hd, r), total in ksum.items():
        km_ref[0, r, pl.ds(hd, 1), :] = total * (1.0 / MOBA_BLOCK)


def _kvq_proj(x2d, B, S, gkv, wkv, kg, gq, wq, qg, cosf, sinf):
    T, D = x2d.shape
    tm = ROW_TILE
    ns = S // tm
    nb = S // MOBA_BLOCK
    rpt = tm // MOBA_BLOCK
    d_attn = N_HEADS * HEAD_DIM
    vec = lambda n: pl.BlockSpec((1, n), lambda b, s: (0, 0))
    wspec = pl.BlockSpec((D, 2 * d_attn), lambda b, s: (0, 0))
    rope = pl.BlockSpec((tm, HEAD_DIM), lambda b, s: (s, 0))
    heads = pl.BlockSpec((1, N_HEADS, tm, HEAD_DIM), lambda b, s: (b, 0, s, 0))
    hshape = lambda dt: jax.ShapeDtypeStruct((B, N_HEADS, S, HEAD_DIM), dt)
    return pl.pallas_call(
        _kvq_proj_kernel,
        grid=(B, ns),
        in_specs=[pl.BlockSpec((tm, D), lambda b, s: (b * ns + s, 0)),
                  vec(D), wspec, vec(HEAD_DIM), vec(D), wspec, vec(HEAD_DIM), rope, rope],
        out_specs=[heads, heads,
                   pl.BlockSpec((1, rpt, N_HEADS, HEAD_DIM), lambda b, s: (b, s, 0, 0)),
                   pl.BlockSpec((1, N_HEADS, tm // 2, LANES), lambda b, s: (b, 0, s, 0)), heads,
                   pl.BlockSpec((tm, d_attn), lambda b, s: (b * ns + s, 0))],
        out_shape=[hshape(BF16), hshape(BF16),
                   jax.ShapeDtypeStruct((B, nb, N_HEADS, HEAD_DIM), F32),
                   jax.ShapeDtypeStruct((B, N_HEADS, S // 2, LANES), I32), hshape(BF16),
                   jax.ShapeDtypeStruct((T, d_attn), F32)],
        compiler_params=_cparams(("arbitrary", "arbitrary")),
        name="kvq_proj",
    )(x2d, gkv, wkv, kg, gq, wq, qg, cosf, sinf)


def _q_proj_kernel(x_ref, g_ref, w_ref, qg_ref, cos_ref, sin_ref, q_ref, q16_ref, sg_ref):
    for t in range(x_ref.shape[0] // SUB_ROWS):
        rows = pl.ds(t * SUB_ROWS, SUB_ROWS)
        h = _rms_norm(x_ref[rows, :], g_ref[...]).astype(BF16)
        u = jnp.dot(h, w_ref[...], preferred_element_type=F32)
        _emit_q(u, qg_ref[...], cos_ref[rows, :], sin_ref[rows, :], t, q_ref, q16_ref, sg_ref)


def _q_proj(x2d, B, S, g, w, qg, cosf, sinf):
    T, D = x2d.shape
    tm = ROW_TILE
    ns = S // tm
    d_attn = N_HEADS * HEAD_DIM
    heads = pl.BlockSpec((1, N_HEADS, tm, HEAD_DIM), lambda b, s: (b, 0, s, 0))
    return pl.pallas_call(
        _q_proj_kernel,
        grid=(B, ns),
        in_specs=[pl.BlockSpec((tm, D), lambda b, s: (b * ns + s, 0)),
                  pl.BlockSpec((1, D), lambda b, s: (0, 0)),
                  pl.BlockSpec((D, 2 * d_attn), lambda b, s: (0, 0)),
                  pl.BlockSpec((1, HEAD_DIM), lambda b, s: (0, 0)),
                  pl.BlockSpec((tm, HEAD_DIM), lambda b, s: (s, 0)),
                  pl.BlockSpec((tm, HEAD_DIM), lambda b, s: (s, 0))],
        out_specs=[pl.BlockSpec((1, N_HEADS, tm // 2, LANES), lambda b, s: (b, 0, s, 0)), heads,
                   pl.BlockSpec((tm, d_attn), lambda b, s: (b * ns + s, 0))],
        out_shape=[jax.ShapeDtypeStruct((B, N_HEADS, S // 2, LANES), I32),
                   jax.ShapeDtypeStruct((B, N_HEADS, S, HEAD_DIM), BF16),
                   jax.ShapeDtypeStruct((T, d_attn), F32)],
        compiler_params=_cparams(("arbitrary", "arbitrary")),
        name="q_proj",
    )(x2d, g, w, qg, cosf, sinf)


def _moba_route_kernel(q_ref, km_ref, rt_ref, cnt_ref, run_ref):
    step = pl.program_id(2)
    nb = km_ref.shape[2]

    @pl.when(step == 0)
    def _():
        run_ref[...] = jnp.zeros_like(run_ref)

    km = km_ref[0, 0].astype(BF16)
    blk = lax.broadcasted_iota(I32, (nb, MOBA_BLOCK), 0)
    qa = lax.broadcasted_iota(I32, (MOBA_BLOCK, MOBA_BLOCK), 0)
    qb = lax.broadcasted_iota(I32, (MOBA_BLOCK, MOBA_BLOCK), 1)
    earlier = jnp.where(qa < qb, 1.0, 0.0).astype(BF16)
    run = run_ref[...]
    for u in range(ROUTE_TILES):
        j = step * ROUTE_TILES + u
        g = _nt_dot(km, q_ref[0, 0, pl.ds(u * MOBA_BLOCK, MOBA_BLOCK), :].astype(BF16))
        g = jnp.where(blk < j, g, -jnp.inf)
        hits, ids = [], []
        for _ in range(MOBA_TOPK):
            mx = jnp.max(g, axis=0, keepdims=True)
            idx = jnp.min(jnp.where(g == mx, blk, nb), axis=0, keepdims=True)
            hit = blk == idx
            ok = mx > -jnp.inf
            hits.append(jnp.where(hit & ok, 1.0, 0.0))
            ids.append(jnp.where(ok, idx, -1))
            g = jnp.where(hit, -jnp.inf, g)
        sel = hits[0] + hits[1] + hits[2]

        rank = jnp.dot(sel.astype(BF16), earlier, preferred_element_type=F32)
        dest = run + rank
        run = run + jnp.sum(sel, axis=1, keepdims=True)
        for r in range(MOBA_TOPK):
            loc = jnp.sum(hits[r] * dest, axis=0, keepdims=True).astype(I32)
            rt_ref[u, pl.ds(r, 1), :] = jnp.where(ids[r] >= 0, ids[r] * ROUTE_BLK_UNIT + loc, -1)
        rt_ref[u, pl.ds(MOBA_TOPK, SUBLANES - MOBA_TOPK), :] = jnp.full(
            (SUBLANES - MOBA_TOPK, MOBA_BLOCK), -1, I32)
    run_ref[...] = run
    cnt_ref[0] = run.astype(I32)


def _moba_route(q, kmean, b):
    _, H, S, Dh = q.shape
    nb = S // MOBA_BLOCK
    rows = ROUTE_TILES * MOBA_BLOCK
    nsteps = nb // ROUTE_TILES
    return pl.pallas_call(
        _moba_route_kernel,
        grid=(1, H, nsteps),
        in_specs=[pl.BlockSpec((1, 1, rows, Dh), lambda _, h, j: (b, h, j, 0)),
                  pl.BlockSpec((1, 1, nb, Dh), lambda _, h, j: (b, h, 0, 0))],
        out_specs=[pl.BlockSpec((ROUTE_TILES, SUBLANES, MOBA_BLOCK), lambda _, h, j: (h * nsteps + j, 0, 0)),
                   pl.BlockSpec((1, nb, 1), lambda _, h, j: (h, 0, 0))],
        out_shape=[jax.ShapeDtypeStruct((H * nb, SUBLANES, MOBA_BLOCK), I32),
                   jax.ShapeDtypeStruct((H, nb, 1), I32)],
        scratch_shapes=[pltpu.VMEM((nb, 1), F32)],
        compiler_params=_cparams(("arbitrary", "arbitrary", "arbitrary")),
        name="moba_route",
    )(q, kmean)


def _moba_layout_kernel(cnt_ref, rt_ref, pos_ref, qpos_ref, tblk_ref, nt_ref, base_ref, *, max_tiles):
    g = pl.program_id(0)
    nb = rt_ref.shape[0]
    slab_rows = max_tiles * GROUP_TILE

    def per_block(n, first_tile):
        count = cnt_ref[g * nb + n]
        tiles = (count + (GROUP_TILE - 1)) // GROUP_TILE
        base_ref[n] = first_tile * GROUP_TILE

        def per_tile(i, carry):
            tblk_ref[g * max_tiles + first_tile + i] = n
            return carry

        lax.fori_loop(0, tiles, per_tile, 0)
        return first_tile + tiles

    used = lax.fori_loop(0, nb, per_block, 0)
    nt_ref[g] = used
    null_row0 = used * GROUP_TILE

    def unused_tile(t, carry):
        tblk_ref[g * max_tiles + t] = nb - 1
        return carry

    lax.fori_loop(used, max_tiles, unused_tile, 0)

    code = rt_ref[...].reshape(nb * SUBLANES, MOBA_BLOCK)
    blk = lax.shift_right_arithmetic(code, ROUTE_BLK_SHIFT)
    base = jnp.zeros_like(code)
    for n in range(nb):
        base = jnp.where(blk == n, base_ref[n], base)
    lane = lax.broadcasted_iota(I32, code.shape, 1)
    row = jnp.where(code >= 0, base + (code & (ROUTE_BLK_UNIT - 1)), null_row0 + lane) + g * slab_rows
    pos_ref[...] = row.reshape(pos_ref.shape)
    p = row & (GROUP_TILE - 1)
    assert GROUP_TILE == 2 * IDX_CHUNK
    slot = ((p & (IDX_CHUNK - 1)) << 1) | lax.shift_right_logical(p, IDX_CHUNK.bit_length() - 1)
    qpos_ref[...] = ((row - p) | slot).reshape(qpos_ref.shape)


def _moba_layout(rt, counts, nb, max_tiles):
    G = rt.shape[0] // nb
    smem = pl.BlockSpec(memory_space=pltpu.SMEM)
    return pl.pallas_call(
        functools.partial(_moba_layout_kernel, max_tiles=max_tiles),
        grid_spec=pltpu.PrefetchScalarGridSpec(
            num_scalar_prefetch=1,
            grid=(G,),
            in_specs=[pl.BlockSpec((nb, SUBLANES, MOBA_BLOCK), lambda g, cnt: (g, 0, 0))],
            out_specs=[pl.BlockSpec((nb, SUBLANES, MOBA_BLOCK), lambda g, cnt: (g, 0, 0)),
                       pl.BlockSpec((nb, SUBLANES, MOBA_BLOCK), lambda g, cnt: (g, 0, 0)), smem, smem],
            scratch_shapes=[pltpu.SMEM((nb,), I32)],
        ),
        out_shape=[jax.ShapeDtypeStruct((G * nb, SUBLANES, MOBA_BLOCK), I32),
                   jax.ShapeDtypeStruct((G * nb, SUBLANES, MOBA_BLOCK), I32),
                   jax.ShapeDtypeStruct((G * max_tiles,), I32),
                   jax.ShapeDtypeStruct((G,), I32)],
        compiler_params=_cparams(("arbitrary",)),
        name="moba_layout",
    )(counts, rt)


def _sc_mesh():
    return plsc.VectorSubcoreMesh(core_axis_name="c", subcore_axis_name="s")


def _sc_load_indices(pos_hbm, idx_v, t0, n_tiles):
    for c in range(SC_HALVES):
        pltpu.sync_copy(pos_hbm.at[pl.ds(t0, n_tiles), pl.ds(0, MOBA_TOPK), pl.ds(c * IDX_CHUNK, IDX_CHUNK)],
                        idx_v.at[c])


def _sc_scatter_rows(src, src_tile0, pos, n_out_rows):
    D = src.shape[-1]
    n_tiles = pos.shape[0]
    per_w = n_tiles // SC_WORKERS
    assert per_w * SC_WORKERS == n_tiles and src.shape[1:3] == (IDX_CHUNK, SC_HALVES)

    @functools.partial(
        pl.kernel, mesh=_sc_mesh(),
        out_type=jax.ShapeDtypeStruct((n_out_rows, D), src.dtype),
        scratch_types=[pltpu.VMEM((SC_HALVES, per_w, MOBA_TOPK, IDX_CHUNK), I32),
                       pltpu.VMEM((SC_HALVES, IDX_CHUNK, D), src.dtype),
                       pltpu.SemaphoreType.DMA((SC_HALVES,)),
                       pltpu.SemaphoreType.DMA((SC_HALVES,))],
        compiler_params=pltpu.CompilerParams(use_tc_tiling_on_sc=False),
        name="sc_scatter_rows",
    )
    def k(src_hbm, pos_hbm, out_hbm, idx_v, rows_v, load_sem, scat_sem):
        t0 = (lax.axis_index("s") * SC_CORES + lax.axis_index("c")) * per_w
        _sc_load_indices(pos_hbm, idx_v, t0, per_w)

        def load(i, c):
            return pltpu.make_async_copy(src_hbm.at[src_tile0 + t0 + i, :, c, :], rows_v.at[c], load_sem.at[c])

        def scatters(i, c):
            return [pltpu.make_async_copy(rows_v.at[c], out_hbm.at[idx_v.at[c, i, r]], scat_sem.at[c])
                    for r in range(MOBA_TOPK)]

        load(0, 0).start()

        @pl.loop(0, per_w)
        def _(i):
            for c in range(SC_HALVES):
                pi, pc = (i, 0) if c else (i - 1, 1)
                ni, nc = (i + 1, 0) if c else (i, 1)
                load(i, c).wait()
                for cp in scatters(i, c):
                    cp.start()

                @pl.when(pi >= 0)
                def _():
                    for cp in scatters(pi, pc):
                        cp.wait()

                @pl.when(ni < per_w)
                def _():
                    load(ni, nc).start()

        for cp in scatters(per_w - 1, SC_HALVES - 1):
            cp.wait()

    return k(src, pos)


def _sc_gather_rows(table, pos):
    R, D = table.shape
    n_tiles = pos.shape[0]
    N = n_tiles * MOBA_BLOCK
    per_w = n_tiles // SC_WORKERS
    assert per_w * SC_WORKERS == n_tiles

    items = SC_HALVES * MOBA_TOPK
    n_buf = 2
    assert items % n_buf == 0

    @functools.partial(
        pl.kernel, mesh=_sc_mesh(),
        out_type=jax.ShapeDtypeStruct((MOBA_TOPK, N, D), table.dtype),
        scratch_types=[pltpu.VMEM((SC_HALVES, per_w, MOBA_TOPK, IDX_CHUNK), I32),
                       pltpu.VMEM((n_buf, IDX_CHUNK, D), table.dtype),
                       pltpu.SemaphoreType.DMA((n_buf,)),
                       pltpu.SemaphoreType.DMA((n_buf,))],
        name="sc_gather_rows",
    )
    def k(table_hbm, pos_hbm, out_hbm, idx_v, rows_v, gat_sem, put_sem):
        t0 = (lax.axis_index("s") * SC_CORES + lax.axis_index("c")) * per_w
        _sc_load_indices(pos_hbm, idx_v, t0, per_w)

        def gather(i, j):
            c, r = divmod(j, MOBA_TOPK)
            return pltpu.make_async_copy(table_hbm.at[idx_v.at[c, i, r]], rows_v.at[j % n_buf], gat_sem.at[j % n_buf])

        def put(i, j):
            c, r = divmod(j, MOBA_TOPK)
            row0 = pl.multiple_of((t0 + i) * MOBA_BLOCK + c * IDX_CHUNK, IDX_CHUNK)
            return pltpu.make_async_copy(rows_v.at[j % n_buf], out_hbm.at[r, pl.ds(row0, IDX_CHUNK)],
                                         put_sem.at[j % n_buf])

        gather(0, 0).start()

        @pl.loop(0, per_w)
        def _(i):
            for j in range(items):
                pi, pj = (i, j - 1) if j else (i - 1, items - 1)
                ni, nj = (i, j + 1) if j + 1 < items else (i + 1, 0)
                gather(i, j).wait()
                put(i, j).start()

                @pl.when(pi >= 0)
                def _():
                    put(pi, pj).wait()

                @pl.when(ni < per_w)
                def _():
                    gather(ni, nj).start()

        put(per_w - 1, items - 1).wait()

    return k(table, pos)


def _bits(x):
    return lax.bitcast_convert_type(x, I32)


def _pack_partial(part, m):
    lane = lax.broadcasted_iota(I32, (part.shape[0], HEAD_DIM), 1)
    lo = _bits(part[:, :HEAD_DIM]) + BF16_HALF_ULP
    hi = _bits(jnp.where(lane == L_LANE, m, part[:, HEAD_DIM:])) + BF16_HALF_ULP
    return (hi & HI16_MASK) | lax.shift_right_logical(lo, 16)


def _null_partial(rows):
    lane = lax.broadcasted_iota(I32, (rows, HEAD_DIM), 1)
    return jnp.where(lane == L_LANE, NEG_INF_BITS & HI16_MASK, 0)


def _unpack_partial(word):
    lo = lax.bitcast_convert_type(lax.shift_left(word, 16), F32)
    hi = lax.bitcast_convert_type(word & HI16_MASK, F32)
    return lo, hi


def _augment_v(v):
    vf = v.astype(F32)
    lane = lax.broadcasted_iota(I32, vf.shape, 1)
    lo = jnp.where(lane < HALF, vf, jnp.where(lane == L_LANE, 1.0, 0.0))
    hi = jnp.where(lane < HALF, pltpu.roll(vf, HALF, 1), 0.0)
    return jnp.concatenate([lo, hi], axis=1).astype(BF16)


def _moba_group_kernel(tblk_ref, nt_ref, qg_ref, k_ref, v_ref, o_ref, va_ref, k2_ref, *, max_tiles):
    g, step = pl.program_id(0), pl.program_id(1)
    t0 = g * max_tiles + step * TILES_PER_STEP
    scale = HEAD_DIM ** -0.5
    null_tile = nt_ref[g]
    last_step = null_tile // TILES_PER_STEP

    @pl.when(step == 0)
    def _():
        def build(c, carry):
            rows = pl.ds(pl.multiple_of(c * V_BUILD_ROWS, V_BUILD_ROWS), V_BUILD_ROWS)
            va_ref[rows, :] = _augment_v(v_ref[0, 0, rows, :])
            kf = k_ref[0, 0, rows, :].astype(F32)
            lane = lax.broadcasted_iota(I32, kf.shape, 1)
            swapped = pltpu.roll(kf, HALF, 1)
            k2_ref[rows, :] = jnp.concatenate([jnp.where(lane < HALF, kf, swapped),
                                               jnp.where(lane < HALF, swapped, kf)], axis=1).astype(BF16)
            return carry

        lax.fori_loop(0, va_ref.shape[0] // V_BUILD_ROWS, build, 0)

    @pl.when(step <= last_step)
    def _():
        half_rows = GROUP_TILE // 2
        lane = lax.broadcasted_iota(I32, (half_rows, HEAD_DIM), 1)
        for u in range(TILES_PER_STEP):
            word = qg_ref[pl.ds(u * half_rows, half_rows), :]
            first = lax.bitcast_convert_type(lax.shift_left(word, 16), F32)
            second = lax.bitcast_convert_type(word & HI16_MASK, F32)
            n0 = pl.multiple_of(tblk_ref[t0 + u] * MOBA_BLOCK, MOBA_BLOCK)
            left, right = lane < HALF, lane >= HALF
            q = jnp.concatenate(
                [jnp.concatenate([jnp.where(left, first, 0.0), jnp.where(left, second, 0.0)], axis=1),
                 jnp.concatenate([jnp.where(right, first, 0.0), jnp.where(right, second, 0.0)], axis=1)],
                axis=0).astype(BF16)
            s = _nt_dot(q, k2_ref[pl.ds(n0, MOBA_BLOCK), :])
            m = (jnp.max(s, axis=1, keepdims=True) * (scale * LOG2_E)).astype(BF16).astype(F32)
            p = jnp.exp2(s * (scale * LOG2_E) - m).astype(BF16)
            part = jnp.dot(p, va_ref[pl.ds(n0, MOBA_BLOCK), :], preferred_element_type=F32)
            o_ref[pl.ds(u * GROUP_TILE, GROUP_TILE), :] = _pack_partial(part, m)

    @pl.when(step == last_step)
    def _():
        row0 = pl.multiple_of((null_tile - last_step * TILES_PER_STEP) * GROUP_TILE, GROUP_TILE)
        o_ref[pl.ds(row0, GROUP_TILE), :] = _null_partial(GROUP_TILE)


def _moba_group(qg, k, v, b, tblk, ntiles, max_tiles):
    _, H, S, Dh = k.shape
    G = H
    n_steps = max_tiles // TILES_PER_STEP
    step_rows = TILES_PER_STEP * GROUP_TILE

    def tile_idx(g, s, tblk, nt):
        return (g * n_steps + jnp.minimum(s, nt[g] // TILES_PER_STEP), 0)

    def kv_idx(g, s, *_):
        return (b, jnp.minimum(g + jnp.minimum(s, 1), G - 1), 0, 0)

    return pl.pallas_call(
        functools.partial(_moba_group_kernel, max_tiles=max_tiles),
        grid_spec=pltpu.PrefetchScalarGridSpec(
            num_scalar_prefetch=2,
            grid=(G, n_steps),
            in_specs=[pl.BlockSpec((step_rows // 2, LANES), tile_idx),
                      pl.BlockSpec((1, 1, S, Dh), kv_idx),
                      pl.BlockSpec((1, 1, S, Dh), kv_idx)],
            out_specs=pl.BlockSpec((step_rows, LANES), tile_idx),
            scratch_shapes=[pltpu.VMEM((S, VA_W), BF16), pltpu.VMEM((S, 2 * Dh), BF16)],
        ),
        out_shape=jax.ShapeDtypeStruct((G * max_tiles * GROUP_TILE, LANES), I32),
        compiler_params=_cparams(("arbitrary", "arbitrary")),
        name="moba_group",
    )(tblk, ntiles, qg, k, v)


def _moba_merge_kernel(*refs, n_skip):
    q_ref, k_ref, v_ref, sg_ref, x_ref, outw_ref = refs[n_skip:n_skip + 6]
    og_refs, out_ref, y_ref = refs[n_skip + 6:-2], refs[-2], refs[-1]
    scale = HEAD_DIM ** -0.5
    qi = lax.broadcasted_iota(I32, (MOBA_BLOCK, MOBA_BLOCK), 0)
    ki = lax.broadcasted_iota(I32, (MOBA_BLOCK, MOBA_BLOCK), 1)
    lane = lax.broadcasted_iota(I32, (MOBA_BLOCK, HEAD_DIM), 1)
    spread = jnp.where(lax.broadcasted_iota(I32, (HEAD_DIM, HEAD_DIM), 0) == L_LANE, 1.0, 0.0).astype(BF16)
    for u, t in [(u, t) for t in range(MERGE_TILES) for u in range(len(og_refs))]:
        og_ref = og_refs[u]
        rows = pl.ds(t * MOBA_BLOCK, MOBA_BLOCK)
        q = q_ref[0, u, rows, :].astype(BF16)
        s = _nt_dot(q, k_ref[0, u, rows, :])
        s = jnp.where(ki <= qi, s, NEG_INF)
        m_raw = jnp.max(s, axis=1, keepdims=True)
        p = jnp.exp2((s - m_raw) * (scale * LOG2_E)).astype(BF16)
        own = jnp.dot(p, _augment_v(v_ref[0, u, rows, :]), preferred_element_type=F32)
        parts = [_unpack_partial(og_ref[r, rows, :]) for r in range(MOBA_TOPK)]
        ms = [jnp.dot(jnp.where(lane == L_LANE, hi, 0.0).astype(BF16), spread, preferred_element_type=F32)
              for _, hi in parts]
        m_own = jnp.broadcast_to(m_raw * (scale * LOG2_E), lane.shape)
        m_all = m_own
        for m in ms:
            m_all = jnp.maximum(m_all, m)
        w_own = jnp.exp2(m_own - m_all)
        tot_lo = w_own * own[:, :HEAD_DIM]
        tot_hi = w_own * own[:, HEAD_DIM:]
        for (lo, hi), m in zip(parts, ms):
            w = jnp.exp2(m - m_all)
            tot_lo = tot_lo + w * lo
            tot_hi = tot_hi + w * hi
        acc = jnp.where(lane < HALF, tot_lo, pltpu.roll(tot_hi, HALF, 1))
        cols = pl.ds(u * HEAD_DIM, HEAD_DIM)
        y_ref[rows, cols] = ((acc / tot_lo[:, L_LANE:L_LANE + 1]) * sg_ref[rows, cols]).astype(BF16)

    out_ref[...] = x_ref[...] + jnp.dot(y_ref[...], outw_ref[...], preferred_element_type=F32)


def _moba_merge(q, k, v, sg, og, x2d, out_w, out_batch, n_out_batch, o_prev):
    _, H, S, Dh = q.shape
    D = x2d.shape[1]
    b = 0
    hps = HEADS_PER_STEP
    assert hps == H
    rows = MERGE_TILES * MOBA_BLOCK
    ns = S // rows
    og_spec = lambda u: pl.BlockSpec((MOBA_TOPK, rows, LANES),
                                     lambda _, hp, j: (0, (hp * hps + u) * ns + j, 0))
    prev_specs = [] if o_prev is None else [pl.BlockSpec(memory_space=pl.ANY)]
    prev_args = [] if o_prev is None else [o_prev]
    return pl.pallas_call(
        functools.partial(_moba_merge_kernel, n_skip=len(prev_args)),
        grid=(1, H // hps, ns),
        in_specs=prev_specs
                 + [pl.BlockSpec((1, hps, rows, Dh), lambda _, hp, j: (b, hp, j, 0)),
                    pl.BlockSpec((1, hps, rows, Dh), lambda _, hp, j: (b, hp, j, 0)),
                    pl.BlockSpec((1, hps, rows, Dh), lambda _, hp, j: (b, hp, j, 0)),
                    pl.BlockSpec((rows, hps * Dh), lambda _, hp, j: (b * ns + j, hp)),
                    pl.BlockSpec((rows, D), lambda _, hp, j: (b * ns + j, 0)),
                    pl.BlockSpec((H * Dh, D), lambda _, hp, j: (0, 0))]
                 + [og_spec(u) for u in range(hps)],
        out_specs=pl.BlockSpec((rows, D), lambda _, hp, j: (out_batch * ns + j, 0)),
        out_shape=jax.ShapeDtypeStruct((n_out_batch * S, D), F32),
        scratch_shapes=[pltpu.VMEM((rows, H * Dh), BF16)],
        input_output_aliases={} if o_prev is None else {0: 0},
        compiler_params=_cparams(("arbitrary", "arbitrary", "arbitrary")),
        name="moba_merge",
    )(*prev_args, q, k, v, sg, x2d, out_w, *([og] * hps))


def _moba_layer(qp, q16, k, v, kmean, sg, x2d, out_w, out_batch, n_out_batch, o_prev):
    _, H, S, Dh = q16.shape
    nb = S // MOBA_BLOCK
    max_tiles = (MOBA_TOPK * S) // GROUP_TILE + nb + 1
    max_tiles = -(-max_tiles // TILES_PER_STEP) * TILES_PER_STEP
    slab_rows = max_tiles * GROUP_TILE

    q_rows = qp.reshape(H * nb, IDX_CHUNK, SC_HALVES, HALF)
    rt, counts = _moba_route(q16, kmean, 0)
    pos, qpos, tblk, ntiles = _moba_layout(rt, counts.reshape(-1), nb, max_tiles)
    qg = _sc_scatter_rows(q_rows, 0, qpos, H * slab_rows)
    qg = qg.reshape(H * slab_rows // 2, LANES)
    parts = _moba_group(qg, k, v, 0, tblk, ntiles, max_tiles)
    og = _sc_gather_rows(parts, pos)
    return _moba_merge(q16, k, v, sg, og, x2d, out_w, out_batch, n_out_batch, o_prev)


def _rope_tables(S):
    inv = np.float32(ROPE_THETA) ** (-np.arange(HALF, dtype=np.float32) / np.float32(HALF))
    ang = np.arange(S, dtype=np.float32)[:, None] * inv[None, :].astype(np.float32)
    cos, sin = np.cos(ang).astype(np.float32), np.sin(ang).astype(np.float32)
    return (jnp.asarray(np.concatenate([cos, cos], axis=-1)),
            jnp.asarray(np.concatenate([-sin, sin], axis=-1)))


def kernel(x, a_norm, a_in_w, a_conv_w, a_conv_b, a_r_w, a_r_b, a_i_w, a_i_b, a_lambda, a_out_w,
           kv_norm, kv_w, k_norm, b_norm, b_in_w, q_norm, b_out_w):
    B, S, D = x.shape
    assert S % ROW_TILE == 0 and ROW_TILE % MOBA_BLOCK == 0
    assert D == N_RG_BLOCKS * RG_BW == N_HEADS * HEAD_DIM
    assert S % (ROUTE_TILES * MOBA_BLOCK) == 0 and S % (MERGE_TILES * MOBA_BLOCK) == 0
    assert S % V_BUILD_ROWS == 0 and S <= ROUTE_BLK_UNIT
    assert (N_HEADS * (S // MOBA_BLOCK)) % SC_WORKERS == 0
    n_a, n_b = a_in_w.shape[0], b_in_w.shape[0]
    assert n_a >= 1 and n_b >= 1
    x2d = x.reshape(B * S, D)
    row = lambda v: v.reshape(1, -1)
    cosf, sinf = _rope_tables(S)

    out = None
    for b in range(B):
        xs, b_in = x2d, b
        for l in range(n_a):
            c_rnn = a_in_w.shape[2] // 2
            wri = (0.5 * jnp.concatenate([a_r_w[l], a_i_w[l]], axis=-1)).astype(BF16)
            in_w = jnp.concatenate([a_in_w[l][:, :c_rnn], 0.5 * a_in_w[l][:, c_rnn:]], axis=-1).astype(BF16)
            xs = _hawk_layer(xs, b_in, S, row(a_norm[l]), in_w, a_conv_w[l], row(a_conv_b[l]),
                             wri, row(0.5 * a_r_b[l]), row(0.5 * a_i_b[l]), row(a_lambda[l]),
                             a_out_w[l].astype(BF16))
            b_in = 0
        for jl in range(n_b):
            q_args = (row(b_norm[jl]), b_in_w[jl].astype(BF16), row(q_norm[jl]))
            if jl == 0:
                k, v, kmean, qp, q16, sg = _kvq_proj(xs, 1, S, row(kv_norm), kv_w.astype(BF16), row(k_norm),
                                                    *q_args, cosf, sinf)
                kmean = jnp.transpose(kmean, (0, 2, 1, 3))
            else:
                qp, q16, sg = _q_proj(xs, 1, S, *q_args, cosf, sinf)
            w_out = b_out_w[jl].astype(BF16)
            if jl == n_b - 1:
                out = xs = _moba_layer(qp, q16, k, v, kmean, sg, xs, w_out, b, B, out)
            else:
                xs = _moba_layer(qp, q16, k, v, kmean, sg, xs, w_out, 0, 1, None)
    return out.reshape(B, S, D)
```

```python
import functools

import jax
import jax.numpy as jnp
import numpy as np
from jax import lax
from jax.experimental import pallas as pl
from jax.experimental.pallas import tpu as pltpu
from jax.experimental.pallas import tpu_sc as plsc

N_HEADS = 8
HEAD_DIM = 128
MOBA_BLOCK = 256
MOBA_TOPK = 3
CONV_WIDTH = 4
N_RG_BLOCKS = 8
RG_BW = 128
RG_C = 8.0
ROPE_THETA = 10000.0
EPS = 1e-6
NEG_INF = -1e30
LOG2_E = 1.4426950408889634
TINY = 1e-30

SUBLANES = 8
LANES = 128
ROW_TILE = 512
SUB_ROWS = 128
VMEM_LIMIT = 56 * 1024 * 1024

GROUP_TILE = MOBA_BLOCK
TILES_PER_STEP = 16
V_BUILD_ROWS = 1024
HEADS_PER_STEP = 8
MERGE_TILES = 2
ROUTE_TILES = 16
ROUTE_BLK_SHIFT = 16
ROUTE_BLK_UNIT = 1 << ROUTE_BLK_SHIFT
HALF = HEAD_DIM // 2
VA_W = 2 * HEAD_DIM
L_LANE = HALF
HI16_MASK = -65536
BF16_HALF_ULP = 0x8000
NEG_INF_BITS = int(np.float32(NEG_INF).view(np.int32))

SC_CORES = 2
SC_SUBCORES = 16
SC_WORKERS = SC_CORES * SC_SUBCORES
IDX_CHUNK = 128
SC_HALVES = MOBA_BLOCK // IDX_CHUNK

F32 = jnp.float32
BF16 = jnp.bfloat16
I32 = jnp.int32


def _cparams(sem):
    return pltpu.CompilerParams(dimension_semantics=sem, vmem_limit_bytes=VMEM_LIMIT)


def _rms_norm(x, g):
    ms = jnp.mean(x * x, axis=-1, keepdims=True)
    return (x * lax.rsqrt(ms + EPS)) * g


def _silu(x):
    hx = 0.5 * x
    return hx * jnp.tanh(hx) + hx


def _nt_dot(a, b):
    return lax.dot_general(a, b, (((1,), (1,)), ((), ())), preferred_element_type=F32)


def _head_norm_rope(z, g, cosf, sinf):
    outs = []
    for h in range(N_HEADS):
        zh = z[:, h * HEAD_DIM:(h + 1) * HEAD_DIM]
        zh = _rms_norm(zh, g)
        outs.append(zh * cosf + pltpu.roll(zh, HEAD_DIM // 2, 1) * sinf)
    return outs


def _hawk_kernel(x_ref, g_ref, inw_ref, cw_ref, cb_ref, wri_ref, rb_ref, ib_ref, lam_ref, outw_ref,
                 o_ref, gate_ref, xpad_ref, a_ref, b_ref, hc_ref):
    tm, C = gate_ref.shape
    s = pl.program_id(1)

    @pl.when(s == 0)
    def _():
        xpad_ref[pl.ds(0, SUBLANES), :] = jnp.zeros((SUBLANES, xpad_ref.shape[1]), F32)
        hc_ref[...] = jnp.zeros_like(hc_ref)

    h = _rms_norm(x_ref[...], g_ref[...]).astype(BF16)
    gate_ref[...] = jnp.dot(h, inw_ref[:, C:], preferred_element_type=F32)

    xpad_ref[pl.ds(SUBLANES, tm), :] = jnp.dot(h, inw_ref[:, :C], preferred_element_type=F32)
    cw = cw_ref[...]
    xc = xpad_ref[pl.ds(SUBLANES - (CONV_WIDTH - 1), tm), :] * cw[0:1, :]
    for k in range(1, CONV_WIDTH):
        xc = xc + xpad_ref[pl.ds(SUBLANES - (CONV_WIDTH - 1) + k, tm), :] * cw[k:k + 1, :]
    xc = xc + cb_ref[...]
    xpad_ref[pl.ds(0, SUBLANES), :] = xpad_ref[pl.ds(tm, SUBLANES), :]

    lam = lam_ref[...]
    sp = jnp.maximum(-lam, 0.0) + jnp.log1p(jnp.exp(-jnp.abs(lam)))
    half_log2a = (-0.5 * RG_C * LOG2_E) * sp
    xc16 = xc.astype(BF16)
    for g in range(N_RG_BLOCKS):
        lo = g * RG_BW
        z = jnp.dot(xc16[:, lo:lo + RG_BW], wri_ref[g], preferred_element_type=F32)
        tr = jnp.tanh(z[:, :RG_BW] + rb_ref[:, lo:lo + RG_BW])
        ti = jnp.tanh(z[:, RG_BW:] + ib_ref[:, lo:lo + RG_BW])
        a = jnp.exp2(tr * half_log2a[:, lo:lo + RG_BW] + half_log2a[:, lo:lo + RG_BW])
        om = 1.0 - a * a
        mult = om * lax.rsqrt(jnp.maximum(om, TINY))
        hx = 0.5 * xc[:, lo:lo + RG_BW]
        a_ref[:, lo:lo + RG_BW] = a
        b_ref[:, lo:lo + RG_BW] = mult * (hx * ti + hx)

    row = lax.broadcasted_iota(I32, (SUBLANES, a_ref.shape[1]), 0)

    def scan_body(c, hc):
        r0 = pl.multiple_of(c * SUBLANES, SUBLANES)
        a = a_ref[pl.ds(r0, SUBLANES), :]
        b = b_ref[pl.ds(r0, SUBLANES), :]
        for k in (1, 2, 4):
            keep = row >= k
            b = jnp.where(keep, a * pltpu.roll(b, k, 0) + b, b)
            a = jnp.where(keep, a * pltpu.roll(a, k, 0), a)
        h = a * hc + b
        b_ref[pl.ds(r0, SUBLANES), :] = h
        return h[SUBLANES - 1:SUBLANES, :]

    hc = lax.fori_loop(0, tm // SUBLANES, scan_body, hc_ref[0:1, :])
    hc_ref[0:1, :] = hc

    gh = gate_ref[...]
    y = (b_ref[...] * (gh * jnp.tanh(gh) + gh)).astype(BF16)
    o_ref[...] = x_ref[...] + jnp.dot(y, outw_ref[...], preferred_element_type=F32)


def _hawk_layer(x2d, b_in, S, g, in_w, cw, cb, wri, rb, ib, lam, out_w):
    D = x2d.shape[1]
    C = in_w.shape[1] // 2
    tm = ROW_TILE
    ns = S // tm
    vec = lambda n: pl.BlockSpec((1, n), lambda b, s: (0, 0))
    return pl.pallas_call(
        _hawk_kernel,
        grid=(1, ns),
        in_specs=[pl.BlockSpec((tm, D), lambda b, s: (b_in * ns + s, 0)),
                  vec(D),
                  pl.BlockSpec((D, 2 * C), lambda b, s: (0, 0)),
                  pl.BlockSpec((CONV_WIDTH, C), lambda b, s: (0, 0)),
                  vec(C),
                  pl.BlockSpec((N_RG_BLOCKS, RG_BW, 2 * RG_BW), lambda b, s: (0, 0, 0)),
                  vec(C), vec(C), vec(C),
                  pl.BlockSpec((C, D), lambda b, s: (0, 0))],
        out_specs=pl.BlockSpec((tm, D), lambda b, s: (s, 0)),
        out_shape=jax.ShapeDtypeStruct((S, D), F32),
        scratch_shapes=[pltpu.VMEM((tm, C), F32),
                        pltpu.VMEM((tm + SUBLANES, C), F32),
                        pltpu.VMEM((tm, C), F32),
                        pltpu.VMEM((tm, C), F32),
                        pltpu.VMEM((SUBLANES, C), F32)],
        compiler_params=_cparams(("arbitrary", "arbitrary")),
        name="hawk_layer",
    )(x2d, g, in_w, cw, cb, wri, rb, ib, lam, out_w)


def _emit_q(u, qg, cosf, sinf, t, qp_ref, q16_ref, sg_ref):
    assert SUB_ROWS == IDX_CHUNK
    d_attn = N_HEADS * HEAD_DIM
    rows = pl.ds(t * SUB_ROWS, SUB_ROWS)
    prow = pl.ds((t // 2) * SUB_ROWS, SUB_ROWS)
    qs = _head_norm_rope(u[:, :d_attn], qg, cosf, sinf)
    for hd in range(N_HEADS):
        q16 = qs[hd].astype(BF16)
        q16_ref[0, hd, rows, :] = q16
        r = _bits(q16.astype(F32))
        swapped = pltpu.roll(r, HALF, 1)
        if t % 2 == 0:
            w = (swapped & HI16_MASK) | lax.shift_right_logical(r, 16)
            qp_ref[0, hd, prow, pl.ds(0, HALF)] = w[:, :HALF]
        else:
            w = (r & HI16_MASK) | lax.shift_right_logical(swapped, 16)
            qp_ref[0, hd, prow, pl.ds(HALF, HALF)] = w[:, HALF:]
    sg_ref[rows, :] = _silu(u[:, d_attn:])


def _kvq_proj_kernel(x_ref, gkv_ref, wkv_ref, kg_ref, gq_ref, wq_ref, qg_ref, cos_ref, sin_ref,
                     k_ref, v_ref, km_ref, q_ref, q16_ref, sg_ref):
    tm = x_ref.shape[0]
    d_attn = N_HEADS * HEAD_DIM
    ksum = {}
    for t in range(tm // SUB_ROWS):
        rows = pl.ds(t * SUB_ROWS, SUB_ROWS)
        x = x_ref[rows, :]
        xn = x * lax.rsqrt(jnp.mean(x * x, axis=-1, keepdims=True) + EPS)
        kv = jnp.dot((xn * gkv_ref[...]).astype(BF16), wkv_ref[...], preferred_element_type=F32)
        u = jnp.dot((xn * gq_ref[...]).astype(BF16), wq_ref[...], preferred_element_type=F32)
        cosf, sinf = cos_ref[rows, :], sin_ref[rows, :]
        ks = _head_norm_rope(kv[:, :d_attn], kg_ref[...], cosf, sinf)
        for hd in range(N_HEADS):
            k_ref[0, hd, rows, :] = ks[hd].astype(BF16)
            v_ref[0, hd, rows, :] = kv[:, d_attn + hd * HEAD_DIM:d_attn + (hd + 1) * HEAD_DIM].astype(BF16)
            key = (hd, (t * SUB_ROWS) // MOBA_BLOCK)
            part = jnp.sum(ks[hd], axis=0, keepdims=True)
            ksum[key] = part if key not in ksum else ksum[key] + part
        _emit_q(u, qg_ref[...], cosf, sinf, t, q_ref, q16_ref, sg_ref)
    for (hd, r), total in ksum.items():
        km_ref[0, r, pl.ds(hd, 1), :] = total * (1.0 / MOBA_BLOCK)


def _kvq_proj(x2d, B, S, gkv, wkv, kg, gq, wq, qg, cosf, sinf):
    T, D = x2d.shape
    tm = ROW_TILE
    ns = S // tm
    nb = S // MOBA_BLOCK
    rpt = tm // MOBA_BLOCK
    d_attn = N_HEADS * HEAD_DIM
    vec = lambda n: pl.BlockSpec((1, n), lambda b, s: (0, 0))
    wspec = pl.BlockSpec((D, 2 * d_attn), lambda b, s: (0, 0))
    rope = pl.BlockSpec((tm, HEAD_DIM), lambda b, s: (s, 0))
    heads = pl.BlockSpec((1, N_HEADS, tm, HEAD_DIM), lambda b, s: (b, 0, s, 0))
    hshape = lambda dt: jax.ShapeDtypeStruct((B, N_HEADS, S, HEAD_DIM), dt)
    return pl.pallas_call(
        _kvq_proj_kernel,
        grid=(B, ns),
        in_specs=[pl.BlockSpec((tm, D), lambda b, s: (b * ns + s, 0)),
                  vec(D), wspec, vec(HEAD_DIM), vec(D), wspec, vec(HEAD_DIM), rope, rope],
        out_specs=[heads, heads,
                   pl.BlockSpec((1, rpt, N_HEADS, HEAD_DIM), lambda b, s: (b, s, 0, 0)),
                   pl.BlockSpec((1, N_HEADS, tm // 2, LANES), lambda b, s: (b, 0, s, 0)), heads,
                   pl.BlockSpec((tm, d_attn), lambda b, s: (b * ns + s, 0))],
        out_shape=[hshape(BF16), hshape(BF16),
                   jax.ShapeDtypeStruct((B, nb, N_HEADS, HEAD_DIM), F32),
                   jax.ShapeDtypeStruct((B, N_HEADS, S // 2, LANES), I32), hshape(BF16),
                   jax.ShapeDtypeStruct((T, d_attn), F32)],
        compiler_params=_cparams(("arbitrary", "arbitrary")),
        name="kvq_proj",
    )(x2d, gkv, wkv, kg, gq, wq, qg, cosf, sinf)


def _q_proj_kernel(x_ref, g_ref, w_ref, qg_ref, cos_ref, sin_ref, q_ref, q16_ref, sg_ref):
    for t in range(x_ref.shape[0] // SUB_ROWS):
        rows = pl.ds(t * SUB_ROWS, SUB_ROWS)
        h = _rms_norm(x_ref[rows, :], g_ref[...]).astype(BF16)
        u = jnp.dot(h, w_ref[...], preferred_element_type=F32)
        _emit_q(u, qg_ref[...], cos_ref[rows, :], sin_ref[rows, :], t, q_ref, q16_ref, sg_ref)


def _q_proj(x2d, B, S, g, w, qg, cosf, sinf):
    T, D = x2d.shape
    tm = ROW_TILE
    ns = S // tm
    d_attn = N_HEADS * HEAD_DIM
    heads = pl.BlockSpec((1, N_HEADS, tm, HEAD_DIM), lambda b, s: (b, 0, s, 0))
    return pl.pallas_call(
        _q_proj_kernel,
        grid=(B, ns),
        in_specs=[pl.BlockSpec((tm, D), lambda b, s: (b * ns + s, 0)),
                  pl.BlockSpec((1, D), lambda b, s: (0, 0)),
                  pl.BlockSpec((D, 2 * d_attn), lambda b, s: (0, 0)),
                  pl.BlockSpec((1, HEAD_DIM), lambda b, s: (0, 0)),
                  pl.BlockSpec((tm, HEAD_DIM), lambda b, s: (s, 0)),
                  pl.BlockSpec((tm, HEAD_DIM), lambda b, s: (s, 0))],
        out_specs=[pl.BlockSpec((1, N_HEADS, tm // 2, LANES), lambda b, s: (b, 0, s, 0)), heads,
                   pl.BlockSpec((tm, d_attn), lambda b, s: (b * ns + s, 0))],
        out_shape=[jax.ShapeDtypeStruct((B, N_HEADS, S // 2, LANES), I32),
                   jax.ShapeDtypeStruct((B, N_HEADS, S, HEAD_DIM), BF16),
                   jax.ShapeDtypeStruct((T, d_attn), F32)],
        compiler_params=_cparams(("arbitrary", "arbitrary")),
        name="q_proj",
    )(x2d, g, w, qg, cosf, sinf)


def _moba_route_kernel(q_ref, km_ref, rt_ref, cnt_ref, run_ref):
    step = pl.program_id(2)
    nb = km_ref.shape[2]

    @pl.when(step == 0)
    def _():
        run_ref[...] = jnp.zeros_like(run_ref)

    km = km_ref[0, 0].astype(BF16)
    blk = lax.broadcasted_iota(I32, (nb, MOBA_BLOCK), 0)
    qa = lax.broadcasted_iota(I32, (MOBA_BLOCK, MOBA_BLOCK), 0)
    qb = lax.broadcasted_iota(I32, (MOBA_BLOCK, MOBA_BLOCK), 1)
    earlier = jnp.where(qa < qb, 1.0, 0.0).astype(BF16)
    run = run_ref[...]
    for u in range(ROUTE_TILES):
        j = step * ROUTE_TILES + u
        g = _nt_dot(km, q_ref[0, 0, pl.ds(u * MOBA_BLOCK, MOBA_BLOCK), :].astype(BF16))
        g = jnp.where(blk < j, g, -jnp.inf)
        hits, ids = [], []
        for _ in range(MOBA_TOPK):
            mx = jnp.max(g, axis=0, keepdims=True)
            idx = jnp.min(jnp.where(g == mx, blk, nb), axis=0, keepdims=True)
            hit = blk == idx
            ok = mx > -jnp.inf
            hits.append(jnp.where(hit & ok, 1.0, 0.0))
            ids.append(jnp.where(ok, idx, -1))
            g = jnp.where(hit, -jnp.inf, g)
        sel = hits[0] + hits[1] + hits[2]

        rank = jnp.dot(sel.astype(BF16), earlier, preferred_element_type=F32)
        dest = run + rank
        run = run + jnp.sum(sel, axis=1, keepdims=True)
        for r in range(MOBA_TOPK):
            loc = jnp.sum(hits[r] * dest, axis=0, keepdims=True).astype(I32)
            rt_ref[u, pl.ds(r, 1), :] = jnp.where(ids[r] >= 0, ids[r] * ROUTE_BLK_UNIT + loc, -1)
        rt_ref[u, pl.ds(MOBA_TOPK, SUBLANES - MOBA_TOPK), :] = jnp.full(
            (SUBLANES - MOBA_TOPK, MOBA_BLOCK), -1, I32)
    run_ref[...] = run
    cnt_ref[0] = run.astype(I32)


def _moba_route(q, kmean, b):
    _, H, S, Dh = q.shape
    nb = S // MOBA_BLOCK
    rows = ROUTE_TILES * MOBA_BLOCK
    nsteps = nb // ROUTE_TILES
    return pl.pallas_call(
        _moba_route_kernel,
        grid=(1, H, nsteps),
        in_specs=[pl.BlockSpec((1, 1, rows, Dh), lambda _, h, j: (b, h, j, 0)),
                  pl.BlockSpec((1, 1, nb, Dh), lambda _, h, j: (b, h, 0, 0))],
        out_specs=[pl.BlockSpec((ROUTE_TILES, SUBLANES, MOBA_BLOCK), lambda _, h, j: (h * nsteps + j, 0, 0)),
                   pl.BlockSpec((1, nb, 1), lambda _, h, j: (h, 0, 0))],
        out_shape=[jax.ShapeDtypeStruct((H * nb, SUBLANES, MOBA_BLOCK), I32),
                   jax.ShapeDtypeStruct((H, nb, 1), I32)],
        scratch_shapes=[pltpu.VMEM((nb, 1), F32)],
        compiler_params=_cparams(("arbitrary", "arbitrary", "arbitrary")),
        name="moba_route",
    )(q, kmean)


def _moba_layout_kernel(cnt_ref, rt_ref, pos_ref, qpos_ref, tblk_ref, nt_ref, base_ref, *, max_tiles):
    g = pl.program_id(0)
    nb = rt_ref.shape[0]
    slab_rows = max_tiles * GROUP_TILE

    def per_block(n, first_tile):
        count = cnt_ref[g * nb + n]
        tiles = (count + (GROUP_TILE - 1)) // GROUP_TILE
        base_ref[n] = first_tile * GROUP_TILE

        def per_tile(i, carry):
            tblk_ref[g * max_tiles + first_tile + i] = n
            return carry

        lax.fori_loop(0, tiles, per_tile, 0)
        return first_tile + tiles

    used = lax.fori_loop(0, nb, per_block, 0)
    nt_ref[g] = used
    null_row0 = used * GROUP_TILE

    def unused_tile(t, carry):
        tblk_ref[g * max_tiles + t] = nb - 1
        return carry

    lax.fori_loop(used, max_tiles, unused_tile, 0)

    code = rt_ref[...].reshape(nb * SUBLANES, MOBA_BLOCK)
    blk = lax.shift_right_arithmetic(code, ROUTE_BLK_SHIFT)
    base = jnp.zeros_like(code)
    for n in range(nb):
        base = jnp.where(blk == n, base_ref[n], base)
    lane = lax.broadcasted_iota(I32, code.shape, 1)
    row = jnp.where(code >= 0, base + (code & (ROUTE_BLK_UNIT - 1)), null_row0 + lane) + g * slab_rows
    pos_ref[...] = row.reshape(pos_ref.shape)
    p = row & (GROUP_TILE - 1)
    assert GROUP_TILE == 2 * IDX_CHUNK
    slot = ((p & (IDX_CHUNK - 1)) << 1) | lax.shift_right_logical(p, IDX_CHUNK.bit_length() - 1)
    qpos_ref[...] = ((row - p) | slot).reshape(qpos_ref.shape)


def _moba_layout(rt, counts, nb, max_tiles):
    G = rt.shape[0] // nb
    smem = pl.BlockSpec(memory_space=pltpu.SMEM)
    return pl.pallas_call(
        functools.partial(_moba_layout_kernel, max_tiles=max_tiles),
        grid_spec=pltpu.PrefetchScalarGridSpec(
            num_scalar_prefetch=1,
            grid=(G,),
            in_specs=[pl.BlockSpec((nb, SUBLANES, MOBA_BLOCK), lambda g, cnt: (g, 0, 0))],
            out_specs=[pl.BlockSpec((nb, SUBLANES, MOBA_BLOCK), lambda g, cnt: (g, 0, 0)),
                       pl.BlockSpec((nb, SUBLANES, MOBA_BLOCK), lambda g, cnt: (g, 0, 0)), smem, smem],
            scratch_shapes=[pltpu.SMEM((nb,), I32)],
        ),
        out_shape=[jax.ShapeDtypeStruct((G * nb, SUBLANES, MOBA_BLOCK), I32),
                   jax.ShapeDtypeStruct((G * nb, SUBLANES, MOBA_BLOCK), I32),
                   jax.ShapeDtypeStruct((G * max_tiles,), I32),
                   jax.ShapeDtypeStruct((G,), I32)],
        compiler_params=_cparams(("arbitrary",)),
        name="moba_layout",
    )(counts, rt)


def _sc_mesh():
    return plsc.VectorSubcoreMesh(core_axis_name="c", subcore_axis_name="s")


def _sc_load_indices(pos_hbm, idx_v, t0, n_tiles):
    for c in range(SC_HALVES):
        pltpu.sync_copy(pos_hbm.at[pl.ds(t0, n_tiles), pl.ds(0, MOBA_TOPK), pl.ds(c * IDX_CHUNK, IDX_CHUNK)],
                        idx_v.at[c])


def _sc_scatter_rows(src, src_tile0, pos, n_out_rows):
    D = src.shape[-1]
    n_tiles = pos.shape[0]
    per_w = n_tiles // SC_WORKERS
    assert per_w * SC_WORKERS == n_tiles and src.shape[1:3] == (IDX_CHUNK, SC_HALVES)

    @functools.partial(
        pl.kernel, mesh=_sc_mesh(),
        out_type=jax.ShapeDtypeStruct((n_out_rows, D), src.dtype),
        scratch_types=[pltpu.VMEM((SC_HALVES, per_w, MOBA_TOPK, IDX_CHUNK), I32),
                       pltpu.VMEM((SC_HALVES, IDX_CHUNK, D), src.dtype),
                       pltpu.SemaphoreType.DMA((SC_HALVES,)),
                       pltpu.SemaphoreType.DMA((SC_HALVES,))],
        compiler_params=pltpu.CompilerParams(use_tc_tiling_on_sc=False),
        name="sc_scatter_rows",
    )
    def k(src_hbm, pos_hbm, out_hbm, idx_v, rows_v, load_sem, scat_sem):
        t0 = (lax.axis_index("s") * SC_CORES + lax.axis_index("c")) * per_w
        _sc_load_indices(pos_hbm, idx_v, t0, per_w)

        def load(i, c):
            return pltpu.make_async_copy(src_hbm.at[src_tile0 + t0 + i, :, c, :], rows_v.at[c], load_sem.at[c])

        def scatters(i, c):
            return [pltpu.make_async_copy(rows_v.at[c], out_hbm.at[idx_v.at[c, i, r]], scat_sem.at[c])
                    for r in range(MOBA_TOPK)]

        load(0, 0).start()

        @pl.loop(0, per_w)
        def _(i):
            for c in range(SC_HALVES):
                pi, pc = (i, 0) if c else (i - 1, 1)
                ni, nc = (i + 1, 0) if c else (i, 1)
                load(i, c).wait()
                for cp in scatters(i, c):
                    cp.start()

                @pl.when(pi >= 0)
                def _():
                    for cp in scatters(pi, pc):
                        cp.wait()

                @pl.when(ni < per_w)
                def _():
                    load(ni, nc).start()

        for cp in scatters(per_w - 1, SC_HALVES - 1):
            cp.wait()

    return k(src, pos)


def _sc_gather_rows(table, pos):
    R, D = table.shape
    n_tiles = pos.shape[0]
    N = n_tiles * MOBA_BLOCK
    per_w = n_tiles // SC_WORKERS
    assert per_w * SC_WORKERS == n_tiles

    items = SC_HALVES * MOBA_TOPK
    n_buf = 2
    assert items % n_buf == 0

    @functools.partial(
        pl.kernel, mesh=_sc_mesh(),
        out_type=jax.ShapeDtypeStruct((MOBA_TOPK, N, D), table.dtype),
        scratch_types=[pltpu.VMEM((SC_HALVES, per_w, MOBA_TOPK, IDX_CHUNK), I32),
                       pltpu.VMEM((n_buf, IDX_CHUNK, D), table.dtype),
                       pltpu.SemaphoreType.DMA((n_buf,)),
                       pltpu.SemaphoreType.DMA((n_buf,))],
        name="sc_gather_rows",
    )
    def k(table_hbm, pos_hbm, out_hbm, idx_v, rows_v, gat_sem, put_sem):
        t0 = (lax.axis_index("s") * SC_CORES + lax.axis_index("c")) * per_w
        _sc_load_indices(pos_hbm, idx_v, t0, per_w)

        def gather(i, j):
            c, r = divmod(j, MOBA_TOPK)
            return pltpu.make_async_copy(table_hbm.at[idx_v.at[c, i, r]], rows_v.at[j % n_buf], gat_sem.at[j % n_buf])

        def put(i, j):
            c, r = divmod(j, MOBA_TOPK)
            row0 = pl.multiple_of((t0 + i) * MOBA_BLOCK + c * IDX_CHUNK, IDX_CHUNK)
            return pltpu.make_async_copy(rows_v.at[j % n_buf], out_hbm.at[r, pl.ds(row0, IDX_CHUNK)],
                                         put_sem.at[j % n_buf])

        gather(0, 0).start()

        @pl.loop(0, per_w)
        def _(i):
            for j in range(items):
                pi, pj = (i, j - 1) if j else (i - 1, items - 1)
                ni, nj = (i, j + 1) if j + 1 < items else (i + 1, 0)
                gather(i, j).wait()
                put(i, j).start()

                @pl.when(pi >= 0)
                def _():
                    put(pi, pj).wait()

                @pl.when(ni < per_w)
                def _():
                    gather(ni, nj).start()

        put(per_w - 1, items - 1).wait()

    return k(table, pos)


def _bits(x):
    return lax.bitcast_convert_type(x, I32)


def _pack_partial(part, m):
    lane = lax.broadcasted_iota(I32, (part.shape[0], HEAD_DIM), 1)
    lo = _bits(part[:, :HEAD_DIM]) + BF16_HALF_ULP
    hi = _bits(jnp.where(lane == L_LANE, m, part[:, HEAD_DIM:])) + BF16_HALF_ULP
    return (hi & HI16_MASK) | lax.shift_right_logical(lo, 16)


def _null_partial(rows):
    lane = lax.broadcasted_iota(I32, (rows, HEAD_DIM), 1)
    return jnp.where(lane == L_LANE, NEG_INF_BITS & HI16_MASK, 0)


def _unpack_partial(word):
    lo = lax.bitcast_convert_type(lax.shift_left(word, 16), F32)
    hi = lax.bitcast_convert_type(word & HI16_MASK, F32)
    return lo, hi


def _augment_v(v):
    vf = v.astype(F32)
    lane = lax.broadcasted_iota(I32, vf.shape, 1)
    lo = jnp.where(lane < HALF, vf, jnp.where(lane == L_LANE, 1.0, 0.0))
    hi = jnp.where(lane < HALF, pltpu.roll(vf, HALF, 1), 0.0)
    return jnp.concatenate([lo, hi], axis=1).astype(BF16)


def _moba_group_kernel(tblk_ref, nt_ref, qg_ref, k_ref, v_ref, o_ref, va_ref, k2_ref, *, max_tiles):
    g, step = pl.program_id(0), pl.program_id(1)
    t0 = g * max_tiles + step * TILES_PER_STEP
    scale = HEAD_DIM ** -0.5
    null_tile = nt_ref[g]
    last_step = null_tile // TILES_PER_STEP

    @pl.when(step == 0)
    def _():
        def build(c, carry):
            rows = pl.ds(pl.multiple_of(c * V_BUILD_ROWS, V_BUILD_ROWS), V_BUILD_ROWS)
            va_ref[rows, :] = _augment_v(v_ref[0, 0, rows, :])
            kf = k_ref[0, 0, rows, :].astype(F32)
            lane = lax.broadcasted_iota(I32, kf.shape, 1)
            swapped = pltpu.roll(kf, HALF, 1)
            k2_ref[rows, :] = jnp.concatenate([jnp.where(lane < HALF, kf, swapped),
                                               jnp.where(lane < HALF, swapped, kf)], axis=1).astype(BF16)
            return carry

        lax.fori_loop(0, va_ref.shape[0] // V_BUILD_ROWS, build, 0)

    @pl.when(step <= last_step)
    def _():
        half_rows = GROUP_TILE // 2
        lane = lax.broadcasted_iota(I32, (half_rows, HEAD_DIM), 1)
        for u in range(TILES_PER_STEP):
            word = qg_ref[pl.ds(u * half_rows, half_rows), :]
            first = lax.bitcast_convert_type(lax.shift_left(word, 16), F32)
            second = lax.bitcast_convert_type(word & HI16_MASK, F32)
            n0 = pl.multiple_of(tblk_ref[t0 + u] * MOBA_BLOCK, MOBA_BLOCK)
            left, right = lane < HALF, lane >= HALF
            q = jnp.concatenate(
                [jnp.concatenate([jnp.where(left, first, 0.0), jnp.where(left, second, 0.0)], axis=1),
                 jnp.concatenate([jnp.where(right, first, 0.0), jnp.where(right, second, 0.0)], axis=1)],
                axis=0).astype(BF16)
            s = _nt_dot(q, k2_ref[pl.ds(n0, MOBA_BLOCK), :])
            m = (jnp.max(s, axis=1, keepdims=True) * (scale * LOG2_E)).astype(BF16).astype(F32)
            p = jnp.exp2(s * (scale * LOG2_E) - m).astype(BF16)
            part = jnp.dot(p, va_ref[pl.ds(n0, MOBA_BLOCK), :], preferred_element_type=F32)
            o_ref[pl.ds(u * GROUP_TILE, GROUP_TILE), :] = _pack_partial(part, m)

    @pl.when(step == last_step)
    def _():
        row0 = pl.multiple_of((null_tile - last_step * TILES_PER_STEP) * GROUP_TILE, GROUP_TILE)
        o_ref[pl.ds(row0, GROUP_TILE), :] = _null_partial(GROUP_TILE)


def _moba_group(qg, k, v, b, tblk, ntiles, max_tiles):
    _, H, S, Dh = k.shape
    G = H
    n_steps = max_tiles // TILES_PER_STEP
    step_rows = TILES_PER_STEP * GROUP_TILE

    def tile_idx(g, s, tblk, nt):
        return (g * n_steps + jnp.minimum(s, nt[g] // TILES_PER_STEP), 0)

    def kv_idx(g, s, *_):
        return (b, jnp.minimum(g + jnp.minimum(s, 1), G - 1), 0, 0)

    return pl.pallas_call(
        functools.partial(_moba_group_kernel, max_tiles=max_tiles),
        grid_spec=pltpu.PrefetchScalarGridSpec(
            num_scalar_prefetch=2,
            grid=(G, n_steps),
            in_specs=[pl.BlockSpec((step_rows // 2, LANES), tile_idx),
                      pl.BlockSpec((1, 1, S, Dh), kv_idx),
                      pl.BlockSpec((1, 1, S, Dh), kv_idx)],
            out_specs=pl.BlockSpec((step_rows, LANES), tile_idx),
            scratch_shapes=[pltpu.VMEM((S, VA_W), BF16), pltpu.VMEM((S, 2 * Dh), BF16)],
        ),
        out_shape=jax.ShapeDtypeStruct((G * max_tiles * GROUP_TILE, LANES), I32),
        compiler_params=_cparams(("arbitrary", "arbitrary")),
        name="moba_group",
    )(tblk, ntiles, qg, k, v)


def _moba_merge_kernel(*refs, n_skip):
    q_ref, k_ref, v_ref, sg_ref, x_ref, outw_ref = refs[n_skip:n_skip + 6]
    og_refs, out_ref, y_ref = refs[n_skip + 6:-2], refs[-2], refs[-1]
    scale = HEAD_DIM ** -0.5
    qi = lax.broadcasted_iota(I32, (MOBA_BLOCK, MOBA_BLOCK), 0)
    ki = lax.broadcasted_iota(I32, (MOBA_BLOCK, MOBA_BLOCK), 1)
    lane = lax.broadcasted_iota(I32, (MOBA_BLOCK, HEAD_DIM), 1)
    spread = jnp.where(lax.broadcasted_iota(I32, (HEAD_DIM, HEAD_DIM), 0) == L_LANE, 1.0, 0.0).astype(BF16)
    for u, t in [(u, t) for t in range(MERGE_TILES) for u in range(len(og_refs))]:
        og_ref = og_refs[u]
        rows = pl.ds(t * MOBA_BLOCK, MOBA_BLOCK)
        q = q_ref[0, u, rows, :].astype(BF16)
        s = _nt_dot(q, k_ref[0, u, rows, :])
        s = jnp.where(ki <= qi, s, NEG_INF)
        m_raw = jnp.max(s, axis=1, keepdims=True)
        p = jnp.exp2((s - m_raw) * (scale * LOG2_E)).astype(BF16)
        own = jnp.dot(p, _augment_v(v_ref[0, u, rows, :]), preferred_element_type=F32)
        parts = [_unpack_partial(og_ref[r, rows, :]) for r in range(MOBA_TOPK)]
        ms = [jnp.dot(jnp.where(lane == L_LANE, hi, 0.0).astype(BF16), spread, preferred_element_type=F32)
              for _, hi in parts]
        m_own = jnp.broadcast_to(m_raw * (scale * LOG2_E), lane.shape)
        m_all = m_own
        for m in ms:
            m_all = jnp.maximum(m_all, m)
        w_own = jnp.exp2(m_own - m_all)
        tot_lo = w_own * own[:, :HEAD_DIM]
        tot_hi = w_own * own[:, HEAD_DIM:]
        for (lo, hi), m in zip(parts, ms):
            w = jnp.exp2(m - m_all)
            tot_lo = tot_lo + w * lo
            tot_hi = tot_hi + w * hi
        acc = jnp.where(lane < HALF, tot_lo, pltpu.roll(tot_hi, HALF, 1))
        cols = pl.ds(u * HEAD_DIM, HEAD_DIM)
        y_ref[rows, cols] = ((acc / tot_lo[:, L_LANE:L_LANE + 1]) * sg_ref[rows, cols]).astype(BF16)

    out_ref[...] = x_ref[...] + jnp.dot(y_ref[...], outw_ref[...], preferred_element_type=F32)


def _moba_merge(q, k, v, sg, og, x2d, out_w, out_batch, n_out_batch, o_prev):
    _, H, S, Dh = q.shape
    D = x2d.shape[1]
    b = 0
    hps = HEADS_PER_STEP
    assert hps == H
    rows = MERGE_TILES * MOBA_BLOCK
    ns = S // rows
    og_spec = lambda u: pl.BlockSpec((MOBA_TOPK, rows, LANES),
                                     lambda _, hp, j: (0, (hp * hps + u) * ns + j, 0))
    prev_specs = [] if o_prev is None else [pl.BlockSpec(memory_space=pl.ANY)]
    prev_args = [] if o_prev is None else [o_prev]
    return pl.pallas_call(
        functools.partial(_moba_merge_kernel, n_skip=len(prev_args)),
        grid=(1, H // hps, ns),
        in_specs=prev_specs
                 + [pl.BlockSpec((1, hps, rows, Dh), lambda _, hp, j: (b, hp, j, 0)),
                    pl.BlockSpec((1, hps, rows, Dh), lambda _, hp, j: (b, hp, j, 0)),
                    pl.BlockSpec((1, hps, rows, Dh), lambda _, hp, j: (b, hp, j, 0)),
                    pl.BlockSpec((rows, hps * Dh), lambda _, hp, j: (b * ns + j, hp)),
                    pl.BlockSpec((rows, D), lambda _, hp, j: (b * ns + j, 0)),
                    pl.BlockSpec((H * Dh, D), lambda _, hp, j: (0, 0))]
                 + [og_spec(u) for u in range(hps)],
        out_specs=pl.BlockSpec((rows, D), lambda _, hp, j: (out_batch * ns + j, 0)),
        out_shape=jax.ShapeDtypeStruct((n_out_batch * S, D), F32),
        scratch_shapes=[pltpu.VMEM((rows, H * Dh), BF16)],
        input_output_aliases={} if o_prev is None else {0: 0},
        compiler_params=_cparams(("arbitrary", "arbitrary", "arbitrary")),
        name="moba_merge",
    )(*prev_args, q, k, v, sg, x2d, out_w, *([og] * hps))


def _moba_layer(qp, q16, k, v, kmean, sg, x2d, out_w, out_batch, n_out_batch, o_prev):
    _, H, S, Dh = q16.shape
    nb = S // MOBA_BLOCK
    max_tiles = (MOBA_TOPK * S) // GROUP_TILE + nb + 1
    max_tiles = -(-max_tiles // TILES_PER_STEP) * TILES_PER_STEP
    slab_rows = max_tiles * GROUP_TILE

    q_rows = qp.reshape(H * nb, IDX_CHUNK, SC_HALVES, HALF)
    rt, counts = _moba_route(q16, kmean, 0)
    pos, qpos, tblk, ntiles = _moba_layout(rt, counts.reshape(-1), nb, max_tiles)
    qg = _sc_scatter_rows(q_rows, 0, qpos, H * slab_rows)
    qg = qg.reshape(H * slab_rows // 2, LANES)
    parts = _moba_group(qg, k, v, 0, tblk, ntiles, max_tiles)
    og = _sc_gather_rows(parts, pos)
    return _moba_merge(q16, k, v, sg, og, x2d, out_w, out_batch, n_out_batch, o_prev)


def _rope_tables(S):
    inv = np.float32(ROPE_THETA) ** (-np.arange(HALF, dtype=np.float32) / np.float32(HALF))
    ang = np.arange(S, dtype=np.float32)[:, None] * inv[None, :].astype(np.float32)
    cos, sin = np.cos(ang).astype(np.float32), np.sin(ang).astype(np.float32)
    return (jnp.asarray(np.concatenate([cos, cos], axis=-1)),
            jnp.asarray(np.concatenate([-sin, sin], axis=-1)))


def kernel(x, a_norm, a_in_w, a_conv_w, a_conv_b, a_r_w, a_r_b, a_i_w, a_i_b, a_lambda, a_out_w,
           kv_norm, kv_w, k_norm, b_norm, b_in_w, q_norm, b_out_w):
    B, S, D = x.shape
    assert S % ROW_TILE == 0 and ROW_TILE % MOBA_BLOCK == 0
    assert D == N_RG_BLOCKS * RG_BW == N_HEADS * HEAD_DIM
    assert S % (ROUTE_TILES * MOBA_BLOCK) == 0 and S % (MERGE_TILES * MOBA_BLOCK) == 0
    assert S % V_BUILD_ROWS == 0 and S <= ROUTE_BLK_UNIT
    assert (N_HEADS * (S // MOBA_BLOCK)) % SC_WORKERS == 0
    n_a, n_b = a_in_w.shape[0], b_in_w.shape[0]
    assert n_a >= 1 and n_b >= 1
    x2d = x.reshape(B * S, D)
    row = lambda v: v.reshape(1, -1)
    cosf, sinf = _rope_tables(S)

    out = None
    for b in range(B):
        xs, b_in = x2d, b
        for l in range(n_a):
            c_rnn = a_in_w.shape[2] // 2
            wri = (0.5 * jnp.concatenate([a_r_w[l], a_i_w[l]], axis=-1)).astype(BF16)
            in_w = jnp.concatenate([a_in_w[l][:, :c_rnn], 0.5 * a_in_w[l][:, c_rnn:]], axis=-1).astype(BF16)
            xs = _hawk_layer(xs, b_in, S, row(a_norm[l]), in_w, a_conv_w[l], row(a_conv_b[l]),
                             wri, row(0.5 * a_r_b[l]), row(0.5 * a_i_b[l]), row(a_lambda[l]),
                             a_out_w[l].astype(BF16))
            b_in = 0
        for jl in range(n_b):
            q_args = (row(b_norm[jl]), b_in_w[jl].astype(BF16), row(q_norm[jl]))
            if jl == 0:
                k, v, kmean, qp, q16, sg = _kvq_proj(xs, 1, S, row(kv_norm), kv_w.astype(BF16), row(k_norm),
                                                    *q_args, cosf, sinf)
                kmean = jnp.transpose(kmean, (0, 2, 1, 3))
            else:
                qp, q16, sg = _q_proj(xs, 1, S, *q_args, cosf, sinf)
            w_out = b_out_w[jl].astype(BF16)
            if jl == n_b - 1:
                out = xs = _moba_layer(qp, q16, k, v, kmean, sg, xs, w_out, b, B, out)
            else:
                xs = _moba_layer(qp, q16, k, v, kmean, sg, xs, w_out, 0, 1, None)
    return out.reshape(B, S, D)
```

```python
import functools

import jax
import jax.numpy as jnp
import numpy as np
from jax import lax
from jax.experimental import pallas as pl
from jax.experimental.pallas import tpu as pltpu
from jax.experimental.pallas import tpu_sc as plsc

N_HEADS = 8
HEAD_DIM = 128
MOBA_BLOCK = 256
MOBA_TOPK = 3
CONV_WIDTH = 4
N_RG_BLOCKS = 8
RG_BW = 128
RG_C = 8.0
ROPE_THETA = 10000.0
EPS = 1e-6
NEG_INF = -1e30
LOG2_E = 1.4426950408889634
TINY = 1e-30

SUBLANES = 8
LANES = 128
ROW_TILE = 512
SUB_ROWS = 128
VMEM_LIMIT = 56 * 1024 * 1024

GROUP_TILE = MOBA_BLOCK
TILES_PER_STEP = 32
V_BUILD_ROWS = 1024
HEADS_PER_STEP = 8
MERGE_TILES = 2
OG_BUFFERS = 3
ROUTE_TILES = 16
ROUTE_BLK_SHIFT = 16
ROUTE_BLK_UNIT = 1 << ROUTE_BLK_SHIFT
HALF = HEAD_DIM // 2
VA_W = 2 * HEAD_DIM
L_LANE = HALF
HI16_MASK = -65536
BF16_HALF_ULP = 0x8000
NEG_INF_BITS = int(np.float32(NEG_INF).view(np.int32))

SC_CORES = 2
SC_SUBCORES = 16
SC_WORKERS = SC_CORES * SC_SUBCORES
IDX_CHUNK = 128
SC_HALVES = MOBA_BLOCK // IDX_CHUNK

F32 = jnp.float32
BF16 = jnp.bfloat16
I32 = jnp.int32


def _cparams(sem):
    return pltpu.CompilerParams(dimension_semantics=sem, vmem_limit_bytes=VMEM_LIMIT)


def _rms_norm(x, g):
    ms = jnp.mean(x * x, axis=-1, keepdims=True)
    return (x * lax.rsqrt(ms + EPS)) * g


def _silu(x):
    hx = 0.5 * x
    return hx * jnp.tanh(hx) + hx


def _nt_dot(a, b):
    return lax.dot_general(a, b, (((1,), (1,)), ((), ())), preferred_element_type=F32)


def _head_norm_rope(z, g, cosf, sinf):
    outs = []
    for h in range(N_HEADS):
        zh = z[:, h * HEAD_DIM:(h + 1) * HEAD_DIM]
        zh = _rms_norm(zh, g)
        outs.append(zh * cosf + pltpu.roll(zh, HEAD_DIM // 2, 1) * sinf)
    return outs


def _hawk_kernel(x_ref, g_ref, inw_ref, cw_ref, cb_ref, wri_ref, rb_ref, ib_ref, lam_ref, outw_ref,
                 o_ref, gate_ref, xpad_ref, a_ref, b_ref, hc_ref):
    tm, C = gate_ref.shape
    s = pl.program_id(1)

    @pl.when(s == 0)
    def _():
        xpad_ref[pl.ds(0, SUBLANES), :] = jnp.zeros((SUBLANES, xpad_ref.shape[1]), F32)
        hc_ref[...] = jnp.zeros_like(hc_ref)

    h = _rms_norm(x_ref[...], g_ref[...]).astype(BF16)
    gate_ref[...] = jnp.dot(h, inw_ref[:, C:], preferred_element_type=F32)

    xpad_ref[pl.ds(SUBLANES, tm), :] = jnp.dot(h, inw_ref[:, :C], preferred_element_type=F32)
    cw = cw_ref[...]
    xc = xpad_ref[pl.ds(SUBLANES - (CONV_WIDTH - 1), tm), :] * cw[0:1, :]
    for k in range(1, CONV_WIDTH):
        xc = xc + xpad_ref[pl.ds(SUBLANES - (CONV_WIDTH - 1) + k, tm), :] * cw[k:k + 1, :]
    xc = xc + cb_ref[...]
    xpad_ref[pl.ds(0, SUBLANES), :] = xpad_ref[pl.ds(tm, SUBLANES), :]

    lam = lam_ref[...]
    sp = jnp.maximum(-lam, 0.0) + jnp.log1p(jnp.exp(-jnp.abs(lam)))
    half_log2a = (-0.5 * RG_C * LOG2_E) * sp
    xc16 = xc.astype(BF16)
    for g in range(N_RG_BLOCKS):
        lo = g * RG_BW
        z = jnp.dot(xc16[:, lo:lo + RG_BW], wri_ref[g], preferred_element_type=F32)
        tr = jnp.tanh(z[:, :RG_BW] + rb_ref[:, lo:lo + RG_BW])
        ti = jnp.tanh(z[:, RG_BW:] + ib_ref[:, lo:lo + RG_BW])
        a = jnp.exp2(tr * half_log2a[:, lo:lo + RG_BW] + half_log2a[:, lo:lo + RG_BW])
        om = 1.0 - a * a
        mult = om * lax.rsqrt(jnp.maximum(om, TINY))
        hx = 0.5 * xc[:, lo:lo + RG_BW]
        a_ref[:, lo:lo + RG_BW] = a
        b_ref[:, lo:lo + RG_BW] = mult * (hx * ti + hx)

    row = lax.broadcasted_iota(I32, (SUBLANES, a_ref.shape[1]), 0)

    def scan_body(c, hc):
        r0 = pl.multiple_of(c * SUBLANES, SUBLANES)
        a = a_ref[pl.ds(r0, SUBLANES), :]
        b = b_ref[pl.ds(r0, SUBLANES), :]
        for k in (1, 2, 4):
            keep = row >= k
            b = jnp.where(keep, a * pltpu.roll(b, k, 0) + b, b)
            a = jnp.where(keep, a * pltpu.roll(a, k, 0), a)
        h = a * hc + b
        b_ref[pl.ds(r0, SUBLANES), :] = h
        return h[SUBLANES - 1:SUBLANES, :]

    hc = lax.fori_loop(0, tm // SUBLANES, scan_body, hc_ref[0:1, :])
    hc_ref[0:1, :] = hc

    gh = gate_ref[...]
    y = (b_ref[...] * (gh * jnp.tanh(gh) + gh)).astype(BF16)
    o_ref[...] = x_ref[...] + jnp.dot(y, outw_ref[...], preferred_element_type=F32)


def _hawk_layer(x2d, b_in, S, g, in_w, cw, cb, wri, rb, ib, lam, out_w):
    D = x2d.shape[1]
    C = in_w.shape[1] // 2
    tm = ROW_TILE
    ns = S // tm
    vec = lambda n: pl.BlockSpec((1, n), lambda b, s: (0, 0))
    return pl.pallas_call(
        _hawk_kernel,
        grid=(1, ns),
        in_specs=[pl.BlockSpec((tm, D), lambda b, s: (b_in * ns + s, 0)),
                  vec(D),
                  pl.BlockSpec((D, 2 * C), lambda b, s: (0, 0)),
                  pl.BlockSpec((CONV_WIDTH, C), lambda b, s: (0, 0)),
                  vec(C),
                  pl.BlockSpec((N_RG_BLOCKS, RG_BW, 2 * RG_BW), lambda b, s: (0, 0, 0)),
                  vec(C), vec(C), vec(C),
                  pl.BlockSpec((C, D), lambda b, s: (0, 0))],
        out_specs=pl.BlockSpec((tm, D), lambda b, s: (s, 0)),
        out_shape=jax.ShapeDtypeStruct((S, D), F32),
        scratch_shapes=[pltpu.VMEM((tm, C), F32),
                        pltpu.VMEM((tm + SUBLANES, C), F32),
                        pltpu.VMEM((tm, C), F32),
                        pltpu.VMEM((tm, C), F32),
                        pltpu.VMEM((SUBLANES, C), F32)],
        compiler_params=_cparams(("arbitrary", "arbitrary")),
        name="hawk_layer",
    )(x2d, g, in_w, cw, cb, wri, rb, ib, lam, out_w)


def _emit_q(u, qg, cosf, sinf, t, qp_ref, q16_ref, sg_ref):
    assert SUB_ROWS == IDX_CHUNK
    d_attn = N_HEADS * HEAD_DIM
    rows = pl.ds(t * SUB_ROWS, SUB_ROWS)
    prow = pl.ds((t // 2) * SUB_ROWS, SUB_ROWS)
    qs = _head_norm_rope(u[:, :d_attn], qg, cosf, sinf)
    for hd in range(N_HEADS):
        q16 = qs[hd].astype(BF16)
        q16_ref[0, hd, rows, :] = q16
        r = _bits(q16.astype(F32))
        swapped = pltpu.roll(r, HALF, 1)
        if t % 2 == 0:
            w = (swapped & HI16_MASK) | lax.shift_right_logical(r, 16)
            qp_ref[0, hd, prow, pl.ds(0, HALF)] = w[:, :HALF]
        else:
            w = (r & HI16_MASK) | lax.shift_right_logical(swapped, 16)
            qp_ref[0, hd, prow, pl.ds(HALF, HALF)] = w[:, HALF:]
    sg_ref[rows, :] = _silu(u[:, d_attn:])


def _kvq_proj_kernel(x_ref, gkv_ref, wkv_ref, kg_ref, gq_ref, wq_ref, qg_ref, cos_ref, sin_ref,
                     k_ref, v_ref, km_ref, q_ref, q16_ref, sg_ref):
    tm = x_ref.shape[0]
    d_attn = N_HEADS * HEAD_DIM
    ksum = {}
    for t in range(tm // SUB_ROWS):
        rows = pl.ds(t * SUB_ROWS, SUB_ROWS)
        x = x_ref[rows, :]
        xn = x * lax.rsqrt(jnp.mean(x * x, axis=-1, keepdims=True) + EPS)
        kv = jnp.dot((xn * gkv_ref[...]).astype(BF16), wkv_ref[...], preferred_element_type=F32)
        u = jnp.dot((xn * gq_ref[...]).astype(BF16), wq_ref[...], preferred_element_type=F32)
        cosf, sinf = cos_ref[rows, :], sin_ref[rows, :]
        ks = _head_norm_rope(kv[:, :d_attn], kg_ref[...], cosf, sinf)
        for hd in range(N_HEADS):
            k_ref[0, hd, rows, :] = ks[hd].astype(BF16)
            v_ref[0, hd, rows, :] = kv[:, d_attn + hd * HEAD_DIM:d_attn + (hd + 1) * HEAD_DIM].astype(BF16)
            key = (hd, (t * SUB_ROWS) // MOBA_BLOCK)
            part = jnp.sum(ks[hd], axis=0, keepdims=True)
            ksum[key] = part if key not in ksum else ksum[key] + part
        _emit_q(u, qg_ref[...], cosf, sinf, t, q_ref, q16_ref, sg_ref)
    for (hd, r), total in ksum.items():
        km_ref[0, r, pl.ds(hd, 1), :] = total * (1.0 / MOBA_BLOCK)


def _kvq_proj(x2d, B, S, gkv, wkv, kg, gq, wq, qg, cosf, sinf):
    T, D = x2d.shape
    tm = ROW_TILE
    ns = S // tm
    nb = S // MOBA_BLOCK
    rpt = tm // MOBA_BLOCK
    d_attn = N_HEADS * HEAD_DIM
    vec = lambda n: pl.BlockSpec((1, n), lambda b, s: (0, 0))
    wspec = pl.BlockSpec((D, 2 * d_attn), lambda b, s: (0, 0))
    rope = pl.BlockSpec((tm, HEAD_DIM), lambda b, s: (s, 0))
    heads = pl.BlockSpec((1, N_HEADS, tm, HEAD_DIM), lambda b, s: (b, 0, s, 0))
    hshape = lambda dt: jax.ShapeDtypeStruct((B, N_HEADS, S, HEAD_DIM), dt)
    return pl.pallas_call(
        _kvq_proj_kernel,
        grid=(B, ns),
        in_specs=[pl.BlockSpec((tm, D), lambda b, s: (b * ns + s, 0)),
                  vec(D), wspec, vec(HEAD_DIM), vec(D), wspec, vec(HEAD_DIM), rope, rope],
        out_specs=[heads, heads,
                   pl.BlockSpec((1, rpt, N_HEADS, HEAD_DIM), lambda b, s: (b, s, 0, 0)),
                   pl.BlockSpec((1, N_HEADS, tm // 2, LANES), lambda b, s: (b, 0, s, 0)), heads,
                   pl.BlockSpec((tm, d_attn), lambda b, s: (b * ns + s, 0))],
        out_shape=[hshape(BF16), hshape(BF16),
                   jax.ShapeDtypeStruct((B, nb, N_HEADS, HEAD_DIM), F32),
                   jax.ShapeDtypeStruct((B, N_HEADS, S // 2, LANES), I32), hshape(BF16),
                   jax.ShapeDtypeStruct((T, d_attn), F32)],
        compiler_params=_cparams(("arbitrary", "arbitrary")),
        name="kvq_proj",
    )(x2d, gkv, wkv, kg, gq, wq, qg, cosf, sinf)


def _q_proj_kernel(x_ref, g_ref, w_ref, qg_ref, cos_ref, sin_ref, q_ref, q16_ref, sg_ref):
    for t in range(x_ref.shape[0] // SUB_ROWS):
        rows = pl.ds(t * SUB_ROWS, SUB_ROWS)
        h = _rms_norm(x_ref[rows, :], g_ref[...]).astype(BF16)
        u = jnp.dot(h, w_ref[...], preferred_element_type=F32)
        _emit_q(u, qg_ref[...], cos_ref[rows, :], sin_ref[rows, :], t, q_ref, q16_ref, sg_ref)


def _q_proj(x2d, B, S, g, w, qg, cosf, sinf):
    T, D = x2d.shape
    tm = ROW_TILE
    ns = S // tm
    d_attn = N_HEADS * HEAD_DIM
    heads = pl.BlockSpec((1, N_HEADS, tm, HEAD_DIM), lambda b, s: (b, 0, s, 0))
    return pl.pallas_call(
        _q_proj_kernel,
        grid=(B, ns),
        in_specs=[pl.BlockSpec((tm, D), lambda b, s: (b * ns + s, 0)),
                  pl.BlockSpec((1, D), lambda b, s: (0, 0)),
                  pl.BlockSpec((D, 2 * d_attn), lambda b, s: (0, 0)),
                  pl.BlockSpec((1, HEAD_DIM), lambda b, s: (0, 0)),
                  pl.BlockSpec((tm, HEAD_DIM), lambda b, s: (s, 0)),
                  pl.BlockSpec((tm, HEAD_DIM), lambda b, s: (s, 0))],
        out_specs=[pl.BlockSpec((1, N_HEADS, tm // 2, LANES), lambda b, s: (b, 0, s, 0)), heads,
                   pl.BlockSpec((tm, d_attn), lambda b, s: (b * ns + s, 0))],
        out_shape=[jax.ShapeDtypeStruct((B, N_HEADS, S // 2, LANES), I32),
                   jax.ShapeDtypeStruct((B, N_HEADS, S, HEAD_DIM), BF16),
                   jax.ShapeDtypeStruct((T, d_attn), F32)],
        compiler_params=_cparams(("arbitrary", "arbitrary")),
        name="q_proj",
    )(x2d, g, w, qg, cosf, sinf)


def _moba_route_kernel(q_ref, km_ref, rt_ref, cnt_ref, run_ref):
    step = pl.program_id(2)
    nb = km_ref.shape[2]

    @pl.when(step == 0)
    def _():
        run_ref[...] = jnp.zeros_like(run_ref)

    km = km_ref[0, 0].astype(BF16)
    blk = lax.broadcasted_iota(I32, (nb, MOBA_BLOCK), 0)
    qa = lax.broadcasted_iota(I32, (MOBA_BLOCK, MOBA_BLOCK), 0)
    qb = lax.broadcasted_iota(I32, (MOBA_BLOCK, MOBA_BLOCK), 1)
    earlier = jnp.where(qa < qb, 1.0, 0.0).astype(BF16)
    run = run_ref[...]
    for u in range(ROUTE_TILES):
        j = step * ROUTE_TILES + u
        g = _nt_dot(km, q_ref[0, 0, pl.ds(u * MOBA_BLOCK, MOBA_BLOCK), :].astype(BF16))
        g = jnp.where(blk < j, g, -jnp.inf)
        hits, ids = [], []
        for _ in range(MOBA_TOPK):
            mx = jnp.max(g, axis=0, keepdims=True)
            idx = jnp.min(jnp.where(g == mx, blk, nb), axis=0, keepdims=True)
            hit = blk == idx
            ok = mx > -jnp.inf
            hits.append(jnp.where(hit & ok, 1.0, 0.0))
            ids.append(jnp.where(ok, idx, -1))
            g = jnp.where(hit, -jnp.inf, g)
        sel = hits[0] + hits[1] + hits[2]

        rank = jnp.dot(sel.astype(BF16), earlier, preferred_element_type=F32)
        dest = run + rank
        run = run + jnp.sum(sel, axis=1, keepdims=True)
        for r in range(MOBA_TOPK):
            loc = jnp.sum(hits[r] * dest, axis=0, keepdims=True).astype(I32)
            rt_ref[u, pl.ds(r, 1), :] = jnp.where(ids[r] >= 0, ids[r] * ROUTE_BLK_UNIT + loc, -1)
        rt_ref[u, pl.ds(MOBA_TOPK, SUBLANES - MOBA_TOPK), :] = jnp.full(
            (SUBLANES - MOBA_TOPK, MOBA_BLOCK), -1, I32)
    run_ref[...] = run
    cnt_ref[0] = run.astype(I32)


def _moba_route(q, kmean, b):
    _, H, S, Dh = q.shape
    nb = S // MOBA_BLOCK
    rows = ROUTE_TILES * MOBA_BLOCK
    nsteps = nb // ROUTE_TILES
    return pl.pallas_call(
        _moba_route_kernel,
        grid=(1, H, nsteps),
        in_specs=[pl.BlockSpec((1, 1, rows, Dh), lambda _, h, j: (b, h, j, 0)),
                  pl.BlockSpec((1, 1, nb, Dh), lambda _, h, j: (b, h, 0, 0))],
        out_specs=[pl.BlockSpec((ROUTE_TILES, SUBLANES, MOBA_BLOCK), lambda _, h, j: (h * nsteps + j, 0, 0)),
                   pl.BlockSpec((1, nb, 1), lambda _, h, j: (h, 0, 0))],
        out_shape=[jax.ShapeDtypeStruct((H * nb, SUBLANES, MOBA_BLOCK), I32),
                   jax.ShapeDtypeStruct((H, nb, 1), I32)],
        scratch_shapes=[pltpu.VMEM((nb, 1), F32)],
        compiler_params=_cparams(("arbitrary", "arbitrary", "arbitrary")),
        name="moba_route",
    )(q, kmean)


def _moba_layout_kernel(cnt_ref, rt_ref, pos_ref, qpos_ref, tblk_ref, nt_ref, base_ref, *, max_tiles):
    g = pl.program_id(0)
    nb = rt_ref.shape[0]
    slab_rows = max_tiles * GROUP_TILE

    def per_block(n, first_tile):
        count = cnt_ref[g * nb + n]
        tiles = (count + (GROUP_TILE - 1)) // GROUP_TILE
        base_ref[n] = first_tile * GROUP_TILE

        def per_tile(i, carry):
            tblk_ref[g * max_tiles + first_tile + i] = n
            return carry

        lax.fori_loop(0, tiles, per_tile, 0)
        return first_tile + tiles

    used = lax.fori_loop(0, nb, per_block, 0)
    nt_ref[g] = used
    null_row0 = used * GROUP_TILE

    def unused_tile(t, carry):
        tblk_ref[g * max_tiles + t] = nb - 1
        return carry

    lax.fori_loop(used, max_tiles, unused_tile, 0)

    code = rt_ref[...].reshape(nb * SUBLANES, MOBA_BLOCK)
    blk = lax.shift_right_arithmetic(code, ROUTE_BLK_SHIFT)
    base = jnp.zeros_like(code)
    for n in range(nb):
        base = jnp.where(blk == n, base_ref[n], base)
    lane = lax.broadcasted_iota(I32, code.shape, 1)
    row = jnp.where(code >= 0, base + (code & (ROUTE_BLK_UNIT - 1)), null_row0 + lane) + g * slab_rows
    pos_ref[...] = row.reshape(pos_ref.shape)
    p = row & (GROUP_TILE - 1)
    assert GROUP_TILE == 2 * IDX_CHUNK
    slot = ((p & (IDX_CHUNK - 1)) << 1) | lax.shift_right_logical(p, IDX_CHUNK.bit_length() - 1)
    qpos_ref[...] = ((row - p) | slot).reshape(qpos_ref.shape)


def _moba_layout(rt, counts, nb, max_tiles):
    G = rt.shape[0] // nb
    smem = pl.BlockSpec(memory_space=pltpu.SMEM)
    return pl.pallas_call(
        functools.partial(_moba_layout_kernel, max_tiles=max_tiles),
        grid_spec=pltpu.PrefetchScalarGridSpec(
            num_scalar_prefetch=1,
            grid=(G,),
            in_specs=[pl.BlockSpec((nb, SUBLANES, MOBA_BLOCK), lambda g, cnt: (g, 0, 0))],
            out_specs=[pl.BlockSpec((nb, SUBLANES, MOBA_BLOCK), lambda g, cnt: (g, 0, 0)),
                       pl.BlockSpec((nb, SUBLANES, MOBA_BLOCK), lambda g, cnt: (g, 0, 0)), smem, smem],
            scratch_shapes=[pltpu.SMEM((nb,), I32)],
        ),
        out_shape=[jax.ShapeDtypeStruct((G * nb, SUBLANES, MOBA_BLOCK), I32),
                   jax.ShapeDtypeStruct((G * nb, SUBLANES, MOBA_BLOCK), I32),
                   jax.ShapeDtypeStruct((G * max_tiles,), I32),
                   jax.ShapeDtypeStruct((G,), I32)],
        compiler_params=_cparams(("arbitrary",)),
        name="moba_layout",
    )(counts, rt)


def _sc_mesh():
    return plsc.VectorSubcoreMesh(core_axis_name="c", subcore_axis_name="s")


def _sc_load_indices(pos_hbm, idx_v, t0, n_tiles):
    for c in range(SC_HALVES):
        pltpu.sync_copy(pos_hbm.at[pl.ds(t0, n_tiles), pl.ds(0, MOBA_TOPK), pl.ds(c * IDX_CHUNK, IDX_CHUNK)],
                        idx_v.at[c])


def _sc_scatter_rows(src, src_tile0, pos, n_out_rows):
    D = src.shape[-1]
    n_tiles = pos.shape[0]
    per_w = n_tiles // SC_WORKERS
    assert per_w * SC_WORKERS == n_tiles and src.shape[1:3] == (IDX_CHUNK, SC_HALVES)

    @functools.partial(
        pl.kernel, mesh=_sc_mesh(),
        out_type=jax.ShapeDtypeStruct((n_out_rows, D), src.dtype),
        scratch_types=[pltpu.VMEM((SC_HALVES, per_w, MOBA_TOPK, IDX_CHUNK), I32),
                       pltpu.VMEM((SC_HALVES, IDX_CHUNK, D), src.dtype),
                       pltpu.SemaphoreType.DMA((SC_HALVES,)),
                       pltpu.SemaphoreType.DMA((SC_HALVES,))],
        compiler_params=pltpu.CompilerParams(use_tc_tiling_on_sc=False),
        name="sc_scatter_rows",
    )
    def k(src_hbm, pos_hbm, out_hbm, idx_v, rows_v, load_sem, scat_sem):
        t0 = (lax.axis_index("s") * SC_CORES + lax.axis_index("c")) * per_w
        _sc_load_indices(pos_hbm, idx_v, t0, per_w)

        def load(i, c):
            return pltpu.make_async_copy(src_hbm.at[src_tile0 + t0 + i, :, c, :], rows_v.at[c], load_sem.at[c])

        def scatters(i, c):
            return [pltpu.make_async_copy(rows_v.at[c], out_hbm.at[idx_v.at[c, i, r]], scat_sem.at[c])
                    for r in range(MOBA_TOPK)]

        load(0, 0).start()

        @pl.loop(0, per_w)
        def _(i):
            for c in range(SC_HALVES):
                pi, pc = (i, 0) if c else (i - 1, 1)
                ni, nc = (i + 1, 0) if c else (i, 1)
                load(i, c).wait()
                for cp in scatters(i, c):
                    cp.start()

                @pl.when(pi >= 0)
                def _():
                    for cp in scatters(pi, pc):
                        cp.wait()

                @pl.when(ni < per_w)
                def _():
                    load(ni, nc).start()

        for cp in scatters(per_w - 1, SC_HALVES - 1):
            cp.wait()

    return k(src, pos)


def _sc_gather_rows(table, pos):
    R, D = table.shape
    n_tiles = pos.shape[0]
    N = n_tiles * MOBA_BLOCK
    per_w = n_tiles // SC_WORKERS
    assert per_w * SC_WORKERS == n_tiles

    items = SC_HALVES * MOBA_TOPK
    n_buf = 2
    assert items % n_buf == 0

    @functools.partial(
        pl.kernel, mesh=_sc_mesh(),
        out_type=jax.ShapeDtypeStruct((MOBA_TOPK, N, D), table.dtype),
        scratch_types=[pltpu.VMEM((SC_HALVES, per_w, MOBA_TOPK, IDX_CHUNK), I32),
                       pltpu.VMEM((n_buf, IDX_CHUNK, D), table.dtype),
                       pltpu.SemaphoreType.DMA((n_buf,)),
                       pltpu.SemaphoreType.DMA((n_buf,))],
        name="sc_gather_rows",
    )
    def k(table_hbm, pos_hbm, out_hbm, idx_v, rows_v, gat_sem, put_sem):
        t0 = (lax.axis_index("s") * SC_CORES + lax.axis_index("c")) * per_w
        _sc_load_indices(pos_hbm, idx_v, t0, per_w)

        def gather(i, j):
            c, r = divmod(j, MOBA_TOPK)
            return pltpu.make_async_copy(table_hbm.at[idx_v.at[c, i, r]], rows_v.at[j % n_buf], gat_sem.at[j % n_buf])

        def put(i, j):
            c, r = divmod(j, MOBA_TOPK)
            row0 = pl.multiple_of((t0 + i) * MOBA_BLOCK + c * IDX_CHUNK, IDX_CHUNK)
            return pltpu.make_async_copy(rows_v.at[j % n_buf], out_hbm.at[r, pl.ds(row0, IDX_CHUNK)],
                                         put_sem.at[j % n_buf])

        gather(0, 0).start()

        @pl.loop(0, per_w)
        def _(i):
            for j in range(items):
                pi, pj = (i, j - 1) if j else (i - 1, items - 1)
                ni, nj = (i, j + 1) if j + 1 < items else (i + 1, 0)
                gather(i, j).wait()
                put(i, j).start()

                @pl.when(pi >= 0)
                def _():
                    put(pi, pj).wait()

                @pl.when(ni < per_w)
                def _():
                    gather(ni, nj).start()

        put(per_w - 1, items - 1).wait()

    return k(table, pos)


def _bits(x):
    return lax.bitcast_convert_type(x, I32)


def _pack_partial(part, m):
    lane = lax.broadcasted_iota(I32, (part.shape[0], HEAD_DIM), 1)
    lo = _bits(part[:, :HEAD_DIM]) + BF16_HALF_ULP
    hi = _bits(jnp.where(lane == L_LANE, m, part[:, HEAD_DIM:])) + BF16_HALF_ULP
    return (hi & HI16_MASK) | lax.shift_right_logical(lo, 16)


def _null_partial(rows):
    lane = lax.broadcasted_iota(I32, (rows, HEAD_DIM), 1)
    return jnp.where(lane == L_LANE, NEG_INF_BITS & HI16_MASK, 0)


def _unpack_partial(word):
    lo = lax.bitcast_convert_type(lax.shift_left(word, 16), F32)
    hi = lax.bitcast_convert_type(word & HI16_MASK, F32)
    return lo, hi


def _augment_v(v):
    vf = v.astype(F32)
    lane = lax.broadcasted_iota(I32, vf.shape, 1)
    lo = jnp.where(lane < HALF, vf, jnp.where(lane == L_LANE, 1.0, 0.0))
    hi = jnp.where(lane < HALF, pltpu.roll(vf, HALF, 1), 0.0)
    return jnp.concatenate([lo, hi], axis=1).astype(BF16)


def _moba_group_kernel(tblk_ref, nt_ref, qg_ref, k_ref, v_ref, o_ref, va_ref, k2_ref, *, max_tiles):
    g, step = pl.program_id(0), pl.program_id(1)
    t0 = g * max_tiles + step * TILES_PER_STEP
    scale = HEAD_DIM ** -0.5
    null_tile = nt_ref[g]
    last_step = null_tile // TILES_PER_STEP

    @pl.when(step == 0)
    def _():
        def build(c, carry):
            rows = pl.ds(pl.multiple_of(c * V_BUILD_ROWS, V_BUILD_ROWS), V_BUILD_ROWS)
            va_ref[rows, :] = _augment_v(v_ref[0, 0, rows, :])
            kf = k_ref[0, 0, rows, :].astype(F32)
            lane = lax.broadcasted_iota(I32, kf.shape, 1)
            swapped = pltpu.roll(kf, HALF, 1)
            k2_ref[rows, :] = jnp.concatenate([jnp.where(lane < HALF, kf, swapped),
                                               jnp.where(lane < HALF, swapped, kf)], axis=1).astype(BF16)
            return carry

        lax.fori_loop(0, va_ref.shape[0] // V_BUILD_ROWS, build, 0)

    @pl.when(step <= last_step)
    def _():
        half_rows = GROUP_TILE // 2
        lane = lax.broadcasted_iota(I32, (half_rows, HEAD_DIM), 1)
        for u in range(TILES_PER_STEP):
            word = qg_ref[pl.ds(u * half_rows, half_rows), :]
            first = lax.bitcast_convert_type(lax.shift_left(word, 16), F32)
            second = lax.bitcast_convert_type(word & HI16_MASK, F32)
            n0 = pl.multiple_of(tblk_ref[t0 + u] * MOBA_BLOCK, MOBA_BLOCK)
            left, right = lane < HALF, lane >= HALF
            q = jnp.concatenate(
                [jnp.concatenate([jnp.where(left, first, 0.0), jnp.where(left, second, 0.0)], axis=1),
                 jnp.concatenate([jnp.where(right, first, 0.0), jnp.where(right, second, 0.0)], axis=1)],
                axis=0).astype(BF16)
            s = _nt_dot(q, k2_ref[pl.ds(n0, MOBA_BLOCK), :])
            m = (jnp.max(s, axis=1, keepdims=True) * (scale * LOG2_E)).astype(BF16).astype(F32)
            p = jnp.exp2(s * (scale * LOG2_E) - m).astype(BF16)
            part = jnp.dot(p, va_ref[pl.ds(n0, MOBA_BLOCK), :], preferred_element_type=F32)
            o_ref[pl.ds(u * GROUP_TILE, GROUP_TILE), :] = _pack_partial(part, m)

    @pl.when(step == last_step)
    def _():
        row0 = pl.multiple_of((null_tile - last_step * TILES_PER_STEP) * GROUP_TILE, GROUP_TILE)
        o_ref[pl.ds(row0, GROUP_TILE), :] = _null_partial(GROUP_TILE)


def _moba_group(qg, k, v, b, tblk, ntiles, max_tiles):
    _, H, S, Dh = k.shape
    G = H
    n_steps = max_tiles // TILES_PER_STEP
    step_rows = TILES_PER_STEP * GROUP_TILE

    def tile_idx(g, s, tblk, nt):
        return (g * n_steps + jnp.minimum(s, nt[g] // TILES_PER_STEP), 0)

    def kv_idx(g, s, *_):
        return (b, jnp.minimum(g + jnp.minimum(s, 1), G - 1), 0, 0)

    return pl.pallas_call(
        functools.partial(_moba_group_kernel, max_tiles=max_tiles),
        grid_spec=pltpu.PrefetchScalarGridSpec(
            num_scalar_prefetch=2,
            grid=(G, n_steps),
            in_specs=[pl.BlockSpec((step_rows // 2, LANES), tile_idx),
                      pl.BlockSpec((1, 1, S, Dh), kv_idx),
                      pl.BlockSpec((1, 1, S, Dh), kv_idx)],
            out_specs=pl.BlockSpec((step_rows, LANES), tile_idx),
            scratch_shapes=[pltpu.VMEM((S, VA_W), BF16), pltpu.VMEM((S, 2 * Dh), BF16)],
        ),
        out_shape=jax.ShapeDtypeStruct((G * max_tiles * GROUP_TILE, LANES), I32),
        compiler_params=_cparams(("arbitrary", "arbitrary")),
        name="moba_group",
    )(tblk, ntiles, qg, k, v)


def _moba_merge_kernel(*refs, n_skip):
    q_ref, k_ref, v_ref, sg_ref, x_ref, outw_ref, og_hbm = refs[n_skip:n_skip + 7]
    out_ref, y_ref, og_buf, og_sem = refs[n_skip + 7:]
    n_heads = q_ref.shape[1]
    step, n_steps = pl.program_id(2), pl.num_programs(2)
    step_rows = MERGE_TILES * MOBA_BLOCK

    def og_copies(s):
        slot = s % OG_BUFFERS
        return [pltpu.make_async_copy(
            og_hbm.at[:, pl.ds(pl.multiple_of((u * n_steps + s) * step_rows, step_rows), step_rows), :],
            og_buf.at[slot, u], og_sem.at[slot]) for u in range(n_heads)]

    @pl.when(step == 0)
    def _():
        for s in range(OG_BUFFERS - 1):
            @pl.when(s < n_steps)
            def _():
                for cp in og_copies(s):
                    cp.start()

    @pl.when(step + (OG_BUFFERS - 1) < n_steps)
    def _():
        for cp in og_copies(step + (OG_BUFFERS - 1)):
            cp.start()

    for cp in og_copies(step):
        cp.wait()
    og_refs = [og_buf.at[step % OG_BUFFERS, u] for u in range(n_heads)]
    scale = HEAD_DIM ** -0.5
    qi = lax.broadcasted_iota(I32, (MOBA_BLOCK, MOBA_BLOCK), 0)
    ki = lax.broadcasted_iota(I32, (MOBA_BLOCK, MOBA_BLOCK), 1)
    lane = lax.broadcasted_iota(I32, (MOBA_BLOCK, HEAD_DIM), 1)
    spread = jnp.where(lax.broadcasted_iota(I32, (HEAD_DIM, HEAD_DIM), 0) == L_LANE, 1.0, 0.0).astype(BF16)
    for u, t in [(u, t) for t in range(MERGE_TILES) for u in range(len(og_refs))]:
        og_ref = og_refs[u]
        rows = pl.ds(t * MOBA_BLOCK, MOBA_BLOCK)
        q = q_ref[0, u, rows, :].astype(BF16)
        s = _nt_dot(q, k_ref[0, u, rows, :])
        s = jnp.where(ki <= qi, s, NEG_INF)
        m_raw = jnp.max(s, axis=1, keepdims=True)
        p = jnp.exp2((s - m_raw) * (scale * LOG2_E)).astype(BF16)
        own = jnp.dot(p, _augment_v(v_ref[0, u, rows, :]), preferred_element_type=F32)
        parts = [_unpack_partial(og_ref[r, rows, :]) for r in range(MOBA_TOPK)]
        ms = [jnp.dot(jnp.where(lane == L_LANE, hi, 0.0).astype(BF16), spread, preferred_element_type=F32)
              for _, hi in parts]
        m_own = jnp.broadcast_to(m_raw * (scale * LOG2_E), lane.shape)
        m_all = m_own
        for m in ms:
            m_all = jnp.maximum(m_all, m)
        w_own = jnp.exp2(m_own - m_all)
        tot_lo = w_own * own[:, :HEAD_DIM]
        tot_hi = w_own * own[:, HEAD_DIM:]
        for (lo, hi), m in zip(parts, ms):
            w = jnp.exp2(m - m_all)
            tot_lo = tot_lo + w * lo
            tot_hi = tot_hi + w * hi
        acc = jnp.where(lane < HALF, tot_lo, pltpu.roll(tot_hi, HALF, 1))
        cols = pl.ds(u * HEAD_DIM, HEAD_DIM)
        y_ref[rows, cols] = ((acc / tot_lo[:, L_LANE:L_LANE + 1]) * sg_ref[rows, cols]).astype(BF16)

    out_ref[...] = x_ref[...] + jnp.dot(y_ref[...], outw_ref[...], preferred_element_type=F32)


def _moba_merge(q, k, v, sg, og, x2d, out_w, out_batch, n_out_batch, o_prev):
    _, H, S, Dh = q.shape
    D = x2d.shape[1]
    b = 0
    hps = HEADS_PER_STEP
    assert hps == H
    rows = MERGE_TILES * MOBA_BLOCK
    ns = S // rows
    prev_specs = [] if o_prev is None else [pl.BlockSpec(memory_space=pl.ANY)]
    prev_args = [] if o_prev is None else [o_prev]
    return pl.pallas_call(
        functools.partial(_moba_merge_kernel, n_skip=len(prev_args)),
        grid=(1, H // hps, ns),
        in_specs=prev_specs
                 + [pl.BlockSpec((1, hps, rows, Dh), lambda _, hp, j: (b, hp, j, 0)),
                    pl.BlockSpec((1, hps, rows, Dh), lambda _, hp, j: (b, hp, j, 0)),
                    pl.BlockSpec((1, hps, rows, Dh), lambda _, hp, j: (b, hp, j, 0)),
                    pl.BlockSpec((rows, hps * Dh), lambda _, hp, j: (b * ns + j, hp)),
                    pl.BlockSpec((rows, D), lambda _, hp, j: (b * ns + j, 0)),
                    pl.BlockSpec((H * Dh, D), lambda _, hp, j: (0, 0)),
                    pl.BlockSpec(memory_space=pl.ANY)],
        out_specs=pl.BlockSpec((rows, D), lambda _, hp, j: (out_batch * ns + j, 0)),
        out_shape=jax.ShapeDtypeStruct((n_out_batch * S, D), F32),
        scratch_shapes=[pltpu.VMEM((rows, H * Dh), BF16),
                        pltpu.VMEM((OG_BUFFERS, H, MOBA_TOPK, rows, LANES), og.dtype),
                        pltpu.SemaphoreType.DMA((OG_BUFFERS,))],
        input_output_aliases={} if o_prev is None else {0: 0},
        compiler_params=_cparams(("arbitrary", "arbitrary", "arbitrary")),
        name="moba_merge",
    )(*prev_args, q, k, v, sg, x2d, out_w, og)


def _moba_layer(qp, q16, k, v, kmean, sg, x2d, out_w, out_batch, n_out_batch, o_prev):
    _, H, S, Dh = q16.shape
    nb = S // MOBA_BLOCK
    max_tiles = (MOBA_TOPK * S) // GROUP_TILE + nb + 1
    max_tiles = -(-max_tiles // TILES_PER_STEP) * TILES_PER_STEP
    slab_rows = max_tiles * GROUP_TILE

    q_rows = qp.reshape(H * nb, IDX_CHUNK, SC_HALVES, HALF)
    rt, counts = _moba_route(q16, kmean, 0)
    pos, qpos, tblk, ntiles = _moba_layout(rt, counts.reshape(-1), nb, max_tiles)
    qg = _sc_scatter_rows(q_rows, 0, qpos, H * slab_rows)
    qg = qg.reshape(H * slab_rows // 2, LANES)
    parts = _moba_group(qg, k, v, 0, tblk, ntiles, max_tiles)
    og = _sc_gather_rows(parts, pos)
    return _moba_merge(q16, k, v, sg, og, x2d, out_w, out_batch, n_out_batch, o_prev)


def _rope_tables(S):
    inv = np.float32(ROPE_THETA) ** (-np.arange(HALF, dtype=np.float32) / np.float32(HALF))
    ang = np.arange(S, dtype=np.float32)[:, None] * inv[None, :].astype(np.float32)
    cos, sin = np.cos(ang).astype(np.float32), np.sin(ang).astype(np.float32)
    return (jnp.asarray(np.concatenate([cos, cos], axis=-1)),
            jnp.asarray(np.concatenate([-sin, sin], axis=-1)))


def kernel(x, a_norm, a_in_w, a_conv_w, a_conv_b, a_r_w, a_r_b, a_i_w, a_i_b, a_lambda, a_out_w,
           kv_norm, kv_w, k_norm, b_norm, b_in_w, q_norm, b_out_w):
    B, S, D = x.shape
    assert S % ROW_TILE == 0 and ROW_TILE % MOBA_BLOCK == 0
    assert D == N_RG_BLOCKS * RG_BW == N_HEADS * HEAD_DIM
    assert S % (ROUTE_TILES * MOBA_BLOCK) == 0 and S % (MERGE_TILES * MOBA_BLOCK) == 0
    assert S % V_BUILD_ROWS == 0 and S <= ROUTE_BLK_UNIT
    assert (N_HEADS * (S // MOBA_BLOCK)) % SC_WORKERS == 0
    n_a, n_b = a_in_w.shape[0], b_in_w.shape[0]
    assert n_a >= 1 and n_b >= 1
    x2d = x.reshape(B * S, D)
    row = lambda v: v.reshape(1, -1)
    cosf, sinf = _rope_tables(S)

    out = None
    for b in range(B):
        xs, b_in = x2d, b
        for l in range(n_a):
            c_rnn = a_in_w.shape[2] // 2
            wri = (0.5 * jnp.concatenate([a_r_w[l], a_i_w[l]], axis=-1)).astype(BF16)
            in_w = jnp.concatenate([a_in_w[l][:, :c_rnn], 0.5 * a_in_w[l][:, c_rnn:]], axis=-1).astype(BF16)
            xs = _hawk_layer(xs, b_in, S, row(a_norm[l]), in_w, a_conv_w[l], row(a_conv_b[l]),
                             wri, row(0.5 * a_r_b[l]), row(0.5 * a_i_b[l]), row(a_lambda[l]),
                             a_out_w[l].astype(BF16))
            b_in = 0
        for jl in range(n_b):
            q_args = (row(b_norm[jl]), b_in_w[jl].astype(BF16), row(q_norm[jl]))
            if jl == 0:
                k, v, kmean, qp, q16, sg = _kvq_proj(xs, 1, S, row(kv_norm), kv_w.astype(BF16), row(k_norm),
                                                    *q_args, cosf, sinf)
                kmean = jnp.transpose(kmean, (0, 2, 1, 3))
            else:
                qp, q16, sg = _q_proj(xs, 1, S, *q_args, cosf, sinf)
            w_out = b_out_w[jl].astype(BF16)
            if jl == n_b - 1:
                out = xs = _moba_layer(qp, q16, k, v, kmean, sg, xs, w_out, b, B, out)
            else:
                xs = _moba_layer(qp, q16, k, v, kmean, sg, xs, w_out, 0, 1, None)
    return out.reshape(B, S, D)
```

```python
import functools

import jax
import jax.numpy as jnp
import numpy as np
from jax import lax
from jax.experimental import pallas as pl
from jax.experimental.pallas import tpu as pltpu
from jax.experimental.pallas import tpu_sc as plsc

N_HEADS = 8
HEAD_DIM = 128
MOBA_BLOCK = 256
MOBA_TOPK = 3
CONV_WIDTH = 4
N_RG_BLOCKS = 8
RG_BW = 128
RG_C = 8.0
ROPE_THETA = 10000.0
EPS = 1e-6
NEG_INF = -1e30
LOG2_E = 1.4426950408889634
TINY = 1e-30

SUBLANES = 8
LANES = 128
ROW_TILE = 512
SUB_ROWS = 128
VMEM_LIMIT = 56 * 1024 * 1024

GROUP_TILE = MOBA_BLOCK
TILES_PER_STEP = 32
V_BUILD_ROWS = 1024
HEADS_PER_STEP = 8
MERGE_TILES = 2
ROUTE_TILES = 16
ROUTE_BLK_SHIFT = 16
ROUTE_BLK_UNIT = 1 << ROUTE_BLK_SHIFT
HALF = HEAD_DIM // 2
VA_W = 2 * HEAD_DIM
L_LANE = HALF
HI16_MASK = -65536
BF16_HALF_ULP = 0x8000
NEG_INF_BITS = int(np.float32(NEG_INF).view(np.int32))

SC_CORES = 2
SC_SUBCORES = 16
SC_WORKERS = SC_CORES * SC_SUBCORES
IDX_CHUNK = 128
SC_HALVES = MOBA_BLOCK // IDX_CHUNK

F32 = jnp.float32
BF16 = jnp.bfloat16
I32 = jnp.int32


def _cparams(sem):
    return pltpu.CompilerParams(dimension_semantics=sem, vmem_limit_bytes=VMEM_LIMIT)


def _rms_norm(x, g):
    ms = jnp.mean(x * x, axis=-1, keepdims=True)
    return (x * lax.rsqrt(ms + EPS)) * g


def _silu(x):
    hx = 0.5 * x
    return hx * jnp.tanh(hx) + hx


def _nt_dot(a, b):
    return lax.dot_general(a, b, (((1,), (1,)), ((), ())), preferred_element_type=F32)


def _head_norm_rope(z, g, cosf, sinf):
    outs = []
    for h in range(N_HEADS):
        zh = z[:, h * HEAD_DIM:(h + 1) * HEAD_DIM]
        zh = _rms_norm(zh, g)
        outs.append(zh * cosf + pltpu.roll(zh, HEAD_DIM // 2, 1) * sinf)
    return outs


def _hawk_kernel(x_ref, g_ref, inw_ref, cw_ref, cb_ref, wri_ref, rb_ref, ib_ref, lam_ref, outw_ref,
                 o_ref, gate_ref, xpad_ref, a_ref, b_ref, hc_ref):
    tm, C = gate_ref.shape
    s = pl.program_id(1)

    @pl.when(s == 0)
    def _():
        xpad_ref[pl.ds(0, SUBLANES), :] = jnp.zeros((SUBLANES, xpad_ref.shape[1]), F32)
        hc_ref[...] = jnp.zeros_like(hc_ref)

    h = _rms_norm(x_ref[...], g_ref[...]).astype(BF16)
    gate_ref[...] = jnp.dot(h, inw_ref[:, C:], preferred_element_type=F32)

    xpad_ref[pl.ds(SUBLANES, tm), :] = jnp.dot(h, inw_ref[:, :C], preferred_element_type=F32)
    cw = cw_ref[...]
    xc = xpad_ref[pl.ds(SUBLANES - (CONV_WIDTH - 1), tm), :] * cw[0:1, :]
    for k in range(1, CONV_WIDTH):
        xc = xc + xpad_ref[pl.ds(SUBLANES - (CONV_WIDTH - 1) + k, tm), :] * cw[k:k + 1, :]
    xc = xc + cb_ref[...]
    xpad_ref[pl.ds(0, SUBLANES), :] = xpad_ref[pl.ds(tm, SUBLANES), :]

    lam = lam_ref[...]
    sp = jnp.maximum(-lam, 0.0) + jnp.log1p(jnp.exp(-jnp.abs(lam)))
    half_log2a = (-0.5 * RG_C * LOG2_E) * sp
    xc16 = xc.astype(BF16)
    for g in range(N_RG_BLOCKS):
        lo = g * RG_BW
        z = jnp.dot(xc16[:, lo:lo + RG_BW], wri_ref[g], preferred_element_type=F32)
        tr = jnp.tanh(z[:, :RG_BW] + rb_ref[:, lo:lo + RG_BW])
        ti = jnp.tanh(z[:, RG_BW:] + ib_ref[:, lo:lo + RG_BW])
        a = jnp.exp2(tr * half_log2a[:, lo:lo + RG_BW] + half_log2a[:, lo:lo + RG_BW])
        om = 1.0 - a * a
        mult = om * lax.rsqrt(jnp.maximum(om, TINY))
        hx = 0.5 * xc[:, lo:lo + RG_BW]
        a_ref[:, lo:lo + RG_BW] = a
        b_ref[:, lo:lo + RG_BW] = mult * (hx * ti + hx)

    row = lax.broadcasted_iota(I32, (SUBLANES, a_ref.shape[1]), 0)

    def scan_body(c, hc):
        r0 = pl.multiple_of(c * SUBLANES, SUBLANES)
        a = a_ref[pl.ds(r0, SUBLANES), :]
        b = b_ref[pl.ds(r0, SUBLANES), :]
        for k in (1, 2, 4):
            keep = row >= k
            b = jnp.where(keep, a * pltpu.roll(b, k, 0) + b, b)
            a = jnp.where(keep, a * pltpu.roll(a, k, 0), a)
        h = a * hc + b
        b_ref[pl.ds(r0, SUBLANES), :] = h
        return h[SUBLANES - 1:SUBLANES, :]

    hc = lax.fori_loop(0, tm // SUBLANES, scan_body, hc_ref[0:1, :])
    hc_ref[0:1, :] = hc

    gh = gate_ref[...]
    y = (b_ref[...] * (gh * jnp.tanh(gh) + gh)).astype(BF16)
    o_ref[...] = x_ref[...] + jnp.dot(y, outw_ref[...], preferred_element_type=F32)


def _hawk_layer(x2d, b_in, S, g, in_w, cw, cb, wri, rb, ib, lam, out_w):
    D = x2d.shape[1]
    C = in_w.shape[1] // 2
    tm = ROW_TILE
    ns = S // tm
    vec = lambda n: pl.BlockSpec((1, n), lambda b, s: (0, 0))
    return pl.pallas_call(
        _hawk_kernel,
        grid=(1, ns),
        in_specs=[pl.BlockSpec((tm, D), lambda b, s: (b_in * ns + s, 0)),
                  vec(D),
                  pl.BlockSpec((D, 2 * C), lambda b, s: (0, 0), pipeline_mode=pl.Buffered(1)),
                  pl.BlockSpec((CONV_WIDTH, C), lambda b, s: (0, 0)),
                  vec(C),
                  pl.BlockSpec((N_RG_BLOCKS, RG_BW, 2 * RG_BW), lambda b, s: (0, 0, 0)),
                  vec(C), vec(C), vec(C),
                  pl.BlockSpec((C, D), lambda b, s: (0, 0), pipeline_mode=pl.Buffered(1))],
        out_specs=pl.BlockSpec((tm, D), lambda b, s: (s, 0)),
        out_shape=jax.ShapeDtypeStruct((S, D), F32),
        scratch_shapes=[pltpu.VMEM((tm, C), F32),
                        pltpu.VMEM((tm + SUBLANES, C), F32),
                        pltpu.VMEM((tm, C), F32),
                        pltpu.VMEM((tm, C), F32),
                        pltpu.VMEM((SUBLANES, C), F32)],
        compiler_params=_cparams(("arbitrary", "arbitrary")),
        name="hawk_layer",
    )(x2d, g, in_w, cw, cb, wri, rb, ib, lam, out_w)


def _emit_q(u, qg, cosf, sinf, t, qp_ref, q16_ref, sg_ref):
    assert SUB_ROWS == IDX_CHUNK
    d_attn = N_HEADS * HEAD_DIM
    rows = pl.ds(t * SUB_ROWS, SUB_ROWS)
    prow = pl.ds((t // 2) * SUB_ROWS, SUB_ROWS)
    qs = _head_norm_rope(u[:, :d_attn], qg, cosf, sinf)
    for hd in range(N_HEADS):
        q16 = qs[hd].astype(BF16)
        q16_ref[0, hd, rows, :] = q16
        r = _bits(q16.astype(F32))
        swapped = pltpu.roll(r, HALF, 1)
        if t % 2 == 0:
            w = (swapped & HI16_MASK) | lax.shift_right_logical(r, 16)
            qp_ref[0, hd, prow, pl.ds(0, HALF)] = w[:, :HALF]
        else:
            w = (r & HI16_MASK) | lax.shift_right_logical(swapped, 16)
            qp_ref[0, hd, prow, pl.ds(HALF, HALF)] = w[:, HALF:]
    sg_ref[rows, :] = _silu(u[:, d_attn:])


def _kvq_proj_kernel(x_ref, gkv_ref, wkv_ref, kg_ref, gq_ref, wq_ref, qg_ref, cos_ref, sin_ref,
                     k_ref, v_ref, km_ref, q_ref, q16_ref, sg_ref):
    tm = x_ref.shape[0]
    d_attn = N_HEADS * HEAD_DIM
    ksum = {}
    for t in range(tm // SUB_ROWS):
        rows = pl.ds(t * SUB_ROWS, SUB_ROWS)
        x = x_ref[rows, :]
        xn = x * lax.rsqrt(jnp.mean(x * x, axis=-1, keepdims=True) + EPS)
        kv = jnp.dot((xn * gkv_ref[...]).astype(BF16), wkv_ref[...], preferred_element_type=F32)
        u = jnp.dot((xn * gq_ref[...]).astype(BF16), wq_ref[...], preferred_element_type=F32)
        cosf, sinf = cos_ref[rows, :], sin_ref[rows, :]
        ks = _head_norm_rope(kv[:, :d_attn], kg_ref[...], cosf, sinf)
        for hd in range(N_HEADS):
            k_ref[0, hd, rows, :] = ks[hd].astype(BF16)
            v_ref[0, hd, rows, :] = kv[:, d_attn + hd * HEAD_DIM:d_attn + (hd + 1) * HEAD_DIM].astype(BF16)
            key = (hd, (t * SUB_ROWS) // MOBA_BLOCK)
            part = jnp.sum(ks[hd], axis=0, keepdims=True)
            ksum[key] = part if key not in ksum else ksum[key] + part
        _emit_q(u, qg_ref[...], cosf, sinf, t, q_ref, q16_ref, sg_ref)
    for (hd, r), total in ksum.items():
        km_ref[0, r, pl.ds(hd, 1), :] = total * (1.0 / MOBA_BLOCK)


def _kvq_proj(x2d, B, S, gkv, wkv, kg, gq, wq, qg, cosf, sinf):
    T, D = x2d.shape
    tm = ROW_TILE
    ns = S // tm
    nb = S // MOBA_BLOCK
    rpt = tm // MOBA_BLOCK
    d_attn = N_HEADS * HEAD_DIM
    vec = lambda n: pl.BlockSpec((1, n), lambda b, s: (0, 0))
    wspec = pl.BlockSpec((D, 2 * d_attn), lambda b, s: (0, 0), pipeline_mode=pl.Buffered(1))
    rope = pl.BlockSpec((tm, HEAD_DIM), lambda b, s: (s, 0))
    heads = pl.BlockSpec((1, N_HEADS, tm, HEAD_DIM), lambda b, s: (b, 0, s, 0))
    hshape = lambda dt: jax.ShapeDtypeStruct((B, N_HEADS, S, HEAD_DIM), dt)
    return pl.pallas_call(
        _kvq_proj_kernel,
        grid=(B, ns),
        in_specs=[pl.BlockSpec((tm, D), lambda b, s: (b * ns + s, 0)),
                  vec(D), wspec, vec(HEAD_DIM), vec(D), wspec, vec(HEAD_DIM), rope, rope],
        out_specs=[heads, heads,
                   pl.BlockSpec((1, rpt, N_HEADS, HEAD_DIM), lambda b, s: (b, s, 0, 0)),
                   pl.BlockSpec((1, N_HEADS, tm // 2, LANES), lambda b, s: (b, 0, s, 0)), heads,
                   pl.BlockSpec((tm, d_attn), lambda b, s: (b * ns + s, 0))],
        out_shape=[hshape(BF16), hshape(BF16),
                   jax.ShapeDtypeStruct((B, nb, N_HEADS, HEAD_DIM), F32),
                   jax.ShapeDtypeStruct((B, N_HEADS, S // 2, LANES), I32), hshape(BF16),
                   jax.ShapeDtypeStruct((T, d_attn), F32)],
        compiler_params=_cparams(("arbitrary", "arbitrary")),
        name="kvq_proj",
    )(x2d, gkv, wkv, kg, gq, wq, qg, cosf, sinf)


def _q_proj_kernel(x_ref, g_ref, w_ref, qg_ref, cos_ref, sin_ref, q_ref, q16_ref, sg_ref):
    for t in range(x_ref.shape[0] // SUB_ROWS):
        rows = pl.ds(t * SUB_ROWS, SUB_ROWS)
        h = _rms_norm(x_ref[rows, :], g_ref[...]).astype(BF16)
        u = jnp.dot(h, w_ref[...], preferred_element_type=F32)
        _emit_q(u, qg_ref[...], cos_ref[rows, :], sin_ref[rows, :], t, q_ref, q16_ref, sg_ref)


def _q_proj(x2d, B, S, g, w, qg, cosf, sinf):
    T, D = x2d.shape
    tm = ROW_TILE
    ns = S // tm
    d_attn = N_HEADS * HEAD_DIM
    heads = pl.BlockSpec((1, N_HEADS, tm, HEAD_DIM), lambda b, s: (b, 0, s, 0))
    return pl.pallas_call(
        _q_proj_kernel,
        grid=(B, ns),
        in_specs=[pl.BlockSpec((tm, D), lambda b, s: (b * ns + s, 0)),
                  pl.BlockSpec((1, D), lambda b, s: (0, 0)),
                  pl.BlockSpec((D, 2 * d_attn), lambda b, s: (0, 0)),
                  pl.BlockSpec((1, HEAD_DIM), lambda b, s: (0, 0)),
                  pl.BlockSpec((tm, HEAD_DIM), lambda b, s: (s, 0)),
                  pl.BlockSpec((tm, HEAD_DIM), lambda b, s: (s, 0))],
        out_specs=[pl.BlockSpec((1, N_HEADS, tm // 2, LANES), lambda b, s: (b, 0, s, 0)), heads,
                   pl.BlockSpec((tm, d_attn), lambda b, s: (b * ns + s, 0))],
        out_shape=[jax.ShapeDtypeStruct((B, N_HEADS, S // 2, LANES), I32),
                   jax.ShapeDtypeStruct((B, N_HEADS, S, HEAD_DIM), BF16),
                   jax.ShapeDtypeStruct((T, d_attn), F32)],
        compiler_params=_cparams(("arbitrary", "arbitrary")),
        name="q_proj",
    )(x2d, g, w, qg, cosf, sinf)


def _moba_route_kernel(q_ref, km_ref, rt_ref, cnt_ref, run_ref):
    step = pl.program_id(2)
    nb = km_ref.shape[2]

    @pl.when(step == 0)
    def _():
        run_ref[...] = jnp.zeros_like(run_ref)

    km = km_ref[0, 0].astype(BF16)
    blk = lax.broadcasted_iota(I32, (nb, MOBA_BLOCK), 0)
    qa = lax.broadcasted_iota(I32, (MOBA_BLOCK, MOBA_BLOCK), 0)
    qb = lax.broadcasted_iota(I32, (MOBA_BLOCK, MOBA_BLOCK), 1)
    earlier = jnp.where(qa < qb, 1.0, 0.0).astype(BF16)
    run = run_ref[...]
    for u in range(ROUTE_TILES):
        j = step * ROUTE_TILES + u
        g = _nt_dot(km, q_ref[0, 0, pl.ds(u * MOBA_BLOCK, MOBA_BLOCK), :].astype(BF16))
        g = jnp.where(blk < j, g, -jnp.inf)
        hits, ids = [], []
        for _ in range(MOBA_TOPK):
            mx = jnp.max(g, axis=0, keepdims=True)
            idx = jnp.min(jnp.where(g == mx, blk, nb), axis=0, keepdims=True)
            hit = blk == idx
            ok = mx > -jnp.inf
            hits.append(jnp.where(hit & ok, 1.0, 0.0))
            ids.append(jnp.where(ok, idx, -1))
            g = jnp.where(hit, -jnp.inf, g)
        sel = hits[0] + hits[1] + hits[2]

        rank = jnp.dot(sel.astype(BF16), earlier, preferred_element_type=F32)
        dest = run + rank
        run = run + jnp.sum(sel, axis=1, keepdims=True)
        for r in range(MOBA_TOPK):
            loc = jnp.sum(hits[r] * dest, axis=0, keepdims=True).astype(I32)
            rt_ref[u, pl.ds(r, 1), :] = jnp.where(ids[r] >= 0, ids[r] * ROUTE_BLK_UNIT + loc, -1)
        rt_ref[u, pl.ds(MOBA_TOPK, SUBLANES - MOBA_TOPK), :] = jnp.full(
            (SUBLANES - MOBA_TOPK, MOBA_BLOCK), -1, I32)
    run_ref[...] = run
    cnt_ref[0] = run.astype(I32)


def _moba_route(q, kmean, b):
    _, H, S, Dh = q.shape
    nb = S // MOBA_BLOCK
    rows = ROUTE_TILES * MOBA_BLOCK
    nsteps = nb // ROUTE_TILES
    return pl.pallas_call(
        _moba_route_kernel,
        grid=(1, H, nsteps),
        in_specs=[pl.BlockSpec((1, 1, rows, Dh), lambda _, h, j: (b, h, j, 0)),
                  pl.BlockSpec((1, 1, nb, Dh), lambda _, h, j: (b, h, 0, 0))],
        out_specs=[pl.BlockSpec((ROUTE_TILES, SUBLANES, MOBA_BLOCK), lambda _, h, j: (h * nsteps + j, 0, 0)),
                   pl.BlockSpec((1, nb, 1), lambda _, h, j: (h, 0, 0))],
        out_shape=[jax.ShapeDtypeStruct((H * nb, SUBLANES, MOBA_BLOCK), I32),
                   jax.ShapeDtypeStruct((H, nb, 1), I32)],
        scratch_shapes=[pltpu.VMEM((nb, 1), F32)],
        compiler_params=_cparams(("arbitrary", "arbitrary", "arbitrary")),
        name="moba_route",
    )(q, kmean)


def _moba_layout_kernel(cnt_ref, rt_ref, pos_ref, qpos_ref, tblk_ref, nt_ref, base_ref, *, max_tiles):
    g = pl.program_id(0)
    nb = rt_ref.shape[0]
    slab_rows = max_tiles * GROUP_TILE

    def per_block(n, first_tile):
        count = cnt_ref[g * nb + n]
        tiles = (count + (GROUP_TILE - 1)) // GROUP_TILE
        base_ref[n] = first_tile * GROUP_TILE

        def per_tile(i, carry):
            tblk_ref[g * max_tiles + first_tile + i] = n
            return carry

        lax.fori_loop(0, tiles, per_tile, 0)
        return first_tile + tiles

    used = lax.fori_loop(0, nb, per_block, 0)
    nt_ref[g] = used
    null_row0 = used * GROUP_TILE

    def unused_tile(t, carry):
        tblk_ref[g * max_tiles + t] = nb - 1
        return carry

    lax.fori_loop(used, max_tiles, unused_tile, 0)

    code = rt_ref[...].reshape(nb * SUBLANES, MOBA_BLOCK)
    blk = lax.shift_right_arithmetic(code, ROUTE_BLK_SHIFT)
    base = jnp.zeros_like(code)
    for n in range(nb):
        base = jnp.where(blk == n, base_ref[n], base)
    lane = lax.broadcasted_iota(I32, code.shape, 1)
    row = jnp.where(code >= 0, base + (code & (ROUTE_BLK_UNIT - 1)), null_row0 + lane) + g * slab_rows
    pos_ref[...] = row.reshape(pos_ref.shape)
    p = row & (GROUP_TILE - 1)
    assert GROUP_TILE == 2 * IDX_CHUNK
    slot = ((p & (IDX_CHUNK - 1)) << 1) | lax.shift_right_logical(p, IDX_CHUNK.bit_length() - 1)
    qpos_ref[...] = ((row - p) | slot).reshape(qpos_ref.shape)


def _moba_layout(rt, counts, nb, max_tiles):
    G = rt.shape[0] // nb
    smem = pl.BlockSpec(memory_space=pltpu.SMEM)
    return pl.pallas_call(
        functools.partial(_moba_layout_kernel, max_tiles=max_tiles),
        grid_spec=pltpu.PrefetchScalarGridSpec(
            num_scalar_prefetch=1,
            grid=(G,),
            in_specs=[pl.BlockSpec((nb, SUBLANES, MOBA_BLOCK), lambda g, cnt: (g, 0, 0))],
            out_specs=[pl.BlockSpec((nb, SUBLANES, MOBA_BLOCK), lambda g, cnt: (g, 0, 0)),
                       pl.BlockSpec((nb, SUBLANES, MOBA_BLOCK), lambda g, cnt: (g, 0, 0)), smem, smem],
            scratch_shapes=[pltpu.SMEM((nb,), I32)],
        ),
        out_shape=[jax.ShapeDtypeStruct((G * nb, SUBLANES, MOBA_BLOCK), I32),
                   jax.ShapeDtypeStruct((G * nb, SUBLANES, MOBA_BLOCK), I32),
                   jax.ShapeDtypeStruct((G * max_tiles,), I32),
                   jax.ShapeDtypeStruct((G,), I32)],
        compiler_params=_cparams(("arbitrary",)),
        name="moba_layout",
    )(counts, rt)


def _sc_mesh():
    return plsc.VectorSubcoreMesh(core_axis_name="c", subcore_axis_name="s")


def _sc_load_indices(pos_hbm, idx_v, t0, n_tiles):
    for c in range(SC_HALVES):
        pltpu.sync_copy(pos_hbm.at[pl.ds(t0, n_tiles), pl.ds(0, MOBA_TOPK), pl.ds(c * IDX_CHUNK, IDX_CHUNK)],
                        idx_v.at[c])


def _sc_scatter_rows(src, src_tile0, pos, n_out_rows):
    D = src.shape[-1]
    n_tiles = pos.shape[0]
    per_w = n_tiles // SC_WORKERS
    assert per_w * SC_WORKERS == n_tiles and src.shape[1:3] == (IDX_CHUNK, SC_HALVES)

    @functools.partial(
        pl.kernel, mesh=_sc_mesh(),
        out_type=jax.ShapeDtypeStruct((n_out_rows, D), src.dtype),
        scratch_types=[pltpu.VMEM((SC_HALVES, per_w, MOBA_TOPK, IDX_CHUNK), I32),
                       pltpu.VMEM((SC_HALVES, IDX_CHUNK, D), src.dtype),
                       pltpu.SemaphoreType.DMA((SC_HALVES,)),
                       pltpu.SemaphoreType.DMA((SC_HALVES,))],
        compiler_params=pltpu.CompilerParams(use_tc_tiling_on_sc=False),
        name="sc_scatter_rows",
    )
    def k(src_hbm, pos_hbm, out_hbm, idx_v, rows_v, load_sem, scat_sem):
        t0 = (lax.axis_index("s") * SC_CORES + lax.axis_index("c")) * per_w
        _sc_load_indices(pos_hbm, idx_v, t0, per_w)

        def load(i, c):
            return pltpu.make_async_copy(src_hbm.at[src_tile0 + t0 + i, :, c, :], rows_v.at[c], load_sem.at[c])

        def scatters(i, c):
            return [pltpu.make_async_copy(rows_v.at[c], out_hbm.at[idx_v.at[c, i, r]], scat_sem.at[c])
                    for r in range(MOBA_TOPK)]

        load(0, 0).start()

        @pl.loop(0, per_w)
        def _(i):
            for c in range(SC_HALVES):
                pi, pc = (i, 0) if c else (i - 1, 1)
                ni, nc = (i + 1, 0) if c else (i, 1)
                load(i, c).wait()
                for cp in scatters(i, c):
                    cp.start()

                @pl.when(pi >= 0)
                def _():
                    for cp in scatters(pi, pc):
                        cp.wait()

                @pl.when(ni < per_w)
                def _():
                    load(ni, nc).start()

        for cp in scatters(per_w - 1, SC_HALVES - 1):
            cp.wait()

    return k(src, pos)


def _sc_gather_rows(table, pos):
    R, D = table.shape
    n_tiles = pos.shape[0]
    N = n_tiles * MOBA_BLOCK
    per_w = n_tiles // SC_WORKERS
    assert per_w * SC_WORKERS == n_tiles

    items = SC_HALVES * MOBA_TOPK
    n_buf = 2
    assert items % n_buf == 0

    @functools.partial(
        pl.kernel, mesh=_sc_mesh(),
        out_type=jax.ShapeDtypeStruct((MOBA_TOPK, N, D), table.dtype),
        scratch_types=[pltpu.VMEM((SC_HALVES, per_w, MOBA_TOPK, IDX_CHUNK), I32),
                       pltpu.VMEM((n_buf, IDX_CHUNK, D), table.dtype),
                       pltpu.SemaphoreType.DMA((n_buf,)),
                       pltpu.SemaphoreType.DMA((n_buf,))],
        name="sc_gather_rows",
    )
    def k(table_hbm, pos_hbm, out_hbm, idx_v, rows_v, gat_sem, put_sem):
        t0 = (lax.axis_index("s") * SC_CORES + lax.axis_index("c")) * per_w
        _sc_load_indices(pos_hbm, idx_v, t0, per_w)

        def gather(i, j):
            c, r = divmod(j, MOBA_TOPK)
            return pltpu.make_async_copy(table_hbm.at[idx_v.at[c, i, r]], rows_v.at[j % n_buf], gat_sem.at[j % n_buf])

        def put(i, j):
            c, r = divmod(j, MOBA_TOPK)
            row0 = pl.multiple_of((t0 + i) * MOBA_BLOCK + c * IDX_CHUNK, IDX_CHUNK)
            return pltpu.make_async_copy(rows_v.at[j % n_buf], out_hbm.at[r, pl.ds(row0, IDX_CHUNK)],
                                         put_sem.at[j % n_buf])

        gather(0, 0).start()

        @pl.loop(0, per_w)
        def _(i):
            for j in range(items):
                pi, pj = (i, j - 1) if j else (i - 1, items - 1)
                ni, nj = (i, j + 1) if j + 1 < items else (i + 1, 0)
                gather(i, j).wait()
                put(i, j).start()

                @pl.when(pi >= 0)
                def _():
                    put(pi, pj).wait()

                @pl.when(ni < per_w)
                def _():
                    gather(ni, nj).start()

        put(per_w - 1, items - 1).wait()

    return k(table, pos)


def _bits(x):
    return lax.bitcast_convert_type(x, I32)


def _pack_partial(part, m):
    lane = lax.broadcasted_iota(I32, (part.shape[0], HEAD_DIM), 1)
    lo = _bits(part[:, :HEAD_DIM]) + BF16_HALF_ULP
    hi = _bits(jnp.where(lane == L_LANE, m, part[:, HEAD_DIM:])) + BF16_HALF_ULP
    return (hi & HI16_MASK) | lax.shift_right_logical(lo, 16)


def _null_partial(rows):
    lane = lax.broadcasted_iota(I32, (rows, HEAD_DIM), 1)
    return jnp.where(lane == L_LANE, NEG_INF_BITS & HI16_MASK, 0)


def _unpack_partial(word):
    lo = lax.bitcast_convert_type(lax.shift_left(word, 16), F32)
    hi = lax.bitcast_convert_type(word & HI16_MASK, F32)
    return lo, hi


def _augment_v(v):
    vf = v.astype(F32)
    lane = lax.broadcasted_iota(I32, vf.shape, 1)
    lo = jnp.where(lane < HALF, vf, jnp.where(lane == L_LANE, 1.0, 0.0))
    hi = jnp.where(lane < HALF, pltpu.roll(vf, HALF, 1), 0.0)
    return jnp.concatenate([lo, hi], axis=1).astype(BF16)


def _moba_group_kernel(tblk_ref, nt_ref, qg_ref, k_ref, v_ref, o_ref, va_ref, k2_ref, *, max_tiles):
    g, step = pl.program_id(0), pl.program_id(1)
    t0 = g * max_tiles + step * TILES_PER_STEP
    scale = HEAD_DIM ** -0.5
    null_tile = nt_ref[g]
    last_step = null_tile // TILES_PER_STEP

    @pl.when(step == 0)
    def _():
        def build(c, carry):
            rows = pl.ds(pl.multiple_of(c * V_BUILD_ROWS, V_BUILD_ROWS), V_BUILD_ROWS)
            va_ref[rows, :] = _augment_v(v_ref[0, 0, rows, :])
            kf = k_ref[0, 0, rows, :].astype(F32)
            lane = lax.broadcasted_iota(I32, kf.shape, 1)
            swapped = pltpu.roll(kf, HALF, 1)
            k2_ref[rows, :] = jnp.concatenate([jnp.where(lane < HALF, kf, swapped),
                                               jnp.where(lane < HALF, swapped, kf)], axis=1).astype(BF16)
            return carry

        lax.fori_loop(0, va_ref.shape[0] // V_BUILD_ROWS, build, 0)

    @pl.when(step <= last_step)
    def _():
        half_rows = GROUP_TILE // 2
        lane = lax.broadcasted_iota(I32, (half_rows, HEAD_DIM), 1)
        for u in range(TILES_PER_STEP):
            word = qg_ref[pl.ds(u * half_rows, half_rows), :]
            first = lax.bitcast_convert_type(lax.shift_left(word, 16), F32)
            second = lax.bitcast_convert_type(word & HI16_MASK, F32)
            n0 = pl.multiple_of(tblk_ref[t0 + u] * MOBA_BLOCK, MOBA_BLOCK)
            left, right = lane < HALF, lane >= HALF
            q = jnp.concatenate(
                [jnp.concatenate([jnp.where(left, first, 0.0), jnp.where(left, second, 0.0)], axis=1),
                 jnp.concatenate([jnp.where(right, first, 0.0), jnp.where(right, second, 0.0)], axis=1)],
                axis=0).astype(BF16)
            s = _nt_dot(q, k2_ref[pl.ds(n0, MOBA_BLOCK), :])
            m = (jnp.max(s, axis=1, keepdims=True) * (scale * LOG2_E)).astype(BF16).astype(F32)
            p = jnp.exp2(s * (scale * LOG2_E) - m).astype(BF16)
            part = jnp.dot(p, va_ref[pl.ds(n0, MOBA_BLOCK), :], preferred_element_type=F32)
            o_ref[pl.ds(u * GROUP_TILE, GROUP_TILE), :] = _pack_partial(part, m)

    @pl.when(step == last_step)
    def _():
        row0 = pl.multiple_of((null_tile - last_step * TILES_PER_STEP) * GROUP_TILE, GROUP_TILE)
        o_ref[pl.ds(row0, GROUP_TILE), :] = _null_partial(GROUP_TILE)


def _moba_group(qg, k, v, b, tblk, ntiles, max_tiles):
    _, H, S, Dh = k.shape
    G = H
    n_steps = max_tiles // TILES_PER_STEP
    step_rows = TILES_PER_STEP * GROUP_TILE

    def tile_idx(g, s, tblk, nt):
        return (g * n_steps + jnp.minimum(s, nt[g] // TILES_PER_STEP), 0)

    def kv_idx(g, s, *_):
        return (b, jnp.minimum(g + jnp.minimum(s, 1), G - 1), 0, 0)

    return pl.pallas_call(
        functools.partial(_moba_group_kernel, max_tiles=max_tiles),
        grid_spec=pltpu.PrefetchScalarGridSpec(
            num_scalar_prefetch=2,
            grid=(G, n_steps),
            in_specs=[pl.BlockSpec((step_rows // 2, LANES), tile_idx),
                      pl.BlockSpec((1, 1, S, Dh), kv_idx),
                      pl.BlockSpec((1, 1, S, Dh), kv_idx)],
            out_specs=pl.BlockSpec((step_rows, LANES), tile_idx),
            scratch_shapes=[pltpu.VMEM((S, VA_W), BF16), pltpu.VMEM((S, 2 * Dh), BF16)],
        ),
        out_shape=jax.ShapeDtypeStruct((G * max_tiles * GROUP_TILE, LANES), I32),
        compiler_params=_cparams(("arbitrary", "arbitrary")),
        name="moba_group",
    )(tblk, ntiles, qg, k, v)


def _moba_merge_kernel(*refs, n_skip):
    q_ref, k_ref, v_ref, sg_ref, x_ref, outw_ref = refs[n_skip:n_skip + 6]
    og_refs, out_ref, y_ref = refs[n_skip + 6:-2], refs[-2], refs[-1]
    scale = HEAD_DIM ** -0.5
    qi = lax.broadcasted_iota(I32, (MOBA_BLOCK, MOBA_BLOCK), 0)
    ki = lax.broadcasted_iota(I32, (MOBA_BLOCK, MOBA_BLOCK), 1)
    lane = lax.broadcasted_iota(I32, (MOBA_BLOCK, HEAD_DIM), 1)
    spread = jnp.where(lax.broadcasted_iota(I32, (HEAD_DIM, HEAD_DIM), 0) == L_LANE, 1.0, 0.0).astype(BF16)
    for u, t in [(u, t) for t in range(MERGE_TILES) for u in range(len(og_refs))]:
        og_ref = og_refs[u]
        rows = pl.ds(t * MOBA_BLOCK, MOBA_BLOCK)
        q = q_ref[0, u, rows, :].astype(BF16)
        s = _nt_dot(q, k_ref[0, u, rows, :])
        s = jnp.where(ki <= qi, s, NEG_INF)
        m_raw = jnp.max(s, axis=1, keepdims=True)
        p = jnp.exp2((s - m_raw) * (scale * LOG2_E)).astype(BF16)
        own = jnp.dot(p, _augment_v(v_ref[0, u, rows, :]), preferred_element_type=F32)
        parts = [_unpack_partial(og_ref[r, rows, :]) for r in range(MOBA_TOPK)]
        ms = [jnp.dot(jnp.where(lane == L_LANE, hi, 0.0).astype(BF16), spread, preferred_element_type=F32)
              for _, hi in parts]
        m_own = jnp.broadcast_to(m_raw * (scale * LOG2_E), lane.shape)
        m_all = m_own
        for m in ms:
            m_all = jnp.maximum(m_all, m)
        w_own = jnp.exp2(m_own - m_all)
        tot_lo = w_own * own[:, :HEAD_DIM]
        tot_hi = w_own * own[:, HEAD_DIM:]
        for (lo, hi), m in zip(parts, ms):
            w = jnp.exp2(m - m_all)
            tot_lo = tot_lo + w * lo
            tot_hi = tot_hi + w * hi
        acc = jnp.where(lane < HALF, tot_lo, pltpu.roll(tot_hi, HALF, 1))
        cols = pl.ds(u * HEAD_DIM, HEAD_DIM)
        y_ref[rows, cols] = ((acc / tot_lo[:, L_LANE:L_LANE + 1]) * sg_ref[rows, cols]).astype(BF16)

    out_ref[...] = x_ref[...] + jnp.dot(y_ref[...], outw_ref[...], preferred_element_type=F32)


def _moba_merge(q, k, v, sg, og, x2d, out_w, out_batch, n_out_batch, o_prev):
    _, H, S, Dh = q.shape
    D = x2d.shape[1]
    b = 0
    hps = HEADS_PER_STEP
    assert hps == H
    rows = MERGE_TILES * MOBA_BLOCK
    ns = S // rows
    og_spec = lambda u: pl.BlockSpec((MOBA_TOPK, rows, LANES),
                                     lambda _, hp, j: (0, (hp * hps + u) * ns + j, 0))
    prev_specs = [] if o_prev is None else [pl.BlockSpec(memory_space=pl.ANY)]
    prev_args = [] if o_prev is None else [o_prev]
    return pl.pallas_call(
        functools.partial(_moba_merge_kernel, n_skip=len(prev_args)),
        grid=(1, H // hps, ns),
        in_specs=prev_specs
                 + [pl.BlockSpec((1, hps, rows, Dh), lambda _, hp, j: (b, hp, j, 0)),
                    pl.BlockSpec((1, hps, rows, Dh), lambda _, hp, j: (b, hp, j, 0)),
                    pl.BlockSpec((1, hps, rows, Dh), lambda _, hp, j: (b, hp, j, 0)),
                    pl.BlockSpec((rows, hps * Dh), lambda _, hp, j: (b * ns + j, hp)),
                    pl.BlockSpec((rows, D), lambda _, hp, j: (b * ns + j, 0)),
                    pl.BlockSpec((H * Dh, D), lambda _, hp, j: (0, 0))]
                 + [og_spec(u) for u in range(hps)],
        out_specs=pl.BlockSpec((rows, D), lambda _, hp, j: (out_batch * ns + j, 0)),
        out_shape=jax.ShapeDtypeStruct((n_out_batch * S, D), F32),
        scratch_shapes=[pltpu.VMEM((rows, H * Dh), BF16)],
        input_output_aliases={} if o_prev is None else {0: 0},
        compiler_params=_cparams(("arbitrary", "arbitrary", "arbitrary")),
        name="moba_merge",
    )(*prev_args, q, k, v, sg, x2d, out_w, *([og] * hps))


def _moba_layer(qp, q16, k, v, kmean, sg, x2d, out_w, out_batch, n_out_batch, o_prev):
    _, H, S, Dh = q16.shape
    nb = S // MOBA_BLOCK
    max_tiles = (MOBA_TOPK * S) // GROUP_TILE + nb + 1
    max_tiles = -(-max_tiles // TILES_PER_STEP) * TILES_PER_STEP
    slab_rows = max_tiles * GROUP_TILE

    q_rows = qp.reshape(H * nb, IDX_CHUNK, SC_HALVES, HALF)
    rt, counts = _moba_route(q16, kmean, 0)
    pos, qpos, tblk, ntiles = _moba_layout(rt, counts.reshape(-1), nb, max_tiles)
    qg = _sc_scatter_rows(q_rows, 0, qpos, H * slab_rows)
    qg = qg.reshape(H * slab_rows // 2, LANES)
    parts = _moba_group(qg, k, v, 0, tblk, ntiles, max_tiles)
    og = _sc_gather_rows(parts, pos)
    return _moba_merge(q16, k, v, sg, og, x2d, out_w, out_batch, n_out_batch, o_prev)


def _rope_tables(S):
    inv = np.float32(ROPE_THETA) ** (-np.arange(HALF, dtype=np.float32) / np.float32(HALF))
    ang = np.arange(S, dtype=np.float32)[:, None] * inv[None, :].astype(np.float32)
    cos, sin = np.cos(ang).astype(np.float32), np.sin(ang).astype(np.float32)
    return (jnp.asarray(np.concatenate([cos, cos], axis=-1)),
            jnp.asarray(np.concatenate([-sin, sin], axis=-1)))


def kernel(x, a_norm, a_in_w, a_conv_w, a_conv_b, a_r_w, a_r_b, a_i_w, a_i_b, a_lambda, a_out_w,
           kv_norm, kv_w, k_norm, b_norm, b_in_w, q_norm, b_out_w):
    B, S, D = x.shape
    assert S % ROW_TILE == 0 and ROW_TILE % MOBA_BLOCK == 0
    assert D == N_RG_BLOCKS * RG_BW == N_HEADS * HEAD_DIM
    assert S % (ROUTE_TILES * MOBA_BLOCK) == 0 and S % (MERGE_TILES * MOBA_BLOCK) == 0
    assert S % V_BUILD_ROWS == 0 and S <= ROUTE_BLK_UNIT
    assert (N_HEADS * (S // MOBA_BLOCK)) % SC_WORKERS == 0
    n_a, n_b = a_in_w.shape[0], b_in_w.shape[0]
    assert n_a >= 1 and n_b >= 1
    x2d = x.reshape(B * S, D)
    row = lambda v: v.reshape(1, -1)
    cosf, sinf = _rope_tables(S)

    out = None
    for b in range(B):
        xs, b_in = x2d, b
        for l in range(n_a):
            c_rnn = a_in_w.shape[2] // 2
            wri = (0.5 * jnp.concatenate([a_r_w[l], a_i_w[l]], axis=-1)).astype(BF16)
            in_w = jnp.concatenate([a_in_w[l][:, :c_rnn], 0.5 * a_in_w[l][:, c_rnn:]], axis=-1).astype(BF16)
            xs = _hawk_layer(xs, b_in, S, row(a_norm[l]), in_w, a_conv_w[l], row(a_conv_b[l]),
                             wri, row(0.5 * a_r_b[l]), row(0.5 * a_i_b[l]), row(a_lambda[l]),
                             a_out_w[l].astype(BF16))
            b_in = 0
        for jl in range(n_b):
            q_args = (row(b_norm[jl]), b_in_w[jl].astype(BF16), row(q_norm[jl]))
            if jl == 0:
                k, v, kmean, qp, q16, sg = _kvq_proj(xs, 1, S, row(kv_norm), kv_w.astype(BF16), row(k_norm),
                                                    *q_args, cosf, sinf)
                kmean = jnp.transpose(kmean, (0, 2, 1, 3))
            else:
                qp, q16, sg = _q_proj(xs, 1, S, *q_args, cosf, sinf)
            w_out = b_out_w[jl].astype(BF16)
            if jl == n_b - 1:
                out = xs = _moba_layer(qp, q16, k, v, kmean, sg, xs, w_out, b, B, out)
            else:
                xs = _moba_layer(qp, q16, k, v, kmean, sg, xs, w_out, 0, 1, None)
    return out.reshape(B, S, D)
```
